```python
import math
import jax
import jax.numpy as jnp
from jax import lax
import numpy as np

D_MODEL = 1024
BATCH = 8
SEQ = 2048
DEPTH = 2

GRID_W = 64
CTX_LEN = 256
HEAD_DIM = 64
ROPE_BASE = 10000.0
ROPE_FREQS = HEAD_DIM // 4
LN_EPS = 1e-5
RMS_EPS = 1e-5
NEG_INF = -1e30

SWA_HEADS = 8
SWA_KV_HEADS = 2
SWA_GROUP = SWA_HEADS // SWA_KV_HEADS
SWA_WINDOW = 128
SWA_BLOCK = 128
SWA_Q_W = SWA_HEADS * HEAD_DIM
SWA_KV_W = SWA_KV_HEADS * HEAD_DIM

SSM_WIDTH = D_MODEL // 2
SSM_GROUP = 16
SSM_GROUPS = SSM_WIDTH // SSM_GROUP
SSM_STATE = 64
SSM_DT_MIN = 1e-3
SSM_DT_MAX = 1e-1

AB_IN_W = SWA_Q_W + 2 * SWA_KV_W + SSM_WIDTH
AB_OUT_W = SWA_Q_W + SSM_WIDTH

DIF_HEADS = D_MODEL // (2 * HEAD_DIM)
DIF_QK_W = DIF_HEADS * 2 * HEAD_DIM
DIF_V_HEAD = 2 * HEAD_DIM
DIF_V_W = DIF_HEADS * DIF_V_HEAD
DIF_BLOCK = 128

MOE_GROUPS = 4
MOE_EXPERTS_PER_GROUP = 8
MOE_EXPERTS = MOE_GROUPS * MOE_EXPERTS_PER_GROUP
MOE_TOPK = 2
MOE_HIDDEN = D_MODEL // 4

DEEPNORM_ALPHA = (2 * DEPTH) ** 0.25
DEEPNORM_BETA = (8 * DEPTH) ** -0.25
N_EVEN = (DEPTH + 1) // 2
N_ODD = DEPTH // 2

kernel_name = 'hybrid_swa_s5_diffattn_hmoe_dit'


def layer_norm(x, g, b):
    xf = x.astype(jnp.float32)
    mu = jnp.mean(xf, -1, keepdims=True)
    var = jnp.mean(jnp.square(xf - mu), -1, keepdims=True)
    return ((xf - mu) * lax.rsqrt(var + LN_EPS) * g + b).astype(x.dtype)


def axial_rope_tables(n_tokens):
    rows = n_tokens // GRID_W
    row = jnp.repeat(jnp.arange(rows, dtype=jnp.float32), GRID_W)
    col = jnp.tile(jnp.arange(GRID_W, dtype=jnp.float32), rows)
    inv = ROPE_BASE ** (-jnp.arange(ROPE_FREQS, dtype=jnp.float32) / ROPE_FREQS)
    ang_r = row[:, None] * inv[None, :]
    ang_c = col[:, None] * inv[None, :]
    return (jnp.cos(ang_r), jnp.sin(ang_r), jnp.cos(ang_c), jnp.sin(ang_c))


def apply_axial_rope(t, rope):
    cos_r, sin_r, cos_c, sin_c = rope
    n = t.shape[1]
    bshape = (n,) + (1,) * (t.ndim - 3) + (ROPE_FREQS,)
    tf = t.astype(jnp.float32)
    half = HEAD_DIM // 2

    def rot(u, cs, sn):
        cs = cs.reshape(bshape)
        sn = sn.reshape(bshape)
        u1, u2 = u[..., :ROPE_FREQS], u[..., ROPE_FREQS:]
        return jnp.concatenate([u1 * cs - u2 * sn, u2 * cs + u1 * sn], -1)

    out = jnp.concatenate([rot(tf[..., :half], cos_r, sin_r), rot(tf[..., half:], cos_c, sin_c)], -1)
    return out.astype(t.dtype)


def windowed_gqa_sink(q, k, v, qc, kc, vc, sink, need_ctx):
    bsz, n = q.shape[:2]
    ctx_len = kc.shape[1]
    nb = n // SWA_BLOCK
    span = 3 * SWA_BLOCK
    scale = HEAD_DIM ** -0.5
    qb = q.reshape(bsz, nb, SWA_BLOCK, SWA_KV_HEADS, SWA_GROUP, HEAD_DIM)
    pad = ((0, 0), (SWA_BLOCK, SWA_BLOCK), (0, 0), (0, 0))
    kp = jnp.pad(k, pad).reshape(bsz, nb + 2, SWA_BLOCK, SWA_KV_HEADS, HEAD_DIM)
    vp = jnp.pad(v, pad).reshape(bsz, nb + 2, SWA_BLOCK, SWA_KV_HEADS, HEAD_DIM)
    kb = jnp.concatenate([kp[:, :-2], kp[:, 1:-1], kp[:, 2:]], axis=2)
    vb = jnp.concatenate([vp[:, :-2], vp[:, 1:-1], vp[:, 2:]], axis=2)
    qi = jnp.arange(SWA_BLOCK)[:, None] + SWA_BLOCK
    ki = jnp.arange(span)[None, :]
    kabs = (jnp.arange(nb) * SWA_BLOCK - SWA_BLOCK)[:, None] + jnp.arange(span)[None, :]
    mask = (jnp.abs(qi - ki) <= SWA_WINDOW)[None] & ((kabs >= 0) & (kabs < n))[:, None, :]
    s_loc = jnp.einsum('bnqhgd,bnkhd->bnhgqk', qb, kb, preferred_element_type=jnp.float32) * scale
    s_loc = jnp.where(mask[None, :, None, None], s_loc, NEG_INF)
    s_ctx = jnp.einsum('bnqhgd,blhd->bnhgql', qb, kc, preferred_element_type=jnp.float32) * scale
    sk = sink.astype(jnp.float32).reshape(SWA_KV_HEADS, SWA_GROUP)
    s_sink = jnp.broadcast_to(sk[None, None, :, :, None, None], s_loc.shape[:-1] + (1,))
    p = jax.nn.softmax(jnp.concatenate([s_loc, s_ctx, s_sink], -1), -1).astype(v.dtype)
    o = (jnp.einsum('bnhgqk,bnkhd->bnqhgd', p[..., :span], vb)
         + jnp.einsum('bnhgql,blhd->bnqhgd', p[..., span:span + ctx_len], vc))
    o = o.reshape(bsz, n, SWA_Q_W)
    oc = None
    if need_ctx:
        sc = jnp.einsum('blhgd,bmhd->bhglm', qc, kc, preferred_element_type=jnp.float32) * scale
        sc_sink = jnp.broadcast_to(sk[None, :, :, None, None], sc.shape[:-1] + (1,))
        pc = jax.nn.softmax(jnp.concatenate([sc, sc_sink], -1), -1).astype(vc.dtype)
        oc = jnp.einsum('bhglm,bmhd->blhgd', pc[..., :ctx_len], vc).reshape(bsz, ctx_len, SWA_Q_W)
    return o, oc


def s5_discretize(a_re, a_im, log_step, b_re, b_im):
    dt = jnp.exp(log_step.astype(jnp.float32))[:, None]
    ar = a_re.astype(jnp.float32)
    ai = a_im.astype(jnp.float32)
    mag = jnp.exp(dt * ar)
    abar_re = mag * jnp.cos(dt * ai)
    abar_im = mag * jnp.sin(dt * ai)
    den = ar * ar + ai * ai
    nr = abar_re - 1.0
    coef_re = (nr * ar + abar_im * ai) / den
    coef_im = (abar_im * ar - nr * ai) / den
    br = b_re.astype(jnp.float32)
    bi = b_im.astype(jnp.float32)
    bb_re = coef_re[..., None] * br - coef_im[..., None] * bi
    bb_im = coef_re[..., None] * bi + coef_im[..., None] * br
    return abar_re, abar_im, bb_re, bb_im


def complex_linear_combine(e1, e2):
    a1r, a1i, b1r, b1i = e1
    a2r, a2i, b2r, b2i = e2
    return (a2r * a1r - a2i * a1i,
            a2r * a1i + a2i * a1r,
            a2r * b1r - a2i * b1i + b2r,
            a2r * b1i + a2i * b1r + b2i)


def s5_scan(u, abar_re, abar_im, bb_re, bb_im, reverse):
    t = u.shape[1]
    bu_re = jnp.einsum('btgc,gpc->btgp', u, bb_re)
    bu_im = jnp.einsum('btgc,gpc->btgp', u, bb_im)
    a_re = jnp.broadcast_to(abar_re, (1, t) + abar_re.shape)
    a_im = jnp.broadcast_to(abar_im, (1, t) + abar_im.shape)
    return lax.associative_scan(complex_linear_combine, (a_re, a_im, bu_re, bu_im), reverse=reverse, axis=1)


def s5_readout(s_re, s_im, c_re, c_im):
    return (jnp.einsum('btgp,gcp->btgc', s_re, c_re.astype(jnp.float32))
            - jnp.einsum('btgp,gcp->btgc', s_im, c_im.astype(jnp.float32)))


def s5_bidirectional(u, uc, a_re, a_im, log_step, b_re, b_im, c_re, c_im, d_skip, glu_w, glu_b, need_ctx):
    bsz, n = u.shape[:2]
    ctx_len = uc.shape[1]
    ug = u.astype(jnp.float32).reshape(bsz, n, SSM_GROUPS, SSM_GROUP)
    ucg = uc.astype(jnp.float32).reshape(bsz, ctx_len, SSM_GROUPS, SSM_GROUP)
    dg = d_skip.astype(jnp.float32).reshape(SSM_GROUPS, SSM_GROUP)
    y = ug * dg
    yc = ucg * dg if need_ctx else None
    for direction, reverse in ((0, False), (1, True)):
        ab_re, ab_im, bb_re, bb_im = s5_discretize(a_re[direction], a_im[direction], log_step[direction],
                                                   b_re[direction], b_im[direction])
        _, _, sc_re, sc_im = s5_scan(ucg, ab_re, ab_im, bb_re, bb_im, reverse)
        last = 0 if reverse else ctx_len - 1
        s0_re = sc_re[:, last:last + 1]
        s0_im = sc_im[:, last:last + 1]
        acc_re, acc_im, sl_re, sl_im = s5_scan(ug, ab_re, ab_im, bb_re, bb_im, reverse)
        s_re = acc_re * s0_re - acc_im * s0_im + sl_re
        s_im = acc_re * s0_im + acc_im * s0_re + sl_im
        y = y + s5_readout(s_re, s_im, c_re[direction], c_im[direction])
        if need_ctx:
            yc = yc + s5_readout(sc_re, sc_im, c_re[direction], c_im[direction])

    def glu(t):
        g = jax.nn.gelu(t.reshape(t.shape[:2] + (SSM_WIDTH,)))
        return g * jax.nn.sigmoid(g @ glu_w.astype(jnp.float32) + glu_b.astype(jnp.float32))

    out = glu(y).astype(u.dtype)
    outc = glu(yc).astype(uc.dtype) if need_ctx else None
    return out, outc


def mixer_swa_ssm(h, hc, rope, w_in, w_out, sink, a_re, a_im, log_step, b_re, b_im, c_re, c_im,
                  d_skip, glu_w, glu_b, need_ctx):
    splits = [SWA_Q_W, SWA_Q_W + SWA_KV_W, SWA_Q_W + 2 * SWA_KV_W]

    def proj(t):
        s = t.shape[:2]
        q, k, v, u = jnp.split(t @ w_in, splits, -1)
        return (q.reshape(s + (SWA_KV_HEADS, SWA_GROUP, HEAD_DIM)),
                k.reshape(s + (SWA_KV_HEADS, HEAD_DIM)),
                v.reshape(s + (SWA_KV_HEADS, HEAD_DIM)), u)

    q, k, v, u = proj(h)
    qc, kc, vc, uc = proj(hc)
    q = apply_axial_rope(q, rope)
    k = apply_axial_rope(k, rope)
    att, att_c = windowed_gqa_sink(q, k, v, qc, kc, vc, sink, need_ctx)
    ssm, ssm_c = s5_bidirectional(u, uc, a_re, a_im, log_step, b_re, b_im, c_re, c_im,
                                  d_skip, glu_w, glu_b, need_ctx)
    y = jnp.concatenate([att, ssm], -1) @ w_out
    yc = jnp.concatenate([att_c, ssm_c], -1) @ w_out if need_ctx else None
    return y, yc


def dif_head_out(o, g, lam_init):
    of = o.astype(jnp.float32)
    of = of * lax.rsqrt(jnp.mean(of * of, -1, keepdims=True) + RMS_EPS) * g.astype(jnp.float32)
    return (of * (1.0 - lam_init)).astype(o.dtype).reshape(o.shape[:2] + (DIF_V_W,))


def mixer_diff(h, hc, rope, w_in, w_out, lq1, lk1, lq2, lk2, subln_g, lam_init, need_ctx):
    bsz, n = h.shape[:2]
    scale = HEAD_DIM ** -0.5

    def proj(t):
        s = t.shape[:2]
        q, k, v = jnp.split(t @ w_in, [DIF_QK_W, 2 * DIF_QK_W], -1)
        return (q.reshape(s + (DIF_HEADS, 2, HEAD_DIM)), k.reshape(s + (DIF_HEADS, 2, HEAD_DIM)),
                v.reshape(s + (DIF_HEADS, DIF_V_HEAD)))

    q, k, v = proj(h)
    qc, kc, vc = proj(hc)
    q = apply_axial_rope(q, rope)
    k = apply_axial_rope(k, rope)
    lam = (jnp.exp(jnp.sum(lq1.astype(jnp.float32) * lk1.astype(jnp.float32)))
           - jnp.exp(jnp.sum(lq2.astype(jnp.float32) * lk2.astype(jnp.float32))) + lam_init)

    def attend(qq, kk, vv):
        s = jnp.einsum('bqhcd,bkhcd->bhcqk', qq, kk, preferred_element_type=jnp.float32) * scale
        p = jax.nn.softmax(s, -1)
        pd = (p[:, :, 0] - lam * p[:, :, 1]).astype(vv.dtype)
        return jnp.einsum('bhqk,bkhe->bqhe', pd, vv)

    kall = jnp.concatenate([k, kc], 1)
    vall = jnp.concatenate([v, vc], 1)
    nb = n // DIF_BLOCK
    qb = jnp.moveaxis(q.reshape(bsz, nb, DIF_BLOCK, DIF_HEADS, 2, HEAD_DIM), 1, 0)
    o = lax.map(lambda qq: attend(qq, kall, vall), qb)
    o = jnp.moveaxis(o, 0, 1).reshape(bsz, n, DIF_HEADS, DIF_V_HEAD)
    y = dif_head_out(o, subln_g, lam_init) @ w_out
    yc = dif_head_out(attend(qc, kc, vc), subln_g, lam_init) @ w_out if need_ctx else None
    return y, yc


def hier_moe(t, wg, bg, we, be, w1, w3, w2):
    pg = jax.nn.softmax((t @ wg).astype(jnp.float32) + bg.astype(jnp.float32), -1)
    gp, gi = lax.top_k(pg, 1)
    le = jnp.einsum('td,gde->tge', t, we).astype(jnp.float32) + be.astype(jnp.float32)
    le_sel = jnp.take_along_axis(le, gi[:, :, None], axis=1)[:, 0]
    tv, ti = lax.top_k(le_sel, MOE_TOPK)
    w_sel = jax.nn.softmax(tv, -1) * gp
    gate_e = jnp.sum(jax.nn.one_hot(ti, MOE_EXPERTS_PER_GROUP, dtype=jnp.float32) * w_sel[..., None], 1)
    gates = (jax.nn.one_hot(gi[:, 0], MOE_GROUPS, dtype=jnp.float32)[:, :, None]
             * gate_e[:, None, :]).astype(t.dtype)
    out = jnp.zeros_like(t)
    for g in range(MOE_GROUPS):
        sl = slice(g * MOE_EXPERTS_PER_GROUP, (g + 1) * MOE_EXPERTS_PER_GROUP)
        h1 = jnp.einsum('td,edf->tef', t, w1[sl])
        h3 = jnp.einsum('td,edf->tef', t, w3[sl])
        hh = jax.nn.silu(h1) * h3 * gates[:, g, :, None]
        out = out + jnp.einsum('tef,efd->td', hh, w2[sl])
    return out


def setup_inputs(seed: int = 0) -> dict:
    key = jax.random.key(seed)
    counter = [0]

    def nk():
        counter[0] += 1
        return jax.random.fold_in(key, counter[0])

    def nrm(shape, scale):
        return jax.random.normal(nk(), shape, jnp.float32) * scale

    D = D_MODEL
    NE, NO = N_EVEN, N_ODD
    G, P, C = SSM_GROUPS, SSM_STATE, SSM_GROUP
    E, F = MOE_EXPERTS, MOE_HIDDEN
    return {
        'x': nrm((BATCH, SEQ, D), 1.0),
        'c': nrm((BATCH, D), 1.0),
        'ctx': nrm((BATCH, CTX_LEN, D), 1.0),
        'c_ctx': nrm((D,), 1.0),
        'mod_w': nrm((DEPTH, D, 6 * D), D ** -0.5),
        'mod_b': nrm((DEPTH, 6 * D), 0.02),
        'ln1_g': 1.0 + nrm((DEPTH, D), 0.02),
        'ln1_b': nrm((DEPTH, D), 0.02),
        'ln2_g': 1.0 + nrm((DEPTH, D), 0.02),
        'ln2_b': nrm((DEPTH, D), 0.02),
        'swa_ssm_w_in': nrm((NE, D, AB_IN_W), D ** -0.5),
        'swa_ssm_w_out': nrm((NE, AB_OUT_W, D), AB_OUT_W ** -0.5 * DEEPNORM_BETA),
        'swa_sink': nrm((NE, SWA_HEADS), 0.5),
        'ssm_a_re': -0.5 + nrm((NE, 2, G, P), 0.01),
        'ssm_a_im': jnp.pi * jnp.arange(P, dtype=jnp.float32) + nrm((NE, 2, G, P), 0.01),
        'ssm_log_step': jax.random.uniform(nk(), (NE, 2, G), jnp.float32,
                                           math.log(SSM_DT_MIN), math.log(SSM_DT_MAX)),
        'ssm_b_re': nrm((NE, 2, G, P, C), (2 * C) ** -0.5),
        'ssm_b_im': nrm((NE, 2, G, P, C), (2 * C) ** -0.5),
        'ssm_c_re': nrm((NE, 2, G, C, P), P ** -0.5),
        'ssm_c_im': nrm((NE, 2, G, C, P), P ** -0.5),
        'ssm_d': nrm((NE, SSM_WIDTH), 1.0),
        'ssm_glu_w': nrm((NE, SSM_WIDTH, SSM_WIDTH), SSM_WIDTH ** -0.5),
        'ssm_glu_b': nrm((NE, SSM_WIDTH), 0.02),
        'dif_w_in': nrm((NO, D, 2 * DIF_QK_W + DIF_V_W), D ** -0.5),
        'dif_w_out': nrm((NO, DIF_V_W, D), DIF_V_W ** -0.5 * DEEPNORM_BETA),
        'dif_lam_q1': nrm((NO, HEAD_DIM), 0.1),
        'dif_lam_k1': nrm((NO, HEAD_DIM), 0.1),
        'dif_lam_q2': nrm((NO, HEAD_DIM), 0.1),
        'dif_lam_k2': nrm((NO, HEAD_DIM), 0.1),
        'dif_subln_g': 1.0 + nrm((NO, DIF_V_HEAD), 0.02),
        'moe_wg': nrm((DEPTH, D, MOE_GROUPS), D ** -0.5),
        'moe_bg': nrm((DEPTH, MOE_GROUPS), 0.01),
        'moe_we': nrm((DEPTH, MOE_GROUPS, D, MOE_EXPERTS_PER_GROUP), D ** -0.5),
        'moe_be': nrm((DEPTH, MOE_GROUPS, MOE_EXPERTS_PER_GROUP), 0.01),
        'moe_w1': nrm((DEPTH, E, D, F), D ** -0.5),
        'moe_w3': nrm((DEPTH, E, D, F), D ** -0.5),
        'moe_w2': nrm((DEPTH, E, F, D), F ** -0.5 * DEEPNORM_BETA),
    }


def reference(x, c, ctx, c_ctx, mod_w, mod_b, ln1_g, ln1_b, ln2_g, ln2_b,
              swa_ssm_w_in, swa_ssm_w_out, swa_sink,
              ssm_a_re, ssm_a_im, ssm_log_step, ssm_b_re, ssm_b_im, ssm_c_re, ssm_c_im,
              ssm_d, ssm_glu_w, ssm_glu_b,
              dif_w_in, dif_w_out, dif_lam_q1, dif_lam_k1, dif_lam_q2, dif_lam_k2, dif_subln_g,
              moe_wg, moe_bg, moe_we, moe_be, moe_w1, moe_w3, moe_w2):
    bsz, n, d = x.shape
    ctx_len = ctx.shape[1]
    rope = axial_rope_tables(n)
    xl, xc = x, ctx
    for layer in range(DEPTH):
        need_ctx = layer < DEPTH - 1
        i = layer // 2
        m_lat = jnp.split(jax.nn.silu(c) @ mod_w[layer] + mod_b[layer], 6, -1)
        sh1, sc1, g1, sh2, sc2, g2 = [m[:, None, :] for m in m_lat]
        csh1, csc1, cg1, csh2, csc2, cg2 = jnp.split(jax.nn.silu(c_ctx) @ mod_w[layer] + mod_b[layer], 6, -1)
        hl = xl * (1.0 + sc1) + sh1
        hc = xc * (1.0 + csc1) + csh1
        if layer % 2 == 0:
            yl, yc = mixer_swa_ssm(hl, hc, rope, swa_ssm_w_in[i], swa_ssm_w_out[i], swa_sink[i],
                                   ssm_a_re[i], ssm_a_im[i], ssm_log_step[i], ssm_b_re[i], ssm_b_im[i],
                                   ssm_c_re[i], ssm_c_im[i], ssm_d[i], ssm_glu_w[i], ssm_glu_b[i], need_ctx)
        else:
            lam_init = 0.8 - 0.6 * math.exp(-0.3 * layer)
            yl, yc = mixer_diff(hl, hc, rope, dif_w_in[i], dif_w_out[i], dif_lam_q1[i], dif_lam_k1[i],
                                dif_lam_q2[i], dif_lam_k2[i], dif_subln_g[i], lam_init, need_ctx)
        xl = layer_norm(DEEPNORM_ALPHA * xl + g1 * yl, ln1_g[layer], ln1_b[layer])
        hl = xl * (1.0 + sc2) + sh2
        if need_ctx:
            xc = layer_norm(DEEPNORM_ALPHA * xc + cg1 * yc, ln1_g[layer], ln1_b[layer])
            hc = xc * (1.0 + csc2) + csh2
            tok = jnp.concatenate([hl.reshape(-1, d), hc.reshape(-1, d)], 0)
        else:
            tok = hl.reshape(-1, d)
        f = hier_moe(tok, moe_wg[layer], moe_bg[layer], moe_we[layer], moe_be[layer],
                     moe_w1[layer], moe_w3[layer], moe_w2[layer])
        xl = layer_norm(DEEPNORM_ALPHA * xl + g2 * f[:bsz * n].reshape(bsz, n, d), ln2_g[layer], ln2_b[layer])
        if need_ctx:
            xc = layer_norm(DEEPNORM_ALPHA * xc + cg2 * f[bsz * n:].reshape(bsz, ctx_len, d),
                            ln2_g[layer], ln2_b[layer])
    return xl
```

```python
import functools
import math

import jax
import jax.numpy as jnp
from jax import lax
from jax.experimental import pallas as pl
from jax.experimental.pallas import tpu as pltpu

F32 = jnp.float32
BF16 = jnp.bfloat16
HIGHEST = lax.Precision.HIGHEST

D_MODEL = 1024
GRID_W = 64
HEAD_DIM = 64
ROPE_BASE = 10000.0
ROPE_FREQS = HEAD_DIM // 4
LN_EPS = 1e-5
RMS_EPS = 1e-5
NEG_INF = -1e30
LANES = 128
HALF = LANES // 2

SWA_HEADS = 8
SWA_KV_HEADS = 2
SWA_WINDOW = 128
SWA_BLOCK = 128
SWA_Q_W = SWA_HEADS * HEAD_DIM
SWA_KV_W = SWA_KV_HEADS * HEAD_DIM

SSM_WIDTH = D_MODEL // 2
SSM_GROUP = 16
SSM_GROUPS = SSM_WIDTH // SSM_GROUP
SSM_STATE = 64
SSM_CHUNK = 16
SSM_CW = SSM_CHUNK * SSM_GROUP

AB_IN_W = SWA_Q_W + 2 * SWA_KV_W + SSM_WIDTH

DIF_HEADS = D_MODEL // (2 * HEAD_DIM)
DIF_QK_W = DIF_HEADS * 2 * HEAD_DIM
DIF_V_HEAD = 2 * HEAD_DIM
DIF_V_W = DIF_HEADS * DIF_V_HEAD

MOE_GROUPS = 4
MOE_EPG = 8
MOE_EXPERTS = MOE_GROUPS * MOE_EPG
MOE_HIDDEN = D_MODEL // 4
ROUTER_EXPERT_LANE0 = 8

VMEM_LIMIT = 56 * 1024 * 1024


def _cparams(*sem):
    return pltpu.CompilerParams(dimension_semantics=sem, vmem_limit_bytes=VMEM_LIMIT)


def _sds(shape, dtype):
    return jax.ShapeDtypeStruct(shape, dtype)


def _nt_dot(a, b):
    return lax.dot_general(a, b, (((1,), (1,)), ((), ())), preferred_element_type=F32)


def _layer_norm(r, g, b):
    mu = jnp.mean(r, -1, keepdims=True)
    rc = r - mu
    var = jnp.mean(rc * rc, -1, keepdims=True)
    return rc * lax.rsqrt(var + LN_EPS) * g + b


def _mod_kernel(c_ref, w_ref, b_ref, o_ref):
    cv = c_ref[...]
    s = cv * jax.nn.sigmoid(cv)
    o_ref[0] = jnp.dot(s, w_ref[0], preferred_element_type=F32, precision=HIGHEST) + b_ref[0]


def _modulation(cvec, mod_w, mod_b):
    depth, d, w6 = mod_w.shape
    tn = 1536
    return pl.pallas_call(
        _mod_kernel,
        out_shape=_sds((depth, cvec.shape[0], w6), F32),
        grid=(depth, w6 // tn),
        in_specs=[pl.BlockSpec(cvec.shape, lambda l, j: (0, 0)),
                  pl.BlockSpec((1, d, tn), lambda l, j: (l, 0, j)),
                  pl.BlockSpec((1, 1, tn), lambda l, j: (l, 0, j))],
        out_specs=pl.BlockSpec((1, cvec.shape[0], tn), lambda l, j: (l, 0, j)),
        compiler_params=_cparams("arbitrary", "arbitrary"),
        name="modulation",
    )(cvec, mod_w, mod_b.reshape(depth, 1, w6))


def _rope_tables(n):
    rows = n // GRID_W
    row = jnp.repeat(jnp.arange(rows, dtype=F32), GRID_W)
    col = jnp.tile(jnp.arange(GRID_W, dtype=F32), rows)
    inv = ROPE_BASE ** (-jnp.arange(ROPE_FREQS, dtype=F32) / ROPE_FREQS)
    ang_r = row[:, None] * inv[None, :]
    ang_c = col[:, None] * inv[None, :]
    zeros = jnp.zeros_like(ang_r)
    cos64 = jnp.concatenate([jnp.cos(ang_r), jnp.cos(ang_r), jnp.cos(ang_c), jnp.cos(ang_c)], -1)
    sa64 = jnp.concatenate([-jnp.sin(ang_r), zeros, -jnp.sin(ang_c), zeros], -1)
    sb64 = jnp.concatenate([zeros, jnp.sin(ang_r), zeros, jnp.sin(ang_c)], -1)
    return tuple(jnp.tile(t, (1, LANES // HEAD_DIM)) for t in (cos64, sa64, sb64))


def _rot(t, tabs):
    if tabs is None:
        return t
    cos, sa, sb = tabs
    return t * cos + pltpu.roll(t, LANES - ROPE_FREQS, 1) * sa + pltpu.roll(t, ROPE_FREQS, 1) * sb


def _dup_halves(t):
    lo = lax.broadcasted_iota(jnp.int32, t.shape, 1) < HALF
    ta = jnp.where(lo, t, 0.0)
    tb = t - ta
    return ta + pltpu.roll(ta, HALF, 1), tb + pltpu.roll(tb, HALF, 1)


def _proj0_kernel(*refs, rope):
    if rope:
        x_ref, sc_ref, sh_ref, w_ref, cos_ref, sa_ref, sb_ref, q_ref, k_ref, v_ref, u_ref = refs
        tabs = (cos_ref[...], sa_ref[...], sb_ref[...])
    else:
        x_ref, sc_ref, sh_ref, w_ref, q_ref, k_ref, v_ref, u_ref = refs
        tabs = None
    h = (x_ref[0] * (1.0 + sc_ref[0]) + sh_ref[0]).astype(BF16)
    r = jnp.dot(h, w_ref[...], preferred_element_type=F32)
    scale = HEAD_DIM ** -0.5
    for s in range(SWA_Q_W // LANES):
        q_ref[0, :, s * LANES:(s + 1) * LANES] = (_rot(r[:, s * LANES:(s + 1) * LANES], tabs) * scale).astype(BF16)
    k0, k1 = _dup_halves(_rot(r[:, SWA_Q_W:SWA_Q_W + LANES], tabs))
    v0, v1 = _dup_halves(r[:, SWA_Q_W + LANES:SWA_Q_W + 2 * LANES])
    k_ref[0, 0] = k0.astype(BF16)
    k_ref[0, 1] = k1.astype(BF16)
    v_ref[0, 0] = v0.astype(BF16)
    v_ref[0, 1] = v1.astype(BF16)
    u_ref[0] = r[:, SWA_Q_W + 2 * LANES:]


def _proj0(x, sc, sh, w_bf16, tabs, tm):
    b, n, d = x.shape
    rope = tabs is not None
    vec = pl.BlockSpec((1, 1, d), lambda i, j: (i, 0, 0))
    in_specs = [pl.BlockSpec((1, tm, d), lambda i, j: (i, j, 0)), vec, vec,
                pl.BlockSpec(w_bf16.shape, lambda i, j: (0, 0))]
    args = [x, sc, sh, w_bf16]
    if rope:
        in_specs += [pl.BlockSpec((tm, LANES), lambda i, j: (j, 0))] * 3
        args += list(tabs)
    kv_spec = pl.BlockSpec((1, SWA_KV_HEADS, tm, LANES), lambda i, j: (i, 0, j, 0))
    return pl.pallas_call(
        functools.partial(_proj0_kernel, rope=rope),
        out_shape=(_sds((b, n, SWA_Q_W), BF16), _sds((b, SWA_KV_HEADS, n, LANES), BF16),
                   _sds((b, SWA_KV_HEADS, n, LANES), BF16), _sds((b, n, SSM_WIDTH), F32)),
        grid=(b, n // tm),
        in_specs=in_specs,
        out_specs=(pl.BlockSpec((1, tm, SWA_Q_W), lambda i, j: (i, j, 0)), kv_spec, kv_spec,
                   pl.BlockSpec((1, tm, SSM_WIDTH), lambda i, j: (i, j, 0))),
        compiler_params=_cparams("parallel", "parallel"),
        name="proj0_rope" if rope else "proj0_ctx",
    )(*args)


def _swa_kernel(*refs, tq, local, n_lat):
    if local:
        sink_ref, q_ref, k_ref, v_ref, kc_ref, vc_ref, o_ref = refs
    else:
        sink_ref, q_ref, kc_ref, vc_ref, o_ref = refs
    j = pl.program_id(1)
    rows = 4 * tq
    lo = lax.broadcasted_iota(jnp.int32, (tq, LANES), 1) < HALF
    rown = lax.broadcasted_iota(jnp.int32, (rows, 1), 0)
    if local:
        span = 3 * SWA_BLOCK
        start = pl.multiple_of(jnp.clip((j - 1) * SWA_BLOCK, 0, n_lat - span), SWA_BLOCK)
        rr = lax.broadcasted_iota(jnp.int32, (rows, span), 0)
        cc = lax.broadcasted_iota(jnp.int32, (rows, span), 1)
        qpos = j * tq + (rr & (tq - 1))
        mask = jnp.abs(qpos - (start + cc)) <= SWA_WINDOW
    for h in range(SWA_KV_HEADS):
        qa = q_ref[0, :, (2 * h) * LANES:(2 * h + 1) * LANES].astype(F32)
        qb = q_ref[0, :, (2 * h + 1) * LANES:(2 * h + 2) * LANES].astype(F32)
        q4 = jnp.concatenate([jnp.where(lo, qa, 0.0), jnp.where(lo, 0.0, qa),
                              jnp.where(lo, qb, 0.0), jnp.where(lo, 0.0, qb)], 0).astype(BF16)
        sink = jnp.where(rown < tq, sink_ref[4 * h],
                         jnp.where(rown < 2 * tq, sink_ref[4 * h + 1],
                                   jnp.where(rown < 3 * tq, sink_ref[4 * h + 2], sink_ref[4 * h + 3])))
        s_ctx = _nt_dot(q4, kc_ref[0, h])
        m = jnp.maximum(jnp.max(s_ctx, -1, keepdims=True), sink)
        if local:
            s_loc = jnp.where(mask, _nt_dot(q4, k_ref[0, h, pl.ds(start, span), :]), NEG_INF)
            m = jnp.maximum(m, jnp.max(s_loc, -1, keepdims=True))
        p_ctx = jnp.exp(s_ctx - m)
        den = jnp.sum(p_ctx, -1, keepdims=True) + jnp.exp(sink - m)
        o4 = jnp.dot(p_ctx.astype(BF16), vc_ref[0, h], preferred_element_type=F32)
        if local:
            p_loc = jnp.exp(s_loc - m)
            den = den + jnp.sum(p_loc, -1, keepdims=True)
            o4 = o4 + jnp.dot(p_loc.astype(BF16), v_ref[0, h, pl.ds(start, span), :], preferred_element_type=F32)
        o4 = o4 * (1.0 / den)
        o_ref[0, :, (2 * h) * LANES:(2 * h + 1) * LANES] = jnp.where(lo, o4[0:tq], o4[tq:2 * tq]).astype(BF16)
        o_ref[0, :, (2 * h + 1) * LANES:(2 * h + 2) * LANES] = jnp.where(lo, o4[2 * tq:3 * tq], o4[3 * tq:]).astype(BF16)


def _swa_attention(sink, q, k, v, kc, vc):
    b, n, _ = q.shape
    nc = kc.shape[2]
    tq = SWA_BLOCK
    full = lambda m: pl.BlockSpec((1, SWA_KV_HEADS, m, LANES), lambda i, j: (i, 0, 0, 0))
    return pl.pallas_call(
        functools.partial(_swa_kernel, tq=tq, local=True, n_lat=n),
        out_shape=_sds((b, n, SWA_Q_W), BF16),
        grid=(b, n // tq),
        in_specs=[pl.BlockSpec(memory_space=pltpu.SMEM),
                  pl.BlockSpec((1, tq, SWA_Q_W), lambda i, j: (i, j, 0)),
                  full(n), full(n), full(nc), full(nc)],
        out_specs=pl.BlockSpec((1, tq, SWA_Q_W), lambda i, j: (i, j, 0)),
        compiler_params=_cparams("parallel", "arbitrary"),
        name="swa_attention",
    )(sink, q, k, v, kc, vc)


def _ctx_attention(sink, qc, kc, vc):
    b, nc, _ = qc.shape
    full = pl.BlockSpec((1, SWA_KV_HEADS, nc, LANES), lambda i, j: (i, 0, 0, 0))
    return pl.pallas_call(
        functools.partial(_swa_kernel, tq=nc, local=False, n_lat=0),
        out_shape=_sds((b, nc, SWA_Q_W), BF16),
        grid=(b, 1),
        in_specs=[pl.BlockSpec(memory_space=pltpu.SMEM),
                  pl.BlockSpec((1, nc, SWA_Q_W), lambda i, j: (i, 0, 0)), full, full],
        out_specs=pl.BlockSpec((1, nc, SWA_Q_W), lambda i, j: (i, 0, 0)),
        compiler_params=_cparams("parallel", "arbitrary"),
        name="ctx_attention",
    )(sink, qc, kc, vc)


def _ssm_matrices(a_re, a_im, log_step, b_re, b_im, c_re, c_im):
    L = SSM_CHUNK
    dt = jnp.exp(log_step.astype(F32))[..., None]
    lam = lax.complex(dt * a_re.astype(F32), dt * a_im.astype(F32))
    a_c = lax.complex(a_re.astype(F32), a_im.astype(F32))
    abar = jnp.exp(lam)
    bb = ((abar - 1.0) / a_c)[..., None] * lax.complex(b_re.astype(F32), b_im.astype(F32))
    cc = lax.complex(c_re.astype(F32), c_im.astype(F32))
    m = jnp.arange(L + 1, dtype=F32)
    pw = jnp.exp(m[:, None, None, None] * lam[None])
    kern = jnp.real(jnp.einsum('dgop,mdgp,dgpc->dmgoc', cc, pw[:L], bb))
    ii = jnp.arange(L)[None, :]
    jj = jnp.arange(L)[:, None]
    lag_f = ii - jj
    kf = jnp.where((lag_f >= 0)[:, :, None, None, None], kern[0][jnp.clip(lag_f, 0, L - 1)], 0.0)
    kr = jnp.where((lag_f <= 0)[:, :, None, None, None], kern[1][jnp.clip(-lag_f, 0, L - 1)], 0.0)
    m_intra = jnp.transpose(kf + kr, (2, 0, 4, 1, 3)).reshape(SSM_GROUPS, SSM_CW, SSM_CW)
    in_f = pw[L - 1 - jnp.arange(L), 0][..., None] * bb[0][None]
    in_r = pw[jnp.arange(L), 1][..., None] * bb[1][None]
    to_rows = lambda t: jnp.transpose(t, (1, 0, 3, 2)).reshape(SSM_GROUPS, SSM_CW, SSM_STATE)
    m_in = jnp.concatenate([to_rows(jnp.real(in_f)), to_rows(jnp.real(in_r)),
                            to_rows(jnp.imag(in_f)), to_rows(jnp.imag(in_r))], -1)
    st_f = cc[0][None] * pw[1 + jnp.arange(L), 0][:, :, None, :]
    st_r = cc[1][None] * pw[L - jnp.arange(L), 1][:, :, None, :]
    to_cols = lambda t: jnp.transpose(t, (1, 3, 0, 2)).reshape(SSM_GROUPS, SSM_STATE, SSM_CW)
    m_state = jnp.concatenate([to_cols(jnp.real(st_f)), to_cols(jnp.real(st_r)),
                               -to_cols(jnp.imag(st_f)), -to_cols(jnp.imag(st_r))], 1)
    al = pw[L]
    a_l = jnp.stack([jnp.concatenate([jnp.real(al[0]), jnp.real(al[1])], -1),
                     jnp.concatenate([jnp.imag(al[0]), jnp.imag(al[1])], -1)], 1)
    return m_in.astype(BF16), m_intra.astype(BF16), m_state.astype(BF16), a_l


def _ssm_kernel(x_ref, min_ref, mintra_ref, mstate_ref, al_ref, y_ref, v_scr, s_scr, *, nb, nc_ctx, n_chunks):
    xv = x_ref[0]
    v_scr[...] = jnp.dot(xv, min_ref[0], preferred_element_type=F32)
    ar = al_ref[0, 0:1, :]
    ai = al_ref[0, 1:2, :]
    lo = lax.broadcasted_iota(jnp.int32, (nb, LANES), 1) < HALF

    def body(k, carry):
        sre, sim = carry
        kr = jnp.where(k < nc_ctx, nc_ctx - 1 - k, n_chunks - 1 + nc_ctx - k)
        rf = pl.multiple_of(k * nb, nb)
        rr = pl.multiple_of(kr * nb, nb)
        s_scr[pl.ds(rf, nb), 0:HALF] = sre[:, 0:HALF]
        s_scr[pl.ds(rr, nb), HALF:LANES] = sre[:, HALF:LANES]
        s_scr[pl.ds(rf, nb), LANES:LANES + HALF] = sim[:, 0:HALF]
        s_scr[pl.ds(rr, nb), LANES + HALF:2 * LANES] = sim[:, HALF:LANES]
        vre = jnp.where(lo, v_scr[pl.ds(rf, nb), 0:LANES], v_scr[pl.ds(rr, nb), 0:LANES])
        vim = jnp.where(lo, v_scr[pl.ds(rf, nb), LANES:2 * LANES], v_scr[pl.ds(rr, nb), LANES:2 * LANES])
        return ar * sre - ai * sim + vre, ar * sim + ai * sre + vim

    zero = jnp.zeros((nb, LANES), F32)
    lax.fori_loop(0, n_chunks, body, (zero, zero))
    y_ref[0] = (jnp.dot(xv, mintra_ref[0], preferred_element_type=F32)
                + jnp.dot(s_scr[...].astype(BF16), mstate_ref[0], preferred_element_type=F32))


def _ssm_scan(u, uc, mats):
    m_in, m_intra, m_state, a_l = mats
    b, n, _ = u.shape
    nc = uc.shape[1]
    nc_ctx = nc // SSM_CHUNK
    n_chunks = (n + nc) // SSM_CHUNK
    r = n_chunks * b
    ua = jnp.concatenate([uc, u], 1).astype(BF16)
    xg = ua.reshape(b, n_chunks, SSM_CHUNK, SSM_GROUPS, SSM_GROUP).transpose(3, 1, 0, 2, 4).reshape(SSM_GROUPS, r, SSM_CW)
    mat = pl.BlockSpec((1, SSM_CW, SSM_CW), lambda g: (g, 0, 0))
    yg = pl.pallas_call(
        functools.partial(_ssm_kernel, nb=b, nc_ctx=nc_ctx, n_chunks=n_chunks),
        out_shape=_sds((SSM_GROUPS, r, SSM_CW), F32),
        grid=(SSM_GROUPS,),
        in_specs=[pl.BlockSpec((1, r, SSM_CW), lambda g: (g, 0, 0)), mat, mat, mat,
                  pl.BlockSpec((1, 2, LANES), lambda g: (g, 0, 0))],
        out_specs=pl.BlockSpec((1, r, SSM_CW), lambda g: (g, 0, 0)),
        scratch_shapes=[pltpu.VMEM((r, SSM_CW), F32), pltpu.VMEM((r, SSM_CW), F32)],
        compiler_params=_cparams("parallel"),
        name="ssm_scan",
    )(xg, m_in, m_intra, m_state, a_l)
    ya = yg.reshape(SSM_GROUPS, n_chunks, b, SSM_CHUNK, SSM_GROUP).transpose(2, 1, 3, 0, 4).reshape(b, n + nc, SSM_WIDTH)
    return ya[:, nc:], ya[:, :nc]


def _post_kernel(*refs, alpha, with_ssm):
    if with_ssm:
        (att_ref, ys_ref, u_ref, dsk_ref, gw_ref, gb_ref, wo_ref, x_ref, g1_ref, lg_ref, lb_ref, sc2_ref, sh2_ref,
         x1_ref, h2_ref) = refs
        y = ys_ref[0] + u_ref[0] * dsk_ref[...]
        gl = jax.nn.gelu(y)
        gate = jax.nn.sigmoid(jnp.dot(gl.astype(BF16), gw_ref[...], preferred_element_type=F32) + gb_ref[...])
        ssm = (gl * gate).astype(BF16)
        mix = (jnp.dot(att_ref[0], wo_ref[0:SWA_Q_W, :], preferred_element_type=F32)
               + jnp.dot(ssm, wo_ref[SWA_Q_W:, :], preferred_element_type=F32))
    else:
        att_ref, wo_ref, x_ref, g1_ref, lg_ref, lb_ref, sc2_ref, sh2_ref, x1_ref, h2_ref = refs
        mix = jnp.dot(att_ref[0], wo_ref[...], preferred_element_type=F32)
    x1 = _layer_norm(alpha * x_ref[0] + g1_ref[0] * mix, lg_ref[...], lb_ref[...])
    x1_ref[0] = x1
    h2_ref[0] = x1 * (1.0 + sc2_ref[0]) + sh2_ref[0]


def _post(att, ssm_args, w_out_bf16, x, g1, ln_g, ln_b, sc2, sh2, alpha, tm):
    b, n, d = x.shape
    tok = lambda w: pl.BlockSpec((1, tm, w), lambda i, j: (i, j, 0))
    vec = pl.BlockSpec((1, 1, d), lambda i, j: (i, 0, 0))
    const = lambda a: pl.BlockSpec(a.shape, lambda i, j: (0,) * a.ndim)
    in_specs = [tok(att.shape[-1])]
    args = [att]
    if ssm_args is not None:
        ys, u, dsk, gw, gb = ssm_args
        in_specs += [tok(SSM_WIDTH), tok(SSM_WIDTH), const(dsk), const(gw), const(gb)]
        args += [ys, u, dsk, gw, gb]
    in_specs += [const(w_out_bf16), tok(d), vec, const(ln_g), const(ln_b), vec, vec]
    args += [w_out_bf16, x, g1, ln_g, ln_b, sc2, sh2]
    return pl.pallas_call(
        functools.partial(_post_kernel, alpha=alpha, with_ssm=ssm_args is not None),
        out_shape=(_sds((b, n, d), F32), _sds((b, n, d), F32)),
        grid=(b, n // tm),
        in_specs=in_specs,
        out_specs=(tok(d), tok(d)),
        compiler_params=_cparams("parallel", "parallel"),
        name="post_mixer_ssm" if ssm_args is not None else "post_mixer",
    )(*args)


def _router_kernel(h_ref, w_ref, b_ref, g_ref):
    logits = jnp.dot(h_ref[...], w_ref[...], preferred_element_type=F32, precision=HIGHEST) + b_ref[...]
    lane = lax.broadcasted_iota(jnp.int32, logits.shape, 1)
    big = jnp.int32(4 * LANES)
    is_grp = lane < MOE_GROUPS
    gl = jnp.where(is_grp, logits, NEG_INF)
    gmax = jnp.max(gl, -1, keepdims=True)
    gsum = jnp.sum(jnp.where(is_grp, jnp.exp(gl - gmax), 0.0), -1, keepdims=True)
    gp = 1.0 / gsum
    gi = jnp.min(jnp.where(gl == gmax, lane, big), -1, keepdims=True)
    e0 = ROUTER_EXPERT_LANE0 + MOE_EPG * gi
    le = jnp.where((lane >= e0) & (lane < e0 + MOE_EPG), logits, NEG_INF)
    m1 = jnp.max(le, -1, keepdims=True)
    i1 = jnp.min(jnp.where(le == m1, lane, big), -1, keepdims=True)
    le2 = jnp.where(lane == i1, NEG_INF, le)
    m2 = jnp.max(le2, -1, keepdims=True)
    i2 = jnp.min(jnp.where(le2 == m2, lane, big), -1, keepdims=True)
    t = jnp.exp(m2 - m1)
    w1 = gp / (1.0 + t)
    w2 = gp * t / (1.0 + t)
    g_ref[...] = jnp.where(lane == i1, w1, 0.0) + jnp.where(lane == i2, w2, 0.0)


def _router(tok, wg, bg, we, be, tm):
    t, d = tok.shape
    w = jnp.zeros((d, LANES), F32)
    w = w.at[:, :MOE_GROUPS].set(wg)
    w = w.at[:, ROUTER_EXPERT_LANE0:ROUTER_EXPERT_LANE0 + MOE_EXPERTS].set(
        jnp.transpose(we, (1, 0, 2)).reshape(d, MOE_EXPERTS))
    bias = jnp.zeros((1, LANES), F32)
    bias = bias.at[0, :MOE_GROUPS].set(bg)
    bias = bias.at[0, ROUTER_EXPERT_LANE0:ROUTER_EXPERT_LANE0 + MOE_EXPERTS].set(be.reshape(-1))
    return pl.pallas_call(
        _router_kernel,
        out_shape=_sds((t, LANES), F32),
        grid=(t // tm,),
        in_specs=[pl.BlockSpec((tm, d), lambda i: (i, 0)), pl.BlockSpec((d, LANES), lambda i: (0, 0)),
                  pl.BlockSpec((1, LANES), lambda i: (0, 0))],
        out_specs=pl.BlockSpec((tm, LANES), lambda i: (i, 0)),
        compiler_params=_cparams("parallel"),
        name="moe_router",
    )(tok, w, bias)


def _moe_kernel(h_ref, g_ref, w1_ref, w3_ref, w2_ref, o_ref, hb_scr):
    e = pl.program_id(1)

    @pl.when(e == 0)
    def _():
        hb_scr[...] = h_ref[...].astype(BF16)
        o_ref[...] = jnp.zeros_like(o_ref)

    hb = hb_scr[...]
    h1 = jnp.dot(hb, w1_ref[0], preferred_element_type=F32)
    h3 = jnp.dot(hb, w3_ref[0], preferred_element_type=F32)
    lane = lax.broadcasted_iota(jnp.int32, g_ref.shape, 1)
    gcol = jnp.sum(jnp.where(lane == e + ROUTER_EXPERT_LANE0, g_ref[...], 0.0), -1, keepdims=True)
    hh = (h1 * jax.nn.sigmoid(h1) * h3 * gcol).astype(BF16)
    o_ref[...] += jnp.dot(hh, w2_ref[0], preferred_element_type=F32)


def _moe(tok, gates, w1, w3, w2, tm):
    t, d = tok.shape
    ne, _, f = w1.shape
    return pl.pallas_call(
        _moe_kernel,
        out_shape=_sds((t, d), F32),
        grid=(t // tm, ne),
        in_specs=[pl.BlockSpec((tm, d), lambda i, e: (i, 0)), pl.BlockSpec((tm, LANES), lambda i, e: (i, 0)),
                  pl.BlockSpec((1, d, f), lambda i, e: (e, 0, 0)), pl.BlockSpec((1, d, f), lambda i, e: (e, 0, 0)),
                  pl.BlockSpec((1, f, d), lambda i, e: (e, 0, 0))],
        out_specs=pl.BlockSpec((tm, d), lambda i, e: (i, 0)),
        scratch_shapes=[pltpu.VMEM((tm, d), BF16)],
        compiler_params=_cparams("parallel", "arbitrary"),
        name="moe_experts",
    )(tok, gates, w1, w3, w2)


def _ln2_kernel(x_ref, f_ref, g2_ref, lg_ref, lb_ref, o_ref, *, alpha):
    o_ref[0] = _layer_norm(alpha * x_ref[0] + g2_ref[0] * f_ref[0], lg_ref[...], lb_ref[...])


def _ln2(x1, f, g2, ln_g, ln_b, alpha, tm):
    b, n, d = x1.shape
    tok = pl.BlockSpec((1, tm, d), lambda i, j: (i, j, 0))
    const = pl.BlockSpec((1, d), lambda i, j: (0, 0))
    return pl.pallas_call(
        functools.partial(_ln2_kernel, alpha=alpha),
        out_shape=_sds((b, n, d), F32),
        grid=(b, n // tm),
        in_specs=[tok, tok, pl.BlockSpec((1, 1, d), lambda i, j: (i, 0, 0)), const, const],
        out_specs=tok,
        compiler_params=_cparams("parallel", "parallel"),
        name="deepnorm_ln2",
    )(x1, f, g2, ln_g, ln_b)


def _proj1_kernel(*refs, rope, with_q):
    x_ref, sc_ref, sh_ref, w_ref = refs[:4]
    refs = refs[4:]
    tabs = None
    if rope:
        tabs = (refs[0][...], refs[1][...], refs[2][...])
        refs = refs[3:]
    h = (x_ref[0] * (1.0 + sc_ref[0]) + sh_ref[0]).astype(BF16)
    r = jnp.dot(h, w_ref[...], preferred_element_type=F32)
    off = 0
    if with_q:
        q_ref, k_ref, v_ref = refs
        for hd in range(DIF_HEADS):
            q_ref[0, hd] = (_rot(r[:, hd * LANES:(hd + 1) * LANES], tabs) * HEAD_DIM ** -0.5).astype(BF16)
        off = DIF_QK_W
    else:
        k_ref, v_ref = refs
    for hd in range(DIF_HEADS):
        k_ref[0, hd] = _rot(r[:, off + hd * LANES:off + (hd + 1) * LANES], tabs).astype(BF16)
        v_ref[0, hd] = r[:, off + DIF_QK_W + hd * LANES:off + DIF_QK_W + (hd + 1) * LANES].astype(BF16)


def _proj1(x, sc, sh, w_bf16, tabs, with_q, tm):
    b, n, d = x.shape
    rope = tabs is not None
    vec = pl.BlockSpec((1, 1, d), lambda i, j: (i, 0, 0))
    in_specs = [pl.BlockSpec((1, tm, d), lambda i, j: (i, j, 0)), vec, vec,
                pl.BlockSpec(w_bf16.shape, lambda i, j: (0, 0))]
    args = [x, sc, sh, w_bf16]
    if rope:
        in_specs += [pl.BlockSpec((tm, LANES), lambda i, j: (j, 0))] * 3
        args += list(tabs)
    hm = pl.BlockSpec((1, DIF_HEADS, tm, LANES), lambda i, j: (i, 0, j, 0))
    n_out = 3 if with_q else 2
    return pl.pallas_call(
        functools.partial(_proj1_kernel, rope=rope, with_q=with_q),
        out_shape=(_sds((b, DIF_HEADS, n, LANES), BF16),) * n_out,
        grid=(b, n // tm),
        in_specs=in_specs,
        out_specs=(hm,) * n_out,
        compiler_params=_cparams("parallel", "parallel"),
        name="proj1_qkv" if with_q else "proj1_kv_ctx",
    )(*args)


def _diff_kernel(lam_ref, q_ref, kl_ref, kc_ref, vl_ref, vc_ref, g_ref, o_ref, *, tq, out_scale):
    q = q_ref[0, 0].astype(F32)
    lo = lax.broadcasted_iota(jnp.int32, q.shape, 1) < HALF
    q2 = jnp.concatenate([jnp.where(lo, q, 0.0), jnp.where(lo, 0.0, q)], 0).astype(BF16)
    s_l = _nt_dot(q2, kl_ref[0, 0])
    s_c = _nt_dot(q2, kc_ref[0, 0])
    m = jnp.maximum(jnp.max(s_l, -1, keepdims=True), jnp.max(s_c, -1, keepdims=True))
    p_l = jnp.exp(s_l - m)
    p_c = jnp.exp(s_c - m)
    inv = 1.0 / (jnp.sum(p_l, -1, keepdims=True) + jnp.sum(p_c, -1, keepdims=True))
    w0 = inv[:tq]
    w1 = lam_ref[0] * inv[tq:]
    pd_l = (p_l[:tq] * w0 - p_l[tq:] * w1).astype(BF16)
    pd_c = (p_c[:tq] * w0 - p_c[tq:] * w1).astype(BF16)
    o = (jnp.dot(pd_l, vl_ref[0, 0], preferred_element_type=F32)
         + jnp.dot(pd_c, vc_ref[0, 0], preferred_element_type=F32))
    o = o * lax.rsqrt(jnp.mean(o * o, -1, keepdims=True) + RMS_EPS) * g_ref[...]
    o_ref[0] = (o * out_scale).astype(BF16)


def _diff_attention(lam, q, kl, kc, vl, vc, subln_g, lam_init, tq):
    b, nh, n, _ = q.shape
    nc = kc.shape[2]
    kv = lambda m: pl.BlockSpec((1, 1, m, LANES), lambda i, h, j: (i, h, 0, 0))
    return pl.pallas_call(
        functools.partial(_diff_kernel, tq=tq, out_scale=1.0 - lam_init),
        out_shape=_sds((b, n, nh * LANES), BF16),
        grid=(b, nh, n // tq),
        in_specs=[pl.BlockSpec(memory_space=pltpu.SMEM),
                  pl.BlockSpec((1, 1, tq, LANES), lambda i, h, j: (i, h, j, 0)),
                  kv(n), kv(nc), kv(n), kv(nc),
                  pl.BlockSpec((1, LANES), lambda i, h, j: (0, 0))],
        out_specs=pl.BlockSpec((1, tq, LANES), lambda i, h, j: (i, j, h)),
        compiler_params=_cparams("parallel", "parallel", "arbitrary"),
        name="diff_attention",
    )(lam, q, kl, kc, vl, vc, subln_g)


def _moe_block(tok, layer, moe_wg, moe_bg, moe_we, moe_be, moe_w1, moe_w3, moe_w2):
    gates = _router(tok, moe_wg[layer], moe_bg[layer], moe_we[layer], moe_be[layer], tm=512)
    return _moe(tok, gates, moe_w1[layer].astype(BF16), moe_w3[layer].astype(BF16), moe_w2[layer].astype(BF16),
                tm=1024)


def kernel(x, c, ctx, c_ctx, mod_w, mod_b, ln1_g, ln1_b, ln2_g, ln2_b, swa_ssm_w_in, swa_ssm_w_out, swa_sink, ssm_a_re, ssm_a_im, ssm_log_step, ssm_b_re, ssm_b_im, ssm_c_re, ssm_c_im, ssm_d, ssm_glu_w, ssm_glu_b, dif_w_in, dif_w_out, dif_lam_q1, dif_lam_k1, dif_lam_q2, dif_lam_k2, dif_subln_g, moe_wg, moe_bg, moe_we, moe_be, moe_w1, moe_w3, moe_w2):
    bsz, n, d = x.shape
    ctx_len = ctx.shape[1]
    depth = mod_w.shape[0]
    alpha = (2 * depth) ** 0.25
    tabs = _rope_tables(n)

    n_vec = 16
    cvec = jnp.zeros((n_vec, d), F32).at[:bsz].set(c).at[bsz].set(c_ctx)
    mods = _modulation(cvec, mod_w, mod_b)

    xl, xc = x, ctx
    for layer in range(depth):
        need_ctx = layer < depth - 1
        i = layer // 2
        lat = [mods[layer, :bsz, k * d:(k + 1) * d].reshape(bsz, 1, d) for k in range(6)]
        cx = [jnp.broadcast_to(mods[layer, bsz, k * d:(k + 1) * d].reshape(1, 1, d), (bsz, 1, d)) for k in range(6)]
        sh1, sc1, g1, sh2, sc2, g2 = lat
        csh1, csc1, cg1, csh2, csc2, cg2 = cx
        lg1, lb1 = ln1_g[layer].reshape(1, d), ln1_b[layer].reshape(1, d)
        lg2, lb2 = ln2_g[layer].reshape(1, d), ln2_b[layer].reshape(1, d)
        if layer % 2 == 0:
            w_in = swa_ssm_w_in[i].astype(BF16)
            w_out = swa_ssm_w_out[i].astype(BF16)
            q, k, v, u = _proj0(xl, sc1, sh1, w_in, tabs, tm=512)
            qc, kc, vc, uc = _proj0(xc, csc1, csh1, w_in, None, tm=ctx_len)
            sink = swa_sink[i].astype(F32)
            att = _swa_attention(sink, q, k, v, kc, vc)
            mats = _ssm_matrices(ssm_a_re[i], ssm_a_im[i], ssm_log_step[i], ssm_b_re[i], ssm_b_im[i],
                                 ssm_c_re[i], ssm_c_im[i])
            ys, ysc = _ssm_scan(u, uc, mats)
            glu = (ssm_d[i].reshape(1, SSM_WIDTH).astype(F32), ssm_glu_w[i].astype(BF16),
                   ssm_glu_b[i].reshape(1, SSM_WIDTH).astype(F32))
            x1, h2 = _post(att, (ys, u) + glu, w_out, xl, g1, lg1, lb1, sc2, sh2, alpha, tm=512)
            if need_ctx:
                att_c = _ctx_attention(sink, qc, kc, vc)
                xc1, hc2 = _post(att_c, (ysc, uc) + glu, w_out, xc, cg1, lg1, lb1, csc2, csh2, alpha, tm=ctx_len)
        else:
            lam_init = 0.8 - 0.6 * math.exp(-0.3 * layer)
            w_in = dif_w_in[i].astype(BF16)
            w_out = dif_w_out[i].astype(BF16)
            q, k, v = _proj1(xl, sc1, sh1, w_in, tabs, True, tm=512)
            kc, vc = _proj1(xc, csc1, csh1, w_in[:, DIF_QK_W:], None, False, tm=ctx_len)
            lam = (jnp.exp(jnp.sum(dif_lam_q1[i].astype(F32) * dif_lam_k1[i].astype(F32)))
                   - jnp.exp(jnp.sum(dif_lam_q2[i].astype(F32) * dif_lam_k2[i].astype(F32))) + lam_init).reshape(1)
            att = _diff_attention(lam, q, k, kc, v, vc, dif_subln_g[i].reshape(1, DIF_V_HEAD).astype(F32),
                                  lam_init, tq=256)
            x1, h2 = _post(att, None, w_out, xl, g1, lg1, lb1, sc2, sh2, alpha, tm=512)
            if need_ctx:
                raise NotImplementedError("a differential-attention layer followed by another layer")
        if need_ctx:
            tok = jnp.concatenate([h2.reshape(-1, d), hc2.reshape(-1, d)], 0)
        else:
            tok = h2.reshape(-1, d)
        f = _moe_block(tok, layer, moe_wg, moe_bg, moe_we, moe_be, moe_w1, moe_w3, moe_w2)
        xl = _ln2(x1, f[:bsz * n].reshape(bsz, n, d), g2, lg2, lb2, alpha, tm=512)
        if need_ctx:
            xc = _ln2(xc1, f[bsz * n:].reshape(bsz, ctx_len, d), cg2, lg2, lb2, alpha, tm=ctx_len)
    return xl
```

```python
import functools
import math

import jax
import jax.numpy as jnp
from jax import lax
from jax.experimental import pallas as pl
from jax.experimental.pallas import tpu as pltpu

F32 = jnp.float32
BF16 = jnp.bfloat16
HIGHEST = lax.Precision.HIGHEST

D_MODEL = 1024
GRID_W = 64
HEAD_DIM = 64
ROPE_BASE = 10000.0
ROPE_FREQS = HEAD_DIM // 4
LN_EPS = 1e-5
RMS_EPS = 1e-5
NEG_INF = -1e30
LANES = 128
HALF = LANES // 2

SWA_HEADS = 8
SWA_KV_HEADS = 2
SWA_WINDOW = 128
SWA_BLOCK = 128
SWA_Q_W = SWA_HEADS * HEAD_DIM
SWA_KV_W = SWA_KV_HEADS * HEAD_DIM

SSM_WIDTH = D_MODEL // 2
SSM_GROUP = 16
SSM_GROUPS = SSM_WIDTH // SSM_GROUP
SSM_STATE = 64
SSM_CHUNK = 16
SSM_CW = SSM_CHUNK * SSM_GROUP

AB_IN_W = SWA_Q_W + 2 * SWA_KV_W + SSM_WIDTH

DIF_HEADS = D_MODEL // (2 * HEAD_DIM)
DIF_QK_W = DIF_HEADS * 2 * HEAD_DIM
DIF_V_HEAD = 2 * HEAD_DIM
DIF_V_W = DIF_HEADS * DIF_V_HEAD

MOE_GROUPS = 4
MOE_EPG = 8
MOE_EXPERTS = MOE_GROUPS * MOE_EPG
MOE_HIDDEN = D_MODEL // 4
MOE_TOPK = 2
SUBLANES = 8
ROUTER_EXPERT_ROW0 = SUBLANES
MOE_ROW_TILE = 512
MOE_DMA_TILE = 256

VMEM_LIMIT = 56 * 1024 * 1024


def _cparams(*sem):
    return pltpu.CompilerParams(dimension_semantics=sem, vmem_limit_bytes=VMEM_LIMIT)


def _sds(shape, dtype):
    return jax.ShapeDtypeStruct(shape, dtype)


def _nt_dot(a, b):
    return lax.dot_general(a, b, (((1,), (1,)), ((), ())), preferred_element_type=F32)


def _layer_norm(r, g, b):
    mu = jnp.mean(r, -1, keepdims=True)
    rc = r - mu
    var = jnp.mean(rc * rc, -1, keepdims=True)
    return rc * lax.rsqrt(var + LN_EPS) * g + b


def _mod_kernel(c_ref, w_ref, b_ref, o_ref):
    cv = c_ref[...]
    s = cv * jax.nn.sigmoid(cv)
    o_ref[0] = jnp.dot(s, w_ref[0], preferred_element_type=F32, precision=HIGHEST) + b_ref[0]


def _modulation(cvec, mod_w, mod_b):
    depth, d, w6 = mod_w.shape
    tn = 1536
    return pl.pallas_call(
        _mod_kernel,
        out_shape=_sds((depth, cvec.shape[0], w6), F32),
        grid=(depth, w6 // tn),
        in_specs=[pl.BlockSpec(cvec.shape, lambda l, j: (0, 0)),
                  pl.BlockSpec((1, d, tn), lambda l, j: (l, 0, j)),
                  pl.BlockSpec((1, 1, tn), lambda l, j: (l, 0, j))],
        out_specs=pl.BlockSpec((1, cvec.shape[0], tn), lambda l, j: (l, 0, j)),
        compiler_params=_cparams("arbitrary", "arbitrary"),
        name="modulation",
    )(cvec, mod_w, mod_b.reshape(depth, 1, w6))


def _rope_tables(n):
    rows = n // GRID_W
    row = jnp.repeat(jnp.arange(rows, dtype=F32), GRID_W)
    col = jnp.tile(jnp.arange(GRID_W, dtype=F32), rows)
    inv = ROPE_BASE ** (-jnp.arange(ROPE_FREQS, dtype=F32) / ROPE_FREQS)
    ang_r = row[:, None] * inv[None, :]
    ang_c = col[:, None] * inv[None, :]
    zeros = jnp.zeros_like(ang_r)
    cos64 = jnp.concatenate([jnp.cos(ang_r), jnp.cos(ang_r), jnp.cos(ang_c), jnp.cos(ang_c)], -1)
    sa64 = jnp.concatenate([-jnp.sin(ang_r), zeros, -jnp.sin(ang_c), zeros], -1)
    sb64 = jnp.concatenate([zeros, jnp.sin(ang_r), zeros, jnp.sin(ang_c)], -1)
    return tuple(jnp.tile(t, (1, LANES // HEAD_DIM)) for t in (cos64, sa64, sb64))


def _rot(t, tabs):
    if tabs is None:
        return t
    cos, sa, sb = tabs
    return t * cos + pltpu.roll(t, LANES - ROPE_FREQS, 1) * sa + pltpu.roll(t, ROPE_FREQS, 1) * sb


def _dup_halves(t):
    lo = lax.broadcasted_iota(jnp.int32, t.shape, 1) < HALF
    ta = jnp.where(lo, t, 0.0)
    tb = t - ta
    return ta + pltpu.roll(ta, HALF, 1), tb + pltpu.roll(tb, HALF, 1)


def _proj0_kernel(*refs, rope):
    if rope:
        x_ref, sc_ref, sh_ref, w_ref, cos_ref, sa_ref, sb_ref, q_ref, k_ref, v_ref, u_ref = refs
        tabs = (cos_ref[...], sa_ref[...], sb_ref[...])
    else:
        x_ref, sc_ref, sh_ref, w_ref, q_ref, k_ref, v_ref, u_ref = refs
        tabs = None
    h = (x_ref[0] * (1.0 + sc_ref[0]) + sh_ref[0]).astype(BF16)
    r = jnp.dot(h, w_ref[...], preferred_element_type=F32)
    scale = HEAD_DIM ** -0.5
    for s in range(SWA_Q_W // LANES):
        q_ref[0, :, s * LANES:(s + 1) * LANES] = (_rot(r[:, s * LANES:(s + 1) * LANES], tabs) * scale).astype(BF16)
    k0, k1 = _dup_halves(_rot(r[:, SWA_Q_W:SWA_Q_W + LANES], tabs))
    v0, v1 = _dup_halves(r[:, SWA_Q_W + LANES:SWA_Q_W + 2 * LANES])
    k_ref[0, 0] = k0.astype(BF16)
    k_ref[0, 1] = k1.astype(BF16)
    v_ref[0, 0] = v0.astype(BF16)
    v_ref[0, 1] = v1.astype(BF16)
    u_ref[0] = r[:, SWA_Q_W + 2 * LANES:]


def _proj0(x, sc, sh, w_bf16, tabs, tm):
    b, n, d = x.shape
    rope = tabs is not None
    vec = pl.BlockSpec((1, 1, d), lambda i, j: (i, 0, 0))
    in_specs = [pl.BlockSpec((1, tm, d), lambda i, j: (i, j, 0)), vec, vec,
                pl.BlockSpec(w_bf16.shape, lambda i, j: (0, 0))]
    args = [x, sc, sh, w_bf16]
    if rope:
        in_specs += [pl.BlockSpec((tm, LANES), lambda i, j: (j, 0))] * 3
        args += list(tabs)
    kv_spec = pl.BlockSpec((1, SWA_KV_HEADS, tm, LANES), lambda i, j: (i, 0, j, 0))
    return pl.pallas_call(
        functools.partial(_proj0_kernel, rope=rope),
        out_shape=(_sds((b, n, SWA_Q_W), BF16), _sds((b, SWA_KV_HEADS, n, LANES), BF16),
                   _sds((b, SWA_KV_HEADS, n, LANES), BF16), _sds((b, n, SSM_WIDTH), F32)),
        grid=(b, n // tm),
        in_specs=in_specs,
        out_specs=(pl.BlockSpec((1, tm, SWA_Q_W), lambda i, j: (i, j, 0)), kv_spec, kv_spec,
                   pl.BlockSpec((1, tm, SSM_WIDTH), lambda i, j: (i, j, 0))),
        compiler_params=_cparams("parallel", "parallel"),
        name="proj0_rope" if rope else "proj0_ctx",
    )(*args)


def _swa_kernel(*refs, tq, local, n_lat):
    if local:
        sink_ref, q_ref, k_ref, v_ref, kc_ref, vc_ref, o_ref = refs
    else:
        sink_ref, q_ref, kc_ref, vc_ref, o_ref = refs
    j = pl.program_id(1)
    rows = 4 * tq
    lo = lax.broadcasted_iota(jnp.int32, (tq, LANES), 1) < HALF
    rown = lax.broadcasted_iota(jnp.int32, (rows, 1), 0)
    if local:
        span = 3 * SWA_BLOCK
        start = pl.multiple_of(jnp.clip((j - 1) * SWA_BLOCK, 0, n_lat - span), SWA_BLOCK)
        rr = lax.broadcasted_iota(jnp.int32, (rows, span), 0)
        cc = lax.broadcasted_iota(jnp.int32, (rows, span), 1)
        qpos = j * tq + (rr & (tq - 1))
        mask = jnp.abs(qpos - (start + cc)) <= SWA_WINDOW
    for h in range(SWA_KV_HEADS):
        qa = q_ref[0, :, (2 * h) * LANES:(2 * h + 1) * LANES].astype(F32)
        qb = q_ref[0, :, (2 * h + 1) * LANES:(2 * h + 2) * LANES].astype(F32)
        q4 = jnp.concatenate([jnp.where(lo, qa, 0.0), jnp.where(lo, 0.0, qa),
                              jnp.where(lo, qb, 0.0), jnp.where(lo, 0.0, qb)], 0).astype(BF16)
        sink = jnp.where(rown < tq, sink_ref[4 * h],
                         jnp.where(rown < 2 * tq, sink_ref[4 * h + 1],
                                   jnp.where(rown < 3 * tq, sink_ref[4 * h + 2], sink_ref[4 * h + 3])))
        s_ctx = _nt_dot(q4, kc_ref[0, h])
        m = jnp.maximum(jnp.max(s_ctx, -1, keepdims=True), sink)
        if local:
            s_loc = jnp.where(mask, _nt_dot(q4, k_ref[0, h, pl.ds(start, span), :]), NEG_INF)
            m = jnp.maximum(m, jnp.max(s_loc, -1, keepdims=True))
        p_ctx = jnp.exp(s_ctx - m)
        den = jnp.sum(p_ctx, -1, keepdims=True) + jnp.exp(sink - m)
        o4 = jnp.dot(p_ctx.astype(BF16), vc_ref[0, h], preferred_element_type=F32)
        if local:
            p_loc = jnp.exp(s_loc - m)
            den = den + jnp.sum(p_loc, -1, keepdims=True)
            o4 = o4 + jnp.dot(p_loc.astype(BF16), v_ref[0, h, pl.ds(start, span), :], preferred_element_type=F32)
        o4 = o4 * (1.0 / den)
        o_ref[0, :, (2 * h) * LANES:(2 * h + 1) * LANES] = jnp.where(lo, o4[0:tq], o4[tq:2 * tq]).astype(BF16)
        o_ref[0, :, (2 * h + 1) * LANES:(2 * h + 2) * LANES] = jnp.where(lo, o4[2 * tq:3 * tq], o4[3 * tq:]).astype(BF16)


def _swa_attention(sink, q, k, v, kc, vc):
    b, n, _ = q.shape
    nc = kc.shape[2]
    tq = SWA_BLOCK
    full = lambda m: pl.BlockSpec((1, SWA_KV_HEADS, m, LANES), lambda i, j: (i, 0, 0, 0))
    return pl.pallas_call(
        functools.partial(_swa_kernel, tq=tq, local=True, n_lat=n),
        out_shape=_sds((b, n, SWA_Q_W), BF16),
        grid=(b, n // tq),
        in_specs=[pl.BlockSpec(memory_space=pltpu.SMEM),
                  pl.BlockSpec((1, tq, SWA_Q_W), lambda i, j: (i, j, 0)),
                  full(n), full(n), full(nc), full(nc)],
        out_specs=pl.BlockSpec((1, tq, SWA_Q_W), lambda i, j: (i, j, 0)),
        compiler_params=_cparams("parallel", "arbitrary"),
        name="swa_attention",
    )(sink, q, k, v, kc, vc)


def _ctx_attention(sink, qc, kc, vc):
    b, nc, _ = qc.shape
    full = pl.BlockSpec((1, SWA_KV_HEADS, nc, LANES), lambda i, j: (i, 0, 0, 0))
    return pl.pallas_call(
        functools.partial(_swa_kernel, tq=nc, local=False, n_lat=0),
        out_shape=_sds((b, nc, SWA_Q_W), BF16),
        grid=(b, 1),
        in_specs=[pl.BlockSpec(memory_space=pltpu.SMEM),
                  pl.BlockSpec((1, nc, SWA_Q_W), lambda i, j: (i, 0, 0)), full, full],
        out_specs=pl.BlockSpec((1, nc, SWA_Q_W), lambda i, j: (i, 0, 0)),
        compiler_params=_cparams("parallel", "arbitrary"),
        name="ctx_attention",
    )(sink, qc, kc, vc)


def _ssm_matrices(a_re, a_im, log_step, b_re, b_im, c_re, c_im):
    L = SSM_CHUNK
    ar = a_re.astype(F32)
    ai = a_im.astype(F32)
    dt = jnp.exp(log_step.astype(F32))[..., None]
    m = jnp.arange(L + 1, dtype=F32)[:, None, None, None]
    mag = jnp.exp(m * (dt * ar)[None])
    pw_re = mag * jnp.cos(m * (dt * ai)[None])
    pw_im = mag * jnp.sin(m * (dt * ai)[None])
    den = ar * ar + ai * ai
    nr = pw_re[1] - 1.0
    coef_re = (nr * ar + pw_im[1] * ai) / den
    coef_im = (pw_im[1] * ar - nr * ai) / den
    br = b_re.astype(F32)
    bi = b_im.astype(F32)
    bb_re = coef_re[..., None] * br - coef_im[..., None] * bi
    bb_im = coef_re[..., None] * bi + coef_im[..., None] * br
    cr = c_re.astype(F32)[None]
    ci = c_im.astype(F32)[None]
    ca_re = cr * pw_re[:, :, :, None, :] - ci * pw_im[:, :, :, None, :]
    ca_im = cr * pw_im[:, :, :, None, :] + ci * pw_re[:, :, :, None, :]
    kern = (jnp.einsum('mdgop,dgpc->dmgoc', ca_re[:L], bb_re, precision=HIGHEST)
            - jnp.einsum('mdgop,dgpc->dmgoc', ca_im[:L], bb_im, precision=HIGHEST))
    ii = jnp.arange(L)[None, :]
    jj = jnp.arange(L)[:, None]
    lag_f = ii - jj
    kf = jnp.where((lag_f >= 0)[:, :, None, None, None], kern[0][jnp.clip(lag_f, 0, L - 1)], 0.0)
    kr = jnp.where((lag_f <= 0)[:, :, None, None, None], kern[1][jnp.clip(-lag_f, 0, L - 1)], 0.0)
    m_intra = jnp.transpose(kf + kr, (2, 0, 4, 1, 3)).reshape(SSM_GROUPS, SSM_CW, SSM_CW)
    pf_re, pf_im = pw_re[L - 1 - jnp.arange(L), 0][..., None], pw_im[L - 1 - jnp.arange(L), 0][..., None]
    pr_re, pr_im = pw_re[jnp.arange(L), 1][..., None], pw_im[jnp.arange(L), 1][..., None]
    to_rows = lambda t: jnp.transpose(t, (1, 0, 3, 2)).reshape(SSM_GROUPS, SSM_CW, SSM_STATE)
    m_in = jnp.concatenate([to_rows(pf_re * bb_re[0] - pf_im * bb_im[0]), to_rows(pr_re * bb_re[1] - pr_im * bb_im[1]),
                            to_rows(pf_re * bb_im[0] + pf_im * bb_re[0]), to_rows(pr_re * bb_im[1] + pr_im * bb_re[1])],
                           -1)
    to_cols = lambda t: jnp.transpose(t, (1, 3, 0, 2)).reshape(SSM_GROUPS, SSM_STATE, SSM_CW)
    fi = 1 + jnp.arange(L)
    ri = L - jnp.arange(L)
    m_state = jnp.concatenate([to_cols(ca_re[fi, 0]), to_cols(ca_re[ri, 1]),
                               -to_cols(ca_im[fi, 0]), -to_cols(ca_im[ri, 1])], 1)
    a_l = jnp.stack([jnp.concatenate([pw_re[L, 0], pw_re[L, 1]], -1),
                     jnp.concatenate([pw_im[L, 0], pw_im[L, 1]], -1)], 1)
    return m_in.astype(BF16), m_intra.astype(BF16), m_state.astype(BF16), a_l


def _ssm_kernel(x_ref, min_ref, mintra_ref, mstate_ref, al_ref, y_ref, v_scr, s_scr, *, nb, nc_ctx, n_chunks):
    xv = x_ref[0]
    v_scr[...] = jnp.dot(xv, min_ref[0], preferred_element_type=F32)
    ar = al_ref[0, 0:1, :]
    ai = al_ref[0, 1:2, :]
    lo = lax.broadcasted_iota(jnp.int32, (nb, LANES), 1) < HALF

    def body(k, carry):
        sre, sim = carry
        kr = jnp.where(k < nc_ctx, nc_ctx - 1 - k, n_chunks - 1 + nc_ctx - k)
        rf = pl.multiple_of(k * nb, nb)
        rr = pl.multiple_of(kr * nb, nb)
        s_scr[pl.ds(rf, nb), 0:HALF] = sre[:, 0:HALF]
        s_scr[pl.ds(rr, nb), HALF:LANES] = sre[:, HALF:LANES]
        s_scr[pl.ds(rf, nb), LANES:LANES + HALF] = sim[:, 0:HALF]
        s_scr[pl.ds(rr, nb), LANES + HALF:2 * LANES] = sim[:, HALF:LANES]
        vre = jnp.where(lo, v_scr[pl.ds(rf, nb), 0:LANES], v_scr[pl.ds(rr, nb), 0:LANES])
        vim = jnp.where(lo, v_scr[pl.ds(rf, nb), LANES:2 * LANES], v_scr[pl.ds(rr, nb), LANES:2 * LANES])
        return ar * sre - ai * sim + vre, ar * sim + ai * sre + vim

    zero = jnp.zeros((nb, LANES), F32)
    lax.fori_loop(0, n_chunks, body, (zero, zero))
    y_ref[0] = (jnp.dot(xv, mintra_ref[0], preferred_element_type=F32)
                + jnp.dot(s_scr[...].astype(BF16), mstate_ref[0], preferred_element_type=F32))


def _ssm_scan(u, uc, mats):
    m_in, m_intra, m_state, a_l = mats
    b, n, _ = u.shape
    nc = uc.shape[1]
    nc_ctx = nc // SSM_CHUNK
    n_chunks = (n + nc) // SSM_CHUNK
    r = n_chunks * b
    ua = jnp.concatenate([uc, u], 1).astype(BF16)
    xg = ua.reshape(b, n_chunks, SSM_CHUNK, SSM_GROUPS, SSM_GROUP).transpose(3, 1, 0, 2, 4).reshape(SSM_GROUPS, r, SSM_CW)
    mat = pl.BlockSpec((1, SSM_CW, SSM_CW), lambda g: (g, 0, 0))
    yg = pl.pallas_call(
        functools.partial(_ssm_kernel, nb=b, nc_ctx=nc_ctx, n_chunks=n_chunks),
        out_shape=_sds((SSM_GROUPS, r, SSM_CW), F32),
        grid=(SSM_GROUPS,),
        in_specs=[pl.BlockSpec((1, r, SSM_CW), lambda g: (g, 0, 0)), mat, mat, mat,
                  pl.BlockSpec((1, 2, LANES), lambda g: (g, 0, 0))],
        out_specs=pl.BlockSpec((1, r, SSM_CW), lambda g: (g, 0, 0)),
        scratch_shapes=[pltpu.VMEM((r, SSM_CW), F32), pltpu.VMEM((r, SSM_CW), F32)],
        compiler_params=_cparams("parallel"),
        name="ssm_scan",
    )(xg, m_in, m_intra, m_state, a_l)
    ya = yg.reshape(SSM_GROUPS, n_chunks, b, SSM_CHUNK, SSM_GROUP).transpose(2, 1, 3, 0, 4).reshape(b, n + nc, SSM_WIDTH)
    return ya[:, nc:], ya[:, :nc]


def _post_kernel(*refs, alpha, with_ssm):
    if with_ssm:
        (att_ref, ys_ref, u_ref, dsk_ref, gw_ref, gb_ref, wo_ref, x_ref, g1_ref, lg_ref, lb_ref, sc2_ref, sh2_ref,
         x1_ref, h2_ref) = refs
        y = ys_ref[0] + u_ref[0] * dsk_ref[...]
        gl = jax.nn.gelu(y)
        gate = jax.nn.sigmoid(jnp.dot(gl.astype(BF16), gw_ref[...], preferred_element_type=F32) + gb_ref[...])
        ssm = (gl * gate).astype(BF16)
        mix = (jnp.dot(att_ref[0], wo_ref[0:SWA_Q_W, :], preferred_element_type=F32)
               + jnp.dot(ssm, wo_ref[SWA_Q_W:, :], preferred_element_type=F32))
    else:
        att_ref, wo_ref, x_ref, g1_ref, lg_ref, lb_ref, sc2_ref, sh2_ref, x1_ref, h2_ref = refs
        mix = jnp.dot(att_ref[0], wo_ref[...], preferred_element_type=F32)
    x1 = _layer_norm(alpha * x_ref[0] + g1_ref[0] * mix, lg_ref[...], lb_ref[...])
    x1_ref[0] = x1
    h2_ref[0] = x1 * (1.0 + sc2_ref[0]) + sh2_ref[0]


def _post(att, ssm_args, w_out_bf16, x, g1, ln_g, ln_b, sc2, sh2, alpha, tm):
    b, n, d = x.shape
    tok = lambda w: pl.BlockSpec((1, tm, w), lambda i, j: (i, j, 0))
    vec = pl.BlockSpec((1, 1, d), lambda i, j: (i, 0, 0))
    const = lambda a: pl.BlockSpec(a.shape, lambda i, j: (0,) * a.ndim)
    in_specs = [tok(att.shape[-1])]
    args = [att]
    if ssm_args is not None:
        ys, u, dsk, gw, gb = ssm_args
        in_specs += [tok(SSM_WIDTH), tok(SSM_WIDTH), const(dsk), const(gw), const(gb)]
        args += [ys, u, dsk, gw, gb]
    in_specs += [const(w_out_bf16), tok(d), vec, const(ln_g), const(ln_b), vec, vec]
    args += [w_out_bf16, x, g1, ln_g, ln_b, sc2, sh2]
    return pl.pallas_call(
        functools.partial(_post_kernel, alpha=alpha, with_ssm=ssm_args is not None),
        out_shape=(_sds((b, n, d), F32), _sds((b, n, d), F32)),
        grid=(b, n // tm),
        in_specs=in_specs,
        out_specs=(tok(d), tok(d)),
        compiler_params=_cparams("parallel", "parallel"),
        name="post_mixer_ssm" if ssm_args is not None else "post_mixer",
    )(*args)


def _first_max(v, sub):
    m = jnp.max(v, 0, keepdims=True)
    idx = jnp.min(jnp.where(v == m, sub, float(SUBLANES)), 0, keepdims=True)
    return m, idx


def _router_kernel(h_ref, w_ref, b_ref, ids_ref, wts_ref, rank_ref, cnt_ref, carry_scr):
    step = pl.program_id(0)

    @pl.when(step == 0)
    def _():
        carry_scr[...] = jnp.zeros_like(carry_scr)

    logits = lax.dot_general(w_ref[...], h_ref[...], (((1,), (1,)), ((), ())), preferred_element_type=F32,
                             precision=HIGHEST) + b_ref[...]
    tm = logits.shape[1]
    sub = lax.broadcasted_iota(jnp.int32, (SUBLANES, tm), 0).astype(F32)
    gl = logits[0:SUBLANES]
    gmax, gi = _first_max(gl, sub)
    gp = 1.0 / jnp.sum(jnp.exp(gl - gmax), 0, keepdims=True)
    le = logits[ROUTER_EXPERT_ROW0:ROUTER_EXPERT_ROW0 + MOE_EPG]
    for g in range(1, MOE_GROUPS):
        le = jnp.where(gi == float(g), logits[ROUTER_EXPERT_ROW0 + g * MOE_EPG:ROUTER_EXPERT_ROW0 + (g + 1) * MOE_EPG], le)
    m1, i1 = _first_max(le, sub)
    m2, i2 = _first_max(jnp.where(sub == i1, NEG_INF, le), sub)
    t = jnp.exp(m2 - m1)
    e1 = gi * float(MOE_EPG) + i1
    e2 = gi * float(MOE_EPG) + i2
    ids_ref[0:1, :] = e1.astype(jnp.int32)
    ids_ref[1:2, :] = e2.astype(jnp.int32)
    wts_ref[0:1, :] = gp / (1.0 + t)
    wts_ref[1:2, :] = gp * t / (1.0 + t)
    esub = lax.broadcasted_iota(jnp.int32, (MOE_EXPERTS, tm), 0).astype(F32)
    oh1 = (esub == e1).astype(F32)
    oh2 = (esub == e2).astype(F32)
    both = oh1 + oh2
    earlier = (lax.broadcasted_iota(jnp.int32, (tm, tm), 0) < lax.broadcasted_iota(jnp.int32, (tm, tm), 1))
    prefix = jnp.dot(both.astype(BF16), earlier.astype(BF16), preferred_element_type=F32) + carry_scr[...]
    rank_ref[0:1, :] = jnp.sum(oh1 * prefix, 0, keepdims=True).astype(jnp.int32)
    rank_ref[1:2, :] = jnp.sum(oh2 * prefix, 0, keepdims=True).astype(jnp.int32)
    carry_scr[...] += jnp.sum(both, 1, keepdims=True)
    cnt_ref[...] = jnp.broadcast_to(carry_scr[...], cnt_ref.shape)


def _router(tok, wg, bg, we, be, tm):
    t, d = tok.shape
    rows = ROUTER_EXPERT_ROW0 + MOE_EXPERTS
    w = jnp.zeros((rows, d), F32)
    w = w.at[:MOE_GROUPS].set(wg.T)
    w = w.at[ROUTER_EXPERT_ROW0:].set(jnp.transpose(we, (0, 2, 1)).reshape(MOE_EXPERTS, d))
    bias = jnp.full((rows, 1), NEG_INF, F32)
    bias = bias.at[:MOE_GROUPS, 0].set(bg)
    bias = bias.at[ROUTER_EXPERT_ROW0:, 0].set(be.reshape(-1))
    pair = pl.BlockSpec((MOE_TOPK, tm), lambda i: (0, i))
    ids, wts, rank, cnt = pl.pallas_call(
        _router_kernel,
        out_shape=(_sds((MOE_TOPK, t), jnp.int32), _sds((MOE_TOPK, t), F32), _sds((MOE_TOPK, t), jnp.int32),
                   _sds((MOE_EXPERTS, LANES), F32)),
        grid=(t // tm,),
        in_specs=[pl.BlockSpec((tm, d), lambda i: (i, 0)), pl.BlockSpec((rows, d), lambda i: (0, 0)),
                  pl.BlockSpec((rows, 1), lambda i: (0, 0))],
        out_specs=(pair, pair, pair, pl.BlockSpec((MOE_EXPERTS, LANES), lambda i: (0, 0))),
        scratch_shapes=[pltpu.VMEM((MOE_EXPERTS, 1), F32)],
        compiler_params=_cparams("arbitrary"),
        name="moe_router",
    )(tok, w, bias)
    return ids, wts, rank, cnt[:, 0].astype(jnp.int32)


def _moe_plan(ids, rank, counts, n_tok):
    tm = MOE_ROW_TILE
    padded = ((counts + tm - 1) // tm) * tm
    ends = jnp.cumsum(padded)
    offs = ends - padded
    dest = (offs[ids] + rank).astype(jnp.int32)
    n_tiles = (MOE_TOPK * n_tok + MOE_EXPERTS * (tm - 1)) // tm
    starts = jnp.arange(n_tiles, dtype=jnp.int32) * tm
    tile_expert = jnp.minimum(jnp.searchsorted(ends, starts, side='right'), MOE_EXPERTS - 1).astype(jnp.int32)
    n_valid = (ends[-1] // tm).astype(jnp.int32).reshape(1)
    return dest, tile_expert, n_valid, n_tiles * tm


def _dispatch_kernel(dest_ref, tok_ref, xs_init_ref, xs_ref, sem, *, tb):
    del xs_init_ref
    base = pl.program_id(0) * tb

    def body(r, carry):
        for k in range(MOE_TOPK):
            pltpu.make_async_copy(tok_ref.at[pl.ds(base + r, 1)], xs_ref.at[pl.ds(dest_ref[0, k, r], 1)], sem).start()
        return carry

    lax.fori_loop(0, tb, body, 0)
    pltpu.make_async_copy(tok_ref.at[pl.ds(0, MOE_TOPK * tb)], xs_ref.at[pl.ds(0, MOE_TOPK * tb)], sem).wait()


def _dispatch(tok, dest3, n_rows):
    t, d = tok.shape
    nblk, _, tb = dest3.shape
    return pl.pallas_call(
        functools.partial(_dispatch_kernel, tb=tb),
        out_shape=_sds((n_rows, d), tok.dtype),
        grid=(nblk,),
        in_specs=[pl.BlockSpec((1, MOE_TOPK, tb), lambda i: (i, 0, 0), memory_space=pltpu.SMEM),
                  pl.BlockSpec(memory_space=pl.ANY), pl.BlockSpec(memory_space=pl.ANY)],
        out_specs=pl.BlockSpec(memory_space=pl.ANY),
        scratch_shapes=[pltpu.SemaphoreType.DMA(())],
        input_output_aliases={2: 0},
        compiler_params=_cparams("arbitrary"),
        name="moe_dispatch",
    )(dest3, tok, jnp.zeros((n_rows, d), tok.dtype))


def _ffn_kernel(te_ref, nv_ref, x_ref, w1_ref, w3_ref, w2_ref, y_ref, w13_scr, w2_scr):
    i = pl.program_id(0)
    f = w1_ref.shape[2]

    @pl.when((i == 0) | (te_ref[i] != te_ref[jnp.maximum(i - 1, 0)]))
    def _():
        w13_scr[:, 0:f] = w1_ref[0].astype(BF16)
        w13_scr[:, f:2 * f] = w3_ref[0].astype(BF16)
        w2_scr[...] = w2_ref[0].astype(BF16)

    @pl.when(i < nv_ref[0])
    def _():
        h13 = jnp.dot(x_ref[...].astype(BF16), w13_scr[...], preferred_element_type=F32)
        h1 = h13[:, 0:f]
        hh = (h1 * jax.nn.sigmoid(h1) * h13[:, f:2 * f]).astype(BF16)
        y_ref[...] = jnp.dot(hh, w2_scr[...], preferred_element_type=F32)

    @pl.when(i >= nv_ref[0])
    def _():
        y_ref[...] = jnp.zeros_like(y_ref)


def _expert_ffn(tile_expert, n_valid, xs, w1, w3, w2):
    p, d = xs.shape
    f = w1.shape[2]
    tm = MOE_ROW_TILE
    return pl.pallas_call(
        _ffn_kernel,
        out_shape=_sds((p, d), F32),
        grid_spec=pltpu.PrefetchScalarGridSpec(
            num_scalar_prefetch=2,
            grid=(p // tm,),
            in_specs=[pl.BlockSpec((tm, d), lambda i, te, nv: (i, 0)),
                      pl.BlockSpec((1, d, f), lambda i, te, nv: (te[i], 0, 0)),
                      pl.BlockSpec((1, d, f), lambda i, te, nv: (te[i], 0, 0)),
                      pl.BlockSpec((1, f, d), lambda i, te, nv: (te[i], 0, 0))],
            out_specs=pl.BlockSpec((tm, d), lambda i, te, nv: (i, 0)),
            scratch_shapes=[pltpu.VMEM((d, 2 * f), BF16), pltpu.VMEM((f, d), BF16)]),
        compiler_params=_cparams("arbitrary"),
        name="moe_expert_ffn",
    )(tile_expert, n_valid, xs, w1, w3, w2)


def _combine_ln2_kernel(dest_ref, x1_ref, ys_ref, wt_ref, g2_ref, lg_ref, lb_ref, o_ref, buf, sem, *, alpha, tb):
    def body(r, carry):
        for k in range(MOE_TOPK):
            pltpu.make_async_copy(ys_ref.at[pl.ds(dest_ref[0, k, r], 1)], buf.at[k, pl.ds(r, 1)], sem.at[k]).start()
        return carry

    lax.fori_loop(0, tb, body, 0)
    for k in range(MOE_TOPK):
        pltpu.make_async_copy(ys_ref.at[pl.ds(0, tb)], buf.at[k], sem.at[k]).wait()
    f = wt_ref[:, 0:1] * buf[0] + wt_ref[:, 1:2] * buf[1]
    o_ref[...] = _layer_norm(alpha * x1_ref[...] + g2_ref[0] * f, lg_ref[...], lb_ref[...])


def _combine_ln2(dest3, x1, ys, wt, g2, ln_g, ln_b, alpha, n_per_sample):
    t, d = x1.shape
    nblk, _, tb = dest3.shape
    const = pl.BlockSpec((1, d), lambda i: (0, 0))
    return pl.pallas_call(
        functools.partial(_combine_ln2_kernel, alpha=alpha, tb=tb),
        out_shape=_sds((t, d), F32),
        grid=(nblk,),
        in_specs=[pl.BlockSpec((1, MOE_TOPK, tb), lambda i: (i, 0, 0), memory_space=pltpu.SMEM),
                  pl.BlockSpec((tb, d), lambda i: (i, 0)),
                  pl.BlockSpec(memory_space=pl.ANY),
                  pl.BlockSpec((tb, MOE_TOPK), lambda i: (i, 0)),
                  pl.BlockSpec((1, 1, d), lambda i: ((i * tb) // n_per_sample, 0, 0)),
                  const, const],
        out_specs=pl.BlockSpec((tb, d), lambda i: (i, 0)),
        scratch_shapes=[pltpu.VMEM((MOE_TOPK, tb, d), F32), pltpu.SemaphoreType.DMA((MOE_TOPK,))],
        compiler_params=_cparams("arbitrary"),
        name="moe_combine_ln2",
    )(dest3, x1, ys, wt, g2, ln_g, ln_b)


def _proj1_kernel(*refs, rope, with_q):
    x_ref, sc_ref, sh_ref, w_ref = refs[:4]
    refs = refs[4:]
    tabs = None
    if rope:
        tabs = (refs[0][...], refs[1][...], refs[2][...])
        refs = refs[3:]
    h = (x_ref[0] * (1.0 + sc_ref[0]) + sh_ref[0]).astype(BF16)
    r = jnp.dot(h, w_ref[...], preferred_element_type=F32)
    off = 0
    if with_q:
        q_ref, k_ref, v_ref = refs
        for hd in range(DIF_HEADS):
            q_ref[0, hd] = (_rot(r[:, hd * LANES:(hd + 1) * LANES], tabs) * HEAD_DIM ** -0.5).astype(BF16)
        off = DIF_QK_W
    else:
        k_ref, v_ref = refs
    for hd in range(DIF_HEADS):
        k_ref[0, hd] = _rot(r[:, off + hd * LANES:off + (hd + 1) * LANES], tabs).astype(BF16)
        v_ref[0, hd] = r[:, off + DIF_QK_W + hd * LANES:off + DIF_QK_W + (hd + 1) * LANES].astype(BF16)


def _proj1(x, sc, sh, w_bf16, tabs, with_q, tm):
    b, n, d = x.shape
    rope = tabs is not None
    vec = pl.BlockSpec((1, 1, d), lambda i, j: (i, 0, 0))
    in_specs = [pl.BlockSpec((1, tm, d), lambda i, j: (i, j, 0)), vec, vec,
                pl.BlockSpec(w_bf16.shape, lambda i, j: (0, 0))]
    args = [x, sc, sh, w_bf16]
    if rope:
        in_specs += [pl.BlockSpec((tm, LANES), lambda i, j: (j, 0))] * 3
        args += list(tabs)
    hm = pl.BlockSpec((1, DIF_HEADS, tm, LANES), lambda i, j: (i, 0, j, 0))
    n_out = 3 if with_q else 2
    return pl.pallas_call(
        functools.partial(_proj1_kernel, rope=rope, with_q=with_q),
        out_shape=(_sds((b, DIF_HEADS, n, LANES), BF16),) * n_out,
        grid=(b, n // tm),
        in_specs=in_specs,
        out_specs=(hm,) * n_out,
        compiler_params=_cparams("parallel", "parallel"),
        name="proj1_qkv" if with_q else "proj1_kv_ctx",
    )(*args)


def _diff_kernel(lam_ref, q_ref, kl_ref, kc_ref, vl_ref, vc_ref, g_ref, o_ref, *, tq, out_scale):
    q = q_ref[0, 0].astype(F32)
    lo = lax.broadcasted_iota(jnp.int32, q.shape, 1) < HALF
    q2 = jnp.concatenate([jnp.where(lo, q, 0.0), jnp.where(lo, 0.0, q)], 0).astype(BF16)
    s_l = _nt_dot(q2, kl_ref[0, 0])
    s_c = _nt_dot(q2, kc_ref[0, 0])
    m = jnp.maximum(jnp.max(s_l, -1, keepdims=True), jnp.max(s_c, -1, keepdims=True))
    p_l = jnp.exp(s_l - m)
    p_c = jnp.exp(s_c - m)
    inv = 1.0 / (jnp.sum(p_l, -1, keepdims=True) + jnp.sum(p_c, -1, keepdims=True))
    w0 = inv[:tq]
    w1 = lam_ref[0] * inv[tq:]
    pd_l = (p_l[:tq] * w0 - p_l[tq:] * w1).astype(BF16)
    pd_c = (p_c[:tq] * w0 - p_c[tq:] * w1).astype(BF16)
    o = (jnp.dot(pd_l, vl_ref[0, 0], preferred_element_type=F32)
         + jnp.dot(pd_c, vc_ref[0, 0], preferred_element_type=F32))
    o = o * lax.rsqrt(jnp.mean(o * o, -1, keepdims=True) + RMS_EPS) * g_ref[...]
    o_ref[0] = (o * out_scale).astype(BF16)


def _diff_attention(lam, q, kl, kc, vl, vc, subln_g, lam_init, tq):
    b, nh, n, _ = q.shape
    nc = kc.shape[2]
    kv = lambda m: pl.BlockSpec((1, 1, m, LANES), lambda i, h, j: (i, h, 0, 0))
    return pl.pallas_call(
        functools.partial(_diff_kernel, tq=tq, out_scale=1.0 - lam_init),
        out_shape=_sds((b, n, nh * LANES), BF16),
        grid=(b, nh, n // tq),
        in_specs=[pl.BlockSpec(memory_space=pltpu.SMEM),
                  pl.BlockSpec((1, 1, tq, LANES), lambda i, h, j: (i, h, j, 0)),
                  kv(n), kv(nc), kv(n), kv(nc),
                  pl.BlockSpec((1, LANES), lambda i, h, j: (0, 0))],
        out_specs=pl.BlockSpec((1, tq, LANES), lambda i, h, j: (i, j, h)),
        compiler_params=_cparams("parallel", "parallel", "arbitrary"),
        name="diff_attention",
    )(lam, q, kl, kc, vl, vc, subln_g)


def _moe_block(tok, layer, moe_wg, moe_bg, moe_we, moe_be, moe_w1, moe_w3, moe_w2):
    t = tok.shape[0]
    tb = MOE_DMA_TILE
    ids, wts, rank, counts = _router(tok, moe_wg[layer], moe_bg[layer], moe_we[layer], moe_be[layer], tm=512)
    dest, tile_expert, n_valid, n_rows = _moe_plan(ids, rank, counts, t)
    dest3 = dest.reshape(MOE_TOPK, t // tb, tb).transpose(1, 0, 2)
    xs = _dispatch(tok, dest3, n_rows)
    ys = _expert_ffn(tile_expert, n_valid, xs, moe_w1[layer], moe_w3[layer], moe_w2[layer])
    return ys, dest3, wts.T


def kernel(x, c, ctx, c_ctx, mod_w, mod_b, ln1_g, ln1_b, ln2_g, ln2_b, swa_ssm_w_in, swa_ssm_w_out, swa_sink, ssm_a_re, ssm_a_im, ssm_log_step, ssm_b_re, ssm_b_im, ssm_c_re, ssm_c_im, ssm_d, ssm_glu_w, ssm_glu_b, dif_w_in, dif_w_out, dif_lam_q1, dif_lam_k1, dif_lam_q2, dif_lam_k2, dif_subln_g, moe_wg, moe_bg, moe_we, moe_be, moe_w1, moe_w3, moe_w2):
    bsz, n, d = x.shape
    ctx_len = ctx.shape[1]
    depth = mod_w.shape[0]
    alpha = (2 * depth) ** 0.25
    tabs = _rope_tables(n)

    n_vec = 16
    cvec = jnp.zeros((n_vec, d), F32).at[:bsz].set(c).at[bsz].set(c_ctx)
    mods = _modulation(cvec, mod_w, mod_b)

    xl, xc = x, ctx
    for layer in range(depth):
        need_ctx = layer < depth - 1
        i = layer // 2
        lat = [mods[layer, :bsz, k * d:(k + 1) * d].reshape(bsz, 1, d) for k in range(6)]
        cx = [jnp.broadcast_to(mods[layer, bsz, k * d:(k + 1) * d].reshape(1, 1, d), (bsz, 1, d)) for k in range(6)]
        sh1, sc1, g1, sh2, sc2, g2 = lat
        csh1, csc1, cg1, csh2, csc2, cg2 = cx
        lg1, lb1 = ln1_g[layer].reshape(1, d), ln1_b[layer].reshape(1, d)
        lg2, lb2 = ln2_g[layer].reshape(1, d), ln2_b[layer].reshape(1, d)
        if layer % 2 == 0:
            w_in = swa_ssm_w_in[i].astype(BF16)
            w_out = swa_ssm_w_out[i].astype(BF16)
            q, k, v, u = _proj0(xl, sc1, sh1, w_in, tabs, tm=512)
            qc, kc, vc, uc = _proj0(xc, csc1, csh1, w_in, None, tm=ctx_len)
            sink = swa_sink[i].astype(F32)
            att = _swa_attention(sink, q, k, v, kc, vc)
            mats = _ssm_matrices(ssm_a_re[i], ssm_a_im[i], ssm_log_step[i], ssm_b_re[i], ssm_b_im[i],
                                 ssm_c_re[i], ssm_c_im[i])
            ys, ysc = _ssm_scan(u, uc, mats)
            glu = (ssm_d[i].reshape(1, SSM_WIDTH).astype(F32), ssm_glu_w[i].astype(BF16),
                   ssm_glu_b[i].reshape(1, SSM_WIDTH).astype(F32))
            x1, h2 = _post(att, (ys, u) + glu, w_out, xl, g1, lg1, lb1, sc2, sh2, alpha, tm=512)
            if need_ctx:
                att_c = _ctx_attention(sink, qc, kc, vc)
                xc1, hc2 = _post(att_c, (ysc, uc) + glu, w_out, xc, cg1, lg1, lb1, csc2, csh2, alpha, tm=ctx_len)
        else:
            lam_init = 0.8 - 0.6 * math.exp(-0.3 * layer)
            w_in = dif_w_in[i].astype(BF16)
            w_out = dif_w_out[i].astype(BF16)
            q, k, v = _proj1(xl, sc1, sh1, w_in, tabs, True, tm=512)
            kc, vc = _proj1(xc, csc1, csh1, w_in[:, DIF_QK_W:], None, False, tm=ctx_len)
            lam = (jnp.exp(jnp.sum(dif_lam_q1[i].astype(F32) * dif_lam_k1[i].astype(F32)))
                   - jnp.exp(jnp.sum(dif_lam_q2[i].astype(F32) * dif_lam_k2[i].astype(F32))) + lam_init).reshape(1)
            att = _diff_attention(lam, q, k, kc, v, vc, dif_subln_g[i].reshape(1, DIF_V_HEAD).astype(F32),
                                  lam_init, tq=256)
            x1, h2 = _post(att, None, w_out, xl, g1, lg1, lb1, sc2, sh2, alpha, tm=512)
            if need_ctx:
                raise NotImplementedError("a differential-attention layer followed by another layer")
        if need_ctx:
            tok = jnp.concatenate([h2.reshape(-1, d), hc2.reshape(-1, d)], 0)
        else:
            tok = h2.reshape(-1, d)
        ys_moe, dest3, wt = _moe_block(tok, layer, moe_wg, moe_bg, moe_we, moe_be, moe_w1, moe_w3, moe_w2)
        n_lat_blk = bsz * n // MOE_DMA_TILE
        xl = _combine_ln2(dest3[:n_lat_blk], x1.reshape(-1, d), ys_moe, wt[:bsz * n], g2, lg2, lb2, alpha,
                          n).reshape(bsz, n, d)
        if need_ctx:
            xc = _combine_ln2(dest3[n_lat_blk:], xc1.reshape(-1, d), ys_moe, wt[bsz * n:], cg2, lg2, lb2, alpha,
                              ctx_len).reshape(bsz, ctx_len, d)
    return xl
```

```python
import functools
import math

import jax
import jax.numpy as jnp
from jax import lax
from jax.experimental import pallas as pl
from jax.experimental.pallas import tpu as pltpu

F32 = jnp.float32
BF16 = jnp.bfloat16
HIGHEST = lax.Precision.HIGHEST

D_MODEL = 1024
GRID_W = 64
HEAD_DIM = 64
ROPE_BASE = 10000.0
ROPE_FREQS = HEAD_DIM // 4
LN_EPS = 1e-5
RMS_EPS = 1e-5
NEG_INF = -1e30
LOG2_E = math.log2(math.e)
LANES = 128
HALF = LANES // 2

SWA_HEADS = 8
SWA_KV_HEADS = 2
SWA_WINDOW = 128
SWA_BLOCK = 128
SWA_Q_W = SWA_HEADS * HEAD_DIM
SWA_KV_W = SWA_KV_HEADS * HEAD_DIM

SSM_WIDTH = D_MODEL // 2
SSM_GROUP = 16
SSM_GROUPS = SSM_WIDTH // SSM_GROUP
SSM_STATE = 64
SSM_CHUNK = 16
SSM_CW = SSM_CHUNK * SSM_GROUP

AB_IN_W = SWA_Q_W + 2 * SWA_KV_W + SSM_WIDTH

DIF_HEADS = D_MODEL // (2 * HEAD_DIM)
DIF_QK_W = DIF_HEADS * 2 * HEAD_DIM
DIF_V_HEAD = 2 * HEAD_DIM
DIF_V_W = DIF_HEADS * DIF_V_HEAD

MOE_GROUPS = 4
MOE_EPG = 8
MOE_EXPERTS = MOE_GROUPS * MOE_EPG
MOE_HIDDEN = D_MODEL // 4
MOE_TOPK = 2
SUBLANES = 8
ROUTER_EXPERT_ROW0 = SUBLANES
MOE_ROW_TILE = 512
MOE_DMA_TILE = 256

VMEM_LIMIT = 56 * 1024 * 1024


def _cparams(*sem):
    return pltpu.CompilerParams(dimension_semantics=sem, vmem_limit_bytes=VMEM_LIMIT)


def _sds(shape, dtype):
    return jax.ShapeDtypeStruct(shape, dtype)


def _nt_dot(a, b):
    return lax.dot_general(a, b, (((1,), (1,)), ((), ())), preferred_element_type=F32)


def _layer_norm(r, g, b):
    mu = jnp.mean(r, -1, keepdims=True)
    rc = r - mu
    var = jnp.mean(rc * rc, -1, keepdims=True)
    return rc * lax.rsqrt(var + LN_EPS) * g + b


def _mod_kernel(c_ref, w_ref, b_ref, o_ref):
    cv = c_ref[...]
    s = cv * jax.nn.sigmoid(cv)
    o_ref[0] = jnp.dot(s, w_ref[0], preferred_element_type=F32, precision=HIGHEST) + b_ref[0]


def _modulation(cvec, mod_w, mod_b):
    depth, d, w6 = mod_w.shape
    tn = 1536
    return pl.pallas_call(
        _mod_kernel,
        out_shape=_sds((depth, cvec.shape[0], w6), F32),
        grid=(depth, w6 // tn),
        in_specs=[pl.BlockSpec(cvec.shape, lambda l, j: (0, 0)),
                  pl.BlockSpec((1, d, tn), lambda l, j: (l, 0, j)),
                  pl.BlockSpec((1, 1, tn), lambda l, j: (l, 0, j))],
        out_specs=pl.BlockSpec((1, cvec.shape[0], tn), lambda l, j: (l, 0, j)),
        compiler_params=_cparams("arbitrary", "arbitrary"),
        name="modulation",
    )(cvec, mod_w, mod_b.reshape(depth, 1, w6))


def _rope_tables(n):
    rows = n // GRID_W
    row = jnp.repeat(jnp.arange(rows, dtype=F32), GRID_W)
    col = jnp.tile(jnp.arange(GRID_W, dtype=F32), rows)
    inv = ROPE_BASE ** (-jnp.arange(ROPE_FREQS, dtype=F32) / ROPE_FREQS)
    ang_r = row[:, None] * inv[None, :]
    ang_c = col[:, None] * inv[None, :]
    zeros = jnp.zeros_like(ang_r)
    cos64 = jnp.concatenate([jnp.cos(ang_r), jnp.cos(ang_r), jnp.cos(ang_c), jnp.cos(ang_c)], -1)
    sa64 = jnp.concatenate([-jnp.sin(ang_r), zeros, -jnp.sin(ang_c), zeros], -1)
    sb64 = jnp.concatenate([zeros, jnp.sin(ang_r), zeros, jnp.sin(ang_c)], -1)
    return tuple(jnp.tile(t, (1, LANES // HEAD_DIM)) for t in (cos64, sa64, sb64))


def _rot(t, tabs):
    if tabs is None:
        return t
    cos, sa, sb = tabs
    return t * cos + pltpu.roll(t, LANES - ROPE_FREQS, 1) * sa + pltpu.roll(t, ROPE_FREQS, 1) * sb


def _dup_halves(t):
    lo = lax.broadcasted_iota(jnp.int32, t.shape, 1) < HALF
    ta = jnp.where(lo, t, 0.0)
    tb = t - ta
    return ta + pltpu.roll(ta, HALF, 1), tb + pltpu.roll(tb, HALF, 1)


def _proj0_kernel(*refs, rope):
    if rope:
        x_ref, sc_ref, sh_ref, w_ref, cos_ref, sa_ref, sb_ref, q_ref, k_ref, v_ref, u_ref = refs
        tabs = (cos_ref[...], sa_ref[...], sb_ref[...])
    else:
        x_ref, sc_ref, sh_ref, w_ref, q_ref, k_ref, v_ref, u_ref = refs
        tabs = None
    h = (x_ref[0] * (1.0 + sc_ref[0]) + sh_ref[0]).astype(BF16)
    r = jnp.dot(h, w_ref[...], preferred_element_type=F32)
    scale = HEAD_DIM ** -0.5
    for s in range(SWA_Q_W // LANES):
        q_ref[0, :, s * LANES:(s + 1) * LANES] = (_rot(r[:, s * LANES:(s + 1) * LANES], tabs) * scale).astype(BF16)
    k0, k1 = _dup_halves(_rot(r[:, SWA_Q_W:SWA_Q_W + LANES], tabs))
    v0, v1 = _dup_halves(r[:, SWA_Q_W + LANES:SWA_Q_W + 2 * LANES])
    k_ref[0, 0] = k0.astype(BF16)
    k_ref[0, 1] = k1.astype(BF16)
    v_ref[0, 0] = v0.astype(BF16)
    v_ref[0, 1] = v1.astype(BF16)
    u_ref[0] = r[:, SWA_Q_W + 2 * LANES:]


def _proj0(x, sc, sh, w_bf16, tabs, tm):
    b, n, d = x.shape
    rope = tabs is not None
    vec = pl.BlockSpec((1, 1, d), lambda i, j: (i, 0, 0))
    in_specs = [pl.BlockSpec((1, tm, d), lambda i, j: (i, j, 0)), vec, vec,
                pl.BlockSpec(w_bf16.shape, lambda i, j: (0, 0))]
    args = [x, sc, sh, w_bf16]
    if rope:
        in_specs += [pl.BlockSpec((tm, LANES), lambda i, j: (j, 0))] * 3
        args += list(tabs)
    kv_spec = pl.BlockSpec((1, SWA_KV_HEADS, tm, LANES), lambda i, j: (i, 0, j, 0))
    return pl.pallas_call(
        functools.partial(_proj0_kernel, rope=rope),
        out_shape=(_sds((b, n, SWA_Q_W), BF16), _sds((b, SWA_KV_HEADS, n, LANES), BF16),
                   _sds((b, SWA_KV_HEADS, n, LANES), BF16), _sds((b, n, SSM_WIDTH), F32)),
        grid=(b, n // tm),
        in_specs=in_specs,
        out_specs=(pl.BlockSpec((1, tm, SWA_Q_W), lambda i, j: (i, j, 0)), kv_spec, kv_spec,
                   pl.BlockSpec((1, tm, SSM_WIDTH), lambda i, j: (i, j, 0))),
        compiler_params=_cparams("parallel", "parallel"),
        name="proj0_rope" if rope else "proj0_ctx",
    )(*args)


def _swa_kernel(*refs, tq, local, n_lat):
    if local:
        sink_ref, q_ref, k_ref, v_ref, kc_ref, vc_ref, o_ref = refs
    else:
        sink_ref, q_ref, kc_ref, vc_ref, o_ref = refs
    j = pl.program_id(1)
    rows = 4 * tq
    lo = lax.broadcasted_iota(jnp.int32, (tq, LANES), 1) < HALF
    rown = lax.broadcasted_iota(jnp.int32, (rows, 1), 0)
    if local:
        span = 3 * SWA_BLOCK
        start = pl.multiple_of(jnp.clip((j - 1) * SWA_BLOCK, 0, n_lat - span), SWA_BLOCK)
        rr = lax.broadcasted_iota(jnp.int32, (rows, span), 0)
        cc = lax.broadcasted_iota(jnp.int32, (rows, span), 1)
        qpos = j * tq + (rr & (tq - 1))
        mask = jnp.abs(qpos - (start + cc)) <= SWA_WINDOW
    for h in range(SWA_KV_HEADS):
        qa = q_ref[0, :, (2 * h) * LANES:(2 * h + 1) * LANES].astype(F32)
        qb = q_ref[0, :, (2 * h + 1) * LANES:(2 * h + 2) * LANES].astype(F32)
        q4 = jnp.concatenate([jnp.where(lo, qa, 0.0), jnp.where(lo, 0.0, qa),
                              jnp.where(lo, qb, 0.0), jnp.where(lo, 0.0, qb)], 0).astype(BF16)
        sink = jnp.where(rown < tq, sink_ref[4 * h],
                         jnp.where(rown < 2 * tq, sink_ref[4 * h + 1],
                                   jnp.where(rown < 3 * tq, sink_ref[4 * h + 2], sink_ref[4 * h + 3])))
        s_ctx = _nt_dot(q4, kc_ref[0, h])
        m = jnp.maximum(jnp.max(s_ctx, -1, keepdims=True), sink)
        if local:
            s_loc = jnp.where(mask, _nt_dot(q4, k_ref[0, h, pl.ds(start, span), :]), NEG_INF)
            m = jnp.maximum(m, jnp.max(s_loc, -1, keepdims=True))
        p_ctx = jnp.exp(s_ctx - m)
        den = jnp.sum(p_ctx, -1, keepdims=True) + jnp.exp(sink - m)
        o4 = jnp.dot(p_ctx.astype(BF16), vc_ref[0, h], preferred_element_type=F32)
        if local:
            p_loc = jnp.exp(s_loc - m)
            den = den + jnp.sum(p_loc, -1, keepdims=True)
            o4 = o4 + jnp.dot(p_loc.astype(BF16), v_ref[0, h, pl.ds(start, span), :], preferred_element_type=F32)
        o4 = o4 * (1.0 / den)
        o_ref[0, :, (2 * h) * LANES:(2 * h + 1) * LANES] = jnp.where(lo, o4[0:tq], o4[tq:2 * tq]).astype(BF16)
        o_ref[0, :, (2 * h + 1) * LANES:(2 * h + 2) * LANES] = jnp.where(lo, o4[2 * tq:3 * tq], o4[3 * tq:]).astype(BF16)


def _swa_attention(sink, q, k, v, kc, vc):
    b, n, _ = q.shape
    nc = kc.shape[2]
    tq = SWA_BLOCK
    full = lambda m: pl.BlockSpec((1, SWA_KV_HEADS, m, LANES), lambda i, j: (i, 0, 0, 0))
    return pl.pallas_call(
        functools.partial(_swa_kernel, tq=tq, local=True, n_lat=n),
        out_shape=_sds((b, n, SWA_Q_W), BF16),
        grid=(b, n // tq),
        in_specs=[pl.BlockSpec(memory_space=pltpu.SMEM),
                  pl.BlockSpec((1, tq, SWA_Q_W), lambda i, j: (i, j, 0)),
                  full(n), full(n), full(nc), full(nc)],
        out_specs=pl.BlockSpec((1, tq, SWA_Q_W), lambda i, j: (i, j, 0)),
        compiler_params=_cparams("parallel", "arbitrary"),
        name="swa_attention",
    )(sink, q, k, v, kc, vc)


def _ctx_attention(sink, qc, kc, vc):
    b, nc, _ = qc.shape
    full = pl.BlockSpec((1, SWA_KV_HEADS, nc, LANES), lambda i, j: (i, 0, 0, 0))
    return pl.pallas_call(
        functools.partial(_swa_kernel, tq=nc, local=False, n_lat=0),
        out_shape=_sds((b, nc, SWA_Q_W), BF16),
        grid=(b, 1),
        in_specs=[pl.BlockSpec(memory_space=pltpu.SMEM),
                  pl.BlockSpec((1, nc, SWA_Q_W), lambda i, j: (i, 0, 0)), full, full],
        out_specs=pl.BlockSpec((1, nc, SWA_Q_W), lambda i, j: (i, 0, 0)),
        compiler_params=_cparams("parallel", "arbitrary"),
        name="ctx_attention",
    )(sink, qc, kc, vc)


def _ssm_matrices(a_re, a_im, log_step, b_re, b_im, c_re, c_im):
    L = SSM_CHUNK
    ar = a_re.astype(F32)
    ai = a_im.astype(F32)
    dt = jnp.exp(log_step.astype(F32))[..., None]
    m = jnp.arange(L + 1, dtype=F32)[:, None, None, None]
    mag = jnp.exp(m * (dt * ar)[None])
    pw_re = mag * jnp.cos(m * (dt * ai)[None])
    pw_im = mag * jnp.sin(m * (dt * ai)[None])
    den = ar * ar + ai * ai
    nr = pw_re[1] - 1.0
    coef_re = (nr * ar + pw_im[1] * ai) / den
    coef_im = (pw_im[1] * ar - nr * ai) / den
    br = b_re.astype(F32)
    bi = b_im.astype(F32)
    bb_re = coef_re[..., None] * br - coef_im[..., None] * bi
    bb_im = coef_re[..., None] * bi + coef_im[..., None] * br
    cr = c_re.astype(F32)[None]
    ci = c_im.astype(F32)[None]
    ca_re = cr * pw_re[:, :, :, None, :] - ci * pw_im[:, :, :, None, :]
    ca_im = cr * pw_im[:, :, :, None, :] + ci * pw_re[:, :, :, None, :]
    kern = (jnp.einsum('mdgop,dgpc->dmgoc', ca_re[:L], bb_re, precision=HIGHEST)
            - jnp.einsum('mdgop,dgpc->dmgoc', ca_im[:L], bb_im, precision=HIGHEST))
    ii = jnp.arange(L)[None, :]
    jj = jnp.arange(L)[:, None]
    lag_f = ii - jj
    kf = jnp.where((lag_f >= 0)[:, :, None, None, None], kern[0][jnp.clip(lag_f, 0, L - 1)], 0.0)
    kr = jnp.where((lag_f <= 0)[:, :, None, None, None], kern[1][jnp.clip(-lag_f, 0, L - 1)], 0.0)
    m_intra = jnp.transpose(kf + kr, (2, 0, 4, 1, 3)).reshape(SSM_GROUPS, SSM_CW, SSM_CW)
    pf_re, pf_im = pw_re[L - 1 - jnp.arange(L), 0][..., None], pw_im[L - 1 - jnp.arange(L), 0][..., None]
    pr_re, pr_im = pw_re[jnp.arange(L), 1][..., None], pw_im[jnp.arange(L), 1][..., None]
    to_rows = lambda t: jnp.transpose(t, (1, 0, 3, 2)).reshape(SSM_GROUPS, SSM_CW, SSM_STATE)
    m_in = jnp.concatenate([to_rows(pf_re * bb_re[0] - pf_im * bb_im[0]), to_rows(pr_re * bb_re[1] - pr_im * bb_im[1]),
                            to_rows(pf_re * bb_im[0] + pf_im * bb_re[0]), to_rows(pr_re * bb_im[1] + pr_im * bb_re[1])],
                           -1)
    to_cols = lambda t: jnp.transpose(t, (1, 3, 0, 2)).reshape(SSM_GROUPS, SSM_STATE, SSM_CW)
    fi = 1 + jnp.arange(L)
    ri = L - jnp.arange(L)
    m_state = jnp.concatenate([to_cols(ca_re[fi, 0]), to_cols(ca_re[ri, 1]),
                               -to_cols(ca_im[fi, 0]), -to_cols(ca_im[ri, 1])], 1)
    a_l = jnp.stack([jnp.concatenate([pw_re[L, 0], pw_re[L, 1]], -1),
                     jnp.concatenate([pw_im[L, 0], pw_im[L, 1]], -1)], 1)
    return m_in.astype(BF16), m_intra.astype(BF16), m_state.astype(BF16), a_l


def _ssm_kernel(x_ref, min_ref, mintra_ref, mstate_ref, al_ref, y_ref, v_scr, s_scr, *, nb, nc_ctx, n_chunks):
    xv = x_ref[0]
    v_scr[...] = jnp.dot(xv, min_ref[0], preferred_element_type=F32)
    ar = al_ref[0, 0:1, :]
    ai = al_ref[0, 1:2, :]
    lo = lax.broadcasted_iota(jnp.int32, (nb, LANES), 1) < HALF

    def body(k, carry):
        sre, sim = carry
        kr = jnp.where(k < nc_ctx, nc_ctx - 1 - k, n_chunks - 1 + nc_ctx - k)
        rf = pl.multiple_of(k * nb, nb)
        rr = pl.multiple_of(kr * nb, nb)
        s_scr[pl.ds(rf, nb), 0:HALF] = sre[:, 0:HALF]
        s_scr[pl.ds(rr, nb), HALF:LANES] = sre[:, HALF:LANES]
        s_scr[pl.ds(rf, nb), LANES:LANES + HALF] = sim[:, 0:HALF]
        s_scr[pl.ds(rr, nb), LANES + HALF:2 * LANES] = sim[:, HALF:LANES]
        vre = jnp.where(lo, v_scr[pl.ds(rf, nb), 0:LANES], v_scr[pl.ds(rr, nb), 0:LANES])
        vim = jnp.where(lo, v_scr[pl.ds(rf, nb), LANES:2 * LANES], v_scr[pl.ds(rr, nb), LANES:2 * LANES])
        return ar * sre - ai * sim + vre, ar * sim + ai * sre + vim

    zero = jnp.zeros((nb, LANES), F32)
    lax.fori_loop(0, n_chunks, body, (zero, zero))
    y_ref[0] = (jnp.dot(xv, mintra_ref[0], preferred_element_type=F32)
                + jnp.dot(s_scr[...].astype(BF16), mstate_ref[0], preferred_element_type=F32))


def _ssm_scan(u, uc, mats):
    m_in, m_intra, m_state, a_l = mats
    b, n, _ = u.shape
    nc = uc.shape[1]
    nc_ctx = nc // SSM_CHUNK
    n_chunks = (n + nc) // SSM_CHUNK
    r = n_chunks * b
    ua = jnp.concatenate([uc, u], 1).astype(BF16)
    xg = ua.reshape(b, n_chunks, SSM_CHUNK, SSM_GROUPS, SSM_GROUP).transpose(3, 1, 0, 2, 4).reshape(SSM_GROUPS, r, SSM_CW)
    mat = pl.BlockSpec((1, SSM_CW, SSM_CW), lambda g: (g, 0, 0))
    yg = pl.pallas_call(
        functools.partial(_ssm_kernel, nb=b, nc_ctx=nc_ctx, n_chunks=n_chunks),
        out_shape=_sds((SSM_GROUPS, r, SSM_CW), F32),
        grid=(SSM_GROUPS,),
        in_specs=[pl.BlockSpec((1, r, SSM_CW), lambda g: (g, 0, 0)), mat, mat, mat,
                  pl.BlockSpec((1, 2, LANES), lambda g: (g, 0, 0))],
        out_specs=pl.BlockSpec((1, r, SSM_CW), lambda g: (g, 0, 0)),
        scratch_shapes=[pltpu.VMEM((r, SSM_CW), F32), pltpu.VMEM((r, SSM_CW), F32)],
        compiler_params=_cparams("parallel"),
        name="ssm_scan",
    )(xg, m_in, m_intra, m_state, a_l)
    ya = yg.reshape(SSM_GROUPS, n_chunks, b, SSM_CHUNK, SSM_GROUP).transpose(2, 1, 3, 0, 4).reshape(b, n + nc, SSM_WIDTH)
    return ya[:, nc:], ya[:, :nc]


def _split_rows(ref, val):
    for s in range(val.shape[1] // LANES):
        ref[:, s, :] = val[:, s * LANES:(s + 1) * LANES]


def _merge_rows(ref):
    return jnp.concatenate([ref[:, s, :] for s in range(ref.shape[1])], axis=1)


def _post_kernel(*refs, alpha, with_ssm):
    if with_ssm:
        (att_ref, ys_ref, u_ref, dsk_ref, gw_ref, gb_ref, wo_ref, x_ref, g1_ref, lg_ref, lb_ref, sc2_ref, sh2_ref,
         x1_ref, h2_ref) = refs
        y = ys_ref[0] + u_ref[0] * dsk_ref[...]
        gl = jax.nn.gelu(y)
        gate = jax.nn.sigmoid(jnp.dot(gl.astype(BF16), gw_ref[...], preferred_element_type=F32) + gb_ref[...])
        ssm = (gl * gate).astype(BF16)
        mix = (jnp.dot(att_ref[0], wo_ref[0:SWA_Q_W, :], preferred_element_type=F32)
               + jnp.dot(ssm, wo_ref[SWA_Q_W:, :], preferred_element_type=F32))
    else:
        att_ref, wo_ref, x_ref, g1_ref, lg_ref, lb_ref, sc2_ref, sh2_ref, x1_ref, h2_ref = refs
        mix = jnp.dot(att_ref[0], wo_ref[...], preferred_element_type=F32)
    x1 = _layer_norm(alpha * x_ref[0] + g1_ref[0] * mix, lg_ref[...], lb_ref[...])
    x1_ref[0] = x1
    _split_rows(h2_ref, x1 * (1.0 + sc2_ref[0]) + sh2_ref[0])


def _post(att, ssm_args, w_out_bf16, x, g1, ln_g, ln_b, sc2, sh2, alpha, tm):
    b, n, d = x.shape
    tok = lambda w: pl.BlockSpec((1, tm, w), lambda i, j: (i, j, 0))
    vec = pl.BlockSpec((1, 1, d), lambda i, j: (i, 0, 0))
    const = lambda a: pl.BlockSpec(a.shape, lambda i, j: (0,) * a.ndim)
    in_specs = [tok(att.shape[-1])]
    args = [att]
    if ssm_args is not None:
        ys, u, dsk, gw, gb = ssm_args
        in_specs += [tok(SSM_WIDTH), tok(SSM_WIDTH), const(dsk), const(gw), const(gb)]
        args += [ys, u, dsk, gw, gb]
    in_specs += [const(w_out_bf16), tok(d), vec, const(ln_g), const(ln_b), vec, vec]
    args += [w_out_bf16, x, g1, ln_g, ln_b, sc2, sh2]
    per_b = n // tm
    return pl.pallas_call(
        functools.partial(_post_kernel, alpha=alpha, with_ssm=ssm_args is not None),
        out_shape=(_sds((b, n, d), F32), _sds((b * n, d // LANES, LANES), F32)),
        grid=(b, n // tm),
        in_specs=in_specs,
        out_specs=(tok(d), pl.BlockSpec((tm, d // LANES, LANES), lambda i, j: (i * per_b + j, 0, 0))),
        compiler_params=_cparams("parallel", "parallel"),
        name="post_mixer_ssm" if ssm_args is not None else "post_mixer",
    )(*args)


def _first_max(v, sub):
    m = jnp.max(v, 0, keepdims=True)
    idx = jnp.min(jnp.where(v == m, sub, float(SUBLANES)), 0, keepdims=True)
    return m, idx


def _stream_blocks(toks, tb):
    starts = [0]
    for t in toks:
        starts.append(starts[-1] + t.shape[0] // tb)
    return starts


def _stream_spec(tok, tb, start):
    last = tok.shape[0] // tb - 1
    return pl.BlockSpec((tb,) + tok.shape[1:], lambda i, *_: (jnp.clip(i - start, 0, last), 0, 0))


def _router_kernel(*refs, starts):
    n_streams = len(starts) - 1
    tok_refs = refs[:n_streams]
    w_ref, b_ref, ids_ref, wts_ref, rank_ref, cnt_ref, carry_scr = refs[n_streams:]
    step = pl.program_id(0)

    @pl.when(step == 0)
    def _():
        carry_scr[...] = jnp.zeros_like(carry_scr)

    h = _merge_rows(tok_refs[0])
    for ref, start in zip(tok_refs[1:], starts[1:]):
        h = jnp.where(step >= start, _merge_rows(ref), h)
    logits = lax.dot_general(w_ref[...], h, (((1,), (1,)), ((), ())), preferred_element_type=F32,
                             precision=HIGHEST) + b_ref[...]
    tm = logits.shape[1]
    sub = lax.broadcasted_iota(jnp.int32, (SUBLANES, tm), 0).astype(F32)
    gl = logits[0:SUBLANES]
    gmax, gi = _first_max(gl, sub)
    gp = 1.0 / jnp.sum(jnp.exp(gl - gmax), 0, keepdims=True)
    le = logits[ROUTER_EXPERT_ROW0:ROUTER_EXPERT_ROW0 + MOE_EPG]
    for g in range(1, MOE_GROUPS):
        le = jnp.where(gi == float(g), logits[ROUTER_EXPERT_ROW0 + g * MOE_EPG:ROUTER_EXPERT_ROW0 + (g + 1) * MOE_EPG], le)
    m1, i1 = _first_max(le, sub)
    m2, i2 = _first_max(jnp.where(sub == i1, NEG_INF, le), sub)
    t = jnp.exp(m2 - m1)
    e1 = gi * float(MOE_EPG) + i1
    e2 = gi * float(MOE_EPG) + i2
    ids_ref[0:1, :] = e1.astype(jnp.int32)
    ids_ref[1:2, :] = e2.astype(jnp.int32)
    wts_ref[0:1, :] = gp / (1.0 + t)
    wts_ref[1:2, :] = gp * t / (1.0 + t)
    esub = lax.broadcasted_iota(jnp.int32, (MOE_EXPERTS, tm), 0).astype(F32)
    oh1 = (esub == e1).astype(F32)
    oh2 = (esub == e2).astype(F32)
    both = oh1 + oh2
    earlier = (lax.broadcasted_iota(jnp.int32, (tm, tm), 0) < lax.broadcasted_iota(jnp.int32, (tm, tm), 1))
    prefix = jnp.dot(both.astype(BF16), earlier.astype(BF16), preferred_element_type=F32) + carry_scr[...]
    rank_ref[0:1, :] = jnp.sum(oh1 * prefix, 0, keepdims=True).astype(jnp.int32)
    rank_ref[1:2, :] = jnp.sum(oh2 * prefix, 0, keepdims=True).astype(jnp.int32)
    carry_scr[...] += jnp.sum(both, 1, keepdims=True)
    cnt_ref[...] = jnp.broadcast_to(carry_scr[...], cnt_ref.shape)


def _router(toks, wg, bg, we, be, tm):
    starts = _stream_blocks(toks, tm)
    t = starts[-1] * tm
    d = toks[0].shape[1] * toks[0].shape[2]
    rows = ROUTER_EXPERT_ROW0 + MOE_EXPERTS
    w = jnp.zeros((rows, d), F32)
    w = w.at[:MOE_GROUPS].set(wg.T)
    w = w.at[ROUTER_EXPERT_ROW0:].set(jnp.transpose(we, (0, 2, 1)).reshape(MOE_EXPERTS, d))
    bias = jnp.full((rows, 1), NEG_INF, F32)
    bias = bias.at[:MOE_GROUPS, 0].set(bg)
    bias = bias.at[ROUTER_EXPERT_ROW0:, 0].set(be.reshape(-1))
    pair = pl.BlockSpec((MOE_TOPK, tm), lambda i: (0, i))
    ids, wts, rank, cnt = pl.pallas_call(
        functools.partial(_router_kernel, starts=starts),
        out_shape=(_sds((MOE_TOPK, t), jnp.int32), _sds((MOE_TOPK, t), F32), _sds((MOE_TOPK, t), jnp.int32),
                   _sds((MOE_EXPERTS, LANES), F32)),
        grid=(t // tm,),
        in_specs=[_stream_spec(tok, tm, s0) for tok, s0 in zip(toks, starts)]
        + [pl.BlockSpec((rows, d), lambda i: (0, 0)), pl.BlockSpec((rows, 1), lambda i: (0, 0))],
        out_specs=(pair, pair, pair, pl.BlockSpec((MOE_EXPERTS, LANES), lambda i: (0, 0))),
        scratch_shapes=[pltpu.VMEM((MOE_EXPERTS, 1), F32)],
        compiler_params=_cparams("arbitrary"),
        name="moe_router",
    )(*toks, w, bias)
    return ids, wts, rank, cnt[:, 0].astype(jnp.int32)


def _moe_plan(ids, rank, counts, n_tok):
    tm = MOE_ROW_TILE
    padded = ((counts + tm - 1) // tm) * tm
    ends = jnp.cumsum(padded)
    offs = ends - padded
    experts = jnp.arange(MOE_EXPERTS, dtype=jnp.int32)
    dest = (jnp.sum(jnp.where(ids[..., None] == experts, offs, 0), -1) + rank).astype(jnp.int32)
    n_tiles = (MOE_TOPK * n_tok + MOE_EXPERTS * (tm - 1)) // tm
    starts = jnp.arange(n_tiles, dtype=jnp.int32) * tm
    tile_expert = jnp.minimum(jnp.sum((ends[None, :] <= starts[:, None]).astype(jnp.int32), -1), MOE_EXPERTS - 1)
    n_valid = (ends[-1] // tm).astype(jnp.int32).reshape(1)
    return dest, tile_expert, n_valid, n_tiles * tm


def _dispatch_kernel(*refs, tb, starts):
    n_streams = len(starts) - 1
    dest_ref = refs[0]
    tok_refs = refs[1:1 + n_streams]
    xs_ref, sem = refs[2 + n_streams:]
    step = pl.program_id(0)

    for s, tok_ref in enumerate(tok_refs):
        @pl.when((step >= starts[s]) & (step < starts[s + 1]))
        def _(tok_ref=tok_ref):
            def body(r, carry):
                for k in range(MOE_TOPK):
                    pltpu.make_async_copy(tok_ref.at[r], xs_ref.at[dest_ref[0, k, r]], sem).start()
                return carry

            lax.fori_loop(0, tb, body, 0, unroll=8)

    for k in range(MOE_TOPK):
        pltpu.make_async_copy(tok_refs[0], xs_ref.at[pl.ds(0, tb)], sem).wait()


def _dispatch(toks, dest3, n_rows):
    nblk, _, tb = dest3.shape
    tile = toks[0].shape[1:]
    starts = _stream_blocks(toks, tb)
    n_in = 1 + len(toks)
    return pl.pallas_call(
        functools.partial(_dispatch_kernel, tb=tb, starts=starts),
        out_shape=_sds((n_rows,) + tile, F32),
        grid=(nblk,),
        in_specs=[pl.BlockSpec((1, MOE_TOPK, tb), lambda i: (i, 0, 0), memory_space=pltpu.SMEM)]
        + [_stream_spec(tok, tb, s0) for tok, s0 in zip(toks, starts)] + [pl.BlockSpec(memory_space=pl.ANY)],
        out_specs=pl.BlockSpec(memory_space=pl.ANY),
        scratch_shapes=[pltpu.SemaphoreType.DMA(())],
        input_output_aliases={n_in: 0},
        compiler_params=_cparams("arbitrary"),
        name="moe_dispatch",
    )(dest3, *toks, jnp.zeros((n_rows,) + tile, F32))


def _ffn_kernel(te_ref, nv_ref, x_ref, w1_ref, w3_ref, w2_ref, y_ref, w13_scr, w2_scr):
    i = pl.program_id(0)
    f = w1_ref.shape[2]

    @pl.when((i == 0) | (te_ref[i] != te_ref[jnp.maximum(i - 1, 0)]))
    def _():
        w13_scr[:, 0:f] = w1_ref[0].astype(BF16)
        w13_scr[:, f:2 * f] = w3_ref[0].astype(BF16)
        w2_scr[...] = w2_ref[0].astype(BF16)

    @pl.when(i < nv_ref[0])
    def _():
        h13 = jnp.dot(_merge_rows(x_ref).astype(BF16), w13_scr[...], preferred_element_type=F32)
        h1 = h13[:, 0:f]
        hh = (h1 * jax.nn.sigmoid(h1) * h13[:, f:2 * f]).astype(BF16)
        _split_rows(y_ref, jnp.dot(hh, w2_scr[...], preferred_element_type=F32))

    @pl.when(i >= nv_ref[0])
    def _():
        y_ref[...] = jnp.zeros_like(y_ref)


def _expert_ffn(tile_expert, n_valid, xs, w1, w3, w2):
    p = xs.shape[0]
    tile = xs.shape[1:]
    _, d, f = w1.shape
    tm = MOE_ROW_TILE
    return pl.pallas_call(
        _ffn_kernel,
        out_shape=_sds(xs.shape, F32),
        grid_spec=pltpu.PrefetchScalarGridSpec(
            num_scalar_prefetch=2,
            grid=(p // tm,),
            in_specs=[pl.BlockSpec((tm,) + tile, lambda i, te, nv: (i, 0, 0)),
                      pl.BlockSpec((1, d, f), lambda i, te, nv: (te[i], 0, 0)),
                      pl.BlockSpec((1, d, f), lambda i, te, nv: (te[i], 0, 0)),
                      pl.BlockSpec((1, f, d), lambda i, te, nv: (te[i], 0, 0))],
            out_specs=pl.BlockSpec((tm,) + tile, lambda i, te, nv: (i, 0, 0)),
            scratch_shapes=[pltpu.VMEM((d, 2 * f), BF16), pltpu.VMEM((f, d), BF16)]),
        compiler_params=_cparams("arbitrary"),
        name="moe_expert_ffn",
    )(tile_expert, n_valid, xs, w1, w3, w2)


def _combine_ln2_kernel(dest_ref, x1_ref, ys_ref, wt_ref, g2_ref, lg_ref, lb_ref, o_ref, buf, sem, *, alpha, tb):
    def body(r, carry):
        for k in range(MOE_TOPK):
            pltpu.make_async_copy(ys_ref.at[dest_ref[0, k, r]], buf.at[k, r], sem.at[k]).start()
        return carry

    lax.fori_loop(0, tb, body, 0, unroll=8)
    for k in range(MOE_TOPK):
        pltpu.make_async_copy(ys_ref.at[pl.ds(0, tb)], buf.at[k], sem.at[k]).wait()
    f = wt_ref[:, 0:1] * _merge_rows(buf.at[0]) + wt_ref[:, 1:2] * _merge_rows(buf.at[1])
    o_ref[...] = _layer_norm(alpha * x1_ref[...] + g2_ref[0] * f, lg_ref[...], lb_ref[...])


def _combine_ln2(dest3, x1, ys, wt, g2, ln_g, ln_b, alpha, n_per_sample):
    t, d = x1.shape
    nblk, _, tb = dest3.shape
    const = pl.BlockSpec((1, d), lambda i: (0, 0))
    return pl.pallas_call(
        functools.partial(_combine_ln2_kernel, alpha=alpha, tb=tb),
        out_shape=_sds((t, d), F32),
        grid=(nblk,),
        in_specs=[pl.BlockSpec((1, MOE_TOPK, tb), lambda i: (i, 0, 0), memory_space=pltpu.SMEM),
                  pl.BlockSpec((tb, d), lambda i: (i, 0)),
                  pl.BlockSpec(memory_space=pl.ANY),
                  pl.BlockSpec((tb, MOE_TOPK), lambda i: (i, 0)),
                  pl.BlockSpec((1, 1, d), lambda i: ((i * tb) // n_per_sample, 0, 0)),
                  const, const],
        out_specs=pl.BlockSpec((tb, d), lambda i: (i, 0)),
        scratch_shapes=[pltpu.VMEM((MOE_TOPK, tb) + ys.shape[1:], F32), pltpu.SemaphoreType.DMA((MOE_TOPK,))],
        compiler_params=_cparams("arbitrary"),
        name="moe_combine_ln2",
    )(dest3, x1, ys, wt, g2, ln_g, ln_b)


def _proj1_kernel(*refs, rope, with_q):
    x_ref, sc_ref, sh_ref, w_ref = refs[:4]
    refs = refs[4:]
    tabs = None
    if rope:
        tabs = (refs[0][...], refs[1][...], refs[2][...])
        refs = refs[3:]
    h = (x_ref[0] * (1.0 + sc_ref[0]) + sh_ref[0]).astype(BF16)
    r = jnp.dot(h, w_ref[...], preferred_element_type=F32)
    off = 0
    if with_q:
        q_ref, k_ref, v_ref = refs
        for hd in range(DIF_HEADS):
            q_ref[0, hd] = (_rot(r[:, hd * LANES:(hd + 1) * LANES], tabs) * (LOG2_E * HEAD_DIM ** -0.5)).astype(BF16)
        off = DIF_QK_W
    else:
        k_ref, v_ref = refs
    for hd in range(DIF_HEADS):
        k_ref[0, hd] = _rot(r[:, off + hd * LANES:off + (hd + 1) * LANES], tabs).astype(BF16)
        v_ref[0, hd] = r[:, off + DIF_QK_W + hd * LANES:off + DIF_QK_W + (hd + 1) * LANES].astype(BF16)


def _proj1(x, sc, sh, w_bf16, tabs, with_q, tm):
    b, n, d = x.shape
    rope = tabs is not None
    vec = pl.BlockSpec((1, 1, d), lambda i, j: (i, 0, 0))
    in_specs = [pl.BlockSpec((1, tm, d), lambda i, j: (i, j, 0)), vec, vec,
                pl.BlockSpec(w_bf16.shape, lambda i, j: (0, 0))]
    args = [x, sc, sh, w_bf16]
    if rope:
        in_specs += [pl.BlockSpec((tm, LANES), lambda i, j: (j, 0))] * 3
        args += list(tabs)
    hm = pl.BlockSpec((1, DIF_HEADS, tm, LANES), lambda i, j: (i, 0, j, 0))
    n_out = 3 if with_q else 2
    return pl.pallas_call(
        functools.partial(_proj1_kernel, rope=rope, with_q=with_q),
        out_shape=(_sds((b, DIF_HEADS, n, LANES), BF16),) * n_out,
        grid=(b, n // tm),
        in_specs=in_specs,
        out_specs=(hm,) * n_out,
        compiler_params=_cparams("parallel", "parallel"),
        name="proj1_qkv" if with_q else "proj1_kv_ctx",
    )(*args)


def _diff_kernel(lam_ref, q_ref, kl_ref, kc_ref, vl_ref, vc_ref, g_ref, o_ref, k_scr, v_scr, *, tq, n_lat, out_scale):
    @pl.when(pl.program_id(2) == 0)
    def _():
        k_scr[0:n_lat] = kl_ref[0, 0]
        k_scr[n_lat:] = kc_ref[0, 0]
        v_scr[0:n_lat, 0:LANES] = vl_ref[0, 0]
        v_scr[n_lat:, 0:LANES] = vc_ref[0, 0]
        v_scr[:, LANES:] = jnp.ones((v_scr.shape[0], LANES), BF16)

    q = q_ref[0, 0].astype(F32)
    lo = lax.broadcasted_iota(jnp.int32, q.shape, 1) < HALF
    q2 = jnp.concatenate([jnp.where(lo, q, 0.0), jnp.where(lo, 0.0, q)], 0).astype(BF16)
    s = _nt_dot(q2, k_scr[...])
    p = jnp.exp2(s - jnp.max(s, -1, keepdims=True)).astype(BF16)
    oe = jnp.dot(p, v_scr[...], preferred_element_type=F32)
    o = (oe[:tq, 0:LANES] * (1.0 / oe[:tq, LANES:])
         - lam_ref[0] * (oe[tq:, 0:LANES] * (1.0 / oe[tq:, LANES:])))
    o = o * lax.rsqrt(jnp.mean(o * o, -1, keepdims=True) + RMS_EPS) * g_ref[...]
    o_ref[0] = (o * out_scale).astype(BF16)


def _diff_attention(lam, q, kl, kc, vl, vc, subln_g, lam_init, tq):
    b, nh, n, _ = q.shape
    nc = kc.shape[2]
    kv = lambda m: pl.BlockSpec((1, 1, m, LANES), lambda i, h, j: (i, h, 0, 0))
    return pl.pallas_call(
        functools.partial(_diff_kernel, tq=tq, n_lat=n, out_scale=1.0 - lam_init),
        out_shape=_sds((b, n, nh * LANES), BF16),
        grid=(b, nh, n // tq),
        in_specs=[pl.BlockSpec(memory_space=pltpu.SMEM),
                  pl.BlockSpec((1, 1, tq, LANES), lambda i, h, j: (i, h, j, 0)),
                  kv(n), kv(nc), kv(n), kv(nc),
                  pl.BlockSpec((1, LANES), lambda i, h, j: (0, 0))],
        out_specs=pl.BlockSpec((1, tq, LANES), lambda i, h, j: (i, j, h)),
        scratch_shapes=[pltpu.VMEM((n + nc, LANES), BF16), pltpu.VMEM((n + nc, 2 * LANES), BF16)],
        compiler_params=_cparams("parallel", "parallel", "arbitrary"),
        name="diff_attention",
    )(lam, q, kl, kc, vl, vc, subln_g)


def _moe_block(toks, layer, moe_wg, moe_bg, moe_we, moe_be, moe_w1, moe_w3, moe_w2):
    t = sum(tok.shape[0] for tok in toks)
    tb = MOE_DMA_TILE
    ids, wts, rank, counts = _router(toks, moe_wg[layer], moe_bg[layer], moe_we[layer], moe_be[layer], tm=512)
    dest, tile_expert, n_valid, n_rows = _moe_plan(ids, rank, counts, t)
    dest3 = dest.reshape(MOE_TOPK, t // tb, tb).transpose(1, 0, 2)
    xs = _dispatch(toks, dest3, n_rows)
    ys = _expert_ffn(tile_expert, n_valid, xs, moe_w1[layer], moe_w3[layer], moe_w2[layer])
    return ys, dest3, wts.T


def kernel(x, c, ctx, c_ctx, mod_w, mod_b, ln1_g, ln1_b, ln2_g, ln2_b, swa_ssm_w_in, swa_ssm_w_out, swa_sink, ssm_a_re, ssm_a_im, ssm_log_step, ssm_b_re, ssm_b_im, ssm_c_re, ssm_c_im, ssm_d, ssm_glu_w, ssm_glu_b, dif_w_in, dif_w_out, dif_lam_q1, dif_lam_k1, dif_lam_q2, dif_lam_k2, dif_subln_g, moe_wg, moe_bg, moe_we, moe_be, moe_w1, moe_w3, moe_w2):
    bsz, n, d = x.shape
    ctx_len = ctx.shape[1]
    depth = mod_w.shape[0]
    alpha = (2 * depth) ** 0.25
    tabs = _rope_tables(n)

    n_vec = 16
    cvec = jnp.zeros((n_vec, d), F32).at[:bsz].set(c).at[bsz].set(c_ctx)
    mods = _modulation(cvec, mod_w, mod_b)

    xl, xc = x, ctx
    for layer in range(depth):
        need_ctx = layer < depth - 1
        i = layer // 2
        lat = [mods[layer, :bsz, k * d:(k + 1) * d].reshape(bsz, 1, d) for k in range(6)]
        cx = [jnp.broadcast_to(mods[layer, bsz, k * d:(k + 1) * d].reshape(1, 1, d), (bsz, 1, d)) for k in range(6)]
        sh1, sc1, g1, sh2, sc2, g2 = lat
        csh1, csc1, cg1, csh2, csc2, cg2 = cx
        lg1, lb1 = ln1_g[layer].reshape(1, d), ln1_b[layer].reshape(1, d)
        lg2, lb2 = ln2_g[layer].reshape(1, d), ln2_b[layer].reshape(1, d)
        if layer % 2 == 0:
            w_in = swa_ssm_w_in[i].astype(BF16)
            w_out = swa_ssm_w_out[i].astype(BF16)
            q, k, v, u = _proj0(xl, sc1, sh1, w_in, tabs, tm=512)
            qc, kc, vc, uc = _proj0(xc, csc1, csh1, w_in, None, tm=ctx_len)
            sink = swa_sink[i].astype(F32)
            att = _swa_attention(sink, q, k, v, kc, vc)
            mats = _ssm_matrices(ssm_a_re[i], ssm_a_im[i], ssm_log_step[i], ssm_b_re[i], ssm_b_im[i],
                                 ssm_c_re[i], ssm_c_im[i])
            ys, ysc = _ssm_scan(u, uc, mats)
            glu = (ssm_d[i].reshape(1, SSM_WIDTH).astype(F32), ssm_glu_w[i].astype(BF16),
                   ssm_glu_b[i].reshape(1, SSM_WIDTH).astype(F32))
            x1, tok = _post(att, (ys, u) + glu, w_out, xl, g1, lg1, lb1, sc2, sh2, alpha, 512)
            toks = [tok]
            if need_ctx:
                att_c = _ctx_attention(sink, qc, kc, vc)
                xc1, tok_c = _post(att_c, (ysc, uc) + glu, w_out, xc, cg1, lg1, lb1, csc2, csh2, alpha, ctx_len)
                toks.append(tok_c)
        else:
            lam_init = 0.8 - 0.6 * math.exp(-0.3 * layer)
            w_in = dif_w_in[i].astype(BF16)
            w_out = dif_w_out[i].astype(BF16)
            q, k, v = _proj1(xl, sc1, sh1, w_in, tabs, True, tm=512)
            kc, vc = _proj1(xc, csc1, csh1, w_in[:, DIF_QK_W:], None, False, tm=ctx_len)
            lam = (jnp.exp(jnp.sum(dif_lam_q1[i].astype(F32) * dif_lam_k1[i].astype(F32)))
                   - jnp.exp(jnp.sum(dif_lam_q2[i].astype(F32) * dif_lam_k2[i].astype(F32))) + lam_init).reshape(1)
            att = _diff_attention(lam, q, k, kc, v, vc, dif_subln_g[i].reshape(1, DIF_V_HEAD).astype(F32),
                                  lam_init, tq=256)
            x1, tok = _post(att, None, w_out, xl, g1, lg1, lb1, sc2, sh2, alpha, 512)
            toks = [tok]
            if need_ctx:
                raise NotImplementedError("a differential-attention layer followed by another layer")
        ys_moe, dest3, wt = _moe_block(toks, layer, moe_wg, moe_bg, moe_we, moe_be, moe_w1, moe_w3, moe_w2)
        n_lat_blk = bsz * n // MOE_DMA_TILE
        xl = _combine_ln2(dest3[:n_lat_blk], x1.reshape(-1, d), ys_moe, wt[:bsz * n], g2, lg2, lb2, alpha,
                          n).reshape(bsz, n, d)
        if need_ctx:
            xc = _combine_ln2(dest3[n_lat_blk:], xc1.reshape(-1, d), ys_moe, wt[bsz * n:], cg2, lg2, lb2, alpha,
                              ctx_len).reshape(bsz, ctx_len, d)
    return xl
```

```python
import functools
import math

import jax
import jax.numpy as jnp
from jax import lax
from jax.experimental import pallas as pl
from jax.experimental.pallas import tpu as pltpu

F32 = jnp.float32
BF16 = jnp.bfloat16
HIGHEST = lax.Precision.HIGHEST

D_MODEL = 1024
GRID_W = 64
HEAD_DIM = 64
ROPE_BASE = 10000.0
ROPE_FREQS = HEAD_DIM // 4
LN_EPS = 1e-5
RMS_EPS = 1e-5
NEG_INF = -1e30
LOG2_E = math.log2(math.e)
LANES = 128
HALF = LANES // 2

SWA_HEADS = 8
SWA_KV_HEADS = 2
SWA_WINDOW = 128
SWA_BLOCK = 128
SWA_Q_W = SWA_HEADS * HEAD_DIM
SWA_KV_W = SWA_KV_HEADS * HEAD_DIM

SSM_WIDTH = D_MODEL // 2
SSM_GROUP = 16
SSM_GROUPS = SSM_WIDTH // SSM_GROUP
SSM_STATE = 64
SSM_CHUNK = 16
SSM_CW = SSM_CHUNK * SSM_GROUP
SSM_SLAB_GROUPS = LANES // SSM_GROUP

AB_IN_W = SWA_Q_W + 2 * SWA_KV_W + SSM_WIDTH

DIF_HEADS = D_MODEL // (2 * HEAD_DIM)
DIF_QK_W = DIF_HEADS * 2 * HEAD_DIM
DIF_V_HEAD = 2 * HEAD_DIM
DIF_V_W = DIF_HEADS * DIF_V_HEAD
DIF_CHAIN_ROWS = 128

MOE_GROUPS = 4
MOE_EPG = 8
MOE_EXPERTS = MOE_GROUPS * MOE_EPG
MOE_HIDDEN = D_MODEL // 4
MOE_TOPK = 2
SUBLANES = 8
ROUTER_EXPERT_ROW0 = SUBLANES
MOE_ROW_TILE = 512
MOE_DMA_TILE = 256
TOKEN_TILE_ROWS = D_MODEL // LANES

VMEM_LIMIT = 56 * 1024 * 1024


def _cparams(*sem):
    return pltpu.CompilerParams(dimension_semantics=sem, vmem_limit_bytes=VMEM_LIMIT)


def _sds(shape, dtype):
    return jax.ShapeDtypeStruct(shape, dtype)


def _nt_dot(a, b):
    return lax.dot_general(a, b, (((1,), (1,)), ((), ())), preferred_element_type=F32)


def _layer_norm(r, g, b):
    mu = jnp.mean(r, -1, keepdims=True)
    rc = r - mu
    var = jnp.mean(rc * rc, -1, keepdims=True)
    return rc * lax.rsqrt(var + LN_EPS) * g + b


def _mod_kernel(c_ref, w_ref, b_ref, o_ref):
    cv = c_ref[...]
    s = cv * jax.nn.sigmoid(cv)
    o_ref[0] = jnp.dot(s, w_ref[0], preferred_element_type=F32, precision=HIGHEST) + b_ref[0]


def _modulation(cvec, mod_w, mod_b):
    depth, d, w6 = mod_w.shape
    tn = 1536
    return pl.pallas_call(
        _mod_kernel,
        out_shape=_sds((depth, cvec.shape[0], w6), F32),
        grid=(depth, w6 // tn),
        in_specs=[pl.BlockSpec(cvec.shape, lambda l, j: (0, 0)),
                  pl.BlockSpec((1, d, tn), lambda l, j: (l, 0, j)),
                  pl.BlockSpec((1, 1, tn), lambda l, j: (l, 0, j))],
        out_specs=pl.BlockSpec((1, cvec.shape[0], tn), lambda l, j: (l, 0, j)),
        compiler_params=_cparams("arbitrary", "arbitrary"),
        name="modulation",
    )(cvec, mod_w, mod_b.reshape(depth, 1, w6))


def _rope_tables(n):
    rows = n // GRID_W
    row = jnp.repeat(jnp.arange(rows, dtype=F32), GRID_W)
    col = jnp.tile(jnp.arange(GRID_W, dtype=F32), rows)
    inv = ROPE_BASE ** (-jnp.arange(ROPE_FREQS, dtype=F32) / ROPE_FREQS)
    ang_r = row[:, None] * inv[None, :]
    ang_c = col[:, None] * inv[None, :]
    zeros = jnp.zeros_like(ang_r)
    cos64 = jnp.concatenate([jnp.cos(ang_r), jnp.cos(ang_r), jnp.cos(ang_c), jnp.cos(ang_c)], -1)
    sa64 = jnp.concatenate([-jnp.sin(ang_r), zeros, -jnp.sin(ang_c), zeros], -1)
    sb64 = jnp.concatenate([zeros, jnp.sin(ang_r), zeros, jnp.sin(ang_c)], -1)
    return tuple(jnp.tile(t, (1, LANES // HEAD_DIM)) for t in (cos64, sa64, sb64))


def _rot(t, tabs):
    if tabs is None:
        return t
    cos, sa, sb = tabs
    return t * cos + pltpu.roll(t, LANES - ROPE_FREQS, 1) * sa + pltpu.roll(t, ROPE_FREQS, 1) * sb


def _dup_halves(t):
    lo = lax.broadcasted_iota(jnp.int32, t.shape, 1) < HALF
    ta = jnp.where(lo, t, 0.0)
    tb = t - ta
    return ta + pltpu.roll(ta, HALF, 1), tb + pltpu.roll(tb, HALF, 1)


def _proj0_kernel(*refs, rope):
    if rope:
        x_ref, sc_ref, sh_ref, w_ref, cos_ref, sa_ref, sb_ref, q_ref, k_ref, v_ref, u_ref = refs
        tabs = (cos_ref[...], sa_ref[...], sb_ref[...])
    else:
        x_ref, sc_ref, sh_ref, w_ref, q_ref, k_ref, v_ref, u_ref = refs
        tabs = None
    h = (x_ref[0] * (1.0 + sc_ref[0]) + sh_ref[0]).astype(BF16)
    r = jnp.dot(h, w_ref[...], preferred_element_type=F32)
    scale = HEAD_DIM ** -0.5
    for s in range(SWA_Q_W // LANES):
        q_ref[0, :, s * LANES:(s + 1) * LANES] = (_rot(r[:, s * LANES:(s + 1) * LANES], tabs) * scale).astype(BF16)
    k0, k1 = _dup_halves(_rot(r[:, SWA_Q_W:SWA_Q_W + LANES], tabs))
    v0, v1 = _dup_halves(r[:, SWA_Q_W + LANES:SWA_Q_W + 2 * LANES])
    k_ref[0, 0] = k0.astype(BF16)
    k_ref[0, 1] = k1.astype(BF16)
    v_ref[0, 0] = v0.astype(BF16)
    v_ref[0, 1] = v1.astype(BF16)
    u_ref[0] = r[:, SWA_Q_W + 2 * LANES:]


def _proj0(x, sc, sh, w_bf16, tabs, tm):
    b, n, d = x.shape
    rope = tabs is not None
    vec = pl.BlockSpec((1, 1, d), lambda i, j: (i, 0, 0))
    in_specs = [pl.BlockSpec((1, tm, d), lambda i, j: (i, j, 0)), vec, vec,
                pl.BlockSpec(w_bf16.shape, lambda i, j: (0, 0))]
    args = [x, sc, sh, w_bf16]
    if rope:
        in_specs += [pl.BlockSpec((tm, LANES), lambda i, j: (j, 0))] * 3
        args += list(tabs)
    kv_spec = pl.BlockSpec((1, SWA_KV_HEADS, tm, LANES), lambda i, j: (i, 0, j, 0))
    return pl.pallas_call(
        functools.partial(_proj0_kernel, rope=rope),
        out_shape=(_sds((b, n, SWA_Q_W), BF16), _sds((b, SWA_KV_HEADS, n, LANES), BF16),
                   _sds((b, SWA_KV_HEADS, n, LANES), BF16), _sds((b, n, SSM_WIDTH), F32)),
        grid=(b, n // tm),
        in_specs=in_specs,
        out_specs=(pl.BlockSpec((1, tm, SWA_Q_W), lambda i, j: (i, j, 0)), kv_spec, kv_spec,
                   pl.BlockSpec((1, tm, SSM_WIDTH), lambda i, j: (i, j, 0))),
        compiler_params=_cparams("parallel", "parallel"),
        name="proj0_rope" if rope else "proj0_ctx",
    )(*args)


def _swa_kernel(*refs, tq, local, n_lat):
    if local:
        sink_ref, q_ref, k_ref, v_ref, kc_ref, vc_ref, o_ref = refs
    else:
        sink_ref, q_ref, kc_ref, vc_ref, o_ref = refs
    j = pl.program_id(1)
    rows = 4 * tq
    lo = lax.broadcasted_iota(jnp.int32, (tq, LANES), 1) < HALF
    rown = lax.broadcasted_iota(jnp.int32, (rows, 1), 0)
    if local:
        span = 3 * SWA_BLOCK
        start = pl.multiple_of(jnp.clip((j - 1) * SWA_BLOCK, 0, n_lat - span), SWA_BLOCK)
        rr = lax.broadcasted_iota(jnp.int32, (rows, span), 0)
        cc = lax.broadcasted_iota(jnp.int32, (rows, span), 1)
        qpos = j * tq + (rr & (tq - 1))
        mask = jnp.abs(qpos - (start + cc)) <= SWA_WINDOW
    for h in range(SWA_KV_HEADS):
        qa = q_ref[0, :, (2 * h) * LANES:(2 * h + 1) * LANES].astype(F32)
        qb = q_ref[0, :, (2 * h + 1) * LANES:(2 * h + 2) * LANES].astype(F32)
        q4 = jnp.concatenate([jnp.where(lo, qa, 0.0), jnp.where(lo, 0.0, qa),
                              jnp.where(lo, qb, 0.0), jnp.where(lo, 0.0, qb)], 0).astype(BF16)
        sink = jnp.where(rown < tq, sink_ref[4 * h],
                         jnp.where(rown < 2 * tq, sink_ref[4 * h + 1],
                                   jnp.where(rown < 3 * tq, sink_ref[4 * h + 2], sink_ref[4 * h + 3])))
        s_ctx = _nt_dot(q4, kc_ref[0, h])
        m = jnp.maximum(jnp.max(s_ctx, -1, keepdims=True), sink)
        if local:
            s_loc = jnp.where(mask, _nt_dot(q4, k_ref[0, h, pl.ds(start, span), :]), NEG_INF)
            m = jnp.maximum(m, jnp.max(s_loc, -1, keepdims=True))
        p_ctx = jnp.exp(s_ctx - m)
        den = jnp.sum(p_ctx, -1, keepdims=True) + jnp.exp(sink - m)
        o4 = jnp.dot(p_ctx.astype(BF16), vc_ref[0, h], preferred_element_type=F32)
        if local:
            p_loc = jnp.exp(s_loc - m)
            den = den + jnp.sum(p_loc, -1, keepdims=True)
            o4 = o4 + jnp.dot(p_loc.astype(BF16), v_ref[0, h, pl.ds(start, span), :], preferred_element_type=F32)
        o4 = o4 * (1.0 / den)
        o_ref[0, :, (2 * h) * LANES:(2 * h + 1) * LANES] = jnp.where(lo, o4[0:tq], o4[tq:2 * tq]).astype(BF16)
        o_ref[0, :, (2 * h + 1) * LANES:(2 * h + 2) * LANES] = jnp.where(lo, o4[2 * tq:3 * tq], o4[3 * tq:]).astype(BF16)


def _swa_attention(sink, q, k, v, kc, vc):
    b, n, _ = q.shape
    nc = kc.shape[2]
    tq = SWA_BLOCK
    full = lambda m: pl.BlockSpec((1, SWA_KV_HEADS, m, LANES), lambda i, j: (i, 0, 0, 0))
    return pl.pallas_call(
        functools.partial(_swa_kernel, tq=tq, local=True, n_lat=n),
        out_shape=_sds((b, n, SWA_Q_W), BF16),
        grid=(b, n // tq),
        in_specs=[pl.BlockSpec(memory_space=pltpu.SMEM),
                  pl.BlockSpec((1, tq, SWA_Q_W), lambda i, j: (i, j, 0)),
                  full(n), full(n), full(nc), full(nc)],
        out_specs=pl.BlockSpec((1, tq, SWA_Q_W), lambda i, j: (i, j, 0)),
        compiler_params=_cparams("parallel", "arbitrary"),
        name="swa_attention",
    )(sink, q, k, v, kc, vc)


def _ctx_attention(sink, qc, kc, vc):
    b, nc, _ = qc.shape
    full = pl.BlockSpec((1, SWA_KV_HEADS, nc, LANES), lambda i, j: (i, 0, 0, 0))
    return pl.pallas_call(
        functools.partial(_swa_kernel, tq=nc, local=False, n_lat=0),
        out_shape=_sds((b, nc, SWA_Q_W), BF16),
        grid=(b, 1),
        in_specs=[pl.BlockSpec(memory_space=pltpu.SMEM),
                  pl.BlockSpec((1, nc, SWA_Q_W), lambda i, j: (i, 0, 0)), full, full],
        out_specs=pl.BlockSpec((1, nc, SWA_Q_W), lambda i, j: (i, 0, 0)),
        compiler_params=_cparams("parallel", "arbitrary"),
        name="ctx_attention",
    )(sink, qc, kc, vc)


def _ssm_matrices(a_re, a_im, log_step, b_re, b_im, c_re, c_im):
    L = SSM_CHUNK
    ar = a_re.astype(F32)
    ai = a_im.astype(F32)
    dt = jnp.exp(log_step.astype(F32))[..., None]
    m = jnp.arange(L + 1, dtype=F32)[:, None, None, None]
    mag = jnp.exp(m * (dt * ar)[None])
    pw_re = mag * jnp.cos(m * (dt * ai)[None])
    pw_im = mag * jnp.sin(m * (dt * ai)[None])
    den = ar * ar + ai * ai
    nr = pw_re[1] - 1.0
    coef_re = (nr * ar + pw_im[1] * ai) / den
    coef_im = (pw_im[1] * ar - nr * ai) / den
    br = b_re.astype(F32)
    bi = b_im.astype(F32)
    bb_re = coef_re[..., None] * br - coef_im[..., None] * bi
    bb_im = coef_re[..., None] * bi + coef_im[..., None] * br
    cr = c_re.astype(F32)[None]
    ci = c_im.astype(F32)[None]
    ca_re = cr * pw_re[:, :, :, None, :] - ci * pw_im[:, :, :, None, :]
    ca_im = cr * pw_im[:, :, :, None, :] + ci * pw_re[:, :, :, None, :]
    kern = jnp.sum(ca_re[:L, ..., None] * bb_re[None, :, :, None] - ca_im[:L, ..., None] * bb_im[None, :, :, None],
                   axis=4)
    lag = (jnp.arange(L)[None, :] - jnp.arange(L)[:, None])[:, :, None, None, None]
    toep = jnp.zeros((L, L) + kern.shape[2:], F32)
    for m in range(L):
        toep = toep + jnp.where(lag == m, kern[m, 0], 0.0) + jnp.where(lag == -m, kern[m, 1], 0.0)
    m_intra = jnp.transpose(toep, (2, 0, 4, 1, 3)).reshape(SSM_GROUPS, SSM_CW, SSM_CW)
    pf_re, pf_im = pw_re[L - 1 - jnp.arange(L), 0][..., None], pw_im[L - 1 - jnp.arange(L), 0][..., None]
    pr_re, pr_im = pw_re[jnp.arange(L), 1][..., None], pw_im[jnp.arange(L), 1][..., None]
    to_rows = lambda t: jnp.transpose(t, (1, 0, 3, 2)).reshape(SSM_GROUPS, SSM_CW, SSM_STATE)
    m_in = jnp.concatenate([to_rows(pf_re * bb_re[0] - pf_im * bb_im[0]), to_rows(pr_re * bb_re[1] - pr_im * bb_im[1]),
                            to_rows(pf_re * bb_im[0] + pf_im * bb_re[0]), to_rows(pr_re * bb_im[1] + pr_im * bb_re[1])],
                           -1)
    to_cols = lambda t: jnp.transpose(t, (1, 3, 0, 2)).reshape(SSM_GROUPS, SSM_STATE, SSM_CW)
    fi = 1 + jnp.arange(L)
    ri = L - jnp.arange(L)
    m_state = jnp.concatenate([to_cols(ca_re[fi, 0]), to_cols(ca_re[ri, 1]),
                               -to_cols(ca_im[fi, 0]), -to_cols(ca_im[ri, 1])], 1)
    a_l = jnp.stack([jnp.concatenate([pw_re[L, 0], pw_re[L, 1]], -1),
                     jnp.concatenate([pw_im[L, 0], pw_im[L, 1]], -1)], 1)
    return m_in.astype(BF16), m_intra.astype(BF16), m_state.astype(BF16), a_l


def _ssm_pack_kernel(uc_ref, u_ref, x_ref, *, nc_ctx, nc_lat):
    for src, row0, nch in ((uc_ref, 0, nc_ctx), (u_ref, nc_ctx, nc_lat)):
        steps = [src[0, pl.ds(j, nch, stride=SSM_CHUNK), :] for j in range(SSM_CHUNK)]
        lane = lax.broadcasted_iota(jnp.int32, (nch, LANES), 1)
        piece = [(lane >= jj * SSM_GROUP) & (lane < (jj + 1) * SSM_GROUP) for jj in range(SSM_SLAB_GROUPS)]
        for g in range(SSM_SLAB_GROUPS):
            for h in range(SSM_CW // LANES):
                acc = jnp.zeros((nch, LANES), F32)
                for jj in range(SSM_SLAB_GROUPS):
                    z = steps[h * SSM_SLAB_GROUPS + jj]
                    shift = ((jj - g) * SSM_GROUP) % LANES
                    acc = jnp.where(piece[jj], z if shift == 0 else pltpu.roll(z, shift, 1), acc)
                x_ref[g, row0:row0 + nch, h * LANES:(h + 1) * LANES] = acc.astype(BF16)


def _ssm_unpack_kernel(y_ref, oc_ref, o_ref, *, nc_ctx, nc_lat):
    for dst, row0, nch in ((oc_ref, 0, nc_ctx), (o_ref, nc_ctx, nc_lat)):
        lane = lax.broadcasted_iota(jnp.int32, (nch, LANES), 1)
        piece = [(lane >= g * SSM_GROUP) & (lane < (g + 1) * SSM_GROUP) for g in range(SSM_SLAB_GROUPS)]
        for i in range(SSM_CHUNK):
            h, ii = divmod(i, SSM_SLAB_GROUPS)
            acc = jnp.zeros((nch, LANES), F32)
            for g in range(SSM_SLAB_GROUPS):
                z = y_ref[g, row0:row0 + nch, h * LANES:(h + 1) * LANES]
                shift = ((g - ii) * SSM_GROUP) % LANES
                acc = jnp.where(piece[g], z if shift == 0 else pltpu.roll(z, shift, 1), acc)
            dst[0, pl.ds(i, nch, stride=SSM_CHUNK), :] = acc


def _ssm_kernel(x_ref, min_ref, mintra_ref, mstate_ref, al_ref, y_ref, v_scr, s_scr, *, nb, nc_ctx, n_chunks):
    xv = x_ref[0]
    v = jnp.dot(xv, min_ref[0], preferred_element_type=F32)
    v_scr[0] = v[:, 0:LANES]
    v_scr[1] = v[:, LANES:]
    ar = al_ref[0, 0:1, :]
    ai = al_ref[0, 1:2, :]
    lo = lax.broadcasted_iota(jnp.int32, (nb, LANES), 1) < HALF

    def body(k, carry):
        sre, sim = carry
        kr = jnp.where(k < nc_ctx, nc_ctx - 1 - k, n_chunks - 1 + nc_ctx - k)
        rf = pl.ds(k, nb, stride=n_chunks)
        rr = pl.ds(kr, nb, stride=n_chunks)
        s_scr[0, rf, :] = sre
        s_scr[1, rr, :] = sre
        s_scr[2, rf, :] = sim
        s_scr[3, rr, :] = sim
        vre = jnp.where(lo, v_scr[0, rf, :], v_scr[0, rr, :])
        vim = jnp.where(lo, v_scr[1, rf, :], v_scr[1, rr, :])
        return ar * sre - ai * sim + vre, ar * sim + ai * sre + vim

    zero = jnp.zeros((nb, LANES), F32)
    lax.fori_loop(0, n_chunks, body, (zero, zero))
    lo_all = lax.broadcasted_iota(jnp.int32, (s_scr.shape[1], LANES), 1) < HALF
    s_in = jnp.concatenate([jnp.where(lo_all, s_scr[0], s_scr[1]), jnp.where(lo_all, s_scr[2], s_scr[3])],
                           axis=1).astype(BF16)
    y_ref[0] = (jnp.dot(xv, mintra_ref[0], preferred_element_type=F32)
                + jnp.dot(s_in, mstate_ref[0], preferred_element_type=F32))


def _ssm_scan(u, uc, mats):
    m_in, m_intra, m_state, a_l = mats
    b, n, _ = u.shape
    nc = uc.shape[1]
    nc_ctx, nc_lat = nc // SSM_CHUNK, n // SSM_CHUNK
    n_chunks = nc_ctx + nc_lat
    r = n_chunks * b
    nat = lambda m: pl.BlockSpec((1, m, LANES), lambda i, s: (i, 0, s))
    grp = pl.BlockSpec((SSM_SLAB_GROUPS, n_chunks, SSM_CW), lambda i, s: (s, i, 0))
    xg = pl.pallas_call(
        functools.partial(_ssm_pack_kernel, nc_ctx=nc_ctx, nc_lat=nc_lat),
        out_shape=_sds((SSM_GROUPS, r, SSM_CW), BF16),
        grid=(b, SSM_GROUPS // SSM_SLAB_GROUPS),
        in_specs=[nat(nc), nat(n)],
        out_specs=grp,
        compiler_params=_cparams("parallel", "parallel"),
        name="ssm_pack",
    )(uc, u)
    mat = pl.BlockSpec((1, SSM_CW, SSM_CW), lambda g: (g, 0, 0))
    yg = pl.pallas_call(
        functools.partial(_ssm_kernel, nb=b, nc_ctx=nc_ctx, n_chunks=n_chunks),
        out_shape=_sds((SSM_GROUPS, r, SSM_CW), F32),
        grid=(SSM_GROUPS,),
        in_specs=[pl.BlockSpec((1, r, SSM_CW), lambda g: (g, 0, 0)), mat, mat, mat,
                  pl.BlockSpec((1, 2, LANES), lambda g: (g, 0, 0))],
        out_specs=pl.BlockSpec((1, r, SSM_CW), lambda g: (g, 0, 0)),
        scratch_shapes=[pltpu.VMEM((2, r, LANES), F32), pltpu.VMEM((4, r, LANES), F32)],
        compiler_params=_cparams("parallel"),
        name="ssm_scan",
    )(xg, m_in, m_intra, m_state, a_l)
    ysc, ys = pl.pallas_call(
        functools.partial(_ssm_unpack_kernel, nc_ctx=nc_ctx, nc_lat=nc_lat),
        out_shape=(_sds((b, nc, SSM_WIDTH), F32), _sds((b, n, SSM_WIDTH), F32)),
        grid=(b, SSM_GROUPS // SSM_SLAB_GROUPS),
        in_specs=[grp],
        out_specs=(nat(nc), nat(n)),
        compiler_params=_cparams("parallel", "parallel"),
        name="ssm_unpack",
    )(yg)
    return ys, ysc


def _split_rows(ref, val):
    m = val.shape[0]
    for s in range(TOKEN_TILE_ROWS):
        ref[pl.ds(s, m, stride=TOKEN_TILE_ROWS), :] = val[:, s * LANES:(s + 1) * LANES]


def _merge_rows(ref):
    m = ref.shape[0] // TOKEN_TILE_ROWS
    return jnp.concatenate([ref[pl.ds(s, m, stride=TOKEN_TILE_ROWS), :] for s in range(TOKEN_TILE_ROWS)], axis=1)


def _token_tile(ref, t):
    return ref.at[pl.ds(pl.multiple_of(t * TOKEN_TILE_ROWS, TOKEN_TILE_ROWS), TOKEN_TILE_ROWS)]


def _post_kernel(*refs, alpha, with_ssm):
    if with_ssm:
        (att_ref, ys_ref, u_ref, dsk_ref, gw_ref, gb_ref, wo_ref, x_ref, g1_ref, lg_ref, lb_ref, sc2_ref, sh2_ref,
         x1_ref, h2_ref) = refs
        y = ys_ref[0] + u_ref[0] * dsk_ref[...]
        gl = jax.nn.gelu(y)
        gate = jax.nn.sigmoid(jnp.dot(gl.astype(BF16), gw_ref[...], preferred_element_type=F32) + gb_ref[...])
        ssm = (gl * gate).astype(BF16)
        mix = (jnp.dot(att_ref[0], wo_ref[0:SWA_Q_W, :], preferred_element_type=F32)
               + jnp.dot(ssm, wo_ref[SWA_Q_W:, :], preferred_element_type=F32))
    else:
        att_ref, wo_ref, x_ref, g1_ref, lg_ref, lb_ref, sc2_ref, sh2_ref, x1_ref, h2_ref = refs
        mix = jnp.dot(att_ref[0], wo_ref[...], preferred_element_type=F32)
    x1 = _layer_norm(alpha * x_ref[0] + g1_ref[0] * mix, lg_ref[...], lb_ref[...])
    x1_ref[0] = x1
    _split_rows(h2_ref, x1 * (1.0 + sc2_ref[0]) + sh2_ref[0])


def _post(att, ssm_args, w_out_bf16, x, g1, ln_g, ln_b, sc2, sh2, alpha, tm):
    b, n, d = x.shape
    tok = lambda w: pl.BlockSpec((1, tm, w), lambda i, j: (i, j, 0))
    vec = pl.BlockSpec((1, 1, d), lambda i, j: (i, 0, 0))
    const = lambda a: pl.BlockSpec(a.shape, lambda i, j: (0,) * a.ndim)
    in_specs = [tok(att.shape[-1])]
    args = [att]
    if ssm_args is not None:
        ys, u, dsk, gw, gb = ssm_args
        in_specs += [tok(SSM_WIDTH), tok(SSM_WIDTH), const(dsk), const(gw), const(gb)]
        args += [ys, u, dsk, gw, gb]
    in_specs += [const(w_out_bf16), tok(d), vec, const(ln_g), const(ln_b), vec, vec]
    args += [w_out_bf16, x, g1, ln_g, ln_b, sc2, sh2]
    per_b = n // tm
    return pl.pallas_call(
        functools.partial(_post_kernel, alpha=alpha, with_ssm=ssm_args is not None),
        out_shape=(_sds((b, n, d), F32), _sds((b * n * TOKEN_TILE_ROWS, LANES), F32)),
        grid=(b, n // tm),
        in_specs=in_specs,
        out_specs=(tok(d), pl.BlockSpec((tm * TOKEN_TILE_ROWS, LANES), lambda i, j: (i * per_b + j, 0))),
        compiler_params=_cparams("parallel", "parallel"),
        name="post_mixer_ssm" if ssm_args is not None else "post_mixer",
    )(*args)


def _first_max(v, sub):
    m = jnp.max(v, 0, keepdims=True)
    idx = jnp.min(jnp.where(v == m, sub, float(SUBLANES)), 0, keepdims=True)
    return m, idx


def _stream_blocks(toks, tb):
    starts = [0]
    for t in toks:
        starts.append(starts[-1] + t.shape[0] // (tb * TOKEN_TILE_ROWS))
    return starts


def _stream_spec(tok, tb, start):
    last = tok.shape[0] // (tb * TOKEN_TILE_ROWS) - 1
    return pl.BlockSpec((tb * TOKEN_TILE_ROWS, LANES), lambda i, *_: (jnp.clip(i - start, 0, last), 0))


def _router_kernel(*refs, starts):
    n_streams = len(starts) - 1
    tok_refs = refs[:n_streams]
    w_ref, b_ref, ids_ref, wts_ref, rank_ref, cnt_ref, carry_scr = refs[n_streams:]
    step = pl.program_id(0)

    @pl.when(step == 0)
    def _():
        carry_scr[...] = jnp.zeros_like(carry_scr)

    h = _merge_rows(tok_refs[0])
    for ref, start in zip(tok_refs[1:], starts[1:]):
        h = jnp.where(step >= start, _merge_rows(ref), h)
    logits = lax.dot_general(w_ref[...], h, (((1,), (1,)), ((), ())), preferred_element_type=F32,
                             precision=HIGHEST) + b_ref[...]
    tm = logits.shape[1]
    sub = lax.broadcasted_iota(jnp.int32, (SUBLANES, tm), 0).astype(F32)
    gl = logits[0:SUBLANES]
    gmax, gi = _first_max(gl, sub)
    gp = 1.0 / jnp.sum(jnp.exp(gl - gmax), 0, keepdims=True)
    le = logits[ROUTER_EXPERT_ROW0:ROUTER_EXPERT_ROW0 + MOE_EPG]
    for g in range(1, MOE_GROUPS):
        le = jnp.where(gi == float(g), logits[ROUTER_EXPERT_ROW0 + g * MOE_EPG:ROUTER_EXPERT_ROW0 + (g + 1) * MOE_EPG], le)
    m1, i1 = _first_max(le, sub)
    m2, i2 = _first_max(jnp.where(sub == i1, NEG_INF, le), sub)
    t = jnp.exp(m2 - m1)
    e1 = gi * float(MOE_EPG) + i1
    e2 = gi * float(MOE_EPG) + i2
    ids_ref[0:1, :] = e1.astype(jnp.int32)
    ids_ref[1:2, :] = e2.astype(jnp.int32)
    wts_ref[0:1, :] = gp / (1.0 + t)
    wts_ref[1:2, :] = gp * t / (1.0 + t)
    esub = lax.broadcasted_iota(jnp.int32, (MOE_EXPERTS, tm), 0).astype(F32)
    oh1 = (esub == e1).astype(F32)
    oh2 = (esub == e2).astype(F32)
    both = oh1 + oh2
    earlier = (lax.broadcasted_iota(jnp.int32, (tm, tm), 0) < lax.broadcasted_iota(jnp.int32, (tm, tm), 1))
    prefix = jnp.dot(both.astype(BF16), earlier.astype(BF16), preferred_element_type=F32) + carry_scr[...]
    rank_ref[0:1, :] = jnp.sum(oh1 * prefix, 0, keepdims=True).astype(jnp.int32)
    rank_ref[1:2, :] = jnp.sum(oh2 * prefix, 0, keepdims=True).astype(jnp.int32)
    carry_scr[...] += jnp.sum(both, 1, keepdims=True)
    cnt_ref[...] = jnp.broadcast_to(carry_scr[...], cnt_ref.shape)


def _router(toks, wg, bg, we, be, tm):
    starts = _stream_blocks(toks, tm)
    t = starts[-1] * tm
    d = D_MODEL
    rows = ROUTER_EXPERT_ROW0 + MOE_EXPERTS
    w = jnp.zeros((rows, d), F32)
    w = w.at[:MOE_GROUPS].set(wg.T)
    w = w.at[ROUTER_EXPERT_ROW0:].set(jnp.transpose(we, (0, 2, 1)).reshape(MOE_EXPERTS, d))
    bias = jnp.full((rows, 1), NEG_INF, F32)
    bias = bias.at[:MOE_GROUPS, 0].set(bg)
    bias = bias.at[ROUTER_EXPERT_ROW0:, 0].set(be.reshape(-1))
    pair = pl.BlockSpec((MOE_TOPK, tm), lambda i: (0, i))
    ids, wts, rank, cnt = pl.pallas_call(
        functools.partial(_router_kernel, starts=starts),
        out_shape=(_sds((MOE_TOPK, t), jnp.int32), _sds((MOE_TOPK, t), F32), _sds((MOE_TOPK, t), jnp.int32),
                   _sds((MOE_EXPERTS, LANES), F32)),
        grid=(t // tm,),
        in_specs=[_stream_spec(tok, tm, s0) for tok, s0 in zip(toks, starts)]
        + [pl.BlockSpec((rows, d), lambda i: (0, 0)), pl.BlockSpec((rows, 1), lambda i: (0, 0))],
        out_specs=(pair, pair, pair, pl.BlockSpec((MOE_EXPERTS, LANES), lambda i: (0, 0))),
        scratch_shapes=[pltpu.VMEM((MOE_EXPERTS, 1), F32)],
        compiler_params=_cparams("arbitrary"),
        name="moe_router",
    )(*toks, w, bias)
    return ids, wts, rank, cnt[:, 0].astype(jnp.int32)


def _moe_plan(ids, rank, counts, n_tok):
    tm = MOE_ROW_TILE
    padded = ((counts + tm - 1) // tm) * tm
    ends = jnp.cumsum(padded)
    offs = ends - padded
    experts = jnp.arange(MOE_EXPERTS, dtype=jnp.int32)
    dest = (jnp.sum(jnp.where(ids[..., None] == experts, offs, 0), -1) + rank).astype(jnp.int32)
    n_tiles = (MOE_TOPK * n_tok + MOE_EXPERTS * (tm - 1)) // tm
    starts = jnp.arange(n_tiles, dtype=jnp.int32) * tm
    tile_expert = jnp.minimum(jnp.sum((ends[None, :] <= starts[:, None]).astype(jnp.int32), -1), MOE_EXPERTS - 1)
    n_valid = (ends[-1] // tm).astype(jnp.int32).reshape(1)
    return dest, tile_expert, n_valid, n_tiles * tm


def _dispatch_kernel(*refs, tb, starts):
    n_streams = len(starts) - 1
    dest_ref = refs[0]
    tok_refs = refs[1:1 + n_streams]
    xs_ref, sem = refs[2 + n_streams:]
    step = pl.program_id(0)

    for s, tok_ref in enumerate(tok_refs):
        @pl.when((step >= starts[s]) & (step < starts[s + 1]))
        def _(tok_ref=tok_ref):
            def body(r, carry):
                for k in range(MOE_TOPK):
                    pltpu.make_async_copy(_token_tile(tok_ref, r), _token_tile(xs_ref, dest_ref[0, k, r]), sem).start()
                return carry

            lax.fori_loop(0, tb, body, 0, unroll=8)

    for k in range(MOE_TOPK):
        pltpu.make_async_copy(tok_refs[0], xs_ref.at[pl.ds(0, tb * TOKEN_TILE_ROWS)], sem).wait()


def _dispatch(toks, dest3, n_rows):
    nblk, _, tb = dest3.shape
    starts = _stream_blocks(toks, tb)
    n_in = 1 + len(toks)
    xs_shape = (n_rows * TOKEN_TILE_ROWS, LANES)
    return pl.pallas_call(
        functools.partial(_dispatch_kernel, tb=tb, starts=starts),
        out_shape=_sds(xs_shape, F32),
        grid=(nblk,),
        in_specs=[pl.BlockSpec((1, MOE_TOPK, tb), lambda i: (i, 0, 0), memory_space=pltpu.SMEM)]
        + [_stream_spec(tok, tb, s0) for tok, s0 in zip(toks, starts)] + [pl.BlockSpec(memory_space=pl.ANY)],
        out_specs=pl.BlockSpec(memory_space=pl.ANY),
        scratch_shapes=[pltpu.SemaphoreType.DMA(())],
        input_output_aliases={n_in: 0},
        compiler_params=_cparams("arbitrary"),
        name="moe_dispatch",
    )(dest3, *toks, jnp.zeros(xs_shape, F32))


def _ffn_kernel(te_ref, nv_ref, x_ref, w1_ref, w3_ref, w2_ref, y_ref, w13_scr, w2_scr):
    i = pl.program_id(0)
    f = w1_ref.shape[2]

    @pl.when((i == 0) | (te_ref[i] != te_ref[jnp.maximum(i - 1, 0)]))
    def _():
        w13_scr[:, 0:f] = w1_ref[0].astype(BF16)
        w13_scr[:, f:2 * f] = w3_ref[0].astype(BF16)
        w2_scr[...] = w2_ref[0].astype(BF16)

    @pl.when(i < nv_ref[0])
    def _():
        h13 = jnp.dot(_merge_rows(x_ref).astype(BF16), w13_scr[...], preferred_element_type=F32)
        h1 = h13[:, 0:f]
        hh = (h1 * jax.nn.sigmoid(h1) * h13[:, f:2 * f]).astype(BF16)
        _split_rows(y_ref, jnp.dot(hh, w2_scr[...], preferred_element_type=F32))

    @pl.when(i >= nv_ref[0])
    def _():
        y_ref[...] = jnp.zeros_like(y_ref)


def _expert_ffn(tile_expert, n_valid, xs, w1, w3, w2):
    p = xs.shape[0] // TOKEN_TILE_ROWS
    _, d, f = w1.shape
    tm = MOE_ROW_TILE
    rows = pl.BlockSpec((tm * TOKEN_TILE_ROWS, LANES), lambda i, te, nv: (i, 0))
    return pl.pallas_call(
        _ffn_kernel,
        out_shape=_sds(xs.shape, F32),
        grid_spec=pltpu.PrefetchScalarGridSpec(
            num_scalar_prefetch=2,
            grid=(p // tm,),
            in_specs=[rows,
                      pl.BlockSpec((1, d, f), lambda i, te, nv: (te[i], 0, 0)),
                      pl.BlockSpec((1, d, f), lambda i, te, nv: (te[i], 0, 0)),
                      pl.BlockSpec((1, f, d), lambda i, te, nv: (te[i], 0, 0))],
            out_specs=rows,
            scratch_shapes=[pltpu.VMEM((d, 2 * f), BF16), pltpu.VMEM((f, d), BF16)]),
        compiler_params=_cparams("arbitrary"),
        name="moe_expert_ffn",
    )(tile_expert, n_valid, xs, w1, w3, w2)


def _combine_ln2_kernel(dest_ref, x1_ref, ys_ref, wt_ref, g2_ref, lg_ref, lb_ref, o_ref, buf, sem, *, alpha, tb):
    def body(r, carry):
        for k in range(MOE_TOPK):
            pltpu.make_async_copy(_token_tile(ys_ref, dest_ref[0, k, r]), _token_tile(buf.at[k], r), sem.at[k]).start()
        return carry

    lax.fori_loop(0, tb, body, 0, unroll=8)
    for k in range(MOE_TOPK):
        pltpu.make_async_copy(ys_ref.at[pl.ds(0, tb * TOKEN_TILE_ROWS)], buf.at[k], sem.at[k]).wait()
    f = wt_ref[:, 0:1] * _merge_rows(buf.at[0]) + wt_ref[:, 1:2] * _merge_rows(buf.at[1])
    o_ref[...] = _layer_norm(alpha * x1_ref[...] + g2_ref[0] * f, lg_ref[...], lb_ref[...])


def _combine_ln2(dest3, x1, ys, wt, g2, ln_g, ln_b, alpha, n_per_sample):
    t, d = x1.shape
    nblk, _, tb = dest3.shape
    const = pl.BlockSpec((1, d), lambda i: (0, 0))
    return pl.pallas_call(
        functools.partial(_combine_ln2_kernel, alpha=alpha, tb=tb),
        out_shape=_sds((t, d), F32),
        grid=(nblk,),
        in_specs=[pl.BlockSpec((1, MOE_TOPK, tb), lambda i: (i, 0, 0), memory_space=pltpu.SMEM),
                  pl.BlockSpec((tb, d), lambda i: (i, 0)),
                  pl.BlockSpec(memory_space=pl.ANY),
                  pl.BlockSpec((tb, MOE_TOPK), lambda i: (i, 0)),
                  pl.BlockSpec((1, 1, d), lambda i: ((i * tb) // n_per_sample, 0, 0)),
                  const, const],
        out_specs=pl.BlockSpec((tb, d), lambda i: (i, 0)),
        scratch_shapes=[pltpu.VMEM((MOE_TOPK, tb * TOKEN_TILE_ROWS, LANES), F32),
                        pltpu.SemaphoreType.DMA((MOE_TOPK,))],
        compiler_params=_cparams("arbitrary"),
        name="moe_combine_ln2",
    )(dest3, x1, ys, wt, g2, ln_g, ln_b)


def _proj1_kernel(*refs, rope, with_q):
    x_ref, sc_ref, sh_ref, w_ref = refs[:4]
    refs = refs[4:]
    tabs = None
    if rope:
        tabs = (refs[0][...], refs[1][...], refs[2][...])
        refs = refs[3:]
    h = (x_ref[0] * (1.0 + sc_ref[0]) + sh_ref[0]).astype(BF16)
    r = jnp.dot(h, w_ref[...], preferred_element_type=F32)
    off = 0
    if with_q:
        q_ref, k_ref, v_ref = refs
        for hd in range(DIF_HEADS):
            q_ref[0, hd] = (_rot(r[:, hd * LANES:(hd + 1) * LANES], tabs) * (LOG2_E * HEAD_DIM ** -0.5)).astype(BF16)
        off = DIF_QK_W
    else:
        k_ref, v_ref = refs
    for hd in range(DIF_HEADS):
        k_ref[0, hd] = _rot(r[:, off + hd * LANES:off + (hd + 1) * LANES], tabs).astype(BF16)
        v_ref[0, hd] = r[:, off + DIF_QK_W + hd * LANES:off + DIF_QK_W + (hd + 1) * LANES].astype(BF16)


def _proj1(x, sc, sh, w_bf16, tabs, with_q, tm):
    b, n, d = x.shape
    rope = tabs is not None
    vec = pl.BlockSpec((1, 1, d), lambda i, j: (i, 0, 0))
    in_specs = [pl.BlockSpec((1, tm, d), lambda i, j: (i, j, 0)), vec, vec,
                pl.BlockSpec(w_bf16.shape, lambda i, j: (0, 0))]
    args = [x, sc, sh, w_bf16]
    if rope:
        in_specs += [pl.BlockSpec((tm, LANES), lambda i, j: (j, 0))] * 3
        args += list(tabs)
    hm = pl.BlockSpec((1, DIF_HEADS, tm, LANES), lambda i, j: (i, 0, j, 0))
    n_out = 3 if with_q else 2
    return pl.pallas_call(
        functools.partial(_proj1_kernel, rope=rope, with_q=with_q),
        out_shape=(_sds((b, DIF_HEADS, n, LANES), BF16),) * n_out,
        grid=(b, n // tm),
        in_specs=in_specs,
        out_specs=(hm,) * n_out,
        compiler_params=_cparams("parallel", "parallel"),
        name="proj1_qkv" if with_q else "proj1_kv_ctx",
    )(*args)


def _diff_kernel(lam_ref, q_ref, kl_ref, kc_ref, vl_ref, vc_ref, g_ref, o_ref, k_scr, v_scr, *, tq, n_lat, out_scale):
    @pl.when(pl.program_id(2) == 0)
    def _():
        k_scr[0:n_lat] = kl_ref[0, 0]
        k_scr[n_lat:] = kc_ref[0, 0]
        v_scr[0:n_lat, 0:LANES] = vl_ref[0, 0]
        v_scr[n_lat:, 0:LANES] = vc_ref[0, 0]
        v_scr[:, LANES:] = jnp.ones((v_scr.shape[0], LANES), BF16)

    lo = lax.broadcasted_iota(jnp.int32, (DIF_CHAIN_ROWS, LANES), 1) < HALF
    for r0 in range(0, tq, DIF_CHAIN_ROWS):
        q = q_ref[0, 0, r0:r0 + DIF_CHAIN_ROWS, :].astype(F32)
        maps = []
        for qm in (jnp.where(lo, q, 0.0), jnp.where(lo, 0.0, q)):
            s = _nt_dot(qm.astype(BF16), k_scr[...])
            p = jnp.exp2(s - jnp.max(s, -1, keepdims=True)).astype(BF16)
            oe = jnp.dot(p, v_scr[...], preferred_element_type=F32)
            maps.append(oe[:, 0:LANES] * (1.0 / oe[:, LANES:]))
        o = maps[0] - lam_ref[0] * maps[1]
        o = o * lax.rsqrt(jnp.mean(o * o, -1, keepdims=True) + RMS_EPS) * g_ref[...]
        o_ref[0, r0:r0 + DIF_CHAIN_ROWS, :] = (o * out_scale).astype(BF16)


def _diff_attention(lam, q, kl, kc, vl, vc, subln_g, lam_init, tq):
    b, nh, n, _ = q.shape
    nc = kc.shape[2]
    kv = lambda m: pl.BlockSpec((1, 1, m, LANES), lambda i, h, j: (i, h, 0, 0))
    return pl.pallas_call(
        functools.partial(_diff_kernel, tq=tq, n_lat=n, out_scale=1.0 - lam_init),
        out_shape=_sds((b, n, nh * LANES), BF16),
        grid=(b, nh, n // tq),
        in_specs=[pl.BlockSpec(memory_space=pltpu.SMEM),
                  pl.BlockSpec((1, 1, tq, LANES), lambda i, h, j: (i, h, j, 0)),
                  kv(n), kv(nc), kv(n), kv(nc),
                  pl.BlockSpec((1, LANES), lambda i, h, j: (0, 0))],
        out_specs=pl.BlockSpec((1, tq, LANES), lambda i, h, j: (i, j, h)),
        scratch_shapes=[pltpu.VMEM((n + nc, LANES), BF16), pltpu.VMEM((n + nc, 2 * LANES), BF16)],
        compiler_params=_cparams("parallel", "parallel", "arbitrary"),
        name="diff_attention",
    )(lam, q, kl, kc, vl, vc, subln_g)


def _moe_block(toks, layer, moe_wg, moe_bg, moe_we, moe_be, moe_w1, moe_w3, moe_w2):
    t = sum(tok.shape[0] for tok in toks) // TOKEN_TILE_ROWS
    tb = MOE_DMA_TILE
    ids, wts, rank, counts = _router(toks, moe_wg[layer], moe_bg[layer], moe_we[layer], moe_be[layer], tm=512)
    dest, tile_expert, n_valid, n_rows = _moe_plan(ids, rank, counts, t)
    dest3 = dest.reshape(MOE_TOPK, t // tb, tb).transpose(1, 0, 2)
    xs = _dispatch(toks, dest3, n_rows)
    ys = _expert_ffn(tile_expert, n_valid, xs, moe_w1[layer], moe_w3[layer], moe_w2[layer])
    return ys, dest3, wts.T


def kernel(x, c, ctx, c_ctx, mod_w, mod_b, ln1_g, ln1_b, ln2_g, ln2_b, swa_ssm_w_in, swa_ssm_w_out, swa_sink, ssm_a_re, ssm_a_im, ssm_log_step, ssm_b_re, ssm_b_im, ssm_c_re, ssm_c_im, ssm_d, ssm_glu_w, ssm_glu_b, dif_w_in, dif_w_out, dif_lam_q1, dif_lam_k1, dif_lam_q2, dif_lam_k2, dif_subln_g, moe_wg, moe_bg, moe_we, moe_be, moe_w1, moe_w3, moe_w2):
    bsz, n, d = x.shape
    ctx_len = ctx.shape[1]
    depth = mod_w.shape[0]
    alpha = (2 * depth) ** 0.25
    tabs = _rope_tables(n)

    n_vec = 16
    cvec = jnp.zeros((n_vec, d), F32).at[:bsz].set(c).at[bsz].set(c_ctx)
    mods = _modulation(cvec, mod_w, mod_b)

    xl, xc = x, ctx
    for layer in range(depth):
        need_ctx = layer < depth - 1
        i = layer // 2
        lat = [mods[layer, :bsz, k * d:(k + 1) * d].reshape(bsz, 1, d) for k in range(6)]
        cx = [jnp.broadcast_to(mods[layer, bsz, k * d:(k + 1) * d].reshape(1, 1, d), (bsz, 1, d)) for k in range(6)]
        sh1, sc1, g1, sh2, sc2, g2 = lat
        csh1, csc1, cg1, csh2, csc2, cg2 = cx
        lg1, lb1 = ln1_g[layer].reshape(1, d), ln1_b[layer].reshape(1, d)
        lg2, lb2 = ln2_g[layer].reshape(1, d), ln2_b[layer].reshape(1, d)
        if layer % 2 == 0:
            w_in = swa_ssm_w_in[i].astype(BF16)
            w_out = swa_ssm_w_out[i].astype(BF16)
            q, k, v, u = _proj0(xl, sc1, sh1, w_in, tabs, tm=512)
            qc, kc, vc, uc = _proj0(xc, csc1, csh1, w_in, None, tm=ctx_len)
            sink = swa_sink[i].astype(F32)
            att = _swa_attention(sink, q, k, v, kc, vc)
            mats = _ssm_matrices(ssm_a_re[i], ssm_a_im[i], ssm_log_step[i], ssm_b_re[i], ssm_b_im[i],
                                 ssm_c_re[i], ssm_c_im[i])
            ys, ysc = _ssm_scan(u, uc, mats)
            glu = (ssm_d[i].reshape(1, SSM_WIDTH).astype(F32), ssm_glu_w[i].astype(BF16),
                   ssm_glu_b[i].reshape(1, SSM_WIDTH).astype(F32))
            x1, tok = _post(att, (ys, u) + glu, w_out, xl, g1, lg1, lb1, sc2, sh2, alpha, 512)
            toks = [tok]
            if need_ctx:
                att_c = _ctx_attention(sink, qc, kc, vc)
                xc1, tok_c = _post(att_c, (ysc, uc) + glu, w_out, xc, cg1, lg1, lb1, csc2, csh2, alpha, ctx_len)
                toks.append(tok_c)
        else:
            lam_init = 0.8 - 0.6 * math.exp(-0.3 * layer)
            w_in = dif_w_in[i].astype(BF16)
            w_out = dif_w_out[i].astype(BF16)
            q, k, v = _proj1(xl, sc1, sh1, w_in, tabs, True, tm=512)
            kc, vc = _proj1(xc, csc1, csh1, w_in[:, DIF_QK_W:], None, False, tm=ctx_len)
            lam = (jnp.exp(jnp.sum(dif_lam_q1[i].astype(F32) * dif_lam_k1[i].astype(F32)))
                   - jnp.exp(jnp.sum(dif_lam_q2[i].astype(F32) * dif_lam_k2[i].astype(F32))) + lam_init).reshape(1)
            att = _diff_attention(lam, q, k, kc, v, vc, dif_subln_g[i].reshape(1, DIF_V_HEAD).astype(F32),
                                  lam_init, tq=n)
            x1, tok = _post(att, None, w_out, xl, g1, lg1, lb1, sc2, sh2, alpha, 512)
            toks = [tok]
            if need_ctx:
                raise NotImplementedError("a differential-attention layer followed by another layer")
        ys_moe, dest3, wt = _moe_block(toks, layer, moe_wg, moe_bg, moe_we, moe_be, moe_w1, moe_w3, moe_w2)
        n_lat_blk = bsz * n // MOE_DMA_TILE
        xl = _combine_ln2(dest3[:n_lat_blk], x1.reshape(-1, d), ys_moe, wt[:bsz * n], g2, lg2, lb2, alpha,
                          n).reshape(bsz, n, d)
        if need_ctx:
            xc = _combine_ln2(dest3[n_lat_blk:], xc1.reshape(-1, d), ys_moe, wt[bsz * n:], cg2, lg2, lb2, alpha,
                              ctx_len).reshape(bsz, ctx_len, d)
    return xl
```

```python
import functools
import math

import jax
import jax.numpy as jnp
from jax import lax
from jax.experimental import pallas as pl
from jax.experimental.pallas import tpu as pltpu

F32 = jnp.float32
BF16 = jnp.bfloat16
HIGHEST = lax.Precision.HIGHEST

D_MODEL = 1024
GRID_W = 64
HEAD_DIM = 64
ROPE_BASE = 10000.0
ROPE_FREQS = HEAD_DIM // 4
LN_EPS = 1e-5
RMS_EPS = 1e-5
NEG_INF = -1e30
LOG2_E = math.log2(math.e)
LANES = 128
HALF = LANES // 2

SWA_HEADS = 8
SWA_KV_HEADS = 2
SWA_WINDOW = 128
SWA_BLOCK = 128
SWA_Q_W = SWA_HEADS * HEAD_DIM
SWA_KV_W = SWA_KV_HEADS * HEAD_DIM

SSM_WIDTH = D_MODEL // 2
SSM_GROUP = 16
SSM_GROUPS = SSM_WIDTH // SSM_GROUP
SSM_STATE = 64
SSM_CHUNK = 16
SSM_CW = SSM_CHUNK * SSM_GROUP
SSM_SLAB_GROUPS = LANES // SSM_GROUP

AB_IN_W = SWA_Q_W + 2 * SWA_KV_W + SSM_WIDTH

DIF_HEADS = D_MODEL // (2 * HEAD_DIM)
DIF_QK_W = DIF_HEADS * 2 * HEAD_DIM
DIF_V_HEAD = 2 * HEAD_DIM
DIF_V_W = DIF_HEADS * DIF_V_HEAD
DIF_CHAIN_ROWS = 128

MOE_GROUPS = 4
MOE_EPG = 8
MOE_EXPERTS = MOE_GROUPS * MOE_EPG
MOE_HIDDEN = D_MODEL // 4
MOE_TOPK = 2
SUBLANES = 8
ROUTER_EXPERT_ROW0 = SUBLANES
MOE_ROW_TILE = 512
MOE_DMA_TILE = 256
TOKEN_TILE_ROWS = D_MODEL // LANES

VMEM_LIMIT = 56 * 1024 * 1024


def _cparams(*sem):
    return pltpu.CompilerParams(dimension_semantics=sem, vmem_limit_bytes=VMEM_LIMIT)


def _sds(shape, dtype):
    return jax.ShapeDtypeStruct(shape, dtype)


def _nt_dot(a, b):
    return lax.dot_general(a, b, (((1,), (1,)), ((), ())), preferred_element_type=F32)


def _layer_norm(r, g, b):
    mu = jnp.mean(r, -1, keepdims=True)
    rc = r - mu
    var = jnp.mean(rc * rc, -1, keepdims=True)
    return rc * lax.rsqrt(var + LN_EPS) * g + b


def _mod_kernel(c_ref, w_ref, b_ref, o_ref):
    cv = c_ref[...]
    s = cv * jax.nn.sigmoid(cv)
    o_ref[0] = jnp.dot(s, w_ref[0], preferred_element_type=F32, precision=HIGHEST) + b_ref[0]


def _modulation(cvec, mod_w, mod_b):
    depth, d, w6 = mod_w.shape
    tn = 1536
    return pl.pallas_call(
        _mod_kernel,
        out_shape=_sds((depth, cvec.shape[0], w6), F32),
        grid=(depth, w6 // tn),
        in_specs=[pl.BlockSpec(cvec.shape, lambda l, j: (0, 0)),
                  pl.BlockSpec((1, d, tn), lambda l, j: (l, 0, j)),
                  pl.BlockSpec((1, 1, tn), lambda l, j: (l, 0, j))],
        out_specs=pl.BlockSpec((1, cvec.shape[0], tn), lambda l, j: (l, 0, j)),
        compiler_params=_cparams("arbitrary", "arbitrary"),
        name="modulation",
    )(cvec, mod_w, mod_b.reshape(depth, 1, w6))


def _rope_tables(n):
    rows = n // GRID_W
    row = jnp.repeat(jnp.arange(rows, dtype=F32), GRID_W)
    col = jnp.tile(jnp.arange(GRID_W, dtype=F32), rows)
    inv = ROPE_BASE ** (-jnp.arange(ROPE_FREQS, dtype=F32) / ROPE_FREQS)
    ang_r = row[:, None] * inv[None, :]
    ang_c = col[:, None] * inv[None, :]
    zeros = jnp.zeros_like(ang_r)
    cos64 = jnp.concatenate([jnp.cos(ang_r), jnp.cos(ang_r), jnp.cos(ang_c), jnp.cos(ang_c)], -1)
    sa64 = jnp.concatenate([-jnp.sin(ang_r), zeros, -jnp.sin(ang_c), zeros], -1)
    sb64 = jnp.concatenate([zeros, jnp.sin(ang_r), zeros, jnp.sin(ang_c)], -1)
    return tuple(jnp.tile(t, (1, LANES // HEAD_DIM)) for t in (cos64, sa64, sb64))


def _rot(t, tabs):
    if tabs is None:
        return t
    cos, sa, sb = tabs
    return t * cos + pltpu.roll(t, LANES - ROPE_FREQS, 1) * sa + pltpu.roll(t, ROPE_FREQS, 1) * sb


def _dup_halves(t):
    lo = lax.broadcasted_iota(jnp.int32, t.shape, 1) < HALF
    ta = jnp.where(lo, t, 0.0)
    tb = t - ta
    return ta + pltpu.roll(ta, HALF, 1), tb + pltpu.roll(tb, HALF, 1)


def _proj0_kernel(*refs, rope):
    if rope:
        x_ref, sc_ref, sh_ref, w_ref, cos_ref, sa_ref, sb_ref, q_ref, k_ref, v_ref, u_ref = refs
        tabs = (cos_ref[...], sa_ref[...], sb_ref[...])
    else:
        x_ref, sc_ref, sh_ref, w_ref, q_ref, k_ref, v_ref, u_ref = refs
        tabs = None
    h = (x_ref[0] * (1.0 + sc_ref[0]) + sh_ref[0]).astype(BF16)
    r = jnp.dot(h, w_ref[...], preferred_element_type=F32)
    scale = LOG2_E * HEAD_DIM ** -0.5
    for s in range(SWA_Q_W // LANES):
        q_ref[0, :, s * LANES:(s + 1) * LANES] = (_rot(r[:, s * LANES:(s + 1) * LANES], tabs) * scale).astype(BF16)
    k0, k1 = _dup_halves(_rot(r[:, SWA_Q_W:SWA_Q_W + LANES], tabs))
    v0, v1 = _dup_halves(r[:, SWA_Q_W + LANES:SWA_Q_W + 2 * LANES])
    k_ref[0, 0] = k0.astype(BF16)
    k_ref[0, 1] = k1.astype(BF16)
    v_ref[0, 0] = v0.astype(BF16)
    v_ref[0, 1] = v1.astype(BF16)
    u_ref[0] = r[:, SWA_Q_W + 2 * LANES:]


def _proj0(x, sc, sh, w_bf16, tabs, tm):
    b, n, d = x.shape
    rope = tabs is not None
    vec = pl.BlockSpec((1, 1, d), lambda i, j: (i, 0, 0))
    in_specs = [pl.BlockSpec((1, tm, d), lambda i, j: (i, j, 0)), vec, vec,
                pl.BlockSpec(w_bf16.shape, lambda i, j: (0, 0))]
    args = [x, sc, sh, w_bf16]
    if rope:
        in_specs += [pl.BlockSpec((tm, LANES), lambda i, j: (j, 0))] * 3
        args += list(tabs)
    kv_spec = pl.BlockSpec((1, SWA_KV_HEADS, tm, LANES), lambda i, j: (i, 0, j, 0))
    return pl.pallas_call(
        functools.partial(_proj0_kernel, rope=rope),
        out_shape=(_sds((b, n, SWA_Q_W), BF16), _sds((b, SWA_KV_HEADS, n, LANES), BF16),
                   _sds((b, SWA_KV_HEADS, n, LANES), BF16), _sds((b, n, SSM_WIDTH), F32)),
        grid=(b, n // tm),
        in_specs=in_specs,
        out_specs=(pl.BlockSpec((1, tm, SWA_Q_W), lambda i, j: (i, j, 0)), kv_spec, kv_spec,
                   pl.BlockSpec((1, tm, SSM_WIDTH), lambda i, j: (i, j, 0))),
        compiler_params=_cparams("parallel", "parallel"),
        name="proj0_rope" if rope else "proj0_ctx",
    )(*args)


def _swa_kernel(*refs, tq, local, n_lat):
    if local:
        sink_ref, q_ref, k_ref, v_ref, kc_ref, vc_ref, o_ref = refs
    else:
        sink_ref, q_ref, kc_ref, vc_ref, o_ref = refs
    j = pl.program_id(1)
    rows = 4 * tq
    lo = lax.broadcasted_iota(jnp.int32, (tq, LANES), 1) < HALF
    rown = lax.broadcasted_iota(jnp.int32, (rows, 1), 0)
    if local:
        span = 3 * SWA_BLOCK
        start = pl.multiple_of(jnp.clip((j - 1) * SWA_BLOCK, 0, n_lat - span), SWA_BLOCK)
        rr = lax.broadcasted_iota(jnp.int32, (rows, span), 0)
        cc = lax.broadcasted_iota(jnp.int32, (rows, span), 1)
        qpos = j * tq + (rr & (tq - 1))
        mask = jnp.abs(qpos - (start + cc)) <= SWA_WINDOW
    for h in range(SWA_KV_HEADS):
        qa = q_ref[0, :, (2 * h) * LANES:(2 * h + 1) * LANES].astype(F32)
        qb = q_ref[0, :, (2 * h + 1) * LANES:(2 * h + 2) * LANES].astype(F32)
        q4 = jnp.concatenate([jnp.where(lo, qa, 0.0), jnp.where(lo, 0.0, qa),
                              jnp.where(lo, qb, 0.0), jnp.where(lo, 0.0, qb)], 0).astype(BF16)
        sink = LOG2_E * jnp.where(rown < tq, sink_ref[4 * h],
                                  jnp.where(rown < 2 * tq, sink_ref[4 * h + 1],
                                            jnp.where(rown < 3 * tq, sink_ref[4 * h + 2], sink_ref[4 * h + 3])))
        s_ctx = _nt_dot(q4, kc_ref[0, h])
        m = jnp.maximum(jnp.max(s_ctx, -1, keepdims=True), sink)
        if local:
            s_loc = jnp.where(mask, _nt_dot(q4, k_ref[0, h, pl.ds(start, span), :]), NEG_INF)
            m = jnp.maximum(m, jnp.max(s_loc, -1, keepdims=True))
        p_ctx = jnp.exp2(s_ctx - m)
        den = jnp.sum(p_ctx, -1, keepdims=True) + jnp.exp2(sink - m)
        o4 = jnp.dot(p_ctx.astype(BF16), vc_ref[0, h], preferred_element_type=F32)
        if local:
            p_loc = jnp.exp2(s_loc - m)
            den = den + jnp.sum(p_loc, -1, keepdims=True)
            o4 = o4 + jnp.dot(p_loc.astype(BF16), v_ref[0, h, pl.ds(start, span), :], preferred_element_type=F32)
        o4 = o4 * (1.0 / den)
        o_ref[0, :, (2 * h) * LANES:(2 * h + 1) * LANES] = jnp.where(lo, o4[0:tq], o4[tq:2 * tq]).astype(BF16)
        o_ref[0, :, (2 * h + 1) * LANES:(2 * h + 2) * LANES] = jnp.where(lo, o4[2 * tq:3 * tq], o4[3 * tq:]).astype(BF16)


def _swa_attention(sink, q, k, v, kc, vc):
    b, n, _ = q.shape
    nc = kc.shape[2]
    tq = SWA_BLOCK
    full = lambda m: pl.BlockSpec((1, SWA_KV_HEADS, m, LANES), lambda i, j: (i, 0, 0, 0))
    return pl.pallas_call(
        functools.partial(_swa_kernel, tq=tq, local=True, n_lat=n),
        out_shape=_sds((b, n, SWA_Q_W), BF16),
        grid=(b, n // tq),
        in_specs=[pl.BlockSpec(memory_space=pltpu.SMEM),
                  pl.BlockSpec((1, tq, SWA_Q_W), lambda i, j: (i, j, 0)),
                  full(n), full(n), full(nc), full(nc)],
        out_specs=pl.BlockSpec((1, tq, SWA_Q_W), lambda i, j: (i, j, 0)),
        compiler_params=_cparams("parallel", "arbitrary"),
        name="swa_attention",
    )(sink, q, k, v, kc, vc)


def _ctx_attention(sink, qc, kc, vc):
    b, nc, _ = qc.shape
    full = pl.BlockSpec((1, SWA_KV_HEADS, nc, LANES), lambda i, j: (i, 0, 0, 0))
    return pl.pallas_call(
        functools.partial(_swa_kernel, tq=nc, local=False, n_lat=0),
        out_shape=_sds((b, nc, SWA_Q_W), BF16),
        grid=(b, 1),
        in_specs=[pl.BlockSpec(memory_space=pltpu.SMEM),
                  pl.BlockSpec((1, nc, SWA_Q_W), lambda i, j: (i, 0, 0)), full, full],
        out_specs=pl.BlockSpec((1, nc, SWA_Q_W), lambda i, j: (i, 0, 0)),
        compiler_params=_cparams("parallel", "arbitrary"),
        name="ctx_attention",
    )(sink, qc, kc, vc)


def _ssm_matrices(a_re, a_im, log_step, b_re, b_im, c_re, c_im):
    L = SSM_CHUNK
    ar = a_re.astype(F32)
    ai = a_im.astype(F32)
    dt = jnp.exp(log_step.astype(F32))[..., None]
    m = jnp.arange(L + 1, dtype=F32)[:, None, None, None]
    mag = jnp.exp(m * (dt * ar)[None])
    pw_re = mag * jnp.cos(m * (dt * ai)[None])
    pw_im = mag * jnp.sin(m * (dt * ai)[None])
    den = ar * ar + ai * ai
    nr = pw_re[1] - 1.0
    coef_re = (nr * ar + pw_im[1] * ai) / den
    coef_im = (pw_im[1] * ar - nr * ai) / den
    br = b_re.astype(F32)
    bi = b_im.astype(F32)
    bb_re = coef_re[..., None] * br - coef_im[..., None] * bi
    bb_im = coef_re[..., None] * bi + coef_im[..., None] * br
    cr = c_re.astype(F32)[None]
    ci = c_im.astype(F32)[None]
    ca_re = cr * pw_re[:, :, :, None, :] - ci * pw_im[:, :, :, None, :]
    ca_im = cr * pw_im[:, :, :, None, :] + ci * pw_re[:, :, :, None, :]
    kern = jnp.sum(ca_re[:L, ..., None] * bb_re[None, :, :, None] - ca_im[:L, ..., None] * bb_im[None, :, :, None],
                   axis=4)
    G, C = SSM_GROUPS, SSM_GROUP
    k_f = jnp.transpose(kern[:, 0], (1, 3, 0, 2)).reshape(G, C, SSM_CW)
    k_r = jnp.transpose(kern[::-1, 1], (1, 3, 0, 2)).reshape(G, C, SSM_CW)
    rows = []
    for j in range(L):
        z_f = jnp.zeros((G, C, j * C), F32)
        z_r = jnp.zeros((G, C, (L - 1 - j) * C), F32)
        rows.append(jnp.concatenate([z_f, k_f[:, :, :SSM_CW - j * C]], -1)
                    + jnp.concatenate([k_r[:, :, (L - 1 - j) * C:], z_r], -1))
    m_intra = jnp.stack(rows, 1).reshape(G, SSM_CW, SSM_CW)
    both = lambda t_f, t_r: jnp.concatenate([t_f, t_r], -1)
    pin_re = jnp.transpose(both(pw_re[:L, 0][::-1], pw_re[:L, 1]), (1, 0, 2))[:, :, None, :]
    pin_im = jnp.transpose(both(pw_im[:L, 0][::-1], pw_im[:L, 1]), (1, 0, 2))[:, :, None, :]
    bt_re = both(jnp.transpose(bb_re[0], (0, 2, 1)), jnp.transpose(bb_re[1], (0, 2, 1)))[:, None]
    bt_im = both(jnp.transpose(bb_im[0], (0, 2, 1)), jnp.transpose(bb_im[1], (0, 2, 1)))[:, None]
    m_in = jnp.concatenate([pin_re * bt_re - pin_im * bt_im, pin_re * bt_im + pin_im * bt_re],
                           -1).reshape(G, SSM_CW, SSM_CW)
    pst_re = jnp.transpose(both(pw_re[1:, 0], pw_re[1:, 1][::-1]), (1, 0, 2))[:, :, None, :]
    pst_im = jnp.transpose(both(pw_im[1:, 0], pw_im[1:, 1][::-1]), (1, 0, 2))[:, :, None, :]
    ct_re = both(c_re[0].astype(F32), c_re[1].astype(F32))[:, None]
    ct_im = both(c_im[0].astype(F32), c_im[1].astype(F32))[:, None]
    m_state_t = jnp.concatenate([pst_re * ct_re - pst_im * ct_im, -(pst_re * ct_im + pst_im * ct_re)],
                                -1).reshape(G, SSM_CW, SSM_CW)
    a_l = jnp.stack([both(pw_re[L, 0], pw_re[L, 1]), both(pw_im[L, 0], pw_im[L, 1])], 1)
    return m_in.astype(BF16), m_intra.astype(BF16), m_state_t.astype(BF16), a_l


def _ssm_pack_kernel(uc_ref, u_ref, x_ref, *, nc_ctx, nc_lat):
    for src, row0, nch in ((uc_ref, 0, nc_ctx), (u_ref, nc_ctx, nc_lat)):
        steps = [src[0, pl.ds(j, nch, stride=SSM_CHUNK), :] for j in range(SSM_CHUNK)]
        lane = lax.broadcasted_iota(jnp.int32, (nch, LANES), 1)
        piece = [(lane >= jj * SSM_GROUP) & (lane < (jj + 1) * SSM_GROUP) for jj in range(SSM_SLAB_GROUPS)]
        for g in range(SSM_SLAB_GROUPS):
            for h in range(SSM_CW // LANES):
                acc = jnp.zeros((nch, LANES), F32)
                for jj in range(SSM_SLAB_GROUPS):
                    z = steps[h * SSM_SLAB_GROUPS + jj]
                    shift = ((jj - g) * SSM_GROUP) % LANES
                    acc = jnp.where(piece[jj], z if shift == 0 else pltpu.roll(z, shift, 1), acc)
                x_ref[g, row0:row0 + nch, h * LANES:(h + 1) * LANES] = acc.astype(BF16)


def _ssm_unpack_kernel(y_ref, oc_ref, o_ref, *, nc_ctx, nc_lat):
    for dst, row0, nch in ((oc_ref, 0, nc_ctx), (o_ref, nc_ctx, nc_lat)):
        lane = lax.broadcasted_iota(jnp.int32, (nch, LANES), 1)
        piece = [(lane >= g * SSM_GROUP) & (lane < (g + 1) * SSM_GROUP) for g in range(SSM_SLAB_GROUPS)]
        for i in range(SSM_CHUNK):
            h, ii = divmod(i, SSM_SLAB_GROUPS)
            acc = jnp.zeros((nch, LANES), F32)
            for g in range(SSM_SLAB_GROUPS):
                z = y_ref[g, row0:row0 + nch, h * LANES:(h + 1) * LANES]
                shift = ((g - ii) * SSM_GROUP) % LANES
                acc = jnp.where(piece[g], z if shift == 0 else pltpu.roll(z, shift, 1), acc)
            dst[0, pl.ds(i, nch, stride=SSM_CHUNK), :] = acc


def _ssm_kernel(x_ref, min_ref, mintra_ref, mstate_ref, al_ref, y_ref, v_scr, s_scr, *, nb, nc_ctx, n_chunks):
    xv = x_ref[0]
    v = jnp.dot(xv, min_ref[0], preferred_element_type=F32)
    v_scr[0] = v[:, 0:LANES]
    v_scr[1] = v[:, LANES:]
    ar = al_ref[0, 0:1, :]
    ai = al_ref[0, 1:2, :]
    lo = lax.broadcasted_iota(jnp.int32, (nb, LANES), 1) < HALF

    def body(k, carry):
        sre, sim = carry
        kr = jnp.where(k < nc_ctx, nc_ctx - 1 - k, n_chunks - 1 + nc_ctx - k)
        rf = pl.ds(k, nb, stride=n_chunks)
        rr = pl.ds(kr, nb, stride=n_chunks)
        s_scr[0, rf, :] = sre
        s_scr[1, rr, :] = sre
        s_scr[2, rf, :] = sim
        s_scr[3, rr, :] = sim
        vre = jnp.where(lo, v_scr[0, rf, :], v_scr[0, rr, :])
        vim = jnp.where(lo, v_scr[1, rf, :], v_scr[1, rr, :])
        return ar * sre - ai * sim + vre, ar * sim + ai * sre + vim

    zero = jnp.zeros((nb, LANES), F32)
    lax.fori_loop(0, n_chunks, body, (zero, zero))
    lo_all = lax.broadcasted_iota(jnp.int32, (s_scr.shape[1], LANES), 1) < HALF
    s_in = jnp.concatenate([jnp.where(lo_all, s_scr[0], s_scr[1]), jnp.where(lo_all, s_scr[2], s_scr[3])],
                           axis=1).astype(BF16)
    y_ref[0] = jnp.dot(xv, mintra_ref[0], preferred_element_type=F32) + _nt_dot(s_in, mstate_ref[0])


def _ssm_scan(u, uc, mats):
    m_in, m_intra, m_state, a_l = mats
    b, n, _ = u.shape
    nc = uc.shape[1]
    nc_ctx, nc_lat = nc // SSM_CHUNK, n // SSM_CHUNK
    n_chunks = nc_ctx + nc_lat
    r = n_chunks * b
    nat = lambda m: pl.BlockSpec((1, m, LANES), lambda i, s: (i, 0, s))
    grp = pl.BlockSpec((SSM_SLAB_GROUPS, n_chunks, SSM_CW), lambda i, s: (s, i, 0))
    xg = pl.pallas_call(
        functools.partial(_ssm_pack_kernel, nc_ctx=nc_ctx, nc_lat=nc_lat),
        out_shape=_sds((SSM_GROUPS, r, SSM_CW), BF16),
        grid=(b, SSM_GROUPS // SSM_SLAB_GROUPS),
        in_specs=[nat(nc), nat(n)],
        out_specs=grp,
        compiler_params=_cparams("parallel", "parallel"),
        name="ssm_pack",
    )(uc, u)
    mat = pl.BlockSpec((1, SSM_CW, SSM_CW), lambda g: (g, 0, 0))
    yg = pl.pallas_call(
        functools.partial(_ssm_kernel, nb=b, nc_ctx=nc_ctx, n_chunks=n_chunks),
        out_shape=_sds((SSM_GROUPS, r, SSM_CW), F32),
        grid=(SSM_GROUPS,),
        in_specs=[pl.BlockSpec((1, r, SSM_CW), lambda g: (g, 0, 0)), mat, mat, mat,
                  pl.BlockSpec((1, 2, LANES), lambda g: (g, 0, 0))],
        out_specs=pl.BlockSpec((1, r, SSM_CW), lambda g: (g, 0, 0)),
        scratch_shapes=[pltpu.VMEM((2, r, LANES), F32), pltpu.VMEM((4, r, LANES), F32)],
        compiler_params=_cparams("parallel"),
        name="ssm_scan",
    )(xg, m_in, m_intra, m_state, a_l)
    ysc, ys = pl.pallas_call(
        functools.partial(_ssm_unpack_kernel, nc_ctx=nc_ctx, nc_lat=nc_lat),
        out_shape=(_sds((b, nc, SSM_WIDTH), F32), _sds((b, n, SSM_WIDTH), F32)),
        grid=(b, SSM_GROUPS // SSM_SLAB_GROUPS),
        in_specs=[grp],
        out_specs=(nat(nc), nat(n)),
        compiler_params=_cparams("parallel", "parallel"),
        name="ssm_unpack",
    )(yg)
    return ys, ysc


def _split_rows(ref, val):
    m = val.shape[0]
    for s in range(TOKEN_TILE_ROWS):
        ref[pl.ds(s, m, stride=TOKEN_TILE_ROWS), :] = val[:, s * LANES:(s + 1) * LANES]


def _merge_rows(ref):
    m = ref.shape[0] // TOKEN_TILE_ROWS
    return jnp.concatenate([ref[pl.ds(s, m, stride=TOKEN_TILE_ROWS), :] for s in range(TOKEN_TILE_ROWS)], axis=1)


def _token_tile(ref, t):
    return ref.at[pl.ds(pl.multiple_of(t * TOKEN_TILE_ROWS, TOKEN_TILE_ROWS), TOKEN_TILE_ROWS)]


def _post_kernel(*refs, alpha, with_ssm):
    if with_ssm:
        (att_ref, ys_ref, u_ref, dsk_ref, gw_ref, gb_ref, wo_ref, x_ref, g1_ref, lg_ref, lb_ref, sc2_ref, sh2_ref,
         x1_ref, h2_ref) = refs
        y = ys_ref[0] + u_ref[0] * dsk_ref[...]
        gl = jax.nn.gelu(y)
        gate = jax.nn.sigmoid(jnp.dot(gl.astype(BF16), gw_ref[...], preferred_element_type=F32) + gb_ref[...])
        ssm = (gl * gate).astype(BF16)
        mix = (jnp.dot(att_ref[0], wo_ref[0:SWA_Q_W, :], preferred_element_type=F32)
               + jnp.dot(ssm, wo_ref[SWA_Q_W:, :], preferred_element_type=F32))
    else:
        att_ref, wo_ref, x_ref, g1_ref, lg_ref, lb_ref, sc2_ref, sh2_ref, x1_ref, h2_ref = refs
        mix = jnp.dot(att_ref[0], wo_ref[...], preferred_element_type=F32)
    x1 = _layer_norm(alpha * x_ref[0] + g1_ref[0] * mix, lg_ref[...], lb_ref[...])
    x1_ref[0] = x1
    _split_rows(h2_ref, x1 * (1.0 + sc2_ref[0]) + sh2_ref[0])


def _post(att, ssm_args, w_out_bf16, x, g1, ln_g, ln_b, sc2, sh2, alpha, tm):
    b, n, d = x.shape
    tok = lambda w: pl.BlockSpec((1, tm, w), lambda i, j: (i, j, 0))
    vec = pl.BlockSpec((1, 1, d), lambda i, j: (i, 0, 0))
    const = lambda a: pl.BlockSpec(a.shape, lambda i, j: (0,) * a.ndim)
    in_specs = [tok(att.shape[-1])]
    args = [att]
    if ssm_args is not None:
        ys, u, dsk, gw, gb = ssm_args
        in_specs += [tok(SSM_WIDTH), tok(SSM_WIDTH), const(dsk), const(gw), const(gb)]
        args += [ys, u, dsk, gw, gb]
    in_specs += [const(w_out_bf16), tok(d), vec, const(ln_g), const(ln_b), vec, vec]
    args += [w_out_bf16, x, g1, ln_g, ln_b, sc2, sh2]
    per_b = n // tm
    return pl.pallas_call(
        functools.partial(_post_kernel, alpha=alpha, with_ssm=ssm_args is not None),
        out_shape=(_sds((b, n, d), F32), _sds((b * n * TOKEN_TILE_ROWS, LANES), F32)),
        grid=(b, n // tm),
        in_specs=in_specs,
        out_specs=(tok(d), pl.BlockSpec((tm * TOKEN_TILE_ROWS, LANES), lambda i, j: (i * per_b + j, 0))),
        compiler_params=_cparams("parallel", "parallel"),
        name="post_mixer_ssm" if ssm_args is not None else "post_mixer",
    )(*args)


def _first_max(v, sub):
    m = jnp.max(v, 0, keepdims=True)
    idx = jnp.min(jnp.where(v == m, sub, float(SUBLANES)), 0, keepdims=True)
    return m, idx


def _stream_blocks(toks, tb):
    starts = [0]
    for t in toks:
        starts.append(starts[-1] + t.shape[0] // (tb * TOKEN_TILE_ROWS))
    return starts


def _stream_spec(tok, tb, start):
    last = tok.shape[0] // (tb * TOKEN_TILE_ROWS) - 1
    return pl.BlockSpec((tb * TOKEN_TILE_ROWS, LANES), lambda i, *_: (jnp.clip(i - start, 0, last), 0))


def _router_kernel(*refs, starts):
    n_streams = len(starts) - 1
    tok_refs = refs[:n_streams]
    w_ref, b_ref, ids_ref, wts_ref, rank_ref, cnt_ref, carry_scr = refs[n_streams:]
    step = pl.program_id(0)

    @pl.when(step == 0)
    def _():
        carry_scr[...] = jnp.zeros_like(carry_scr)

    h = _merge_rows(tok_refs[0])
    for ref, start in zip(tok_refs[1:], starts[1:]):
        h = jnp.where(step >= start, _merge_rows(ref), h)
    w = w_ref[...]
    w_hi, h_hi = w.astype(BF16), h.astype(BF16)
    w_lo, h_lo = (w - w_hi.astype(F32)).astype(BF16), (h - h_hi.astype(F32)).astype(BF16)
    logits = _nt_dot(w_hi, h_hi) + (_nt_dot(w_hi, h_lo) + _nt_dot(w_lo, h_hi)) + b_ref[...]
    tm = logits.shape[1]
    sub = lax.broadcasted_iota(jnp.int32, (SUBLANES, tm), 0).astype(F32)
    gl = logits[0:SUBLANES]
    gmax, gi = _first_max(gl, sub)
    gp = 1.0 / jnp.sum(jnp.exp(gl - gmax), 0, keepdims=True)
    le = logits[ROUTER_EXPERT_ROW0:ROUTER_EXPERT_ROW0 + MOE_EPG]
    for g in range(1, MOE_GROUPS):
        le = jnp.where(gi == float(g), logits[ROUTER_EXPERT_ROW0 + g * MOE_EPG:ROUTER_EXPERT_ROW0 + (g + 1) * MOE_EPG], le)
    m1, i1 = _first_max(le, sub)
    m2, i2 = _first_max(jnp.where(sub == i1, NEG_INF, le), sub)
    t = jnp.exp(m2 - m1)
    e1 = gi * float(MOE_EPG) + i1
    e2 = gi * float(MOE_EPG) + i2
    ids_ref[0:1, :] = e1.astype(jnp.int32)
    ids_ref[1:2, :] = e2.astype(jnp.int32)
    wts_ref[0:1, :] = gp / (1.0 + t)
    wts_ref[1:2, :] = gp * t / (1.0 + t)
    esub = lax.broadcasted_iota(jnp.int32, (MOE_EXPERTS, tm), 0).astype(F32)
    oh1 = (esub == e1).astype(F32)
    oh2 = (esub == e2).astype(F32)
    both = oh1 + oh2
    earlier = (lax.broadcasted_iota(jnp.int32, (tm, tm), 0) < lax.broadcasted_iota(jnp.int32, (tm, tm), 1))
    prefix = jnp.dot(both.astype(BF16), earlier.astype(BF16), preferred_element_type=F32) + carry_scr[...]
    rank_ref[0:1, :] = jnp.sum(oh1 * prefix, 0, keepdims=True).astype(jnp.int32)
    rank_ref[1:2, :] = jnp.sum(oh2 * prefix, 0, keepdims=True).astype(jnp.int32)
    carry_scr[...] += jnp.sum(both, 1, keepdims=True)
    cnt_ref[...] = jnp.broadcast_to(carry_scr[...], cnt_ref.shape)


def _router(toks, wg, bg, we, be, tm):
    starts = _stream_blocks(toks, tm)
    t = starts[-1] * tm
    d = D_MODEL
    rows = ROUTER_EXPERT_ROW0 + MOE_EXPERTS
    w = jnp.zeros((rows, d), F32)
    w = w.at[:MOE_GROUPS].set(wg.T)
    w = w.at[ROUTER_EXPERT_ROW0:].set(jnp.transpose(we, (0, 2, 1)).reshape(MOE_EXPERTS, d))
    bias = jnp.full((rows, 1), NEG_INF, F32)
    bias = bias.at[:MOE_GROUPS, 0].set(bg)
    bias = bias.at[ROUTER_EXPERT_ROW0:, 0].set(be.reshape(-1))
    pair = pl.BlockSpec((MOE_TOPK, tm), lambda i: (0, i))
    ids, wts, rank, cnt = pl.pallas_call(
        functools.partial(_router_kernel, starts=starts),
        out_shape=(_sds((MOE_TOPK, t), jnp.int32), _sds((MOE_TOPK, t), F32), _sds((MOE_TOPK, t), jnp.int32),
                   _sds((MOE_EXPERTS, LANES), F32)),
        grid=(t // tm,),
        in_specs=[_stream_spec(tok, tm, s0) for tok, s0 in zip(toks, starts)]
        + [pl.BlockSpec((rows, d), lambda i: (0, 0)), pl.BlockSpec((rows, 1), lambda i: (0, 0))],
        out_specs=(pair, pair, pair, pl.BlockSpec((MOE_EXPERTS, LANES), lambda i: (0, 0))),
        scratch_shapes=[pltpu.VMEM((MOE_EXPERTS, 1), F32)],
        compiler_params=_cparams("arbitrary"),
        name="moe_router",
    )(*toks, w, bias)
    return ids, wts, rank, cnt[:, 0].astype(jnp.int32)


def _moe_plan(ids, rank, counts, n_tok):
    tm = MOE_ROW_TILE
    padded = ((counts + tm - 1) // tm) * tm
    ends = jnp.cumsum(padded)
    offs = ends - padded
    experts = jnp.arange(MOE_EXPERTS, dtype=jnp.int32)
    dest = (jnp.sum(jnp.where(ids[..., None] == experts, offs, 0), -1) + rank).astype(jnp.int32)
    n_tiles = (MOE_TOPK * n_tok + MOE_EXPERTS * (tm - 1)) // tm
    starts = jnp.arange(n_tiles, dtype=jnp.int32) * tm
    tile_expert = jnp.minimum(jnp.sum((ends[None, :] <= starts[:, None]).astype(jnp.int32), -1), MOE_EXPERTS - 1)
    n_valid = (ends[-1] // tm).astype(jnp.int32).reshape(1)
    pad = jnp.stack([offs + counts, padded - counts, jnp.broadcast_to(n_valid, counts.shape)]).astype(jnp.int32)
    return dest, tile_expert, n_valid, pad, n_tiles * tm


def _pad_fill(pad_ref, zero_scr, xs_ref, sem, wait):
    def per_expert(e, carry):
        first, count = pad_ref[0, e], pad_ref[1, e]
        piece = MOE_ROW_TILE // 2
        while piece >= 1:
            row = first + (count & ~(2 * piece - 1))
            n = piece * TOKEN_TILE_ROWS

            @pl.when((count & piece) != 0)
            def _(row=row, n=n):
                copy = pltpu.make_async_copy(
                    zero_scr.at[pl.ds(0, n)],
                    xs_ref.at[pl.ds(pl.multiple_of(row * TOKEN_TILE_ROWS, TOKEN_TILE_ROWS), n)], sem)
                copy.wait() if wait else copy.start()

            piece //= 2
        return carry

    lax.fori_loop(0, MOE_EXPERTS, per_expert, 0)
    half = MOE_ROW_TILE // 2 * TOKEN_TILE_ROWS
    n_tiles = xs_ref.shape[0] // (2 * half)

    def per_tile(t, carry):
        for h in range(2):
            copy = pltpu.make_async_copy(zero_scr, xs_ref.at[pl.ds(pl.multiple_of((2 * t + h) * half, half), half)],
                                         sem)
            copy.wait() if wait else copy.start()
        return carry

    lax.fori_loop(pad_ref[2, 0], n_tiles, per_tile, 0)


def _dispatch_kernel(*refs, tb, starts):
    n_streams = len(starts) - 1
    dest_ref, pad_ref = refs[:2]
    tok_refs = refs[2:2 + n_streams]
    xs_ref, zero_scr, sem, pad_sem = refs[2 + n_streams:]
    step = pl.program_id(0)

    @pl.when(step == 0)
    def _():
        zero_scr[...] = jnp.zeros_like(zero_scr)
        _pad_fill(pad_ref, zero_scr, xs_ref, pad_sem, wait=False)

    for s, tok_ref in enumerate(tok_refs):
        @pl.when((step >= starts[s]) & (step < starts[s + 1]))
        def _(tok_ref=tok_ref):
            def body(r, carry):
                for k in range(MOE_TOPK):
                    pltpu.make_async_copy(_token_tile(tok_ref, r), _token_tile(xs_ref, dest_ref[0, k, r]), sem).start()
                return carry

            lax.fori_loop(0, tb, body, 0, unroll=8)

    for k in range(MOE_TOPK):
        pltpu.make_async_copy(tok_refs[0], xs_ref.at[pl.ds(0, tb * TOKEN_TILE_ROWS)], sem).wait()

    @pl.when(step == 0)
    def _():
        _pad_fill(pad_ref, zero_scr, xs_ref, pad_sem, wait=True)


def _dispatch(toks, dest3, pad, n_rows):
    nblk, _, tb = dest3.shape
    starts = _stream_blocks(toks, tb)
    return pl.pallas_call(
        functools.partial(_dispatch_kernel, tb=tb, starts=starts),
        out_shape=_sds((n_rows * TOKEN_TILE_ROWS, LANES), F32),
        grid=(nblk,),
        in_specs=[pl.BlockSpec((1, MOE_TOPK, tb), lambda i: (i, 0, 0), memory_space=pltpu.SMEM),
                  pl.BlockSpec(memory_space=pltpu.SMEM)]
        + [_stream_spec(tok, tb, s0) for tok, s0 in zip(toks, starts)],
        out_specs=pl.BlockSpec(memory_space=pl.ANY),
        scratch_shapes=[pltpu.VMEM((MOE_ROW_TILE // 2 * TOKEN_TILE_ROWS, LANES), F32),
                        pltpu.SemaphoreType.DMA(()), pltpu.SemaphoreType.DMA(())],
        compiler_params=_cparams("arbitrary"),
        name="moe_dispatch",
    )(dest3, pad, *toks)


def _ffn_kernel(te_ref, nv_ref, x_ref, w1_ref, w3_ref, w2_ref, y_ref, w13_scr, w2_scr):
    i = pl.program_id(0)
    f = w1_ref.shape[2]

    @pl.when((i == 0) | (te_ref[i] != te_ref[jnp.maximum(i - 1, 0)]))
    def _():
        w13_scr[:, 0:f] = w1_ref[0].astype(BF16)
        w13_scr[:, f:2 * f] = w3_ref[0].astype(BF16)
        w2_scr[...] = w2_ref[0].astype(BF16)

    @pl.when(i < nv_ref[0])
    def _():
        h13 = jnp.dot(_merge_rows(x_ref).astype(BF16), w13_scr[...], preferred_element_type=F32)
        h1 = h13[:, 0:f]
        hh = (h1 * jax.nn.sigmoid(h1) * h13[:, f:2 * f]).astype(BF16)
        _split_rows(y_ref, jnp.dot(hh, w2_scr[...], preferred_element_type=F32))

    @pl.when(i >= nv_ref[0])
    def _():
        y_ref[...] = jnp.zeros_like(y_ref)


def _expert_ffn(tile_expert, n_valid, xs, w1, w3, w2):
    p = xs.shape[0] // TOKEN_TILE_ROWS
    _, d, f = w1.shape
    tm = MOE_ROW_TILE
    rows = pl.BlockSpec((tm * TOKEN_TILE_ROWS, LANES), lambda i, te, nv: (i, 0))
    rows_in = pl.BlockSpec((tm * TOKEN_TILE_ROWS, LANES), lambda i, te, nv: (jnp.minimum(i, nv[0] - 1), 0))
    return pl.pallas_call(
        _ffn_kernel,
        out_shape=_sds(xs.shape, F32),
        grid_spec=pltpu.PrefetchScalarGridSpec(
            num_scalar_prefetch=2,
            grid=(p // tm,),
            in_specs=[rows_in,
                      pl.BlockSpec((1, d, f), lambda i, te, nv: (te[i], 0, 0)),
                      pl.BlockSpec((1, d, f), lambda i, te, nv: (te[i], 0, 0)),
                      pl.BlockSpec((1, f, d), lambda i, te, nv: (te[i], 0, 0))],
            out_specs=rows,
            scratch_shapes=[pltpu.VMEM((d, 2 * f), BF16), pltpu.VMEM((f, d), BF16)]),
        compiler_params=_cparams("arbitrary"),
        name="moe_expert_ffn",
    )(tile_expert, n_valid, xs, w1, w3, w2)


def _combine_ln2_kernel(dest_ref, x1_ref, ys_ref, wt_ref, g2_ref, lg_ref, lb_ref, o_ref, buf, sem, *, alpha, tb):
    def body(r, carry):
        for k in range(MOE_TOPK):
            pltpu.make_async_copy(_token_tile(ys_ref, dest_ref[0, k, r]), _token_tile(buf.at[k], r), sem.at[k]).start()
        return carry

    lax.fori_loop(0, tb, body, 0, unroll=8)
    for k in range(MOE_TOPK):
        pltpu.make_async_copy(ys_ref.at[pl.ds(0, tb * TOKEN_TILE_ROWS)], buf.at[k], sem.at[k]).wait()
    f = wt_ref[:, 0:1] * _merge_rows(buf.at[0]) + wt_ref[:, 1:2] * _merge_rows(buf.at[1])
    o_ref[...] = _layer_norm(alpha * x1_ref[...] + g2_ref[0] * f, lg_ref[...], lb_ref[...])


def _combine_ln2(dest3, x1, ys, wt, g2, ln_g, ln_b, alpha, n_per_sample):
    t, d = x1.shape
    nblk, _, tb = dest3.shape
    const = pl.BlockSpec((1, d), lambda i: (0, 0))
    return pl.pallas_call(
        functools.partial(_combine_ln2_kernel, alpha=alpha, tb=tb),
        out_shape=_sds((t, d), F32),
        grid=(nblk,),
        in_specs=[pl.BlockSpec((1, MOE_TOPK, tb), lambda i: (i, 0, 0), memory_space=pltpu.SMEM),
                  pl.BlockSpec((tb, d), lambda i: (i, 0)),
                  pl.BlockSpec(memory_space=pl.ANY),
                  pl.BlockSpec((tb, MOE_TOPK), lambda i: (i, 0)),
                  pl.BlockSpec((1, 1, d), lambda i: ((i * tb) // n_per_sample, 0, 0)),
                  const, const],
        out_specs=pl.BlockSpec((tb, d), lambda i: (i, 0)),
        scratch_shapes=[pltpu.VMEM((MOE_TOPK, tb * TOKEN_TILE_ROWS, LANES), F32),
                        pltpu.SemaphoreType.DMA((MOE_TOPK,))],
        compiler_params=_cparams("arbitrary"),
        name="moe_combine_ln2",
    )(dest3, x1, ys, wt, g2, ln_g, ln_b)


def _proj1_kernel(*refs, rope, with_q):
    x_ref, sc_ref, sh_ref, w_ref = refs[:4]
    refs = refs[4:]
    tabs = None
    if rope:
        tabs = (refs[0][...], refs[1][...], refs[2][...])
        refs = refs[3:]
    h = (x_ref[0] * (1.0 + sc_ref[0]) + sh_ref[0]).astype(BF16)
    r = jnp.dot(h, w_ref[...], preferred_element_type=F32)
    off = 0
    if with_q:
        q_ref, k_ref, v_ref = refs
        for hd in range(DIF_HEADS):
            q_ref[0, hd] = (_rot(r[:, hd * LANES:(hd + 1) * LANES], tabs) * (LOG2_E * HEAD_DIM ** -0.5)).astype(BF16)
        off = DIF_QK_W
    else:
        k_ref, v_ref = refs
    for hd in range(DIF_HEADS):
        k_ref[0, hd] = _rot(r[:, off + hd * LANES:off + (hd + 1) * LANES], tabs).astype(BF16)
        v_ref[0, hd] = r[:, off + DIF_QK_W + hd * LANES:off + DIF_QK_W + (hd + 1) * LANES].astype(BF16)


def _proj1(x, sc, sh, w_bf16, tabs, with_q, tm):
    b, n, d = x.shape
    rope = tabs is not None
    vec = pl.BlockSpec((1, 1, d), lambda i, j: (i, 0, 0))
    in_specs = [pl.BlockSpec((1, tm, d), lambda i, j: (i, j, 0)), vec, vec,
                pl.BlockSpec(w_bf16.shape, lambda i, j: (0, 0))]
    args = [x, sc, sh, w_bf16]
    if rope:
        in_specs += [pl.BlockSpec((tm, LANES), lambda i, j: (j, 0))] * 3
        args += list(tabs)
    hm = pl.BlockSpec((1, DIF_HEADS, tm, LANES), lambda i, j: (i, 0, j, 0))
    n_out = 3 if with_q else 2
    return pl.pallas_call(
        functools.partial(_proj1_kernel, rope=rope, with_q=with_q),
        out_shape=(_sds((b, DIF_HEADS, n, LANES), BF16),) * n_out,
        grid=(b, n // tm),
        in_specs=in_specs,
        out_specs=(hm,) * n_out,
        compiler_params=_cparams("parallel", "parallel"),
        name="proj1_qkv" if with_q else "proj1_kv_ctx",
    )(*args)


def _diff_kernel(lam_ref, q_ref, kl_ref, kc_ref, vl_ref, vc_ref, g_ref, o_ref, k_scr, v_scr, *, tq, n_lat, out_scale):
    @pl.when(pl.program_id(2) == 0)
    def _():
        k_scr[0:n_lat] = kl_ref[0, 0]
        k_scr[n_lat:] = kc_ref[0, 0]
        v_scr[0:n_lat, 0:LANES] = vl_ref[0, 0]
        v_scr[n_lat:, 0:LANES] = vc_ref[0, 0]
        v_scr[:, LANES:] = jnp.ones((v_scr.shape[0], LANES), BF16)

    lo = lax.broadcasted_iota(jnp.int32, (DIF_CHAIN_ROWS, LANES), 1) < HALF
    for r0 in range(0, tq, DIF_CHAIN_ROWS):
        q = q_ref[0, 0, r0:r0 + DIF_CHAIN_ROWS, :].astype(F32)
        maps = []
        for qm in (jnp.where(lo, q, 0.0), jnp.where(lo, 0.0, q)):
            s = _nt_dot(qm.astype(BF16), k_scr[...])
            p = jnp.exp2(s - jnp.max(s, -1, keepdims=True)).astype(BF16)
            oe = jnp.dot(p, v_scr[...], preferred_element_type=F32)
            maps.append(oe[:, 0:LANES] * (1.0 / oe[:, LANES:]))
        o = maps[0] - lam_ref[0] * maps[1]
        o = o * lax.rsqrt(jnp.mean(o * o, -1, keepdims=True) + RMS_EPS) * g_ref[...]
        o_ref[0, r0:r0 + DIF_CHAIN_ROWS, :] = (o * out_scale).astype(BF16)


def _diff_attention(lam, q, kl, kc, vl, vc, subln_g, lam_init, tq):
    b, nh, n, _ = q.shape
    nc = kc.shape[2]
    kv = lambda m: pl.BlockSpec((1, 1, m, LANES), lambda i, h, j: (i, h, 0, 0))
    return pl.pallas_call(
        functools.partial(_diff_kernel, tq=tq, n_lat=n, out_scale=1.0 - lam_init),
        out_shape=_sds((b, n, nh * LANES), BF16),
        grid=(b, nh, n // tq),
        in_specs=[pl.BlockSpec(memory_space=pltpu.SMEM),
                  pl.BlockSpec((1, 1, tq, LANES), lambda i, h, j: (i, h, j, 0)),
                  kv(n), kv(nc), kv(n), kv(nc),
                  pl.BlockSpec((1, LANES), lambda i, h, j: (0, 0))],
        out_specs=pl.BlockSpec((1, tq, LANES), lambda i, h, j: (i, j, h)),
        scratch_shapes=[pltpu.VMEM((n + nc, LANES), BF16), pltpu.VMEM((n + nc, 2 * LANES), BF16)],
        compiler_params=_cparams("parallel", "parallel", "arbitrary"),
        name="diff_attention",
    )(lam, q, kl, kc, vl, vc, subln_g)


def _moe_block(toks, layer, moe_wg, moe_bg, moe_we, moe_be, moe_w1, moe_w3, moe_w2):
    t = sum(tok.shape[0] for tok in toks) // TOKEN_TILE_ROWS
    tb = MOE_DMA_TILE
    ids, wts, rank, counts = _router(toks, moe_wg[layer], moe_bg[layer], moe_we[layer], moe_be[layer], tm=512)
    dest, tile_expert, n_valid, pad, n_rows = _moe_plan(ids, rank, counts, t)
    dest3 = dest.reshape(MOE_TOPK, t // tb, tb).transpose(1, 0, 2)
    xs = _dispatch(toks, dest3, pad, n_rows)
    ys = _expert_ffn(tile_expert, n_valid, xs, moe_w1[layer], moe_w3[layer], moe_w2[layer])
    return ys, dest3, wts.T


def kernel(x, c, ctx, c_ctx, mod_w, mod_b, ln1_g, ln1_b, ln2_g, ln2_b, swa_ssm_w_in, swa_ssm_w_out, swa_sink, ssm_a_re, ssm_a_im, ssm_log_step, ssm_b_re, ssm_b_im, ssm_c_re, ssm_c_im, ssm_d, ssm_glu_w, ssm_glu_b, dif_w_in, dif_w_out, dif_lam_q1, dif_lam_k1, dif_lam_q2, dif_lam_k2, dif_subln_g, moe_wg, moe_bg, moe_we, moe_be, moe_w1, moe_w3, moe_w2):
    bsz, n, d = x.shape
    ctx_len = ctx.shape[1]
    depth = mod_w.shape[0]
    alpha = (2 * depth) ** 0.25
    tabs = _rope_tables(n)

    n_vec = 16
    cvec = jnp.zeros((n_vec, d), F32).at[:bsz].set(c).at[bsz].set(c_ctx)
    mods = _modulation(cvec, mod_w, mod_b)

    xl, xc = x, ctx
    for layer in range(depth):
        need_ctx = layer < depth - 1
        i = layer // 2
        lat = [mods[layer, :bsz, k * d:(k + 1) * d].reshape(bsz, 1, d) for k in range(6)]
        cx = [jnp.broadcast_to(mods[layer, bsz, k * d:(k + 1) * d].reshape(1, 1, d), (bsz, 1, d)) for k in range(6)]
        sh1, sc1, g1, sh2, sc2, g2 = lat
        csh1, csc1, cg1, csh2, csc2, cg2 = cx
        lg1, lb1 = ln1_g[layer].reshape(1, d), ln1_b[layer].reshape(1, d)
        lg2, lb2 = ln2_g[layer].reshape(1, d), ln2_b[layer].reshape(1, d)
        if layer % 2 == 0:
            w_in = swa_ssm_w_in[i].astype(BF16)
            w_out = swa_ssm_w_out[i].astype(BF16)
            q, k, v, u = _proj0(xl, sc1, sh1, w_in, tabs, tm=512)
            qc, kc, vc, uc = _proj0(xc, csc1, csh1, w_in, None, tm=ctx_len)
            sink = swa_sink[i].astype(F32)
            att = _swa_attention(sink, q, k, v, kc, vc)
            mats = _ssm_matrices(ssm_a_re[i], ssm_a_im[i], ssm_log_step[i], ssm_b_re[i], ssm_b_im[i],
                                 ssm_c_re[i], ssm_c_im[i])
            ys, ysc = _ssm_scan(u, uc, mats)
            glu = (ssm_d[i].reshape(1, SSM_WIDTH).astype(F32), ssm_glu_w[i].astype(BF16),
                   ssm_glu_b[i].reshape(1, SSM_WIDTH).astype(F32))
            x1, tok = _post(att, (ys, u) + glu, w_out, xl, g1, lg1, lb1, sc2, sh2, alpha, 512)
            toks = [tok]
            if need_ctx:
                att_c = _ctx_attention(sink, qc, kc, vc)
                xc1, tok_c = _post(att_c, (ysc, uc) + glu, w_out, xc, cg1, lg1, lb1, csc2, csh2, alpha, ctx_len)
                toks.append(tok_c)
        else:
            lam_init = 0.8 - 0.6 * math.exp(-0.3 * layer)
            w_in = dif_w_in[i].astype(BF16)
            w_out = dif_w_out[i].astype(BF16)
            q, k, v = _proj1(xl, sc1, sh1, w_in, tabs, True, tm=512)
            kc, vc = _proj1(xc, csc1, csh1, w_in[:, DIF_QK_W:], None, False, tm=ctx_len)
            lam = (jnp.exp(jnp.sum(dif_lam_q1[i].astype(F32) * dif_lam_k1[i].astype(F32)))
                   - jnp.exp(jnp.sum(dif_lam_q2[i].astype(F32) * dif_lam_k2[i].astype(F32))) + lam_init).reshape(1)
            att = _diff_attention(lam, q, k, kc, v, vc, dif_subln_g[i].reshape(1, DIF_V_HEAD).astype(F32),
                                  lam_init, tq=n)
            x1, tok = _post(att, None, w_out, xl, g1, lg1, lb1, sc2, sh2, alpha, 512)
            toks = [tok]
            if need_ctx:
                raise NotImplementedError("a differential-attention layer followed by another layer")
        ys_moe, dest3, wt = _moe_block(toks, layer, moe_wg, moe_bg, moe_we, moe_be, moe_w1, moe_w3, moe_w2)
        n_lat_blk = bsz * n // MOE_DMA_TILE
        xl = _combine_ln2(dest3[:n_lat_blk], x1.reshape(-1, d), ys_moe, wt[:bsz * n], g2, lg2, lb2, alpha,
                          n).reshape(bsz, n, d)
        if need_ctx:
            xc = _combine_ln2(dest3[n_lat_blk:], xc1.reshape(-1, d), ys_moe, wt[bsz * n:], cg2, lg2, lb2, alpha,
                              ctx_len).reshape(bsz, ctx_len, d)
    return xl
```

```python
import functools
import math

import jax
import jax.numpy as jnp
from jax import lax
from jax.experimental import pallas as pl
from jax.experimental.pallas import tpu as pltpu

F32 = jnp.float32
BF16 = jnp.bfloat16
HIGHEST = lax.Precision.HIGHEST

D_MODEL = 1024
GRID_W = 64
HEAD_DIM = 64
ROPE_BASE = 10000.0
ROPE_FREQS = HEAD_DIM // 4
LN_EPS = 1e-5
RMS_EPS = 1e-5
NEG_INF = -1e30
LOG2_E = math.log2(math.e)
LANES = 128
HALF = LANES // 2

SWA_HEADS = 8
SWA_KV_HEADS = 2
SWA_WINDOW = 128
SWA_BLOCK = 128
SWA_Q_W = SWA_HEADS * HEAD_DIM
SWA_KV_W = SWA_KV_HEADS * HEAD_DIM

SSM_WIDTH = D_MODEL // 2
SSM_GROUP = 16
SSM_GROUPS = SSM_WIDTH // SSM_GROUP
SSM_STATE = 64
SSM_CHUNK = 16
SSM_CW = SSM_CHUNK * SSM_GROUP
SSM_SLAB_GROUPS = LANES // SSM_GROUP

AB_IN_W = SWA_Q_W + 2 * SWA_KV_W + SSM_WIDTH

DIF_HEADS = D_MODEL // (2 * HEAD_DIM)
DIF_QK_W = DIF_HEADS * 2 * HEAD_DIM
DIF_V_HEAD = 2 * HEAD_DIM
DIF_V_W = DIF_HEADS * DIF_V_HEAD
DIF_CHAIN_ROWS = 128

MOE_GROUPS = 4
MOE_EPG = 8
MOE_EXPERTS = MOE_GROUPS * MOE_EPG
MOE_HIDDEN = D_MODEL // 4
MOE_TOPK = 2
SUBLANES = 8
ROUTER_EXPERT_ROW0 = SUBLANES
MOE_ROW_TILE = 512
MOE_DMA_TILE = 256
TOKEN_TILE_ROWS = D_MODEL // LANES

VMEM_LIMIT = 56 * 1024 * 1024


def _cparams(*sem):
    return pltpu.CompilerParams(dimension_semantics=sem, vmem_limit_bytes=VMEM_LIMIT)


def _sds(shape, dtype):
    return jax.ShapeDtypeStruct(shape, dtype)


def _nt_dot(a, b):
    return lax.dot_general(a, b, (((1,), (1,)), ((), ())), preferred_element_type=F32)


def _layer_norm(r, g, b):
    mu = jnp.mean(r, -1, keepdims=True)
    rc = r - mu
    var = jnp.mean(rc * rc, -1, keepdims=True)
    return rc * lax.rsqrt(var + LN_EPS) * g + b


def _mod_kernel(c_ref, w_ref, b_ref, o_ref):
    cv = c_ref[...]
    s = cv * jax.nn.sigmoid(cv)
    o_ref[0] = jnp.dot(s, w_ref[0], preferred_element_type=F32, precision=HIGHEST) + b_ref[0]


def _modulation(cvec, mod_w, mod_b):
    depth, d, w6 = mod_w.shape
    tn = 1536
    return pl.pallas_call(
        _mod_kernel,
        out_shape=_sds((depth, cvec.shape[0], w6), F32),
        grid=(depth, w6 // tn),
        in_specs=[pl.BlockSpec(cvec.shape, lambda l, j: (0, 0)),
                  pl.BlockSpec((1, d, tn), lambda l, j: (l, 0, j)),
                  pl.BlockSpec((1, 1, tn), lambda l, j: (l, 0, j))],
        out_specs=pl.BlockSpec((1, cvec.shape[0], tn), lambda l, j: (l, 0, j)),
        compiler_params=_cparams("arbitrary", "arbitrary"),
        name="modulation",
    )(cvec, mod_w, mod_b.reshape(depth, 1, w6))


def _rope_tables(n):
    rows = n // GRID_W
    row = jnp.repeat(jnp.arange(rows, dtype=F32), GRID_W)
    col = jnp.tile(jnp.arange(GRID_W, dtype=F32), rows)
    inv = ROPE_BASE ** (-jnp.arange(ROPE_FREQS, dtype=F32) / ROPE_FREQS)
    ang_r = row[:, None] * inv[None, :]
    ang_c = col[:, None] * inv[None, :]
    zeros = jnp.zeros_like(ang_r)
    cos64 = jnp.concatenate([jnp.cos(ang_r), jnp.cos(ang_r), jnp.cos(ang_c), jnp.cos(ang_c)], -1)
    sa64 = jnp.concatenate([-jnp.sin(ang_r), zeros, -jnp.sin(ang_c), zeros], -1)
    sb64 = jnp.concatenate([zeros, jnp.sin(ang_r), zeros, jnp.sin(ang_c)], -1)
    return tuple(jnp.tile(t, (1, LANES // HEAD_DIM)) for t in (cos64, sa64, sb64))


def _rot(t, tabs):
    if tabs is None:
        return t
    cos, sa, sb = tabs
    return t * cos + pltpu.roll(t, LANES - ROPE_FREQS, 1) * sa + pltpu.roll(t, ROPE_FREQS, 1) * sb


def _dup_halves(t):
    lo = lax.broadcasted_iota(jnp.int32, t.shape, 1) < HALF
    ta = jnp.where(lo, t, 0.0)
    tb = t - ta
    return ta + pltpu.roll(ta, HALF, 1), tb + pltpu.roll(tb, HALF, 1)


def _proj0_kernel(*refs, rope):
    if rope:
        x_ref, sc_ref, sh_ref, w_ref, cos_ref, sa_ref, sb_ref, q_ref, k_ref, v_ref, u_ref = refs
        tabs = (cos_ref[...], sa_ref[...], sb_ref[...])
    else:
        x_ref, sc_ref, sh_ref, w_ref, q_ref, k_ref, v_ref, u_ref = refs
        tabs = None
    h = (x_ref[0] * (1.0 + sc_ref[0]) + sh_ref[0]).astype(BF16)
    r = jnp.dot(h, w_ref[...], preferred_element_type=F32)
    scale = LOG2_E * HEAD_DIM ** -0.5
    for s in range(SWA_Q_W // LANES):
        q_ref[0, :, s * LANES:(s + 1) * LANES] = (_rot(r[:, s * LANES:(s + 1) * LANES], tabs) * scale).astype(BF16)
    k0, k1 = _dup_halves(_rot(r[:, SWA_Q_W:SWA_Q_W + LANES], tabs))
    v0, v1 = _dup_halves(r[:, SWA_Q_W + LANES:SWA_Q_W + 2 * LANES])
    k_ref[0, 0] = k0.astype(BF16)
    k_ref[0, 1] = k1.astype(BF16)
    v_ref[0, 0] = v0.astype(BF16)
    v_ref[0, 1] = v1.astype(BF16)
    u_ref[0] = r[:, SWA_Q_W + 2 * LANES:]


def _proj0(x, sc, sh, w_bf16, tabs, tm):
    b, n, d = x.shape
    rope = tabs is not None
    vec = pl.BlockSpec((1, 1, d), lambda i, j: (i, 0, 0))
    in_specs = [pl.BlockSpec((1, tm, d), lambda i, j: (i, j, 0)), vec, vec,
                pl.BlockSpec(w_bf16.shape, lambda i, j: (0, 0))]
    args = [x, sc, sh, w_bf16]
    if rope:
        in_specs += [pl.BlockSpec((tm, LANES), lambda i, j: (j, 0))] * 3
        args += list(tabs)
    kv_spec = pl.BlockSpec((1, SWA_KV_HEADS, tm, LANES), lambda i, j: (i, 0, j, 0))
    return pl.pallas_call(
        functools.partial(_proj0_kernel, rope=rope),
        out_shape=(_sds((b, n, SWA_Q_W), BF16), _sds((b, SWA_KV_HEADS, n, LANES), BF16),
                   _sds((b, SWA_KV_HEADS, n, LANES), BF16), _sds((b, n, SSM_WIDTH), F32)),
        grid=(b, n // tm),
        in_specs=in_specs,
        out_specs=(pl.BlockSpec((1, tm, SWA_Q_W), lambda i, j: (i, j, 0)), kv_spec, kv_spec,
                   pl.BlockSpec((1, tm, SSM_WIDTH), lambda i, j: (i, j, 0))),
        compiler_params=_cparams("parallel", "parallel"),
        name="proj0_rope" if rope else "proj0_ctx",
    )(*args)


def _swa_kernel(*refs, tq, local, n_lat):
    if local:
        sink_ref, q_ref, k_ref, v_ref, kc_ref, vc_ref, o_ref = refs
    else:
        sink_ref, q_ref, kc_ref, vc_ref, o_ref = refs
    j = pl.program_id(1)
    rows = 4 * tq
    lo = lax.broadcasted_iota(jnp.int32, (tq, LANES), 1) < HALF
    rown = lax.broadcasted_iota(jnp.int32, (rows, 1), 0)
    if local:
        span = 3 * SWA_BLOCK
        start = pl.multiple_of(jnp.clip((j - 1) * SWA_BLOCK, 0, n_lat - span), SWA_BLOCK)
        rr = lax.broadcasted_iota(jnp.int32, (rows, span), 0)
        cc = lax.broadcasted_iota(jnp.int32, (rows, span), 1)
        qpos = j * tq + (rr & (tq - 1))
        mask = jnp.abs(qpos - (start + cc)) <= SWA_WINDOW
    for h in range(SWA_KV_HEADS):
        qa = q_ref[0, :, (2 * h) * LANES:(2 * h + 1) * LANES].astype(F32)
        qb = q_ref[0, :, (2 * h + 1) * LANES:(2 * h + 2) * LANES].astype(F32)
        q4 = jnp.concatenate([jnp.where(lo, qa, 0.0), jnp.where(lo, 0.0, qa),
                              jnp.where(lo, qb, 0.0), jnp.where(lo, 0.0, qb)], 0).astype(BF16)
        sink = LOG2_E * jnp.where(rown < tq, sink_ref[4 * h],
                                  jnp.where(rown < 2 * tq, sink_ref[4 * h + 1],
                                            jnp.where(rown < 3 * tq, sink_ref[4 * h + 2], sink_ref[4 * h + 3])))
        s_ctx = _nt_dot(q4, kc_ref[0, h])
        m = jnp.maximum(jnp.max(s_ctx, -1, keepdims=True), sink)
        if local:
            s_loc = jnp.where(mask, _nt_dot(q4, k_ref[0, h, pl.ds(start, span), :]), NEG_INF)
            m = jnp.maximum(m, jnp.max(s_loc, -1, keepdims=True))
        p_ctx = jnp.exp2(s_ctx - m)
        den = jnp.sum(p_ctx, -1, keepdims=True) + jnp.exp2(sink - m)
        o4 = jnp.dot(p_ctx.astype(BF16), vc_ref[0, h], preferred_element_type=F32)
        if local:
            p_loc = jnp.exp2(s_loc - m)
            den = den + jnp.sum(p_loc, -1, keepdims=True)
            o4 = o4 + jnp.dot(p_loc.astype(BF16), v_ref[0, h, pl.ds(start, span), :], preferred_element_type=F32)
        o4 = o4 * (1.0 / den)
        o_ref[0, :, (2 * h) * LANES:(2 * h + 1) * LANES] = jnp.where(lo, o4[0:tq], o4[tq:2 * tq]).astype(BF16)
        o_ref[0, :, (2 * h + 1) * LANES:(2 * h + 2) * LANES] = jnp.where(lo, o4[2 * tq:3 * tq], o4[3 * tq:]).astype(BF16)


def _swa_attention(sink, q, k, v, kc, vc):
    b, n, _ = q.shape
    nc = kc.shape[2]
    tq = SWA_BLOCK
    full = lambda m: pl.BlockSpec((1, SWA_KV_HEADS, m, LANES), lambda i, j: (i, 0, 0, 0))
    return pl.pallas_call(
        functools.partial(_swa_kernel, tq=tq, local=True, n_lat=n),
        out_shape=_sds((b, n, SWA_Q_W), BF16),
        grid=(b, n // tq),
        in_specs=[pl.BlockSpec(memory_space=pltpu.SMEM),
                  pl.BlockSpec((1, tq, SWA_Q_W), lambda i, j: (i, j, 0)),
                  full(n), full(n), full(nc), full(nc)],
        out_specs=pl.BlockSpec((1, tq, SWA_Q_W), lambda i, j: (i, j, 0)),
        compiler_params=_cparams("parallel", "arbitrary"),
        name="swa_attention",
    )(sink, q, k, v, kc, vc)


def _ctx_attention(sink, qc, kc, vc):
    b, nc, _ = qc.shape
    full = pl.BlockSpec((1, SWA_KV_HEADS, nc, LANES), lambda i, j: (i, 0, 0, 0))
    return pl.pallas_call(
        functools.partial(_swa_kernel, tq=nc, local=False, n_lat=0),
        out_shape=_sds((b, nc, SWA_Q_W), BF16),
        grid=(b, 1),
        in_specs=[pl.BlockSpec(memory_space=pltpu.SMEM),
                  pl.BlockSpec((1, nc, SWA_Q_W), lambda i, j: (i, 0, 0)), full, full],
        out_specs=pl.BlockSpec((1, nc, SWA_Q_W), lambda i, j: (i, 0, 0)),
        compiler_params=_cparams("parallel", "arbitrary"),
        name="ctx_attention",
    )(sink, qc, kc, vc)


def _ssm_matrices(a_re, a_im, log_step, b_re, b_im, c_re, c_im):
    L = SSM_CHUNK
    ar = a_re.astype(F32)
    ai = a_im.astype(F32)
    dt = jnp.exp(log_step.astype(F32))[..., None]
    m = jnp.arange(L + 1, dtype=F32)[:, None, None, None]
    mag = jnp.exp(m * (dt * ar)[None])
    pw_re = mag * jnp.cos(m * (dt * ai)[None])
    pw_im = mag * jnp.sin(m * (dt * ai)[None])
    den = ar * ar + ai * ai
    nr = pw_re[1] - 1.0
    coef_re = (nr * ar + pw_im[1] * ai) / den
    coef_im = (pw_im[1] * ar - nr * ai) / den
    br = b_re.astype(F32)
    bi = b_im.astype(F32)
    bb_re = coef_re[..., None] * br - coef_im[..., None] * bi
    bb_im = coef_re[..., None] * bi + coef_im[..., None] * br
    cr = c_re.astype(F32)[None]
    ci = c_im.astype(F32)[None]
    ca_re = cr * pw_re[:, :, :, None, :] - ci * pw_im[:, :, :, None, :]
    ca_im = cr * pw_im[:, :, :, None, :] + ci * pw_re[:, :, :, None, :]
    kern = jnp.sum(ca_re[:L, ..., None] * bb_re[None, :, :, None] - ca_im[:L, ..., None] * bb_im[None, :, :, None],
                   axis=4)
    G, C = SSM_GROUPS, SSM_GROUP
    k_f = jnp.transpose(kern[:, 0], (1, 3, 0, 2)).reshape(G, C, SSM_CW)
    k_r = jnp.transpose(kern[::-1, 1], (1, 3, 0, 2)).reshape(G, C, SSM_CW)
    rows = []
    for j in range(L):
        z_f = jnp.zeros((G, C, j * C), F32)
        z_r = jnp.zeros((G, C, (L - 1 - j) * C), F32)
        rows.append(jnp.concatenate([z_f, k_f[:, :, :SSM_CW - j * C]], -1)
                    + jnp.concatenate([k_r[:, :, (L - 1 - j) * C:], z_r], -1))
    m_intra = jnp.stack(rows, 1).reshape(G, SSM_CW, SSM_CW)
    both = lambda t_f, t_r: jnp.concatenate([t_f, t_r], -1)
    pin_re = jnp.transpose(both(pw_re[:L, 0][::-1], pw_re[:L, 1]), (1, 0, 2))[:, :, None, :]
    pin_im = jnp.transpose(both(pw_im[:L, 0][::-1], pw_im[:L, 1]), (1, 0, 2))[:, :, None, :]
    bt_re = both(jnp.transpose(bb_re[0], (0, 2, 1)), jnp.transpose(bb_re[1], (0, 2, 1)))[:, None]
    bt_im = both(jnp.transpose(bb_im[0], (0, 2, 1)), jnp.transpose(bb_im[1], (0, 2, 1)))[:, None]
    m_in = jnp.concatenate([pin_re * bt_re - pin_im * bt_im, pin_re * bt_im + pin_im * bt_re],
                           -1).reshape(G, SSM_CW, SSM_CW)
    pst_re = jnp.transpose(both(pw_re[1:, 0], pw_re[1:, 1][::-1]), (1, 0, 2))[:, :, None, :]
    pst_im = jnp.transpose(both(pw_im[1:, 0], pw_im[1:, 1][::-1]), (1, 0, 2))[:, :, None, :]
    ct_re = both(c_re[0].astype(F32), c_re[1].astype(F32))[:, None]
    ct_im = both(c_im[0].astype(F32), c_im[1].astype(F32))[:, None]
    m_state_t = jnp.concatenate([pst_re * ct_re - pst_im * ct_im, -(pst_re * ct_im + pst_im * ct_re)],
                                -1).reshape(G, SSM_CW, SSM_CW)
    a_l = jnp.stack([both(pw_re[L, 0], pw_re[L, 1]), both(pw_im[L, 0], pw_im[L, 1])], 1)
    return m_in.astype(BF16), m_intra.astype(BF16), m_state_t.astype(BF16), a_l


def _ssm_pack_kernel(uc_ref, u_ref, x_ref, *, nc_ctx, nc_lat):
    for src, row0, nch in ((uc_ref, 0, nc_ctx), (u_ref, nc_ctx, nc_lat)):
        steps = [src[0, pl.ds(j, nch, stride=SSM_CHUNK), :] for j in range(SSM_CHUNK)]
        lane = lax.broadcasted_iota(jnp.int32, (nch, LANES), 1)
        piece = [(lane >= jj * SSM_GROUP) & (lane < (jj + 1) * SSM_GROUP) for jj in range(SSM_SLAB_GROUPS)]
        for g in range(SSM_SLAB_GROUPS):
            for h in range(SSM_CW // LANES):
                acc = jnp.zeros((nch, LANES), F32)
                for jj in range(SSM_SLAB_GROUPS):
                    z = steps[h * SSM_SLAB_GROUPS + jj]
                    shift = ((jj - g) * SSM_GROUP) % LANES
                    acc = jnp.where(piece[jj], z if shift == 0 else pltpu.roll(z, shift, 1), acc)
                x_ref[g, row0:row0 + nch, h * LANES:(h + 1) * LANES] = acc.astype(BF16)


def _ssm_unpack_kernel(y_ref, oc_ref, o_ref, *, nc_ctx, nc_lat):
    for dst, row0, nch in ((oc_ref, 0, nc_ctx), (o_ref, nc_ctx, nc_lat)):
        lane = lax.broadcasted_iota(jnp.int32, (nch, LANES), 1)
        piece = [(lane >= g * SSM_GROUP) & (lane < (g + 1) * SSM_GROUP) for g in range(SSM_SLAB_GROUPS)]
        for i in range(SSM_CHUNK):
            h, ii = divmod(i, SSM_SLAB_GROUPS)
            acc = jnp.zeros((nch, LANES), F32)
            for g in range(SSM_SLAB_GROUPS):
                z = y_ref[g, row0:row0 + nch, h * LANES:(h + 1) * LANES]
                shift = ((g - ii) * SSM_GROUP) % LANES
                acc = jnp.where(piece[g], z if shift == 0 else pltpu.roll(z, shift, 1), acc)
            dst[0, pl.ds(i, nch, stride=SSM_CHUNK), :] = acc


def _ssm_kernel(x_ref, min_ref, mintra_ref, mstate_ref, al_ref, y_ref, v_scr, s_scr, *, nb, nc_ctx, n_chunks):
    xv = x_ref[0]
    v = jnp.dot(xv, min_ref[0], preferred_element_type=F32)
    v_scr[0] = v[:, 0:LANES]
    v_scr[1] = v[:, LANES:]
    ar = al_ref[0, 0:1, :]
    ai = al_ref[0, 1:2, :]
    lo = lax.broadcasted_iota(jnp.int32, (nb, LANES), 1) < HALF

    def body(k, carry):
        sre, sim = carry
        kr = jnp.where(k < nc_ctx, nc_ctx - 1 - k, n_chunks - 1 + nc_ctx - k)
        rf = pl.ds(k, nb, stride=n_chunks)
        rr = pl.ds(kr, nb, stride=n_chunks)
        s_scr[0, rf, :] = sre
        s_scr[1, rr, :] = sre
        s_scr[2, rf, :] = sim
        s_scr[3, rr, :] = sim
        vre = jnp.where(lo, v_scr[0, rf, :], v_scr[0, rr, :])
        vim = jnp.where(lo, v_scr[1, rf, :], v_scr[1, rr, :])
        return ar * sre - ai * sim + vre, ar * sim + ai * sre + vim

    zero = jnp.zeros((nb, LANES), F32)
    lax.fori_loop(0, n_chunks, body, (zero, zero))
    lo_all = lax.broadcasted_iota(jnp.int32, (s_scr.shape[1], LANES), 1) < HALF
    s_in = jnp.concatenate([jnp.where(lo_all, s_scr[0], s_scr[1]), jnp.where(lo_all, s_scr[2], s_scr[3])],
                           axis=1).astype(BF16)
    y_ref[0] = jnp.dot(xv, mintra_ref[0], preferred_element_type=F32) + _nt_dot(s_in, mstate_ref[0])


def _ssm_scan(u, uc, mats):
    m_in, m_intra, m_state, a_l = mats
    b, n, _ = u.shape
    nc = uc.shape[1]
    nc_ctx, nc_lat = nc // SSM_CHUNK, n // SSM_CHUNK
    n_chunks = nc_ctx + nc_lat
    r = n_chunks * b
    nat = lambda m: pl.BlockSpec((1, m, LANES), lambda i, s: (i, 0, s))
    grp = pl.BlockSpec((SSM_SLAB_GROUPS, n_chunks, SSM_CW), lambda i, s: (s, i, 0))
    xg = pl.pallas_call(
        functools.partial(_ssm_pack_kernel, nc_ctx=nc_ctx, nc_lat=nc_lat),
        out_shape=_sds((SSM_GROUPS, r, SSM_CW), BF16),
        grid=(b, SSM_GROUPS // SSM_SLAB_GROUPS),
        in_specs=[nat(nc), nat(n)],
        out_specs=grp,
        compiler_params=_cparams("parallel", "parallel"),
        name="ssm_pack",
    )(uc, u)
    mat = pl.BlockSpec((1, SSM_CW, SSM_CW), lambda g: (g, 0, 0))
    yg = pl.pallas_call(
        functools.partial(_ssm_kernel, nb=b, nc_ctx=nc_ctx, n_chunks=n_chunks),
        out_shape=_sds((SSM_GROUPS, r, SSM_CW), F32),
        grid=(SSM_GROUPS,),
        in_specs=[pl.BlockSpec((1, r, SSM_CW), lambda g: (g, 0, 0)), mat, mat, mat,
                  pl.BlockSpec((1, 2, LANES), lambda g: (g, 0, 0))],
        out_specs=pl.BlockSpec((1, r, SSM_CW), lambda g: (g, 0, 0)),
        scratch_shapes=[pltpu.VMEM((2, r, LANES), F32), pltpu.VMEM((4, r, LANES), F32)],
        compiler_params=_cparams("parallel"),
        name="ssm_scan",
    )(xg, m_in, m_intra, m_state, a_l)
    ysc, ys = pl.pallas_call(
        functools.partial(_ssm_unpack_kernel, nc_ctx=nc_ctx, nc_lat=nc_lat),
        out_shape=(_sds((b, nc, SSM_WIDTH), F32), _sds((b, n, SSM_WIDTH), F32)),
        grid=(b, SSM_GROUPS // SSM_SLAB_GROUPS),
        in_specs=[grp],
        out_specs=(nat(nc), nat(n)),
        compiler_params=_cparams("parallel", "parallel"),
        name="ssm_unpack",
    )(yg)
    return ys, ysc


def _split_rows(ref, val):
    m = val.shape[0]
    for s in range(TOKEN_TILE_ROWS):
        ref[pl.ds(s, m, stride=TOKEN_TILE_ROWS), :] = val[:, s * LANES:(s + 1) * LANES]


def _merge_rows(ref):
    m = ref.shape[0] // TOKEN_TILE_ROWS
    return jnp.concatenate([ref[pl.ds(s, m, stride=TOKEN_TILE_ROWS), :] for s in range(TOKEN_TILE_ROWS)], axis=1)


def _token_tile(ref, t):
    return ref.at[pl.ds(pl.multiple_of(t * TOKEN_TILE_ROWS, TOKEN_TILE_ROWS), TOKEN_TILE_ROWS)]


def _post_kernel(*refs, alpha, with_ssm):
    if with_ssm:
        (att_ref, ys_ref, u_ref, dsk_ref, gw_ref, gb_ref, wo_ref, x_ref, g1_ref, lg_ref, lb_ref, sc2_ref, sh2_ref,
         x1_ref, h2_ref) = refs
        y = ys_ref[0] + u_ref[0] * dsk_ref[...]
        gl = jax.nn.gelu(y)
        gate = jax.nn.sigmoid(jnp.dot(gl.astype(BF16), gw_ref[...], preferred_element_type=F32) + gb_ref[...])
        ssm = (gl * gate).astype(BF16)
        mix = (jnp.dot(att_ref[0], wo_ref[0:SWA_Q_W, :], preferred_element_type=F32)
               + jnp.dot(ssm, wo_ref[SWA_Q_W:, :], preferred_element_type=F32))
    else:
        att_ref, wo_ref, x_ref, g1_ref, lg_ref, lb_ref, sc2_ref, sh2_ref, x1_ref, h2_ref = refs
        mix = jnp.dot(att_ref[0], wo_ref[...], preferred_element_type=F32)
    x1 = _layer_norm(alpha * x_ref[0] + g1_ref[0] * mix, lg_ref[...], lb_ref[...])
    x1_ref[0] = x1
    _split_rows(h2_ref, x1 * (1.0 + sc2_ref[0]) + sh2_ref[0])


def _post(att, ssm_args, w_out_bf16, x, g1, ln_g, ln_b, sc2, sh2, alpha, tm):
    b, n, d = x.shape
    tok = lambda w: pl.BlockSpec((1, tm, w), lambda i, j: (i, j, 0))
    vec = pl.BlockSpec((1, 1, d), lambda i, j: (i, 0, 0))
    const = lambda a: pl.BlockSpec(a.shape, lambda i, j: (0,) * a.ndim)
    in_specs = [tok(att.shape[-1])]
    args = [att]
    if ssm_args is not None:
        ys, u, dsk, gw, gb = ssm_args
        in_specs += [tok(SSM_WIDTH), tok(SSM_WIDTH), const(dsk), const(gw), const(gb)]
        args += [ys, u, dsk, gw, gb]
    in_specs += [const(w_out_bf16), tok(d), vec, const(ln_g), const(ln_b), vec, vec]
    args += [w_out_bf16, x, g1, ln_g, ln_b, sc2, sh2]
    per_b = n // tm
    return pl.pallas_call(
        functools.partial(_post_kernel, alpha=alpha, with_ssm=ssm_args is not None),
        out_shape=(_sds((b, n, d), F32), _sds((b * n * TOKEN_TILE_ROWS, LANES), F32)),
        grid=(b, n // tm),
        in_specs=in_specs,
        out_specs=(tok(d), pl.BlockSpec((tm * TOKEN_TILE_ROWS, LANES), lambda i, j: (i * per_b + j, 0))),
        compiler_params=_cparams("parallel", "parallel"),
        name="post_mixer_ssm" if ssm_args is not None else "post_mixer",
    )(*args)


def _first_max(v, sub):
    m = jnp.max(v, 0, keepdims=True)
    idx = jnp.min(jnp.where(v == m, sub, float(SUBLANES)), 0, keepdims=True)
    return m, idx


def _stream_blocks(toks, tb):
    starts = [0]
    for t in toks:
        starts.append(starts[-1] + t.shape[0] // (tb * TOKEN_TILE_ROWS))
    return starts


def _stream_spec(tok, tb, start):
    last = tok.shape[0] // (tb * TOKEN_TILE_ROWS) - 1
    return pl.BlockSpec((tb * TOKEN_TILE_ROWS, LANES), lambda i, *_: (jnp.clip(i - start, 0, last), 0))


def _router_kernel(*refs, starts):
    n_streams = len(starts) - 1
    tok_refs = refs[:n_streams]
    w_ref, b_ref, ids_ref, wts_ref, rank_ref, cnt_ref, carry_scr = refs[n_streams:]
    step = pl.program_id(0)

    @pl.when(step == 0)
    def _():
        carry_scr[...] = jnp.zeros_like(carry_scr)

    h = _merge_rows(tok_refs[0])
    for ref, start in zip(tok_refs[1:], starts[1:]):
        h = jnp.where(step >= start, _merge_rows(ref), h)
    w = w_ref[...]
    w_hi, h_hi = w.astype(BF16), h.astype(BF16)
    w_lo, h_lo = (w - w_hi.astype(F32)).astype(BF16), (h - h_hi.astype(F32)).astype(BF16)
    logits = _nt_dot(w_hi, h_hi) + (_nt_dot(w_hi, h_lo) + _nt_dot(w_lo, h_hi)) + b_ref[...]
    tm = logits.shape[1]
    sub = lax.broadcasted_iota(jnp.int32, (SUBLANES, tm), 0).astype(F32)
    gl = logits[0:SUBLANES]
    gmax, gi = _first_max(gl, sub)
    gp = 1.0 / jnp.sum(jnp.exp(gl - gmax), 0, keepdims=True)
    le = logits[ROUTER_EXPERT_ROW0:ROUTER_EXPERT_ROW0 + MOE_EPG]
    for g in range(1, MOE_GROUPS):
        le = jnp.where(gi == float(g), logits[ROUTER_EXPERT_ROW0 + g * MOE_EPG:ROUTER_EXPERT_ROW0 + (g + 1) * MOE_EPG], le)
    m1, i1 = _first_max(le, sub)
    m2, i2 = _first_max(jnp.where(sub == i1, NEG_INF, le), sub)
    t = jnp.exp(m2 - m1)
    e1 = gi * float(MOE_EPG) + i1
    e2 = gi * float(MOE_EPG) + i2
    ids_ref[0:1, :] = e1.astype(jnp.int32)
    ids_ref[1:2, :] = e2.astype(jnp.int32)
    wts_ref[0:1, :] = gp / (1.0 + t)
    wts_ref[1:2, :] = gp * t / (1.0 + t)
    esub = lax.broadcasted_iota(jnp.int32, (MOE_EXPERTS, tm), 0).astype(F32)
    oh1 = (esub == e1).astype(F32)
    oh2 = (esub == e2).astype(F32)
    both = oh1 + oh2
    earlier = (lax.broadcasted_iota(jnp.int32, (tm, tm), 0) < lax.broadcasted_iota(jnp.int32, (tm, tm), 1))
    prefix = jnp.dot(both.astype(BF16), earlier.astype(BF16), preferred_element_type=F32) + carry_scr[...]
    rank_ref[0:1, :] = jnp.sum(oh1 * prefix, 0, keepdims=True).astype(jnp.int32)
    rank_ref[1:2, :] = jnp.sum(oh2 * prefix, 0, keepdims=True).astype(jnp.int32)
    carry_scr[...] += jnp.sum(both, 1, keepdims=True)
    cnt_ref[...] = jnp.broadcast_to(carry_scr[...], cnt_ref.shape)


def _router(toks, wg, bg, we, be, tm):
    starts = _stream_blocks(toks, tm)
    t = starts[-1] * tm
    d = D_MODEL
    rows = ROUTER_EXPERT_ROW0 + MOE_EXPERTS
    w = jnp.zeros((rows, d), F32)
    w = w.at[:MOE_GROUPS].set(wg.T)
    w = w.at[ROUTER_EXPERT_ROW0:].set(jnp.transpose(we, (0, 2, 1)).reshape(MOE_EXPERTS, d))
    bias = jnp.full((rows, 1), NEG_INF, F32)
    bias = bias.at[:MOE_GROUPS, 0].set(bg)
    bias = bias.at[ROUTER_EXPERT_ROW0:, 0].set(be.reshape(-1))
    pair = pl.BlockSpec((MOE_TOPK, tm), lambda i: (0, i))
    ids, wts, rank, cnt = pl.pallas_call(
        functools.partial(_router_kernel, starts=starts),
        out_shape=(_sds((MOE_TOPK, t), jnp.int32), _sds((MOE_TOPK, t), F32), _sds((MOE_TOPK, t), jnp.int32),
                   _sds((MOE_EXPERTS, LANES), F32)),
        grid=(t // tm,),
        in_specs=[_stream_spec(tok, tm, s0) for tok, s0 in zip(toks, starts)]
        + [pl.BlockSpec((rows, d), lambda i: (0, 0)), pl.BlockSpec((rows, 1), lambda i: (0, 0))],
        out_specs=(pair, pair, pair, pl.BlockSpec((MOE_EXPERTS, LANES), lambda i: (0, 0))),
        scratch_shapes=[pltpu.VMEM((MOE_EXPERTS, 1), F32)],
        compiler_params=_cparams("arbitrary"),
        name="moe_router",
    )(*toks, w, bias)
    return ids, wts, rank, cnt[:, 0].astype(jnp.int32)


def _moe_plan(ids, rank, counts, n_tok):
    tm = MOE_ROW_TILE
    padded = ((counts + tm - 1) // tm) * tm
    ends = jnp.cumsum(padded)
    offs = ends - padded
    experts = jnp.arange(MOE_EXPERTS, dtype=jnp.int32)
    dest = (jnp.sum(jnp.where(ids[..., None] == experts, offs, 0), -1) + rank).astype(jnp.int32)
    n_tiles = (MOE_TOPK * n_tok + MOE_EXPERTS * (tm - 1)) // tm
    starts = jnp.arange(n_tiles, dtype=jnp.int32) * tm
    tile_expert = jnp.minimum(jnp.sum((ends[None, :] <= starts[:, None]).astype(jnp.int32), -1), MOE_EXPERTS - 1)
    n_valid = (ends[-1] // tm).astype(jnp.int32).reshape(1)
    pad = jnp.stack([offs + counts, padded - counts, jnp.broadcast_to(n_valid, counts.shape)]).astype(jnp.int32)
    return dest, tile_expert, n_valid, pad, n_tiles * tm


def _pad_fill(pad_ref, zero_scr, xs_ref, sem, wait):
    def per_expert(e, carry):
        first, count = pad_ref[0, e], pad_ref[1, e]
        piece = MOE_ROW_TILE // 2
        while piece >= 1:
            row = first + (count & ~(2 * piece - 1))
            n = piece * TOKEN_TILE_ROWS

            @pl.when((count & piece) != 0)
            def _(row=row, n=n):
                copy = pltpu.make_async_copy(
                    zero_scr.at[pl.ds(0, n)],
                    xs_ref.at[pl.ds(pl.multiple_of(row * TOKEN_TILE_ROWS, TOKEN_TILE_ROWS), n)], sem)
                copy.wait() if wait else copy.start()

            piece //= 2
        return carry

    lax.fori_loop(0, MOE_EXPERTS, per_expert, 0)
    half = MOE_ROW_TILE // 2 * TOKEN_TILE_ROWS
    n_tiles = xs_ref.shape[0] // (2 * half)

    def per_tile(t, carry):
        for h in range(2):
            copy = pltpu.make_async_copy(zero_scr, xs_ref.at[pl.ds(pl.multiple_of((2 * t + h) * half, half), half)],
                                         sem)
            copy.wait() if wait else copy.start()
        return carry

    lax.fori_loop(pad_ref[2, 0], n_tiles, per_tile, 0)


def _dispatch_kernel(*refs, tb, starts):
    n_streams = len(starts) - 1
    dest_ref, pad_ref = refs[:2]
    tok_refs = refs[2:2 + n_streams]
    xs_ref, zero_scr, sem, pad_sem = refs[2 + n_streams:]
    step = pl.program_id(0)

    @pl.when(step == 0)
    def _():
        zero_scr[...] = jnp.zeros_like(zero_scr)
        _pad_fill(pad_ref, zero_scr, xs_ref, pad_sem, wait=False)

    for s, tok_ref in enumerate(tok_refs):
        @pl.when((step >= starts[s]) & (step < starts[s + 1]))
        def _(tok_ref=tok_ref):
            def body(r, carry):
                for k in range(MOE_TOPK):
                    pltpu.make_async_copy(_token_tile(tok_ref, r), _token_tile(xs_ref, dest_ref[0, k, r]), sem).start()
                return carry

            lax.fori_loop(0, tb, body, 0, unroll=8)

    for k in range(MOE_TOPK):
        pltpu.make_async_copy(tok_refs[0], xs_ref.at[pl.ds(0, tb * TOKEN_TILE_ROWS)], sem).wait()

    @pl.when(step == 0)
    def _():
        _pad_fill(pad_ref, zero_scr, xs_ref, pad_sem, wait=True)


def _dispatch(toks, dest3, pad, n_rows):
    nblk, _, tb = dest3.shape
    starts = _stream_blocks(toks, tb)
    return pl.pallas_call(
        functools.partial(_dispatch_kernel, tb=tb, starts=starts),
        out_shape=_sds((n_rows * TOKEN_TILE_ROWS, LANES), F32),
        grid=(nblk,),
        in_specs=[pl.BlockSpec((1, MOE_TOPK, tb), lambda i: (i, 0, 0), memory_space=pltpu.SMEM),
                  pl.BlockSpec(memory_space=pltpu.SMEM)]
        + [_stream_spec(tok, tb, s0) for tok, s0 in zip(toks, starts)],
        out_specs=pl.BlockSpec(memory_space=pl.ANY),
        scratch_shapes=[pltpu.VMEM((MOE_ROW_TILE // 2 * TOKEN_TILE_ROWS, LANES), F32),
                        pltpu.SemaphoreType.DMA(()), pltpu.SemaphoreType.DMA(())],
        compiler_params=_cparams("arbitrary"),
        name="moe_dispatch",
    )(dest3, pad, *toks)


def _ffn_kernel(te_ref, nv_ref, x_ref, w1_ref, w3_ref, w2_ref, y_ref, w13_scr, w2_scr):
    i = pl.program_id(0)
    f = w1_ref.shape[2]

    @pl.when((i == 0) | (te_ref[i] != te_ref[jnp.maximum(i - 1, 0)]))
    def _():
        w13_scr[:, 0:f] = w1_ref[0].astype(BF16)
        w13_scr[:, f:2 * f] = w3_ref[0].astype(BF16)
        w2_scr[...] = w2_ref[0].astype(BF16)

    @pl.when(i < nv_ref[0])
    def _():
        h13 = jnp.dot(_merge_rows(x_ref).astype(BF16), w13_scr[...], preferred_element_type=F32)
        h1 = h13[:, 0:f]
        hh = (h1 * jax.nn.sigmoid(h1) * h13[:, f:2 * f]).astype(BF16)
        _split_rows(y_ref, jnp.dot(hh, w2_scr[...], preferred_element_type=F32))

    @pl.when(i >= nv_ref[0])
    def _():
        y_ref[...] = jnp.zeros_like(y_ref)


def _expert_ffn(tile_expert, n_valid, xs, w1, w3, w2):
    p = xs.shape[0] // TOKEN_TILE_ROWS
    _, d, f = w1.shape
    tm = MOE_ROW_TILE
    rows = pl.BlockSpec((tm * TOKEN_TILE_ROWS, LANES), lambda i, te, nv: (i, 0))
    rows_in = pl.BlockSpec((tm * TOKEN_TILE_ROWS, LANES), lambda i, te, nv: (jnp.minimum(i, nv[0] - 1), 0))
    return pl.pallas_call(
        _ffn_kernel,
        out_shape=_sds(xs.shape, F32),
        grid_spec=pltpu.PrefetchScalarGridSpec(
            num_scalar_prefetch=2,
            grid=(p // tm,),
            in_specs=[rows_in,
                      pl.BlockSpec((1, d, f), lambda i, te, nv: (te[i], 0, 0)),
                      pl.BlockSpec((1, d, f), lambda i, te, nv: (te[i], 0, 0)),
                      pl.BlockSpec((1, f, d), lambda i, te, nv: (te[i], 0, 0))],
            out_specs=rows,
            scratch_shapes=[pltpu.VMEM((d, 2 * f), BF16), pltpu.VMEM((f, d), BF16)]),
        compiler_params=_cparams("arbitrary"),
        name="moe_expert_ffn",
    )(tile_expert, n_valid, xs, w1, w3, w2)


def _combine_ln2_kernel(dest_ref, next_ref, x1_ref, ys_ref, wt_ref, g2_ref, lg_ref, lb_ref, o_ref, buf, sem, *,
                        alpha, tb, nblk):
    step = pl.program_id(0)

    def gather(d_ref, slot):
        def body(r, carry):
            for k in range(MOE_TOPK):
                pltpu.make_async_copy(_token_tile(ys_ref, d_ref[0, k, r]), _token_tile(buf.at[slot, k], r),
                                      sem.at[slot, k]).start()
            return carry

        lax.fori_loop(0, tb, body, 0, unroll=8)

    @pl.when(step == 0)
    def _():
        gather(dest_ref, 0)

    @pl.when(step + 1 < nblk)
    def _():
        gather(next_ref, (step + 1) % 2)

    slot = step % 2
    for k in range(MOE_TOPK):
        pltpu.make_async_copy(ys_ref.at[pl.ds(0, tb * TOKEN_TILE_ROWS)], buf.at[slot, k], sem.at[slot, k]).wait()
    f = wt_ref[:, 0:1] * _merge_rows(buf.at[slot, 0]) + wt_ref[:, 1:2] * _merge_rows(buf.at[slot, 1])
    o_ref[...] = _layer_norm(alpha * x1_ref[...] + g2_ref[0] * f, lg_ref[...], lb_ref[...])


def _combine_ln2(dest3, x1, ys, wt, g2, ln_g, ln_b, alpha, n_per_sample):
    t, d = x1.shape
    nblk, _, tb = dest3.shape
    const = pl.BlockSpec((1, d), lambda i: (0, 0))
    return pl.pallas_call(
        functools.partial(_combine_ln2_kernel, alpha=alpha, tb=tb, nblk=nblk),
        out_shape=_sds((t, d), F32),
        grid=(nblk,),
        in_specs=[pl.BlockSpec((1, MOE_TOPK, tb), lambda i: (i, 0, 0), memory_space=pltpu.SMEM),
                  pl.BlockSpec((1, MOE_TOPK, tb), lambda i: (jnp.minimum(i + 1, nblk - 1), 0, 0),
                               memory_space=pltpu.SMEM),
                  pl.BlockSpec((tb, d), lambda i: (i, 0)),
                  pl.BlockSpec(memory_space=pl.ANY),
                  pl.BlockSpec((tb, MOE_TOPK), lambda i: (i, 0)),
                  pl.BlockSpec((1, 1, d), lambda i: ((i * tb) // n_per_sample, 0, 0)),
                  const, const],
        out_specs=pl.BlockSpec((tb, d), lambda i: (i, 0)),
        scratch_shapes=[pltpu.VMEM((2, MOE_TOPK, tb * TOKEN_TILE_ROWS, LANES), F32),
                        pltpu.SemaphoreType.DMA((2, MOE_TOPK))],
        compiler_params=_cparams("arbitrary"),
        name="moe_combine_ln2",
    )(dest3, dest3, x1, ys, wt, g2, ln_g, ln_b)


def _proj1_kernel(*refs, rope, with_q):
    x_ref, sc_ref, sh_ref, w_ref = refs[:4]
    refs = refs[4:]
    tabs = None
    if rope:
        tabs = (refs[0][...], refs[1][...], refs[2][...])
        refs = refs[3:]
    h = (x_ref[0] * (1.0 + sc_ref[0]) + sh_ref[0]).astype(BF16)
    r = jnp.dot(h, w_ref[...], preferred_element_type=F32)
    off = 0
    if with_q:
        q_ref, k_ref, v_ref = refs
        for hd in range(DIF_HEADS):
            q_ref[0, hd] = (_rot(r[:, hd * LANES:(hd + 1) * LANES], tabs) * (LOG2_E * HEAD_DIM ** -0.5)).astype(BF16)
        off = DIF_QK_W
    else:
        k_ref, v_ref = refs
    for hd in range(DIF_HEADS):
        k_ref[0, hd] = _rot(r[:, off + hd * LANES:off + (hd + 1) * LANES], tabs).astype(BF16)
        v_ref[0, hd] = r[:, off + DIF_QK_W + hd * LANES:off + DIF_QK_W + (hd + 1) * LANES].astype(BF16)


def _proj1(x, sc, sh, w_bf16, tabs, with_q, tm):
    b, n, d = x.shape
    rope = tabs is not None
    vec = pl.BlockSpec((1, 1, d), lambda i, j: (i, 0, 0))
    in_specs = [pl.BlockSpec((1, tm, d), lambda i, j: (i, j, 0)), vec, vec,
                pl.BlockSpec(w_bf16.shape, lambda i, j: (0, 0))]
    args = [x, sc, sh, w_bf16]
    if rope:
        in_specs += [pl.BlockSpec((tm, LANES), lambda i, j: (j, 0))] * 3
        args += list(tabs)
    hm = pl.BlockSpec((1, DIF_HEADS, tm, LANES), lambda i, j: (i, 0, j, 0))
    n_out = 3 if with_q else 2
    return pl.pallas_call(
        functools.partial(_proj1_kernel, rope=rope, with_q=with_q),
        out_shape=(_sds((b, DIF_HEADS, n, LANES), BF16),) * n_out,
        grid=(b, n // tm),
        in_specs=in_specs,
        out_specs=(hm,) * n_out,
        compiler_params=_cparams("parallel", "parallel"),
        name="proj1_qkv" if with_q else "proj1_kv_ctx",
    )(*args)


def _diff_kernel(lam_ref, q_ref, kl_ref, kc_ref, vl_ref, vc_ref, g_ref, o_ref, k_scr, v_scr, *, tq, n_lat, out_scale):
    @pl.when(pl.program_id(2) == 0)
    def _():
        k_scr[0:n_lat] = kl_ref[0, 0]
        k_scr[n_lat:] = kc_ref[0, 0]
        v_scr[0:n_lat, 0:LANES] = vl_ref[0, 0]
        v_scr[n_lat:, 0:LANES] = vc_ref[0, 0]
        v_scr[:, LANES:] = jnp.ones((v_scr.shape[0], LANES), BF16)

    lo = lax.broadcasted_iota(jnp.int32, (DIF_CHAIN_ROWS, LANES), 1) < HALF
    for r0 in range(0, tq, DIF_CHAIN_ROWS):
        q = q_ref[0, 0, r0:r0 + DIF_CHAIN_ROWS, :].astype(F32)
        maps = []
        for qm in (jnp.where(lo, q, 0.0), jnp.where(lo, 0.0, q)):
            s = _nt_dot(qm.astype(BF16), k_scr[...])
            p = jnp.exp2(s - jnp.max(s, -1, keepdims=True)).astype(BF16)
            oe = jnp.dot(p, v_scr[...], preferred_element_type=F32)
            maps.append(oe[:, 0:LANES] * (1.0 / oe[:, LANES:]))
        o = maps[0] - lam_ref[0] * maps[1]
        o = o * lax.rsqrt(jnp.mean(o * o, -1, keepdims=True) + RMS_EPS) * g_ref[...]
        o_ref[0, r0:r0 + DIF_CHAIN_ROWS, :] = (o * out_scale).astype(BF16)


def _diff_attention(lam, q, kl, kc, vl, vc, subln_g, lam_init, tq):
    b, nh, n, _ = q.shape
    nc = kc.shape[2]
    kv = lambda m: pl.BlockSpec((1, 1, m, LANES), lambda i, h, j: (i, h, 0, 0))
    return pl.pallas_call(
        functools.partial(_diff_kernel, tq=tq, n_lat=n, out_scale=1.0 - lam_init),
        out_shape=_sds((b, n, nh * LANES), BF16),
        grid=(b, nh, n // tq),
        in_specs=[pl.BlockSpec(memory_space=pltpu.SMEM),
                  pl.BlockSpec((1, 1, tq, LANES), lambda i, h, j: (i, h, j, 0)),
                  kv(n), kv(nc), kv(n), kv(nc),
                  pl.BlockSpec((1, LANES), lambda i, h, j: (0, 0))],
        out_specs=pl.BlockSpec((1, tq, LANES), lambda i, h, j: (i, j, h)),
        scratch_shapes=[pltpu.VMEM((n + nc, LANES), BF16), pltpu.VMEM((n + nc, 2 * LANES), BF16)],
        compiler_params=_cparams("parallel", "parallel", "arbitrary"),
        name="diff_attention",
    )(lam, q, kl, kc, vl, vc, subln_g)


def _moe_block(toks, layer, moe_wg, moe_bg, moe_we, moe_be, moe_w1, moe_w3, moe_w2):
    t = sum(tok.shape[0] for tok in toks) // TOKEN_TILE_ROWS
    tb = MOE_DMA_TILE
    ids, wts, rank, counts = _router(toks, moe_wg[layer], moe_bg[layer], moe_we[layer], moe_be[layer], tm=512)
    dest, tile_expert, n_valid, pad, n_rows = _moe_plan(ids, rank, counts, t)
    dest3 = dest.reshape(MOE_TOPK, t // tb, tb).transpose(1, 0, 2)
    xs = _dispatch(toks, dest3, pad, n_rows)
    flat = lambda w: w.reshape((-1,) + w.shape[2:])
    ys = _expert_ffn(tile_expert + layer * MOE_EXPERTS, n_valid, xs, flat(moe_w1), flat(moe_w3), flat(moe_w2))
    return ys, dest3, wts.T


def kernel(x, c, ctx, c_ctx, mod_w, mod_b, ln1_g, ln1_b, ln2_g, ln2_b, swa_ssm_w_in, swa_ssm_w_out, swa_sink, ssm_a_re, ssm_a_im, ssm_log_step, ssm_b_re, ssm_b_im, ssm_c_re, ssm_c_im, ssm_d, ssm_glu_w, ssm_glu_b, dif_w_in, dif_w_out, dif_lam_q1, dif_lam_k1, dif_lam_q2, dif_lam_k2, dif_subln_g, moe_wg, moe_bg, moe_we, moe_be, moe_w1, moe_w3, moe_w2):
    bsz, n, d = x.shape
    ctx_len = ctx.shape[1]
    depth = mod_w.shape[0]
    alpha = (2 * depth) ** 0.25
    tabs = _rope_tables(n)

    n_vec = 16
    cvec = jnp.zeros((n_vec, d), F32).at[:bsz].set(c).at[bsz].set(c_ctx)
    mods = _modulation(cvec, mod_w, mod_b)

    xl, xc = x, ctx
    for layer in range(depth):
        need_ctx = layer < depth - 1
        i = layer // 2
        lat = [mods[layer, :bsz, k * d:(k + 1) * d].reshape(bsz, 1, d) for k in range(6)]
        cx = [jnp.broadcast_to(mods[layer, bsz, k * d:(k + 1) * d].reshape(1, 1, d), (bsz, 1, d)) for k in range(6)]
        sh1, sc1, g1, sh2, sc2, g2 = lat
        csh1, csc1, cg1, csh2, csc2, cg2 = cx
        lg1, lb1 = ln1_g[layer].reshape(1, d), ln1_b[layer].reshape(1, d)
        lg2, lb2 = ln2_g[layer].reshape(1, d), ln2_b[layer].reshape(1, d)
        if layer % 2 == 0:
            w_in = swa_ssm_w_in[i].astype(BF16)
            w_out = swa_ssm_w_out[i].astype(BF16)
            q, k, v, u = _proj0(xl, sc1, sh1, w_in, tabs, tm=512)
            qc, kc, vc, uc = _proj0(xc, csc1, csh1, w_in, None, tm=ctx_len)
            sink = swa_sink[i].astype(F32)
            att = _swa_attention(sink, q, k, v, kc, vc)
            mats = _ssm_matrices(ssm_a_re[i], ssm_a_im[i], ssm_log_step[i], ssm_b_re[i], ssm_b_im[i],
                                 ssm_c_re[i], ssm_c_im[i])
            ys, ysc = _ssm_scan(u, uc, mats)
            glu = (ssm_d[i].reshape(1, SSM_WIDTH).astype(F32), ssm_glu_w[i].astype(BF16),
                   ssm_glu_b[i].reshape(1, SSM_WIDTH).astype(F32))
            x1, tok = _post(att, (ys, u) + glu, w_out, xl, g1, lg1, lb1, sc2, sh2, alpha, 512)
            toks = [tok]
            if need_ctx:
                att_c = _ctx_attention(sink, qc, kc, vc)
                xc1, tok_c = _post(att_c, (ysc, uc) + glu, w_out, xc, cg1, lg1, lb1, csc2, csh2, alpha, ctx_len)
                toks.append(tok_c)
        else:
            lam_init = 0.8 - 0.6 * math.exp(-0.3 * layer)
            w_in = dif_w_in[i].astype(BF16)
            w_out = dif_w_out[i].astype(BF16)
            q, k, v = _proj1(xl, sc1, sh1, w_in, tabs, True, tm=512)
            kc, vc = _proj1(xc, csc1, csh1, w_in[:, DIF_QK_W:], None, False, tm=ctx_len)
            lam = (jnp.exp(jnp.sum(dif_lam_q1[i].astype(F32) * dif_lam_k1[i].astype(F32)))
                   - jnp.exp(jnp.sum(dif_lam_q2[i].astype(F32) * dif_lam_k2[i].astype(F32))) + lam_init).reshape(1)
            att = _diff_attention(lam, q, k, kc, v, vc, dif_subln_g[i].reshape(1, DIF_V_HEAD).astype(F32),
                                  lam_init, tq=n)
            x1, tok = _post(att, None, w_out, xl, g1, lg1, lb1, sc2, sh2, alpha, 512)
            toks = [tok]
            if need_ctx:
                raise NotImplementedError("a differential-attention layer followed by another layer")
        ys_moe, dest3, wt = _moe_block(toks, layer, moe_wg, moe_bg, moe_we, moe_be, moe_w1, moe_w3, moe_w2)
        n_lat_blk = bsz * n // MOE_DMA_TILE
        xl = _combine_ln2(dest3[:n_lat_blk], x1.reshape(-1, d), ys_moe, wt[:bsz * n], g2, lg2, lb2, alpha,
                          n).reshape(bsz, n, d)
        if need_ctx:
            xc = _combine_ln2(dest3[n_lat_blk:], xc1.reshape(-1, d), ys_moe, wt[bsz * n:], cg2, lg2, lb2, alpha,
                              ctx_len).reshape(bsz, ctx_len, d)
    return xl
```

```python
import functools
import math

import jax
import jax.numpy as jnp
from jax import lax
from jax.experimental import pallas as pl
from jax.experimental.pallas import tpu as pltpu

F32 = jnp.float32
BF16 = jnp.bfloat16
HIGHEST = lax.Precision.HIGHEST

D_MODEL = 1024
GRID_W = 64
HEAD_DIM = 64
ROPE_BASE = 10000.0
ROPE_FREQS = HEAD_DIM // 4
LN_EPS = 1e-5
RMS_EPS = 1e-5
NEG_INF = -1e30
LOG2_E = math.log2(math.e)
LANES = 128
HALF = LANES // 2

SWA_HEADS = 8
SWA_KV_HEADS = 2
SWA_WINDOW = 128
SWA_BLOCK = 128
SWA_BLOCKS_PER_STEP = 4
SWA_Q_W = SWA_HEADS * HEAD_DIM
SWA_KV_W = SWA_KV_HEADS * HEAD_DIM

SSM_WIDTH = D_MODEL // 2
SSM_GROUP = 16
SSM_GROUPS = SSM_WIDTH // SSM_GROUP
SSM_STATE = 64
SSM_CHUNK = 16
SSM_CW = SSM_CHUNK * SSM_GROUP
SSM_SLAB_GROUPS = LANES // SSM_GROUP

AB_IN_W = SWA_Q_W + 2 * SWA_KV_W + SSM_WIDTH

DIF_HEADS = D_MODEL // (2 * HEAD_DIM)
DIF_QK_W = DIF_HEADS * 2 * HEAD_DIM
DIF_V_HEAD = 2 * HEAD_DIM
DIF_V_W = DIF_HEADS * DIF_V_HEAD
DIF_CHAIN_ROWS = 128

MOE_GROUPS = 4
MOE_EPG = 8
MOE_EXPERTS = MOE_GROUPS * MOE_EPG
MOE_HIDDEN = D_MODEL // 4
MOE_TOPK = 2
SUBLANES = 8
ROUTER_EXPERT_ROW0 = SUBLANES
MOE_ROW_TILE = 512
MOE_DMA_TILE = 512
TOKEN_TILE_ROWS = D_MODEL // LANES

VMEM_LIMIT = 56 * 1024 * 1024


def _cparams(*sem):
    return pltpu.CompilerParams(dimension_semantics=sem, vmem_limit_bytes=VMEM_LIMIT)


def _sds(shape, dtype):
    return jax.ShapeDtypeStruct(shape, dtype)


def _nt_dot(a, b):
    return lax.dot_general(a, b, (((1,), (1,)), ((), ())), preferred_element_type=F32)


def _layer_norm(r, g, b):
    mu = jnp.mean(r, -1, keepdims=True)
    rc = r - mu
    var = jnp.mean(rc * rc, -1, keepdims=True)
    return rc * lax.rsqrt(var + LN_EPS) * g + b


def _mod_kernel(c_ref, w_ref, b_ref, o_ref):
    cv = c_ref[...]
    s = cv * jax.nn.sigmoid(cv)
    o_ref[0] = jnp.dot(s, w_ref[0], preferred_element_type=F32, precision=HIGHEST) + b_ref[0]


def _modulation(cvec, mod_w, mod_b):
    depth, d, w6 = mod_w.shape
    tn = 1536
    return pl.pallas_call(
        _mod_kernel,
        out_shape=_sds((depth, cvec.shape[0], w6), F32),
        grid=(depth, w6 // tn),
        in_specs=[pl.BlockSpec(cvec.shape, lambda l, j: (0, 0)),
                  pl.BlockSpec((1, d, tn), lambda l, j: (l, 0, j)),
                  pl.BlockSpec((1, 1, tn), lambda l, j: (l, 0, j))],
        out_specs=pl.BlockSpec((1, cvec.shape[0], tn), lambda l, j: (l, 0, j)),
        compiler_params=_cparams("arbitrary", "arbitrary"),
        name="modulation",
    )(cvec, mod_w, mod_b.reshape(depth, 1, w6))


def _rope_tables(n):
    rows = n // GRID_W
    row = jnp.repeat(jnp.arange(rows, dtype=F32), GRID_W)
    col = jnp.tile(jnp.arange(GRID_W, dtype=F32), rows)
    inv = ROPE_BASE ** (-jnp.arange(ROPE_FREQS, dtype=F32) / ROPE_FREQS)
    ang_r = row[:, None] * inv[None, :]
    ang_c = col[:, None] * inv[None, :]
    zeros = jnp.zeros_like(ang_r)
    cos64 = jnp.concatenate([jnp.cos(ang_r), jnp.cos(ang_r), jnp.cos(ang_c), jnp.cos(ang_c)], -1)
    sa64 = jnp.concatenate([-jnp.sin(ang_r), zeros, -jnp.sin(ang_c), zeros], -1)
    sb64 = jnp.concatenate([zeros, jnp.sin(ang_r), zeros, jnp.sin(ang_c)], -1)
    return tuple(jnp.tile(t, (1, LANES // HEAD_DIM)) for t in (cos64, sa64, sb64))


def _rot(t, tabs):
    if tabs is None:
        return t
    cos, sa, sb = tabs
    return t * cos + pltpu.roll(t, LANES - ROPE_FREQS, 1) * sa + pltpu.roll(t, ROPE_FREQS, 1) * sb


def _dup_halves(t):
    lo = lax.broadcasted_iota(jnp.int32, t.shape, 1) < HALF
    ta = jnp.where(lo, t, 0.0)
    tb = t - ta
    return ta + pltpu.roll(ta, HALF, 1), tb + pltpu.roll(tb, HALF, 1)


def _proj0_kernel(*refs, rope):
    if rope:
        x_ref, sc_ref, sh_ref, w_ref, cos_ref, sa_ref, sb_ref, q_ref, k_ref, v_ref, u_ref = refs
        tabs = (cos_ref[...], sa_ref[...], sb_ref[...])
    else:
        x_ref, sc_ref, sh_ref, w_ref, q_ref, k_ref, v_ref, u_ref = refs
        tabs = None
    h = (x_ref[0] * (1.0 + sc_ref[0]) + sh_ref[0]).astype(BF16)
    r = jnp.dot(h, w_ref[...], preferred_element_type=F32)
    scale = LOG2_E * HEAD_DIM ** -0.5
    for s in range(SWA_Q_W // LANES):
        q_ref[0, :, s * LANES:(s + 1) * LANES] = (_rot(r[:, s * LANES:(s + 1) * LANES], tabs) * scale).astype(BF16)
    k0, k1 = _dup_halves(_rot(r[:, SWA_Q_W:SWA_Q_W + LANES], tabs))
    vv = r[:, SWA_Q_W + LANES:SWA_Q_W + 2 * LANES]
    lo = lax.broadcasted_iota(jnp.int32, vv.shape, 1) < HALF
    k_ref[0, 0] = k0.astype(BF16)
    k_ref[0, 1] = k1.astype(BF16)
    v_ref[0, 0] = jnp.where(lo, vv, 1.0).astype(BF16)
    v_ref[0, 1] = jnp.where(lo, pltpu.roll(vv, HALF, 1), 1.0).astype(BF16)
    u_ref[0] = r[:, SWA_Q_W + 2 * LANES:]


def _proj0(x, sc, sh, w_bf16, tabs, tm):
    b, n, d = x.shape
    rope = tabs is not None
    vec = pl.BlockSpec((1, 1, d), lambda i, j: (i, 0, 0))
    in_specs = [pl.BlockSpec((1, tm, d), lambda i, j: (i, j, 0)), vec, vec,
                pl.BlockSpec(w_bf16.shape, lambda i, j: (0, 0))]
    args = [x, sc, sh, w_bf16]
    if rope:
        in_specs += [pl.BlockSpec((tm, LANES), lambda i, j: (j, 0))] * 3
        args += list(tabs)
    kv_spec = pl.BlockSpec((1, SWA_KV_HEADS, tm, LANES), lambda i, j: (i, 0, j, 0))
    return pl.pallas_call(
        functools.partial(_proj0_kernel, rope=rope),
        out_shape=(_sds((b, n, SWA_Q_W), BF16), _sds((b, SWA_KV_HEADS, n, LANES), BF16),
                   _sds((b, SWA_KV_HEADS, n, LANES), BF16), _sds((b, n, SSM_WIDTH), F32)),
        grid=(b, n // tm),
        in_specs=in_specs,
        out_specs=(pl.BlockSpec((1, tm, SWA_Q_W), lambda i, j: (i, j, 0)), kv_spec, kv_spec,
                   pl.BlockSpec((1, tm, SSM_WIDTH), lambda i, j: (i, j, 0))),
        compiler_params=_cparams("parallel", "parallel"),
        name="proj0_rope" if rope else "proj0_ctx",
    )(*args)


def _swa_kernel(*refs, tq, nsub, local, n_lat):
    if local:
        sink_ref, q_ref, k_ref, v_ref, kc_ref, vc_ref, o_ref = refs
    else:
        sink_ref, q_ref, kc_ref, vc_ref, o_ref = refs
    rows = 4 * tq
    lo = lax.broadcasted_iota(jnp.int32, (tq, LANES), 1) < HALF
    lo4 = lax.broadcasted_iota(jnp.int32, (rows, LANES), 1) < HALF
    rown = lax.broadcasted_iota(jnp.int32, (rows, 1), 0)
    for sub in range(nsub):
        j = pl.program_id(1) * nsub + sub
        r0 = sub * tq
        if local:
            span = 3 * SWA_BLOCK
            start = pl.multiple_of(jnp.clip((j - 1) * SWA_BLOCK, 0, n_lat - span), SWA_BLOCK)
            rr = lax.broadcasted_iota(jnp.int32, (rows, span), 0)
            cc = lax.broadcasted_iota(jnp.int32, (rows, span), 1)
            qpos = j * tq + (rr & (tq - 1))
            mask = jnp.abs(qpos - (start + cc)) <= SWA_WINDOW
        for h in range(SWA_KV_HEADS):
            qa = q_ref[0, r0:r0 + tq, (2 * h) * LANES:(2 * h + 1) * LANES].astype(F32)
            qb = q_ref[0, r0:r0 + tq, (2 * h + 1) * LANES:(2 * h + 2) * LANES].astype(F32)
            q4 = jnp.concatenate([jnp.where(lo, qa, 0.0), jnp.where(lo, 0.0, qa),
                                  jnp.where(lo, qb, 0.0), jnp.where(lo, 0.0, qb)], 0).astype(BF16)
            sink = LOG2_E * jnp.where(rown < tq, sink_ref[4 * h],
                                      jnp.where(rown < 2 * tq, sink_ref[4 * h + 1],
                                                jnp.where(rown < 3 * tq, sink_ref[4 * h + 2], sink_ref[4 * h + 3])))
            s_ctx = _nt_dot(q4, kc_ref[0, h])
            m = jnp.maximum(jnp.max(s_ctx, -1, keepdims=True), sink)
            if local:
                s_loc = jnp.where(mask, _nt_dot(q4, k_ref[0, h, pl.ds(start, span), :]), NEG_INF)
                m = jnp.maximum(m, jnp.max(s_loc, -1, keepdims=True))
            o4 = jnp.dot(jnp.exp2(s_ctx - m).astype(BF16), vc_ref[0, h], preferred_element_type=F32)
            if local:
                o4 = o4 + jnp.dot(jnp.exp2(s_loc - m).astype(BF16), v_ref[0, h, pl.ds(start, span), :],
                                  preferred_element_type=F32)
            o4 = o4 + jnp.where(lo4, 0.0, jnp.exp2(sink - m))
            o4 = o4 * (1.0 / jnp.where(lo4, pltpu.roll(o4, HALF, 1), 1.0))
            for s in range(2):
                even, odd = o4[2 * s * tq:(2 * s + 1) * tq], o4[(2 * s + 1) * tq:(2 * s + 2) * tq]
                o_ref[0, r0:r0 + tq, (2 * h + s) * LANES:(2 * h + s + 1) * LANES] = jnp.where(
                    lo, even, pltpu.roll(odd, HALF, 1)).astype(BF16)


def _swa_attention(sink, q, k, v, kc, vc):
    b, n, _ = q.shape
    nc = kc.shape[2]
    tq, nsub = SWA_BLOCK, SWA_BLOCKS_PER_STEP
    full = lambda m: pl.BlockSpec((1, SWA_KV_HEADS, m, LANES), lambda i, j: (i, 0, 0, 0))
    return pl.pallas_call(
        functools.partial(_swa_kernel, tq=tq, nsub=nsub, local=True, n_lat=n),
        out_shape=_sds((b, n, SWA_Q_W), BF16),
        grid=(b, n // (tq * nsub)),
        in_specs=[pl.BlockSpec(memory_space=pltpu.SMEM),
                  pl.BlockSpec((1, tq * nsub, SWA_Q_W), lambda i, j: (i, j, 0)),
                  full(n), full(n), full(nc), full(nc)],
        out_specs=pl.BlockSpec((1, tq * nsub, SWA_Q_W), lambda i, j: (i, j, 0)),
        compiler_params=_cparams("parallel", "arbitrary"),
        name="swa_attention",
    )(sink, q, k, v, kc, vc)


def _ctx_attention(sink, qc, kc, vc):
    b, nc, _ = qc.shape
    full = pl.BlockSpec((1, SWA_KV_HEADS, nc, LANES), lambda i, j: (i, 0, 0, 0))
    return pl.pallas_call(
        functools.partial(_swa_kernel, tq=nc, nsub=1, local=False, n_lat=0),
        out_shape=_sds((b, nc, SWA_Q_W), BF16),
        grid=(b, 1),
        in_specs=[pl.BlockSpec(memory_space=pltpu.SMEM),
                  pl.BlockSpec((1, nc, SWA_Q_W), lambda i, j: (i, 0, 0)), full, full],
        out_specs=pl.BlockSpec((1, nc, SWA_Q_W), lambda i, j: (i, 0, 0)),
        compiler_params=_cparams("parallel", "arbitrary"),
        name="ctx_attention",
    )(sink, qc, kc, vc)


def _ssm_matrices(a_re, a_im, log_step, b_re, b_im, c_re, c_im):
    L = SSM_CHUNK
    ar = a_re.astype(F32)
    ai = a_im.astype(F32)
    dt = jnp.exp(log_step.astype(F32))[..., None]
    m = jnp.arange(L + 1, dtype=F32)[:, None, None, None]
    mag = jnp.exp(m * (dt * ar)[None])
    pw_re = mag * jnp.cos(m * (dt * ai)[None])
    pw_im = mag * jnp.sin(m * (dt * ai)[None])
    den = ar * ar + ai * ai
    nr = pw_re[1] - 1.0
    coef_re = (nr * ar + pw_im[1] * ai) / den
    coef_im = (pw_im[1] * ar - nr * ai) / den
    br = b_re.astype(F32)
    bi = b_im.astype(F32)
    bb_re = coef_re[..., None] * br - coef_im[..., None] * bi
    bb_im = coef_re[..., None] * bi + coef_im[..., None] * br
    cr = c_re.astype(F32)[None]
    ci = c_im.astype(F32)[None]
    ca_re = cr * pw_re[:, :, :, None, :] - ci * pw_im[:, :, :, None, :]
    ca_im = cr * pw_im[:, :, :, None, :] + ci * pw_re[:, :, :, None, :]
    kern = jnp.sum(ca_re[:L, ..., None] * bb_re[None, :, :, None] - ca_im[:L, ..., None] * bb_im[None, :, :, None],
                   axis=4)
    G, C = SSM_GROUPS, SSM_GROUP
    k_f = jnp.transpose(kern[:, 0], (1, 3, 0, 2)).reshape(G, C, SSM_CW)
    k_r = jnp.transpose(kern[::-1, 1], (1, 3, 0, 2)).reshape(G, C, SSM_CW)
    rows = []
    for j in range(L):
        z_f = jnp.zeros((G, C, j * C), F32)
        z_r = jnp.zeros((G, C, (L - 1 - j) * C), F32)
        rows.append(jnp.concatenate([z_f, k_f[:, :, :SSM_CW - j * C]], -1)
                    + jnp.concatenate([k_r[:, :, (L - 1 - j) * C:], z_r], -1))
    m_intra = jnp.stack(rows, 1).reshape(G, SSM_CW, SSM_CW)
    both = lambda t_f, t_r: jnp.concatenate([t_f, t_r], -1)
    pin_re = jnp.transpose(both(pw_re[:L, 0][::-1], pw_re[:L, 1]), (1, 0, 2))[:, :, None, :]
    pin_im = jnp.transpose(both(pw_im[:L, 0][::-1], pw_im[:L, 1]), (1, 0, 2))[:, :, None, :]
    bt_re = both(jnp.transpose(bb_re[0], (0, 2, 1)), jnp.transpose(bb_re[1], (0, 2, 1)))[:, None]
    bt_im = both(jnp.transpose(bb_im[0], (0, 2, 1)), jnp.transpose(bb_im[1], (0, 2, 1)))[:, None]
    m_in = jnp.concatenate([pin_re * bt_re - pin_im * bt_im, pin_re * bt_im + pin_im * bt_re],
                           -1).reshape(G, SSM_CW, SSM_CW)
    pst_re = jnp.transpose(both(pw_re[1:, 0], pw_re[1:, 1][::-1]), (1, 0, 2))[:, :, None, :]
    pst_im = jnp.transpose(both(pw_im[1:, 0], pw_im[1:, 1][::-1]), (1, 0, 2))[:, :, None, :]
    ct_re = both(c_re[0].astype(F32), c_re[1].astype(F32))[:, None]
    ct_im = both(c_im[0].astype(F32), c_im[1].astype(F32))[:, None]
    m_state_t = jnp.concatenate([pst_re * ct_re - pst_im * ct_im, -(pst_re * ct_im + pst_im * ct_re)],
                                -1).reshape(G, SSM_CW, SSM_CW)
    a_l = jnp.stack([both(pw_re[L, 0], pw_re[L, 1]), both(pw_im[L, 0], pw_im[L, 1])], 1)
    return m_in.astype(BF16), m_intra.astype(BF16), m_state_t.astype(BF16), a_l


def _ssm_pack_kernel(uc_ref, u_ref, x_ref, *, nc_ctx, nc_lat):
    for src, row0, nch in ((uc_ref, 0, nc_ctx), (u_ref, nc_ctx, nc_lat)):
        steps = [src[0, pl.ds(j, nch, stride=SSM_CHUNK), :] for j in range(SSM_CHUNK)]
        lane = lax.broadcasted_iota(jnp.int32, (nch, LANES), 1)
        piece = [(lane >= jj * SSM_GROUP) & (lane < (jj + 1) * SSM_GROUP) for jj in range(SSM_SLAB_GROUPS)]
        for g in range(SSM_SLAB_GROUPS):
            for h in range(SSM_CW // LANES):
                acc = jnp.zeros((nch, LANES), F32)
                for jj in range(SSM_SLAB_GROUPS):
                    z = steps[h * SSM_SLAB_GROUPS + jj]
                    shift = ((jj - g) * SSM_GROUP) % LANES
                    acc = jnp.where(piece[jj], z if shift == 0 else pltpu.roll(z, shift, 1), acc)
                x_ref[g, row0:row0 + nch, h * LANES:(h + 1) * LANES] = acc.astype(BF16)


def _ssm_unpack_kernel(y_ref, oc_ref, o_ref, *, nc_ctx, nc_lat):
    for dst, row0, nch in ((oc_ref, 0, nc_ctx), (o_ref, nc_ctx, nc_lat)):
        lane = lax.broadcasted_iota(jnp.int32, (nch, LANES), 1)
        piece = [(lane >= g * SSM_GROUP) & (lane < (g + 1) * SSM_GROUP) for g in range(SSM_SLAB_GROUPS)]
        for i in range(SSM_CHUNK):
            h, ii = divmod(i, SSM_SLAB_GROUPS)
            acc = jnp.zeros((nch, LANES), F32)
            for g in range(SSM_SLAB_GROUPS):
                z = y_ref[g, row0:row0 + nch, h * LANES:(h + 1) * LANES]
                shift = ((g - ii) * SSM_GROUP) % LANES
                acc = jnp.where(piece[g], z if shift == 0 else pltpu.roll(z, shift, 1), acc)
            dst[0, pl.ds(i, nch, stride=SSM_CHUNK), :] = acc


def _ssm_kernel(x_ref, min_ref, mintra_ref, mstate_ref, al_ref, y_ref, v_scr, s_scr, *, nb, nc_ctx, n_chunks):
    xv = x_ref[0]
    v = jnp.dot(xv, min_ref[0], preferred_element_type=F32)
    v_scr[0] = v[:, 0:LANES]
    v_scr[1] = v[:, LANES:]
    ar = al_ref[0, 0:1, :]
    ai = al_ref[0, 1:2, :]
    lo = lax.broadcasted_iota(jnp.int32, (nb, LANES), 1) < HALF

    def body(k, carry):
        sre, sim = carry
        kr = jnp.where(k < nc_ctx, nc_ctx - 1 - k, n_chunks - 1 + nc_ctx - k)
        rf = pl.ds(k, nb, stride=n_chunks)
        rr = pl.ds(kr, nb, stride=n_chunks)
        s_scr[0, rf, :] = sre
        s_scr[1, rr, :] = sre
        s_scr[2, rf, :] = sim
        s_scr[3, rr, :] = sim
        vre = jnp.where(lo, v_scr[0, rf, :], v_scr[0, rr, :])
        vim = jnp.where(lo, v_scr[1, rf, :], v_scr[1, rr, :])
        return ar * sre - ai * sim + vre, ar * sim + ai * sre + vim

    zero = jnp.zeros((nb, LANES), F32)
    lax.fori_loop(0, n_chunks, body, (zero, zero))
    lo_all = lax.broadcasted_iota(jnp.int32, (s_scr.shape[1], LANES), 1) < HALF
    s_in = jnp.concatenate([jnp.where(lo_all, s_scr[0], s_scr[1]), jnp.where(lo_all, s_scr[2], s_scr[3])],
                           axis=1).astype(BF16)
    y_ref[0] = jnp.dot(xv, mintra_ref[0], preferred_element_type=F32) + _nt_dot(s_in, mstate_ref[0])


def _ssm_scan(u, uc, mats):
    m_in, m_intra, m_state, a_l = mats
    b, n, _ = u.shape
    nc = uc.shape[1]
    nc_ctx, nc_lat = nc // SSM_CHUNK, n // SSM_CHUNK
    n_chunks = nc_ctx + nc_lat
    r = n_chunks * b
    nat = lambda m: pl.BlockSpec((1, m, LANES), lambda i, s: (i, 0, s))
    grp = pl.BlockSpec((SSM_SLAB_GROUPS, n_chunks, SSM_CW), lambda i, s: (s, i, 0))
    xg = pl.pallas_call(
        functools.partial(_ssm_pack_kernel, nc_ctx=nc_ctx, nc_lat=nc_lat),
        out_shape=_sds((SSM_GROUPS, r, SSM_CW), BF16),
        grid=(b, SSM_GROUPS // SSM_SLAB_GROUPS),
        in_specs=[nat(nc), nat(n)],
        out_specs=grp,
        compiler_params=_cparams("parallel", "parallel"),
        name="ssm_pack",
    )(uc, u)
    mat = pl.BlockSpec((1, SSM_CW, SSM_CW), lambda g: (g, 0, 0))
    yg = pl.pallas_call(
        functools.partial(_ssm_kernel, nb=b, nc_ctx=nc_ctx, n_chunks=n_chunks),
        out_shape=_sds((SSM_GROUPS, r, SSM_CW), F32),
        grid=(SSM_GROUPS,),
        in_specs=[pl.BlockSpec((1, r, SSM_CW), lambda g: (g, 0, 0)), mat, mat, mat,
                  pl.BlockSpec((1, 2, LANES), lambda g: (g, 0, 0))],
        out_specs=pl.BlockSpec((1, r, SSM_CW), lambda g: (g, 0, 0)),
        scratch_shapes=[pltpu.VMEM((2, r, LANES), F32), pltpu.VMEM((4, r, LANES), F32)],
        compiler_params=_cparams("parallel"),
        name="ssm_scan",
    )(xg, m_in, m_intra, m_state, a_l)
    ysc, ys = pl.pallas_call(
        functools.partial(_ssm_unpack_kernel, nc_ctx=nc_ctx, nc_lat=nc_lat),
        out_shape=(_sds((b, nc, SSM_WIDTH), F32), _sds((b, n, SSM_WIDTH), F32)),
        grid=(b, SSM_GROUPS // SSM_SLAB_GROUPS),
        in_specs=[grp],
        out_specs=(nat(nc), nat(n)),
        compiler_params=_cparams("parallel", "parallel"),
        name="ssm_unpack",
    )(yg)
    return ys, ysc


def _split_rows(ref, val):
    m = val.shape[0]
    for s in range(TOKEN_TILE_ROWS):
        ref[pl.ds(s, m, stride=TOKEN_TILE_ROWS), :] = val[:, s * LANES:(s + 1) * LANES]


def _merge_rows(ref):
    m = ref.shape[0] // TOKEN_TILE_ROWS
    return jnp.concatenate([ref[pl.ds(s, m, stride=TOKEN_TILE_ROWS), :] for s in range(TOKEN_TILE_ROWS)], axis=1)


def _token_tile(ref, t):
    return ref.at[pl.ds(pl.multiple_of(t * TOKEN_TILE_ROWS, TOKEN_TILE_ROWS), TOKEN_TILE_ROWS)]


def _post_kernel(*refs, alpha, with_ssm):
    if with_ssm:
        (att_ref, ys_ref, u_ref, dsk_ref, gw_ref, gb_ref, wo_ref, x_ref, g1_ref, lg_ref, lb_ref, sc2_ref, sh2_ref,
         x1_ref, h2_ref) = refs
        y = ys_ref[0] + u_ref[0] * dsk_ref[...]
        gl = jax.nn.gelu(y)
        gate = jax.nn.sigmoid(jnp.dot(gl.astype(BF16), gw_ref[...], preferred_element_type=F32) + gb_ref[...])
        ssm = (gl * gate).astype(BF16)
        mix = (jnp.dot(att_ref[0], wo_ref[0:SWA_Q_W, :], preferred_element_type=F32)
               + jnp.dot(ssm, wo_ref[SWA_Q_W:, :], preferred_element_type=F32))
    else:
        att_ref, wo_ref, x_ref, g1_ref, lg_ref, lb_ref, sc2_ref, sh2_ref, x1_ref, h2_ref = refs
        mix = jnp.dot(att_ref[0], wo_ref[...], preferred_element_type=F32)
    x1 = _layer_norm(alpha * x_ref[0] + g1_ref[0] * mix, lg_ref[...], lb_ref[...])
    x1_ref[0] = x1
    _split_rows(h2_ref, x1 * (1.0 + sc2_ref[0]) + sh2_ref[0])


def _post(att, ssm_args, w_out_bf16, x, g1, ln_g, ln_b, sc2, sh2, alpha, tm):
    b, n, d = x.shape
    tok = lambda w: pl.BlockSpec((1, tm, w), lambda i, j: (i, j, 0))
    vec = pl.BlockSpec((1, 1, d), lambda i, j: (i, 0, 0))
    const = lambda a: pl.BlockSpec(a.shape, lambda i, j: (0,) * a.ndim)
    in_specs = [tok(att.shape[-1])]
    args = [att]
    if ssm_args is not None:
        ys, u, dsk, gw, gb = ssm_args
        in_specs += [tok(SSM_WIDTH), tok(SSM_WIDTH), const(dsk), const(gw), const(gb)]
        args += [ys, u, dsk, gw, gb]
    in_specs += [const(w_out_bf16), tok(d), vec, const(ln_g), const(ln_b), vec, vec]
    args += [w_out_bf16, x, g1, ln_g, ln_b, sc2, sh2]
    per_b = n // tm
    return pl.pallas_call(
        functools.partial(_post_kernel, alpha=alpha, with_ssm=ssm_args is not None),
        out_shape=(_sds((b, n, d), F32), _sds((b * n * TOKEN_TILE_ROWS, LANES), F32)),
        grid=(b, n // tm),
        in_specs=in_specs,
        out_specs=(tok(d), pl.BlockSpec((tm * TOKEN_TILE_ROWS, LANES), lambda i, j: (i * per_b + j, 0))),
        compiler_params=_cparams("parallel", "parallel"),
        name="post_mixer_ssm" if ssm_args is not None else "post_mixer",
    )(*args)


def _first_max(v, sub):
    m = jnp.max(v, 0, keepdims=True)
    idx = jnp.min(jnp.where(v == m, sub, float(SUBLANES)), 0, keepdims=True)
    return m, idx


def _stream_blocks(toks, tb):
    starts = [0]
    for t in toks:
        starts.append(starts[-1] + t.shape[0] // (tb * TOKEN_TILE_ROWS))
    return starts


def _stream_spec(tok, tb, start):
    last = tok.shape[0] // (tb * TOKEN_TILE_ROWS) - 1
    return pl.BlockSpec((tb * TOKEN_TILE_ROWS, LANES), lambda i, *_: (jnp.clip(i - start, 0, last), 0))


def _router_kernel(*refs, starts):
    n_streams = len(starts) - 1
    tok_refs = refs[:n_streams]
    w_ref, b_ref, ids_ref, wts_ref, rank_ref, cnt_ref, carry_scr = refs[n_streams:]
    step = pl.program_id(0)

    @pl.when(step == 0)
    def _():
        carry_scr[...] = jnp.zeros_like(carry_scr)

    h = _merge_rows(tok_refs[0])
    for ref, start in zip(tok_refs[1:], starts[1:]):
        h = jnp.where(step >= start, _merge_rows(ref), h)
    w = w_ref[...]
    w_hi, h_hi = w.astype(BF16), h.astype(BF16)
    w_lo, h_lo = (w - w_hi.astype(F32)).astype(BF16), (h - h_hi.astype(F32)).astype(BF16)
    logits = _nt_dot(w_hi, h_hi) + (_nt_dot(w_hi, h_lo) + _nt_dot(w_lo, h_hi)) + b_ref[...]
    tm = logits.shape[1]
    sub = lax.broadcasted_iota(jnp.int32, (SUBLANES, tm), 0).astype(F32)
    gl = logits[0:SUBLANES]
    gmax, gi = _first_max(gl, sub)
    gp = 1.0 / jnp.sum(jnp.exp(gl - gmax), 0, keepdims=True)
    le = logits[ROUTER_EXPERT_ROW0:ROUTER_EXPERT_ROW0 + MOE_EPG]
    for g in range(1, MOE_GROUPS):
        le = jnp.where(gi == float(g), logits[ROUTER_EXPERT_ROW0 + g * MOE_EPG:ROUTER_EXPERT_ROW0 + (g + 1) * MOE_EPG], le)
    m1, i1 = _first_max(le, sub)
    m2, i2 = _first_max(jnp.where(sub == i1, NEG_INF, le), sub)
    t = jnp.exp(m2 - m1)
    e1 = gi * float(MOE_EPG) + i1
    e2 = gi * float(MOE_EPG) + i2
    ids_ref[0:1, :] = e1.astype(jnp.int32)
    ids_ref[1:2, :] = e2.astype(jnp.int32)
    wts_ref[0:1, :] = gp / (1.0 + t)
    wts_ref[1:2, :] = gp * t / (1.0 + t)
    esub = lax.broadcasted_iota(jnp.int32, (MOE_EXPERTS, tm), 0).astype(F32)
    oh1 = (esub == e1).astype(F32)
    oh2 = (esub == e2).astype(F32)
    both = oh1 + oh2
    earlier = (lax.broadcasted_iota(jnp.int32, (tm, tm), 0) < lax.broadcasted_iota(jnp.int32, (tm, tm), 1))
    prefix = jnp.dot(both.astype(BF16), earlier.astype(BF16), preferred_element_type=F32) + carry_scr[...]
    rank_ref[0:1, :] = jnp.sum(oh1 * prefix, 0, keepdims=True).astype(jnp.int32)
    rank_ref[1:2, :] = jnp.sum(oh2 * prefix, 0, keepdims=True).astype(jnp.int32)
    carry_scr[...] += jnp.sum(both, 1, keepdims=True)
    cnt_ref[...] = jnp.broadcast_to(carry_scr[...], cnt_ref.shape)


def _router(toks, wg, bg, we, be, tm):
    starts = _stream_blocks(toks, tm)
    t = starts[-1] * tm
    d = D_MODEL
    rows = ROUTER_EXPERT_ROW0 + MOE_EXPERTS
    w = jnp.zeros((rows, d), F32)
    w = w.at[:MOE_GROUPS].set(wg.T)
    w = w.at[ROUTER_EXPERT_ROW0:].set(jnp.transpose(we, (0, 2, 1)).reshape(MOE_EXPERTS, d))
    bias = jnp.full((rows, 1), NEG_INF, F32)
    bias = bias.at[:MOE_GROUPS, 0].set(bg)
    bias = bias.at[ROUTER_EXPERT_ROW0:, 0].set(be.reshape(-1))
    pair = pl.BlockSpec((MOE_TOPK, tm), lambda i: (0, i))
    ids, wts, rank, cnt = pl.pallas_call(
        functools.partial(_router_kernel, starts=starts),
        out_shape=(_sds((MOE_TOPK, t), jnp.int32), _sds((MOE_TOPK, t), F32), _sds((MOE_TOPK, t), jnp.int32),
                   _sds((MOE_EXPERTS, LANES), F32)),
        grid=(t // tm,),
        in_specs=[_stream_spec(tok, tm, s0) for tok, s0 in zip(toks, starts)]
        + [pl.BlockSpec((rows, d), lambda i: (0, 0)), pl.BlockSpec((rows, 1), lambda i: (0, 0))],
        out_specs=(pair, pair, pair, pl.BlockSpec((MOE_EXPERTS, LANES), lambda i: (0, 0))),
        scratch_shapes=[pltpu.VMEM((MOE_EXPERTS, 1), F32)],
        compiler_params=_cparams("arbitrary"),
        name="moe_router",
    )(*toks, w, bias)
    return ids, wts, rank, cnt[:, 0].astype(jnp.int32)


def _moe_plan(ids, rank, counts, n_tok):
    tm = MOE_ROW_TILE
    padded = ((counts + tm - 1) // tm) * tm
    ends = jnp.cumsum(padded)
    offs = ends - padded
    experts = jnp.arange(MOE_EXPERTS, dtype=jnp.int32)
    dest = (jnp.sum(jnp.where(ids[..., None] == experts, offs, 0), -1) + rank).astype(jnp.int32)
    n_tiles = (MOE_TOPK * n_tok + MOE_EXPERTS * (tm - 1)) // tm
    starts = jnp.arange(n_tiles, dtype=jnp.int32) * tm
    tile_expert = jnp.minimum(jnp.sum((ends[None, :] <= starts[:, None]).astype(jnp.int32), -1), MOE_EXPERTS - 1)
    n_valid = (ends[-1] // tm).astype(jnp.int32).reshape(1)
    pad = jnp.stack([offs + counts, padded - counts, jnp.broadcast_to(n_valid, counts.shape)]).astype(jnp.int32)
    return dest, tile_expert, n_valid, pad, n_tiles * tm


def _pad_fill(pad_ref, zero_scr, xs_ref, sem, wait):
    def per_expert(e, carry):
        first, count = pad_ref[0, e], pad_ref[1, e]
        piece = MOE_ROW_TILE // 2
        while piece >= 1:
            row = first + (count & ~(2 * piece - 1))
            n = piece * TOKEN_TILE_ROWS

            @pl.when((count & piece) != 0)
            def _(row=row, n=n):
                copy = pltpu.make_async_copy(
                    zero_scr.at[pl.ds(0, n)],
                    xs_ref.at[pl.ds(pl.multiple_of(row * TOKEN_TILE_ROWS, TOKEN_TILE_ROWS), n)], sem)
                copy.wait() if wait else copy.start()

            piece //= 2
        return carry

    lax.fori_loop(0, MOE_EXPERTS, per_expert, 0)
    half = MOE_ROW_TILE // 2 * TOKEN_TILE_ROWS
    n_tiles = xs_ref.shape[0] // (2 * half)

    def per_tile(t, carry):
        for h in range(2):
            copy = pltpu.make_async_copy(zero_scr, xs_ref.at[pl.ds(pl.multiple_of((2 * t + h) * half, half), half)],
                                         sem)
            copy.wait() if wait else copy.start()
        return carry

    lax.fori_loop(pad_ref[2, 0], n_tiles, per_tile, 0)


def _dispatch_kernel(*refs, tb, starts):
    n_streams = len(starts) - 1
    dest_ref, pad_ref = refs[:2]
    tok_refs = refs[2:2 + n_streams]
    xs_ref, zero_scr, sem, pad_sem = refs[2 + n_streams:]
    step = pl.program_id(0)

    @pl.when(step == 0)
    def _():
        zero_scr[...] = jnp.zeros_like(zero_scr)
        _pad_fill(pad_ref, zero_scr, xs_ref, pad_sem, wait=False)

    for s, tok_ref in enumerate(tok_refs):
        @pl.when((step >= starts[s]) & (step < starts[s + 1]))
        def _(tok_ref=tok_ref):
            def body(r, carry):
                for k in range(MOE_TOPK):
                    pltpu.make_async_copy(_token_tile(tok_ref, r), _token_tile(xs_ref, dest_ref[0, k, r]), sem).start()
                return carry

            lax.fori_loop(0, tb, body, 0, unroll=8)

    for k in range(MOE_TOPK):
        pltpu.make_async_copy(tok_refs[0], xs_ref.at[pl.ds(0, tb * TOKEN_TILE_ROWS)], sem).wait()

    @pl.when(step == 0)
    def _():
        _pad_fill(pad_ref, zero_scr, xs_ref, pad_sem, wait=True)


def _dispatch(toks, dest3, pad, n_rows):
    nblk, _, tb = dest3.shape
    starts = _stream_blocks(toks, tb)
    return pl.pallas_call(
        functools.partial(_dispatch_kernel, tb=tb, starts=starts),
        out_shape=_sds((n_rows * TOKEN_TILE_ROWS, LANES), F32),
        grid=(nblk,),
        in_specs=[pl.BlockSpec((1, MOE_TOPK, tb), lambda i: (i, 0, 0), memory_space=pltpu.SMEM),
                  pl.BlockSpec(memory_space=pltpu.SMEM)]
        + [_stream_spec(tok, tb, s0) for tok, s0 in zip(toks, starts)],
        out_specs=pl.BlockSpec(memory_space=pl.ANY),
        scratch_shapes=[pltpu.VMEM((MOE_ROW_TILE // 2 * TOKEN_TILE_ROWS, LANES), F32),
                        pltpu.SemaphoreType.DMA(()), pltpu.SemaphoreType.DMA(())],
        compiler_params=_cparams("arbitrary"),
        name="moe_dispatch",
    )(dest3, pad, *toks)


def _ffn_kernel(te_ref, nv_ref, x_ref, w1_ref, w3_ref, w2_ref, y_ref, w13_scr, w2_scr):
    i = pl.program_id(0)
    f = w1_ref.shape[2]

    @pl.when((i == 0) | (te_ref[i] != te_ref[jnp.maximum(i - 1, 0)]))
    def _():
        w13_scr[:, 0:f] = w1_ref[0].astype(BF16)
        w13_scr[:, f:2 * f] = w3_ref[0].astype(BF16)
        w2_scr[...] = w2_ref[0].astype(BF16)

    @pl.when(i < nv_ref[0])
    def _():
        h13 = jnp.dot(_merge_rows(x_ref).astype(BF16), w13_scr[...], preferred_element_type=F32)
        h1 = h13[:, 0:f]
        hh = (h1 * jax.nn.sigmoid(h1) * h13[:, f:2 * f]).astype(BF16)
        _split_rows(y_ref, jnp.dot(hh, w2_scr[...], preferred_element_type=F32))

    @pl.when(i >= nv_ref[0])
    def _():
        y_ref[...] = jnp.zeros_like(y_ref)


def _expert_ffn(tile_expert, n_valid, xs, w1, w3, w2):
    p = xs.shape[0] // TOKEN_TILE_ROWS
    _, d, f = w1.shape
    tm = MOE_ROW_TILE
    rows = pl.BlockSpec((tm * TOKEN_TILE_ROWS, LANES), lambda i, te, nv: (i, 0))
    rows_in = pl.BlockSpec((tm * TOKEN_TILE_ROWS, LANES), lambda i, te, nv: (jnp.minimum(i, nv[0] - 1), 0))
    return pl.pallas_call(
        _ffn_kernel,
        out_shape=_sds(xs.shape, F32),
        grid_spec=pltpu.PrefetchScalarGridSpec(
            num_scalar_prefetch=2,
            grid=(p // tm,),
            in_specs=[rows_in,
                      pl.BlockSpec((1, d, f), lambda i, te, nv: (te[i], 0, 0)),
                      pl.BlockSpec((1, d, f), lambda i, te, nv: (te[i], 0, 0)),
                      pl.BlockSpec((1, f, d), lambda i, te, nv: (te[i], 0, 0))],
            out_specs=rows,
            scratch_shapes=[pltpu.VMEM((d, 2 * f), BF16), pltpu.VMEM((f, d), BF16)]),
        compiler_params=_cparams("arbitrary"),
        name="moe_expert_ffn",
    )(tile_expert, n_valid, xs, w1, w3, w2)


def _combine_ln2_kernel(dest_ref, next_ref, x1_ref, ys_ref, wt_ref, g2_ref, lg_ref, lb_ref, o_ref, buf, sem, *,
                        alpha, tb, nblk):
    step = pl.program_id(0)

    def gather(d_ref, slot):
        def body(r, carry):
            for k in range(MOE_TOPK):
                pltpu.make_async_copy(_token_tile(ys_ref, d_ref[0, k, r]), _token_tile(buf.at[slot, k], r),
                                      sem.at[slot, k]).start()
            return carry

        lax.fori_loop(0, tb, body, 0, unroll=8)

    @pl.when(step == 0)
    def _():
        gather(dest_ref, 0)

    @pl.when(step + 1 < nblk)
    def _():
        gather(next_ref, (step + 1) % 2)

    slot = step % 2
    for k in range(MOE_TOPK):
        pltpu.make_async_copy(ys_ref.at[pl.ds(0, tb * TOKEN_TILE_ROWS)], buf.at[slot, k], sem.at[slot, k]).wait()
    f = wt_ref[:, 0:1] * _merge_rows(buf.at[slot, 0]) + wt_ref[:, 1:2] * _merge_rows(buf.at[slot, 1])
    o_ref[...] = _layer_norm(alpha * x1_ref[...] + g2_ref[0] * f, lg_ref[...], lb_ref[...])


def _combine_ln2(dest3, x1, ys, wt, g2, ln_g, ln_b, alpha, n_per_sample):
    t, d = x1.shape
    nblk, _, tb = dest3.shape
    const = pl.BlockSpec((1, d), lambda i: (0, 0))
    if g2.shape[0] == 1:
        g2_block = lambda i: (0, 0, 0)
    else:
        assert n_per_sample % tb == 0, (n_per_sample, tb)
        g2_block = lambda i: ((i * tb) // n_per_sample, 0, 0)
    return pl.pallas_call(
        functools.partial(_combine_ln2_kernel, alpha=alpha, tb=tb, nblk=nblk),
        out_shape=_sds((t, d), F32),
        grid=(nblk,),
        in_specs=[pl.BlockSpec((1, MOE_TOPK, tb), lambda i: (i, 0, 0), memory_space=pltpu.SMEM),
                  pl.BlockSpec((1, MOE_TOPK, tb), lambda i: (jnp.minimum(i + 1, nblk - 1), 0, 0),
                               memory_space=pltpu.SMEM),
                  pl.BlockSpec((tb, d), lambda i: (i, 0)),
                  pl.BlockSpec(memory_space=pl.ANY),
                  pl.BlockSpec((tb, MOE_TOPK), lambda i: (i, 0)),
                  pl.BlockSpec((1, 1, d), g2_block),
                  const, const],
        out_specs=pl.BlockSpec((tb, d), lambda i: (i, 0)),
        scratch_shapes=[pltpu.VMEM((2, MOE_TOPK, tb * TOKEN_TILE_ROWS, LANES), F32),
                        pltpu.SemaphoreType.DMA((2, MOE_TOPK))],
        compiler_params=_cparams("arbitrary"),
        name="moe_combine_ln2",
    )(dest3, dest3, x1, ys, wt, g2, ln_g, ln_b)


def _proj1_kernel(*refs, rope, with_q):
    x_ref, sc_ref, sh_ref, w_ref = refs[:4]
    refs = refs[4:]
    tabs = None
    if rope:
        tabs = (refs[0][...], refs[1][...], refs[2][...])
        refs = refs[3:]
    h = (x_ref[0] * (1.0 + sc_ref[0]) + sh_ref[0]).astype(BF16)
    r = jnp.dot(h, w_ref[...], preferred_element_type=F32)
    off = 0
    if with_q:
        q_ref, k_ref, v_ref = refs
        for hd in range(DIF_HEADS):
            q_ref[0, hd] = (_rot(r[:, hd * LANES:(hd + 1) * LANES], tabs) * (LOG2_E * HEAD_DIM ** -0.5)).astype(BF16)
        off = DIF_QK_W
    else:
        k_ref, v_ref = refs
    for hd in range(DIF_HEADS):
        k_ref[0, hd] = _rot(r[:, off + hd * LANES:off + (hd + 1) * LANES], tabs).astype(BF16)
        v_ref[0, hd] = r[:, off + DIF_QK_W + hd * LANES:off + DIF_QK_W + (hd + 1) * LANES].astype(BF16)


def _proj1(x, sc, sh, w_bf16, tabs, with_q, tm):
    b, n, d = x.shape
    rope = tabs is not None
    vec = pl.BlockSpec((1, 1, d), lambda i, j: (i, 0, 0))
    in_specs = [pl.BlockSpec((1, tm, d), lambda i, j: (i, j, 0)), vec, vec,
                pl.BlockSpec(w_bf16.shape, lambda i, j: (0, 0))]
    args = [x, sc, sh, w_bf16]
    if rope:
        in_specs += [pl.BlockSpec((tm, LANES), lambda i, j: (j, 0))] * 3
        args += list(tabs)
    hm = pl.BlockSpec((1, DIF_HEADS, tm, LANES), lambda i, j: (i, 0, j, 0))
    n_out = 3 if with_q else 2
    return pl.pallas_call(
        functools.partial(_proj1_kernel, rope=rope, with_q=with_q),
        out_shape=(_sds((b, DIF_HEADS, n, LANES), BF16),) * n_out,
        grid=(b, n // tm),
        in_specs=in_specs,
        out_specs=(hm,) * n_out,
        compiler_params=_cparams("parallel", "parallel"),
        name="proj1_qkv" if with_q else "proj1_kv_ctx",
    )(*args)


def _diff_kernel(lam_ref, q_ref, kl_ref, kc_ref, vl_ref, vc_ref, g_ref, o_ref, k_scr, v_scr, *, tq, n_lat, out_scale):
    @pl.when(pl.program_id(2) == 0)
    def _():
        k_scr[0:n_lat] = kl_ref[0, 0]
        k_scr[n_lat:] = kc_ref[0, 0]
        v_scr[0:n_lat, 0:LANES] = vl_ref[0, 0]
        v_scr[n_lat:, 0:LANES] = vc_ref[0, 0]
        v_scr[:, LANES:] = jnp.ones((v_scr.shape[0], LANES), BF16)

    lo = lax.broadcasted_iota(jnp.int32, (DIF_CHAIN_ROWS, LANES), 1) < HALF
    for r0 in range(0, tq, DIF_CHAIN_ROWS):
        q = q_ref[0, 0, r0:r0 + DIF_CHAIN_ROWS, :].astype(F32)
        maps = []
        for qm in (jnp.where(lo, q, 0.0), jnp.where(lo, 0.0, q)):
            s = _nt_dot(qm.astype(BF16), k_scr[...])
            p = jnp.exp2(s - jnp.max(s, -1, keepdims=True)).astype(BF16)
            oe = jnp.dot(p, v_scr[...], preferred_element_type=F32)
            maps.append(oe[:, 0:LANES] * (1.0 / oe[:, LANES:]))
        o = maps[0] - lam_ref[0] * maps[1]
        o = o * lax.rsqrt(jnp.mean(o * o, -1, keepdims=True) + RMS_EPS) * g_ref[...]
        o_ref[0, r0:r0 + DIF_CHAIN_ROWS, :] = (o * out_scale).astype(BF16)


def _diff_attention(lam, q, kl, kc, vl, vc, subln_g, lam_init, tq):
    b, nh, n, _ = q.shape
    nc = kc.shape[2]
    kv = lambda m: pl.BlockSpec((1, 1, m, LANES), lambda i, h, j: (i, h, 0, 0))
    return pl.pallas_call(
        functools.partial(_diff_kernel, tq=tq, n_lat=n, out_scale=1.0 - lam_init),
        out_shape=_sds((b, n, nh * LANES), BF16),
        grid=(b, nh, n // tq),
        in_specs=[pl.BlockSpec(memory_space=pltpu.SMEM),
                  pl.BlockSpec((1, 1, tq, LANES), lambda i, h, j: (i, h, j, 0)),
                  kv(n), kv(nc), kv(n), kv(nc),
                  pl.BlockSpec((1, LANES), lambda i, h, j: (0, 0))],
        out_specs=pl.BlockSpec((1, tq, LANES), lambda i, h, j: (i, j, h)),
        scratch_shapes=[pltpu.VMEM((n + nc, LANES), BF16), pltpu.VMEM((n + nc, 2 * LANES), BF16)],
        compiler_params=_cparams("parallel", "parallel", "arbitrary"),
        name="diff_attention",
    )(lam, q, kl, kc, vl, vc, subln_g)


def _moe_block(toks, layer, moe_wg, moe_bg, moe_we, moe_be, moe_w1, moe_w3, moe_w2):
    t = sum(tok.shape[0] for tok in toks) // TOKEN_TILE_ROWS
    tb = MOE_DMA_TILE
    ids, wts, rank, counts = _router(toks, moe_wg[layer], moe_bg[layer], moe_we[layer], moe_be[layer], tm=512)
    dest, tile_expert, n_valid, pad, n_rows = _moe_plan(ids, rank, counts, t)
    dest3 = dest.reshape(MOE_TOPK, t // tb, tb).transpose(1, 0, 2)
    xs = _dispatch(toks, dest3, pad, n_rows)
    flat = lambda w: w.reshape((-1,) + w.shape[2:])
    ys = _expert_ffn(tile_expert + layer * MOE_EXPERTS, n_valid, xs, flat(moe_w1), flat(moe_w3), flat(moe_w2))
    return ys, dest3, wts.T


def kernel(x, c, ctx, c_ctx, mod_w, mod_b, ln1_g, ln1_b, ln2_g, ln2_b, swa_ssm_w_in, swa_ssm_w_out, swa_sink, ssm_a_re, ssm_a_im, ssm_log_step, ssm_b_re, ssm_b_im, ssm_c_re, ssm_c_im, ssm_d, ssm_glu_w, ssm_glu_b, dif_w_in, dif_w_out, dif_lam_q1, dif_lam_k1, dif_lam_q2, dif_lam_k2, dif_subln_g, moe_wg, moe_bg, moe_we, moe_be, moe_w1, moe_w3, moe_w2):
    bsz, n, d = x.shape
    ctx_len = ctx.shape[1]
    depth = mod_w.shape[0]
    alpha = (2 * depth) ** 0.25
    tabs = _rope_tables(n)

    n_vec = 16
    cvec = jnp.zeros((n_vec, d), F32).at[:bsz].set(c).at[bsz].set(c_ctx)
    mods = _modulation(cvec, mod_w, mod_b)

    xl, xc = x, ctx
    for layer in range(depth):
        need_ctx = layer < depth - 1
        i = layer // 2
        lat = [mods[layer, :bsz, k * d:(k + 1) * d].reshape(bsz, 1, d) for k in range(6)]
        cx = [jnp.broadcast_to(mods[layer, bsz, k * d:(k + 1) * d].reshape(1, 1, d), (bsz, 1, d)) for k in range(6)]
        sh1, sc1, g1, sh2, sc2, g2 = lat
        csh1, csc1, cg1, csh2, csc2, cg2 = cx
        lg1, lb1 = ln1_g[layer].reshape(1, d), ln1_b[layer].reshape(1, d)
        lg2, lb2 = ln2_g[layer].reshape(1, d), ln2_b[layer].reshape(1, d)
        if layer % 2 == 0:
            w_in = swa_ssm_w_in[i].astype(BF16)
            w_out = swa_ssm_w_out[i].astype(BF16)
            q, k, v, u = _proj0(xl, sc1, sh1, w_in, tabs, tm=512)
            qc, kc, vc, uc = _proj0(xc, csc1, csh1, w_in, None, tm=ctx_len)
            sink = swa_sink[i].astype(F32)
            att = _swa_attention(sink, q, k, v, kc, vc)
            mats = _ssm_matrices(ssm_a_re[i], ssm_a_im[i], ssm_log_step[i], ssm_b_re[i], ssm_b_im[i],
                                 ssm_c_re[i], ssm_c_im[i])
            ys, ysc = _ssm_scan(u, uc, mats)
            glu = (ssm_d[i].reshape(1, SSM_WIDTH).astype(F32), ssm_glu_w[i].astype(BF16),
                   ssm_glu_b[i].reshape(1, SSM_WIDTH).astype(F32))
            x1, tok = _post(att, (ys, u) + glu, w_out, xl, g1, lg1, lb1, sc2, sh2, alpha, 512)
            toks = [tok]
            if need_ctx:
                att_c = _ctx_attention(sink, qc, kc, vc)
                xc1, tok_c = _post(att_c, (ysc, uc) + glu, w_out, xc, cg1, lg1, lb1, csc2, csh2, alpha, ctx_len)
                toks.append(tok_c)
        else:
            lam_init = 0.8 - 0.6 * math.exp(-0.3 * layer)
            w_in = dif_w_in[i].astype(BF16)
            w_out = dif_w_out[i].astype(BF16)
            q, k, v = _proj1(xl, sc1, sh1, w_in, tabs, True, tm=512)
            kc, vc = _proj1(xc, csc1, csh1, w_in[:, DIF_QK_W:], None, False, tm=ctx_len)
            lam = (jnp.exp(jnp.sum(dif_lam_q1[i].astype(F32) * dif_lam_k1[i].astype(F32)))
                   - jnp.exp(jnp.sum(dif_lam_q2[i].astype(F32) * dif_lam_k2[i].astype(F32))) + lam_init).reshape(1)
            att = _diff_attention(lam, q, k, kc, v, vc, dif_subln_g[i].reshape(1, DIF_V_HEAD).astype(F32),
                                  lam_init, tq=n)
            x1, tok = _post(att, None, w_out, xl, g1, lg1, lb1, sc2, sh2, alpha, 512)
            toks = [tok]
            if need_ctx:
                raise NotImplementedError("a differential-attention layer followed by another layer")
        ys_moe, dest3, wt = _moe_block(toks, layer, moe_wg, moe_bg, moe_we, moe_be, moe_w1, moe_w3, moe_w2)
        n_lat_blk = bsz * n // MOE_DMA_TILE
        xl = _combine_ln2(dest3[:n_lat_blk], x1.reshape(-1, d), ys_moe, wt[:bsz * n], g2, lg2, lb2, alpha,
                          n).reshape(bsz, n, d)
        if need_ctx:
            xc = _combine_ln2(dest3[n_lat_blk:], xc1.reshape(-1, d), ys_moe, wt[bsz * n:], cg2[:1], lg2, lb2, alpha,
                              ctx_len).reshape(bsz, ctx_len, d)
    return xl
```

```python
import functools
import math
import typing

import jax
import jax.numpy as jnp
from jax import lax
from jax.experimental import pallas as pl
from jax.experimental.pallas import tpu as pltpu

F32 = jnp.float32
BF16 = jnp.bfloat16
HIGHEST = lax.Precision.HIGHEST

D_MODEL = 1024
GRID_W = 64
HEAD_DIM = 64
ROPE_BASE = 10000.0
ROPE_FREQS = HEAD_DIM // 4
LN_EPS = 1e-5
RMS_EPS = 1e-5
NEG_INF = -1e30
LOG2_E = math.log2(math.e)
LANES = 128
HALF = LANES // 2

SWA_HEADS = 8
SWA_KV_HEADS = 2
SWA_WINDOW = 128
SWA_BLOCK = 128
SWA_BLOCKS_PER_STEP = 4
SWA_Q_W = SWA_HEADS * HEAD_DIM
SWA_KV_W = SWA_KV_HEADS * HEAD_DIM

SSM_WIDTH = D_MODEL // 2
SSM_GROUP = 16
SSM_GROUPS = SSM_WIDTH // SSM_GROUP
SSM_STATE = 64
SSM_CHUNK = 16
SSM_CW = SSM_CHUNK * SSM_GROUP
SSM_SLAB_GROUPS = LANES // SSM_GROUP

AB_IN_W = SWA_Q_W + 2 * SWA_KV_W + SSM_WIDTH

DIF_HEADS = D_MODEL // (2 * HEAD_DIM)
DIF_QK_W = DIF_HEADS * 2 * HEAD_DIM
DIF_V_HEAD = 2 * HEAD_DIM
DIF_V_W = DIF_HEADS * DIF_V_HEAD
DIF_CHAIN_ROWS = 128

MOE_GROUPS = 4
MOE_EPG = 8
MOE_EXPERTS = MOE_GROUPS * MOE_EPG
MOE_HIDDEN = D_MODEL // 4
MOE_TOPK = 2
SUBLANES = 8
ROUTER_EXPERT_ROW0 = SUBLANES
MOE_ROW_TILE = 512
MOE_DMA_TILE = 512
TOKEN_TILE_ROWS = D_MODEL // LANES

VMEM_LIMIT = 56 * 1024 * 1024


def _cparams(*sem):
    return pltpu.CompilerParams(dimension_semantics=sem, vmem_limit_bytes=VMEM_LIMIT)


def _sds(shape, dtype):
    return jax.ShapeDtypeStruct(shape, dtype)


def _nt_dot(a, b):
    return lax.dot_general(a, b, (((1,), (1,)), ((), ())), preferred_element_type=F32)


def _layer_norm(r, g, b):
    mu = jnp.mean(r, -1, keepdims=True)
    rc = r - mu
    var = jnp.mean(rc * rc, -1, keepdims=True)
    return rc * lax.rsqrt(var + LN_EPS) * g + b


class ModVec(typing.NamedTuple):
    table: jax.Array
    row0: int
    stride: int

    def spec(self, sample_of_step=lambda i, *_: i):
        d = self.table.shape[-1]
        return pl.BlockSpec((1, 1, d), lambda *idx: (self.row0 + self.stride * sample_of_step(*idx), 0, 0))


def _mod_kernel(c_ref, w_ref, b_ref, o_ref):
    cv = c_ref[...]
    s = cv * jax.nn.sigmoid(cv)
    o_ref[0] = jnp.dot(s, w_ref[0], preferred_element_type=F32, precision=HIGHEST) + b_ref[0]


def _modulation(cvec, mod_w, mod_b):
    depth, d, w6 = mod_w.shape
    tn = 1536
    return pl.pallas_call(
        _mod_kernel,
        out_shape=_sds((depth, cvec.shape[0], w6), F32),
        grid=(depth, w6 // tn),
        in_specs=[pl.BlockSpec(cvec.shape, lambda l, j: (0, 0)),
                  pl.BlockSpec((1, d, tn), lambda l, j: (l, 0, j)),
                  pl.BlockSpec((1, 1, tn), lambda l, j: (l, 0, j))],
        out_specs=pl.BlockSpec((1, cvec.shape[0], tn), lambda l, j: (l, 0, j)),
        compiler_params=_cparams("arbitrary", "arbitrary"),
        name="modulation",
    )(cvec, mod_w, mod_b.reshape(depth, 1, w6))


def _rope_tables(n):
    rows = n // GRID_W
    row = jnp.repeat(jnp.arange(rows, dtype=F32), GRID_W)
    col = jnp.tile(jnp.arange(GRID_W, dtype=F32), rows)
    inv = ROPE_BASE ** (-jnp.arange(ROPE_FREQS, dtype=F32) / ROPE_FREQS)
    ang_r = row[:, None] * inv[None, :]
    ang_c = col[:, None] * inv[None, :]
    zeros = jnp.zeros_like(ang_r)
    cos64 = jnp.concatenate([jnp.cos(ang_r), jnp.cos(ang_r), jnp.cos(ang_c), jnp.cos(ang_c)], -1)
    sa64 = jnp.concatenate([-jnp.sin(ang_r), zeros, -jnp.sin(ang_c), zeros], -1)
    sb64 = jnp.concatenate([zeros, jnp.sin(ang_r), zeros, jnp.sin(ang_c)], -1)
    return tuple(jnp.tile(t, (1, LANES // HEAD_DIM)) for t in (cos64, sa64, sb64))


def _rot(t, tabs):
    if tabs is None:
        return t
    cos, sa, sb = tabs
    return t * cos + pltpu.roll(t, LANES - ROPE_FREQS, 1) * sa + pltpu.roll(t, ROPE_FREQS, 1) * sb


def _dup_halves(t):
    lo = lax.broadcasted_iota(jnp.int32, t.shape, 1) < HALF
    ta = jnp.where(lo, t, 0.0)
    tb = t - ta
    return ta + pltpu.roll(ta, HALF, 1), tb + pltpu.roll(tb, HALF, 1)


def _proj0_kernel(*refs, rope):
    if rope:
        x_ref, sc_ref, sh_ref, w_ref, cos_ref, sa_ref, sb_ref, q_ref, k_ref, v_ref, u_ref = refs
        tabs = (cos_ref[...], sa_ref[...], sb_ref[...])
    else:
        x_ref, sc_ref, sh_ref, w_ref, q_ref, k_ref, v_ref, u_ref = refs
        tabs = None
    h = (x_ref[0] * (1.0 + sc_ref[0]) + sh_ref[0]).astype(BF16)
    r = jnp.dot(h, w_ref[...], preferred_element_type=F32)
    scale = LOG2_E * HEAD_DIM ** -0.5
    for s in range(SWA_Q_W // LANES):
        q_ref[0, :, s * LANES:(s + 1) * LANES] = (_rot(r[:, s * LANES:(s + 1) * LANES], tabs) * scale).astype(BF16)
    k0, k1 = _dup_halves(_rot(r[:, SWA_Q_W:SWA_Q_W + LANES], tabs))
    vv = r[:, SWA_Q_W + LANES:SWA_Q_W + 2 * LANES]
    lo = lax.broadcasted_iota(jnp.int32, vv.shape, 1) < HALF
    k_ref[0, 0] = k0.astype(BF16)
    k_ref[0, 1] = k1.astype(BF16)
    v_ref[0, 0] = jnp.where(lo, vv, 1.0).astype(BF16)
    v_ref[0, 1] = jnp.where(lo, pltpu.roll(vv, HALF, 1), 1.0).astype(BF16)
    u_ref[0] = r[:, SWA_Q_W + 2 * LANES:]


def _proj0(x, sc, sh, w_bf16, tabs, tm):
    b, n, d = x.shape
    rope = tabs is not None
    in_specs = [pl.BlockSpec((1, tm, d), lambda i, j: (i, j, 0)), sc.spec(), sh.spec(),
                pl.BlockSpec(w_bf16.shape, lambda i, j: (0, 0))]
    args = [x, sc.table, sh.table, w_bf16]
    if rope:
        in_specs += [pl.BlockSpec((tm, LANES), lambda i, j: (j, 0))] * 3
        args += list(tabs)
    kv_spec = pl.BlockSpec((1, SWA_KV_HEADS, tm, LANES), lambda i, j: (i, 0, j, 0))
    return pl.pallas_call(
        functools.partial(_proj0_kernel, rope=rope),
        out_shape=(_sds((b, n, SWA_Q_W), BF16), _sds((b, SWA_KV_HEADS, n, LANES), BF16),
                   _sds((b, SWA_KV_HEADS, n, LANES), BF16), _sds((b, n, SSM_WIDTH), F32)),
        grid=(b, n // tm),
        in_specs=in_specs,
        out_specs=(pl.BlockSpec((1, tm, SWA_Q_W), lambda i, j: (i, j, 0)), kv_spec, kv_spec,
                   pl.BlockSpec((1, tm, SSM_WIDTH), lambda i, j: (i, j, 0))),
        compiler_params=_cparams("parallel", "parallel"),
        name="proj0_rope" if rope else "proj0_ctx",
    )(*args)


def _swa_kernel(*refs, tq, nsub, local, n_lat):
    if local:
        sink_ref, q_ref, k_ref, v_ref, kc_ref, vc_ref, o_ref = refs
    else:
        sink_ref, q_ref, kc_ref, vc_ref, o_ref = refs
    rows = 4 * tq
    lo = lax.broadcasted_iota(jnp.int32, (tq, LANES), 1) < HALF
    lo4 = lax.broadcasted_iota(jnp.int32, (rows, LANES), 1) < HALF
    rown = lax.broadcasted_iota(jnp.int32, (rows, 1), 0)
    for sub in range(nsub):
        j = pl.program_id(1) * nsub + sub
        r0 = sub * tq
        if local:
            span = 3 * SWA_BLOCK
            start = pl.multiple_of(jnp.clip((j - 1) * SWA_BLOCK, 0, n_lat - span), SWA_BLOCK)
            rr = lax.broadcasted_iota(jnp.int32, (rows, span), 0)
            cc = lax.broadcasted_iota(jnp.int32, (rows, span), 1)
            qpos = j * tq + (rr & (tq - 1))
            mask = jnp.abs(qpos - (start + cc)) <= SWA_WINDOW
        for h in range(SWA_KV_HEADS):
            qa = q_ref[0, r0:r0 + tq, (2 * h) * LANES:(2 * h + 1) * LANES].astype(F32)
            qb = q_ref[0, r0:r0 + tq, (2 * h + 1) * LANES:(2 * h + 2) * LANES].astype(F32)
            q4 = jnp.concatenate([jnp.where(lo, qa, 0.0), jnp.where(lo, 0.0, qa),
                                  jnp.where(lo, qb, 0.0), jnp.where(lo, 0.0, qb)], 0).astype(BF16)
            sink = LOG2_E * jnp.where(rown < tq, sink_ref[4 * h],
                                      jnp.where(rown < 2 * tq, sink_ref[4 * h + 1],
                                                jnp.where(rown < 3 * tq, sink_ref[4 * h + 2], sink_ref[4 * h + 3])))
            s_ctx = _nt_dot(q4, kc_ref[0, h])
            m = jnp.maximum(jnp.max(s_ctx, -1, keepdims=True), sink)
            if local:
                s_loc = jnp.where(mask, _nt_dot(q4, k_ref[0, h, pl.ds(start, span), :]), NEG_INF)
                m = jnp.maximum(m, jnp.max(s_loc, -1, keepdims=True))
            o4 = jnp.dot(jnp.exp2(s_ctx - m).astype(BF16), vc_ref[0, h], preferred_element_type=F32)
            if local:
                o4 = o4 + jnp.dot(jnp.exp2(s_loc - m).astype(BF16), v_ref[0, h, pl.ds(start, span), :],
                                  preferred_element_type=F32)
            o4 = o4 + jnp.where(lo4, 0.0, jnp.exp2(sink - m))
            o4 = o4 * (1.0 / jnp.where(lo4, pltpu.roll(o4, HALF, 1), 1.0))
            for s in range(2):
                even, odd = o4[2 * s * tq:(2 * s + 1) * tq], o4[(2 * s + 1) * tq:(2 * s + 2) * tq]
                o_ref[0, r0:r0 + tq, (2 * h + s) * LANES:(2 * h + s + 1) * LANES] = jnp.where(
                    lo, even, pltpu.roll(odd, HALF, 1)).astype(BF16)


def _swa_attention(sink, q, k, v, kc, vc):
    b, n, _ = q.shape
    nc = kc.shape[2]
    tq, nsub = SWA_BLOCK, SWA_BLOCKS_PER_STEP
    full = lambda m: pl.BlockSpec((1, SWA_KV_HEADS, m, LANES), lambda i, j: (i, 0, 0, 0))
    return pl.pallas_call(
        functools.partial(_swa_kernel, tq=tq, nsub=nsub, local=True, n_lat=n),
        out_shape=_sds((b, n, SWA_Q_W), BF16),
        grid=(b, n // (tq * nsub)),
        in_specs=[pl.BlockSpec(memory_space=pltpu.SMEM),
                  pl.BlockSpec((1, tq * nsub, SWA_Q_W), lambda i, j: (i, j, 0)),
                  full(n), full(n), full(nc), full(nc)],
        out_specs=pl.BlockSpec((1, tq * nsub, SWA_Q_W), lambda i, j: (i, j, 0)),
        compiler_params=_cparams("parallel", "arbitrary"),
        name="swa_attention",
    )(sink, q, k, v, kc, vc)


def _ctx_attention(sink, qc, kc, vc):
    b, nc, _ = qc.shape
    full = pl.BlockSpec((1, SWA_KV_HEADS, nc, LANES), lambda i, j: (i, 0, 0, 0))
    return pl.pallas_call(
        functools.partial(_swa_kernel, tq=nc, nsub=1, local=False, n_lat=0),
        out_shape=_sds((b, nc, SWA_Q_W), BF16),
        grid=(b, 1),
        in_specs=[pl.BlockSpec(memory_space=pltpu.SMEM),
                  pl.BlockSpec((1, nc, SWA_Q_W), lambda i, j: (i, 0, 0)), full, full],
        out_specs=pl.BlockSpec((1, nc, SWA_Q_W), lambda i, j: (i, 0, 0)),
        compiler_params=_cparams("parallel", "arbitrary"),
        name="ctx_attention",
    )(sink, qc, kc, vc)


def _ssm_matrices(a_re, a_im, log_step, b_re, b_im, c_re, c_im):
    L = SSM_CHUNK
    ar = a_re.astype(F32)
    ai = a_im.astype(F32)
    dt = jnp.exp(log_step.astype(F32))[..., None]
    m = jnp.arange(L + 1, dtype=F32)[:, None, None, None]
    mag = jnp.exp(m * (dt * ar)[None])
    pw_re = mag * jnp.cos(m * (dt * ai)[None])
    pw_im = mag * jnp.sin(m * (dt * ai)[None])
    den = ar * ar + ai * ai
    nr = pw_re[1] - 1.0
    coef_re = (nr * ar + pw_im[1] * ai) / den
    coef_im = (pw_im[1] * ar - nr * ai) / den
    br = b_re.astype(F32)
    bi = b_im.astype(F32)
    bb_re = coef_re[..., None] * br - coef_im[..., None] * bi
    bb_im = coef_re[..., None] * bi + coef_im[..., None] * br
    cr = c_re.astype(F32)[None]
    ci = c_im.astype(F32)[None]
    ca_re = cr * pw_re[:, :, :, None, :] - ci * pw_im[:, :, :, None, :]
    ca_im = cr * pw_im[:, :, :, None, :] + ci * pw_re[:, :, :, None, :]
    kern = jnp.sum(ca_re[:L, ..., None] * bb_re[None, :, :, None] - ca_im[:L, ..., None] * bb_im[None, :, :, None],
                   axis=4)
    G, C = SSM_GROUPS, SSM_GROUP
    k_f = jnp.transpose(kern[:, 0], (1, 3, 0, 2)).reshape(G, C, SSM_CW)
    k_r = jnp.transpose(kern[::-1, 1], (1, 3, 0, 2)).reshape(G, C, SSM_CW)
    span = (2 * L - 1) * C
    lagged = (jnp.pad(k_f, ((0, 0), (0, 0), ((L - 1) * C, 0)))
              + jnp.pad(k_r, ((0, 0), (0, 0), (0, (L - 1) * C))))
    wide = jnp.pad(lagged, ((0, 0), (0, 0), (0, C)))
    flat = jnp.broadcast_to(wide[:, :, None, :], (G, C, L, span + C)).reshape(G, C, L * (span + C))
    skew = flat[:, :, (L - 1) * C:(L - 1) * C + L * span].reshape(G, C, L, span)[..., :SSM_CW]
    m_intra = jnp.transpose(skew, (0, 2, 1, 3)).reshape(G, SSM_CW, SSM_CW)
    both = lambda t_f, t_r: jnp.concatenate([t_f, t_r], -1)
    pin_re = jnp.transpose(both(pw_re[:L, 0][::-1], pw_re[:L, 1]), (1, 0, 2))[:, :, None, :]
    pin_im = jnp.transpose(both(pw_im[:L, 0][::-1], pw_im[:L, 1]), (1, 0, 2))[:, :, None, :]
    bt_re = both(jnp.transpose(bb_re[0], (0, 2, 1)), jnp.transpose(bb_re[1], (0, 2, 1)))[:, None]
    bt_im = both(jnp.transpose(bb_im[0], (0, 2, 1)), jnp.transpose(bb_im[1], (0, 2, 1)))[:, None]
    m_in = jnp.concatenate([pin_re * bt_re - pin_im * bt_im, pin_re * bt_im + pin_im * bt_re],
                           -1).reshape(G, SSM_CW, SSM_CW)
    pst_re = jnp.transpose(both(pw_re[1:, 0], pw_re[1:, 1][::-1]), (1, 0, 2))[:, :, None, :]
    pst_im = jnp.transpose(both(pw_im[1:, 0], pw_im[1:, 1][::-1]), (1, 0, 2))[:, :, None, :]
    ct_re = both(c_re[0].astype(F32), c_re[1].astype(F32))[:, None]
    ct_im = both(c_im[0].astype(F32), c_im[1].astype(F32))[:, None]
    m_state_t = jnp.concatenate([pst_re * ct_re - pst_im * ct_im, -(pst_re * ct_im + pst_im * ct_re)],
                                -1).reshape(G, SSM_CW, SSM_CW)
    a_l = jnp.stack([both(pw_re[L, 0], pw_re[L, 1]), both(pw_im[L, 0], pw_im[L, 1])], 1)
    return m_in.astype(BF16), m_intra.astype(BF16), m_state_t.astype(BF16), a_l


def _ssm_pack_kernel(uc_ref, u_ref, x_ref, *, nc_ctx, nc_lat):
    for src, row0, nch in ((uc_ref, 0, nc_ctx), (u_ref, nc_ctx, nc_lat)):
        steps = [src[0, pl.ds(j, nch, stride=SSM_CHUNK), :] for j in range(SSM_CHUNK)]
        lane = lax.broadcasted_iota(jnp.int32, (nch, LANES), 1)
        piece = [(lane >= jj * SSM_GROUP) & (lane < (jj + 1) * SSM_GROUP) for jj in range(SSM_SLAB_GROUPS)]
        for g in range(SSM_SLAB_GROUPS):
            for h in range(SSM_CW // LANES):
                acc = jnp.zeros((nch, LANES), F32)
                for jj in range(SSM_SLAB_GROUPS):
                    z = steps[h * SSM_SLAB_GROUPS + jj]
                    shift = ((jj - g) * SSM_GROUP) % LANES
                    acc = jnp.where(piece[jj], z if shift == 0 else pltpu.roll(z, shift, 1), acc)
                x_ref[g, row0:row0 + nch, h * LANES:(h + 1) * LANES] = acc.astype(BF16)


def _ssm_unpack_kernel(y_ref, oc_ref, o_ref, *, nc_ctx, nc_lat):
    for dst, row0, nch in ((oc_ref, 0, nc_ctx), (o_ref, nc_ctx, nc_lat)):
        lane = lax.broadcasted_iota(jnp.int32, (nch, LANES), 1)
        piece = [(lane >= g * SSM_GROUP) & (lane < (g + 1) * SSM_GROUP) for g in range(SSM_SLAB_GROUPS)]
        for i in range(SSM_CHUNK):
            h, ii = divmod(i, SSM_SLAB_GROUPS)
            acc = jnp.zeros((nch, LANES), F32)
            for g in range(SSM_SLAB_GROUPS):
                z = y_ref[g, row0:row0 + nch, h * LANES:(h + 1) * LANES]
                shift = ((g - ii) * SSM_GROUP) % LANES
                acc = jnp.where(piece[g], z if shift == 0 else pltpu.roll(z, shift, 1), acc)
            dst[0, pl.ds(i, nch, stride=SSM_CHUNK), :] = acc


def _ssm_kernel(x_ref, min_ref, mintra_ref, mstate_ref, al_ref, y_ref, v_scr, s_scr, *, nb, nc_ctx, n_chunks):
    xv = x_ref[0]
    v = jnp.dot(xv, min_ref[0], preferred_element_type=F32)
    v_scr[0] = v[:, 0:LANES]
    v_scr[1] = v[:, LANES:]
    ar = al_ref[0, 0:1, :]
    ai = al_ref[0, 1:2, :]
    lo = lax.broadcasted_iota(jnp.int32, (nb, LANES), 1) < HALF

    def body(k, carry):
        sre, sim = carry
        kr = jnp.where(k < nc_ctx, nc_ctx - 1 - k, n_chunks - 1 + nc_ctx - k)
        rf = pl.ds(k, nb, stride=n_chunks)
        rr = pl.ds(kr, nb, stride=n_chunks)
        s_scr[0, rf, :] = sre
        s_scr[1, rr, :] = sre
        s_scr[2, rf, :] = sim
        s_scr[3, rr, :] = sim
        vre = jnp.where(lo, v_scr[0, rf, :], v_scr[0, rr, :])
        vim = jnp.where(lo, v_scr[1, rf, :], v_scr[1, rr, :])
        return ar * sre - ai * sim + vre, ar * sim + ai * sre + vim

    zero = jnp.zeros((nb, LANES), F32)
    lax.fori_loop(0, n_chunks, body, (zero, zero))
    lo_all = lax.broadcasted_iota(jnp.int32, (s_scr.shape[1], LANES), 1) < HALF
    s_in = jnp.concatenate([jnp.where(lo_all, s_scr[0], s_scr[1]), jnp.where(lo_all, s_scr[2], s_scr[3])],
                           axis=1).astype(BF16)
    y_ref[0] = jnp.dot(xv, mintra_ref[0], preferred_element_type=F32) + _nt_dot(s_in, mstate_ref[0])


def _ssm_scan(u, uc, mats):
    m_in, m_intra, m_state, a_l = mats
    b, n, _ = u.shape
    nc = uc.shape[1]
    nc_ctx, nc_lat = nc // SSM_CHUNK, n // SSM_CHUNK
    n_chunks = nc_ctx + nc_lat
    r = n_chunks * b
    nat = lambda m: pl.BlockSpec((1, m, LANES), lambda i, s: (i, 0, s))
    grp = pl.BlockSpec((SSM_SLAB_GROUPS, n_chunks, SSM_CW), lambda i, s: (s, i, 0))
    xg = pl.pallas_call(
        functools.partial(_ssm_pack_kernel, nc_ctx=nc_ctx, nc_lat=nc_lat),
        out_shape=_sds((SSM_GROUPS, r, SSM_CW), BF16),
        grid=(b, SSM_GROUPS // SSM_SLAB_GROUPS),
        in_specs=[nat(nc), nat(n)],
        out_specs=grp,
        compiler_params=_cparams("parallel", "parallel"),
        name="ssm_pack",
    )(uc, u)
    mat = pl.BlockSpec((1, SSM_CW, SSM_CW), lambda g: (g, 0, 0))
    yg = pl.pallas_call(
        functools.partial(_ssm_kernel, nb=b, nc_ctx=nc_ctx, n_chunks=n_chunks),
        out_shape=_sds((SSM_GROUPS, r, SSM_CW), F32),
        grid=(SSM_GROUPS,),
        in_specs=[pl.BlockSpec((1, r, SSM_CW), lambda g: (g, 0, 0)), mat, mat, mat,
                  pl.BlockSpec((1, 2, LANES), lambda g: (g, 0, 0))],
        out_specs=pl.BlockSpec((1, r, SSM_CW), lambda g: (g, 0, 0)),
        scratch_shapes=[pltpu.VMEM((2, r, LANES), F32), pltpu.VMEM((4, r, LANES), F32)],
        compiler_params=_cparams("parallel"),
        name="ssm_scan",
    )(xg, m_in, m_intra, m_state, a_l)
    ysc, ys = pl.pallas_call(
        functools.partial(_ssm_unpack_kernel, nc_ctx=nc_ctx, nc_lat=nc_lat),
        out_shape=(_sds((b, nc, SSM_WIDTH), F32), _sds((b, n, SSM_WIDTH), F32)),
        grid=(b, SSM_GROUPS // SSM_SLAB_GROUPS),
        in_specs=[grp],
        out_specs=(nat(nc), nat(n)),
        compiler_params=_cparams("parallel", "parallel"),
        name="ssm_unpack",
    )(yg)
    return ys, ysc


def _split_rows(ref, val):
    m = val.shape[0]
    for s in range(TOKEN_TILE_ROWS):
        ref[pl.ds(s, m, stride=TOKEN_TILE_ROWS), :] = val[:, s * LANES:(s + 1) * LANES]


def _merge_rows(ref):
    m = ref.shape[0] // TOKEN_TILE_ROWS
    return jnp.concatenate([ref[pl.ds(s, m, stride=TOKEN_TILE_ROWS), :] for s in range(TOKEN_TILE_ROWS)], axis=1)


def _token_tile(ref, t):
    return ref.at[pl.ds(pl.multiple_of(t * TOKEN_TILE_ROWS, TOKEN_TILE_ROWS), TOKEN_TILE_ROWS)]


def _router_logits(w, bias, h):
    w_hi, h_hi = w.astype(BF16), h.astype(BF16)
    w_lo, h_lo = (w - w_hi.astype(F32)).astype(BF16), (h - h_hi.astype(F32)).astype(BF16)
    return _nt_dot(w_hi, h_hi) + (_nt_dot(w_hi, h_lo) + _nt_dot(w_lo, h_hi)) + bias


def _post_kernel(*refs, alpha, with_ssm):
    wr_ref, br_ref = refs[-5:-3]
    refs = refs[:-5] + refs[-3:]
    if with_ssm:
        (att_ref, ys_ref, u_ref, dsk_ref, gw_ref, gb_ref, wo_ref, x_ref, g1_ref, lg_ref, lb_ref, sc2_ref, sh2_ref,
         x1_ref, h2_ref, lgt_ref) = refs
        y = ys_ref[0] + u_ref[0] * dsk_ref[...]
        gl = jax.nn.gelu(y)
        gate = jax.nn.sigmoid(jnp.dot(gl.astype(BF16), gw_ref[...], preferred_element_type=F32) + gb_ref[...])
        ssm = (gl * gate).astype(BF16)
        mix = (jnp.dot(att_ref[0], wo_ref[0:SWA_Q_W, :], preferred_element_type=F32)
               + jnp.dot(ssm, wo_ref[SWA_Q_W:, :], preferred_element_type=F32))
    else:
        att_ref, wo_ref, x_ref, g1_ref, lg_ref, lb_ref, sc2_ref, sh2_ref, x1_ref, h2_ref, lgt_ref = refs
        mix = jnp.dot(att_ref[0], wo_ref[...], preferred_element_type=F32)
    x1 = _layer_norm(alpha * x_ref[0] + g1_ref[0] * mix, lg_ref[...], lb_ref[...])
    x1_ref[0] = x1
    h2 = x1 * (1.0 + sc2_ref[0]) + sh2_ref[0]
    _split_rows(h2_ref, h2)
    lgt_ref[...] = _router_logits(wr_ref[...], br_ref[...], h2)


def _post(att, ssm_args, w_out_bf16, x, g1, ln_g, ln_b, sc2, sh2, router_wb, alpha, tm):
    b, n, d = x.shape
    tok = lambda w: pl.BlockSpec((1, tm, w), lambda i, j: (i, j, 0))
    const = lambda a: pl.BlockSpec(a.shape, lambda i, j: (0,) * a.ndim)
    in_specs = [tok(att.shape[-1])]
    args = [att]
    if ssm_args is not None:
        ys, u, dsk, gw, gb = ssm_args
        in_specs += [tok(SSM_WIDTH), tok(SSM_WIDTH), const(dsk), const(gw), const(gb)]
        args += [ys, u, dsk, gw, gb]
    in_specs += [const(w_out_bf16), tok(d), g1.spec(), const(ln_g), const(ln_b), sc2.spec(), sh2.spec()]
    args += [w_out_bf16, x, g1.table, ln_g, ln_b, sc2.table, sh2.table]
    in_specs += [const(a) for a in router_wb]
    args += list(router_wb)
    per_b = n // tm
    n_logit = router_wb[0].shape[0]
    return pl.pallas_call(
        functools.partial(_post_kernel, alpha=alpha, with_ssm=ssm_args is not None),
        out_shape=(_sds((b, n, d), F32), _sds((b * n * TOKEN_TILE_ROWS, LANES), F32), _sds((n_logit, b * n), F32)),
        grid=(b, n // tm),
        in_specs=in_specs,
        out_specs=(tok(d), pl.BlockSpec((tm * TOKEN_TILE_ROWS, LANES), lambda i, j: (i * per_b + j, 0)),
                   pl.BlockSpec((n_logit, tm), lambda i, j: (0, i * per_b + j))),
        compiler_params=_cparams("parallel", "parallel"),
        name="post_mixer_ssm" if ssm_args is not None else "post_mixer",
    )(*args)


def _first_max(v, sub):
    m = jnp.max(v, 0, keepdims=True)
    idx = jnp.min(jnp.where(v == m, sub, float(SUBLANES)), 0, keepdims=True)
    return m, idx


def _stream_blocks(toks, tb):
    starts = [0]
    for t in toks:
        starts.append(starts[-1] + t.shape[0] // (tb * TOKEN_TILE_ROWS))
    return starts


def _stream_spec(tok, tb, start):
    last = tok.shape[0] // (tb * TOKEN_TILE_ROWS) - 1
    return pl.BlockSpec((tb * TOKEN_TILE_ROWS, LANES), lambda i, *_: (jnp.clip(i - start, 0, last), 0))


def _router_kernel(*refs, starts):
    n_streams = len(starts) - 1
    lgt_refs = refs[:n_streams]
    ids_ref, wts_ref, rank_ref, cnt_ref, carry_scr = refs[n_streams:]
    step = pl.program_id(0)

    @pl.when(step == 0)
    def _():
        carry_scr[...] = jnp.zeros_like(carry_scr)

    logits = lgt_refs[0][...]
    for ref, start in zip(lgt_refs[1:], starts[1:]):
        logits = jnp.where(step >= start, ref[...], logits)
    tm = logits.shape[1]
    sub = lax.broadcasted_iota(jnp.int32, (SUBLANES, tm), 0).astype(F32)
    gl = logits[0:SUBLANES]
    gmax, gi = _first_max(gl, sub)
    gp = 1.0 / jnp.sum(jnp.exp(gl - gmax), 0, keepdims=True)
    le = logits[ROUTER_EXPERT_ROW0:ROUTER_EXPERT_ROW0 + MOE_EPG]
    for g in range(1, MOE_GROUPS):
        le = jnp.where(gi == float(g), logits[ROUTER_EXPERT_ROW0 + g * MOE_EPG:ROUTER_EXPERT_ROW0 + (g + 1) * MOE_EPG], le)
    m1, i1 = _first_max(le, sub)
    m2, i2 = _first_max(jnp.where(sub == i1, NEG_INF, le), sub)
    t = jnp.exp(m2 - m1)
    e1 = gi * float(MOE_EPG) + i1
    e2 = gi * float(MOE_EPG) + i2
    ids_ref[0:1, :] = e1.astype(jnp.int32)
    ids_ref[1:2, :] = e2.astype(jnp.int32)
    wts_ref[0:1, :] = gp / (1.0 + t)
    wts_ref[1:2, :] = gp * t / (1.0 + t)
    esub = lax.broadcasted_iota(jnp.int32, (MOE_EXPERTS, tm), 0).astype(F32)
    oh1 = (esub == e1).astype(F32)
    oh2 = (esub == e2).astype(F32)
    both = oh1 + oh2
    earlier = (lax.broadcasted_iota(jnp.int32, (tm, tm), 0) < lax.broadcasted_iota(jnp.int32, (tm, tm), 1))
    prefix = jnp.dot(both.astype(BF16), earlier.astype(BF16), preferred_element_type=F32) + carry_scr[...]
    rank_ref[0:1, :] = jnp.sum(oh1 * prefix, 0, keepdims=True).astype(jnp.int32)
    rank_ref[1:2, :] = jnp.sum(oh2 * prefix, 0, keepdims=True).astype(jnp.int32)
    carry_scr[...] += jnp.sum(both, 1, keepdims=True)
    cnt_ref[...] = jnp.broadcast_to(carry_scr[...], cnt_ref.shape)


def _router_weights(wg, bg, we, be):
    d = wg.shape[0]
    rows = ROUTER_EXPERT_ROW0 + MOE_EXPERTS
    w = jnp.zeros((rows, d), F32)
    w = w.at[:MOE_GROUPS].set(wg.T)
    w = w.at[ROUTER_EXPERT_ROW0:].set(jnp.transpose(we, (0, 2, 1)).reshape(MOE_EXPERTS, d))
    bias = jnp.full((rows, 1), NEG_INF, F32)
    bias = bias.at[:MOE_GROUPS, 0].set(bg)
    bias = bias.at[ROUTER_EXPERT_ROW0:, 0].set(be.reshape(-1))
    return w, bias


def _router(logit_streams, tm):
    starts = [0]
    for lg in logit_streams:
        starts.append(starts[-1] + lg.shape[1] // tm)
    t = starts[-1] * tm
    rows = logit_streams[0].shape[0]
    stream_spec = lambda lg, s0: pl.BlockSpec(
        (rows, tm), lambda i: (0, jnp.clip(i - s0, 0, lg.shape[1] // tm - 1)))
    pair = pl.BlockSpec((MOE_TOPK, tm), lambda i: (0, i))
    ids, wts, rank, cnt = pl.pallas_call(
        functools.partial(_router_kernel, starts=starts),
        out_shape=(_sds((MOE_TOPK, t), jnp.int32), _sds((MOE_TOPK, t), F32), _sds((MOE_TOPK, t), jnp.int32),
                   _sds((MOE_EXPERTS, LANES), F32)),
        grid=(t // tm,),
        in_specs=[stream_spec(lg, s0) for lg, s0 in zip(logit_streams, starts)],
        out_specs=(pair, pair, pair, pl.BlockSpec((MOE_EXPERTS, LANES), lambda i: (0, 0))),
        scratch_shapes=[pltpu.VMEM((MOE_EXPERTS, 1), F32)],
        compiler_params=_cparams("arbitrary"),
        name="moe_router",
    )(*logit_streams)
    return ids, wts, rank, cnt[:, 0].astype(jnp.int32)


def _moe_plan(ids, rank, counts, n_tok):
    tm = MOE_ROW_TILE
    padded = ((counts + tm - 1) // tm) * tm
    ends = jnp.cumsum(padded)
    offs = ends - padded
    experts = jnp.arange(MOE_EXPERTS, dtype=jnp.int32)
    dest = (jnp.sum(jnp.where(ids[..., None] == experts, offs, 0), -1) + rank).astype(jnp.int32)
    n_tiles = (MOE_TOPK * n_tok + MOE_EXPERTS * (tm - 1)) // tm
    starts = jnp.arange(n_tiles, dtype=jnp.int32) * tm
    tile_expert = jnp.minimum(jnp.sum((ends[None, :] <= starts[:, None]).astype(jnp.int32), -1), MOE_EXPERTS - 1)
    n_valid = (ends[-1] // tm).astype(jnp.int32).reshape(1)
    pad = jnp.stack([offs + counts, padded - counts, jnp.broadcast_to(n_valid, counts.shape)]).astype(jnp.int32)
    return dest, tile_expert, n_valid, pad, n_tiles * tm


def _pad_fill(pad_ref, zero_scr, xs_ref, sem, wait):
    def per_expert(e, carry):
        first, count = pad_ref[0, e], pad_ref[1, e]
        piece = MOE_ROW_TILE // 2
        while piece >= 1:
            row = first + (count & ~(2 * piece - 1))
            n = piece * TOKEN_TILE_ROWS

            @pl.when((count & piece) != 0)
            def _(row=row, n=n):
                copy = pltpu.make_async_copy(
                    zero_scr.at[pl.ds(0, n)],
                    xs_ref.at[pl.ds(pl.multiple_of(row * TOKEN_TILE_ROWS, TOKEN_TILE_ROWS), n)], sem)
                copy.wait() if wait else copy.start()

            piece //= 2
        return carry

    lax.fori_loop(0, MOE_EXPERTS, per_expert, 0)
    half = MOE_ROW_TILE // 2 * TOKEN_TILE_ROWS
    n_tiles = xs_ref.shape[0] // (2 * half)

    def per_tile(t, carry):
        for h in range(2):
            copy = pltpu.make_async_copy(zero_scr, xs_ref.at[pl.ds(pl.multiple_of((2 * t + h) * half, half), half)],
                                         sem)
            copy.wait() if wait else copy.start()
        return carry

    lax.fori_loop(pad_ref[2, 0], n_tiles, per_tile, 0)


def _dispatch_kernel(*refs, tb, starts):
    n_streams = len(starts) - 1
    dest_ref, pad_ref = refs[:2]
    tok_refs = refs[2:2 + n_streams]
    xs_ref, zero_scr, sem, pad_sem = refs[2 + n_streams:]
    step = pl.program_id(0)

    @pl.when(step == 0)
    def _():
        zero_scr[...] = jnp.zeros_like(zero_scr)
        _pad_fill(pad_ref, zero_scr, xs_ref, pad_sem, wait=False)

    for s, tok_ref in enumerate(tok_refs):
        @pl.when((step >= starts[s]) & (step < starts[s + 1]))
        def _(tok_ref=tok_ref):
            def body(r, carry):
                for k in range(MOE_TOPK):
                    pltpu.make_async_copy(_token_tile(tok_ref, r), _token_tile(xs_ref, dest_ref[0, k, r]), sem).start()
                return carry

            lax.fori_loop(0, tb, body, 0, unroll=8)

    for k in range(MOE_TOPK):
        pltpu.make_async_copy(tok_refs[0], xs_ref.at[pl.ds(0, tb * TOKEN_TILE_ROWS)], sem).wait()

    @pl.when(step == 0)
    def _():
        _pad_fill(pad_ref, zero_scr, xs_ref, pad_sem, wait=True)


def _dispatch(toks, dest3, pad, n_rows):
    nblk, _, tb = dest3.shape
    starts = _stream_blocks(toks, tb)
    return pl.pallas_call(
        functools.partial(_dispatch_kernel, tb=tb, starts=starts),
        out_shape=_sds((n_rows * TOKEN_TILE_ROWS, LANES), F32),
        grid=(nblk,),
        in_specs=[pl.BlockSpec((1, MOE_TOPK, tb), lambda i: (i, 0, 0), memory_space=pltpu.SMEM),
                  pl.BlockSpec(memory_space=pltpu.SMEM)]
        + [_stream_spec(tok, tb, s0) for tok, s0 in zip(toks, starts)],
        out_specs=pl.BlockSpec(memory_space=pl.ANY),
        scratch_shapes=[pltpu.VMEM((MOE_ROW_TILE // 2 * TOKEN_TILE_ROWS, LANES), F32),
                        pltpu.SemaphoreType.DMA(()), pltpu.SemaphoreType.DMA(())],
        compiler_params=_cparams("arbitrary"),
        name="moe_dispatch",
    )(dest3, pad, *toks)


def _ffn_kernel(te_ref, nv_ref, x_ref, w1_ref, w3_ref, w2_ref, y_ref, w13_scr, w2_scr):
    i = pl.program_id(0)
    f = w1_ref.shape[2]

    @pl.when((i == 0) | (te_ref[i] != te_ref[jnp.maximum(i - 1, 0)]))
    def _():
        w13_scr[:, 0:f] = w1_ref[0].astype(BF16)
        w13_scr[:, f:2 * f] = w3_ref[0].astype(BF16)
        w2_scr[...] = w2_ref[0].astype(BF16)

    @pl.when(i < nv_ref[0])
    def _():
        h13 = jnp.dot(_merge_rows(x_ref).astype(BF16), w13_scr[...], preferred_element_type=F32)
        h1 = h13[:, 0:f]
        hh = (h1 * jax.nn.sigmoid(h1) * h13[:, f:2 * f]).astype(BF16)
        _split_rows(y_ref, jnp.dot(hh, w2_scr[...], preferred_element_type=F32))

    @pl.when(i >= nv_ref[0])
    def _():
        y_ref[...] = jnp.zeros_like(y_ref)


def _expert_ffn(tile_expert, n_valid, xs, w1, w3, w2):
    p = xs.shape[0] // TOKEN_TILE_ROWS
    _, d, f = w1.shape
    tm = MOE_ROW_TILE
    rows = pl.BlockSpec((tm * TOKEN_TILE_ROWS, LANES), lambda i, te, nv: (i, 0))
    rows_in = pl.BlockSpec((tm * TOKEN_TILE_ROWS, LANES), lambda i, te, nv: (jnp.minimum(i, nv[0] - 1), 0))
    return pl.pallas_call(
        _ffn_kernel,
        out_shape=_sds(xs.shape, F32),
        grid_spec=pltpu.PrefetchScalarGridSpec(
            num_scalar_prefetch=2,
            grid=(p // tm,),
            in_specs=[rows_in,
                      pl.BlockSpec((1, d, f), lambda i, te, nv: (te[i], 0, 0)),
                      pl.BlockSpec((1, d, f), lambda i, te, nv: (te[i], 0, 0)),
                      pl.BlockSpec((1, f, d), lambda i, te, nv: (te[i], 0, 0))],
            out_specs=rows,
            scratch_shapes=[pltpu.VMEM((d, 2 * f), BF16), pltpu.VMEM((f, d), BF16)]),
        compiler_params=_cparams("arbitrary"),
        name="moe_expert_ffn",
    )(tile_expert, n_valid, xs, w1, w3, w2)


def _combine_ln2_kernel(dest_ref, next_ref, x1_ref, ys_ref, wt_ref, g2_ref, lg_ref, lb_ref, o_ref, buf, sem, *,
                        alpha, tb, nblk):
    step = pl.program_id(0)

    def gather(d_ref, slot):
        def body(r, carry):
            for k in range(MOE_TOPK):
                pltpu.make_async_copy(_token_tile(ys_ref, d_ref[0, k, r]), _token_tile(buf.at[slot, k], r),
                                      sem.at[slot, k]).start()
            return carry

        lax.fori_loop(0, tb, body, 0, unroll=8)

    @pl.when(step == 0)
    def _():
        gather(dest_ref, 0)

    @pl.when(step + 1 < nblk)
    def _():
        gather(next_ref, (step + 1) % 2)

    slot = step % 2
    for k in range(MOE_TOPK):
        pltpu.make_async_copy(ys_ref.at[pl.ds(0, tb * TOKEN_TILE_ROWS)], buf.at[slot, k], sem.at[slot, k]).wait()
    f = wt_ref[:, 0:1] * _merge_rows(buf.at[slot, 0]) + wt_ref[:, 1:2] * _merge_rows(buf.at[slot, 1])
    o_ref[...] = _layer_norm(alpha * x1_ref[...] + g2_ref[0] * f, lg_ref[...], lb_ref[...])


def _combine_ln2(dest3, x1, ys, wt, g2, ln_g, ln_b, alpha, n_per_sample):
    t, d = x1.shape
    nblk, _, tb = dest3.shape
    const = pl.BlockSpec((1, d), lambda i: (0, 0))
    assert g2.stride == 0 or n_per_sample % tb == 0, (n_per_sample, tb)
    return pl.pallas_call(
        functools.partial(_combine_ln2_kernel, alpha=alpha, tb=tb, nblk=nblk),
        out_shape=_sds((t, d), F32),
        grid=(nblk,),
        in_specs=[pl.BlockSpec((1, MOE_TOPK, tb), lambda i: (i, 0, 0), memory_space=pltpu.SMEM),
                  pl.BlockSpec((1, MOE_TOPK, tb), lambda i: (jnp.minimum(i + 1, nblk - 1), 0, 0),
                               memory_space=pltpu.SMEM),
                  pl.BlockSpec((tb, d), lambda i: (i, 0)),
                  pl.BlockSpec(memory_space=pl.ANY),
                  pl.BlockSpec((tb, MOE_TOPK), lambda i: (i, 0)),
                  g2.spec(lambda i: (i * tb) // n_per_sample),
                  const, const],
        out_specs=pl.BlockSpec((tb, d), lambda i: (i, 0)),
        scratch_shapes=[pltpu.VMEM((2, MOE_TOPK, tb * TOKEN_TILE_ROWS, LANES), F32),
                        pltpu.SemaphoreType.DMA((2, MOE_TOPK))],
        compiler_params=_cparams("arbitrary"),
        name="moe_combine_ln2",
    )(dest3, dest3, x1, ys, wt, g2.table, ln_g, ln_b)


def _proj1_kernel(*refs, rope, with_q):
    x_ref, sc_ref, sh_ref, w_ref = refs[:4]
    refs = refs[4:]
    tabs = None
    if rope:
        tabs = (refs[0][...], refs[1][...], refs[2][...])
        refs = refs[3:]
    h = (x_ref[0] * (1.0 + sc_ref[0]) + sh_ref[0]).astype(BF16)
    r = jnp.dot(h, w_ref[...], preferred_element_type=F32)
    off = 0
    if with_q:
        q_ref, k_ref, v_ref = refs
        for hd in range(DIF_HEADS):
            q_ref[0, hd] = (_rot(r[:, hd * LANES:(hd + 1) * LANES], tabs) * (LOG2_E * HEAD_DIM ** -0.5)).astype(BF16)
        off = DIF_QK_W
    else:
        k_ref, v_ref = refs
    for hd in range(DIF_HEADS):
        k_ref[0, hd] = _rot(r[:, off + hd * LANES:off + (hd + 1) * LANES], tabs).astype(BF16)
        v_ref[0, hd] = r[:, off + DIF_QK_W + hd * LANES:off + DIF_QK_W + (hd + 1) * LANES].astype(BF16)


def _proj1(x, sc, sh, w_bf16, tabs, with_q, tm):
    b, n, d = x.shape
    rope = tabs is not None
    in_specs = [pl.BlockSpec((1, tm, d), lambda i, j: (i, j, 0)), sc.spec(), sh.spec(),
                pl.BlockSpec(w_bf16.shape, lambda i, j: (0, 0))]
    args = [x, sc.table, sh.table, w_bf16]
    if rope:
        in_specs += [pl.BlockSpec((tm, LANES), lambda i, j: (j, 0))] * 3
        args += list(tabs)
    hm = pl.BlockSpec((1, DIF_HEADS, tm, LANES), lambda i, j: (i, 0, j, 0))
    n_out = 3 if with_q else 2
    return pl.pallas_call(
        functools.partial(_proj1_kernel, rope=rope, with_q=with_q),
        out_shape=(_sds((b, DIF_HEADS, n, LANES), BF16),) * n_out,
        grid=(b, n // tm),
        in_specs=in_specs,
        out_specs=(hm,) * n_out,
        compiler_params=_cparams("parallel", "parallel"),
        name="proj1_qkv" if with_q else "proj1_kv_ctx",
    )(*args)


def _diff_kernel(lam_ref, q_ref, kl_ref, kc_ref, vl_ref, vc_ref, g_ref, o_ref, k_scr, v_scr, *, tq, n_lat, out_scale):
    @pl.when(pl.program_id(2) == 0)
    def _():
        k_scr[0:n_lat] = kl_ref[0, 0]
        k_scr[n_lat:] = kc_ref[0, 0]
        v_scr[0:n_lat, 0:LANES] = vl_ref[0, 0]
        v_scr[n_lat:, 0:LANES] = vc_ref[0, 0]
        v_scr[:, LANES:] = jnp.ones((v_scr.shape[0], LANES), BF16)

    lo = lax.broadcasted_iota(jnp.int32, (DIF_CHAIN_ROWS, LANES), 1) < HALF
    for r0 in range(0, tq, DIF_CHAIN_ROWS):
        q = q_ref[0, 0, r0:r0 + DIF_CHAIN_ROWS, :].astype(F32)
        maps = []
        for qm in (jnp.where(lo, q, 0.0), jnp.where(lo, 0.0, q)):
            s = _nt_dot(qm.astype(BF16), k_scr[...])
            p = jnp.exp2(s - jnp.max(s, -1, keepdims=True)).astype(BF16)
            oe = jnp.dot(p, v_scr[...], preferred_element_type=F32)
            maps.append(oe[:, 0:LANES] * (1.0 / oe[:, LANES:]))
        o = maps[0] - lam_ref[0] * maps[1]
        o = o * lax.rsqrt(jnp.mean(o * o, -1, keepdims=True) + RMS_EPS) * g_ref[...]
        o_ref[0, r0:r0 + DIF_CHAIN_ROWS, :] = (o * out_scale).astype(BF16)


def _diff_attention(lam, q, kl, kc, vl, vc, subln_g, lam_init, tq):
    b, nh, n, _ = q.shape
    nc = kc.shape[2]
    kv = lambda m: pl.BlockSpec((1, 1, m, LANES), lambda i, h, j: (i, h, 0, 0))
    return pl.pallas_call(
        functools.partial(_diff_kernel, tq=tq, n_lat=n, out_scale=1.0 - lam_init),
        out_shape=_sds((b, n, nh * LANES), BF16),
        grid=(b, nh, n // tq),
        in_specs=[pl.BlockSpec(memory_space=pltpu.SMEM),
                  pl.BlockSpec((1, 1, tq, LANES), lambda i, h, j: (i, h, j, 0)),
                  kv(n), kv(nc), kv(n), kv(nc),
                  pl.BlockSpec((1, LANES), lambda i, h, j: (0, 0))],
        out_specs=pl.BlockSpec((1, tq, LANES), lambda i, h, j: (i, j, h)),
        scratch_shapes=[pltpu.VMEM((n + nc, LANES), BF16), pltpu.VMEM((n + nc, 2 * LANES), BF16)],
        compiler_params=_cparams("parallel", "parallel", "arbitrary"),
        name="diff_attention",
    )(lam, q, kl, kc, vl, vc, subln_g)


def _moe_block(toks, logit_streams, layer, moe_w1, moe_w3, moe_w2):
    t = sum(tok.shape[0] for tok in toks) // TOKEN_TILE_ROWS
    tb = MOE_DMA_TILE
    ids, wts, rank, counts = _router(logit_streams, tm=512)
    dest, tile_expert, n_valid, pad, n_rows = _moe_plan(ids, rank, counts, t)
    dest3 = dest.reshape(MOE_TOPK, t // tb, tb).transpose(1, 0, 2)
    xs = _dispatch(toks, dest3, pad, n_rows)
    flat = lambda w: w.reshape((-1,) + w.shape[2:])
    ys = _expert_ffn(tile_expert + layer * MOE_EXPERTS, n_valid, xs, flat(moe_w1), flat(moe_w3), flat(moe_w2))
    return ys, dest3, wts.T


def kernel(x, c, ctx, c_ctx, mod_w, mod_b, ln1_g, ln1_b, ln2_g, ln2_b, swa_ssm_w_in, swa_ssm_w_out, swa_sink, ssm_a_re, ssm_a_im, ssm_log_step, ssm_b_re, ssm_b_im, ssm_c_re, ssm_c_im, ssm_d, ssm_glu_w, ssm_glu_b, dif_w_in, dif_w_out, dif_lam_q1, dif_lam_k1, dif_lam_q2, dif_lam_k2, dif_subln_g, moe_wg, moe_bg, moe_we, moe_be, moe_w1, moe_w3, moe_w2):
    bsz, n, d = x.shape
    ctx_len = ctx.shape[1]
    depth = mod_w.shape[0]
    alpha = (2 * depth) ** 0.25
    tabs = _rope_tables(n)

    n_vec = 16
    cvec = jnp.zeros((n_vec, d), F32).at[:bsz].set(c).at[bsz].set(c_ctx)
    mods = _modulation(cvec, mod_w, mod_b)
    mod_table = mods.reshape(depth * n_vec * 6, 1, d)

    xl, xc = x, ctx
    for layer in range(depth):
        need_ctx = layer < depth - 1
        i = layer // 2
        lat = [ModVec(mod_table, (layer * n_vec) * 6 + k, 6) for k in range(6)]
        cx = [ModVec(mod_table, (layer * n_vec + bsz) * 6 + k, 0) for k in range(6)]
        sh1, sc1, g1, sh2, sc2, g2 = lat
        csh1, csc1, cg1, csh2, csc2, cg2 = cx
        lg1, lb1 = ln1_g[layer].reshape(1, d), ln1_b[layer].reshape(1, d)
        lg2, lb2 = ln2_g[layer].reshape(1, d), ln2_b[layer].reshape(1, d)
        router_wb = _router_weights(moe_wg[layer], moe_bg[layer], moe_we[layer], moe_be[layer])
        if layer % 2 == 0:
            w_in = swa_ssm_w_in[i].astype(BF16)
            w_out = swa_ssm_w_out[i].astype(BF16)
            q, k, v, u = _proj0(xl, sc1, sh1, w_in, tabs, tm=512)
            qc, kc, vc, uc = _proj0(xc, csc1, csh1, w_in, None, tm=ctx_len)
            sink = swa_sink[i].astype(F32)
            att = _swa_attention(sink, q, k, v, kc, vc)
            mats = _ssm_matrices(ssm_a_re[i], ssm_a_im[i], ssm_log_step[i], ssm_b_re[i], ssm_b_im[i],
                                 ssm_c_re[i], ssm_c_im[i])
            ys, ysc = _ssm_scan(u, uc, mats)
            glu = (ssm_d[i].reshape(1, SSM_WIDTH).astype(F32), ssm_glu_w[i].astype(BF16),
                   ssm_glu_b[i].reshape(1, SSM_WIDTH).astype(F32))
            x1, tok, lgt = _post(att, (ys, u) + glu, w_out, xl, g1, lg1, lb1, sc2, sh2, router_wb, alpha, 512)
            toks, logit_streams = [tok], [lgt]
            if need_ctx:
                att_c = _ctx_attention(sink, qc, kc, vc)
                xc1, tok_c, lgt_c = _post(att_c, (ysc, uc) + glu, w_out, xc, cg1, lg1, lb1, csc2, csh2, router_wb,
                                          alpha, ctx_len)
                toks.append(tok_c)
                logit_streams.append(lgt_c)
        else:
            lam_init = 0.8 - 0.6 * math.exp(-0.3 * layer)
            w_in = dif_w_in[i].astype(BF16)
            w_out = dif_w_out[i].astype(BF16)
            q, k, v = _proj1(xl, sc1, sh1, w_in, tabs, True, tm=512)
            kc, vc = _proj1(xc, csc1, csh1, w_in[:, DIF_QK_W:], None, False, tm=ctx_len)
            lam = (jnp.exp(jnp.sum(dif_lam_q1[i].astype(F32) * dif_lam_k1[i].astype(F32)))
                   - jnp.exp(jnp.sum(dif_lam_q2[i].astype(F32) * dif_lam_k2[i].astype(F32))) + lam_init).reshape(1)
            att = _diff_attention(lam, q, k, kc, v, vc, dif_subln_g[i].reshape(1, DIF_V_HEAD).astype(F32),
                                  lam_init, tq=n)
            x1, tok, lgt = _post(att, None, w_out, xl, g1, lg1, lb1, sc2, sh2, router_wb, alpha, 512)
            toks, logit_streams = [tok], [lgt]
            if need_ctx:
                raise NotImplementedError("a differential-attention layer followed by another layer")
        ys_moe, dest3, wt = _moe_block(toks, logit_streams, layer, moe_w1, moe_w3, moe_w2)
        n_lat_blk = bsz * n // MOE_DMA_TILE
        xl = _combine_ln2(dest3[:n_lat_blk], x1.reshape(-1, d), ys_moe, wt[:bsz * n], g2, lg2, lb2, alpha,
                          n).reshape(bsz, n, d)
        if need_ctx:
            xc = _combine_ln2(dest3[n_lat_blk:], xc1.reshape(-1, d), ys_moe, wt[bsz * n:], cg2, lg2, lb2, alpha,
                              ctx_len).reshape(bsz, ctx_len, d)
    return xl
```

```python
import functools
import math
import typing

import jax
import jax.numpy as jnp
from jax import lax
from jax.experimental import pallas as pl
from jax.experimental.pallas import tpu as pltpu

F32 = jnp.float32
BF16 = jnp.bfloat16
HIGHEST = lax.Precision.HIGHEST

D_MODEL = 1024
GRID_W = 64
HEAD_DIM = 64
ROPE_BASE = 10000.0
ROPE_FREQS = HEAD_DIM // 4
LN_EPS = 1e-5
RMS_EPS = 1e-5
NEG_INF = -1e30
LOG2_E = math.log2(math.e)
LANES = 128
HALF = LANES // 2

SWA_HEADS = 8
SWA_KV_HEADS = 2
SWA_WINDOW = 128
SWA_BLOCK = 128
SWA_BLOCKS_PER_STEP = 4
SWA_Q_W = SWA_HEADS * HEAD_DIM
SWA_KV_W = SWA_KV_HEADS * HEAD_DIM

SSM_WIDTH = D_MODEL // 2
SSM_GROUP = 16
SSM_GROUPS = SSM_WIDTH // SSM_GROUP
SSM_STATE = 64
SSM_CHUNK = 16
SSM_CW = SSM_CHUNK * SSM_GROUP
SSM_SLAB_GROUPS = LANES // SSM_GROUP
SSM_GROUPS_PER_STEP = 4

AB_IN_W = SWA_Q_W + 2 * SWA_KV_W + SSM_WIDTH

DIF_HEADS = D_MODEL // (2 * HEAD_DIM)
DIF_QK_W = DIF_HEADS * 2 * HEAD_DIM
DIF_V_HEAD = 2 * HEAD_DIM
DIF_V_W = DIF_HEADS * DIF_V_HEAD
DIF_CHAIN_ROWS = 128

MOE_GROUPS = 4
MOE_EPG = 8
MOE_EXPERTS = MOE_GROUPS * MOE_EPG
MOE_HIDDEN = D_MODEL // 4
MOE_TOPK = 2
SUBLANES = 8
ROUTER_EXPERT_ROW0 = SUBLANES
MOE_ROW_TILE = 256
MOE_DMA_TILE = 512
TOKEN_TILE_ROWS = D_MODEL // LANES

VMEM_LIMIT = 56 * 1024 * 1024


def _cparams(*sem):
    return pltpu.CompilerParams(dimension_semantics=sem, vmem_limit_bytes=VMEM_LIMIT)


def _sds(shape, dtype):
    return jax.ShapeDtypeStruct(shape, dtype)


def _nt_dot(a, b):
    return lax.dot_general(a, b, (((1,), (1,)), ((), ())), preferred_element_type=F32)


def _layer_norm(r, g, b):
    mu = jnp.mean(r, -1, keepdims=True)
    rc = r - mu
    var = jnp.mean(rc * rc, -1, keepdims=True)
    return rc * lax.rsqrt(var + LN_EPS) * g + b


class ModVec(typing.NamedTuple):
    table: jax.Array
    row0: int
    stride: int

    def spec(self, sample_of_step=lambda i, *_: i):
        d = self.table.shape[-1]
        return pl.BlockSpec((1, 1, d), lambda *idx: (self.row0 + self.stride * sample_of_step(*idx), 0, 0))


def _mod_kernel(c_ref, w_ref, b_ref, o_ref):
    cv = c_ref[...]
    s = cv * jax.nn.sigmoid(cv)
    o_ref[0] = jnp.dot(s, w_ref[0], preferred_element_type=F32, precision=HIGHEST) + b_ref[0]


def _modulation(cvec, mod_w, mod_b):
    depth, d, w6 = mod_w.shape
    tn = 1536
    return pl.pallas_call(
        _mod_kernel,
        out_shape=_sds((depth, cvec.shape[0], w6), F32),
        grid=(depth, w6 // tn),
        in_specs=[pl.BlockSpec(cvec.shape, lambda l, j: (0, 0)),
                  pl.BlockSpec((1, d, tn), lambda l, j: (l, 0, j)),
                  pl.BlockSpec((1, 1, tn), lambda l, j: (l, 0, j))],
        out_specs=pl.BlockSpec((1, cvec.shape[0], tn), lambda l, j: (l, 0, j)),
        compiler_params=_cparams("arbitrary", "arbitrary"),
        name="modulation",
    )(cvec, mod_w, mod_b.reshape(depth, 1, w6))


def _rope_tables(n):
    rows = n // GRID_W
    row = jnp.repeat(jnp.arange(rows, dtype=F32), GRID_W)
    col = jnp.tile(jnp.arange(GRID_W, dtype=F32), rows)
    inv = ROPE_BASE ** (-jnp.arange(ROPE_FREQS, dtype=F32) / ROPE_FREQS)
    ang_r = row[:, None] * inv[None, :]
    ang_c = col[:, None] * inv[None, :]
    zeros = jnp.zeros_like(ang_r)
    cos64 = jnp.concatenate([jnp.cos(ang_r), jnp.cos(ang_r), jnp.cos(ang_c), jnp.cos(ang_c)], -1)
    sa64 = jnp.concatenate([-jnp.sin(ang_r), zeros, -jnp.sin(ang_c), zeros], -1)
    sb64 = jnp.concatenate([zeros, jnp.sin(ang_r), zeros, jnp.sin(ang_c)], -1)
    return tuple(jnp.tile(t, (1, LANES // HEAD_DIM)) for t in (cos64, sa64, sb64))


def _rot(t, tabs):
    if tabs is None:
        return t
    cos, sa, sb = tabs
    return t * cos + pltpu.roll(t, LANES - ROPE_FREQS, 1) * sa + pltpu.roll(t, ROPE_FREQS, 1) * sb


def _dup_halves(t):
    lo = lax.broadcasted_iota(jnp.int32, t.shape, 1) < HALF
    ta = jnp.where(lo, t, 0.0)
    tb = t - ta
    return ta + pltpu.roll(ta, HALF, 1), tb + pltpu.roll(tb, HALF, 1)


def _proj0_kernel(*refs, rope):
    if rope:
        x_ref, sc_ref, sh_ref, w_ref, cos_ref, sa_ref, sb_ref, q_ref, k_ref, v_ref, u_ref = refs
        tabs = (cos_ref[...], sa_ref[...], sb_ref[...])
    else:
        x_ref, sc_ref, sh_ref, w_ref, q_ref, k_ref, v_ref, u_ref = refs
        tabs = None
    h = (x_ref[0] * (1.0 + sc_ref[0]) + sh_ref[0]).astype(BF16)
    r = jnp.dot(h, w_ref[...], preferred_element_type=F32)
    scale = LOG2_E * HEAD_DIM ** -0.5
    for s in range(SWA_Q_W // LANES):
        q_ref[0, :, s * LANES:(s + 1) * LANES] = (_rot(r[:, s * LANES:(s + 1) * LANES], tabs) * scale).astype(BF16)
    k0, k1 = _dup_halves(_rot(r[:, SWA_Q_W:SWA_Q_W + LANES], tabs))
    vv = r[:, SWA_Q_W + LANES:SWA_Q_W + 2 * LANES]
    lo = lax.broadcasted_iota(jnp.int32, vv.shape, 1) < HALF
    k_ref[0, 0] = k0.astype(BF16)
    k_ref[0, 1] = k1.astype(BF16)
    v_ref[0, 0] = jnp.where(lo, vv, 1.0).astype(BF16)
    v_ref[0, 1] = jnp.where(lo, pltpu.roll(vv, HALF, 1), 1.0).astype(BF16)
    u_ref[0] = r[:, SWA_Q_W + 2 * LANES:]


def _proj0(x, sc, sh, w_bf16, tabs, tm):
    b, n, d = x.shape
    rope = tabs is not None
    in_specs = [pl.BlockSpec((1, tm, d), lambda i, j: (i, j, 0)), sc.spec(), sh.spec(),
                pl.BlockSpec(w_bf16.shape, lambda i, j: (0, 0))]
    args = [x, sc.table, sh.table, w_bf16]
    if rope:
        in_specs += [pl.BlockSpec((tm, LANES), lambda i, j: (j, 0))] * 3
        args += list(tabs)
    kv_spec = pl.BlockSpec((1, SWA_KV_HEADS, tm, LANES), lambda i, j: (i, 0, j, 0))
    return pl.pallas_call(
        functools.partial(_proj0_kernel, rope=rope),
        out_shape=(_sds((b, n, SWA_Q_W), BF16), _sds((b, SWA_KV_HEADS, n, LANES), BF16),
                   _sds((b, SWA_KV_HEADS, n, LANES), BF16), _sds((b, n, SSM_WIDTH), F32)),
        grid=(b, n // tm),
        in_specs=in_specs,
        out_specs=(pl.BlockSpec((1, tm, SWA_Q_W), lambda i, j: (i, j, 0)), kv_spec, kv_spec,
                   pl.BlockSpec((1, tm, SSM_WIDTH), lambda i, j: (i, j, 0))),
        compiler_params=_cparams("parallel", "parallel"),
        name="proj0_rope" if rope else "proj0_ctx",
    )(*args)


def _swa_kernel(*refs, tq, nsub, local, n_lat):
    if local:
        sink_ref, q_ref, k_ref, v_ref, kc_ref, vc_ref, o_ref = refs
    else:
        sink_ref, q_ref, kc_ref, vc_ref, o_ref = refs
    rows = 4 * tq
    lo = lax.broadcasted_iota(jnp.int32, (tq, LANES), 1) < HALF
    lo4 = lax.broadcasted_iota(jnp.int32, (rows, LANES), 1) < HALF
    rown = lax.broadcasted_iota(jnp.int32, (rows, 1), 0)
    for sub in range(nsub):
        j = pl.program_id(1) * nsub + sub
        r0 = sub * tq
        if local:
            span = 3 * SWA_BLOCK
            start = pl.multiple_of(jnp.clip((j - 1) * SWA_BLOCK, 0, n_lat - span), SWA_BLOCK)
            rr = lax.broadcasted_iota(jnp.int32, (rows, span), 0)
            cc = lax.broadcasted_iota(jnp.int32, (rows, span), 1)
            qpos = j * tq + (rr & (tq - 1))
            mask = jnp.abs(qpos - (start + cc)) <= SWA_WINDOW
        for h in range(SWA_KV_HEADS):
            qa = q_ref[0, r0:r0 + tq, (2 * h) * LANES:(2 * h + 1) * LANES].astype(F32)
            qb = q_ref[0, r0:r0 + tq, (2 * h + 1) * LANES:(2 * h + 2) * LANES].astype(F32)
            q4 = jnp.concatenate([jnp.where(lo, qa, 0.0), jnp.where(lo, 0.0, qa),
                                  jnp.where(lo, qb, 0.0), jnp.where(lo, 0.0, qb)], 0).astype(BF16)
            sink = LOG2_E * jnp.where(rown < tq, sink_ref[4 * h],
                                      jnp.where(rown < 2 * tq, sink_ref[4 * h + 1],
                                                jnp.where(rown < 3 * tq, sink_ref[4 * h + 2], sink_ref[4 * h + 3])))
            s_ctx = _nt_dot(q4, kc_ref[0, h])
            m = jnp.maximum(jnp.max(s_ctx, -1, keepdims=True), sink)
            if local:
                s_loc = jnp.where(mask, _nt_dot(q4, k_ref[0, h, pl.ds(start, span), :]), NEG_INF)
                m = jnp.maximum(m, jnp.max(s_loc, -1, keepdims=True))
            o4 = jnp.dot(jnp.exp2(s_ctx - m).astype(BF16), vc_ref[0, h], preferred_element_type=F32)
            if local:
                o4 = o4 + jnp.dot(jnp.exp2(s_loc - m).astype(BF16), v_ref[0, h, pl.ds(start, span), :],
                                  preferred_element_type=F32)
            o4 = o4 + jnp.where(lo4, 0.0, jnp.exp2(sink - m))
            o4 = o4 * (1.0 / jnp.where(lo4, pltpu.roll(o4, HALF, 1), 1.0))
            for s in range(2):
                even, odd = o4[2 * s * tq:(2 * s + 1) * tq], o4[(2 * s + 1) * tq:(2 * s + 2) * tq]
                o_ref[0, r0:r0 + tq, (2 * h + s) * LANES:(2 * h + s + 1) * LANES] = jnp.where(
                    lo, even, pltpu.roll(odd, HALF, 1)).astype(BF16)


def _swa_attention(sink, q, k, v, kc, vc):
    b, n, _ = q.shape
    nc = kc.shape[2]
    tq, nsub = SWA_BLOCK, SWA_BLOCKS_PER_STEP
    full = lambda m: pl.BlockSpec((1, SWA_KV_HEADS, m, LANES), lambda i, j: (i, 0, 0, 0))
    return pl.pallas_call(
        functools.partial(_swa_kernel, tq=tq, nsub=nsub, local=True, n_lat=n),
        out_shape=_sds((b, n, SWA_Q_W), BF16),
        grid=(b, n // (tq * nsub)),
        in_specs=[pl.BlockSpec(memory_space=pltpu.SMEM),
                  pl.BlockSpec((1, tq * nsub, SWA_Q_W), lambda i, j: (i, j, 0)),
                  full(n), full(n), full(nc), full(nc)],
        out_specs=pl.BlockSpec((1, tq * nsub, SWA_Q_W), lambda i, j: (i, j, 0)),
        compiler_params=_cparams("parallel", "arbitrary"),
        name="swa_attention",
    )(sink, q, k, v, kc, vc)


def _ctx_attention(sink, qc, kc, vc):
    b, nc, _ = qc.shape
    full = pl.BlockSpec((1, SWA_KV_HEADS, nc, LANES), lambda i, j: (i, 0, 0, 0))
    return pl.pallas_call(
        functools.partial(_swa_kernel, tq=nc, nsub=1, local=False, n_lat=0),
        out_shape=_sds((b, nc, SWA_Q_W), BF16),
        grid=(b, 1),
        in_specs=[pl.BlockSpec(memory_space=pltpu.SMEM),
                  pl.BlockSpec((1, nc, SWA_Q_W), lambda i, j: (i, 0, 0)), full, full],
        out_specs=pl.BlockSpec((1, nc, SWA_Q_W), lambda i, j: (i, 0, 0)),
        compiler_params=_cparams("parallel", "arbitrary"),
        name="ctx_attention",
    )(sink, qc, kc, vc)


def _ssm_matrices(a_re, a_im, log_step, b_re, b_im, c_re, c_im):
    L = SSM_CHUNK
    ar = a_re.astype(F32)
    ai = a_im.astype(F32)
    dt = jnp.exp(log_step.astype(F32))[..., None]
    m = jnp.arange(L + 1, dtype=F32)[:, None, None, None]
    mag = jnp.exp(m * (dt * ar)[None])
    pw_re = mag * jnp.cos(m * (dt * ai)[None])
    pw_im = mag * jnp.sin(m * (dt * ai)[None])
    den = ar * ar + ai * ai
    nr = pw_re[1] - 1.0
    coef_re = (nr * ar + pw_im[1] * ai) / den
    coef_im = (pw_im[1] * ar - nr * ai) / den
    br = b_re.astype(F32)
    bi = b_im.astype(F32)
    bb_re = coef_re[..., None] * br - coef_im[..., None] * bi
    bb_im = coef_re[..., None] * bi + coef_im[..., None] * br
    cr = c_re.astype(F32)[None]
    ci = c_im.astype(F32)[None]
    ca_re = cr * pw_re[:, :, :, None, :] - ci * pw_im[:, :, :, None, :]
    ca_im = cr * pw_im[:, :, :, None, :] + ci * pw_re[:, :, :, None, :]
    kern = jnp.sum(ca_re[:L, ..., None] * bb_re[None, :, :, None] - ca_im[:L, ..., None] * bb_im[None, :, :, None],
                   axis=4)
    G, C = SSM_GROUPS, SSM_GROUP
    k_f = jnp.transpose(kern[:, 0], (1, 3, 0, 2)).reshape(G, C, SSM_CW)
    k_r = jnp.transpose(kern[::-1, 1], (1, 3, 0, 2)).reshape(G, C, SSM_CW)
    span = (2 * L - 1) * C
    lagged = (jnp.pad(k_f, ((0, 0), (0, 0), ((L - 1) * C, 0)))
              + jnp.pad(k_r, ((0, 0), (0, 0), (0, (L - 1) * C))))
    wide = jnp.pad(lagged, ((0, 0), (0, 0), (0, C))).astype(BF16)
    flat = jnp.broadcast_to(wide[:, :, None, :], (G, C, L, span + C)).reshape(G, C, L * (span + C))
    skew = flat[:, :, (L - 1) * C:(L - 1) * C + L * span].reshape(G, C, L, span)[..., :SSM_CW]
    m_intra = jnp.transpose(skew, (0, 2, 1, 3)).reshape(G, SSM_CW, SSM_CW)
    both = lambda t_f, t_r: jnp.concatenate([t_f, t_r], -1)
    pin_re = jnp.transpose(both(pw_re[:L, 0][::-1], pw_re[:L, 1]), (1, 0, 2))[:, :, None, :]
    pin_im = jnp.transpose(both(pw_im[:L, 0][::-1], pw_im[:L, 1]), (1, 0, 2))[:, :, None, :]
    bt_re = both(jnp.transpose(bb_re[0], (0, 2, 1)), jnp.transpose(bb_re[1], (0, 2, 1)))[:, None]
    bt_im = both(jnp.transpose(bb_im[0], (0, 2, 1)), jnp.transpose(bb_im[1], (0, 2, 1)))[:, None]
    m_in = jnp.concatenate([pin_re * bt_re - pin_im * bt_im, pin_re * bt_im + pin_im * bt_re],
                           -1).reshape(G, SSM_CW, SSM_CW)
    pst_re = jnp.transpose(both(pw_re[1:, 0], pw_re[1:, 1][::-1]), (1, 0, 2))[:, :, None, :]
    pst_im = jnp.transpose(both(pw_im[1:, 0], pw_im[1:, 1][::-1]), (1, 0, 2))[:, :, None, :]
    ct_re = both(c_re[0].astype(F32), c_re[1].astype(F32))[:, None]
    ct_im = both(c_im[0].astype(F32), c_im[1].astype(F32))[:, None]
    m_state_t = jnp.concatenate([pst_re * ct_re - pst_im * ct_im, -(pst_re * ct_im + pst_im * ct_re)],
                                -1).reshape(G, SSM_CW, SSM_CW)
    a_l = jnp.stack([both(pw_re[L, 0], pw_re[L, 1]), both(pw_im[L, 0], pw_im[L, 1])], 1)
    return m_in.astype(BF16), m_intra.astype(BF16), m_state_t.astype(BF16), a_l


def _ssm_pack_kernel(uc_ref, u_ref, x_ref, *, nc_ctx, nc_lat):
    for src, row0, nch in ((uc_ref, 0, nc_ctx), (u_ref, nc_ctx, nc_lat)):
        steps = [src[0, pl.ds(j, nch, stride=SSM_CHUNK), :] for j in range(SSM_CHUNK)]
        lane = lax.broadcasted_iota(jnp.int32, (nch, LANES), 1)
        piece = [(lane >= jj * SSM_GROUP) & (lane < (jj + 1) * SSM_GROUP) for jj in range(SSM_SLAB_GROUPS)]
        for g in range(SSM_SLAB_GROUPS):
            for h in range(SSM_CW // LANES):
                acc = jnp.zeros((nch, LANES), F32)
                for jj in range(SSM_SLAB_GROUPS):
                    z = steps[h * SSM_SLAB_GROUPS + jj]
                    shift = ((jj - g) * SSM_GROUP) % LANES
                    acc = jnp.where(piece[jj], z if shift == 0 else pltpu.roll(z, shift, 1), acc)
                x_ref[g, row0:row0 + nch, h * LANES:(h + 1) * LANES] = acc.astype(BF16)


def _ssm_unpack_kernel(y_ref, oc_ref, o_ref, *, nc_ctx, nc_lat):
    for dst, row0, nch in ((oc_ref, 0, nc_ctx), (o_ref, nc_ctx, nc_lat)):
        lane = lax.broadcasted_iota(jnp.int32, (nch, LANES), 1)
        piece = [(lane >= g * SSM_GROUP) & (lane < (g + 1) * SSM_GROUP) for g in range(SSM_SLAB_GROUPS)]
        for i in range(SSM_CHUNK):
            h, ii = divmod(i, SSM_SLAB_GROUPS)
            acc = jnp.zeros((nch, LANES), F32)
            for g in range(SSM_SLAB_GROUPS):
                z = y_ref[g, row0:row0 + nch, h * LANES:(h + 1) * LANES]
                shift = ((g - ii) * SSM_GROUP) % LANES
                acc = jnp.where(piece[g], z if shift == 0 else pltpu.roll(z, shift, 1), acc)
            dst[0, pl.ds(i, nch, stride=SSM_CHUNK), :] = acc


def _ssm_kernel(x_ref, min_ref, mintra_ref, mstate_ref, al_ref, y_ref, v_scr, s_scr, *, nb, nc_ctx, n_chunks):
    n_grp = x_ref.shape[0]
    for g in range(n_grp):
        v = jnp.dot(x_ref[g], min_ref[g], preferred_element_type=F32)
        v_scr[g, 0] = v[:, 0:LANES]
        v_scr[g, 1] = v[:, LANES:]
    ar = [al_ref[g, 0:1, :] for g in range(n_grp)]
    ai = [al_ref[g, 1:2, :] for g in range(n_grp)]
    lo = lax.broadcasted_iota(jnp.int32, (nb, LANES), 1) < HALF

    def body(k, carry):
        kr = jnp.where(k < nc_ctx, nc_ctx - 1 - k, n_chunks - 1 + nc_ctx - k)
        rf = pl.ds(k, nb, stride=n_chunks)
        rr = pl.ds(kr, nb, stride=n_chunks)
        out = []
        for g in range(n_grp):
            sre, sim = carry[2 * g], carry[2 * g + 1]
            s_scr[g, 0, rf, :] = sre
            s_scr[g, 1, rr, :] = sre
            s_scr[g, 2, rf, :] = sim
            s_scr[g, 3, rr, :] = sim
            vre = jnp.where(lo, v_scr[g, 0, rf, :], v_scr[g, 0, rr, :])
            vim = jnp.where(lo, v_scr[g, 1, rf, :], v_scr[g, 1, rr, :])
            out += [ar[g] * sre - ai[g] * sim + vre, ar[g] * sim + ai[g] * sre + vim]
        return tuple(out)

    zero = jnp.zeros((nb, LANES), F32)
    lax.fori_loop(0, n_chunks, body, (zero,) * (2 * n_grp))
    lo_all = lax.broadcasted_iota(jnp.int32, (s_scr.shape[2], LANES), 1) < HALF
    for g in range(n_grp):
        s_in = jnp.concatenate([jnp.where(lo_all, s_scr[g, 0], s_scr[g, 1]),
                                jnp.where(lo_all, s_scr[g, 2], s_scr[g, 3])], axis=1).astype(BF16)
        y_ref[g] = jnp.dot(x_ref[g], mintra_ref[g], preferred_element_type=F32) + _nt_dot(s_in, mstate_ref[g])


def _ssm_scan(u, uc, mats):
    m_in, m_intra, m_state, a_l = mats
    b, n, _ = u.shape
    nc = uc.shape[1]
    nc_ctx, nc_lat = nc // SSM_CHUNK, n // SSM_CHUNK
    n_chunks = nc_ctx + nc_lat
    r = n_chunks * b
    nat = lambda m: pl.BlockSpec((1, m, LANES), lambda i, s: (i, 0, s))
    grp = pl.BlockSpec((SSM_SLAB_GROUPS, n_chunks, SSM_CW), lambda i, s: (s, i, 0))
    xg = pl.pallas_call(
        functools.partial(_ssm_pack_kernel, nc_ctx=nc_ctx, nc_lat=nc_lat),
        out_shape=_sds((SSM_GROUPS, r, SSM_CW), BF16),
        grid=(b, SSM_GROUPS // SSM_SLAB_GROUPS),
        in_specs=[nat(nc), nat(n)],
        out_specs=grp,
        compiler_params=_cparams("parallel", "parallel"),
        name="ssm_pack",
    )(uc, u)
    gs = SSM_GROUPS_PER_STEP
    mat = pl.BlockSpec((gs, SSM_CW, SSM_CW), lambda g: (g, 0, 0))
    yg = pl.pallas_call(
        functools.partial(_ssm_kernel, nb=b, nc_ctx=nc_ctx, n_chunks=n_chunks),
        out_shape=_sds((SSM_GROUPS, r, SSM_CW), F32),
        grid=(SSM_GROUPS // gs,),
        in_specs=[pl.BlockSpec((gs, r, SSM_CW), lambda g: (g, 0, 0)), mat, mat, mat,
                  pl.BlockSpec((gs, 2, LANES), lambda g: (g, 0, 0))],
        out_specs=pl.BlockSpec((gs, r, SSM_CW), lambda g: (g, 0, 0)),
        scratch_shapes=[pltpu.VMEM((gs, 2, r, LANES), F32), pltpu.VMEM((gs, 4, r, LANES), F32)],
        compiler_params=_cparams("parallel"),
        name="ssm_scan",
    )(xg, m_in, m_intra, m_state, a_l)
    ysc, ys = pl.pallas_call(
        functools.partial(_ssm_unpack_kernel, nc_ctx=nc_ctx, nc_lat=nc_lat),
        out_shape=(_sds((b, nc, SSM_WIDTH), F32), _sds((b, n, SSM_WIDTH), F32)),
        grid=(b, SSM_GROUPS // SSM_SLAB_GROUPS),
        in_specs=[grp],
        out_specs=(nat(nc), nat(n)),
        compiler_params=_cparams("parallel", "parallel"),
        name="ssm_unpack",
    )(yg)
    return ys, ysc


def _split_rows(ref, val):
    m = val.shape[0]
    for s in range(TOKEN_TILE_ROWS):
        ref[pl.ds(s, m, stride=TOKEN_TILE_ROWS), :] = val[:, s * LANES:(s + 1) * LANES]


def _merge_rows(ref):
    m = ref.shape[0] // TOKEN_TILE_ROWS
    return jnp.concatenate([ref[pl.ds(s, m, stride=TOKEN_TILE_ROWS), :] for s in range(TOKEN_TILE_ROWS)], axis=1)


def _token_tile(ref, t):
    return ref.at[pl.ds(pl.multiple_of(t * TOKEN_TILE_ROWS, TOKEN_TILE_ROWS), TOKEN_TILE_ROWS)]


def _router_logits(w, bias, h):
    w_hi, h_hi = w.astype(BF16), h.astype(BF16)
    w_lo, h_lo = (w - w_hi.astype(F32)).astype(BF16), (h - h_hi.astype(F32)).astype(BF16)
    return _nt_dot(w_hi, h_hi) + (_nt_dot(w_hi, h_lo) + _nt_dot(w_lo, h_hi)) + bias


def _post_kernel(*refs, alpha, with_ssm):
    wr_ref, br_ref = refs[-5:-3]
    refs = refs[:-5] + refs[-3:]
    if with_ssm:
        (att_ref, ys_ref, u_ref, dsk_ref, gw_ref, gb_ref, wo_ref, x_ref, g1_ref, lg_ref, lb_ref, sc2_ref, sh2_ref,
         x1_ref, h2_ref, lgt_ref) = refs
        y = ys_ref[0] + u_ref[0] * dsk_ref[...]
        gl = jax.nn.gelu(y)
        gate = jax.nn.sigmoid(jnp.dot(gl.astype(BF16), gw_ref[...], preferred_element_type=F32) + gb_ref[...])
        ssm = (gl * gate).astype(BF16)
        mix = (jnp.dot(att_ref[0], wo_ref[0:SWA_Q_W, :], preferred_element_type=F32)
               + jnp.dot(ssm, wo_ref[SWA_Q_W:, :], preferred_element_type=F32))
    else:
        att_ref, wo_ref, x_ref, g1_ref, lg_ref, lb_ref, sc2_ref, sh2_ref, x1_ref, h2_ref, lgt_ref = refs
        mix = jnp.dot(att_ref[0], wo_ref[...], preferred_element_type=F32)
    x1 = _layer_norm(alpha * x_ref[0] + g1_ref[0] * mix, lg_ref[...], lb_ref[...])
    x1_ref[0] = x1
    h2 = x1 * (1.0 + sc2_ref[0]) + sh2_ref[0]
    _split_rows(h2_ref, h2)
    lgt_ref[...] = _router_logits(wr_ref[...], br_ref[...], h2)


def _post(att, ssm_args, w_out_bf16, x, g1, ln_g, ln_b, sc2, sh2, router_wb, alpha, tm):
    b, n, d = x.shape
    tok = lambda w: pl.BlockSpec((1, tm, w), lambda i, j: (i, j, 0))
    const = lambda a: pl.BlockSpec(a.shape, lambda i, j: (0,) * a.ndim)
    in_specs = [tok(att.shape[-1])]
    args = [att]
    if ssm_args is not None:
        ys, u, dsk, gw, gb = ssm_args
        in_specs += [tok(SSM_WIDTH), tok(SSM_WIDTH), const(dsk), const(gw), const(gb)]
        args += [ys, u, dsk, gw, gb]
    in_specs += [const(w_out_bf16), tok(d), g1.spec(), const(ln_g), const(ln_b), sc2.spec(), sh2.spec()]
    args += [w_out_bf16, x, g1.table, ln_g, ln_b, sc2.table, sh2.table]
    in_specs += [const(a) for a in router_wb]
    args += list(router_wb)
    per_b = n // tm
    n_logit = router_wb[0].shape[0]
    return pl.pallas_call(
        functools.partial(_post_kernel, alpha=alpha, with_ssm=ssm_args is not None),
        out_shape=(_sds((b, n, d), F32), _sds((b * n * TOKEN_TILE_ROWS, LANES), F32), _sds((n_logit, b * n), F32)),
        grid=(b, n // tm),
        in_specs=in_specs,
        out_specs=(tok(d), pl.BlockSpec((tm * TOKEN_TILE_ROWS, LANES), lambda i, j: (i * per_b + j, 0)),
                   pl.BlockSpec((n_logit, tm), lambda i, j: (0, i * per_b + j))),
        compiler_params=_cparams("parallel", "parallel"),
        name="post_mixer_ssm" if ssm_args is not None else "post_mixer",
    )(*args)


def _first_max(v, sub):
    m = jnp.max(v, 0, keepdims=True)
    idx = jnp.min(jnp.where(v == m, sub, float(SUBLANES)), 0, keepdims=True)
    return m, idx


def _stream_blocks(toks, tb):
    starts = [0]
    for t in toks:
        starts.append(starts[-1] + t.shape[0] // (tb * TOKEN_TILE_ROWS))
    return starts


def _stream_spec(tok, tb, start):
    last = tok.shape[0] // (tb * TOKEN_TILE_ROWS) - 1
    return pl.BlockSpec((tb * TOKEN_TILE_ROWS, LANES), lambda i, *_: (jnp.clip(i - start, 0, last), 0))


def _router_kernel(*refs, starts):
    n_streams = len(starts) - 1
    lgt_refs = refs[:n_streams]
    ids_ref, wts_ref, rank_ref, cnt_ref, carry_scr = refs[n_streams:]
    step = pl.program_id(0)

    @pl.when(step == 0)
    def _():
        carry_scr[...] = jnp.zeros_like(carry_scr)

    logits = lgt_refs[0][...]
    for ref, start in zip(lgt_refs[1:], starts[1:]):
        logits = jnp.where(step >= start, ref[...], logits)
    tm = logits.shape[1]
    sub = lax.broadcasted_iota(jnp.int32, (SUBLANES, tm), 0).astype(F32)
    gl = logits[0:SUBLANES]
    gmax, gi = _first_max(gl, sub)
    gp = 1.0 / jnp.sum(jnp.exp(gl - gmax), 0, keepdims=True)
    le = logits[ROUTER_EXPERT_ROW0:ROUTER_EXPERT_ROW0 + MOE_EPG]
    for g in range(1, MOE_GROUPS):
        le = jnp.where(gi == float(g), logits[ROUTER_EXPERT_ROW0 + g * MOE_EPG:ROUTER_EXPERT_ROW0 + (g + 1) * MOE_EPG], le)
    m1, i1 = _first_max(le, sub)
    m2, i2 = _first_max(jnp.where(sub == i1, NEG_INF, le), sub)
    t = jnp.exp(m2 - m1)
    e1 = gi * float(MOE_EPG) + i1
    e2 = gi * float(MOE_EPG) + i2
    ids_ref[0:1, :] = e1.astype(jnp.int32)
    ids_ref[1:2, :] = e2.astype(jnp.int32)
    wts_ref[0:1, :] = gp / (1.0 + t)
    wts_ref[1:2, :] = gp * t / (1.0 + t)
    esub = lax.broadcasted_iota(jnp.int32, (MOE_EXPERTS, tm), 0).astype(F32)
    oh1 = (esub == e1).astype(F32)
    oh2 = (esub == e2).astype(F32)
    both = oh1 + oh2
    earlier = (lax.broadcasted_iota(jnp.int32, (tm, tm), 0) < lax.broadcasted_iota(jnp.int32, (tm, tm), 1))
    prefix = jnp.dot(both.astype(BF16), earlier.astype(BF16), preferred_element_type=F32) + carry_scr[...]
    rank_ref[0:1, :] = jnp.sum(oh1 * prefix, 0, keepdims=True).astype(jnp.int32)
    rank_ref[1:2, :] = jnp.sum(oh2 * prefix, 0, keepdims=True).astype(jnp.int32)
    carry_scr[...] += jnp.sum(both, 1, keepdims=True)
    cnt_ref[...] = jnp.broadcast_to(carry_scr[...], cnt_ref.shape)


def _router_weights(wg, bg, we, be):
    d = wg.shape[0]
    rows = ROUTER_EXPERT_ROW0 + MOE_EXPERTS
    w = jnp.zeros((rows, d), F32)
    w = w.at[:MOE_GROUPS].set(wg.T)
    w = w.at[ROUTER_EXPERT_ROW0:].set(jnp.transpose(we, (0, 2, 1)).reshape(MOE_EXPERTS, d))
    bias = jnp.full((rows, 1), NEG_INF, F32)
    bias = bias.at[:MOE_GROUPS, 0].set(bg)
    bias = bias.at[ROUTER_EXPERT_ROW0:, 0].set(be.reshape(-1))
    return w, bias


def _router(logit_streams, tm):
    starts = [0]
    for lg in logit_streams:
        starts.append(starts[-1] + lg.shape[1] // tm)
    t = starts[-1] * tm
    rows = logit_streams[0].shape[0]
    stream_spec = lambda lg, s0: pl.BlockSpec(
        (rows, tm), lambda i: (0, jnp.clip(i - s0, 0, lg.shape[1] // tm - 1)))
    pair = pl.BlockSpec((MOE_TOPK, tm), lambda i: (0, i))
    ids, wts, rank, cnt = pl.pallas_call(
        functools.partial(_router_kernel, starts=starts),
        out_shape=(_sds((MOE_TOPK, t), jnp.int32), _sds((MOE_TOPK, t), F32), _sds((MOE_TOPK, t), jnp.int32),
                   _sds((MOE_EXPERTS, LANES), F32)),
        grid=(t // tm,),
        in_specs=[stream_spec(lg, s0) for lg, s0 in zip(logit_streams, starts)],
        out_specs=(pair, pair, pair, pl.BlockSpec((MOE_EXPERTS, LANES), lambda i: (0, 0))),
        scratch_shapes=[pltpu.VMEM((MOE_EXPERTS, 1), F32)],
        compiler_params=_cparams("arbitrary"),
        name="moe_router",
    )(*logit_streams)
    return ids, wts, rank, cnt[:, 0].astype(jnp.int32)


def _moe_plan(ids, rank, counts, n_tok):
    tm = MOE_ROW_TILE
    padded = ((counts + tm - 1) // tm) * tm
    ends = jnp.cumsum(padded)
    offs = ends - padded
    experts = jnp.arange(MOE_EXPERTS, dtype=jnp.int32)
    dest = (jnp.sum(jnp.where(ids[..., None] == experts, offs, 0), -1) + rank).astype(jnp.int32)
    n_tiles = (MOE_TOPK * n_tok + MOE_EXPERTS * (tm - 1)) // tm
    starts = jnp.arange(n_tiles, dtype=jnp.int32) * tm
    tile_expert = jnp.minimum(jnp.sum((ends[None, :] <= starts[:, None]).astype(jnp.int32), -1), MOE_EXPERTS - 1)
    n_valid = (ends[-1] // tm).astype(jnp.int32).reshape(1)
    pad = jnp.stack([offs + counts, padded - counts, jnp.broadcast_to(n_valid, counts.shape)]).astype(jnp.int32)
    return dest, tile_expert, n_valid, pad, n_tiles * tm


def _pad_fill(pad_ref, zero_scr, xs_ref, sem, wait):
    def per_expert(e, carry):
        first, count = pad_ref[0, e], pad_ref[1, e]
        piece = MOE_ROW_TILE // 2
        while piece >= 1:
            row = first + (count & ~(2 * piece - 1))
            n = piece * TOKEN_TILE_ROWS

            @pl.when((count & piece) != 0)
            def _(row=row, n=n):
                copy = pltpu.make_async_copy(
                    zero_scr.at[pl.ds(0, n)],
                    xs_ref.at[pl.ds(pl.multiple_of(row * TOKEN_TILE_ROWS, TOKEN_TILE_ROWS), n)], sem)
                copy.wait() if wait else copy.start()

            piece //= 2
        return carry

    lax.fori_loop(0, MOE_EXPERTS, per_expert, 0)
    half = MOE_ROW_TILE // 2 * TOKEN_TILE_ROWS
    n_tiles = xs_ref.shape[0] // (2 * half)

    def per_tile(t, carry):
        for h in range(2):
            copy = pltpu.make_async_copy(zero_scr, xs_ref.at[pl.ds(pl.multiple_of((2 * t + h) * half, half), half)],
                                         sem)
            copy.wait() if wait else copy.start()
        return carry

    lax.fori_loop(pad_ref[2, 0], n_tiles, per_tile, 0)


def _dispatch_kernel(*refs, tb, starts):
    n_streams = len(starts) - 1
    dest_ref, pad_ref = refs[:2]
    tok_refs = refs[2:2 + n_streams]
    xs_ref, zero_scr, sem, pad_sem = refs[2 + n_streams:]
    step = pl.program_id(0)

    @pl.when(step == 0)
    def _():
        zero_scr[...] = jnp.zeros_like(zero_scr)
        _pad_fill(pad_ref, zero_scr, xs_ref, pad_sem, wait=False)

    for s, tok_ref in enumerate(tok_refs):
        @pl.when((step >= starts[s]) & (step < starts[s + 1]))
        def _(tok_ref=tok_ref):
            def body(r, carry):
                for k in range(MOE_TOPK):
                    pltpu.make_async_copy(_token_tile(tok_ref, r), _token_tile(xs_ref, dest_ref[0, k, r]), sem).start()
                return carry

            lax.fori_loop(0, tb, body, 0, unroll=8)

    for k in range(MOE_TOPK):
        pltpu.make_async_copy(tok_refs[0], xs_ref.at[pl.ds(0, tb * TOKEN_TILE_ROWS)], sem).wait()

    @pl.when(step == 0)
    def _():
        _pad_fill(pad_ref, zero_scr, xs_ref, pad_sem, wait=True)


def _dispatch(toks, dest3, pad, n_rows):
    nblk, _, tb = dest3.shape
    starts = _stream_blocks(toks, tb)
    return pl.pallas_call(
        functools.partial(_dispatch_kernel, tb=tb, starts=starts),
        out_shape=_sds((n_rows * TOKEN_TILE_ROWS, LANES), F32),
        grid=(nblk,),
        in_specs=[pl.BlockSpec((1, MOE_TOPK, tb), lambda i: (i, 0, 0), memory_space=pltpu.SMEM),
                  pl.BlockSpec(memory_space=pltpu.SMEM)]
        + [_stream_spec(tok, tb, s0) for tok, s0 in zip(toks, starts)],
        out_specs=pl.BlockSpec(memory_space=pl.ANY),
        scratch_shapes=[pltpu.VMEM((MOE_ROW_TILE // 2 * TOKEN_TILE_ROWS, LANES), F32),
                        pltpu.SemaphoreType.DMA(()), pltpu.SemaphoreType.DMA(())],
        compiler_params=_cparams("arbitrary"),
        name="moe_dispatch",
    )(dest3, pad, *toks)


def _ffn_kernel(te_ref, nv_ref, x_ref, w1_ref, w3_ref, w2_ref, y_ref, w13_scr, w2_scr):
    i = pl.program_id(0)
    f = w1_ref.shape[2]

    @pl.when((i == 0) | (te_ref[i] != te_ref[jnp.maximum(i - 1, 0)]))
    def _():
        w13_scr[:, 0:f] = w1_ref[0].astype(BF16)
        w13_scr[:, f:2 * f] = w3_ref[0].astype(BF16)
        w2_scr[...] = w2_ref[0].astype(BF16)

    @pl.when(i < nv_ref[0])
    def _():
        h13 = jnp.dot(_merge_rows(x_ref).astype(BF16), w13_scr[...], preferred_element_type=F32)
        h1 = h13[:, 0:f]
        hh = (h1 * jax.nn.sigmoid(h1) * h13[:, f:2 * f]).astype(BF16)
        _split_rows(y_ref, jnp.dot(hh, w2_scr[...], preferred_element_type=F32))

    @pl.when(i >= nv_ref[0])
    def _():
        y_ref[...] = jnp.zeros_like(y_ref)


def _expert_ffn(tile_expert, n_valid, xs, w1, w3, w2):
    p = xs.shape[0] // TOKEN_TILE_ROWS
    _, d, f = w1.shape
    tm = MOE_ROW_TILE
    rows = pl.BlockSpec((tm * TOKEN_TILE_ROWS, LANES), lambda i, te, nv: (i, 0))
    rows_in = pl.BlockSpec((tm * TOKEN_TILE_ROWS, LANES), lambda i, te, nv: (jnp.minimum(i, nv[0] - 1), 0))
    return pl.pallas_call(
        _ffn_kernel,
        out_shape=_sds(xs.shape, F32),
        grid_spec=pltpu.PrefetchScalarGridSpec(
            num_scalar_prefetch=2,
            grid=(p // tm,),
            in_specs=[rows_in,
                      pl.BlockSpec((1, d, f), lambda i, te, nv: (te[i], 0, 0)),
                      pl.BlockSpec((1, d, f), lambda i, te, nv: (te[i], 0, 0)),
                      pl.BlockSpec((1, f, d), lambda i, te, nv: (te[i], 0, 0))],
            out_specs=rows,
            scratch_shapes=[pltpu.VMEM((d, 2 * f), BF16), pltpu.VMEM((f, d), BF16)]),
        compiler_params=_cparams("arbitrary"),
        name="moe_expert_ffn",
    )(tile_expert, n_valid, xs, w1, w3, w2)


def _combine_ln2_kernel(dest_ref, next_ref, x1_ref, ys_ref, wt_ref, g2_ref, lg_ref, lb_ref, o_ref, buf, sem, *,
                        alpha, tb, nblk):
    step = pl.program_id(0)

    def gather(d_ref, slot):
        def body(r, carry):
            for k in range(MOE_TOPK):
                pltpu.make_async_copy(_token_tile(ys_ref, d_ref[0, k, r]), _token_tile(buf.at[slot, k], r),
                                      sem.at[slot, k]).start()
            return carry

        lax.fori_loop(0, tb, body, 0, unroll=8)

    @pl.when(step == 0)
    def _():
        gather(dest_ref, 0)

    @pl.when(step + 1 < nblk)
    def _():
        gather(next_ref, (step + 1) % 2)

    slot = step % 2
    for k in range(MOE_TOPK):
        pltpu.make_async_copy(ys_ref.at[pl.ds(0, tb * TOKEN_TILE_ROWS)], buf.at[slot, k], sem.at[slot, k]).wait()
    f = wt_ref[:, 0:1] * _merge_rows(buf.at[slot, 0]) + wt_ref[:, 1:2] * _merge_rows(buf.at[slot, 1])
    o_ref[...] = _layer_norm(alpha * x1_ref[...] + g2_ref[0] * f, lg_ref[...], lb_ref[...])


def _combine_ln2(dest3, x1, ys, wt, g2, ln_g, ln_b, alpha, n_per_sample):
    t, d = x1.shape
    nblk, _, tb = dest3.shape
    const = pl.BlockSpec((1, d), lambda i: (0, 0))
    assert g2.stride == 0 or n_per_sample % tb == 0, (n_per_sample, tb)
    return pl.pallas_call(
        functools.partial(_combine_ln2_kernel, alpha=alpha, tb=tb, nblk=nblk),
        out_shape=_sds((t, d), F32),
        grid=(nblk,),
        in_specs=[pl.BlockSpec((1, MOE_TOPK, tb), lambda i: (i, 0, 0), memory_space=pltpu.SMEM),
                  pl.BlockSpec((1, MOE_TOPK, tb), lambda i: (jnp.minimum(i + 1, nblk - 1), 0, 0),
                               memory_space=pltpu.SMEM),
                  pl.BlockSpec((tb, d), lambda i: (i, 0)),
                  pl.BlockSpec(memory_space=pl.ANY),
                  pl.BlockSpec((tb, MOE_TOPK), lambda i: (i, 0)),
                  g2.spec(lambda i: (i * tb) // n_per_sample),
                  const, const],
        out_specs=pl.BlockSpec((tb, d), lambda i: (i, 0)),
        scratch_shapes=[pltpu.VMEM((2, MOE_TOPK, tb * TOKEN_TILE_ROWS, LANES), F32),
                        pltpu.SemaphoreType.DMA((2, MOE_TOPK))],
        compiler_params=_cparams("arbitrary"),
        name="moe_combine_ln2",
    )(dest3, dest3, x1, ys, wt, g2.table, ln_g, ln_b)


def _proj1_kernel(*refs, rope, with_q):
    x_ref, sc_ref, sh_ref, w_ref = refs[:4]
    refs = refs[4:]
    tabs = None
    if rope:
        tabs = (refs[0][...], refs[1][...], refs[2][...])
        refs = refs[3:]
    h = (x_ref[0] * (1.0 + sc_ref[0]) + sh_ref[0]).astype(BF16)
    r = jnp.dot(h, w_ref[...], preferred_element_type=F32)
    off = 0
    if with_q:
        q_ref, k_ref, v_ref = refs
        for hd in range(DIF_HEADS):
            q_ref[0, hd] = (_rot(r[:, hd * LANES:(hd + 1) * LANES], tabs) * (LOG2_E * HEAD_DIM ** -0.5)).astype(BF16)
        off = DIF_QK_W
    else:
        k_ref, v_ref = refs
    for hd in range(DIF_HEADS):
        k_ref[0, hd] = _rot(r[:, off + hd * LANES:off + (hd + 1) * LANES], tabs).astype(BF16)
        v_ref[0, hd] = r[:, off + DIF_QK_W + hd * LANES:off + DIF_QK_W + (hd + 1) * LANES].astype(BF16)


def _proj1(x, sc, sh, w_bf16, tabs, with_q, tm):
    b, n, d = x.shape
    rope = tabs is not None
    in_specs = [pl.BlockSpec((1, tm, d), lambda i, j: (i, j, 0)), sc.spec(), sh.spec(),
                pl.BlockSpec(w_bf16.shape, lambda i, j: (0, 0))]
    args = [x, sc.table, sh.table, w_bf16]
    if rope:
        in_specs += [pl.BlockSpec((tm, LANES), lambda i, j: (j, 0))] * 3
        args += list(tabs)
    hm = pl.BlockSpec((1, DIF_HEADS, tm, LANES), lambda i, j: (i, 0, j, 0))
    n_out = 3 if with_q else 2
    return pl.pallas_call(
        functools.partial(_proj1_kernel, rope=rope, with_q=with_q),
        out_shape=(_sds((b, DIF_HEADS, n, LANES), BF16),) * n_out,
        grid=(b, n // tm),
        in_specs=in_specs,
        out_specs=(hm,) * n_out,
        compiler_params=_cparams("parallel", "parallel"),
        name="proj1_qkv" if with_q else "proj1_kv_ctx",
    )(*args)


def _diff_kernel(lam_ref, q_ref, kl_ref, kc_ref, vl_ref, vc_ref, g_ref, o_ref, k_scr, v_scr, *, tq, n_lat, out_scale):
    @pl.when(pl.program_id(2) == 0)
    def _():
        k_scr[0:n_lat] = kl_ref[0, 0]
        k_scr[n_lat:] = kc_ref[0, 0]
        v_scr[0:n_lat, 0:LANES] = vl_ref[0, 0]
        v_scr[n_lat:, 0:LANES] = vc_ref[0, 0]
        v_scr[:, LANES:] = jnp.ones((v_scr.shape[0], LANES), BF16)

    lo = lax.broadcasted_iota(jnp.int32, (DIF_CHAIN_ROWS, LANES), 1) < HALF
    for r0 in range(0, tq, DIF_CHAIN_ROWS):
        q = q_ref[0, 0, r0:r0 + DIF_CHAIN_ROWS, :].astype(F32)
        maps = []
        for qm in (jnp.where(lo, q, 0.0), jnp.where(lo, 0.0, q)):
            s = _nt_dot(qm.astype(BF16), k_scr[...])
            p = jnp.exp2(s - jnp.max(s, -1, keepdims=True)).astype(BF16)
            oe = jnp.dot(p, v_scr[...], preferred_element_type=F32)
            maps.append(oe[:, 0:LANES] * (1.0 / oe[:, LANES:]))
        o = maps[0] - lam_ref[0] * maps[1]
        o = o * lax.rsqrt(jnp.mean(o * o, -1, keepdims=True) + RMS_EPS) * g_ref[...]
        o_ref[0, r0:r0 + DIF_CHAIN_ROWS, :] = (o * out_scale).astype(BF16)


def _diff_attention(lam, q, kl, kc, vl, vc, subln_g, lam_init, tq):
    b, nh, n, _ = q.shape
    nc = kc.shape[2]
    kv = lambda m: pl.BlockSpec((1, 1, m, LANES), lambda i, h, j: (i, h, 0, 0))
    return pl.pallas_call(
        functools.partial(_diff_kernel, tq=tq, n_lat=n, out_scale=1.0 - lam_init),
        out_shape=_sds((b, n, nh * LANES), BF16),
        grid=(b, nh, n // tq),
        in_specs=[pl.BlockSpec(memory_space=pltpu.SMEM),
                  pl.BlockSpec((1, 1, tq, LANES), lambda i, h, j: (i, h, j, 0)),
                  kv(n), kv(nc), kv(n), kv(nc),
                  pl.BlockSpec((1, LANES), lambda i, h, j: (0, 0))],
        out_specs=pl.BlockSpec((1, tq, LANES), lambda i, h, j: (i, j, h)),
        scratch_shapes=[pltpu.VMEM((n + nc, LANES), BF16), pltpu.VMEM((n + nc, 2 * LANES), BF16)],
        compiler_params=_cparams("parallel", "parallel", "arbitrary"),
        name="diff_attention",
    )(lam, q, kl, kc, vl, vc, subln_g)


def _moe_block(toks, logit_streams, layer, moe_w1, moe_w3, moe_w2):
    t = sum(tok.shape[0] for tok in toks) // TOKEN_TILE_ROWS
    tb = MOE_DMA_TILE
    ids, wts, rank, counts = _router(logit_streams, tm=512)
    dest, tile_expert, n_valid, pad, n_rows = _moe_plan(ids, rank, counts, t)
    dest3 = dest.reshape(MOE_TOPK, t // tb, tb).transpose(1, 0, 2)
    xs = _dispatch(toks, dest3, pad, n_rows)
    flat = lambda w: w.reshape((-1,) + w.shape[2:])
    ys = _expert_ffn(tile_expert + layer * MOE_EXPERTS, n_valid, xs, flat(moe_w1), flat(moe_w3), flat(moe_w2))
    return ys, dest3, wts.T


def kernel(x, c, ctx, c_ctx, mod_w, mod_b, ln1_g, ln1_b, ln2_g, ln2_b, swa_ssm_w_in, swa_ssm_w_out, swa_sink, ssm_a_re, ssm_a_im, ssm_log_step, ssm_b_re, ssm_b_im, ssm_c_re, ssm_c_im, ssm_d, ssm_glu_w, ssm_glu_b, dif_w_in, dif_w_out, dif_lam_q1, dif_lam_k1, dif_lam_q2, dif_lam_k2, dif_subln_g, moe_wg, moe_bg, moe_we, moe_be, moe_w1, moe_w3, moe_w2):
    bsz, n, d = x.shape
    ctx_len = ctx.shape[1]
    depth = mod_w.shape[0]
    alpha = (2 * depth) ** 0.25
    tabs = _rope_tables(n)

    n_vec = 16
    cvec = jnp.zeros((n_vec, d), F32).at[:bsz].set(c).at[bsz].set(c_ctx)
    mods = _modulation(cvec, mod_w, mod_b)
    mod_table = mods.reshape(depth * n_vec * 6, 1, d)

    xl, xc = x, ctx
    for layer in range(depth):
        need_ctx = layer < depth - 1
        i = layer // 2
        lat = [ModVec(mod_table, (layer * n_vec) * 6 + k, 6) for k in range(6)]
        cx = [ModVec(mod_table, (layer * n_vec + bsz) * 6 + k, 0) for k in range(6)]
        sh1, sc1, g1, sh2, sc2, g2 = lat
        csh1, csc1, cg1, csh2, csc2, cg2 = cx
        lg1, lb1 = ln1_g[layer].reshape(1, d), ln1_b[layer].reshape(1, d)
        lg2, lb2 = ln2_g[layer].reshape(1, d), ln2_b[layer].reshape(1, d)
        router_wb = _router_weights(moe_wg[layer], moe_bg[layer], moe_we[layer], moe_be[layer])
        if layer % 2 == 0:
            w_in = swa_ssm_w_in[i].astype(BF16)
            w_out = swa_ssm_w_out[i].astype(BF16)
            q, k, v, u = _proj0(xl, sc1, sh1, w_in, tabs, tm=512)
            qc, kc, vc, uc = _proj0(xc, csc1, csh1, w_in, None, tm=ctx_len)
            sink = swa_sink[i].astype(F32)
            att = _swa_attention(sink, q, k, v, kc, vc)
            mats = _ssm_matrices(ssm_a_re[i], ssm_a_im[i], ssm_log_step[i], ssm_b_re[i], ssm_b_im[i],
                                 ssm_c_re[i], ssm_c_im[i])
            ys, ysc = _ssm_scan(u, uc, mats)
            glu = (ssm_d[i].reshape(1, SSM_WIDTH).astype(F32), ssm_glu_w[i].astype(BF16),
                   ssm_glu_b[i].reshape(1, SSM_WIDTH).astype(F32))
            x1, tok, lgt = _post(att, (ys, u) + glu, w_out, xl, g1, lg1, lb1, sc2, sh2, router_wb, alpha, 512)
            toks, logit_streams = [tok], [lgt]
            if need_ctx:
                att_c = _ctx_attention(sink, qc, kc, vc)
                xc1, tok_c, lgt_c = _post(att_c, (ysc, uc) + glu, w_out, xc, cg1, lg1, lb1, csc2, csh2, router_wb,
                                          alpha, ctx_len)
                toks.append(tok_c)
                logit_streams.append(lgt_c)
        else:
            lam_init = 0.8 - 0.6 * math.exp(-0.3 * layer)
            w_in = dif_w_in[i].astype(BF16)
            w_out = dif_w_out[i].astype(BF16)
            q, k, v = _proj1(xl, sc1, sh1, w_in, tabs, True, tm=512)
            kc, vc = _proj1(xc, csc1, csh1, w_in[:, DIF_QK_W:], None, False, tm=ctx_len)
            lam = (jnp.exp(jnp.sum(dif_lam_q1[i].astype(F32) * dif_lam_k1[i].astype(F32)))
                   - jnp.exp(jnp.sum(dif_lam_q2[i].astype(F32) * dif_lam_k2[i].astype(F32))) + lam_init).reshape(1)
            att = _diff_attention(lam, q, k, kc, v, vc, dif_subln_g[i].reshape(1, DIF_V_HEAD).astype(F32),
                                  lam_init, tq=n)
            x1, tok, lgt = _post(att, None, w_out, xl, g1, lg1, lb1, sc2, sh2, router_wb, alpha, 512)
            toks, logit_streams = [tok], [lgt]
            if need_ctx:
                raise NotImplementedError("a differential-attention layer followed by another layer")
        ys_moe, dest3, wt = _moe_block(toks, logit_streams, layer, moe_w1, moe_w3, moe_w2)
        n_lat_blk = bsz * n // MOE_DMA_TILE
        xl = _combine_ln2(dest3[:n_lat_blk], x1.reshape(-1, d), ys_moe, wt[:bsz * n], g2, lg2, lb2, alpha,
                          n).reshape(bsz, n, d)
        if need_ctx:
            xc = _combine_ln2(dest3[n_lat_blk:], xc1.reshape(-1, d), ys_moe, wt[bsz * n:], cg2, lg2, lb2, alpha,
                              ctx_len).reshape(bsz, ctx_len, d)
    return xl
```

```python
import functools
import math
import typing

import jax
import jax.numpy as jnp
from jax import lax
from jax.experimental import pallas as pl
from jax.experimental.pallas import tpu as pltpu

F32 = jnp.float32
BF16 = jnp.bfloat16
HIGHEST = lax.Precision.HIGHEST

D_MODEL = 1024
GRID_W = 64
HEAD_DIM = 64
ROPE_BASE = 10000.0
ROPE_FREQS = HEAD_DIM // 4
LN_EPS = 1e-5
RMS_EPS = 1e-5
NEG_INF = -1e30
LOG2_E = math.log2(math.e)
LANES = 128
HALF = LANES // 2

SWA_HEADS = 8
SWA_KV_HEADS = 2
SWA_WINDOW = 128
SWA_BLOCK = 128
SWA_BLOCKS_PER_STEP = 4
SWA_Q_W = SWA_HEADS * HEAD_DIM
SWA_KV_W = SWA_KV_HEADS * HEAD_DIM

SSM_WIDTH = D_MODEL // 2
SSM_GROUP = 16
SSM_GROUPS = SSM_WIDTH // SSM_GROUP
SSM_STATE = 64
SSM_CHUNK = 16
SSM_CW = SSM_CHUNK * SSM_GROUP
SSM_SLAB_GROUPS = LANES // SSM_GROUP
SSM_GROUPS_PER_STEP = 4

AB_IN_W = SWA_Q_W + 2 * SWA_KV_W + SSM_WIDTH

DIF_HEADS = D_MODEL // (2 * HEAD_DIM)
DIF_QK_W = DIF_HEADS * 2 * HEAD_DIM
DIF_V_HEAD = 2 * HEAD_DIM
DIF_V_W = DIF_HEADS * DIF_V_HEAD
DIF_CHAIN_ROWS = 128

MOE_GROUPS = 4
MOE_EPG = 8
MOE_EXPERTS = MOE_GROUPS * MOE_EPG
MOE_HIDDEN = D_MODEL // 4
MOE_TOPK = 2
SUBLANES = 8
ROUTER_EXPERT_ROW0 = SUBLANES
MOE_ROW_TILE = 512
MOE_DMA_TILE = 512
TOKEN_TILE_ROWS = D_MODEL // LANES

VMEM_LIMIT = 56 * 1024 * 1024


def _cparams(*sem):
    return pltpu.CompilerParams(dimension_semantics=sem, vmem_limit_bytes=VMEM_LIMIT)


def _sds(shape, dtype):
    return jax.ShapeDtypeStruct(shape, dtype)


def _nt_dot(a, b):
    return lax.dot_general(a, b, (((1,), (1,)), ((), ())), preferred_element_type=F32)


def _layer_norm(r, g, b):
    mu = jnp.mean(r, -1, keepdims=True)
    rc = r - mu
    var = jnp.mean(rc * rc, -1, keepdims=True)
    return rc * lax.rsqrt(var + LN_EPS) * g + b


class ModVec(typing.NamedTuple):
    table: jax.Array
    row0: int
    stride: int

    def spec(self, sample_of_step=lambda i, *_: i):
        d = self.table.shape[-1]
        return pl.BlockSpec((1, 1, d), lambda *idx: (self.row0 + self.stride * sample_of_step(*idx), 0, 0))


def _mod_kernel(c_ref, w_ref, b_ref, o_ref):
    cv = c_ref[...]
    s = cv * jax.nn.sigmoid(cv)
    o_ref[0] = jnp.dot(s, w_ref[0], preferred_element_type=F32, precision=HIGHEST) + b_ref[0]


def _modulation(cvec, mod_w, mod_b):
    depth, d, w6 = mod_w.shape
    tn = 1536
    return pl.pallas_call(
        _mod_kernel,
        out_shape=_sds((depth, cvec.shape[0], w6), F32),
        grid=(depth, w6 // tn),
        in_specs=[pl.BlockSpec(cvec.shape, lambda l, j: (0, 0)),
                  pl.BlockSpec((1, d, tn), lambda l, j: (l, 0, j)),
                  pl.BlockSpec((1, 1, tn), lambda l, j: (l, 0, j))],
        out_specs=pl.BlockSpec((1, cvec.shape[0], tn), lambda l, j: (l, 0, j)),
        compiler_params=_cparams("arbitrary", "arbitrary"),
        name="modulation",
    )(cvec, mod_w, mod_b.reshape(depth, 1, w6))


def _rope_tables(n):
    rows = n // GRID_W
    row = jnp.repeat(jnp.arange(rows, dtype=F32), GRID_W)
    col = jnp.tile(jnp.arange(GRID_W, dtype=F32), rows)
    inv = ROPE_BASE ** (-jnp.arange(ROPE_FREQS, dtype=F32) / ROPE_FREQS)
    ang_r = row[:, None] * inv[None, :]
    ang_c = col[:, None] * inv[None, :]
    zeros = jnp.zeros_like(ang_r)
    cos64 = jnp.concatenate([jnp.cos(ang_r), jnp.cos(ang_r), jnp.cos(ang_c), jnp.cos(ang_c)], -1)
    sa64 = jnp.concatenate([-jnp.sin(ang_r), zeros, -jnp.sin(ang_c), zeros], -1)
    sb64 = jnp.concatenate([zeros, jnp.sin(ang_r), zeros, jnp.sin(ang_c)], -1)
    return tuple(jnp.tile(t, (1, LANES // HEAD_DIM)) for t in (cos64, sa64, sb64))


def _rot(t, tabs):
    if tabs is None:
        return t
    cos, sa, sb = tabs
    return t * cos + pltpu.roll(t, LANES - ROPE_FREQS, 1) * sa + pltpu.roll(t, ROPE_FREQS, 1) * sb


def _dup_halves(t):
    lo = lax.broadcasted_iota(jnp.int32, t.shape, 1) < HALF
    ta = jnp.where(lo, t, 0.0)
    tb = t - ta
    return ta + pltpu.roll(ta, HALF, 1), tb + pltpu.roll(tb, HALF, 1)


def _proj0_kernel(*refs, rope):
    if rope:
        x_ref, sc_ref, sh_ref, w_ref, cos_ref, sa_ref, sb_ref, q_ref, k_ref, v_ref, u_ref = refs
        tabs = (cos_ref[...], sa_ref[...], sb_ref[...])
    else:
        x_ref, sc_ref, sh_ref, w_ref, q_ref, k_ref, v_ref, u_ref = refs
        tabs = None
    h = (x_ref[0] * (1.0 + sc_ref[0]) + sh_ref[0]).astype(BF16)
    r = jnp.dot(h, w_ref[...], preferred_element_type=F32)
    scale = LOG2_E * HEAD_DIM ** -0.5
    for s in range(SWA_Q_W // LANES):
        q_ref[0, :, s * LANES:(s + 1) * LANES] = (_rot(r[:, s * LANES:(s + 1) * LANES], tabs) * scale).astype(BF16)
    k0, k1 = _dup_halves(_rot(r[:, SWA_Q_W:SWA_Q_W + LANES], tabs))
    vv = r[:, SWA_Q_W + LANES:SWA_Q_W + 2 * LANES]
    lo = lax.broadcasted_iota(jnp.int32, vv.shape, 1) < HALF
    k_ref[0, 0] = k0.astype(BF16)
    k_ref[0, 1] = k1.astype(BF16)
    v_ref[0, 0] = jnp.where(lo, vv, 1.0).astype(BF16)
    v_ref[0, 1] = jnp.where(lo, pltpu.roll(vv, HALF, 1), 1.0).astype(BF16)
    u_ref[0] = r[:, SWA_Q_W + 2 * LANES:]


def _proj0(x, sc, sh, w_bf16, tabs, tm):
    b, n, d = x.shape
    rope = tabs is not None
    in_specs = [pl.BlockSpec((1, tm, d), lambda i, j: (i, j, 0)), sc.spec(), sh.spec(),
                pl.BlockSpec(w_bf16.shape, lambda i, j: (0, 0))]
    args = [x, sc.table, sh.table, w_bf16]
    if rope:
        in_specs += [pl.BlockSpec((tm, LANES), lambda i, j: (j, 0))] * 3
        args += list(tabs)
    kv_spec = pl.BlockSpec((1, SWA_KV_HEADS, tm, LANES), lambda i, j: (i, 0, j, 0))
    return pl.pallas_call(
        functools.partial(_proj0_kernel, rope=rope),
        out_shape=(_sds((b, n, SWA_Q_W), BF16), _sds((b, SWA_KV_HEADS, n, LANES), BF16),
                   _sds((b, SWA_KV_HEADS, n, LANES), BF16), _sds((b, n, SSM_WIDTH), F32)),
        grid=(b, n // tm),
        in_specs=in_specs,
        out_specs=(pl.BlockSpec((1, tm, SWA_Q_W), lambda i, j: (i, j, 0)), kv_spec, kv_spec,
                   pl.BlockSpec((1, tm, SSM_WIDTH), lambda i, j: (i, j, 0))),
        compiler_params=_cparams("parallel", "parallel"),
        name="proj0_rope" if rope else "proj0_ctx",
    )(*args)


def _swa_kernel(*refs, tq, nsub, local, n_lat):
    if local:
        sink_ref, q_ref, k_ref, v_ref, kc_ref, vc_ref, o_ref = refs
    else:
        sink_ref, q_ref, kc_ref, vc_ref, o_ref = refs
    rows = 4 * tq
    lo = lax.broadcasted_iota(jnp.int32, (tq, LANES), 1) < HALF
    lo4 = lax.broadcasted_iota(jnp.int32, (rows, LANES), 1) < HALF
    rown = lax.broadcasted_iota(jnp.int32, (rows, 1), 0)
    for sub in range(nsub):
        j = pl.program_id(1) * nsub + sub
        r0 = sub * tq
        if local:
            span = 3 * SWA_BLOCK
            start = pl.multiple_of(jnp.clip((j - 1) * SWA_BLOCK, 0, n_lat - span), SWA_BLOCK)
            rr = lax.broadcasted_iota(jnp.int32, (rows, span), 0)
            cc = lax.broadcasted_iota(jnp.int32, (rows, span), 1)
            qpos = j * tq + (rr & (tq - 1))
            mask = jnp.abs(qpos - (start + cc)) <= SWA_WINDOW
        for h in range(SWA_KV_HEADS):
            qa = q_ref[0, r0:r0 + tq, (2 * h) * LANES:(2 * h + 1) * LANES].astype(F32)
            qb = q_ref[0, r0:r0 + tq, (2 * h + 1) * LANES:(2 * h + 2) * LANES].astype(F32)
            q4 = jnp.concatenate([jnp.where(lo, qa, 0.0), jnp.where(lo, 0.0, qa),
                                  jnp.where(lo, qb, 0.0), jnp.where(lo, 0.0, qb)], 0).astype(BF16)
            sink = LOG2_E * jnp.where(rown < tq, sink_ref[4 * h],
                                      jnp.where(rown < 2 * tq, sink_ref[4 * h + 1],
                                                jnp.where(rown < 3 * tq, sink_ref[4 * h + 2], sink_ref[4 * h + 3])))
            s_ctx = _nt_dot(q4, kc_ref[0, h])
            m = jnp.maximum(jnp.max(s_ctx, -1, keepdims=True), sink)
            if local:
                s_loc = jnp.where(mask, _nt_dot(q4, k_ref[0, h, pl.ds(start, span), :]), NEG_INF)
                m = jnp.maximum(m, jnp.max(s_loc, -1, keepdims=True))
            o4 = jnp.dot(jnp.exp2(s_ctx - m).astype(BF16), vc_ref[0, h], preferred_element_type=F32)
            if local:
                o4 = o4 + jnp.dot(jnp.exp2(s_loc - m).astype(BF16), v_ref[0, h, pl.ds(start, span), :],
                                  preferred_element_type=F32)
            o4 = o4 + jnp.where(lo4, 0.0, jnp.exp2(sink - m))
            o4 = o4 * (1.0 / jnp.where(lo4, pltpu.roll(o4, HALF, 1), 1.0))
            for s in range(2):
                even, odd = o4[2 * s * tq:(2 * s + 1) * tq], o4[(2 * s + 1) * tq:(2 * s + 2) * tq]
                o_ref[0, r0:r0 + tq, (2 * h + s) * LANES:(2 * h + s + 1) * LANES] = jnp.where(
                    lo, even, pltpu.roll(odd, HALF, 1)).astype(BF16)


def _swa_attention(sink, q, k, v, kc, vc):
    b, n, _ = q.shape
    nc = kc.shape[2]
    tq, nsub = SWA_BLOCK, SWA_BLOCKS_PER_STEP
    full = lambda m: pl.BlockSpec((1, SWA_KV_HEADS, m, LANES), lambda i, j: (i, 0, 0, 0))
    return pl.pallas_call(
        functools.partial(_swa_kernel, tq=tq, nsub=nsub, local=True, n_lat=n),
        out_shape=_sds((b, n, SWA_Q_W), BF16),
        grid=(b, n // (tq * nsub)),
        in_specs=[pl.BlockSpec(memory_space=pltpu.SMEM),
                  pl.BlockSpec((1, tq * nsub, SWA_Q_W), lambda i, j: (i, j, 0)),
                  full(n), full(n), full(nc), full(nc)],
        out_specs=pl.BlockSpec((1, tq * nsub, SWA_Q_W), lambda i, j: (i, j, 0)),
        compiler_params=_cparams("parallel", "arbitrary"),
        name="swa_attention",
    )(sink, q, k, v, kc, vc)


def _ctx_attention(sink, qc, kc, vc):
    b, nc, _ = qc.shape
    full = pl.BlockSpec((1, SWA_KV_HEADS, nc, LANES), lambda i, j: (i, 0, 0, 0))
    return pl.pallas_call(
        functools.partial(_swa_kernel, tq=nc, nsub=1, local=False, n_lat=0),
        out_shape=_sds((b, nc, SWA_Q_W), BF16),
        grid=(b, 1),
        in_specs=[pl.BlockSpec(memory_space=pltpu.SMEM),
                  pl.BlockSpec((1, nc, SWA_Q_W), lambda i, j: (i, 0, 0)), full, full],
        out_specs=pl.BlockSpec((1, nc, SWA_Q_W), lambda i, j: (i, 0, 0)),
        compiler_params=_cparams("parallel", "arbitrary"),
        name="ctx_attention",
    )(sink, qc, kc, vc)


def _ssm_matrices(a_re, a_im, log_step, b_re, b_im, c_re, c_im):
    L = SSM_CHUNK
    ar = a_re.astype(F32)
    ai = a_im.astype(F32)
    dt = jnp.exp(log_step.astype(F32))[..., None]
    m = jnp.arange(L + 1, dtype=F32)[:, None, None, None]
    mag = jnp.exp(m * (dt * ar)[None])
    pw_re = mag * jnp.cos(m * (dt * ai)[None])
    pw_im = mag * jnp.sin(m * (dt * ai)[None])
    den = ar * ar + ai * ai
    nr = pw_re[1] - 1.0
    coef_re = (nr * ar + pw_im[1] * ai) / den
    coef_im = (pw_im[1] * ar - nr * ai) / den
    br = b_re.astype(F32)
    bi = b_im.astype(F32)
    bb_re = coef_re[..., None] * br - coef_im[..., None] * bi
    bb_im = coef_re[..., None] * bi + coef_im[..., None] * br
    cr = c_re.astype(F32)[None]
    ci = c_im.astype(F32)[None]
    ca_re = cr * pw_re[:, :, :, None, :] - ci * pw_im[:, :, :, None, :]
    ca_im = cr * pw_im[:, :, :, None, :] + ci * pw_re[:, :, :, None, :]
    kern = jnp.sum(ca_re[:L, ..., None] * bb_re[None, :, :, None] - ca_im[:L, ..., None] * bb_im[None, :, :, None],
                   axis=4)
    G, C = SSM_GROUPS, SSM_GROUP
    k_f = jnp.transpose(kern[:, 0], (1, 3, 0, 2)).reshape(G, C, SSM_CW)
    k_r = jnp.transpose(kern[::-1, 1], (1, 3, 0, 2)).reshape(G, C, SSM_CW)
    span = (2 * L - 1) * C
    lagged = (jnp.pad(k_f, ((0, 0), (0, 0), ((L - 1) * C, 0)))
              + jnp.pad(k_r, ((0, 0), (0, 0), (0, (L - 1) * C))))
    wide = jnp.pad(lagged, ((0, 0), (0, 0), (0, C))).astype(BF16)
    flat = jnp.broadcast_to(wide[:, :, None, :], (G, C, L, span + C)).reshape(G, C, L * (span + C))
    skew = flat[:, :, (L - 1) * C:(L - 1) * C + L * span].reshape(G, C, L, span)[..., :SSM_CW]
    m_intra = jnp.transpose(skew, (0, 2, 1, 3)).reshape(G, SSM_CW, SSM_CW)
    both = lambda t_f, t_r: jnp.concatenate([t_f, t_r], -1)
    pin_re = jnp.transpose(both(pw_re[:L, 0][::-1], pw_re[:L, 1]), (1, 0, 2))[:, :, None, :]
    pin_im = jnp.transpose(both(pw_im[:L, 0][::-1], pw_im[:L, 1]), (1, 0, 2))[:, :, None, :]
    bt_re = both(jnp.transpose(bb_re[0], (0, 2, 1)), jnp.transpose(bb_re[1], (0, 2, 1)))[:, None]
    bt_im = both(jnp.transpose(bb_im[0], (0, 2, 1)), jnp.transpose(bb_im[1], (0, 2, 1)))[:, None]
    m_in = jnp.concatenate([pin_re * bt_re - pin_im * bt_im, pin_re * bt_im + pin_im * bt_re],
                           -1).reshape(G, SSM_CW, SSM_CW)
    pst_re = jnp.transpose(both(pw_re[1:, 0], pw_re[1:, 1][::-1]), (1, 0, 2))[:, :, None, :]
    pst_im = jnp.transpose(both(pw_im[1:, 0], pw_im[1:, 1][::-1]), (1, 0, 2))[:, :, None, :]
    ct_re = both(c_re[0].astype(F32), c_re[1].astype(F32))[:, None]
    ct_im = both(c_im[0].astype(F32), c_im[1].astype(F32))[:, None]
    m_state_t = jnp.concatenate([pst_re * ct_re - pst_im * ct_im, -(pst_re * ct_im + pst_im * ct_re)],
                                -1).reshape(G, SSM_CW, SSM_CW)
    a_l = jnp.stack([both(pw_re[L, 0], pw_re[L, 1]), both(pw_im[L, 0], pw_im[L, 1])], 1)
    return m_in.astype(BF16), m_intra.astype(BF16), m_state_t.astype(BF16), a_l


def _ssm_pack_kernel(uc_ref, u_ref, x_ref, *, nc_ctx, nc_lat):
    for src, row0, nch in ((uc_ref, 0, nc_ctx), (u_ref, nc_ctx, nc_lat)):
        steps = [src[0, pl.ds(j, nch, stride=SSM_CHUNK), :] for j in range(SSM_CHUNK)]
        lane = lax.broadcasted_iota(jnp.int32, (nch, LANES), 1)
        piece = [(lane >= jj * SSM_GROUP) & (lane < (jj + 1) * SSM_GROUP) for jj in range(SSM_SLAB_GROUPS)]
        for g in range(SSM_SLAB_GROUPS):
            for h in range(SSM_CW // LANES):
                acc = jnp.zeros((nch, LANES), F32)
                for jj in range(SSM_SLAB_GROUPS):
                    z = steps[h * SSM_SLAB_GROUPS + jj]
                    shift = ((jj - g) * SSM_GROUP) % LANES
                    acc = jnp.where(piece[jj], z if shift == 0 else pltpu.roll(z, shift, 1), acc)
                x_ref[g, row0:row0 + nch, h * LANES:(h + 1) * LANES] = acc.astype(BF16)


def _ssm_unpack_kernel(y_ref, oc_ref, o_ref, *, nc_ctx, nc_lat):
    for dst, row0, nch in ((oc_ref, 0, nc_ctx), (o_ref, nc_ctx, nc_lat)):
        lane = lax.broadcasted_iota(jnp.int32, (nch, LANES), 1)
        piece = [(lane >= g * SSM_GROUP) & (lane < (g + 1) * SSM_GROUP) for g in range(SSM_SLAB_GROUPS)]
        for i in range(SSM_CHUNK):
            h, ii = divmod(i, SSM_SLAB_GROUPS)
            acc = jnp.zeros((nch, LANES), F32)
            for g in range(SSM_SLAB_GROUPS):
                z = y_ref[g, row0:row0 + nch, h * LANES:(h + 1) * LANES]
                shift = ((g - ii) * SSM_GROUP) % LANES
                acc = jnp.where(piece[g], z if shift == 0 else pltpu.roll(z, shift, 1), acc)
            dst[0, pl.ds(i, nch, stride=SSM_CHUNK), :] = acc


def _ssm_kernel(x_ref, min_ref, mintra_ref, mstate_ref, al_ref, y_ref, v_scr, s_scr, *, nb, nc_ctx, n_chunks):
    n_grp = x_ref.shape[0]
    for g in range(n_grp):
        v = jnp.dot(x_ref[g], min_ref[g], preferred_element_type=F32)
        v_scr[g, 0] = v[:, 0:LANES]
        v_scr[g, 1] = v[:, LANES:]
    ar = [al_ref[g, 0:1, :] for g in range(n_grp)]
    ai = [al_ref[g, 1:2, :] for g in range(n_grp)]
    lo = lax.broadcasted_iota(jnp.int32, (nb, LANES), 1) < HALF

    def body(k, carry):
        kr = jnp.where(k < nc_ctx, nc_ctx - 1 - k, n_chunks - 1 + nc_ctx - k)
        rf = pl.ds(k, nb, stride=n_chunks)
        rr = pl.ds(kr, nb, stride=n_chunks)
        out = []
        for g in range(n_grp):
            sre, sim = carry[2 * g], carry[2 * g + 1]
            s_scr[g, 0, rf, :] = sre
            s_scr[g, 1, rr, :] = sre
            s_scr[g, 2, rf, :] = sim
            s_scr[g, 3, rr, :] = sim
            vre = jnp.where(lo, v_scr[g, 0, rf, :], v_scr[g, 0, rr, :])
            vim = jnp.where(lo, v_scr[g, 1, rf, :], v_scr[g, 1, rr, :])
            out += [ar[g] * sre - ai[g] * sim + vre, ar[g] * sim + ai[g] * sre + vim]
        return tuple(out)

    zero = jnp.zeros((nb, LANES), F32)
    lax.fori_loop(0, n_chunks, body, (zero,) * (2 * n_grp))
    lo_all = lax.broadcasted_iota(jnp.int32, (s_scr.shape[2], LANES), 1) < HALF
    for g in range(n_grp):
        s_in = jnp.concatenate([jnp.where(lo_all, s_scr[g, 0], s_scr[g, 1]),
                                jnp.where(lo_all, s_scr[g, 2], s_scr[g, 3])], axis=1).astype(BF16)
        y_ref[g] = jnp.dot(x_ref[g], mintra_ref[g], preferred_element_type=F32) + _nt_dot(s_in, mstate_ref[g])


def _ssm_scan(u, uc, mats):
    m_in, m_intra, m_state, a_l = mats
    b, n, _ = u.shape
    nc = uc.shape[1]
    nc_ctx, nc_lat = nc // SSM_CHUNK, n // SSM_CHUNK
    n_chunks = nc_ctx + nc_lat
    r = n_chunks * b
    nat = lambda m: pl.BlockSpec((1, m, LANES), lambda i, s: (i, 0, s))
    grp = pl.BlockSpec((SSM_SLAB_GROUPS, n_chunks, SSM_CW), lambda i, s: (s, i, 0))
    xg = pl.pallas_call(
        functools.partial(_ssm_pack_kernel, nc_ctx=nc_ctx, nc_lat=nc_lat),
        out_shape=_sds((SSM_GROUPS, r, SSM_CW), BF16),
        grid=(b, SSM_GROUPS // SSM_SLAB_GROUPS),
        in_specs=[nat(nc), nat(n)],
        out_specs=grp,
        compiler_params=_cparams("parallel", "parallel"),
        name="ssm_pack",
    )(uc, u)
    gs = SSM_GROUPS_PER_STEP
    mat = pl.BlockSpec((gs, SSM_CW, SSM_CW), lambda g: (g, 0, 0))
    yg = pl.pallas_call(
        functools.partial(_ssm_kernel, nb=b, nc_ctx=nc_ctx, n_chunks=n_chunks),
        out_shape=_sds((SSM_GROUPS, r, SSM_CW), F32),
        grid=(SSM_GROUPS // gs,),
        in_specs=[pl.BlockSpec((gs, r, SSM_CW), lambda g: (g, 0, 0)), mat, mat, mat,
                  pl.BlockSpec((gs, 2, LANES), lambda g: (g, 0, 0))],
        out_specs=pl.BlockSpec((gs, r, SSM_CW), lambda g: (g, 0, 0)),
        scratch_shapes=[pltpu.VMEM((gs, 2, r, LANES), F32), pltpu.VMEM((gs, 4, r, LANES), F32)],
        compiler_params=_cparams("parallel"),
        name="ssm_scan",
    )(xg, m_in, m_intra, m_state, a_l)
    ysc, ys = pl.pallas_call(
        functools.partial(_ssm_unpack_kernel, nc_ctx=nc_ctx, nc_lat=nc_lat),
        out_shape=(_sds((b, nc, SSM_WIDTH), F32), _sds((b, n, SSM_WIDTH), F32)),
        grid=(b, SSM_GROUPS // SSM_SLAB_GROUPS),
        in_specs=[grp],
        out_specs=(nat(nc), nat(n)),
        compiler_params=_cparams("parallel", "parallel"),
        name="ssm_unpack",
    )(yg)
    return ys, ysc


def _split_rows(ref, val):
    m = val.shape[0]
    for s in range(TOKEN_TILE_ROWS):
        ref[pl.ds(s, m, stride=TOKEN_TILE_ROWS), :] = val[:, s * LANES:(s + 1) * LANES]


def _merge_rows(ref):
    m = ref.shape[0] // TOKEN_TILE_ROWS
    return jnp.concatenate([ref[pl.ds(s, m, stride=TOKEN_TILE_ROWS), :] for s in range(TOKEN_TILE_ROWS)], axis=1)


def _token_tile(ref, t):
    return ref.at[pl.ds(pl.multiple_of(t * TOKEN_TILE_ROWS, TOKEN_TILE_ROWS), TOKEN_TILE_ROWS)]


def _router_logits(w, bias, h):
    w_hi, h_hi = w.astype(BF16), h.astype(BF16)
    w_lo, h_lo = (w - w_hi.astype(F32)).astype(BF16), (h - h_hi.astype(F32)).astype(BF16)
    return _nt_dot(w_hi, h_hi) + (_nt_dot(w_hi, h_lo) + _nt_dot(w_lo, h_hi)) + bias


def _post_kernel(*refs, alpha, with_ssm):
    wr_ref, br_ref = refs[-5:-3]
    refs = refs[:-5] + refs[-3:]
    if with_ssm:
        (att_ref, ys_ref, u_ref, dsk_ref, gw_ref, gb_ref, wo_ref, x_ref, g1_ref, lg_ref, lb_ref, sc2_ref, sh2_ref,
         x1_ref, h2_ref, lgt_ref) = refs
        y = ys_ref[0] + u_ref[0] * dsk_ref[...]
        gl = jax.nn.gelu(y)
        gate = jax.nn.sigmoid(jnp.dot(gl.astype(BF16), gw_ref[...], preferred_element_type=F32) + gb_ref[...])
        ssm = (gl * gate).astype(BF16)
        mix = (jnp.dot(att_ref[0], wo_ref[0:SWA_Q_W, :], preferred_element_type=F32)
               + jnp.dot(ssm, wo_ref[SWA_Q_W:, :], preferred_element_type=F32))
    else:
        att_ref, wo_ref, x_ref, g1_ref, lg_ref, lb_ref, sc2_ref, sh2_ref, x1_ref, h2_ref, lgt_ref = refs
        mix = jnp.dot(att_ref[0], wo_ref[...], preferred_element_type=F32)
    x1 = _layer_norm(alpha * x_ref[0] + g1_ref[0] * mix, lg_ref[...], lb_ref[...])
    x1_ref[0] = x1
    h2 = x1 * (1.0 + sc2_ref[0]) + sh2_ref[0]
    _split_rows(h2_ref, h2)
    lgt_ref[...] = _router_logits(wr_ref[...], br_ref[...], h2)


def _post(att, ssm_args, w_out_bf16, x, g1, ln_g, ln_b, sc2, sh2, router_wb, alpha, tm):
    b, n, d = x.shape
    tok = lambda w: pl.BlockSpec((1, tm, w), lambda i, j: (i, j, 0))
    const = lambda a: pl.BlockSpec(a.shape, lambda i, j: (0,) * a.ndim)
    in_specs = [tok(att.shape[-1])]
    args = [att]
    if ssm_args is not None:
        ys, u, dsk, gw, gb = ssm_args
        in_specs += [tok(SSM_WIDTH), tok(SSM_WIDTH), const(dsk), const(gw), const(gb)]
        args += [ys, u, dsk, gw, gb]
    in_specs += [const(w_out_bf16), tok(d), g1.spec(), const(ln_g), const(ln_b), sc2.spec(), sh2.spec()]
    args += [w_out_bf16, x, g1.table, ln_g, ln_b, sc2.table, sh2.table]
    in_specs += [const(a) for a in router_wb]
    args += list(router_wb)
    per_b = n // tm
    n_logit = router_wb[0].shape[0]
    return pl.pallas_call(
        functools.partial(_post_kernel, alpha=alpha, with_ssm=ssm_args is not None),
        out_shape=(_sds((b, n, d), F32), _sds((b * n * TOKEN_TILE_ROWS, LANES), F32), _sds((n_logit, b * n), F32)),
        grid=(b, n // tm),
        in_specs=in_specs,
        out_specs=(tok(d), pl.BlockSpec((tm * TOKEN_TILE_ROWS, LANES), lambda i, j: (i * per_b + j, 0)),
                   pl.BlockSpec((n_logit, tm), lambda i, j: (0, i * per_b + j))),
        compiler_params=_cparams("parallel", "parallel"),
        name="post_mixer_ssm" if ssm_args is not None else "post_mixer",
    )(*args)


def _first_max(v, sub):
    m = jnp.max(v, 0, keepdims=True)
    idx = jnp.min(jnp.where(v == m, sub, float(SUBLANES)), 0, keepdims=True)
    return m, idx


def _stream_blocks(toks, tb):
    starts = [0]
    for t in toks:
        starts.append(starts[-1] + t.shape[0] // (tb * TOKEN_TILE_ROWS))
    return starts


def _stream_spec(tok, tb, start):
    last = tok.shape[0] // (tb * TOKEN_TILE_ROWS) - 1
    return pl.BlockSpec((tb * TOKEN_TILE_ROWS, LANES), lambda i, *_: (jnp.clip(i - start, 0, last), 0))


def _router_kernel(*refs, starts):
    n_streams = len(starts) - 1
    lgt_refs = refs[:n_streams]
    ids_ref, wts_ref, rank_ref, cnt_ref, carry_scr = refs[n_streams:]
    step = pl.program_id(0)

    @pl.when(step == 0)
    def _():
        carry_scr[...] = jnp.zeros_like(carry_scr)

    logits = lgt_refs[0][...]
    for ref, start in zip(lgt_refs[1:], starts[1:]):
        logits = jnp.where(step >= start, ref[...], logits)
    tm = logits.shape[1]
    sub = lax.broadcasted_iota(jnp.int32, (SUBLANES, tm), 0).astype(F32)
    gl = logits[0:SUBLANES]
    gmax, gi = _first_max(gl, sub)
    gp = 1.0 / jnp.sum(jnp.exp(gl - gmax), 0, keepdims=True)
    le = logits[ROUTER_EXPERT_ROW0:ROUTER_EXPERT_ROW0 + MOE_EPG]
    for g in range(1, MOE_GROUPS):
        le = jnp.where(gi == float(g), logits[ROUTER_EXPERT_ROW0 + g * MOE_EPG:ROUTER_EXPERT_ROW0 + (g + 1) * MOE_EPG], le)
    m1, i1 = _first_max(le, sub)
    m2, i2 = _first_max(jnp.where(sub == i1, NEG_INF, le), sub)
    t = jnp.exp(m2 - m1)
    e1 = gi * float(MOE_EPG) + i1
    e2 = gi * float(MOE_EPG) + i2
    ids_ref[0:1, :] = e1.astype(jnp.int32)
    ids_ref[1:2, :] = e2.astype(jnp.int32)
    wts_ref[0:1, :] = gp / (1.0 + t)
    wts_ref[1:2, :] = gp * t / (1.0 + t)
    esub = lax.broadcasted_iota(jnp.int32, (MOE_EXPERTS, tm), 0).astype(F32)
    oh1 = (esub == e1).astype(F32)
    oh2 = (esub == e2).astype(F32)
    both = oh1 + oh2
    earlier = (lax.broadcasted_iota(jnp.int32, (tm, tm), 0) < lax.broadcasted_iota(jnp.int32, (tm, tm), 1))
    prefix = jnp.dot(both.astype(BF16), earlier.astype(BF16), preferred_element_type=F32) + carry_scr[...]
    rank_ref[0:1, :] = jnp.sum(oh1 * prefix, 0, keepdims=True).astype(jnp.int32)
    rank_ref[1:2, :] = jnp.sum(oh2 * prefix, 0, keepdims=True).astype(jnp.int32)
    carry_scr[...] += jnp.sum(both, 1, keepdims=True)
    cnt_ref[...] = jnp.broadcast_to(carry_scr[...], cnt_ref.shape)


def _router_weights(wg, bg, we, be):
    d = wg.shape[0]
    rows = ROUTER_EXPERT_ROW0 + MOE_EXPERTS
    w = jnp.zeros((rows, d), F32)
    w = w.at[:MOE_GROUPS].set(wg.T)
    w = w.at[ROUTER_EXPERT_ROW0:].set(jnp.transpose(we, (0, 2, 1)).reshape(MOE_EXPERTS, d))
    bias = jnp.full((rows, 1), NEG_INF, F32)
    bias = bias.at[:MOE_GROUPS, 0].set(bg)
    bias = bias.at[ROUTER_EXPERT_ROW0:, 0].set(be.reshape(-1))
    return w, bias


def _router(logit_streams, tm):
    starts = [0]
    for lg in logit_streams:
        starts.append(starts[-1] + lg.shape[1] // tm)
    t = starts[-1] * tm
    rows = logit_streams[0].shape[0]
    stream_spec = lambda lg, s0: pl.BlockSpec(
        (rows, tm), lambda i: (0, jnp.clip(i - s0, 0, lg.shape[1] // tm - 1)))
    pair = pl.BlockSpec((MOE_TOPK, tm), lambda i: (0, i))
    ids, wts, rank, cnt = pl.pallas_call(
        functools.partial(_router_kernel, starts=starts),
        out_shape=(_sds((MOE_TOPK, t), jnp.int32), _sds((MOE_TOPK, t), F32), _sds((MOE_TOPK, t), jnp.int32),
                   _sds((MOE_EXPERTS, LANES), F32)),
        grid=(t // tm,),
        in_specs=[stream_spec(lg, s0) for lg, s0 in zip(logit_streams, starts)],
        out_specs=(pair, pair, pair, pl.BlockSpec((MOE_EXPERTS, LANES), lambda i: (0, 0))),
        scratch_shapes=[pltpu.VMEM((MOE_EXPERTS, 1), F32)],
        compiler_params=_cparams("arbitrary"),
        name="moe_router",
    )(*logit_streams)
    return ids, wts, rank, cnt[:, 0].astype(jnp.int32)


def _moe_plan(ids, rank, counts, n_tok):
    tm = MOE_ROW_TILE
    padded = ((counts + tm - 1) // tm) * tm
    ends = jnp.cumsum(padded)
    offs = ends - padded
    experts = jnp.arange(MOE_EXPERTS, dtype=jnp.int32)
    dest = (jnp.sum(jnp.where(ids[..., None] == experts, offs, 0), -1) + rank).astype(jnp.int32)
    n_tiles = (MOE_TOPK * n_tok + MOE_EXPERTS * (tm - 1)) // tm
    starts = jnp.arange(n_tiles, dtype=jnp.int32) * tm
    tile_expert = jnp.minimum(jnp.sum((ends[None, :] <= starts[:, None]).astype(jnp.int32), -1), MOE_EXPERTS - 1)
    n_valid = (ends[-1] // tm).astype(jnp.int32).reshape(1)
    pad = jnp.stack([offs + counts, padded - counts, jnp.broadcast_to(n_valid, counts.shape)]).astype(jnp.int32)
    return dest, tile_expert, n_valid, pad, n_tiles * tm


def _pad_fill(pad_ref, zero_scr, xs_ref, sem, wait):
    def per_expert(e, carry):
        first, count = pad_ref[0, e], pad_ref[1, e]
        piece = MOE_ROW_TILE // 2
        while piece >= 1:
            row = first + (count & ~(2 * piece - 1))
            n = piece * TOKEN_TILE_ROWS

            @pl.when((count & piece) != 0)
            def _(row=row, n=n):
                copy = pltpu.make_async_copy(
                    zero_scr.at[pl.ds(0, n)],
                    xs_ref.at[pl.ds(pl.multiple_of(row * TOKEN_TILE_ROWS, TOKEN_TILE_ROWS), n)], sem)
                copy.wait() if wait else copy.start()

            piece //= 2
        return carry

    lax.fori_loop(0, MOE_EXPERTS, per_expert, 0)
    half = MOE_ROW_TILE // 2 * TOKEN_TILE_ROWS
    n_tiles = xs_ref.shape[0] // (2 * half)

    def per_tile(t, carry):
        for h in range(2):
            copy = pltpu.make_async_copy(zero_scr, xs_ref.at[pl.ds(pl.multiple_of((2 * t + h) * half, half), half)],
                                         sem)
            copy.wait() if wait else copy.start()
        return carry

    lax.fori_loop(pad_ref[2, 0], n_tiles, per_tile, 0)


def _dispatch_kernel(*refs, tb, starts):
    n_streams = len(starts) - 1
    dest_ref, pad_ref = refs[:2]
    tok_refs = refs[2:2 + n_streams]
    xs_ref, zero_scr, sem, pad_sem = refs[2 + n_streams:]
    step = pl.program_id(0)

    @pl.when(step == 0)
    def _():
        zero_scr[...] = jnp.zeros_like(zero_scr)
        _pad_fill(pad_ref, zero_scr, xs_ref, pad_sem, wait=False)

    for s, tok_ref in enumerate(tok_refs):
        @pl.when((step >= starts[s]) & (step < starts[s + 1]))
        def _(tok_ref=tok_ref):
            def body(r, carry):
                for k in range(MOE_TOPK):
                    pltpu.make_async_copy(_token_tile(tok_ref, r), _token_tile(xs_ref, dest_ref[0, k, r]),
                                          sem).start(priority=k % 2)
                return carry

            lax.fori_loop(0, tb, body, 0, unroll=8)

    for k in range(MOE_TOPK):
        pltpu.make_async_copy(tok_refs[0], xs_ref.at[pl.ds(0, tb * TOKEN_TILE_ROWS)], sem).wait()

    @pl.when(step == 0)
    def _():
        _pad_fill(pad_ref, zero_scr, xs_ref, pad_sem, wait=True)


def _dispatch(toks, dest3, pad, n_rows):
    nblk, _, tb = dest3.shape
    starts = _stream_blocks(toks, tb)
    return pl.pallas_call(
        functools.partial(_dispatch_kernel, tb=tb, starts=starts),
        out_shape=_sds((n_rows * TOKEN_TILE_ROWS, LANES), F32),
        grid=(nblk,),
        in_specs=[pl.BlockSpec((1, MOE_TOPK, tb), lambda i: (i, 0, 0), memory_space=pltpu.SMEM),
                  pl.BlockSpec(memory_space=pltpu.SMEM)]
        + [_stream_spec(tok, tb, s0) for tok, s0 in zip(toks, starts)],
        out_specs=pl.BlockSpec(memory_space=pl.ANY),
        scratch_shapes=[pltpu.VMEM((MOE_ROW_TILE // 2 * TOKEN_TILE_ROWS, LANES), F32),
                        pltpu.SemaphoreType.DMA(()), pltpu.SemaphoreType.DMA(())],
        compiler_params=_cparams("arbitrary"),
        name="moe_dispatch",
    )(dest3, pad, *toks)


def _ffn_kernel(te_ref, nv_ref, x_ref, w1_ref, w3_ref, w2_ref, y_ref, w13_scr, w2_scr):
    i = pl.program_id(0)
    f = w1_ref.shape[2]

    @pl.when((i == 0) | (te_ref[i] != te_ref[jnp.maximum(i - 1, 0)]))
    def _():
        w13_scr[:, 0:f] = w1_ref[0].astype(BF16)
        w13_scr[:, f:2 * f] = w3_ref[0].astype(BF16)
        w2_scr[...] = w2_ref[0].astype(BF16)

    @pl.when(i < nv_ref[0])
    def _():
        h13 = jnp.dot(_merge_rows(x_ref).astype(BF16), w13_scr[...], preferred_element_type=F32)
        h1 = h13[:, 0:f]
        hh = (h1 * jax.nn.sigmoid(h1) * h13[:, f:2 * f]).astype(BF16)
        _split_rows(y_ref, jnp.dot(hh, w2_scr[...], preferred_element_type=F32))

    @pl.when(i >= nv_ref[0])
    def _():
        y_ref[...] = jnp.zeros_like(y_ref)


def _expert_ffn(tile_expert, n_valid, xs, w1, w3, w2):
    p = xs.shape[0] // TOKEN_TILE_ROWS
    _, d, f = w1.shape
    tm = MOE_ROW_TILE
    rows = pl.BlockSpec((tm * TOKEN_TILE_ROWS, LANES), lambda i, te, nv: (i, 0))
    rows_in = pl.BlockSpec((tm * TOKEN_TILE_ROWS, LANES), lambda i, te, nv: (jnp.minimum(i, nv[0] - 1), 0))
    return pl.pallas_call(
        _ffn_kernel,
        out_shape=_sds(xs.shape, F32),
        grid_spec=pltpu.PrefetchScalarGridSpec(
            num_scalar_prefetch=2,
            grid=(p // tm,),
            in_specs=[rows_in,
                      pl.BlockSpec((1, d, f), lambda i, te, nv: (te[i], 0, 0)),
                      pl.BlockSpec((1, d, f), lambda i, te, nv: (te[i], 0, 0)),
                      pl.BlockSpec((1, f, d), lambda i, te, nv: (te[i], 0, 0))],
            out_specs=rows,
            scratch_shapes=[pltpu.VMEM((d, 2 * f), BF16), pltpu.VMEM((f, d), BF16)]),
        compiler_params=_cparams("arbitrary"),
        name="moe_expert_ffn",
    )(tile_expert, n_valid, xs, w1, w3, w2)


def _combine_ln2_kernel(dest_ref, next_ref, x1_ref, ys_ref, wt_ref, g2_ref, lg_ref, lb_ref, o_ref, buf, sem, *,
                        alpha, tb, nblk):
    step = pl.program_id(0)

    def gather(d_ref, slot):
        def body(r, carry):
            for k in range(MOE_TOPK):
                pltpu.make_async_copy(_token_tile(ys_ref, d_ref[0, k, r]), _token_tile(buf.at[slot, k], r),
                                      sem.at[slot, k]).start(priority=k % 2)
            return carry

        lax.fori_loop(0, tb, body, 0, unroll=8)

    @pl.when(step == 0)
    def _():
        gather(dest_ref, 0)

    @pl.when(step + 1 < nblk)
    def _():
        gather(next_ref, (step + 1) % 2)

    slot = step % 2
    for k in range(MOE_TOPK):
        pltpu.make_async_copy(ys_ref.at[pl.ds(0, tb * TOKEN_TILE_ROWS)], buf.at[slot, k], sem.at[slot, k]).wait()
    f = wt_ref[:, 0:1] * _merge_rows(buf.at[slot, 0]) + wt_ref[:, 1:2] * _merge_rows(buf.at[slot, 1])
    o_ref[...] = _layer_norm(alpha * x1_ref[...] + g2_ref[0] * f, lg_ref[...], lb_ref[...])


def _combine_ln2(dest3, x1, ys, wt, g2, ln_g, ln_b, alpha, n_per_sample):
    t, d = x1.shape
    nblk, _, tb = dest3.shape
    const = pl.BlockSpec((1, d), lambda i: (0, 0))
    assert g2.stride == 0 or n_per_sample % tb == 0, (n_per_sample, tb)
    return pl.pallas_call(
        functools.partial(_combine_ln2_kernel, alpha=alpha, tb=tb, nblk=nblk),
        out_shape=_sds((t, d), F32),
        grid=(nblk,),
        in_specs=[pl.BlockSpec((1, MOE_TOPK, tb), lambda i: (i, 0, 0), memory_space=pltpu.SMEM),
                  pl.BlockSpec((1, MOE_TOPK, tb), lambda i: (jnp.minimum(i + 1, nblk - 1), 0, 0),
                               memory_space=pltpu.SMEM),
                  pl.BlockSpec((tb, d), lambda i: (i, 0)),
                  pl.BlockSpec(memory_space=pl.ANY),
                  pl.BlockSpec((tb, MOE_TOPK), lambda i: (i, 0)),
                  g2.spec(lambda i: (i * tb) // n_per_sample),
                  const, const],
        out_specs=pl.BlockSpec((tb, d), lambda i: (i, 0)),
        scratch_shapes=[pltpu.VMEM((2, MOE_TOPK, tb * TOKEN_TILE_ROWS, LANES), F32),
                        pltpu.SemaphoreType.DMA((2, MOE_TOPK))],
        compiler_params=_cparams("arbitrary"),
        name="moe_combine_ln2",
    )(dest3, dest3, x1, ys, wt, g2.table, ln_g, ln_b)


def _proj1_kernel(*refs, rope, with_q):
    x_ref, sc_ref, sh_ref, w_ref = refs[:4]
    refs = refs[4:]
    tabs = None
    if rope:
        tabs = (refs[0][...], refs[1][...], refs[2][...])
        refs = refs[3:]
    h = (x_ref[0] * (1.0 + sc_ref[0]) + sh_ref[0]).astype(BF16)
    r = jnp.dot(h, w_ref[...], preferred_element_type=F32)
    off = 0
    if with_q:
        q_ref, k_ref, v_ref = refs
        for hd in range(DIF_HEADS):
            q_ref[0, hd] = (_rot(r[:, hd * LANES:(hd + 1) * LANES], tabs) * (LOG2_E * HEAD_DIM ** -0.5)).astype(BF16)
        off = DIF_QK_W
    else:
        k_ref, v_ref = refs
    for hd in range(DIF_HEADS):
        k_ref[0, hd] = _rot(r[:, off + hd * LANES:off + (hd + 1) * LANES], tabs).astype(BF16)
        v_ref[0, hd] = r[:, off + DIF_QK_W + hd * LANES:off + DIF_QK_W + (hd + 1) * LANES].astype(BF16)


def _proj1(x, sc, sh, w_bf16, tabs, with_q, tm):
    b, n, d = x.shape
    rope = tabs is not None
    in_specs = [pl.BlockSpec((1, tm, d), lambda i, j: (i, j, 0)), sc.spec(), sh.spec(),
                pl.BlockSpec(w_bf16.shape, lambda i, j: (0, 0))]
    args = [x, sc.table, sh.table, w_bf16]
    if rope:
        in_specs += [pl.BlockSpec((tm, LANES), lambda i, j: (j, 0))] * 3
        args += list(tabs)
    hm = pl.BlockSpec((1, DIF_HEADS, tm, LANES), lambda i, j: (i, 0, j, 0))
    n_out = 3 if with_q else 2
    return pl.pallas_call(
        functools.partial(_proj1_kernel, rope=rope, with_q=with_q),
        out_shape=(_sds((b, DIF_HEADS, n, LANES), BF16),) * n_out,
        grid=(b, n // tm),
        in_specs=in_specs,
        out_specs=(hm,) * n_out,
        compiler_params=_cparams("parallel", "parallel"),
        name="proj1_qkv" if with_q else "proj1_kv_ctx",
    )(*args)


def _diff_kernel(lam_ref, q_ref, kl_ref, kc_ref, vl_ref, vc_ref, g_ref, o_ref, k_scr, v_scr, *, tq, n_lat, out_scale):
    @pl.when(pl.program_id(2) == 0)
    def _():
        k_scr[0:n_lat] = kl_ref[0, 0]
        k_scr[n_lat:] = kc_ref[0, 0]
        v_scr[0:n_lat, 0:LANES] = vl_ref[0, 0]
        v_scr[n_lat:, 0:LANES] = vc_ref[0, 0]
        v_scr[:, LANES:] = jnp.ones((v_scr.shape[0], LANES), BF16)

    lo = lax.broadcasted_iota(jnp.int32, (DIF_CHAIN_ROWS, LANES), 1) < HALF
    for r0 in range(0, tq, DIF_CHAIN_ROWS):
        q = q_ref[0, 0, r0:r0 + DIF_CHAIN_ROWS, :].astype(F32)
        maps = []
        for qm in (jnp.where(lo, q, 0.0), jnp.where(lo, 0.0, q)):
            s = _nt_dot(qm.astype(BF16), k_scr[...])
            p = jnp.exp2(s - jnp.max(s, -1, keepdims=True)).astype(BF16)
            oe = jnp.dot(p, v_scr[...], preferred_element_type=F32)
            maps.append(oe[:, 0:LANES] * (1.0 / oe[:, LANES:]))
        o = maps[0] - lam_ref[0] * maps[1]
        o = o * lax.rsqrt(jnp.mean(o * o, -1, keepdims=True) + RMS_EPS) * g_ref[...]
        o_ref[0, r0:r0 + DIF_CHAIN_ROWS, :] = (o * out_scale).astype(BF16)


def _diff_attention(lam, q, kl, kc, vl, vc, subln_g, lam_init, tq):
    b, nh, n, _ = q.shape
    nc = kc.shape[2]
    kv = lambda m: pl.BlockSpec((1, 1, m, LANES), lambda i, h, j: (i, h, 0, 0))
    return pl.pallas_call(
        functools.partial(_diff_kernel, tq=tq, n_lat=n, out_scale=1.0 - lam_init),
        out_shape=_sds((b, n, nh * LANES), BF16),
        grid=(b, nh, n // tq),
        in_specs=[pl.BlockSpec(memory_space=pltpu.SMEM),
                  pl.BlockSpec((1, 1, tq, LANES), lambda i, h, j: (i, h, j, 0)),
                  kv(n), kv(nc), kv(n), kv(nc),
                  pl.BlockSpec((1, LANES), lambda i, h, j: (0, 0))],
        out_specs=pl.BlockSpec((1, tq, LANES), lambda i, h, j: (i, j, h)),
        scratch_shapes=[pltpu.VMEM((n + nc, LANES), BF16), pltpu.VMEM((n + nc, 2 * LANES), BF16)],
        compiler_params=_cparams("parallel", "parallel", "arbitrary"),
        name="diff_attention",
    )(lam, q, kl, kc, vl, vc, subln_g)


def _moe_block(toks, logit_streams, layer, moe_w1, moe_w3, moe_w2):
    t = sum(tok.shape[0] for tok in toks) // TOKEN_TILE_ROWS
    tb = MOE_DMA_TILE
    ids, wts, rank, counts = _router(logit_streams, tm=512)
    dest, tile_expert, n_valid, pad, n_rows = _moe_plan(ids, rank, counts, t)
    dest3 = dest.reshape(MOE_TOPK, t // tb, tb).transpose(1, 0, 2)
    xs = _dispatch(toks, dest3, pad, n_rows)
    flat = lambda w: w.reshape((-1,) + w.shape[2:])
    ys = _expert_ffn(tile_expert + layer * MOE_EXPERTS, n_valid, xs, flat(moe_w1), flat(moe_w3), flat(moe_w2))
    return ys, dest3, wts.T


def kernel(x, c, ctx, c_ctx, mod_w, mod_b, ln1_g, ln1_b, ln2_g, ln2_b, swa_ssm_w_in, swa_ssm_w_out, swa_sink, ssm_a_re, ssm_a_im, ssm_log_step, ssm_b_re, ssm_b_im, ssm_c_re, ssm_c_im, ssm_d, ssm_glu_w, ssm_glu_b, dif_w_in, dif_w_out, dif_lam_q1, dif_lam_k1, dif_lam_q2, dif_lam_k2, dif_subln_g, moe_wg, moe_bg, moe_we, moe_be, moe_w1, moe_w3, moe_w2):
    bsz, n, d = x.shape
    ctx_len = ctx.shape[1]
    depth = mod_w.shape[0]
    alpha = (2 * depth) ** 0.25
    tabs = _rope_tables(n)

    n_vec = 16
    cvec = jnp.zeros((n_vec, d), F32).at[:bsz].set(c).at[bsz].set(c_ctx)
    mods = _modulation(cvec, mod_w, mod_b)
    mod_table = mods.reshape(depth * n_vec * 6, 1, d)

    xl, xc = x, ctx
    for layer in range(depth):
        need_ctx = layer < depth - 1
        i = layer // 2
        lat = [ModVec(mod_table, (layer * n_vec) * 6 + k, 6) for k in range(6)]
        cx = [ModVec(mod_table, (layer * n_vec + bsz) * 6 + k, 0) for k in range(6)]
        sh1, sc1, g1, sh2, sc2, g2 = lat
        csh1, csc1, cg1, csh2, csc2, cg2 = cx
        lg1, lb1 = ln1_g[layer].reshape(1, d), ln1_b[layer].reshape(1, d)
        lg2, lb2 = ln2_g[layer].reshape(1, d), ln2_b[layer].reshape(1, d)
        router_wb = _router_weights(moe_wg[layer], moe_bg[layer], moe_we[layer], moe_be[layer])
        if layer % 2 == 0:
            w_in = swa_ssm_w_in[i].astype(BF16)
            w_out = swa_ssm_w_out[i].astype(BF16)
            q, k, v, u = _proj0(xl, sc1, sh1, w_in, tabs, tm=512)
            qc, kc, vc, uc = _proj0(xc, csc1, csh1, w_in, None, tm=ctx_len)
            sink = swa_sink[i].astype(F32)
            att = _swa_attention(sink, q, k, v, kc, vc)
            mats = _ssm_matrices(ssm_a_re[i], ssm_a_im[i], ssm_log_step[i], ssm_b_re[i], ssm_b_im[i],
                                 ssm_c_re[i], ssm_c_im[i])
            ys, ysc = _ssm_scan(u, uc, mats)
            glu = (ssm_d[i].reshape(1, SSM_WIDTH).astype(F32), ssm_glu_w[i].astype(BF16),
                   ssm_glu_b[i].reshape(1, SSM_WIDTH).astype(F32))
            x1, tok, lgt = _post(att, (ys, u) + glu, w_out, xl, g1, lg1, lb1, sc2, sh2, router_wb, alpha, 512)
            toks, logit_streams = [tok], [lgt]
            if need_ctx:
                att_c = _ctx_attention(sink, qc, kc, vc)
                xc1, tok_c, lgt_c = _post(att_c, (ysc, uc) + glu, w_out, xc, cg1, lg1, lb1, csc2, csh2, router_wb,
                                          alpha, ctx_len)
                toks.append(tok_c)
                logit_streams.append(lgt_c)
        else:
            lam_init = 0.8 - 0.6 * math.exp(-0.3 * layer)
            w_in = dif_w_in[i].astype(BF16)
            w_out = dif_w_out[i].astype(BF16)
            q, k, v = _proj1(xl, sc1, sh1, w_in, tabs, True, tm=512)
            kc, vc = _proj1(xc, csc1, csh1, w_in[:, DIF_QK_W:], None, False, tm=ctx_len)
            lam = (jnp.exp(jnp.sum(dif_lam_q1[i].astype(F32) * dif_lam_k1[i].astype(F32)))
                   - jnp.exp(jnp.sum(dif_lam_q2[i].astype(F32) * dif_lam_k2[i].astype(F32))) + lam_init).reshape(1)
            att = _diff_attention(lam, q, k, kc, v, vc, dif_subln_g[i].reshape(1, DIF_V_HEAD).astype(F32),
                                  lam_init, tq=n)
            x1, tok, lgt = _post(att, None, w_out, xl, g1, lg1, lb1, sc2, sh2, router_wb, alpha, 512)
            toks, logit_streams = [tok], [lgt]
            if need_ctx:
                raise NotImplementedError("a differential-attention layer followed by another layer")
        ys_moe, dest3, wt = _moe_block(toks, logit_streams, layer, moe_w1, moe_w3, moe_w2)
        n_lat_blk = bsz * n // MOE_DMA_TILE
        xl = _combine_ln2(dest3[:n_lat_blk], x1.reshape(-1, d), ys_moe, wt[:bsz * n], g2, lg2, lb2, alpha,
                          n).reshape(bsz, n, d)
        if need_ctx:
            xc = _combine_ln2(dest3[n_lat_blk:], xc1.reshape(-1, d), ys_moe, wt[bsz * n:], cg2, lg2, lb2, alpha,
                              ctx_len).reshape(bsz, ctx_len, d)
    return xl
```

```python
import functools
import math
import typing

import jax
import jax.numpy as jnp
from jax import lax
from jax.experimental import pallas as pl
from jax.experimental.pallas import tpu as pltpu

F32 = jnp.float32
BF16 = jnp.bfloat16
HIGHEST = lax.Precision.HIGHEST

D_MODEL = 1024
GRID_W = 64
HEAD_DIM = 64
ROPE_BASE = 10000.0
ROPE_FREQS = HEAD_DIM // 4
LN_EPS = 1e-5
RMS_EPS = 1e-5
NEG_INF = -1e30
LOG2_E = math.log2(math.e)
LANES = 128
HALF = LANES // 2

SWA_HEADS = 8
SWA_KV_HEADS = 2
SWA_WINDOW = 128
SWA_BLOCK = 128
SWA_BLOCKS_PER_STEP = 4
SWA_Q_W = SWA_HEADS * HEAD_DIM
SWA_KV_W = SWA_KV_HEADS * HEAD_DIM

SSM_WIDTH = D_MODEL // 2
SSM_GROUP = 16
SSM_GROUPS = SSM_WIDTH // SSM_GROUP
SSM_STATE = 64
SSM_CHUNK = 16
SSM_CW = SSM_CHUNK * SSM_GROUP
SSM_SLAB_GROUPS = LANES // SSM_GROUP
SSM_GROUPS_PER_STEP = 4

AB_IN_W = SWA_Q_W + 2 * SWA_KV_W + SSM_WIDTH

DIF_HEADS = D_MODEL // (2 * HEAD_DIM)
DIF_QK_W = DIF_HEADS * 2 * HEAD_DIM
DIF_V_HEAD = 2 * HEAD_DIM
DIF_V_W = DIF_HEADS * DIF_V_HEAD
DIF_CHAIN_ROWS = 128

MOE_GROUPS = 4
MOE_EPG = 8
MOE_EXPERTS = MOE_GROUPS * MOE_EPG
MOE_HIDDEN = D_MODEL // 4
MOE_TOPK = 2
SUBLANES = 8
ROUTER_EXPERT_ROW0 = SUBLANES
MOE_ROW_TILE = 512
MOE_DMA_TILE = 512
TOKEN_TILE_ROWS = D_MODEL // LANES

VMEM_LIMIT = 56 * 1024 * 1024


def _cparams(*sem):
    return pltpu.CompilerParams(dimension_semantics=sem, vmem_limit_bytes=VMEM_LIMIT)


def _sds(shape, dtype):
    return jax.ShapeDtypeStruct(shape, dtype)


def _nt_dot(a, b):
    return lax.dot_general(a, b, (((1,), (1,)), ((), ())), preferred_element_type=F32)


def _layer_norm(r, g, b):
    mu = jnp.mean(r, -1, keepdims=True)
    rc = r - mu
    var = jnp.mean(rc * rc, -1, keepdims=True)
    return rc * lax.rsqrt(var + LN_EPS) * g + b


class ModVec(typing.NamedTuple):
    table: jax.Array
    row0: int
    stride: int

    def spec(self, sample_of_step=lambda i, *_: i):
        d = self.table.shape[-1]
        return pl.BlockSpec((1, 1, d), lambda *idx: (self.row0 + self.stride * sample_of_step(*idx), 0, 0))


def _mod_kernel(c_ref, w_ref, b_ref, o_ref):
    cv = c_ref[...]
    s = cv * jax.nn.sigmoid(cv)
    o_ref[0] = jnp.dot(s, w_ref[0], preferred_element_type=F32, precision=HIGHEST) + b_ref[0]


def _modulation(cvec, mod_w, mod_b):
    depth, d, w6 = mod_w.shape
    tn = 1536
    return pl.pallas_call(
        _mod_kernel,
        out_shape=_sds((depth, cvec.shape[0], w6), F32),
        grid=(depth, w6 // tn),
        in_specs=[pl.BlockSpec(cvec.shape, lambda l, j: (0, 0)),
                  pl.BlockSpec((1, d, tn), lambda l, j: (l, 0, j)),
                  pl.BlockSpec((1, 1, tn), lambda l, j: (l, 0, j))],
        out_specs=pl.BlockSpec((1, cvec.shape[0], tn), lambda l, j: (l, 0, j)),
        compiler_params=_cparams("arbitrary", "arbitrary"),
        name="modulation",
    )(cvec, mod_w, mod_b.reshape(depth, 1, w6))


def _rope_tables(n):
    rows = n // GRID_W
    row = jnp.repeat(jnp.arange(rows, dtype=F32), GRID_W)
    col = jnp.tile(jnp.arange(GRID_W, dtype=F32), rows)
    inv = ROPE_BASE ** (-jnp.arange(ROPE_FREQS, dtype=F32) / ROPE_FREQS)
    ang_r = row[:, None] * inv[None, :]
    ang_c = col[:, None] * inv[None, :]
    zeros = jnp.zeros_like(ang_r)
    cos64 = jnp.concatenate([jnp.cos(ang_r), jnp.cos(ang_r), jnp.cos(ang_c), jnp.cos(ang_c)], -1)
    sa64 = jnp.concatenate([-jnp.sin(ang_r), zeros, -jnp.sin(ang_c), zeros], -1)
    sb64 = jnp.concatenate([zeros, jnp.sin(ang_r), zeros, jnp.sin(ang_c)], -1)
    return tuple(jnp.tile(t, (1, LANES // HEAD_DIM)) for t in (cos64, sa64, sb64))


def _rot(t, tabs):
    if tabs is None:
        return t
    cos, sa, sb = tabs
    return t * cos + pltpu.roll(t, LANES - ROPE_FREQS, 1) * sa + pltpu.roll(t, ROPE_FREQS, 1) * sb


def _dup_halves(t):
    lo = lax.broadcasted_iota(jnp.int32, t.shape, 1) < HALF
    ta = jnp.where(lo, t, 0.0)
    tb = t - ta
    return ta + pltpu.roll(ta, HALF, 1), tb + pltpu.roll(tb, HALF, 1)


def _proj0_kernel(*refs, rope):
    if rope:
        x_ref, sc_ref, sh_ref, w_ref, cos_ref, sa_ref, sb_ref, q_ref, k_ref, v_ref, u_ref = refs
        tabs = (cos_ref[...], sa_ref[...], sb_ref[...])
    else:
        x_ref, sc_ref, sh_ref, w_ref, q_ref, k_ref, v_ref, u_ref = refs
        tabs = None
    h = (x_ref[0] * (1.0 + sc_ref[0]) + sh_ref[0]).astype(BF16)
    r = jnp.dot(h, w_ref[...], preferred_element_type=F32)
    scale = LOG2_E * HEAD_DIM ** -0.5
    for s in range(SWA_Q_W // LANES):
        q_ref[0, :, s * LANES:(s + 1) * LANES] = (_rot(r[:, s * LANES:(s + 1) * LANES], tabs) * scale).astype(BF16)
    k0, k1 = _dup_halves(_rot(r[:, SWA_Q_W:SWA_Q_W + LANES], tabs))
    vv = r[:, SWA_Q_W + LANES:SWA_Q_W + 2 * LANES]
    lo = lax.broadcasted_iota(jnp.int32, vv.shape, 1) < HALF
    k_ref[0, 0] = k0.astype(BF16)
    k_ref[0, 1] = k1.astype(BF16)
    v_ref[0, 0] = jnp.where(lo, vv, 1.0).astype(BF16)
    v_ref[0, 1] = jnp.where(lo, pltpu.roll(vv, HALF, 1), 1.0).astype(BF16)
    u_ref[0] = r[:, SWA_Q_W + 2 * LANES:]


def _proj0(x, sc, sh, w_bf16, tabs, tm):
    b, n, d = x.shape
    rope = tabs is not None
    in_specs = [pl.BlockSpec((1, tm, d), lambda i, j: (i, j, 0)), sc.spec(), sh.spec(),
                pl.BlockSpec(w_bf16.shape, lambda i, j: (0, 0))]
    args = [x, sc.table, sh.table, w_bf16]
    if rope:
        in_specs += [pl.BlockSpec((tm, LANES), lambda i, j: (j, 0))] * 3
        args += list(tabs)
    kv_spec = pl.BlockSpec((1, SWA_KV_HEADS, tm, LANES), lambda i, j: (i, 0, j, 0))
    return pl.pallas_call(
        functools.partial(_proj0_kernel, rope=rope),
        out_shape=(_sds((b, n, SWA_Q_W), BF16), _sds((b, SWA_KV_HEADS, n, LANES), BF16),
                   _sds((b, SWA_KV_HEADS, n, LANES), BF16), _sds((b, n, SSM_WIDTH), F32)),
        grid=(b, n // tm),
        in_specs=in_specs,
        out_specs=(pl.BlockSpec((1, tm, SWA_Q_W), lambda i, j: (i, j, 0)), kv_spec, kv_spec,
                   pl.BlockSpec((1, tm, SSM_WIDTH), lambda i, j: (i, j, 0))),
        compiler_params=_cparams("parallel", "parallel"),
        name="proj0_rope" if rope else "proj0_ctx",
    )(*args)


def _swa_kernel(*refs, tq, nsub, local, n_lat):
    if local:
        sink_ref, q_ref, k_ref, v_ref, kc_ref, vc_ref, o_ref = refs
    else:
        sink_ref, q_ref, kc_ref, vc_ref, o_ref = refs
    rows = 4 * tq
    lo = lax.broadcasted_iota(jnp.int32, (tq, LANES), 1) < HALF
    lo4 = lax.broadcasted_iota(jnp.int32, (rows, LANES), 1) < HALF
    rown = lax.broadcasted_iota(jnp.int32, (rows, 1), 0)
    for sub in range(nsub):
        j = pl.program_id(1) * nsub + sub
        r0 = sub * tq
        if local:
            span = 3 * SWA_BLOCK
            start = pl.multiple_of(jnp.clip((j - 1) * SWA_BLOCK, 0, n_lat - span), SWA_BLOCK)
            rr = lax.broadcasted_iota(jnp.int32, (rows, span), 0)
            cc = lax.broadcasted_iota(jnp.int32, (rows, span), 1)
            qpos = j * tq + (rr & (tq - 1))
            mask = jnp.abs(qpos - (start + cc)) <= SWA_WINDOW
        for h in range(SWA_KV_HEADS):
            qa = q_ref[0, r0:r0 + tq, (2 * h) * LANES:(2 * h + 1) * LANES].astype(F32)
            qb = q_ref[0, r0:r0 + tq, (2 * h + 1) * LANES:(2 * h + 2) * LANES].astype(F32)
            q4 = jnp.concatenate([jnp.where(lo, qa, 0.0), jnp.where(lo, 0.0, qa),
                                  jnp.where(lo, qb, 0.0), jnp.where(lo, 0.0, qb)], 0).astype(BF16)
            sink = LOG2_E * jnp.where(rown < tq, sink_ref[4 * h],
                                      jnp.where(rown < 2 * tq, sink_ref[4 * h + 1],
                                                jnp.where(rown < 3 * tq, sink_ref[4 * h + 2], sink_ref[4 * h + 3])))
            s_ctx = _nt_dot(q4, kc_ref[0, h])
            m = jnp.maximum(jnp.max(s_ctx, -1, keepdims=True), sink)
            if local:
                s_loc = jnp.where(mask, _nt_dot(q4, k_ref[0, h, pl.ds(start, span), :]), NEG_INF)
                m = jnp.maximum(m, jnp.max(s_loc, -1, keepdims=True))
            o4 = jnp.dot(jnp.exp2(s_ctx - m).astype(BF16), vc_ref[0, h], preferred_element_type=F32)
            if local:
                o4 = o4 + jnp.dot(jnp.exp2(s_loc - m).astype(BF16), v_ref[0, h, pl.ds(start, span), :],
                                  preferred_element_type=F32)
            o4 = o4 + jnp.where(lo4, 0.0, jnp.exp2(sink - m))
            o4 = o4 * (1.0 / jnp.where(lo4, pltpu.roll(o4, HALF, 1), 1.0))
            for s in range(2):
                even, odd = o4[2 * s * tq:(2 * s + 1) * tq], o4[(2 * s + 1) * tq:(2 * s + 2) * tq]
                o_ref[0, r0:r0 + tq, (2 * h + s) * LANES:(2 * h + s + 1) * LANES] = jnp.where(
                    lo, even, pltpu.roll(odd, HALF, 1)).astype(BF16)


def _swa_attention(sink, q, k, v, kc, vc):
    b, n, _ = q.shape
    nc = kc.shape[2]
    tq, nsub = SWA_BLOCK, SWA_BLOCKS_PER_STEP
    full = lambda m: pl.BlockSpec((1, SWA_KV_HEADS, m, LANES), lambda i, j: (i, 0, 0, 0))
    return pl.pallas_call(
        functools.partial(_swa_kernel, tq=tq, nsub=nsub, local=True, n_lat=n),
        out_shape=_sds((b, n, SWA_Q_W), BF16),
        grid=(b, n // (tq * nsub)),
        in_specs=[pl.BlockSpec(memory_space=pltpu.SMEM),
                  pl.BlockSpec((1, tq * nsub, SWA_Q_W), lambda i, j: (i, j, 0)),
                  full(n), full(n), full(nc), full(nc)],
        out_specs=pl.BlockSpec((1, tq * nsub, SWA_Q_W), lambda i, j: (i, j, 0)),
        compiler_params=_cparams("parallel", "arbitrary"),
        name="swa_attention",
    )(sink, q, k, v, kc, vc)


def _ctx_attention(sink, qc, kc, vc):
    b, nc, _ = qc.shape
    full = pl.BlockSpec((1, SWA_KV_HEADS, nc, LANES), lambda i, j: (i, 0, 0, 0))
    return pl.pallas_call(
        functools.partial(_swa_kernel, tq=nc, nsub=1, local=False, n_lat=0),
        out_shape=_sds((b, nc, SWA_Q_W), BF16),
        grid=(b, 1),
        in_specs=[pl.BlockSpec(memory_space=pltpu.SMEM),
                  pl.BlockSpec((1, nc, SWA_Q_W), lambda i, j: (i, 0, 0)), full, full],
        out_specs=pl.BlockSpec((1, nc, SWA_Q_W), lambda i, j: (i, 0, 0)),
        compiler_params=_cparams("parallel", "arbitrary"),
        name="ctx_attention",
    )(sink, qc, kc, vc)


def _ssm_matrices(a_re, a_im, log_step, b_re, b_im, c_re, c_im):
    L = SSM_CHUNK
    ar = a_re.astype(F32)
    ai = a_im.astype(F32)
    dt = jnp.exp(log_step.astype(F32))[..., None]
    m = jnp.arange(L + 1, dtype=F32)[:, None, None, None]
    mag = jnp.exp(m * (dt * ar)[None])
    pw_re = mag * jnp.cos(m * (dt * ai)[None])
    pw_im = mag * jnp.sin(m * (dt * ai)[None])
    den = ar * ar + ai * ai
    nr = pw_re[1] - 1.0
    coef_re = (nr * ar + pw_im[1] * ai) / den
    coef_im = (pw_im[1] * ar - nr * ai) / den
    br = b_re.astype(F32)
    bi = b_im.astype(F32)
    bb_re = coef_re[..., None] * br - coef_im[..., None] * bi
    bb_im = coef_re[..., None] * bi + coef_im[..., None] * br
    cr = c_re.astype(F32)[None]
    ci = c_im.astype(F32)[None]
    ca_re = cr * pw_re[:, :, :, None, :] - ci * pw_im[:, :, :, None, :]
    ca_im = cr * pw_im[:, :, :, None, :] + ci * pw_re[:, :, :, None, :]
    kern = jnp.sum(ca_re[:L, ..., None] * bb_re[None, :, :, None] - ca_im[:L, ..., None] * bb_im[None, :, :, None],
                   axis=4)
    G, C = SSM_GROUPS, SSM_GROUP
    k_f = jnp.transpose(kern[:, 0], (1, 3, 0, 2)).reshape(G, C, SSM_CW)
    k_r = jnp.transpose(kern[::-1, 1], (1, 3, 0, 2)).reshape(G, C, SSM_CW)
    span = (2 * L - 1) * C
    lagged = (jnp.pad(k_f, ((0, 0), (0, 0), ((L - 1) * C, 0)))
              + jnp.pad(k_r, ((0, 0), (0, 0), (0, (L - 1) * C))))
    wide = jnp.pad(lagged, ((0, 0), (0, 0), (0, C))).astype(BF16)
    flat = jnp.broadcast_to(wide[:, :, None, :], (G, C, L, span + C)).reshape(G, C, L * (span + C))
    skew = flat[:, :, (L - 1) * C:(L - 1) * C + L * span].reshape(G, C, L, span)[..., :SSM_CW]
    m_intra = jnp.transpose(skew, (0, 2, 1, 3)).reshape(G, SSM_CW, SSM_CW)
    both = lambda t_f, t_r: jnp.concatenate([t_f, t_r], -1)
    pin_re = jnp.transpose(both(pw_re[:L, 0][::-1], pw_re[:L, 1]), (1, 0, 2))[:, :, None, :]
    pin_im = jnp.transpose(both(pw_im[:L, 0][::-1], pw_im[:L, 1]), (1, 0, 2))[:, :, None, :]
    bt_re = both(jnp.transpose(bb_re[0], (0, 2, 1)), jnp.transpose(bb_re[1], (0, 2, 1)))[:, None]
    bt_im = both(jnp.transpose(bb_im[0], (0, 2, 1)), jnp.transpose(bb_im[1], (0, 2, 1)))[:, None]
    m_in = jnp.concatenate([pin_re * bt_re - pin_im * bt_im, pin_re * bt_im + pin_im * bt_re],
                           -1).reshape(G, SSM_CW, SSM_CW)
    pst_re = jnp.transpose(both(pw_re[1:, 0], pw_re[1:, 1][::-1]), (1, 0, 2))[:, :, None, :]
    pst_im = jnp.transpose(both(pw_im[1:, 0], pw_im[1:, 1][::-1]), (1, 0, 2))[:, :, None, :]
    ct_re = both(c_re[0].astype(F32), c_re[1].astype(F32))[:, None]
    ct_im = both(c_im[0].astype(F32), c_im[1].astype(F32))[:, None]
    m_state_t = jnp.concatenate([pst_re * ct_re - pst_im * ct_im, -(pst_re * ct_im + pst_im * ct_re)],
                                -1).reshape(G, SSM_CW, SSM_CW)
    a_l = jnp.stack([both(pw_re[L, 0], pw_re[L, 1]), both(pw_im[L, 0], pw_im[L, 1])], 1)
    return m_in.astype(BF16), m_intra.astype(BF16), m_state_t.astype(BF16), a_l


def _ssm_pack_kernel(uc_ref, u_ref, x_ref, *, nc_ctx, nc_lat):
    for src, row0, nch in ((uc_ref, 0, nc_ctx), (u_ref, nc_ctx, nc_lat)):
        steps = [src[0, pl.ds(j, nch, stride=SSM_CHUNK), :] for j in range(SSM_CHUNK)]
        lane = lax.broadcasted_iota(jnp.int32, (nch, LANES), 1)
        piece = [(lane >= jj * SSM_GROUP) & (lane < (jj + 1) * SSM_GROUP) for jj in range(SSM_SLAB_GROUPS)]
        for g in range(SSM_SLAB_GROUPS):
            for h in range(SSM_CW // LANES):
                acc = jnp.zeros((nch, LANES), F32)
                for jj in range(SSM_SLAB_GROUPS):
                    z = steps[h * SSM_SLAB_GROUPS + jj]
                    shift = ((jj - g) * SSM_GROUP) % LANES
                    acc = jnp.where(piece[jj], z if shift == 0 else pltpu.roll(z, shift, 1), acc)
                x_ref[g, row0:row0 + nch, h * LANES:(h + 1) * LANES] = acc.astype(BF16)


def _ssm_unpack_kernel(y_ref, oc_ref, o_ref, *, nc_ctx, nc_lat):
    for dst, row0, nch in ((oc_ref, 0, nc_ctx), (o_ref, nc_ctx, nc_lat)):
        lane = lax.broadcasted_iota(jnp.int32, (nch, LANES), 1)
        piece = [(lane >= g * SSM_GROUP) & (lane < (g + 1) * SSM_GROUP) for g in range(SSM_SLAB_GROUPS)]
        for i in range(SSM_CHUNK):
            h, ii = divmod(i, SSM_SLAB_GROUPS)
            acc = jnp.zeros((nch, LANES), F32)
            for g in range(SSM_SLAB_GROUPS):
                z = y_ref[g, row0:row0 + nch, h * LANES:(h + 1) * LANES]
                shift = ((g - ii) * SSM_GROUP) % LANES
                acc = jnp.where(piece[g], z if shift == 0 else pltpu.roll(z, shift, 1), acc)
            dst[0, pl.ds(i, nch, stride=SSM_CHUNK), :] = acc


def _ssm_kernel(x_ref, min_ref, mintra_ref, mstate_ref, al_ref, y_ref, v_scr, s_scr, *, nb, nc_ctx, n_chunks):
    n_grp = x_ref.shape[0]
    for g in range(n_grp):
        v = jnp.dot(x_ref[g], min_ref[g], preferred_element_type=F32)
        v_scr[g, 0] = v[:, 0:LANES]
        v_scr[g, 1] = v[:, LANES:]
    ar = [al_ref[g, 0:1, :] for g in range(n_grp)]
    ai = [al_ref[g, 1:2, :] for g in range(n_grp)]
    lo = lax.broadcasted_iota(jnp.int32, (nb, LANES), 1) < HALF

    def body(k, carry):
        kr = jnp.where(k < nc_ctx, nc_ctx - 1 - k, n_chunks - 1 + nc_ctx - k)
        rf = pl.ds(k, nb, stride=n_chunks)
        rr = pl.ds(kr, nb, stride=n_chunks)
        out = []
        for g in range(n_grp):
            sre, sim = carry[2 * g], carry[2 * g + 1]
            s_scr[g, 0, rf, :] = sre
            s_scr[g, 1, rr, :] = sre
            s_scr[g, 2, rf, :] = sim
            s_scr[g, 3, rr, :] = sim
            vre = jnp.where(lo, v_scr[g, 0, rf, :], v_scr[g, 0, rr, :])
            vim = jnp.where(lo, v_scr[g, 1, rf, :], v_scr[g, 1, rr, :])
            out += [ar[g] * sre - ai[g] * sim + vre, ar[g] * sim + ai[g] * sre + vim]
        return tuple(out)

    zero = jnp.zeros((nb, LANES), F32)
    lax.fori_loop(0, n_chunks, body, (zero,) * (2 * n_grp))
    lo_all = lax.broadcasted_iota(jnp.int32, (s_scr.shape[2], LANES), 1) < HALF
    for g in range(n_grp):
        s_in = jnp.concatenate([jnp.where(lo_all, s_scr[g, 0], s_scr[g, 1]),
                                jnp.where(lo_all, s_scr[g, 2], s_scr[g, 3])], axis=1).astype(BF16)
        y_ref[g] = jnp.dot(x_ref[g], mintra_ref[g], preferred_element_type=F32) + _nt_dot(s_in, mstate_ref[g])


def _ssm_scan(u, uc, mats):
    m_in, m_intra, m_state, a_l = mats
    b, n, _ = u.shape
    nc = uc.shape[1]
    nc_ctx, nc_lat = nc // SSM_CHUNK, n // SSM_CHUNK
    n_chunks = nc_ctx + nc_lat
    r = n_chunks * b
    nat = lambda m: pl.BlockSpec((1, m, LANES), lambda i, s: (i, 0, s))
    grp = pl.BlockSpec((SSM_SLAB_GROUPS, n_chunks, SSM_CW), lambda i, s: (s, i, 0))
    xg = pl.pallas_call(
        functools.partial(_ssm_pack_kernel, nc_ctx=nc_ctx, nc_lat=nc_lat),
        out_shape=_sds((SSM_GROUPS, r, SSM_CW), BF16),
        grid=(b, SSM_GROUPS // SSM_SLAB_GROUPS),
        in_specs=[nat(nc), nat(n)],
        out_specs=grp,
        compiler_params=_cparams("parallel", "parallel"),
        name="ssm_pack",
    )(uc, u)
    gs = SSM_GROUPS_PER_STEP
    mat = pl.BlockSpec((gs, SSM_CW, SSM_CW), lambda g: (g, 0, 0))
    yg = pl.pallas_call(
        functools.partial(_ssm_kernel, nb=b, nc_ctx=nc_ctx, n_chunks=n_chunks),
        out_shape=_sds((SSM_GROUPS, r, SSM_CW), F32),
        grid=(SSM_GROUPS // gs,),
        in_specs=[pl.BlockSpec((gs, r, SSM_CW), lambda g: (g, 0, 0)), mat, mat, mat,
                  pl.BlockSpec((gs, 2, LANES), lambda g: (g, 0, 0))],
        out_specs=pl.BlockSpec((gs, r, SSM_CW), lambda g: (g, 0, 0)),
        scratch_shapes=[pltpu.VMEM((gs, 2, r, LANES), F32), pltpu.VMEM((gs, 4, r, LANES), F32)],
        compiler_params=_cparams("parallel"),
        name="ssm_scan",
    )(xg, m_in, m_intra, m_state, a_l)
    ysc, ys = pl.pallas_call(
        functools.partial(_ssm_unpack_kernel, nc_ctx=nc_ctx, nc_lat=nc_lat),
        out_shape=(_sds((b, nc, SSM_WIDTH), F32), _sds((b, n, SSM_WIDTH), F32)),
        grid=(b, SSM_GROUPS // SSM_SLAB_GROUPS),
        in_specs=[grp],
        out_specs=(nat(nc), nat(n)),
        compiler_params=_cparams("parallel", "parallel"),
        name="ssm_unpack",
    )(yg)
    return ys, ysc


def _split_rows(ref, val):
    m = val.shape[0]
    for s in range(TOKEN_TILE_ROWS):
        ref[pl.ds(s, m, stride=TOKEN_TILE_ROWS), :] = val[:, s * LANES:(s + 1) * LANES]


def _merge_rows(ref):
    m = ref.shape[0] // TOKEN_TILE_ROWS
    return jnp.concatenate([ref[pl.ds(s, m, stride=TOKEN_TILE_ROWS), :] for s in range(TOKEN_TILE_ROWS)], axis=1)


def _token_tile(ref, t):
    return ref.at[pl.ds(pl.multiple_of(t * TOKEN_TILE_ROWS, TOKEN_TILE_ROWS), TOKEN_TILE_ROWS)]


def _router_logits(w, bias, h):
    w_hi, h_hi = w.astype(BF16), h.astype(BF16)
    w_lo, h_lo = (w - w_hi.astype(F32)).astype(BF16), (h - h_hi.astype(F32)).astype(BF16)
    return _nt_dot(w_hi, h_hi) + (_nt_dot(w_hi, h_lo) + _nt_dot(w_lo, h_hi)) + bias


def _post_kernel(*refs, alpha, with_ssm):
    wr_ref, br_ref = refs[-5:-3]
    refs = refs[:-5] + refs[-3:]
    if with_ssm:
        (att_ref, ys_ref, u_ref, dsk_ref, gw_ref, gb_ref, wo_ref, x_ref, g1_ref, lg_ref, lb_ref, sc2_ref, sh2_ref,
         x1_ref, h2_ref, lgt_ref) = refs
        y = ys_ref[0] + u_ref[0] * dsk_ref[...]
        gl = jax.nn.gelu(y)
        gate = jax.nn.sigmoid(jnp.dot(gl.astype(BF16), gw_ref[...], preferred_element_type=F32) + gb_ref[...])
        ssm = (gl * gate).astype(BF16)
        mix = (jnp.dot(att_ref[0], wo_ref[0:SWA_Q_W, :], preferred_element_type=F32)
               + jnp.dot(ssm, wo_ref[SWA_Q_W:, :], preferred_element_type=F32))
    else:
        att_ref, wo_ref, x_ref, g1_ref, lg_ref, lb_ref, sc2_ref, sh2_ref, x1_ref, h2_ref, lgt_ref = refs
        mix = jnp.dot(att_ref[0], wo_ref[...], preferred_element_type=F32)
    x1 = _layer_norm(alpha * x_ref[0] + g1_ref[0] * mix, lg_ref[...], lb_ref[...])
    x1_ref[0] = x1
    h2 = x1 * (1.0 + sc2_ref[0]) + sh2_ref[0]
    _split_rows(h2_ref, h2)
    lgt_ref[...] = _router_logits(wr_ref[...], br_ref[...], h2)


def _post(att, ssm_args, w_out_bf16, x, g1, ln_g, ln_b, sc2, sh2, router_wb, alpha, tm):
    b, n, d = x.shape
    tok = lambda w: pl.BlockSpec((1, tm, w), lambda i, j: (i, j, 0))
    const = lambda a: pl.BlockSpec(a.shape, lambda i, j: (0,) * a.ndim)
    in_specs = [tok(att.shape[-1])]
    args = [att]
    if ssm_args is not None:
        ys, u, dsk, gw, gb = ssm_args
        in_specs += [tok(SSM_WIDTH), tok(SSM_WIDTH), const(dsk), const(gw), const(gb)]
        args += [ys, u, dsk, gw, gb]
    in_specs += [const(w_out_bf16), tok(d), g1.spec(), const(ln_g), const(ln_b), sc2.spec(), sh2.spec()]
    args += [w_out_bf16, x, g1.table, ln_g, ln_b, sc2.table, sh2.table]
    in_specs += [const(a) for a in router_wb]
    args += list(router_wb)
    per_b = n // tm
    n_logit = router_wb[0].shape[0]
    return pl.pallas_call(
        functools.partial(_post_kernel, alpha=alpha, with_ssm=ssm_args is not None),
        out_shape=(_sds((b, n, d), F32), _sds((b * n * TOKEN_TILE_ROWS, LANES), F32), _sds((n_logit, b * n), F32)),
        grid=(b, n // tm),
        in_specs=in_specs,
        out_specs=(tok(d), pl.BlockSpec((tm * TOKEN_TILE_ROWS, LANES), lambda i, j: (i * per_b + j, 0)),
                   pl.BlockSpec((n_logit, tm), lambda i, j: (0, i * per_b + j))),
        compiler_params=_cparams("parallel", "parallel"),
        name="post_mixer_ssm" if ssm_args is not None else "post_mixer",
    )(*args)


def _first_max(v, sub):
    m = jnp.max(v, 0, keepdims=True)
    idx = jnp.min(jnp.where(v == m, sub, float(SUBLANES)), 0, keepdims=True)
    return m, idx


def _stream_blocks(toks, tb):
    starts = [0]
    for t in toks:
        starts.append(starts[-1] + t.shape[0] // (tb * TOKEN_TILE_ROWS))
    return starts


def _stream_spec(tok, tb, start):
    last = tok.shape[0] // (tb * TOKEN_TILE_ROWS) - 1
    return pl.BlockSpec((tb * TOKEN_TILE_ROWS, LANES), lambda i, *_: (jnp.clip(i - start, 0, last), 0))


def _router_kernel(*refs, starts):
    n_streams = len(starts) - 1
    lgt_refs = refs[:n_streams]
    ids_ref, wts_ref, rank_ref, cnt_ref, carry_scr = refs[n_streams:]
    step = pl.program_id(0)

    @pl.when(step == 0)
    def _():
        carry_scr[...] = jnp.zeros_like(carry_scr)

    logits = lgt_refs[0][...]
    for ref, start in zip(lgt_refs[1:], starts[1:]):
        logits = jnp.where(step >= start, ref[...], logits)
    tm = logits.shape[1]
    sub = lax.broadcasted_iota(jnp.int32, (SUBLANES, tm), 0).astype(F32)
    gl = logits[0:SUBLANES]
    gmax, gi = _first_max(gl, sub)
    gp = 1.0 / jnp.sum(jnp.exp(gl - gmax), 0, keepdims=True)
    le = logits[ROUTER_EXPERT_ROW0:ROUTER_EXPERT_ROW0 + MOE_EPG]
    for g in range(1, MOE_GROUPS):
        le = jnp.where(gi == float(g), logits[ROUTER_EXPERT_ROW0 + g * MOE_EPG:ROUTER_EXPERT_ROW0 + (g + 1) * MOE_EPG], le)
    m1, i1 = _first_max(le, sub)
    m2, i2 = _first_max(jnp.where(sub == i1, NEG_INF, le), sub)
    t = jnp.exp(m2 - m1)
    e1 = gi * float(MOE_EPG) + i1
    e2 = gi * float(MOE_EPG) + i2
    ids_ref[0:1, :] = e1.astype(jnp.int32)
    ids_ref[1:2, :] = e2.astype(jnp.int32)
    wts_ref[0:1, :] = gp / (1.0 + t)
    wts_ref[1:2, :] = gp * t / (1.0 + t)
    esub = lax.broadcasted_iota(jnp.int32, (MOE_EXPERTS, tm), 0).astype(F32)
    oh1 = (esub == e1).astype(F32)
    oh2 = (esub == e2).astype(F32)
    both = oh1 + oh2
    earlier = (lax.broadcasted_iota(jnp.int32, (tm, tm), 0) < lax.broadcasted_iota(jnp.int32, (tm, tm), 1))
    prefix = jnp.dot(both.astype(BF16), earlier.astype(BF16), preferred_element_type=F32) + carry_scr[...]
    rank_ref[0:1, :] = jnp.sum(oh1 * prefix, 0, keepdims=True).astype(jnp.int32)
    rank_ref[1:2, :] = jnp.sum(oh2 * prefix, 0, keepdims=True).astype(jnp.int32)
    carry_scr[...] += jnp.sum(both, 1, keepdims=True)
    cnt_ref[...] = jnp.broadcast_to(carry_scr[...], cnt_ref.shape)


def _router_weights(wg, bg, we, be):
    d = wg.shape[0]
    rows = ROUTER_EXPERT_ROW0 + MOE_EXPERTS
    w = jnp.zeros((rows, d), F32)
    w = w.at[:MOE_GROUPS].set(wg.T)
    w = w.at[ROUTER_EXPERT_ROW0:].set(jnp.transpose(we, (0, 2, 1)).reshape(MOE_EXPERTS, d))
    bias = jnp.full((rows, 1), NEG_INF, F32)
    bias = bias.at[:MOE_GROUPS, 0].set(bg)
    bias = bias.at[ROUTER_EXPERT_ROW0:, 0].set(be.reshape(-1))
    return w, bias


def _router(logit_streams, tm):
    starts = [0]
    for lg in logit_streams:
        starts.append(starts[-1] + lg.shape[1] // tm)
    t = starts[-1] * tm
    rows = logit_streams[0].shape[0]
    stream_spec = lambda lg, s0: pl.BlockSpec(
        (rows, tm), lambda i: (0, jnp.clip(i - s0, 0, lg.shape[1] // tm - 1)))
    pair = pl.BlockSpec((MOE_TOPK, tm), lambda i: (0, i))
    ids, wts, rank, cnt = pl.pallas_call(
        functools.partial(_router_kernel, starts=starts),
        out_shape=(_sds((MOE_TOPK, t), jnp.int32), _sds((MOE_TOPK, t), F32), _sds((MOE_TOPK, t), jnp.int32),
                   _sds((MOE_EXPERTS, LANES), F32)),
        grid=(t // tm,),
        in_specs=[stream_spec(lg, s0) for lg, s0 in zip(logit_streams, starts)],
        out_specs=(pair, pair, pair, pl.BlockSpec((MOE_EXPERTS, LANES), lambda i: (0, 0))),
        scratch_shapes=[pltpu.VMEM((MOE_EXPERTS, 1), F32)],
        compiler_params=_cparams("arbitrary"),
        name="moe_router",
    )(*logit_streams)
    return ids, wts, rank, cnt[:, 0].astype(jnp.int32)


def _moe_plan(ids, rank, counts, n_tok):
    tm = MOE_ROW_TILE
    padded = ((counts + tm - 1) // tm) * tm
    ends = jnp.cumsum(padded)
    offs = ends - padded
    experts = jnp.arange(MOE_EXPERTS, dtype=jnp.int32)
    dest = (jnp.sum(jnp.where(ids[..., None] == experts, offs, 0), -1) + rank).astype(jnp.int32)
    n_tiles = (MOE_TOPK * n_tok + MOE_EXPERTS * (tm - 1)) // tm
    starts = jnp.arange(n_tiles, dtype=jnp.int32) * tm
    tile_expert = jnp.minimum(jnp.sum((ends[None, :] <= starts[:, None]).astype(jnp.int32), -1), MOE_EXPERTS - 1)
    n_valid = (ends[-1] // tm).astype(jnp.int32).reshape(1)
    pad = jnp.stack([offs + counts, padded - counts, jnp.broadcast_to(n_valid, counts.shape)]).astype(jnp.int32)
    return dest, tile_expert, n_valid, pad, n_tiles * tm


def _pad_fill(pad_ref, zero_scr, xs_ref, sem, wait):
    def per_expert(e, carry):
        first, count = pad_ref[0, e], pad_ref[1, e]
        piece = MOE_ROW_TILE // 2
        while piece >= 1:
            row = first + (count & ~(2 * piece - 1))
            n = piece * TOKEN_TILE_ROWS

            @pl.when((count & piece) != 0)
            def _(row=row, n=n):
                copy = pltpu.make_async_copy(
                    zero_scr.at[pl.ds(0, n)],
                    xs_ref.at[pl.ds(pl.multiple_of(row * TOKEN_TILE_ROWS, TOKEN_TILE_ROWS), n)], sem)
                copy.wait() if wait else copy.start()

            piece //= 2
        return carry

    lax.fori_loop(0, MOE_EXPERTS, per_expert, 0)
    half = MOE_ROW_TILE // 2 * TOKEN_TILE_ROWS
    n_tiles = xs_ref.shape[0] // (2 * half)

    def per_tile(t, carry):
        for h in range(2):
            copy = pltpu.make_async_copy(zero_scr, xs_ref.at[pl.ds(pl.multiple_of((2 * t + h) * half, half), half)],
                                         sem)
            copy.wait() if wait else copy.start()
        return carry

    lax.fori_loop(pad_ref[2, 0], n_tiles, per_tile, 0)


def _dispatch_kernel(*refs, tb, starts):
    n_streams = len(starts) - 1
    dest_ref, pad_ref = refs[:2]
    tok_refs = refs[2:2 + n_streams]
    xs_ref, zero_scr, sem, pad_sem = refs[2 + n_streams:]
    step = pl.program_id(0)

    @pl.when(step == 0)
    def _():
        zero_scr[...] = jnp.zeros_like(zero_scr)
        _pad_fill(pad_ref, zero_scr, xs_ref, pad_sem, wait=False)

    for s, tok_ref in enumerate(tok_refs):
        @pl.when((step >= starts[s]) & (step < starts[s + 1]))
        def _(tok_ref=tok_ref):
            def body(r, carry):
                for k in range(MOE_TOPK):
                    pltpu.make_async_copy(_token_tile(tok_ref, r), _token_tile(xs_ref, dest_ref[MOE_TOPK * r + k]),
                                          sem).start(priority=k % 2)
                return carry

            lax.fori_loop(0, tb, body, 0, unroll=8)

    for k in range(MOE_TOPK):
        pltpu.make_async_copy(tok_refs[0], xs_ref.at[pl.ds(0, tb * TOKEN_TILE_ROWS)], sem).wait()

    @pl.when(step == 0)
    def _():
        _pad_fill(pad_ref, zero_scr, xs_ref, pad_sem, wait=True)


def _dispatch(toks, dest, pad, n_rows):
    tb = MOE_DMA_TILE
    nblk = dest.shape[0] // (MOE_TOPK * tb)
    starts = _stream_blocks(toks, tb)
    return pl.pallas_call(
        functools.partial(_dispatch_kernel, tb=tb, starts=starts),
        out_shape=_sds((n_rows * TOKEN_TILE_ROWS, LANES), F32),
        grid=(nblk,),
        in_specs=[pl.BlockSpec((MOE_TOPK * tb,), lambda i: (i,), memory_space=pltpu.SMEM),
                  pl.BlockSpec(memory_space=pltpu.SMEM)]
        + [_stream_spec(tok, tb, s0) for tok, s0 in zip(toks, starts)],
        out_specs=pl.BlockSpec(memory_space=pl.ANY),
        scratch_shapes=[pltpu.VMEM((MOE_ROW_TILE // 2 * TOKEN_TILE_ROWS, LANES), F32),
                        pltpu.SemaphoreType.DMA(()), pltpu.SemaphoreType.DMA(())],
        compiler_params=_cparams("arbitrary"),
        name="moe_dispatch",
    )(dest, pad, *toks)


def _ffn_kernel(te_ref, nv_ref, x_ref, w1_ref, w3_ref, w2_ref, y_ref, w13_scr, w2_scr):
    i = pl.program_id(0)
    f = w1_ref.shape[2]

    @pl.when((i == 0) | (te_ref[i] != te_ref[jnp.maximum(i - 1, 0)]))
    def _():
        w13_scr[:, 0:f] = w1_ref[0].astype(BF16)
        w13_scr[:, f:2 * f] = w3_ref[0].astype(BF16)
        w2_scr[...] = w2_ref[0].astype(BF16)

    @pl.when(i < nv_ref[0])
    def _():
        h13 = jnp.dot(_merge_rows(x_ref).astype(BF16), w13_scr[...], preferred_element_type=F32)
        h1 = h13[:, 0:f]
        hh = (h1 * jax.nn.sigmoid(h1) * h13[:, f:2 * f]).astype(BF16)
        _split_rows(y_ref, jnp.dot(hh, w2_scr[...], preferred_element_type=F32))

    @pl.when(i >= nv_ref[0])
    def _():
        y_ref[...] = jnp.zeros_like(y_ref)


def _expert_ffn(tile_expert, n_valid, xs, w1, w3, w2):
    p = xs.shape[0] // TOKEN_TILE_ROWS
    _, d, f = w1.shape
    tm = MOE_ROW_TILE
    rows = pl.BlockSpec((tm * TOKEN_TILE_ROWS, LANES), lambda i, te, nv: (i, 0))
    rows_in = pl.BlockSpec((tm * TOKEN_TILE_ROWS, LANES), lambda i, te, nv: (jnp.minimum(i, nv[0] - 1), 0))
    return pl.pallas_call(
        _ffn_kernel,
        out_shape=_sds(xs.shape, F32),
        grid_spec=pltpu.PrefetchScalarGridSpec(
            num_scalar_prefetch=2,
            grid=(p // tm,),
            in_specs=[rows_in,
                      pl.BlockSpec((1, d, f), lambda i, te, nv: (te[i], 0, 0)),
                      pl.BlockSpec((1, d, f), lambda i, te, nv: (te[i], 0, 0)),
                      pl.BlockSpec((1, f, d), lambda i, te, nv: (te[i], 0, 0))],
            out_specs=rows,
            scratch_shapes=[pltpu.VMEM((d, 2 * f), BF16), pltpu.VMEM((f, d), BF16)]),
        compiler_params=_cparams("arbitrary"),
        name="moe_expert_ffn",
    )(tile_expert, n_valid, xs, w1, w3, w2)


def _combine_ln2_kernel(dest_ref, next_ref, x1_ref, ys_ref, wt_ref, g2_ref, lg_ref, lb_ref, o_ref, buf, sem, *,
                        alpha, tb, nblk):
    step = pl.program_id(0)

    def gather(d_ref, slot):
        def body(r, carry):
            for k in range(MOE_TOPK):
                pltpu.make_async_copy(_token_tile(ys_ref, d_ref[MOE_TOPK * r + k]), _token_tile(buf.at[slot, k], r),
                                      sem.at[slot, k]).start(priority=k % 2)
            return carry

        lax.fori_loop(0, tb, body, 0, unroll=8)

    @pl.when(step == 0)
    def _():
        gather(dest_ref, 0)

    @pl.when(step + 1 < nblk)
    def _():
        gather(next_ref, (step + 1) % 2)

    slot = step % 2
    for k in range(MOE_TOPK):
        pltpu.make_async_copy(ys_ref.at[pl.ds(0, tb * TOKEN_TILE_ROWS)], buf.at[slot, k], sem.at[slot, k]).wait()
    f = wt_ref[:, 0:1] * _merge_rows(buf.at[slot, 0]) + wt_ref[:, 1:2] * _merge_rows(buf.at[slot, 1])
    o_ref[...] = _layer_norm(alpha * x1_ref[...] + g2_ref[0] * f, lg_ref[...], lb_ref[...])


def _combine_ln2(dest, x1, ys, wt, g2, ln_g, ln_b, alpha, n_per_sample):
    t, d = x1.shape
    tb = MOE_DMA_TILE
    nblk = t // tb
    const = pl.BlockSpec((1, d), lambda i: (0, 0))
    assert g2.stride == 0 or n_per_sample % tb == 0, (n_per_sample, tb)
    return pl.pallas_call(
        functools.partial(_combine_ln2_kernel, alpha=alpha, tb=tb, nblk=nblk),
        out_shape=_sds((t, d), F32),
        grid=(nblk,),
        in_specs=[pl.BlockSpec((MOE_TOPK * tb,), lambda i: (i,), memory_space=pltpu.SMEM),
                  pl.BlockSpec((MOE_TOPK * tb,), lambda i: (jnp.minimum(i + 1, nblk - 1),), memory_space=pltpu.SMEM),
                  pl.BlockSpec((tb, d), lambda i: (i, 0)),
                  pl.BlockSpec(memory_space=pl.ANY),
                  pl.BlockSpec((tb, MOE_TOPK), lambda i: (i, 0)),
                  g2.spec(lambda i: (i * tb) // n_per_sample),
                  const, const],
        out_specs=pl.BlockSpec((tb, d), lambda i: (i, 0)),
        scratch_shapes=[pltpu.VMEM((2, MOE_TOPK, tb * TOKEN_TILE_ROWS, LANES), F32),
                        pltpu.SemaphoreType.DMA((2, MOE_TOPK))],
        compiler_params=_cparams("arbitrary"),
        name="moe_combine_ln2",
    )(dest, dest, x1, ys, wt, g2.table, ln_g, ln_b)


def _proj1_kernel(*refs, rope, with_q):
    x_ref, sc_ref, sh_ref, w_ref = refs[:4]
    refs = refs[4:]
    tabs = None
    if rope:
        tabs = (refs[0][...], refs[1][...], refs[2][...])
        refs = refs[3:]
    h = (x_ref[0] * (1.0 + sc_ref[0]) + sh_ref[0]).astype(BF16)
    r = jnp.dot(h, w_ref[...], preferred_element_type=F32)
    off = 0
    if with_q:
        q_ref, k_ref, v_ref = refs
        for hd in range(DIF_HEADS):
            q_ref[0, hd] = (_rot(r[:, hd * LANES:(hd + 1) * LANES], tabs) * (LOG2_E * HEAD_DIM ** -0.5)).astype(BF16)
        off = DIF_QK_W
    else:
        k_ref, v_ref = refs
    for hd in range(DIF_HEADS):
        k_ref[0, hd] = _rot(r[:, off + hd * LANES:off + (hd + 1) * LANES], tabs).astype(BF16)
        v_ref[0, hd] = r[:, off + DIF_QK_W + hd * LANES:off + DIF_QK_W + (hd + 1) * LANES].astype(BF16)


def _proj1(x, sc, sh, w_bf16, tabs, with_q, tm):
    b, n, d = x.shape
    rope = tabs is not None
    in_specs = [pl.BlockSpec((1, tm, d), lambda i, j: (i, j, 0)), sc.spec(), sh.spec(),
                pl.BlockSpec(w_bf16.shape, lambda i, j: (0, 0))]
    args = [x, sc.table, sh.table, w_bf16]
    if rope:
        in_specs += [pl.BlockSpec((tm, LANES), lambda i, j: (j, 0))] * 3
        args += list(tabs)
    hm = pl.BlockSpec((1, DIF_HEADS, tm, LANES), lambda i, j: (i, 0, j, 0))
    n_out = 3 if with_q else 2
    return pl.pallas_call(
        functools.partial(_proj1_kernel, rope=rope, with_q=with_q),
        out_shape=(_sds((b, DIF_HEADS, n, LANES), BF16),) * n_out,
        grid=(b, n // tm),
        in_specs=in_specs,
        out_specs=(hm,) * n_out,
        compiler_params=_cparams("parallel", "parallel"),
        name="proj1_qkv" if with_q else "proj1_kv_ctx",
    )(*args)


def _diff_kernel(lam_ref, q_ref, kl_ref, kc_ref, vl_ref, vc_ref, g_ref, o_ref, k_scr, v_scr, *, tq, n_lat, out_scale):
    @pl.when(pl.program_id(2) == 0)
    def _():
        k_scr[0:n_lat] = kl_ref[0, 0]
        k_scr[n_lat:] = kc_ref[0, 0]
        v_scr[0:n_lat, 0:LANES] = vl_ref[0, 0]
        v_scr[n_lat:, 0:LANES] = vc_ref[0, 0]
        v_scr[:, LANES:] = jnp.ones((v_scr.shape[0], LANES), BF16)

    lo = lax.broadcasted_iota(jnp.int32, (DIF_CHAIN_ROWS, LANES), 1) < HALF
    for r0 in range(0, tq, DIF_CHAIN_ROWS):
        q = q_ref[0, 0, r0:r0 + DIF_CHAIN_ROWS, :].astype(F32)
        maps = []
        for qm in (jnp.where(lo, q, 0.0), jnp.where(lo, 0.0, q)):
            s = _nt_dot(qm.astype(BF16), k_scr[...])
            p = jnp.exp2(s - jnp.max(s, -1, keepdims=True)).astype(BF16)
            oe = jnp.dot(p, v_scr[...], preferred_element_type=F32)
            maps.append(oe[:, 0:LANES] * (1.0 / oe[:, LANES:]))
        o = maps[0] - lam_ref[0] * maps[1]
        o = o * lax.rsqrt(jnp.mean(o * o, -1, keepdims=True) + RMS_EPS) * g_ref[...]
        o_ref[0, r0:r0 + DIF_CHAIN_ROWS, :] = (o * out_scale).astype(BF16)


def _diff_attention(lam, q, kl, kc, vl, vc, subln_g, lam_init, tq):
    b, nh, n, _ = q.shape
    nc = kc.shape[2]
    kv = lambda m: pl.BlockSpec((1, 1, m, LANES), lambda i, h, j: (i, h, 0, 0))
    return pl.pallas_call(
        functools.partial(_diff_kernel, tq=tq, n_lat=n, out_scale=1.0 - lam_init),
        out_shape=_sds((b, n, nh * LANES), BF16),
        grid=(b, nh, n // tq),
        in_specs=[pl.BlockSpec(memory_space=pltpu.SMEM),
                  pl.BlockSpec((1, 1, tq, LANES), lambda i, h, j: (i, h, j, 0)),
                  kv(n), kv(nc), kv(n), kv(nc),
                  pl.BlockSpec((1, LANES), lambda i, h, j: (0, 0))],
        out_specs=pl.BlockSpec((1, tq, LANES), lambda i, h, j: (i, j, h)),
        scratch_shapes=[pltpu.VMEM((n + nc, LANES), BF16), pltpu.VMEM((n + nc, 2 * LANES), BF16)],
        compiler_params=_cparams("parallel", "parallel", "arbitrary"),
        name="diff_attention",
    )(lam, q, kl, kc, vl, vc, subln_g)


def _moe_block(toks, logit_streams, layer, moe_w1, moe_w3, moe_w2):
    t = sum(tok.shape[0] for tok in toks) // TOKEN_TILE_ROWS
    tb = MOE_DMA_TILE
    ids, wts, rank, counts = _router(logit_streams, tm=512)
    dest, tile_expert, n_valid, pad, n_rows = _moe_plan(ids, rank, counts, t)
    dest = dest.T.reshape(-1)
    xs = _dispatch(toks, dest, pad, n_rows)
    flat = lambda w: w.reshape((-1,) + w.shape[2:])
    ys = _expert_ffn(tile_expert + layer * MOE_EXPERTS, n_valid, xs, flat(moe_w1), flat(moe_w3), flat(moe_w2))
    return ys, dest, wts.T


def kernel(x, c, ctx, c_ctx, mod_w, mod_b, ln1_g, ln1_b, ln2_g, ln2_b, swa_ssm_w_in, swa_ssm_w_out, swa_sink, ssm_a_re, ssm_a_im, ssm_log_step, ssm_b_re, ssm_b_im, ssm_c_re, ssm_c_im, ssm_d, ssm_glu_w, ssm_glu_b, dif_w_in, dif_w_out, dif_lam_q1, dif_lam_k1, dif_lam_q2, dif_lam_k2, dif_subln_g, moe_wg, moe_bg, moe_we, moe_be, moe_w1, moe_w3, moe_w2):
    bsz, n, d = x.shape
    ctx_len = ctx.shape[1]
    depth = mod_w.shape[0]
    alpha = (2 * depth) ** 0.25
    tabs = _rope_tables(n)

    n_vec = 16
    cvec = jnp.zeros((n_vec, d), F32).at[:bsz].set(c).at[bsz].set(c_ctx)
    mods = _modulation(cvec, mod_w, mod_b)
    mod_table = mods.reshape(depth * n_vec * 6, 1, d)

    xl, xc = x, ctx
    for layer in range(depth):
        need_ctx = layer < depth - 1
        i = layer // 2
        lat = [ModVec(mod_table, (layer * n_vec) * 6 + k, 6) for k in range(6)]
        cx = [ModVec(mod_table, (layer * n_vec + bsz) * 6 + k, 0) for k in range(6)]
        sh1, sc1, g1, sh2, sc2, g2 = lat
        csh1, csc1, cg1, csh2, csc2, cg2 = cx
        lg1, lb1 = ln1_g[layer].reshape(1, d), ln1_b[layer].reshape(1, d)
        lg2, lb2 = ln2_g[layer].reshape(1, d), ln2_b[layer].reshape(1, d)
        router_wb = _router_weights(moe_wg[layer], moe_bg[layer], moe_we[layer], moe_be[layer])
        if layer % 2 == 0:
            w_in = swa_ssm_w_in[i].astype(BF16)
            w_out = swa_ssm_w_out[i].astype(BF16)
            q, k, v, u = _proj0(xl, sc1, sh1, w_in, tabs, tm=512)
            qc, kc, vc, uc = _proj0(xc, csc1, csh1, w_in, None, tm=ctx_len)
            sink = swa_sink[i].astype(F32)
            att = _swa_attention(sink, q, k, v, kc, vc)
            mats = _ssm_matrices(ssm_a_re[i], ssm_a_im[i], ssm_log_step[i], ssm_b_re[i], ssm_b_im[i],
                                 ssm_c_re[i], ssm_c_im[i])
            ys, ysc = _ssm_scan(u, uc, mats)
            glu = (ssm_d[i].reshape(1, SSM_WIDTH).astype(F32), ssm_glu_w[i].astype(BF16),
                   ssm_glu_b[i].reshape(1, SSM_WIDTH).astype(F32))
            x1, tok, lgt = _post(att, (ys, u) + glu, w_out, xl, g1, lg1, lb1, sc2, sh2, router_wb, alpha, 512)
            toks, logit_streams = [tok], [lgt]
            if need_ctx:
                att_c = _ctx_attention(sink, qc, kc, vc)
                xc1, tok_c, lgt_c = _post(att_c, (ysc, uc) + glu, w_out, xc, cg1, lg1, lb1, csc2, csh2, router_wb,
                                          alpha, ctx_len)
                toks.append(tok_c)
                logit_streams.append(lgt_c)
        else:
            lam_init = 0.8 - 0.6 * math.exp(-0.3 * layer)
            w_in = dif_w_in[i].astype(BF16)
            w_out = dif_w_out[i].astype(BF16)
            q, k, v = _proj1(xl, sc1, sh1, w_in, tabs, True, tm=512)
            kc, vc = _proj1(xc, csc1, csh1, w_in[:, DIF_QK_W:], None, False, tm=ctx_len)
            lam = (jnp.exp(jnp.sum(dif_lam_q1[i].astype(F32) * dif_lam_k1[i].astype(F32)))
                   - jnp.exp(jnp.sum(dif_lam_q2[i].astype(F32) * dif_lam_k2[i].astype(F32))) + lam_init).reshape(1)
            att = _diff_attention(lam, q, k, kc, v, vc, dif_subln_g[i].reshape(1, DIF_V_HEAD).astype(F32),
                                  lam_init, tq=n)
            x1, tok, lgt = _post(att, None, w_out, xl, g1, lg1, lb1, sc2, sh2, router_wb, alpha, 512)
            toks, logit_streams = [tok], [lgt]
            if need_ctx:
                raise NotImplementedError("a differential-attention layer followed by another layer")
        ys_moe, dest, wt = _moe_block(toks, logit_streams, layer, moe_w1, moe_w3, moe_w2)
        xl = _combine_ln2(dest[:MOE_TOPK * bsz * n], x1.reshape(-1, d), ys_moe, wt[:bsz * n], g2, lg2, lb2, alpha,
                          n).reshape(bsz, n, d)
        if need_ctx:
            xc = _combine_ln2(dest[MOE_TOPK * bsz * n:], xc1.reshape(-1, d), ys_moe, wt[bsz * n:], cg2, lg2, lb2, alpha,
                              ctx_len).reshape(bsz, ctx_len, d)
    return xl
```

```python
import functools
import math
import typing

import jax
import jax.numpy as jnp
from jax import lax
from jax.experimental import pallas as pl
from jax.experimental.pallas import tpu as pltpu

F32 = jnp.float32
BF16 = jnp.bfloat16
HIGHEST = lax.Precision.HIGHEST

D_MODEL = 1024
GRID_W = 64
HEAD_DIM = 64
ROPE_BASE = 10000.0
ROPE_FREQS = HEAD_DIM // 4
LN_EPS = 1e-5
RMS_EPS = 1e-5
NEG_INF = -1e30
LOG2_E = math.log2(math.e)
LANES = 128
HALF = LANES // 2

SWA_HEADS = 8
SWA_KV_HEADS = 2
SWA_WINDOW = 128
SWA_BLOCK = 128
SWA_BLOCKS_PER_STEP = 4
SWA_Q_W = SWA_HEADS * HEAD_DIM
SWA_KV_W = SWA_KV_HEADS * HEAD_DIM

SSM_WIDTH = D_MODEL // 2
SSM_GROUP = 16
SSM_GROUPS = SSM_WIDTH // SSM_GROUP
SSM_STATE = 64
SSM_CHUNK = 16
SSM_CW = SSM_CHUNK * SSM_GROUP
SSM_SLAB_GROUPS = LANES // SSM_GROUP
SSM_GROUPS_PER_STEP = 4

AB_IN_W = SWA_Q_W + 2 * SWA_KV_W + SSM_WIDTH

DIF_HEADS = D_MODEL // (2 * HEAD_DIM)
DIF_QK_W = DIF_HEADS * 2 * HEAD_DIM
DIF_V_HEAD = 2 * HEAD_DIM
DIF_V_W = DIF_HEADS * DIF_V_HEAD
DIF_CHAIN_ROWS = 128

MOE_GROUPS = 4
MOE_EPG = 8
MOE_EXPERTS = MOE_GROUPS * MOE_EPG
MOE_HIDDEN = D_MODEL // 4
MOE_TOPK = 2
SUBLANES = 8
ROUTER_EXPERT_ROW0 = SUBLANES
MOE_ROW_TILE = 512
MOE_DMA_TILE = 512
TOKEN_TILE_ROWS = D_MODEL // LANES

VMEM_LIMIT = 56 * 1024 * 1024


def _cparams(*sem):
    return pltpu.CompilerParams(dimension_semantics=sem, vmem_limit_bytes=VMEM_LIMIT)


def _sds(shape, dtype):
    return jax.ShapeDtypeStruct(shape, dtype)


def _nt_dot(a, b):
    return lax.dot_general(a, b, (((1,), (1,)), ((), ())), preferred_element_type=F32)


def _layer_norm(r, g, b):
    mu = jnp.mean(r, -1, keepdims=True)
    rc = r - mu
    var = jnp.mean(rc * rc, -1, keepdims=True)
    return rc * lax.rsqrt(var + LN_EPS) * g + b


class ModVec(typing.NamedTuple):
    table: jax.Array
    row0: int
    stride: int

    def spec(self, sample_of_step=lambda i, *_: i):
        d = self.table.shape[-1]
        return pl.BlockSpec((1, 1, d), lambda *idx: (self.row0 + self.stride * sample_of_step(*idx), 0, 0))


def _mod_kernel(c_ref, w_ref, b_ref, o_ref):
    cv = c_ref[...]
    s = cv * jax.nn.sigmoid(cv)
    o_ref[0] = jnp.dot(s, w_ref[0], preferred_element_type=F32, precision=HIGHEST) + b_ref[0]


def _modulation(cvec, mod_w, mod_b):
    depth, d, w6 = mod_w.shape
    tn = 1536
    return pl.pallas_call(
        _mod_kernel,
        out_shape=_sds((depth, cvec.shape[0], w6), F32),
        grid=(depth, w6 // tn),
        in_specs=[pl.BlockSpec(cvec.shape, lambda l, j: (0, 0)),
                  pl.BlockSpec((1, d, tn), lambda l, j: (l, 0, j)),
                  pl.BlockSpec((1, 1, tn), lambda l, j: (l, 0, j))],
        out_specs=pl.BlockSpec((1, cvec.shape[0], tn), lambda l, j: (l, 0, j)),
        compiler_params=_cparams("arbitrary", "arbitrary"),
        name="modulation",
    )(cvec, mod_w, mod_b.reshape(depth, 1, w6))


def _rope_tables(n):
    rows = n // GRID_W
    row = jnp.repeat(jnp.arange(rows, dtype=F32), GRID_W)
    col = jnp.tile(jnp.arange(GRID_W, dtype=F32), rows)
    inv = ROPE_BASE ** (-jnp.arange(ROPE_FREQS, dtype=F32) / ROPE_FREQS)
    ang_r = row[:, None] * inv[None, :]
    ang_c = col[:, None] * inv[None, :]
    zeros = jnp.zeros_like(ang_r)
    cos64 = jnp.concatenate([jnp.cos(ang_r), jnp.cos(ang_r), jnp.cos(ang_c), jnp.cos(ang_c)], -1)
    sa64 = jnp.concatenate([-jnp.sin(ang_r), zeros, -jnp.sin(ang_c), zeros], -1)
    sb64 = jnp.concatenate([zeros, jnp.sin(ang_r), zeros, jnp.sin(ang_c)], -1)
    return tuple(jnp.tile(t, (1, LANES // HEAD_DIM)) for t in (cos64, sa64, sb64))


def _rot(t, tabs):
    if tabs is None:
        return t
    cos, sa, sb = tabs
    return t * cos + pltpu.roll(t, LANES - ROPE_FREQS, 1) * sa + pltpu.roll(t, ROPE_FREQS, 1) * sb


def _dup_halves(t):
    lo = lax.broadcasted_iota(jnp.int32, t.shape, 1) < HALF
    ta = jnp.where(lo, t, 0.0)
    tb = t - ta
    return ta + pltpu.roll(ta, HALF, 1), tb + pltpu.roll(tb, HALF, 1)


def _proj0_kernel(*refs, rope):
    if rope:
        x_ref, sc_ref, sh_ref, w_ref, cos_ref, sa_ref, sb_ref, q_ref, k_ref, v_ref, u_ref = refs
        tabs = (cos_ref[...], sa_ref[...], sb_ref[...])
    else:
        x_ref, sc_ref, sh_ref, w_ref, q_ref, k_ref, v_ref, u_ref = refs
        tabs = None
    h = (x_ref[0] * (1.0 + sc_ref[0]) + sh_ref[0]).astype(BF16)
    r = jnp.dot(h, w_ref[...], preferred_element_type=F32)
    scale = LOG2_E * HEAD_DIM ** -0.5
    for s in range(SWA_Q_W // LANES):
        q_ref[0, :, s * LANES:(s + 1) * LANES] = (_rot(r[:, s * LANES:(s + 1) * LANES], tabs) * scale).astype(BF16)
    k0, k1 = _dup_halves(_rot(r[:, SWA_Q_W:SWA_Q_W + LANES], tabs))
    vv = r[:, SWA_Q_W + LANES:SWA_Q_W + 2 * LANES]
    lo = lax.broadcasted_iota(jnp.int32, vv.shape, 1) < HALF
    k_ref[0, 0] = k0.astype(BF16)
    k_ref[0, 1] = k1.astype(BF16)
    v_ref[0, 0] = jnp.where(lo, vv, 1.0).astype(BF16)
    v_ref[0, 1] = jnp.where(lo, pltpu.roll(vv, HALF, 1), 1.0).astype(BF16)
    u_ref[0] = r[:, SWA_Q_W + 2 * LANES:]


def _proj0(x, sc, sh, w_bf16, tabs, tm):
    b, n, d = x.shape
    rope = tabs is not None
    in_specs = [pl.BlockSpec((1, tm, d), lambda i, j: (i, j, 0)), sc.spec(), sh.spec(),
                pl.BlockSpec(w_bf16.shape, lambda i, j: (0, 0))]
    args = [x, sc.table, sh.table, w_bf16]
    if rope:
        in_specs += [pl.BlockSpec((tm, LANES), lambda i, j: (j, 0))] * 3
        args += list(tabs)
    kv_spec = pl.BlockSpec((1, SWA_KV_HEADS, tm, LANES), lambda i, j: (i, 0, j, 0))
    return pl.pallas_call(
        functools.partial(_proj0_kernel, rope=rope),
        out_shape=(_sds((b, n, SWA_Q_W), BF16), _sds((b, SWA_KV_HEADS, n, LANES), BF16),
                   _sds((b, SWA_KV_HEADS, n, LANES), BF16), _sds((b, n, SSM_WIDTH), F32)),
        grid=(b, n // tm),
        in_specs=in_specs,
        out_specs=(pl.BlockSpec((1, tm, SWA_Q_W), lambda i, j: (i, j, 0)), kv_spec, kv_spec,
                   pl.BlockSpec((1, tm, SSM_WIDTH), lambda i, j: (i, j, 0))),
        compiler_params=_cparams("parallel", "parallel"),
        name="proj0_rope" if rope else "proj0_ctx",
    )(*args)


def _swa_kernel(*refs, tq, nsub, local, n_lat):
    if local:
        sink_ref, q_ref, k_ref, v_ref, kc_ref, vc_ref, o_ref = refs
    else:
        sink_ref, q_ref, kc_ref, vc_ref, o_ref = refs
    rows = 4 * tq
    lo = lax.broadcasted_iota(jnp.int32, (tq, LANES), 1) < HALF
    lo4 = lax.broadcasted_iota(jnp.int32, (rows, LANES), 1) < HALF
    rown = lax.broadcasted_iota(jnp.int32, (rows, 1), 0)
    for sub in range(nsub):
        j = pl.program_id(1) * nsub + sub
        r0 = sub * tq
        if local:
            span = 3 * SWA_BLOCK
            start = pl.multiple_of(jnp.clip((j - 1) * SWA_BLOCK, 0, n_lat - span), SWA_BLOCK)
            rr = lax.broadcasted_iota(jnp.int32, (rows, span), 0)
            cc = lax.broadcasted_iota(jnp.int32, (rows, span), 1)
            qpos = j * tq + (rr & (tq - 1))
            mask = jnp.abs(qpos - (start + cc)) <= SWA_WINDOW
        for h in range(SWA_KV_HEADS):
            qa = q_ref[0, r0:r0 + tq, (2 * h) * LANES:(2 * h + 1) * LANES].astype(F32)
            qb = q_ref[0, r0:r0 + tq, (2 * h + 1) * LANES:(2 * h + 2) * LANES].astype(F32)
            q4 = jnp.concatenate([jnp.where(lo, qa, 0.0), jnp.where(lo, 0.0, qa),
                                  jnp.where(lo, qb, 0.0), jnp.where(lo, 0.0, qb)], 0).astype(BF16)
            sink = LOG2_E * jnp.where(rown < tq, sink_ref[4 * h],
                                      jnp.where(rown < 2 * tq, sink_ref[4 * h + 1],
                                                jnp.where(rown < 3 * tq, sink_ref[4 * h + 2], sink_ref[4 * h + 3])))
            s_ctx = _nt_dot(q4, kc_ref[0, h])
            m = jnp.maximum(jnp.max(s_ctx, -1, keepdims=True), sink)
            if local:
                s_loc = jnp.where(mask, _nt_dot(q4, k_ref[0, h, pl.ds(start, span), :]), NEG_INF)
                m = jnp.maximum(m, jnp.max(s_loc, -1, keepdims=True))
            o4 = jnp.dot(jnp.exp2(s_ctx - m).astype(BF16), vc_ref[0, h], preferred_element_type=F32)
            if local:
                o4 = o4 + jnp.dot(jnp.exp2(s_loc - m).astype(BF16), v_ref[0, h, pl.ds(start, span), :],
                                  preferred_element_type=F32)
            o4 = o4 + jnp.where(lo4, 0.0, jnp.exp2(sink - m))
            o4 = o4 * (1.0 / jnp.where(lo4, pltpu.roll(o4, HALF, 1), 1.0))
            for s in range(2):
                even, odd = o4[2 * s * tq:(2 * s + 1) * tq], o4[(2 * s + 1) * tq:(2 * s + 2) * tq]
                o_ref[0, r0:r0 + tq, (2 * h + s) * LANES:(2 * h + s + 1) * LANES] = jnp.where(
                    lo, even, pltpu.roll(odd, HALF, 1)).astype(BF16)


def _swa_attention(sink, q, k, v, kc, vc):
    b, n, _ = q.shape
    nc = kc.shape[2]
    tq, nsub = SWA_BLOCK, SWA_BLOCKS_PER_STEP
    full = lambda m: pl.BlockSpec((1, SWA_KV_HEADS, m, LANES), lambda i, j: (i, 0, 0, 0))
    return pl.pallas_call(
        functools.partial(_swa_kernel, tq=tq, nsub=nsub, local=True, n_lat=n),
        out_shape=_sds((b, n, SWA_Q_W), BF16),
        grid=(b, n // (tq * nsub)),
        in_specs=[pl.BlockSpec(memory_space=pltpu.SMEM),
                  pl.BlockSpec((1, tq * nsub, SWA_Q_W), lambda i, j: (i, j, 0)),
                  full(n), full(n), full(nc), full(nc)],
        out_specs=pl.BlockSpec((1, tq * nsub, SWA_Q_W), lambda i, j: (i, j, 0)),
        compiler_params=_cparams("parallel", "arbitrary"),
        name="swa_attention",
    )(sink, q, k, v, kc, vc)


def _ctx_attention(sink, qc, kc, vc):
    b, nc, _ = qc.shape
    full = pl.BlockSpec((1, SWA_KV_HEADS, nc, LANES), lambda i, j: (i, 0, 0, 0))
    return pl.pallas_call(
        functools.partial(_swa_kernel, tq=nc, nsub=1, local=False, n_lat=0),
        out_shape=_sds((b, nc, SWA_Q_W), BF16),
        grid=(b, 1),
        in_specs=[pl.BlockSpec(memory_space=pltpu.SMEM),
                  pl.BlockSpec((1, nc, SWA_Q_W), lambda i, j: (i, 0, 0)), full, full],
        out_specs=pl.BlockSpec((1, nc, SWA_Q_W), lambda i, j: (i, 0, 0)),
        compiler_params=_cparams("parallel", "arbitrary"),
        name="ctx_attention",
    )(sink, qc, kc, vc)


def _cpow(e, lr, li):
    mag = jnp.exp(e * lr)
    return mag * jnp.cos(e * li), mag * jnp.sin(e * li)


def _s5_prep_kernel(row_ref, col_ref, bt_ref, ct_ref, ctt_ref, min_ref, mintra_ref, mst_ref, al_ref):
    L, C = SSM_CHUNK, SSM_GROUP
    ar, ai, dt = row_ref[0, 0:1, :], row_ref[0, 1:2, :], row_ref[0, 2:3, :]
    lr, li = dt * ar, dt * ai
    a1_re, a1_im = _cpow(1.0, lr, li)
    den = ar * ar + ai * ai
    nr = a1_re - 1.0
    coef_re = (nr * ar + a1_im * ai) / den
    coef_im = (a1_im * ar - nr * ai) / den
    b_re, b_im = bt_ref[0, 0], bt_ref[0, 1]
    bb_re = coef_re * b_re - coef_im * b_im
    bb_im = coef_re * b_im + coef_im * b_re
    c_re, c_im = ct_ref[0, 0], ct_ref[0, 1]
    fwd = lax.broadcasted_iota(jnp.int32, (L, LANES), 1) < HALF
    step = lax.broadcasted_iota(jnp.int32, (L, LANES), 0).astype(F32)
    pin_re, pin_im = _cpow(jnp.where(fwd, (L - 1.0) - step, step), lr, li)
    pst_re, pst_im = _cpow(jnp.where(fwd, step + 1.0, float(L) - step), lr, li)
    for j in range(L):
        pr, pi = pin_re[j:j + 1], pin_im[j:j + 1]
        min_ref[0, j * C:(j + 1) * C, 0:LANES] = (pr * bb_re - pi * bb_im).astype(BF16)
        min_ref[0, j * C:(j + 1) * C, LANES:] = (pr * bb_im + pi * bb_re).astype(BF16)
        pr, pi = pst_re[j:j + 1], pst_im[j:j + 1]
        mst_ref[0, j * C:(j + 1) * C, 0:LANES] = (pr * c_re - pi * c_im).astype(BF16)
        mst_ref[0, j * C:(j + 1) * C, LANES:] = (-(pr * c_im + pi * c_re)).astype(BF16)
    al_re, al_im = _cpow(float(L), lr, li)
    al_ref[0, 0:1, :] = al_re
    al_ref[0, 1:2, :] = al_im
    lag = (lax.broadcasted_iota(jnp.int32, (SSM_STATE, SSM_CW), 1) // C).astype(F32)
    rep = (lax.broadcasted_iota(jnp.int32, (C, SSM_CW), 1) % C
           == lax.broadcasted_iota(jnp.int32, (C, SSM_CW), 0)).astype(F32)
    ks = []
    for d in range(2):
        col = col_ref[0, d]
        p_re, p_im = _cpow(lag if d == 0 else (L - 1.0) - lag, col[:, 2:3] * col[:, 0:1], col[:, 2:3] * col[:, 1:2])
        ct_re = jnp.dot(ctt_ref[0, d, 0], rep, preferred_element_type=F32, precision=HIGHEST)
        ct_im = jnp.dot(ctt_ref[0, d, 1], rep, preferred_element_type=F32, precision=HIGHEST)
        ca_re = ct_re * p_re - ct_im * p_im
        ca_im = ct_re * p_im + ct_im * p_re
        sl = slice(d * HALF, (d + 1) * HALF)
        ks.append(jnp.dot(bb_re[:, sl], ca_re, preferred_element_type=F32, precision=HIGHEST)
                  - jnp.dot(bb_im[:, sl], ca_im, preferred_element_type=F32, precision=HIGHEST))
    zeros = jnp.zeros((C, SSM_CW), F32)
    lagged = (pltpu.roll(jnp.concatenate([ks[0], zeros], axis=1), (L - 1) * C, 1)
              + jnp.concatenate([ks[1], zeros], axis=1))
    for j in range(L):
        shift = (L - 1 - j) * C
        win = lagged if shift == 0 else pltpu.roll(lagged, 2 * SSM_CW - shift, 1)
        mintra_ref[0, j * C:(j + 1) * C, :] = win[:, 0:SSM_CW].astype(BF16)


def _ssm_matrices(a_re, a_im, log_step, b_re, b_im, c_re, c_im):
    G, P, C = SSM_GROUPS, SSM_STATE, SSM_GROUP
    f = lambda t: t.astype(F32)
    dt = jnp.exp(f(log_step))
    dtp = jnp.broadcast_to(dt[..., None], (2, G, P))
    row = jnp.stack([f(a_re), f(a_im), dtp], 0)
    row = jnp.transpose(row, (2, 0, 1, 3)).reshape(G, 3, 2 * P)
    col = jnp.transpose(jnp.stack([f(a_re), f(a_im), dtp], -1), (1, 0, 2, 3))
    lanes = lambda t: jnp.transpose(t, (1, 2, 0, 3)).reshape(G, C, 2 * P)
    bt = jnp.stack([lanes(jnp.transpose(f(b_re), (0, 1, 3, 2))), lanes(jnp.transpose(f(b_im), (0, 1, 3, 2)))], 1)
    ct = jnp.stack([lanes(f(c_re)), lanes(f(c_im))], 1)
    ctt = jnp.stack([jnp.transpose(f(c_re), (1, 0, 3, 2)), jnp.transpose(f(c_im), (1, 0, 3, 2))], 2)
    mat = pl.BlockSpec((1, SSM_CW, SSM_CW), lambda g: (g, 0, 0))
    return pl.pallas_call(
        _s5_prep_kernel,
        out_shape=(_sds((G, SSM_CW, SSM_CW), BF16),) * 3 + (_sds((G, 2, LANES), F32),),
        grid=(G,),
        in_specs=[pl.BlockSpec((1, 3, LANES), lambda g: (g, 0, 0)),
                  pl.BlockSpec((1, 2, P, 3), lambda g: (g, 0, 0, 0)),
                  pl.BlockSpec((1, 2, C, LANES), lambda g: (g, 0, 0, 0)),
                  pl.BlockSpec((1, 2, C, LANES), lambda g: (g, 0, 0, 0)),
                  pl.BlockSpec((1, 2, 2, P, C), lambda g: (g, 0, 0, 0, 0))],
        out_specs=(mat, mat, mat, pl.BlockSpec((1, 2, LANES), lambda g: (g, 0, 0))),
        compiler_params=_cparams("parallel"),
        name="s5_prep",
    )(row, col, bt, ct, ctt)


def _ssm_pack_kernel(uc_ref, u_ref, x_ref, *, nc_ctx, nc_lat):
    for src, row0, nch in ((uc_ref, 0, nc_ctx), (u_ref, nc_ctx, nc_lat)):
        steps = [src[0, pl.ds(j, nch, stride=SSM_CHUNK), :] for j in range(SSM_CHUNK)]
        lane = lax.broadcasted_iota(jnp.int32, (nch, LANES), 1)
        piece = [(lane >= jj * SSM_GROUP) & (lane < (jj + 1) * SSM_GROUP) for jj in range(SSM_SLAB_GROUPS)]
        for g in range(SSM_SLAB_GROUPS):
            for h in range(SSM_CW // LANES):
                acc = jnp.zeros((nch, LANES), F32)
                for jj in range(SSM_SLAB_GROUPS):
                    z = steps[h * SSM_SLAB_GROUPS + jj]
                    shift = ((jj - g) * SSM_GROUP) % LANES
                    acc = jnp.where(piece[jj], z if shift == 0 else pltpu.roll(z, shift, 1), acc)
                x_ref[g, row0:row0 + nch, h * LANES:(h + 1) * LANES] = acc.astype(BF16)


def _ssm_unpack_kernel(y_ref, oc_ref, o_ref, *, nc_ctx, nc_lat):
    for dst, row0, nch in ((oc_ref, 0, nc_ctx), (o_ref, nc_ctx, nc_lat)):
        lane = lax.broadcasted_iota(jnp.int32, (nch, LANES), 1)
        piece = [(lane >= g * SSM_GROUP) & (lane < (g + 1) * SSM_GROUP) for g in range(SSM_SLAB_GROUPS)]
        for i in range(SSM_CHUNK):
            h, ii = divmod(i, SSM_SLAB_GROUPS)
            acc = jnp.zeros((nch, LANES), F32)
            for g in range(SSM_SLAB_GROUPS):
                z = y_ref[g, row0:row0 + nch, h * LANES:(h + 1) * LANES]
                shift = ((g - ii) * SSM_GROUP) % LANES
                acc = jnp.where(piece[g], z if shift == 0 else pltpu.roll(z, shift, 1), acc)
            dst[0, pl.ds(i, nch, stride=SSM_CHUNK), :] = acc


def _ssm_kernel(x_ref, min_ref, mintra_ref, mstate_ref, al_ref, y_ref, v_scr, s_scr, *, nb, nc_ctx, n_chunks):
    n_grp = x_ref.shape[0]
    for g in range(n_grp):
        v = jnp.dot(x_ref[g], min_ref[g], preferred_element_type=F32)
        v_scr[g, 0] = v[:, 0:LANES]
        v_scr[g, 1] = v[:, LANES:]
    ar = [al_ref[g, 0:1, :] for g in range(n_grp)]
    ai = [al_ref[g, 1:2, :] for g in range(n_grp)]
    lo = lax.broadcasted_iota(jnp.int32, (nb, LANES), 1) < HALF

    def body(k, carry):
        kr = jnp.where(k < nc_ctx, nc_ctx - 1 - k, n_chunks - 1 + nc_ctx - k)
        rf = pl.ds(k, nb, stride=n_chunks)
        rr = pl.ds(kr, nb, stride=n_chunks)
        out = []
        for g in range(n_grp):
            sre, sim = carry[2 * g], carry[2 * g + 1]
            s_scr[g, 0, rf, :] = sre
            s_scr[g, 1, rr, :] = sre
            s_scr[g, 2, rf, :] = sim
            s_scr[g, 3, rr, :] = sim
            vre = jnp.where(lo, v_scr[g, 0, rf, :], v_scr[g, 0, rr, :])
            vim = jnp.where(lo, v_scr[g, 1, rf, :], v_scr[g, 1, rr, :])
            out += [ar[g] * sre - ai[g] * sim + vre, ar[g] * sim + ai[g] * sre + vim]
        return tuple(out)

    zero = jnp.zeros((nb, LANES), F32)
    lax.fori_loop(0, n_chunks, body, (zero,) * (2 * n_grp))
    lo_all = lax.broadcasted_iota(jnp.int32, (s_scr.shape[2], LANES), 1) < HALF
    for g in range(n_grp):
        s_in = jnp.concatenate([jnp.where(lo_all, s_scr[g, 0], s_scr[g, 1]),
                                jnp.where(lo_all, s_scr[g, 2], s_scr[g, 3])], axis=1).astype(BF16)
        y_ref[g] = jnp.dot(x_ref[g], mintra_ref[g], preferred_element_type=F32) + _nt_dot(s_in, mstate_ref[g])


def _ssm_scan(u, uc, mats):
    m_in, m_intra, m_state, a_l = mats
    b, n, _ = u.shape
    nc = uc.shape[1]
    nc_ctx, nc_lat = nc // SSM_CHUNK, n // SSM_CHUNK
    n_chunks = nc_ctx + nc_lat
    r = n_chunks * b
    nat = lambda m: pl.BlockSpec((1, m, LANES), lambda i, s: (i, 0, s))
    grp = pl.BlockSpec((SSM_SLAB_GROUPS, n_chunks, SSM_CW), lambda i, s: (s, i, 0))
    xg = pl.pallas_call(
        functools.partial(_ssm_pack_kernel, nc_ctx=nc_ctx, nc_lat=nc_lat),
        out_shape=_sds((SSM_GROUPS, r, SSM_CW), BF16),
        grid=(b, SSM_GROUPS // SSM_SLAB_GROUPS),
        in_specs=[nat(nc), nat(n)],
        out_specs=grp,
        compiler_params=_cparams("parallel", "parallel"),
        name="ssm_pack",
    )(uc, u)
    gs = SSM_GROUPS_PER_STEP
    mat = pl.BlockSpec((gs, SSM_CW, SSM_CW), lambda g: (g, 0, 0))
    yg = pl.pallas_call(
        functools.partial(_ssm_kernel, nb=b, nc_ctx=nc_ctx, n_chunks=n_chunks),
        out_shape=_sds((SSM_GROUPS, r, SSM_CW), F32),
        grid=(SSM_GROUPS // gs,),
        in_specs=[pl.BlockSpec((gs, r, SSM_CW), lambda g: (g, 0, 0)), mat, mat, mat,
                  pl.BlockSpec((gs, 2, LANES), lambda g: (g, 0, 0))],
        out_specs=pl.BlockSpec((gs, r, SSM_CW), lambda g: (g, 0, 0)),
        scratch_shapes=[pltpu.VMEM((gs, 2, r, LANES), F32), pltpu.VMEM((gs, 4, r, LANES), F32)],
        compiler_params=_cparams("parallel"),
        name="ssm_scan",
    )(xg, m_in, m_intra, m_state, a_l)
    ysc, ys = pl.pallas_call(
        functools.partial(_ssm_unpack_kernel, nc_ctx=nc_ctx, nc_lat=nc_lat),
        out_shape=(_sds((b, nc, SSM_WIDTH), F32), _sds((b, n, SSM_WIDTH), F32)),
        grid=(b, SSM_GROUPS // SSM_SLAB_GROUPS),
        in_specs=[grp],
        out_specs=(nat(nc), nat(n)),
        compiler_params=_cparams("parallel", "parallel"),
        name="ssm_unpack",
    )(yg)
    return ys, ysc


def _split_rows(ref, val):
    m = val.shape[0]
    for s in range(TOKEN_TILE_ROWS):
        ref[pl.ds(s, m, stride=TOKEN_TILE_ROWS), :] = val[:, s * LANES:(s + 1) * LANES]


def _merge_rows(ref):
    m = ref.shape[0] // TOKEN_TILE_ROWS
    return jnp.concatenate([ref[pl.ds(s, m, stride=TOKEN_TILE_ROWS), :] for s in range(TOKEN_TILE_ROWS)], axis=1)


def _token_tile(ref, t):
    return ref.at[pl.ds(pl.multiple_of(t * TOKEN_TILE_ROWS, TOKEN_TILE_ROWS), TOKEN_TILE_ROWS)]


def _router_logits(w, bias, h):
    w_hi, h_hi = w.astype(BF16), h.astype(BF16)
    w_lo, h_lo = (w - w_hi.astype(F32)).astype(BF16), (h - h_hi.astype(F32)).astype(BF16)
    return _nt_dot(w_hi, h_hi) + (_nt_dot(w_hi, h_lo) + _nt_dot(w_lo, h_hi)) + bias


def _post_kernel(*refs, alpha, with_ssm):
    wr_ref, br_ref = refs[-5:-3]
    refs = refs[:-5] + refs[-3:]
    if with_ssm:
        (att_ref, ys_ref, u_ref, dsk_ref, gw_ref, gb_ref, wo_ref, x_ref, g1_ref, lg_ref, lb_ref, sc2_ref, sh2_ref,
         x1_ref, h2_ref, lgt_ref) = refs
        y = ys_ref[0] + u_ref[0] * dsk_ref[...]
        gl = jax.nn.gelu(y)
        gate = jax.nn.sigmoid(jnp.dot(gl.astype(BF16), gw_ref[...], preferred_element_type=F32) + gb_ref[...])
        ssm = (gl * gate).astype(BF16)
        mix = (jnp.dot(att_ref[0], wo_ref[0:SWA_Q_W, :], preferred_element_type=F32)
               + jnp.dot(ssm, wo_ref[SWA_Q_W:, :], preferred_element_type=F32))
    else:
        att_ref, wo_ref, x_ref, g1_ref, lg_ref, lb_ref, sc2_ref, sh2_ref, x1_ref, h2_ref, lgt_ref = refs
        mix = jnp.dot(att_ref[0], wo_ref[...], preferred_element_type=F32)
    x1 = _layer_norm(alpha * x_ref[0] + g1_ref[0] * mix, lg_ref[...], lb_ref[...])
    x1_ref[0] = x1
    h2 = x1 * (1.0 + sc2_ref[0]) + sh2_ref[0]
    _split_rows(h2_ref, h2)
    lgt_ref[...] = _router_logits(wr_ref[...], br_ref[...], h2)


def _post(att, ssm_args, w_out_bf16, x, g1, ln_g, ln_b, sc2, sh2, router_wb, alpha, tm):
    b, n, d = x.shape
    tok = lambda w: pl.BlockSpec((1, tm, w), lambda i, j: (i, j, 0))
    const = lambda a: pl.BlockSpec(a.shape, lambda i, j: (0,) * a.ndim)
    in_specs = [tok(att.shape[-1])]
    args = [att]
    if ssm_args is not None:
        ys, u, dsk, gw, gb = ssm_args
        in_specs += [tok(SSM_WIDTH), tok(SSM_WIDTH), const(dsk), const(gw), const(gb)]
        args += [ys, u, dsk, gw, gb]
    in_specs += [const(w_out_bf16), tok(d), g1.spec(), const(ln_g), const(ln_b), sc2.spec(), sh2.spec()]
    args += [w_out_bf16, x, g1.table, ln_g, ln_b, sc2.table, sh2.table]
    in_specs += [const(a) for a in router_wb]
    args += list(router_wb)
    per_b = n // tm
    n_logit = router_wb[0].shape[0]
    return pl.pallas_call(
        functools.partial(_post_kernel, alpha=alpha, with_ssm=ssm_args is not None),
        out_shape=(_sds((b, n, d), F32), _sds((b * n * TOKEN_TILE_ROWS, LANES), F32), _sds((n_logit, b * n), F32)),
        grid=(b, n // tm),
        in_specs=in_specs,
        out_specs=(tok(d), pl.BlockSpec((tm * TOKEN_TILE_ROWS, LANES), lambda i, j: (i * per_b + j, 0)),
                   pl.BlockSpec((n_logit, tm), lambda i, j: (0, i * per_b + j))),
        compiler_params=_cparams("parallel", "parallel"),
        name="post_mixer_ssm" if ssm_args is not None else "post_mixer",
    )(*args)


def _first_max(v, sub):
    m = jnp.max(v, 0, keepdims=True)
    idx = jnp.min(jnp.where(v == m, sub, float(SUBLANES)), 0, keepdims=True)
    return m, idx


def _stream_blocks(toks, tb):
    starts = [0]
    for t in toks:
        starts.append(starts[-1] + t.shape[0] // (tb * TOKEN_TILE_ROWS))
    return starts


def _stream_spec(tok, tb, start):
    last = tok.shape[0] // (tb * TOKEN_TILE_ROWS) - 1
    return pl.BlockSpec((tb * TOKEN_TILE_ROWS, LANES), lambda i, *_: (jnp.clip(i - start, 0, last), 0))


def _router_kernel(*refs, starts):
    n_streams = len(starts) - 1
    lgt_refs = refs[:n_streams]
    ids_ref, wts_ref, rank_ref, cnt_ref, carry_scr = refs[n_streams:]
    step = pl.program_id(0)

    @pl.when(step == 0)
    def _():
        carry_scr[...] = jnp.zeros_like(carry_scr)

    logits = lgt_refs[0][...]
    for ref, start in zip(lgt_refs[1:], starts[1:]):
        logits = jnp.where(step >= start, ref[...], logits)
    tm = logits.shape[1]
    sub = lax.broadcasted_iota(jnp.int32, (SUBLANES, tm), 0).astype(F32)
    gl = logits[0:SUBLANES]
    gmax, gi = _first_max(gl, sub)
    gp = 1.0 / jnp.sum(jnp.exp(gl - gmax), 0, keepdims=True)
    le = logits[ROUTER_EXPERT_ROW0:ROUTER_EXPERT_ROW0 + MOE_EPG]
    for g in range(1, MOE_GROUPS):
        le = jnp.where(gi == float(g), logits[ROUTER_EXPERT_ROW0 + g * MOE_EPG:ROUTER_EXPERT_ROW0 + (g + 1) * MOE_EPG], le)
    m1, i1 = _first_max(le, sub)
    m2, i2 = _first_max(jnp.where(sub == i1, NEG_INF, le), sub)
    t = jnp.exp(m2 - m1)
    e1 = gi * float(MOE_EPG) + i1
    e2 = gi * float(MOE_EPG) + i2
    ids_ref[0:1, :] = e1.astype(jnp.int32)
    ids_ref[1:2, :] = e2.astype(jnp.int32)
    wts_ref[0:1, :] = gp / (1.0 + t)
    wts_ref[1:2, :] = gp * t / (1.0 + t)
    esub = lax.broadcasted_iota(jnp.int32, (MOE_EXPERTS, tm), 0).astype(F32)
    oh1 = (esub == e1).astype(F32)
    oh2 = (esub == e2).astype(F32)
    both = oh1 + oh2
    earlier = (lax.broadcasted_iota(jnp.int32, (tm, tm), 0) < lax.broadcasted_iota(jnp.int32, (tm, tm), 1))
    prefix = jnp.dot(both.astype(BF16), earlier.astype(BF16), preferred_element_type=F32) + carry_scr[...]
    rank_ref[0:1, :] = jnp.sum(oh1 * prefix, 0, keepdims=True).astype(jnp.int32)
    rank_ref[1:2, :] = jnp.sum(oh2 * prefix, 0, keepdims=True).astype(jnp.int32)
    carry_scr[...] += jnp.sum(both, 1, keepdims=True)
    cnt_ref[...] = jnp.broadcast_to(carry_scr[...], cnt_ref.shape)


def _router_weights(wg, bg, we, be):
    d = wg.shape[0]
    rows = ROUTER_EXPERT_ROW0 + MOE_EXPERTS
    w = jnp.zeros((rows, d), F32)
    w = w.at[:MOE_GROUPS].set(wg.T)
    w = w.at[ROUTER_EXPERT_ROW0:].set(jnp.transpose(we, (0, 2, 1)).reshape(MOE_EXPERTS, d))
    bias = jnp.full((rows, 1), NEG_INF, F32)
    bias = bias.at[:MOE_GROUPS, 0].set(bg)
    bias = bias.at[ROUTER_EXPERT_ROW0:, 0].set(be.reshape(-1))
    return w, bias


def _router(logit_streams, tm):
    starts = [0]
    for lg in logit_streams:
        starts.append(starts[-1] + lg.shape[1] // tm)
    t = starts[-1] * tm
    rows = logit_streams[0].shape[0]
    stream_spec = lambda lg, s0: pl.BlockSpec(
        (rows, tm), lambda i: (0, jnp.clip(i - s0, 0, lg.shape[1] // tm - 1)))
    pair = pl.BlockSpec((MOE_TOPK, tm), lambda i: (0, i))
    ids, wts, rank, cnt = pl.pallas_call(
        functools.partial(_router_kernel, starts=starts),
        out_shape=(_sds((MOE_TOPK, t), jnp.int32), _sds((MOE_TOPK, t), F32), _sds((MOE_TOPK, t), jnp.int32),
                   _sds((MOE_EXPERTS, LANES), F32)),
        grid=(t // tm,),
        in_specs=[stream_spec(lg, s0) for lg, s0 in zip(logit_streams, starts)],
        out_specs=(pair, pair, pair, pl.BlockSpec((MOE_EXPERTS, LANES), lambda i: (0, 0))),
        scratch_shapes=[pltpu.VMEM((MOE_EXPERTS, 1), F32)],
        compiler_params=_cparams("arbitrary"),
        name="moe_router",
    )(*logit_streams)
    return ids, wts, rank, cnt[:, 0].astype(jnp.int32)


def _moe_plan(ids, rank, counts, n_tok):
    tm = MOE_ROW_TILE
    padded = ((counts + tm - 1) // tm) * tm
    ends = jnp.cumsum(padded)
    offs = ends - padded
    experts = jnp.arange(MOE_EXPERTS, dtype=jnp.int32)
    dest = (jnp.sum(jnp.where(ids[..., None] == experts, offs, 0), -1) + rank).astype(jnp.int32)
    n_tiles = (MOE_TOPK * n_tok + MOE_EXPERTS * (tm - 1)) // tm
    starts = jnp.arange(n_tiles, dtype=jnp.int32) * tm
    tile_expert = jnp.minimum(jnp.sum((ends[None, :] <= starts[:, None]).astype(jnp.int32), -1), MOE_EXPERTS - 1)
    n_valid = (ends[-1] // tm).astype(jnp.int32).reshape(1)
    pad = jnp.stack([offs + counts, padded - counts, jnp.broadcast_to(n_valid, counts.shape)]).astype(jnp.int32)
    return dest, tile_expert, n_valid, pad, n_tiles * tm


def _pad_fill(pad_ref, zero_scr, xs_ref, sem, wait):
    def per_expert(e, carry):
        first, count = pad_ref[0, e], pad_ref[1, e]
        piece = MOE_ROW_TILE // 2
        while piece >= 1:
            row = first + (count & ~(2 * piece - 1))
            n = piece * TOKEN_TILE_ROWS

            @pl.when((count & piece) != 0)
            def _(row=row, n=n):
                copy = pltpu.make_async_copy(
                    zero_scr.at[pl.ds(0, n)],
                    xs_ref.at[pl.ds(pl.multiple_of(row * TOKEN_TILE_ROWS, TOKEN_TILE_ROWS), n)], sem)
                copy.wait() if wait else copy.start()

            piece //= 2
        return carry

    lax.fori_loop(0, MOE_EXPERTS, per_expert, 0)
    half = MOE_ROW_TILE // 2 * TOKEN_TILE_ROWS
    n_tiles = xs_ref.shape[0] // (2 * half)

    def per_tile(t, carry):
        for h in range(2):
            copy = pltpu.make_async_copy(zero_scr, xs_ref.at[pl.ds(pl.multiple_of((2 * t + h) * half, half), half)],
                                         sem)
            copy.wait() if wait else copy.start()
        return carry

    lax.fori_loop(pad_ref[2, 0], n_tiles, per_tile, 0)


def _dispatch_kernel(*refs, tb, starts):
    n_streams = len(starts) - 1
    dest_ref, pad_ref = refs[:2]
    tok_refs = refs[2:2 + n_streams]
    xs_ref, zero_scr, sem, pad_sem = refs[2 + n_streams:]
    step = pl.program_id(0)

    @pl.when(step == 0)
    def _():
        zero_scr[...] = jnp.zeros_like(zero_scr)
        _pad_fill(pad_ref, zero_scr, xs_ref, pad_sem, wait=False)

    for s, tok_ref in enumerate(tok_refs):
        @pl.when((step >= starts[s]) & (step < starts[s + 1]))
        def _(tok_ref=tok_ref):
            def body(r, carry):
                for k in range(MOE_TOPK):
                    pltpu.make_async_copy(_token_tile(tok_ref, r), _token_tile(xs_ref, dest_ref[MOE_TOPK * r + k]),
                                          sem).start(priority=k % 2)
                return carry

            lax.fori_loop(0, tb, body, 0, unroll=8)

    for k in range(MOE_TOPK):
        pltpu.make_async_copy(tok_refs[0], xs_ref.at[pl.ds(0, tb * TOKEN_TILE_ROWS)], sem).wait()

    @pl.when(step == 0)
    def _():
        _pad_fill(pad_ref, zero_scr, xs_ref, pad_sem, wait=True)


def _dispatch(toks, dest, pad, n_rows):
    tb = MOE_DMA_TILE
    nblk = dest.shape[0] // (MOE_TOPK * tb)
    starts = _stream_blocks(toks, tb)
    return pl.pallas_call(
        functools.partial(_dispatch_kernel, tb=tb, starts=starts),
        out_shape=_sds((n_rows * TOKEN_TILE_ROWS, LANES), F32),
        grid=(nblk,),
        in_specs=[pl.BlockSpec((MOE_TOPK * tb,), lambda i: (i,), memory_space=pltpu.SMEM),
                  pl.BlockSpec(memory_space=pltpu.SMEM)]
        + [_stream_spec(tok, tb, s0) for tok, s0 in zip(toks, starts)],
        out_specs=pl.BlockSpec(memory_space=pl.ANY),
        scratch_shapes=[pltpu.VMEM((MOE_ROW_TILE // 2 * TOKEN_TILE_ROWS, LANES), F32),
                        pltpu.SemaphoreType.DMA(()), pltpu.SemaphoreType.DMA(())],
        compiler_params=_cparams("arbitrary"),
        name="moe_dispatch",
    )(dest, pad, *toks)


def _ffn_kernel(te_ref, nv_ref, x_ref, w1_ref, w3_ref, w2_ref, y_ref, w13_scr, w2_scr):
    i = pl.program_id(0)
    f = w1_ref.shape[2]

    @pl.when((i == 0) | (te_ref[i] != te_ref[jnp.maximum(i - 1, 0)]))
    def _():
        w13_scr[:, 0:f] = w1_ref[0].astype(BF16)
        w13_scr[:, f:2 * f] = w3_ref[0].astype(BF16)
        w2_scr[...] = w2_ref[0].astype(BF16)

    @pl.when(i < nv_ref[0])
    def _():
        h13 = jnp.dot(_merge_rows(x_ref).astype(BF16), w13_scr[...], preferred_element_type=F32)
        h1 = h13[:, 0:f]
        hh = (h1 * jax.nn.sigmoid(h1) * h13[:, f:2 * f]).astype(BF16)
        _split_rows(y_ref, jnp.dot(hh, w2_scr[...], preferred_element_type=F32))

    @pl.when(i >= nv_ref[0])
    def _():
        y_ref[...] = jnp.zeros_like(y_ref)


def _expert_ffn(tile_expert, n_valid, xs, w1, w3, w2):
    p = xs.shape[0] // TOKEN_TILE_ROWS
    _, d, f = w1.shape
    tm = MOE_ROW_TILE
    rows = pl.BlockSpec((tm * TOKEN_TILE_ROWS, LANES), lambda i, te, nv: (i, 0))
    rows_in = pl.BlockSpec((tm * TOKEN_TILE_ROWS, LANES), lambda i, te, nv: (jnp.minimum(i, nv[0] - 1), 0))
    return pl.pallas_call(
        _ffn_kernel,
        out_shape=_sds(xs.shape, F32),
        grid_spec=pltpu.PrefetchScalarGridSpec(
            num_scalar_prefetch=2,
            grid=(p // tm,),
            in_specs=[rows_in,
                      pl.BlockSpec((1, d, f), lambda i, te, nv: (te[i], 0, 0)),
                      pl.BlockSpec((1, d, f), lambda i, te, nv: (te[i], 0, 0)),
                      pl.BlockSpec((1, f, d), lambda i, te, nv: (te[i], 0, 0))],
            out_specs=rows,
            scratch_shapes=[pltpu.VMEM((d, 2 * f), BF16), pltpu.VMEM((f, d), BF16)]),
        compiler_params=_cparams("arbitrary"),
        name="moe_expert_ffn",
    )(tile_expert, n_valid, xs, w1, w3, w2)


def _combine_ln2_kernel(dest_ref, next_ref, x1_ref, ys_ref, wt_ref, g2_ref, lg_ref, lb_ref, o_ref, buf, sem, *,
                        alpha, tb, nblk):
    step = pl.program_id(0)

    def gather(d_ref, slot):
        def body(r, carry):
            for k in range(MOE_TOPK):
                pltpu.make_async_copy(_token_tile(ys_ref, d_ref[MOE_TOPK * r + k]), _token_tile(buf.at[slot, k], r),
                                      sem.at[slot, k]).start(priority=k % 2)
            return carry

        lax.fori_loop(0, tb, body, 0, unroll=8)

    @pl.when(step == 0)
    def _():
        gather(dest_ref, 0)

    @pl.when(step + 1 < nblk)
    def _():
        gather(next_ref, (step + 1) % 2)

    slot = step % 2
    for k in range(MOE_TOPK):
        pltpu.make_async_copy(ys_ref.at[pl.ds(0, tb * TOKEN_TILE_ROWS)], buf.at[slot, k], sem.at[slot, k]).wait()
    f = wt_ref[:, 0:1] * _merge_rows(buf.at[slot, 0]) + wt_ref[:, 1:2] * _merge_rows(buf.at[slot, 1])
    o_ref[...] = _layer_norm(alpha * x1_ref[...] + g2_ref[0] * f, lg_ref[...], lb_ref[...])


def _combine_ln2(dest, x1, ys, wt, g2, ln_g, ln_b, alpha, n_per_sample):
    t, d = x1.shape
    tb = MOE_DMA_TILE
    nblk = t // tb
    const = pl.BlockSpec((1, d), lambda i: (0, 0))
    assert g2.stride == 0 or n_per_sample % tb == 0, (n_per_sample, tb)
    return pl.pallas_call(
        functools.partial(_combine_ln2_kernel, alpha=alpha, tb=tb, nblk=nblk),
        out_shape=_sds((t, d), F32),
        grid=(nblk,),
        in_specs=[pl.BlockSpec((MOE_TOPK * tb,), lambda i: (i,), memory_space=pltpu.SMEM),
                  pl.BlockSpec((MOE_TOPK * tb,), lambda i: (jnp.minimum(i + 1, nblk - 1),), memory_space=pltpu.SMEM),
                  pl.BlockSpec((tb, d), lambda i: (i, 0)),
                  pl.BlockSpec(memory_space=pl.ANY),
                  pl.BlockSpec((tb, MOE_TOPK), lambda i: (i, 0)),
                  g2.spec(lambda i: (i * tb) // n_per_sample),
                  const, const],
        out_specs=pl.BlockSpec((tb, d), lambda i: (i, 0)),
        scratch_shapes=[pltpu.VMEM((2, MOE_TOPK, tb * TOKEN_TILE_ROWS, LANES), F32),
                        pltpu.SemaphoreType.DMA((2, MOE_TOPK))],
        compiler_params=_cparams("arbitrary"),
        name="moe_combine_ln2",
    )(dest, dest, x1, ys, wt, g2.table, ln_g, ln_b)


def _proj1_kernel(*refs, rope, with_q):
    x_ref, sc_ref, sh_ref, w_ref = refs[:4]
    refs = refs[4:]
    tabs = None
    if rope:
        tabs = (refs[0][...], refs[1][...], refs[2][...])
        refs = refs[3:]
    h = (x_ref[0] * (1.0 + sc_ref[0]) + sh_ref[0]).astype(BF16)
    r = jnp.dot(h, w_ref[...], preferred_element_type=F32)
    off = 0
    if with_q:
        q_ref, k_ref, v_ref = refs
        for hd in range(DIF_HEADS):
            q_ref[0, hd] = (_rot(r[:, hd * LANES:(hd + 1) * LANES], tabs) * (LOG2_E * HEAD_DIM ** -0.5)).astype(BF16)
        off = DIF_QK_W
    else:
        k_ref, v_ref = refs
    for hd in range(DIF_HEADS):
        k_ref[0, hd] = _rot(r[:, off + hd * LANES:off + (hd + 1) * LANES], tabs).astype(BF16)
        v_ref[0, hd] = r[:, off + DIF_QK_W + hd * LANES:off + DIF_QK_W + (hd + 1) * LANES].astype(BF16)


def _proj1(x, sc, sh, w_bf16, tabs, with_q, tm):
    b, n, d = x.shape
    rope = tabs is not None
    in_specs = [pl.BlockSpec((1, tm, d), lambda i, j: (i, j, 0)), sc.spec(), sh.spec(),
                pl.BlockSpec(w_bf16.shape, lambda i, j: (0, 0))]
    args = [x, sc.table, sh.table, w_bf16]
    if rope:
        in_specs += [pl.BlockSpec((tm, LANES), lambda i, j: (j, 0))] * 3
        args += list(tabs)
    hm = pl.BlockSpec((1, DIF_HEADS, tm, LANES), lambda i, j: (i, 0, j, 0))
    n_out = 3 if with_q else 2
    return pl.pallas_call(
        functools.partial(_proj1_kernel, rope=rope, with_q=with_q),
        out_shape=(_sds((b, DIF_HEADS, n, LANES), BF16),) * n_out,
        grid=(b, n // tm),
        in_specs=in_specs,
        out_specs=(hm,) * n_out,
        compiler_params=_cparams("parallel", "parallel"),
        name="proj1_qkv" if with_q else "proj1_kv_ctx",
    )(*args)


def _diff_kernel(lam_ref, q_ref, kl_ref, kc_ref, vl_ref, vc_ref, g_ref, o_ref, k_scr, v_scr, *, tq, n_lat, out_scale):
    @pl.when(pl.program_id(2) == 0)
    def _():
        k_scr[0:n_lat] = kl_ref[0, 0]
        k_scr[n_lat:] = kc_ref[0, 0]
        v_scr[0:n_lat, 0:LANES] = vl_ref[0, 0]
        v_scr[n_lat:, 0:LANES] = vc_ref[0, 0]
        v_scr[:, LANES:] = jnp.ones((v_scr.shape[0], LANES), BF16)

    lo = lax.broadcasted_iota(jnp.int32, (DIF_CHAIN_ROWS, LANES), 1) < HALF
    for r0 in range(0, tq, DIF_CHAIN_ROWS):
        q = q_ref[0, 0, r0:r0 + DIF_CHAIN_ROWS, :].astype(F32)
        maps = []
        for qm in (jnp.where(lo, q, 0.0), jnp.where(lo, 0.0, q)):
            s = _nt_dot(qm.astype(BF16), k_scr[...])
            p = jnp.exp2(s - jnp.max(s, -1, keepdims=True)).astype(BF16)
            oe = jnp.dot(p, v_scr[...], preferred_element_type=F32)
            maps.append(oe[:, 0:LANES] * (1.0 / oe[:, LANES:]))
        o = maps[0] - lam_ref[0] * maps[1]
        o = o * lax.rsqrt(jnp.mean(o * o, -1, keepdims=True) + RMS_EPS) * g_ref[...]
        o_ref[0, r0:r0 + DIF_CHAIN_ROWS, :] = (o * out_scale).astype(BF16)


def _diff_attention(lam, q, kl, kc, vl, vc, subln_g, lam_init, tq):
    b, nh, n, _ = q.shape
    nc = kc.shape[2]
    kv = lambda m: pl.BlockSpec((1, 1, m, LANES), lambda i, h, j: (i, h, 0, 0))
    return pl.pallas_call(
        functools.partial(_diff_kernel, tq=tq, n_lat=n, out_scale=1.0 - lam_init),
        out_shape=_sds((b, n, nh * LANES), BF16),
        grid=(b, nh, n // tq),
        in_specs=[pl.BlockSpec(memory_space=pltpu.SMEM),
                  pl.BlockSpec((1, 1, tq, LANES), lambda i, h, j: (i, h, j, 0)),
                  kv(n), kv(nc), kv(n), kv(nc),
                  pl.BlockSpec((1, LANES), lambda i, h, j: (0, 0))],
        out_specs=pl.BlockSpec((1, tq, LANES), lambda i, h, j: (i, j, h)),
        scratch_shapes=[pltpu.VMEM((n + nc, LANES), BF16), pltpu.VMEM((n + nc, 2 * LANES), BF16)],
        compiler_params=_cparams("parallel", "parallel", "arbitrary"),
        name="diff_attention",
    )(lam, q, kl, kc, vl, vc, subln_g)


def _moe_block(toks, logit_streams, layer, moe_w1, moe_w3, moe_w2):
    t = sum(tok.shape[0] for tok in toks) // TOKEN_TILE_ROWS
    tb = MOE_DMA_TILE
    ids, wts, rank, counts = _router(logit_streams, tm=512)
    dest, tile_expert, n_valid, pad, n_rows = _moe_plan(ids, rank, counts, t)
    dest = dest.T.reshape(-1)
    xs = _dispatch(toks, dest, pad, n_rows)
    flat = lambda w: w.reshape((-1,) + w.shape[2:])
    ys = _expert_ffn(tile_expert + layer * MOE_EXPERTS, n_valid, xs, flat(moe_w1), flat(moe_w3), flat(moe_w2))
    return ys, dest, wts.T


def kernel(x, c, ctx, c_ctx, mod_w, mod_b, ln1_g, ln1_b, ln2_g, ln2_b, swa_ssm_w_in, swa_ssm_w_out, swa_sink, ssm_a_re, ssm_a_im, ssm_log_step, ssm_b_re, ssm_b_im, ssm_c_re, ssm_c_im, ssm_d, ssm_glu_w, ssm_glu_b, dif_w_in, dif_w_out, dif_lam_q1, dif_lam_k1, dif_lam_q2, dif_lam_k2, dif_subln_g, moe_wg, moe_bg, moe_we, moe_be, moe_w1, moe_w3, moe_w2):
    bsz, n, d = x.shape
    ctx_len = ctx.shape[1]
    depth = mod_w.shape[0]
    alpha = (2 * depth) ** 0.25
    tabs = _rope_tables(n)

    n_vec = 16
    cvec = jnp.zeros((n_vec, d), F32).at[:bsz].set(c).at[bsz].set(c_ctx)
    mods = _modulation(cvec, mod_w, mod_b)
    mod_table = mods.reshape(depth * n_vec * 6, 1, d)

    xl, xc = x, ctx
    for layer in range(depth):
        need_ctx = layer < depth - 1
        i = layer // 2
        lat = [ModVec(mod_table, (layer * n_vec) * 6 + k, 6) for k in range(6)]
        cx = [ModVec(mod_table, (layer * n_vec + bsz) * 6 + k, 0) for k in range(6)]
        sh1, sc1, g1, sh2, sc2, g2 = lat
        csh1, csc1, cg1, csh2, csc2, cg2 = cx
        lg1, lb1 = ln1_g[layer].reshape(1, d), ln1_b[layer].reshape(1, d)
        lg2, lb2 = ln2_g[layer].reshape(1, d), ln2_b[layer].reshape(1, d)
        router_wb = _router_weights(moe_wg[layer], moe_bg[layer], moe_we[layer], moe_be[layer])
        if layer % 2 == 0:
            w_in = swa_ssm_w_in[i].astype(BF16)
            w_out = swa_ssm_w_out[i].astype(BF16)
            q, k, v, u = _proj0(xl, sc1, sh1, w_in, tabs, tm=512)
            qc, kc, vc, uc = _proj0(xc, csc1, csh1, w_in, None, tm=ctx_len)
            sink = swa_sink[i].astype(F32)
            att = _swa_attention(sink, q, k, v, kc, vc)
            mats = _ssm_matrices(ssm_a_re[i], ssm_a_im[i], ssm_log_step[i], ssm_b_re[i], ssm_b_im[i],
                                 ssm_c_re[i], ssm_c_im[i])
            ys, ysc = _ssm_scan(u, uc, mats)
            glu = (ssm_d[i].reshape(1, SSM_WIDTH).astype(F32), ssm_glu_w[i].astype(BF16),
                   ssm_glu_b[i].reshape(1, SSM_WIDTH).astype(F32))
            x1, tok, lgt = _post(att, (ys, u) + glu, w_out, xl, g1, lg1, lb1, sc2, sh2, router_wb, alpha, 512)
            toks, logit_streams = [tok], [lgt]
            if need_ctx:
                att_c = _ctx_attention(sink, qc, kc, vc)
                xc1, tok_c, lgt_c = _post(att_c, (ysc, uc) + glu, w_out, xc, cg1, lg1, lb1, csc2, csh2, router_wb,
                                          alpha, ctx_len)
                toks.append(tok_c)
                logit_streams.append(lgt_c)
        else:
            lam_init = 0.8 - 0.6 * math.exp(-0.3 * layer)
            w_in = dif_w_in[i].astype(BF16)
            w_out = dif_w_out[i].astype(BF16)
            q, k, v = _proj1(xl, sc1, sh1, w_in, tabs, True, tm=512)
            kc, vc = _proj1(xc, csc1, csh1, w_in[:, DIF_QK_W:], None, False, tm=ctx_len)
            lam = (jnp.exp(jnp.sum(dif_lam_q1[i].astype(F32) * dif_lam_k1[i].astype(F32)))
                   - jnp.exp(jnp.sum(dif_lam_q2[i].astype(F32) * dif_lam_k2[i].astype(F32))) + lam_init).reshape(1)
            att = _diff_attention(lam, q, k, kc, v, vc, dif_subln_g[i].reshape(1, DIF_V_HEAD).astype(F32),
                                  lam_init, tq=n)
            x1, tok, lgt = _post(att, None, w_out, xl, g1, lg1, lb1, sc2, sh2, router_wb, alpha, 512)
            toks, logit_streams = [tok], [lgt]
            if need_ctx:
                raise NotImplementedError("a differential-attention layer followed by another layer")
        ys_moe, dest, wt = _moe_block(toks, logit_streams, layer, moe_w1, moe_w3, moe_w2)
        xl = _combine_ln2(dest[:MOE_TOPK * bsz * n], x1.reshape(-1, d), ys_moe, wt[:bsz * n], g2, lg2, lb2, alpha,
                          n).reshape(bsz, n, d)
        if need_ctx:
            xc = _combine_ln2(dest[MOE_TOPK * bsz * n:], xc1.reshape(-1, d), ys_moe, wt[bsz * n:], cg2, lg2, lb2, alpha,
                              ctx_len).reshape(bsz, ctx_len, d)
    return xl
```

```python
import functools
import math
import typing

import jax
import jax.numpy as jnp
from jax import lax
from jax.experimental import pallas as pl
from jax.experimental.pallas import tpu as pltpu

F32 = jnp.float32
BF16 = jnp.bfloat16
HIGHEST = lax.Precision.HIGHEST

D_MODEL = 1024
GRID_W = 64
HEAD_DIM = 64
ROPE_BASE = 10000.0
ROPE_FREQS = HEAD_DIM // 4
LN_EPS = 1e-5
RMS_EPS = 1e-5
NEG_INF = -1e30
LOG2_E = math.log2(math.e)
LANES = 128
HALF = LANES // 2

SWA_HEADS = 8
SWA_KV_HEADS = 2
SWA_WINDOW = 128
SWA_BLOCK = 128
SWA_BLOCKS_PER_STEP = 4
SWA_Q_W = SWA_HEADS * HEAD_DIM
SWA_KV_W = SWA_KV_HEADS * HEAD_DIM

SSM_WIDTH = D_MODEL // 2
SSM_GROUP = 16
SSM_GROUPS = SSM_WIDTH // SSM_GROUP
SSM_STATE = 64
SSM_CHUNK = 16
SSM_CW = SSM_CHUNK * SSM_GROUP
SSM_SLAB_GROUPS = LANES // SSM_GROUP
SSM_GROUPS_PER_STEP = 4

DIF_HEADS = D_MODEL // (2 * HEAD_DIM)
DIF_QK_W = DIF_HEADS * 2 * HEAD_DIM
DIF_V_HEAD = 2 * HEAD_DIM
DIF_CHAIN_ROWS = 128

MOE_GROUPS = 4
MOE_EPG = 8
MOE_EXPERTS = MOE_GROUPS * MOE_EPG
MOE_TOPK = 2
SUBLANES = 8
ROUTER_EXPERT_ROW0 = SUBLANES
MOE_ROW_TILE = 512
MOE_DMA_TILE = 1024
TOKEN_TILE_ROWS = D_MODEL // LANES

VMEM_LIMIT = 56 * 1024 * 1024


def _cparams(*sem):
    return pltpu.CompilerParams(dimension_semantics=sem, vmem_limit_bytes=VMEM_LIMIT)


def _sds(shape, dtype):
    return jax.ShapeDtypeStruct(shape, dtype)


def _nt_dot(a, b):
    return lax.dot_general(a, b, (((1,), (1,)), ((), ())), preferred_element_type=F32)


def _layer_norm(r, g, b):
    mu = jnp.mean(r, -1, keepdims=True)
    rc = r - mu
    var = jnp.mean(rc * rc, -1, keepdims=True)
    return rc * lax.rsqrt(var + LN_EPS) * g + b


class ModVec(typing.NamedTuple):
    table: jax.Array
    row0: int
    stride: int

    def spec(self, sample_of_step=lambda i, *_: i):
        d = self.table.shape[-1]
        return pl.BlockSpec((1, 1, d), lambda *idx: (self.row0 + self.stride * sample_of_step(*idx), 0, 0))


def _mod_kernel(c_ref, w_ref, b_ref, o_ref):
    cv = c_ref[...]
    s = cv * jax.nn.sigmoid(cv)
    o_ref[0] = jnp.dot(s, w_ref[0], preferred_element_type=F32, precision=HIGHEST) + b_ref[0]


def _modulation(cvec, mod_w, mod_b):
    depth, d, w6 = mod_w.shape
    tn = 1536
    return pl.pallas_call(
        _mod_kernel,
        out_shape=_sds((depth, cvec.shape[0], w6), F32),
        grid=(depth, w6 // tn),
        in_specs=[pl.BlockSpec(cvec.shape, lambda l, j: (0, 0)),
                  pl.BlockSpec((1, d, tn), lambda l, j: (l, 0, j)),
                  pl.BlockSpec((1, 1, tn), lambda l, j: (l, 0, j))],
        out_specs=pl.BlockSpec((1, cvec.shape[0], tn), lambda l, j: (l, 0, j)),
        compiler_params=_cparams("arbitrary", "arbitrary"),
        name="modulation",
    )(cvec, mod_w, mod_b.reshape(depth, 1, w6))


def _rope_tables(n):
    rows = n // GRID_W
    row = jnp.repeat(jnp.arange(rows, dtype=F32), GRID_W)
    col = jnp.tile(jnp.arange(GRID_W, dtype=F32), rows)
    inv = ROPE_BASE ** (-jnp.arange(ROPE_FREQS, dtype=F32) / ROPE_FREQS)
    ang_r = row[:, None] * inv[None, :]
    ang_c = col[:, None] * inv[None, :]
    zeros = jnp.zeros_like(ang_r)
    cos64 = jnp.concatenate([jnp.cos(ang_r), jnp.cos(ang_r), jnp.cos(ang_c), jnp.cos(ang_c)], -1)
    sa64 = jnp.concatenate([-jnp.sin(ang_r), zeros, -jnp.sin(ang_c), zeros], -1)
    sb64 = jnp.concatenate([zeros, jnp.sin(ang_r), zeros, jnp.sin(ang_c)], -1)
    return tuple(jnp.tile(t, (1, LANES // HEAD_DIM)) for t in (cos64, sa64, sb64))


def _rot(t, tabs):
    if tabs is None:
        return t
    cos, sa, sb = tabs
    return t * cos + pltpu.roll(t, LANES - ROPE_FREQS, 1) * sa + pltpu.roll(t, ROPE_FREQS, 1) * sb


def _dup_halves(t):
    lo = lax.broadcasted_iota(jnp.int32, t.shape, 1) < HALF
    ta = jnp.where(lo, t, 0.0)
    tb = t - ta
    return ta + pltpu.roll(ta, HALF, 1), tb + pltpu.roll(tb, HALF, 1)


def _proj0_kernel(*refs, rope):
    if rope:
        x_ref, sc_ref, sh_ref, w_ref, cos_ref, sa_ref, sb_ref, q_ref, k_ref, v_ref, u_ref = refs
        tabs = (cos_ref[...], sa_ref[...], sb_ref[...])
    else:
        x_ref, sc_ref, sh_ref, w_ref, q_ref, k_ref, v_ref, u_ref = refs
        tabs = None
    h = (x_ref[0] * (1.0 + sc_ref[0]) + sh_ref[0]).astype(BF16)
    r = jnp.dot(h, w_ref[...], preferred_element_type=F32)
    scale = LOG2_E * HEAD_DIM ** -0.5
    for s in range(SWA_Q_W // LANES):
        q_ref[0, :, s * LANES:(s + 1) * LANES] = (_rot(r[:, s * LANES:(s + 1) * LANES], tabs) * scale).astype(BF16)
    k0, k1 = _dup_halves(_rot(r[:, SWA_Q_W:SWA_Q_W + LANES], tabs))
    vv = r[:, SWA_Q_W + LANES:SWA_Q_W + 2 * LANES]
    lo = lax.broadcasted_iota(jnp.int32, vv.shape, 1) < HALF
    k_ref[0, 0] = k0.astype(BF16)
    k_ref[0, 1] = k1.astype(BF16)
    v_ref[0, 0] = jnp.where(lo, vv, 1.0).astype(BF16)
    v_ref[0, 1] = jnp.where(lo, pltpu.roll(vv, HALF, 1), 1.0).astype(BF16)
    u_ref[0] = r[:, SWA_Q_W + 2 * LANES:]


def _proj0(x, sc, sh, w_bf16, tabs, tm):
    b, n, d = x.shape
    rope = tabs is not None
    in_specs = [pl.BlockSpec((1, tm, d), lambda i, j: (i, j, 0)), sc.spec(), sh.spec(),
                pl.BlockSpec(w_bf16.shape, lambda i, j: (0, 0))]
    args = [x, sc.table, sh.table, w_bf16]
    if rope:
        in_specs += [pl.BlockSpec((tm, LANES), lambda i, j: (j, 0))] * 3
        args += list(tabs)
    kv_spec = pl.BlockSpec((1, SWA_KV_HEADS, tm, LANES), lambda i, j: (i, 0, j, 0))
    return pl.pallas_call(
        functools.partial(_proj0_kernel, rope=rope),
        out_shape=(_sds((b, n, SWA_Q_W), BF16), _sds((b, SWA_KV_HEADS, n, LANES), BF16),
                   _sds((b, SWA_KV_HEADS, n, LANES), BF16), _sds((b, n, SSM_WIDTH), F32)),
        grid=(b, n // tm),
        in_specs=in_specs,
        out_specs=(pl.BlockSpec((1, tm, SWA_Q_W), lambda i, j: (i, j, 0)), kv_spec, kv_spec,
                   pl.BlockSpec((1, tm, SSM_WIDTH), lambda i, j: (i, j, 0))),
        compiler_params=_cparams("parallel", "parallel"),
        name="proj0_rope" if rope else "proj0_ctx",
    )(*args)


def _swa_kernel(*refs, tq, nsub, local, n_lat):
    if local:
        sink_ref, q_ref, k_ref, v_ref, kc_ref, vc_ref, o_ref = refs
    else:
        sink_ref, q_ref, kc_ref, vc_ref, o_ref = refs
    rows = 4 * tq
    lo = lax.broadcasted_iota(jnp.int32, (tq, LANES), 1) < HALF
    lo4 = lax.broadcasted_iota(jnp.int32, (rows, LANES), 1) < HALF
    rown = lax.broadcasted_iota(jnp.int32, (rows, 1), 0)
    for sub in range(nsub):
        j = pl.program_id(1) * nsub + sub
        r0 = sub * tq
        if local:
            span = 3 * SWA_BLOCK
            start = pl.multiple_of(jnp.clip((j - 1) * SWA_BLOCK, 0, n_lat - span), SWA_BLOCK)
            rr = lax.broadcasted_iota(jnp.int32, (rows, span), 0)
            cc = lax.broadcasted_iota(jnp.int32, (rows, span), 1)
            qpos = j * tq + (rr & (tq - 1))
            mask = jnp.abs(qpos - (start + cc)) <= SWA_WINDOW
        for h in range(SWA_KV_HEADS):
            qa = q_ref[0, r0:r0 + tq, (2 * h) * LANES:(2 * h + 1) * LANES].astype(F32)
            qb = q_ref[0, r0:r0 + tq, (2 * h + 1) * LANES:(2 * h + 2) * LANES].astype(F32)
            q4 = jnp.concatenate([jnp.where(lo, qa, 0.0), jnp.where(lo, 0.0, qa),
                                  jnp.where(lo, qb, 0.0), jnp.where(lo, 0.0, qb)], 0).astype(BF16)
            sink = LOG2_E * jnp.where(rown < tq, sink_ref[4 * h],
                                      jnp.where(rown < 2 * tq, sink_ref[4 * h + 1],
                                                jnp.where(rown < 3 * tq, sink_ref[4 * h + 2], sink_ref[4 * h + 3])))
            s_ctx = _nt_dot(q4, kc_ref[0, h])
            m = jnp.maximum(jnp.max(s_ctx, -1, keepdims=True), sink)
            if local:
                s_loc = jnp.where(mask, _nt_dot(q4, k_ref[0, h, pl.ds(start, span), :]), NEG_INF)
                m = jnp.maximum(m, jnp.max(s_loc, -1, keepdims=True))
            o4 = jnp.dot(jnp.exp2(s_ctx - m).astype(BF16), vc_ref[0, h], preferred_element_type=F32)
            if local:
                o4 = o4 + jnp.dot(jnp.exp2(s_loc - m).astype(BF16), v_ref[0, h, pl.ds(start, span), :],
                                  preferred_element_type=F32)
            o4 = o4 + jnp.where(lo4, 0.0, jnp.exp2(sink - m))
            o4 = o4 * (1.0 / jnp.where(lo4, pltpu.roll(o4, HALF, 1), 1.0))
            for s in range(2):
                even, odd = o4[2 * s * tq:(2 * s + 1) * tq], o4[(2 * s + 1) * tq:(2 * s + 2) * tq]
                o_ref[0, r0:r0 + tq, (2 * h + s) * LANES:(2 * h + s + 1) * LANES] = jnp.where(
                    lo, even, pltpu.roll(odd, HALF, 1)).astype(BF16)


def _swa_attention(sink, q, k, v, kc, vc):
    b, n, _ = q.shape
    nc = kc.shape[2]
    tq, nsub = SWA_BLOCK, SWA_BLOCKS_PER_STEP
    full = lambda m: pl.BlockSpec((1, SWA_KV_HEADS, m, LANES), lambda i, j: (i, 0, 0, 0))
    return pl.pallas_call(
        functools.partial(_swa_kernel, tq=tq, nsub=nsub, local=True, n_lat=n),
        out_shape=_sds((b, n, SWA_Q_W), BF16),
        grid=(b, n // (tq * nsub)),
        in_specs=[pl.BlockSpec(memory_space=pltpu.SMEM),
                  pl.BlockSpec((1, tq * nsub, SWA_Q_W), lambda i, j: (i, j, 0)),
                  full(n), full(n), full(nc), full(nc)],
        out_specs=pl.BlockSpec((1, tq * nsub, SWA_Q_W), lambda i, j: (i, j, 0)),
        compiler_params=_cparams("parallel", "arbitrary"),
        name="swa_attention",
    )(sink, q, k, v, kc, vc)


def _ctx_attention(sink, qc, kc, vc):
    b, nc, _ = qc.shape
    full = pl.BlockSpec((1, SWA_KV_HEADS, nc, LANES), lambda i, j: (i, 0, 0, 0))
    return pl.pallas_call(
        functools.partial(_swa_kernel, tq=nc, nsub=1, local=False, n_lat=0),
        out_shape=_sds((b, nc, SWA_Q_W), BF16),
        grid=(b, 1),
        in_specs=[pl.BlockSpec(memory_space=pltpu.SMEM),
                  pl.BlockSpec((1, nc, SWA_Q_W), lambda i, j: (i, 0, 0)), full, full],
        out_specs=pl.BlockSpec((1, nc, SWA_Q_W), lambda i, j: (i, 0, 0)),
        compiler_params=_cparams("parallel", "arbitrary"),
        name="ctx_attention",
    )(sink, qc, kc, vc)


def _cpow(e, lr, li):
    mag = jnp.exp(e * lr)
    return mag * jnp.cos(e * li), mag * jnp.sin(e * li)


def _s5_prep_kernel(row_ref, col_ref, bt_ref, ct_ref, ctt_ref, min_ref, mintra_ref, mst_ref, al_ref):
    L, C = SSM_CHUNK, SSM_GROUP
    ar, ai, dt = row_ref[0, 0:1, :], row_ref[0, 1:2, :], row_ref[0, 2:3, :]
    lr, li = dt * ar, dt * ai
    a1_re, a1_im = _cpow(1.0, lr, li)
    den = ar * ar + ai * ai
    nr = a1_re - 1.0
    coef_re = (nr * ar + a1_im * ai) / den
    coef_im = (a1_im * ar - nr * ai) / den
    b_re, b_im = bt_ref[0, 0], bt_ref[0, 1]
    bb_re = coef_re * b_re - coef_im * b_im
    bb_im = coef_re * b_im + coef_im * b_re
    c_re, c_im = ct_ref[0, 0], ct_ref[0, 1]
    fwd = lax.broadcasted_iota(jnp.int32, (L, LANES), 1) < HALF
    step = lax.broadcasted_iota(jnp.int32, (L, LANES), 0).astype(F32)
    pin_re, pin_im = _cpow(jnp.where(fwd, (L - 1.0) - step, step), lr, li)
    pst_re, pst_im = _cpow(jnp.where(fwd, step + 1.0, float(L) - step), lr, li)
    for j in range(L):
        pr, pi = pin_re[j:j + 1], pin_im[j:j + 1]
        min_ref[0, j * C:(j + 1) * C, 0:LANES] = (pr * bb_re - pi * bb_im).astype(BF16)
        min_ref[0, j * C:(j + 1) * C, LANES:] = (pr * bb_im + pi * bb_re).astype(BF16)
        pr, pi = pst_re[j:j + 1], pst_im[j:j + 1]
        mst_ref[0, j * C:(j + 1) * C, 0:LANES] = (pr * c_re - pi * c_im).astype(BF16)
        mst_ref[0, j * C:(j + 1) * C, LANES:] = (-(pr * c_im + pi * c_re)).astype(BF16)
    al_re, al_im = _cpow(float(L), lr, li)
    al_ref[0, 0:1, :] = al_re
    al_ref[0, 1:2, :] = al_im
    lag = (lax.broadcasted_iota(jnp.int32, (SSM_STATE, SSM_CW), 1) // C).astype(F32)
    rep = (lax.broadcasted_iota(jnp.int32, (C, SSM_CW), 1) % C
           == lax.broadcasted_iota(jnp.int32, (C, SSM_CW), 0)).astype(F32)
    ks = []
    for d in range(2):
        col = col_ref[0, d]
        p_re, p_im = _cpow(lag if d == 0 else (L - 1.0) - lag, col[:, 2:3] * col[:, 0:1], col[:, 2:3] * col[:, 1:2])
        ct_re = jnp.dot(ctt_ref[0, d, 0], rep, preferred_element_type=F32, precision=HIGHEST)
        ct_im = jnp.dot(ctt_ref[0, d, 1], rep, preferred_element_type=F32, precision=HIGHEST)
        ca_re = ct_re * p_re - ct_im * p_im
        ca_im = ct_re * p_im + ct_im * p_re
        sl = slice(d * HALF, (d + 1) * HALF)
        ks.append(jnp.dot(bb_re[:, sl], ca_re, preferred_element_type=F32, precision=HIGHEST)
                  - jnp.dot(bb_im[:, sl], ca_im, preferred_element_type=F32, precision=HIGHEST))
    zeros = jnp.zeros((C, SSM_CW), F32)
    lagged = (pltpu.roll(jnp.concatenate([ks[0], zeros], axis=1), (L - 1) * C, 1)
              + jnp.concatenate([ks[1], zeros], axis=1))
    for j in range(L):
        shift = (L - 1 - j) * C
        win = lagged if shift == 0 else pltpu.roll(lagged, 2 * SSM_CW - shift, 1)
        mintra_ref[0, j * C:(j + 1) * C, :] = win[:, 0:SSM_CW].astype(BF16)


def _ssm_matrices(a_re, a_im, log_step, b_re, b_im, c_re, c_im):
    G, P, C = SSM_GROUPS, SSM_STATE, SSM_GROUP
    f = lambda t: t.astype(F32)
    dt = jnp.exp(f(log_step))
    dtp = jnp.broadcast_to(dt[..., None], (2, G, P))
    row = jnp.stack([f(a_re), f(a_im), dtp], 0)
    row = jnp.transpose(row, (2, 0, 1, 3)).reshape(G, 3, 2 * P)
    col = jnp.transpose(jnp.stack([f(a_re), f(a_im), dtp], -1), (1, 0, 2, 3))
    lanes = lambda t: jnp.transpose(t, (1, 2, 0, 3)).reshape(G, C, 2 * P)
    bt = jnp.stack([lanes(jnp.transpose(f(b_re), (0, 1, 3, 2))), lanes(jnp.transpose(f(b_im), (0, 1, 3, 2)))], 1)
    ct = jnp.stack([lanes(f(c_re)), lanes(f(c_im))], 1)
    ctt = jnp.stack([jnp.transpose(f(c_re), (1, 0, 3, 2)), jnp.transpose(f(c_im), (1, 0, 3, 2))], 2)
    mat = pl.BlockSpec((1, SSM_CW, SSM_CW), lambda g: (g, 0, 0))
    return pl.pallas_call(
        _s5_prep_kernel,
        out_shape=(_sds((G, SSM_CW, SSM_CW), BF16),) * 3 + (_sds((G, 2, LANES), F32),),
        grid=(G,),
        in_specs=[pl.BlockSpec((1, 3, LANES), lambda g: (g, 0, 0)),
                  pl.BlockSpec((1, 2, P, 3), lambda g: (g, 0, 0, 0)),
                  pl.BlockSpec((1, 2, C, LANES), lambda g: (g, 0, 0, 0)),
                  pl.BlockSpec((1, 2, C, LANES), lambda g: (g, 0, 0, 0)),
                  pl.BlockSpec((1, 2, 2, P, C), lambda g: (g, 0, 0, 0, 0))],
        out_specs=(mat, mat, mat, pl.BlockSpec((1, 2, LANES), lambda g: (g, 0, 0))),
        compiler_params=_cparams("parallel"),
        name="s5_prep",
    )(row, col, bt, ct, ctt)


def _ssm_pack_kernel(uc_ref, u_ref, x_ref, *, nc_ctx, nc_lat):
    for src, row0, nch in ((uc_ref, 0, nc_ctx), (u_ref, nc_ctx, nc_lat)):
        steps = [src[0, pl.ds(j, nch, stride=SSM_CHUNK), :] for j in range(SSM_CHUNK)]
        lane = lax.broadcasted_iota(jnp.int32, (nch, LANES), 1)
        piece = [(lane >= jj * SSM_GROUP) & (lane < (jj + 1) * SSM_GROUP) for jj in range(SSM_SLAB_GROUPS)]
        for g in range(SSM_SLAB_GROUPS):
            for h in range(SSM_CW // LANES):
                acc = jnp.zeros((nch, LANES), F32)
                for jj in range(SSM_SLAB_GROUPS):
                    z = steps[h * SSM_SLAB_GROUPS + jj]
                    shift = ((jj - g) * SSM_GROUP) % LANES
                    acc = jnp.where(piece[jj], z if shift == 0 else pltpu.roll(z, shift, 1), acc)
                x_ref[g, row0:row0 + nch, h * LANES:(h + 1) * LANES] = acc.astype(BF16)


def _ssm_unpack_kernel(y_ref, oc_ref, o_ref, *, nc_ctx, nc_lat):
    for dst, row0, nch in ((oc_ref, 0, nc_ctx), (o_ref, nc_ctx, nc_lat)):
        lane = lax.broadcasted_iota(jnp.int32, (nch, LANES), 1)
        piece = [(lane >= g * SSM_GROUP) & (lane < (g + 1) * SSM_GROUP) for g in range(SSM_SLAB_GROUPS)]
        for i in range(SSM_CHUNK):
            h, ii = divmod(i, SSM_SLAB_GROUPS)
            acc = jnp.zeros((nch, LANES), F32)
            for g in range(SSM_SLAB_GROUPS):
                z = y_ref[g, row0:row0 + nch, h * LANES:(h + 1) * LANES]
                shift = ((g - ii) * SSM_GROUP) % LANES
                acc = jnp.where(piece[g], z if shift == 0 else pltpu.roll(z, shift, 1), acc)
            dst[0, pl.ds(i, nch, stride=SSM_CHUNK), :] = acc


def _ssm_kernel(x_ref, min_ref, mintra_ref, mstate_ref, al_ref, y_ref, v_scr, s_scr, *, nb, nc_ctx, n_chunks):
    n_grp = x_ref.shape[0]
    for g in range(n_grp):
        v = jnp.dot(x_ref[g], min_ref[g], preferred_element_type=F32)
        v_scr[g, 0] = v[:, 0:LANES]
        v_scr[g, 1] = v[:, LANES:]
    ar = [al_ref[g, 0:1, :] for g in range(n_grp)]
    ai = [al_ref[g, 1:2, :] for g in range(n_grp)]
    lo = lax.broadcasted_iota(jnp.int32, (nb, LANES), 1) < HALF

    def body(k, carry):
        kr = jnp.where(k < nc_ctx, nc_ctx - 1 - k, n_chunks - 1 + nc_ctx - k)
        rf = pl.ds(k, nb, stride=n_chunks)
        rr = pl.ds(kr, nb, stride=n_chunks)
        out = []
        for g in range(n_grp):
            sre, sim = carry[2 * g], carry[2 * g + 1]
            s_scr[g, 0, rf, :] = sre
            s_scr[g, 1, rr, :] = sre
            s_scr[g, 2, rf, :] = sim
            s_scr[g, 3, rr, :] = sim
            vre = jnp.where(lo, v_scr[g, 0, rf, :], v_scr[g, 0, rr, :])
            vim = jnp.where(lo, v_scr[g, 1, rf, :], v_scr[g, 1, rr, :])
            out += [ar[g] * sre - ai[g] * sim + vre, ar[g] * sim + ai[g] * sre + vim]
        return tuple(out)

    zero = jnp.zeros((nb, LANES), F32)
    lax.fori_loop(0, n_chunks, body, (zero,) * (2 * n_grp))
    lo_all = lax.broadcasted_iota(jnp.int32, (s_scr.shape[2], LANES), 1) < HALF
    for g in range(n_grp):
        s_in = jnp.concatenate([jnp.where(lo_all, s_scr[g, 0], s_scr[g, 1]),
                                jnp.where(lo_all, s_scr[g, 2], s_scr[g, 3])], axis=1).astype(BF16)
        y_ref[g] = jnp.dot(x_ref[g], mintra_ref[g], preferred_element_type=F32) + _nt_dot(s_in, mstate_ref[g])


def _ssm_scan(u, uc, mats):
    m_in, m_intra, m_state, a_l = mats
    b, n, _ = u.shape
    nc = uc.shape[1]
    nc_ctx, nc_lat = nc // SSM_CHUNK, n // SSM_CHUNK
    n_chunks = nc_ctx + nc_lat
    r = n_chunks * b
    nat = lambda m: pl.BlockSpec((1, m, LANES), lambda i, s: (i, 0, s))
    grp = pl.BlockSpec((SSM_SLAB_GROUPS, n_chunks, SSM_CW), lambda i, s: (s, i, 0))
    xg = pl.pallas_call(
        functools.partial(_ssm_pack_kernel, nc_ctx=nc_ctx, nc_lat=nc_lat),
        out_shape=_sds((SSM_GROUPS, r, SSM_CW), BF16),
        grid=(b, SSM_GROUPS // SSM_SLAB_GROUPS),
        in_specs=[nat(nc), nat(n)],
        out_specs=grp,
        compiler_params=_cparams("parallel", "parallel"),
        name="ssm_pack",
    )(uc, u)
    gs = SSM_GROUPS_PER_STEP
    mat = pl.BlockSpec((gs, SSM_CW, SSM_CW), lambda g: (g, 0, 0))
    yg = pl.pallas_call(
        functools.partial(_ssm_kernel, nb=b, nc_ctx=nc_ctx, n_chunks=n_chunks),
        out_shape=_sds((SSM_GROUPS, r, SSM_CW), F32),
        grid=(SSM_GROUPS // gs,),
        in_specs=[pl.BlockSpec((gs, r, SSM_CW), lambda g: (g, 0, 0)), mat, mat, mat,
                  pl.BlockSpec((gs, 2, LANES), lambda g: (g, 0, 0))],
        out_specs=pl.BlockSpec((gs, r, SSM_CW), lambda g: (g, 0, 0)),
        scratch_shapes=[pltpu.VMEM((gs, 2, r, LANES), F32), pltpu.VMEM((gs, 4, r, LANES), F32)],
        compiler_params=_cparams("parallel"),
        name="ssm_scan",
    )(xg, m_in, m_intra, m_state, a_l)
    ysc, ys = pl.pallas_call(
        functools.partial(_ssm_unpack_kernel, nc_ctx=nc_ctx, nc_lat=nc_lat),
        out_shape=(_sds((b, nc, SSM_WIDTH), F32), _sds((b, n, SSM_WIDTH), F32)),
        grid=(b, SSM_GROUPS // SSM_SLAB_GROUPS),
        in_specs=[grp],
        out_specs=(nat(nc), nat(n)),
        compiler_params=_cparams("parallel", "parallel"),
        name="ssm_unpack",
    )(yg)
    return ys, ysc


def _split_rows(ref, val):
    m = val.shape[0]
    for s in range(TOKEN_TILE_ROWS):
        ref[pl.ds(s, m, stride=TOKEN_TILE_ROWS), :] = val[:, s * LANES:(s + 1) * LANES]


def _merge_rows(ref):
    m = ref.shape[0] // TOKEN_TILE_ROWS
    return jnp.concatenate([ref[pl.ds(s, m, stride=TOKEN_TILE_ROWS), :] for s in range(TOKEN_TILE_ROWS)], axis=1)


def _token_tile(ref, t):
    return ref.at[pl.ds(pl.multiple_of(t * TOKEN_TILE_ROWS, TOKEN_TILE_ROWS), TOKEN_TILE_ROWS)]


def _router_logits(w, bias, h):
    w_hi, h_hi = w.astype(BF16), h.astype(BF16)
    w_lo, h_lo = (w - w_hi.astype(F32)).astype(BF16), (h - h_hi.astype(F32)).astype(BF16)
    return _nt_dot(w_hi, h_hi) + (_nt_dot(w_hi, h_lo) + _nt_dot(w_lo, h_hi)) + bias


def _post_kernel(*refs, alpha, with_ssm):
    wr_ref, br_ref = refs[-5:-3]
    refs = refs[:-5] + refs[-3:]
    if with_ssm:
        (att_ref, ys_ref, u_ref, dsk_ref, gw_ref, gb_ref, wo_ref, x_ref, g1_ref, lg_ref, lb_ref, sc2_ref, sh2_ref,
         x1_ref, h2_ref, lgt_ref) = refs
        y = ys_ref[0] + u_ref[0] * dsk_ref[...]
        gl = jax.nn.gelu(y)
        gate = jax.nn.sigmoid(jnp.dot(gl.astype(BF16), gw_ref[...], preferred_element_type=F32) + gb_ref[...])
        ssm = (gl * gate).astype(BF16)
        mix = (jnp.dot(att_ref[0], wo_ref[0:SWA_Q_W, :], preferred_element_type=F32)
               + jnp.dot(ssm, wo_ref[SWA_Q_W:, :], preferred_element_type=F32))
    else:
        att_ref, wo_ref, x_ref, g1_ref, lg_ref, lb_ref, sc2_ref, sh2_ref, x1_ref, h2_ref, lgt_ref = refs
        mix = jnp.dot(att_ref[0], wo_ref[...], preferred_element_type=F32)
    x1 = _layer_norm(alpha * x_ref[0] + g1_ref[0] * mix, lg_ref[...], lb_ref[...])
    x1_ref[0] = x1
    h2 = x1 * (1.0 + sc2_ref[0]) + sh2_ref[0]
    _split_rows(h2_ref, h2)
    lgt_ref[...] = _router_logits(wr_ref[...], br_ref[...], h2)


def _post(att, ssm_args, w_out_bf16, x, g1, ln_g, ln_b, sc2, sh2, router_wb, alpha, tm):
    b, n, d = x.shape
    tok = lambda w: pl.BlockSpec((1, tm, w), lambda i, j: (i, j, 0))
    const = lambda a: pl.BlockSpec(a.shape, lambda i, j: (0,) * a.ndim)
    in_specs = [tok(att.shape[-1])]
    args = [att]
    if ssm_args is not None:
        ys, u, dsk, gw, gb = ssm_args
        in_specs += [tok(SSM_WIDTH), tok(SSM_WIDTH), const(dsk), const(gw), const(gb)]
        args += [ys, u, dsk, gw, gb]
    in_specs += [const(w_out_bf16), tok(d), g1.spec(), const(ln_g), const(ln_b), sc2.spec(), sh2.spec()]
    args += [w_out_bf16, x, g1.table, ln_g, ln_b, sc2.table, sh2.table]
    in_specs += [const(a) for a in router_wb]
    args += list(router_wb)
    per_b = n // tm
    n_logit = router_wb[0].shape[0]
    return pl.pallas_call(
        functools.partial(_post_kernel, alpha=alpha, with_ssm=ssm_args is not None),
        out_shape=(_sds((b, n, d), F32), _sds((b * n * TOKEN_TILE_ROWS, LANES), F32), _sds((n_logit, b * n), F32)),
        grid=(b, n // tm),
        in_specs=in_specs,
        out_specs=(tok(d), pl.BlockSpec((tm * TOKEN_TILE_ROWS, LANES), lambda i, j: (i * per_b + j, 0)),
                   pl.BlockSpec((n_logit, tm), lambda i, j: (0, i * per_b + j))),
        compiler_params=_cparams("parallel", "parallel"),
        name="post_mixer_ssm" if ssm_args is not None else "post_mixer",
    )(*args)


def _first_max(v, sub):
    m = jnp.max(v, 0, keepdims=True)
    idx = jnp.min(jnp.where(v == m, sub, float(SUBLANES)), 0, keepdims=True)
    return m, idx


def _stream_blocks(toks, tb):
    starts = [0]
    for t in toks:
        starts.append(starts[-1] + t.shape[0] // (tb * TOKEN_TILE_ROWS))
    return starts


def _stream_spec(tok, tb, start):
    last = tok.shape[0] // (tb * TOKEN_TILE_ROWS) - 1
    return pl.BlockSpec((tb * TOKEN_TILE_ROWS, LANES), lambda i, *_: (jnp.clip(i - start, 0, last), 0))


def _router_kernel(*refs, starts):
    n_streams = len(starts) - 1
    lgt_refs = refs[:n_streams]
    ids_ref, wts_ref, rank_ref, cnt_ref, carry_scr = refs[n_streams:]
    step = pl.program_id(0)

    @pl.when(step == 0)
    def _():
        carry_scr[...] = jnp.zeros_like(carry_scr)

    logits = lgt_refs[0][...]
    for ref, start in zip(lgt_refs[1:], starts[1:]):
        logits = jnp.where(step >= start, ref[...], logits)
    tm = logits.shape[1]
    sub = lax.broadcasted_iota(jnp.int32, (SUBLANES, tm), 0).astype(F32)
    gl = logits[0:SUBLANES]
    gmax, gi = _first_max(gl, sub)
    gp = 1.0 / jnp.sum(jnp.exp(gl - gmax), 0, keepdims=True)
    le = logits[ROUTER_EXPERT_ROW0:ROUTER_EXPERT_ROW0 + MOE_EPG]
    for g in range(1, MOE_GROUPS):
        le = jnp.where(gi == float(g), logits[ROUTER_EXPERT_ROW0 + g * MOE_EPG:ROUTER_EXPERT_ROW0 + (g + 1) * MOE_EPG], le)
    m1, i1 = _first_max(le, sub)
    m2, i2 = _first_max(jnp.where(sub == i1, NEG_INF, le), sub)
    t = jnp.exp(m2 - m1)
    e1 = gi * float(MOE_EPG) + i1
    e2 = gi * float(MOE_EPG) + i2
    ids_ref[0:1, :] = e1.astype(jnp.int32)
    ids_ref[1:2, :] = e2.astype(jnp.int32)
    wts_ref[0:1, :] = gp / (1.0 + t)
    wts_ref[1:2, :] = gp * t / (1.0 + t)
    esub = lax.broadcasted_iota(jnp.int32, (MOE_EXPERTS, tm), 0).astype(F32)
    oh1 = (esub == e1).astype(F32)
    oh2 = (esub == e2).astype(F32)
    both = oh1 + oh2
    earlier = (lax.broadcasted_iota(jnp.int32, (tm, tm), 0) < lax.broadcasted_iota(jnp.int32, (tm, tm), 1))
    prefix = jnp.dot(both.astype(BF16), earlier.astype(BF16), preferred_element_type=F32) + carry_scr[...]
    rank_ref[0:1, :] = jnp.sum(oh1 * prefix, 0, keepdims=True).astype(jnp.int32)
    rank_ref[1:2, :] = jnp.sum(oh2 * prefix, 0, keepdims=True).astype(jnp.int32)
    carry_scr[...] += jnp.sum(both, 1, keepdims=True)
    cnt_ref[...] = jnp.broadcast_to(carry_scr[...], cnt_ref.shape)


def _router_weights(wg, bg, we, be):
    d = wg.shape[0]
    rows = ROUTER_EXPERT_ROW0 + MOE_EXPERTS
    w = jnp.zeros((rows, d), F32)
    w = w.at[:MOE_GROUPS].set(wg.T)
    w = w.at[ROUTER_EXPERT_ROW0:].set(jnp.transpose(we, (0, 2, 1)).reshape(MOE_EXPERTS, d))
    bias = jnp.full((rows, 1), NEG_INF, F32)
    bias = bias.at[:MOE_GROUPS, 0].set(bg)
    bias = bias.at[ROUTER_EXPERT_ROW0:, 0].set(be.reshape(-1))
    return w, bias


def _router(logit_streams, tm):
    starts = [0]
    for lg in logit_streams:
        starts.append(starts[-1] + lg.shape[1] // tm)
    t = starts[-1] * tm
    rows = logit_streams[0].shape[0]
    stream_spec = lambda lg, s0: pl.BlockSpec(
        (rows, tm), lambda i: (0, jnp.clip(i - s0, 0, lg.shape[1] // tm - 1)))
    pair = pl.BlockSpec((MOE_TOPK, tm), lambda i: (0, i))
    ids, wts, rank, cnt = pl.pallas_call(
        functools.partial(_router_kernel, starts=starts),
        out_shape=(_sds((MOE_TOPK, t), jnp.int32), _sds((MOE_TOPK, t), F32), _sds((MOE_TOPK, t), jnp.int32),
                   _sds((MOE_EXPERTS, LANES), F32)),
        grid=(t // tm,),
        in_specs=[stream_spec(lg, s0) for lg, s0 in zip(logit_streams, starts)],
        out_specs=(pair, pair, pair, pl.BlockSpec((MOE_EXPERTS, LANES), lambda i: (0, 0))),
        scratch_shapes=[pltpu.VMEM((MOE_EXPERTS, 1), F32)],
        compiler_params=_cparams("arbitrary"),
        name="moe_router",
    )(*logit_streams)
    return ids, wts, rank, cnt[:, 0].astype(jnp.int32)


def _moe_plan(ids, rank, counts, n_tok):
    tm = MOE_ROW_TILE
    padded = ((counts + tm - 1) // tm) * tm
    ends = jnp.cumsum(padded)
    offs = ends - padded
    experts = jnp.arange(MOE_EXPERTS, dtype=jnp.int32)
    dest = (jnp.sum(jnp.where(ids[..., None] == experts, offs, 0), -1) + rank).astype(jnp.int32)
    n_tiles = (MOE_TOPK * n_tok + MOE_EXPERTS * (tm - 1)) // tm
    starts = jnp.arange(n_tiles, dtype=jnp.int32) * tm
    tile_expert = jnp.minimum(jnp.sum((ends[None, :] <= starts[:, None]).astype(jnp.int32), -1), MOE_EXPERTS - 1)
    n_valid = (ends[-1] // tm).astype(jnp.int32).reshape(1)
    pad = jnp.stack([offs + counts, padded - counts, jnp.broadcast_to(n_valid, counts.shape)]).astype(jnp.int32)
    return dest, tile_expert, n_valid, pad, n_tiles * tm


def _pad_fill(pad_ref, zero_scr, xs_ref, sem, wait):
    def per_expert(e, carry):
        first, count = pad_ref[0, e], pad_ref[1, e]
        piece = MOE_ROW_TILE // 2
        while piece >= 1:
            row = first + (count & ~(2 * piece - 1))
            n = piece * TOKEN_TILE_ROWS

            @pl.when((count & piece) != 0)
            def _(row=row, n=n):
                copy = pltpu.make_async_copy(
                    zero_scr.at[pl.ds(0, n)],
                    xs_ref.at[pl.ds(pl.multiple_of(row * TOKEN_TILE_ROWS, TOKEN_TILE_ROWS), n)], sem)
                copy.wait() if wait else copy.start()

            piece //= 2
        return carry

    lax.fori_loop(0, MOE_EXPERTS, per_expert, 0)
    half = MOE_ROW_TILE // 2 * TOKEN_TILE_ROWS
    n_tiles = xs_ref.shape[0] // (2 * half)

    def per_tile(t, carry):
        for h in range(2):
            copy = pltpu.make_async_copy(zero_scr, xs_ref.at[pl.ds(pl.multiple_of((2 * t + h) * half, half), half)],
                                         sem)
            copy.wait() if wait else copy.start()
        return carry

    lax.fori_loop(pad_ref[2, 0], n_tiles, per_tile, 0)


def _dispatch_kernel(*refs, tb, starts):
    n_streams = len(starts) - 1
    dest_ref, pad_ref = refs[:2]
    tok_refs = refs[2:2 + n_streams]
    xs_ref, zero_scr, sem, pad_sem = refs[2 + n_streams:]
    step = pl.program_id(0)

    @pl.when(step == 0)
    def _():
        zero_scr[...] = jnp.zeros_like(zero_scr)
        _pad_fill(pad_ref, zero_scr, xs_ref, pad_sem, wait=False)

    for s, tok_ref in enumerate(tok_refs):
        @pl.when((step >= starts[s]) & (step < starts[s + 1]))
        def _(tok_ref=tok_ref):
            def body(r, carry):
                for k in range(MOE_TOPK):
                    pltpu.make_async_copy(_token_tile(tok_ref, r), _token_tile(xs_ref, dest_ref[MOE_TOPK * r + k]),
                                          sem).start(priority=k % 2)
                return carry

            lax.fori_loop(0, tb, body, 0, unroll=8)

    for k in range(MOE_TOPK):
        pltpu.make_async_copy(tok_refs[0], xs_ref.at[pl.ds(0, tb * TOKEN_TILE_ROWS)], sem).wait()

    @pl.when(step == 0)
    def _():
        _pad_fill(pad_ref, zero_scr, xs_ref, pad_sem, wait=True)


def _dispatch(toks, dest, pad, n_rows):
    tb = MOE_DMA_TILE
    nblk = dest.shape[0] // (MOE_TOPK * tb)
    starts = _stream_blocks(toks, tb)
    return pl.pallas_call(
        functools.partial(_dispatch_kernel, tb=tb, starts=starts),
        out_shape=_sds((n_rows * TOKEN_TILE_ROWS, LANES), F32),
        grid=(nblk,),
        in_specs=[pl.BlockSpec((MOE_TOPK * tb,), lambda i: (i,), memory_space=pltpu.SMEM),
                  pl.BlockSpec(memory_space=pltpu.SMEM)]
        + [_stream_spec(tok, tb, s0) for tok, s0 in zip(toks, starts)],
        out_specs=pl.BlockSpec(memory_space=pl.ANY),
        scratch_shapes=[pltpu.VMEM((MOE_ROW_TILE // 2 * TOKEN_TILE_ROWS, LANES), F32),
                        pltpu.SemaphoreType.DMA(()), pltpu.SemaphoreType.DMA(())],
        compiler_params=_cparams("arbitrary"),
        name="moe_dispatch",
    )(dest, pad, *toks)


def _ffn_kernel(te_ref, nv_ref, x_ref, w1_ref, w3_ref, w2_ref, y_ref, w13_scr, w2_scr):
    i = pl.program_id(0)
    f = w1_ref.shape[2]

    @pl.when((i == 0) | (te_ref[i] != te_ref[jnp.maximum(i - 1, 0)]))
    def _():
        w13_scr[:, 0:f] = w1_ref[0].astype(BF16)
        w13_scr[:, f:2 * f] = w3_ref[0].astype(BF16)
        w2_scr[...] = w2_ref[0].astype(BF16)

    @pl.when(i < nv_ref[0])
    def _():
        h13 = jnp.dot(_merge_rows(x_ref).astype(BF16), w13_scr[...], preferred_element_type=F32)
        h1 = h13[:, 0:f]
        hh = (h1 * jax.nn.sigmoid(h1) * h13[:, f:2 * f]).astype(BF16)
        _split_rows(y_ref, jnp.dot(hh, w2_scr[...], preferred_element_type=F32))

    @pl.when(i >= nv_ref[0])
    def _():
        y_ref[...] = jnp.zeros_like(y_ref)


def _expert_ffn(tile_expert, n_valid, xs, w1, w3, w2):
    p = xs.shape[0] // TOKEN_TILE_ROWS
    _, d, f = w1.shape
    tm = MOE_ROW_TILE
    rows = pl.BlockSpec((tm * TOKEN_TILE_ROWS, LANES), lambda i, te, nv: (i, 0))
    rows_in = pl.BlockSpec((tm * TOKEN_TILE_ROWS, LANES), lambda i, te, nv: (jnp.minimum(i, nv[0] - 1), 0))
    return pl.pallas_call(
        _ffn_kernel,
        out_shape=_sds(xs.shape, F32),
        grid_spec=pltpu.PrefetchScalarGridSpec(
            num_scalar_prefetch=2,
            grid=(p // tm,),
            in_specs=[rows_in,
                      pl.BlockSpec((1, d, f), lambda i, te, nv: (te[i], 0, 0)),
                      pl.BlockSpec((1, d, f), lambda i, te, nv: (te[i], 0, 0)),
                      pl.BlockSpec((1, f, d), lambda i, te, nv: (te[i], 0, 0))],
            out_specs=rows,
            scratch_shapes=[pltpu.VMEM((d, 2 * f), BF16), pltpu.VMEM((f, d), BF16)]),
        compiler_params=_cparams("arbitrary"),
        name="moe_expert_ffn",
    )(tile_expert, n_valid, xs, w1, w3, w2)


def _combine_ln2_kernel(dest_ref, next_ref, x1_ref, ys_ref, wt_ref, g2_ref, lg_ref, lb_ref, o_ref, buf, sem, *,
                        alpha, tb, nblk):
    step = pl.program_id(0)

    def gather(d_ref, slot):
        def body(r, carry):
            for k in range(MOE_TOPK):
                pltpu.make_async_copy(_token_tile(ys_ref, d_ref[MOE_TOPK * r + k]), _token_tile(buf.at[slot, k], r),
                                      sem.at[slot, k]).start(priority=k % 2)
            return carry

        lax.fori_loop(0, tb, body, 0, unroll=8)

    @pl.when(step == 0)
    def _():
        gather(dest_ref, 0)

    @pl.when(step + 1 < nblk)
    def _():
        gather(next_ref, (step + 1) % 2)

    slot = step % 2
    for k in range(MOE_TOPK):
        pltpu.make_async_copy(ys_ref.at[pl.ds(0, tb * TOKEN_TILE_ROWS)], buf.at[slot, k], sem.at[slot, k]).wait()
    f = wt_ref[:, 0:1] * _merge_rows(buf.at[slot, 0]) + wt_ref[:, 1:2] * _merge_rows(buf.at[slot, 1])
    o_ref[...] = _layer_norm(alpha * x1_ref[...] + g2_ref[0] * f, lg_ref[...], lb_ref[...])


def _combine_ln2(dest, x1, ys, wt, g2, ln_g, ln_b, alpha, n_per_sample):
    t, d = x1.shape
    tb = MOE_DMA_TILE
    nblk = t // tb
    const = pl.BlockSpec((1, d), lambda i: (0, 0))
    assert g2.stride == 0 or n_per_sample % tb == 0, (n_per_sample, tb)
    return pl.pallas_call(
        functools.partial(_combine_ln2_kernel, alpha=alpha, tb=tb, nblk=nblk),
        out_shape=_sds((t, d), F32),
        grid=(nblk,),
        in_specs=[pl.BlockSpec((MOE_TOPK * tb,), lambda i: (i,), memory_space=pltpu.SMEM),
                  pl.BlockSpec((MOE_TOPK * tb,), lambda i: (jnp.minimum(i + 1, nblk - 1),), memory_space=pltpu.SMEM),
                  pl.BlockSpec((tb, d), lambda i: (i, 0)),
                  pl.BlockSpec(memory_space=pl.ANY),
                  pl.BlockSpec((tb, MOE_TOPK), lambda i: (i, 0)),
                  g2.spec(lambda i: (i * tb) // n_per_sample),
                  const, const],
        out_specs=pl.BlockSpec((tb, d), lambda i: (i, 0)),
        scratch_shapes=[pltpu.VMEM((2, MOE_TOPK, tb * TOKEN_TILE_ROWS, LANES), F32),
                        pltpu.SemaphoreType.DMA((2, MOE_TOPK))],
        compiler_params=_cparams("arbitrary"),
        name="moe_combine_ln2",
    )(dest, dest, x1, ys, wt, g2.table, ln_g, ln_b)


def _proj1_kernel(*refs, rope, with_q):
    x_ref, sc_ref, sh_ref, w_ref = refs[:4]
    refs = refs[4:]
    tabs = None
    if rope:
        tabs = (refs[0][...], refs[1][...], refs[2][...])
        refs = refs[3:]
    h = (x_ref[0] * (1.0 + sc_ref[0]) + sh_ref[0]).astype(BF16)
    r = jnp.dot(h, w_ref[...], preferred_element_type=F32)
    off = 0
    if with_q:
        q_ref, k_ref, v_ref = refs
        for hd in range(DIF_HEADS):
            q_ref[0, hd] = (_rot(r[:, hd * LANES:(hd + 1) * LANES], tabs) * (LOG2_E * HEAD_DIM ** -0.5)).astype(BF16)
        off = DIF_QK_W
    else:
        k_ref, v_ref = refs
    for hd in range(DIF_HEADS):
        k_ref[0, hd] = _rot(r[:, off + hd * LANES:off + (hd + 1) * LANES], tabs).astype(BF16)
        v_ref[0, hd] = r[:, off + DIF_QK_W + hd * LANES:off + DIF_QK_W + (hd + 1) * LANES].astype(BF16)


def _proj1(x, sc, sh, w_bf16, tabs, with_q, tm):
    b, n, d = x.shape
    rope = tabs is not None
    in_specs = [pl.BlockSpec((1, tm, d), lambda i, j: (i, j, 0)), sc.spec(), sh.spec(),
                pl.BlockSpec(w_bf16.shape, lambda i, j: (0, 0))]
    args = [x, sc.table, sh.table, w_bf16]
    if rope:
        in_specs += [pl.BlockSpec((tm, LANES), lambda i, j: (j, 0))] * 3
        args += list(tabs)
    hm = pl.BlockSpec((1, DIF_HEADS, tm, LANES), lambda i, j: (i, 0, j, 0))
    n_out = 3 if with_q else 2
    return pl.pallas_call(
        functools.partial(_proj1_kernel, rope=rope, with_q=with_q),
        out_shape=(_sds((b, DIF_HEADS, n, LANES), BF16),) * n_out,
        grid=(b, n // tm),
        in_specs=in_specs,
        out_specs=(hm,) * n_out,
        compiler_params=_cparams("parallel", "parallel"),
        name="proj1_qkv" if with_q else "proj1_kv_ctx",
    )(*args)


def _diff_kernel(lam_ref, q_ref, kl_ref, kc_ref, vl_ref, vc_ref, g_ref, o_ref, k_scr, v_scr, *, tq, n_lat, out_scale):
    @pl.when(pl.program_id(2) == 0)
    def _():
        k_scr[0:n_lat] = kl_ref[0, 0]
        k_scr[n_lat:] = kc_ref[0, 0]
        v_scr[0:n_lat, 0:LANES] = vl_ref[0, 0]
        v_scr[n_lat:, 0:LANES] = vc_ref[0, 0]
        v_scr[:, LANES:] = jnp.ones((v_scr.shape[0], LANES), BF16)

    lo = lax.broadcasted_iota(jnp.int32, (DIF_CHAIN_ROWS, LANES), 1) < HALF
    for r0 in range(0, tq, DIF_CHAIN_ROWS):
        q = q_ref[0, 0, r0:r0 + DIF_CHAIN_ROWS, :].astype(F32)
        maps = []
        for qm in (jnp.where(lo, q, 0.0), jnp.where(lo, 0.0, q)):
            s = _nt_dot(qm.astype(BF16), k_scr[...])
            p = jnp.exp2(s - jnp.max(s, -1, keepdims=True)).astype(BF16)
            oe = jnp.dot(p, v_scr[...], preferred_element_type=F32)
            maps.append(oe[:, 0:LANES] * (1.0 / oe[:, LANES:]))
        o = maps[0] - lam_ref[0] * maps[1]
        o = o * lax.rsqrt(jnp.mean(o * o, -1, keepdims=True) + RMS_EPS) * g_ref[...]
        o_ref[0, r0:r0 + DIF_CHAIN_ROWS, :] = (o * out_scale).astype(BF16)


def _diff_attention(lam, q, kl, kc, vl, vc, subln_g, lam_init, tq):
    b, nh, n, _ = q.shape
    nc = kc.shape[2]
    kv = lambda m: pl.BlockSpec((1, 1, m, LANES), lambda i, h, j: (i, h, 0, 0))
    return pl.pallas_call(
        functools.partial(_diff_kernel, tq=tq, n_lat=n, out_scale=1.0 - lam_init),
        out_shape=_sds((b, n, nh * LANES), BF16),
        grid=(b, nh, n // tq),
        in_specs=[pl.BlockSpec(memory_space=pltpu.SMEM),
                  pl.BlockSpec((1, 1, tq, LANES), lambda i, h, j: (i, h, j, 0)),
                  kv(n), kv(nc), kv(n), kv(nc),
                  pl.BlockSpec((1, LANES), lambda i, h, j: (0, 0))],
        out_specs=pl.BlockSpec((1, tq, LANES), lambda i, h, j: (i, j, h)),
        scratch_shapes=[pltpu.VMEM((n + nc, LANES), BF16), pltpu.VMEM((n + nc, 2 * LANES), BF16)],
        compiler_params=_cparams("parallel", "parallel", "arbitrary"),
        name="diff_attention",
    )(lam, q, kl, kc, vl, vc, subln_g)


def _moe_block(toks, logit_streams, layer, moe_w1, moe_w3, moe_w2):
    t = sum(tok.shape[0] for tok in toks) // TOKEN_TILE_ROWS
    tb = MOE_DMA_TILE
    ids, wts, rank, counts = _router(logit_streams, tm=512)
    dest, tile_expert, n_valid, pad, n_rows = _moe_plan(ids, rank, counts, t)
    dest = dest.T.reshape(-1)
    xs = _dispatch(toks, dest, pad, n_rows)
    flat = lambda w: w.reshape((-1,) + w.shape[2:])
    ys = _expert_ffn(tile_expert + layer * MOE_EXPERTS, n_valid, xs, flat(moe_w1), flat(moe_w3), flat(moe_w2))
    return ys, dest, wts.T


def kernel(x, c, ctx, c_ctx, mod_w, mod_b, ln1_g, ln1_b, ln2_g, ln2_b, swa_ssm_w_in, swa_ssm_w_out, swa_sink, ssm_a_re, ssm_a_im, ssm_log_step, ssm_b_re, ssm_b_im, ssm_c_re, ssm_c_im, ssm_d, ssm_glu_w, ssm_glu_b, dif_w_in, dif_w_out, dif_lam_q1, dif_lam_k1, dif_lam_q2, dif_lam_k2, dif_subln_g, moe_wg, moe_bg, moe_we, moe_be, moe_w1, moe_w3, moe_w2):
    bsz, n, d = x.shape
    ctx_len = ctx.shape[1]
    depth = mod_w.shape[0]
    alpha = (2 * depth) ** 0.25
    tabs = _rope_tables(n)

    n_vec = 16
    cvec = jnp.zeros((n_vec, d), F32).at[:bsz].set(c).at[bsz].set(c_ctx)
    mods = _modulation(cvec, mod_w, mod_b)
    mod_table = mods.reshape(depth * n_vec * 6, 1, d)

    xl, xc = x, ctx
    for layer in range(depth):
        need_ctx = layer < depth - 1
        i = layer // 2
        lat = [ModVec(mod_table, (layer * n_vec) * 6 + k, 6) for k in range(6)]
        cx = [ModVec(mod_table, (layer * n_vec + bsz) * 6 + k, 0) for k in range(6)]
        sh1, sc1, g1, sh2, sc2, g2 = lat
        csh1, csc1, cg1, csh2, csc2, cg2 = cx
        lg1, lb1 = ln1_g[layer].reshape(1, d), ln1_b[layer].reshape(1, d)
        lg2, lb2 = ln2_g[layer].reshape(1, d), ln2_b[layer].reshape(1, d)
        router_wb = _router_weights(moe_wg[layer], moe_bg[layer], moe_we[layer], moe_be[layer])
        if layer % 2 == 0:
            w_in = swa_ssm_w_in[i].astype(BF16)
            w_out = swa_ssm_w_out[i].astype(BF16)
            q, k, v, u = _proj0(xl, sc1, sh1, w_in, tabs, tm=512)
            qc, kc, vc, uc = _proj0(xc, csc1, csh1, w_in, None, tm=ctx_len)
            sink = swa_sink[i].astype(F32)
            att = _swa_attention(sink, q, k, v, kc, vc)
            mats = _ssm_matrices(ssm_a_re[i], ssm_a_im[i], ssm_log_step[i], ssm_b_re[i], ssm_b_im[i],
                                 ssm_c_re[i], ssm_c_im[i])
            ys, ysc = _ssm_scan(u, uc, mats)
            glu = (ssm_d[i].reshape(1, SSM_WIDTH).astype(F32), ssm_glu_w[i].astype(BF16),
                   ssm_glu_b[i].reshape(1, SSM_WIDTH).astype(F32))
            x1, tok, lgt = _post(att, (ys, u) + glu, w_out, xl, g1, lg1, lb1, sc2, sh2, router_wb, alpha, 512)
            toks, logit_streams = [tok], [lgt]
            if need_ctx:
                att_c = _ctx_attention(sink, qc, kc, vc)
                xc1, tok_c, lgt_c = _post(att_c, (ysc, uc) + glu, w_out, xc, cg1, lg1, lb1, csc2, csh2, router_wb,
                                          alpha, ctx_len)
                toks.append(tok_c)
                logit_streams.append(lgt_c)
        else:
            lam_init = 0.8 - 0.6 * math.exp(-0.3 * layer)
            w_in = dif_w_in[i].astype(BF16)
            w_out = dif_w_out[i].astype(BF16)
            q, k, v = _proj1(xl, sc1, sh1, w_in, tabs, True, tm=512)
            kc, vc = _proj1(xc, csc1, csh1, w_in[:, DIF_QK_W:], None, False, tm=ctx_len)
            lam = (jnp.exp(jnp.sum(dif_lam_q1[i].astype(F32) * dif_lam_k1[i].astype(F32)))
                   - jnp.exp(jnp.sum(dif_lam_q2[i].astype(F32) * dif_lam_k2[i].astype(F32))) + lam_init).reshape(1)
            att = _diff_attention(lam, q, k, kc, v, vc, dif_subln_g[i].reshape(1, DIF_V_HEAD).astype(F32),
                                  lam_init, tq=n)
            x1, tok, lgt = _post(att, None, w_out, xl, g1, lg1, lb1, sc2, sh2, router_wb, alpha, 512)
            toks, logit_streams = [tok], [lgt]
            if need_ctx:
                raise NotImplementedError("a differential-attention layer followed by another layer")
        ys_moe, dest, wt = _moe_block(toks, logit_streams, layer, moe_w1, moe_w3, moe_w2)
        xl = _combine_ln2(dest[:MOE_TOPK * bsz * n], x1.reshape(-1, d), ys_moe, wt[:bsz * n], g2, lg2, lb2, alpha,
                          n).reshape(bsz, n, d)
        if need_ctx:
            xc = _combine_ln2(dest[MOE_TOPK * bsz * n:], xc1.reshape(-1, d), ys_moe, wt[bsz * n:], cg2, lg2, lb2, alpha,
                              ctx_len).reshape(bsz, ctx_len, d)
    return xl
```

```python
import functools
import math
import typing

import jax
import jax.numpy as jnp
from jax import lax
from jax.experimental import pallas as pl
from jax.experimental.pallas import tpu as pltpu

F32 = jnp.float32
BF16 = jnp.bfloat16
HIGHEST = lax.Precision.HIGHEST

D_MODEL = 1024
GRID_W = 64
HEAD_DIM = 64
ROPE_BASE = 10000.0
ROPE_FREQS = HEAD_DIM // 4
LN_EPS = 1e-5
RMS_EPS = 1e-5
NEG_INF = -1e30
LOG2_E = math.log2(math.e)
LANES = 128
HALF = LANES // 2

SWA_HEADS = 8
SWA_KV_HEADS = 2
SWA_WINDOW = 128
SWA_BLOCK = 128
SWA_BLOCKS_PER_STEP = 4
SWA_Q_W = SWA_HEADS * HEAD_DIM
SWA_KV_W = SWA_KV_HEADS * HEAD_DIM

SSM_WIDTH = D_MODEL // 2
SSM_GROUP = 16
SSM_GROUPS = SSM_WIDTH // SSM_GROUP
SSM_STATE = 64
SSM_CHUNK = 16
SSM_CW = SSM_CHUNK * SSM_GROUP
SSM_SLAB_GROUPS = LANES // SSM_GROUP
SSM_GROUPS_PER_STEP = 4

DIF_HEADS = D_MODEL // (2 * HEAD_DIM)
DIF_QK_W = DIF_HEADS * 2 * HEAD_DIM
DIF_V_HEAD = 2 * HEAD_DIM
DIF_CHAIN_ROWS = 128

MOE_GROUPS = 4
MOE_EPG = 8
MOE_EXPERTS = MOE_GROUPS * MOE_EPG
MOE_TOPK = 2
SUBLANES = 8
ROUTER_EXPERT_ROW0 = SUBLANES
MOE_ROW_TILE = 512
MOE_DISPATCH_TILE = 1024
MOE_COMBINE_TILE = 512
TOKEN_TILE_ROWS = D_MODEL // LANES

VMEM_LIMIT = 56 * 1024 * 1024


def _cparams(*sem):
    return pltpu.CompilerParams(dimension_semantics=sem, vmem_limit_bytes=VMEM_LIMIT)


def _sds(shape, dtype):
    return jax.ShapeDtypeStruct(shape, dtype)


def _nt_dot(a, b):
    return lax.dot_general(a, b, (((1,), (1,)), ((), ())), preferred_element_type=F32)


def _layer_norm(r, g, b):
    mu = jnp.mean(r, -1, keepdims=True)
    rc = r - mu
    var = jnp.mean(rc * rc, -1, keepdims=True)
    return rc * lax.rsqrt(var + LN_EPS) * g + b


class ModVec(typing.NamedTuple):
    table: jax.Array
    row0: int
    stride: int

    def spec(self, sample_of_step=lambda i, *_: i):
        d = self.table.shape[-1]
        return pl.BlockSpec((1, 1, d), lambda *idx: (self.row0 + self.stride * sample_of_step(*idx), 0, 0))


def _mod_kernel(c_ref, w_ref, b_ref, o_ref):
    cv = c_ref[...]
    s = cv * jax.nn.sigmoid(cv)
    o_ref[0] = jnp.dot(s, w_ref[0], preferred_element_type=F32, precision=HIGHEST) + b_ref[0]


def _modulation(cvec, mod_w, mod_b):
    depth, d, w6 = mod_w.shape
    tn = 1536
    return pl.pallas_call(
        _mod_kernel,
        out_shape=_sds((depth, cvec.shape[0], w6), F32),
        grid=(depth, w6 // tn),
        in_specs=[pl.BlockSpec(cvec.shape, lambda l, j: (0, 0)),
                  pl.BlockSpec((1, d, tn), lambda l, j: (l, 0, j)),
                  pl.BlockSpec((1, 1, tn), lambda l, j: (l, 0, j))],
        out_specs=pl.BlockSpec((1, cvec.shape[0], tn), lambda l, j: (l, 0, j)),
        compiler_params=_cparams("arbitrary", "arbitrary"),
        name="modulation",
    )(cvec, mod_w, mod_b.reshape(depth, 1, w6))


def _rope_tables(n):
    rows = n // GRID_W
    row = jnp.repeat(jnp.arange(rows, dtype=F32), GRID_W)
    col = jnp.tile(jnp.arange(GRID_W, dtype=F32), rows)
    inv = ROPE_BASE ** (-jnp.arange(ROPE_FREQS, dtype=F32) / ROPE_FREQS)
    ang_r = row[:, None] * inv[None, :]
    ang_c = col[:, None] * inv[None, :]
    zeros = jnp.zeros_like(ang_r)
    cos64 = jnp.concatenate([jnp.cos(ang_r), jnp.cos(ang_r), jnp.cos(ang_c), jnp.cos(ang_c)], -1)
    sa64 = jnp.concatenate([-jnp.sin(ang_r), zeros, -jnp.sin(ang_c), zeros], -1)
    sb64 = jnp.concatenate([zeros, jnp.sin(ang_r), zeros, jnp.sin(ang_c)], -1)
    return tuple(jnp.tile(t, (1, LANES // HEAD_DIM)) for t in (cos64, sa64, sb64))


def _rot(t, tabs):
    if tabs is None:
        return t
    cos, sa, sb = tabs
    return t * cos + pltpu.roll(t, LANES - ROPE_FREQS, 1) * sa + pltpu.roll(t, ROPE_FREQS, 1) * sb


def _dup_halves(t):
    lo = lax.broadcasted_iota(jnp.int32, t.shape, 1) < HALF
    ta = jnp.where(lo, t, 0.0)
    tb = t - ta
    return ta + pltpu.roll(ta, HALF, 1), tb + pltpu.roll(tb, HALF, 1)


def _proj0_kernel(*refs, rope):
    if rope:
        x_ref, sc_ref, sh_ref, w_ref, cos_ref, sa_ref, sb_ref, q_ref, k_ref, v_ref, u_ref = refs
        tabs = (cos_ref[...], sa_ref[...], sb_ref[...])
    else:
        x_ref, sc_ref, sh_ref, w_ref, q_ref, k_ref, v_ref, u_ref = refs
        tabs = None
    h = (x_ref[0] * (1.0 + sc_ref[0]) + sh_ref[0]).astype(BF16)
    r = jnp.dot(h, w_ref[...], preferred_element_type=F32)
    scale = LOG2_E * HEAD_DIM ** -0.5
    for s in range(SWA_Q_W // LANES):
        q_ref[0, :, s * LANES:(s + 1) * LANES] = (_rot(r[:, s * LANES:(s + 1) * LANES], tabs) * scale).astype(BF16)
    k0, k1 = _dup_halves(_rot(r[:, SWA_Q_W:SWA_Q_W + LANES], tabs))
    vv = r[:, SWA_Q_W + LANES:SWA_Q_W + 2 * LANES]
    lo = lax.broadcasted_iota(jnp.int32, vv.shape, 1) < HALF
    k_ref[0, 0] = k0.astype(BF16)
    k_ref[0, 1] = k1.astype(BF16)
    v_ref[0, 0] = jnp.where(lo, vv, 1.0).astype(BF16)
    v_ref[0, 1] = jnp.where(lo, pltpu.roll(vv, HALF, 1), 1.0).astype(BF16)
    u_ref[0] = r[:, SWA_Q_W + 2 * LANES:]


def _proj0(x, sc, sh, w_bf16, tabs, tm):
    b, n, d = x.shape
    rope = tabs is not None
    in_specs = [pl.BlockSpec((1, tm, d), lambda i, j: (i, j, 0)), sc.spec(), sh.spec(),
                pl.BlockSpec(w_bf16.shape, lambda i, j: (0, 0))]
    args = [x, sc.table, sh.table, w_bf16]
    if rope:
        in_specs += [pl.BlockSpec((tm, LANES), lambda i, j: (j, 0))] * 3
        args += list(tabs)
    kv_spec = pl.BlockSpec((1, SWA_KV_HEADS, tm, LANES), lambda i, j: (i, 0, j, 0))
    return pl.pallas_call(
        functools.partial(_proj0_kernel, rope=rope),
        out_shape=(_sds((b, n, SWA_Q_W), BF16), _sds((b, SWA_KV_HEADS, n, LANES), BF16),
                   _sds((b, SWA_KV_HEADS, n, LANES), BF16), _sds((b, n, SSM_WIDTH), F32)),
        grid=(b, n // tm),
        in_specs=in_specs,
        out_specs=(pl.BlockSpec((1, tm, SWA_Q_W), lambda i, j: (i, j, 0)), kv_spec, kv_spec,
                   pl.BlockSpec((1, tm, SSM_WIDTH), lambda i, j: (i, j, 0))),
        compiler_params=_cparams("parallel", "parallel"),
        name="proj0_rope" if rope else "proj0_ctx",
    )(*args)


def _swa_kernel(*refs, tq, nsub, local, n_lat):
    if local:
        sink_ref, q_ref, k_ref, v_ref, kc_ref, vc_ref, o_ref = refs
    else:
        sink_ref, q_ref, kc_ref, vc_ref, o_ref = refs
    rows = 4 * tq
    lo = lax.broadcasted_iota(jnp.int32, (tq, LANES), 1) < HALF
    lo4 = lax.broadcasted_iota(jnp.int32, (rows, LANES), 1) < HALF
    rown = lax.broadcasted_iota(jnp.int32, (rows, 1), 0)
    for sub in range(nsub):
        j = pl.program_id(1) * nsub + sub
        r0 = sub * tq
        if local:
            span = 3 * SWA_BLOCK
            start = pl.multiple_of(jnp.clip((j - 1) * SWA_BLOCK, 0, n_lat - span), SWA_BLOCK)
            rr = lax.broadcasted_iota(jnp.int32, (rows, span), 0)
            cc = lax.broadcasted_iota(jnp.int32, (rows, span), 1)
            qpos = j * tq + (rr & (tq - 1))
            mask = jnp.abs(qpos - (start + cc)) <= SWA_WINDOW
        for h in range(SWA_KV_HEADS):
            qa = q_ref[0, r0:r0 + tq, (2 * h) * LANES:(2 * h + 1) * LANES].astype(F32)
            qb = q_ref[0, r0:r0 + tq, (2 * h + 1) * LANES:(2 * h + 2) * LANES].astype(F32)
            q4 = jnp.concatenate([jnp.where(lo, qa, 0.0), jnp.where(lo, 0.0, qa),
                                  jnp.where(lo, qb, 0.0), jnp.where(lo, 0.0, qb)], 0).astype(BF16)
            sink = LOG2_E * jnp.where(rown < tq, sink_ref[4 * h],
                                      jnp.where(rown < 2 * tq, sink_ref[4 * h + 1],
                                                jnp.where(rown < 3 * tq, sink_ref[4 * h + 2], sink_ref[4 * h + 3])))
            s_ctx = _nt_dot(q4, kc_ref[0, h])
            m = jnp.maximum(jnp.max(s_ctx, -1, keepdims=True), sink)
            if local:
                s_loc = jnp.where(mask, _nt_dot(q4, k_ref[0, h, pl.ds(start, span), :]), NEG_INF)
                m = jnp.maximum(m, jnp.max(s_loc, -1, keepdims=True))
            o4 = jnp.dot(jnp.exp2(s_ctx - m).astype(BF16), vc_ref[0, h], preferred_element_type=F32)
            if local:
                o4 = o4 + jnp.dot(jnp.exp2(s_loc - m).astype(BF16), v_ref[0, h, pl.ds(start, span), :],
                                  preferred_element_type=F32)
            o4 = o4 + jnp.where(lo4, 0.0, jnp.exp2(sink - m))
            o4 = o4 * (1.0 / jnp.where(lo4, pltpu.roll(o4, HALF, 1), 1.0))
            for s in range(2):
                even, odd = o4[2 * s * tq:(2 * s + 1) * tq], o4[(2 * s + 1) * tq:(2 * s + 2) * tq]
                o_ref[0, r0:r0 + tq, (2 * h + s) * LANES:(2 * h + s + 1) * LANES] = jnp.where(
                    lo, even, pltpu.roll(odd, HALF, 1)).astype(BF16)


def _swa_attention(sink, q, k, v, kc, vc):
    b, n, _ = q.shape
    nc = kc.shape[2]
    tq, nsub = SWA_BLOCK, SWA_BLOCKS_PER_STEP
    full = lambda m: pl.BlockSpec((1, SWA_KV_HEADS, m, LANES), lambda i, j: (i, 0, 0, 0))
    return pl.pallas_call(
        functools.partial(_swa_kernel, tq=tq, nsub=nsub, local=True, n_lat=n),
        out_shape=_sds((b, n, SWA_Q_W), BF16),
        grid=(b, n // (tq * nsub)),
        in_specs=[pl.BlockSpec(memory_space=pltpu.SMEM),
                  pl.BlockSpec((1, tq * nsub, SWA_Q_W), lambda i, j: (i, j, 0)),
                  full(n), full(n), full(nc), full(nc)],
        out_specs=pl.BlockSpec((1, tq * nsub, SWA_Q_W), lambda i, j: (i, j, 0)),
        compiler_params=_cparams("parallel", "arbitrary"),
        name="swa_attention",
    )(sink, q, k, v, kc, vc)


def _ctx_attention(sink, qc, kc, vc):
    b, nc, _ = qc.shape
    full = pl.BlockSpec((1, SWA_KV_HEADS, nc, LANES), lambda i, j: (i, 0, 0, 0))
    return pl.pallas_call(
        functools.partial(_swa_kernel, tq=nc, nsub=1, local=False, n_lat=0),
        out_shape=_sds((b, nc, SWA_Q_W), BF16),
        grid=(b, 1),
        in_specs=[pl.BlockSpec(memory_space=pltpu.SMEM),
                  pl.BlockSpec((1, nc, SWA_Q_W), lambda i, j: (i, 0, 0)), full, full],
        out_specs=pl.BlockSpec((1, nc, SWA_Q_W), lambda i, j: (i, 0, 0)),
        compiler_params=_cparams("parallel", "arbitrary"),
        name="ctx_attention",
    )(sink, qc, kc, vc)


def _cpow(e, lr, li):
    mag = jnp.exp(e * lr)
    return mag * jnp.cos(e * li), mag * jnp.sin(e * li)


def _s5_prep_kernel(row_ref, col_ref, bt_ref, ct_ref, ctt_ref, min_ref, mintra_ref, mst_ref, al_ref):
    L, C = SSM_CHUNK, SSM_GROUP
    ar, ai, dt = row_ref[0, 0:1, :], row_ref[0, 1:2, :], row_ref[0, 2:3, :]
    lr, li = dt * ar, dt * ai
    a1_re, a1_im = _cpow(1.0, lr, li)
    den = ar * ar + ai * ai
    nr = a1_re - 1.0
    coef_re = (nr * ar + a1_im * ai) / den
    coef_im = (a1_im * ar - nr * ai) / den
    b_re, b_im = bt_ref[0, 0], bt_ref[0, 1]
    bb_re = coef_re * b_re - coef_im * b_im
    bb_im = coef_re * b_im + coef_im * b_re
    c_re, c_im = ct_ref[0, 0], ct_ref[0, 1]
    fwd = lax.broadcasted_iota(jnp.int32, (L, LANES), 1) < HALF
    step = lax.broadcasted_iota(jnp.int32, (L, LANES), 0).astype(F32)
    pin_re, pin_im = _cpow(jnp.where(fwd, (L - 1.0) - step, step), lr, li)
    pst_re, pst_im = _cpow(jnp.where(fwd, step + 1.0, float(L) - step), lr, li)
    for j in range(L):
        pr, pi = pin_re[j:j + 1], pin_im[j:j + 1]
        min_ref[0, j * C:(j + 1) * C, 0:LANES] = (pr * bb_re - pi * bb_im).astype(BF16)
        min_ref[0, j * C:(j + 1) * C, LANES:] = (pr * bb_im + pi * bb_re).astype(BF16)
        pr, pi = pst_re[j:j + 1], pst_im[j:j + 1]
        mst_ref[0, j * C:(j + 1) * C, 0:LANES] = (pr * c_re - pi * c_im).astype(BF16)
        mst_ref[0, j * C:(j + 1) * C, LANES:] = (-(pr * c_im + pi * c_re)).astype(BF16)
    al_re, al_im = _cpow(float(L), lr, li)
    al_ref[0, 0:1, :] = al_re
    al_ref[0, 1:2, :] = al_im
    lag = (lax.broadcasted_iota(jnp.int32, (SSM_STATE, SSM_CW), 1) // C).astype(F32)
    rep = (lax.broadcasted_iota(jnp.int32, (C, SSM_CW), 1) % C
           == lax.broadcasted_iota(jnp.int32, (C, SSM_CW), 0)).astype(F32)
    ks = []
    for d in range(2):
        col = col_ref[0, d]
        p_re, p_im = _cpow(lag if d == 0 else (L - 1.0) - lag, col[:, 2:3] * col[:, 0:1], col[:, 2:3] * col[:, 1:2])
        ct_re = jnp.dot(ctt_ref[0, d, 0], rep, preferred_element_type=F32, precision=HIGHEST)
        ct_im = jnp.dot(ctt_ref[0, d, 1], rep, preferred_element_type=F32, precision=HIGHEST)
        ca_re = ct_re * p_re - ct_im * p_im
        ca_im = ct_re * p_im + ct_im * p_re
        sl = slice(d * HALF, (d + 1) * HALF)
        ks.append(jnp.dot(bb_re[:, sl], ca_re, preferred_element_type=F32, precision=HIGHEST)
                  - jnp.dot(bb_im[:, sl], ca_im, preferred_element_type=F32, precision=HIGHEST))
    zeros = jnp.zeros((C, SSM_CW), F32)
    lagged = (pltpu.roll(jnp.concatenate([ks[0], zeros], axis=1), (L - 1) * C, 1)
              + jnp.concatenate([ks[1], zeros], axis=1))
    for j in range(L):
        shift = (L - 1 - j) * C
        win = lagged if shift == 0 else pltpu.roll(lagged, 2 * SSM_CW - shift, 1)
        mintra_ref[0, j * C:(j + 1) * C, :] = win[:, 0:SSM_CW].astype(BF16)


def _ssm_matrices(a_re, a_im, log_step, b_re, b_im, c_re, c_im):
    G, P, C = SSM_GROUPS, SSM_STATE, SSM_GROUP
    f = lambda t: t.astype(F32)
    dt = jnp.exp(f(log_step))
    dtp = jnp.broadcast_to(dt[..., None], (2, G, P))
    row = jnp.stack([f(a_re), f(a_im), dtp], 0)
    row = jnp.transpose(row, (2, 0, 1, 3)).reshape(G, 3, 2 * P)
    col = jnp.transpose(jnp.stack([f(a_re), f(a_im), dtp], -1), (1, 0, 2, 3))
    lanes = lambda t: jnp.transpose(t, (1, 2, 0, 3)).reshape(G, C, 2 * P)
    bt = jnp.stack([lanes(jnp.transpose(f(b_re), (0, 1, 3, 2))), lanes(jnp.transpose(f(b_im), (0, 1, 3, 2)))], 1)
    ct = jnp.stack([lanes(f(c_re)), lanes(f(c_im))], 1)
    ctt = jnp.stack([jnp.transpose(f(c_re), (1, 0, 3, 2)), jnp.transpose(f(c_im), (1, 0, 3, 2))], 2)
    mat = pl.BlockSpec((1, SSM_CW, SSM_CW), lambda g: (g, 0, 0))
    return pl.pallas_call(
        _s5_prep_kernel,
        out_shape=(_sds((G, SSM_CW, SSM_CW), BF16),) * 3 + (_sds((G, 2, LANES), F32),),
        grid=(G,),
        in_specs=[pl.BlockSpec((1, 3, LANES), lambda g: (g, 0, 0)),
                  pl.BlockSpec((1, 2, P, 3), lambda g: (g, 0, 0, 0)),
                  pl.BlockSpec((1, 2, C, LANES), lambda g: (g, 0, 0, 0)),
                  pl.BlockSpec((1, 2, C, LANES), lambda g: (g, 0, 0, 0)),
                  pl.BlockSpec((1, 2, 2, P, C), lambda g: (g, 0, 0, 0, 0))],
        out_specs=(mat, mat, mat, pl.BlockSpec((1, 2, LANES), lambda g: (g, 0, 0))),
        compiler_params=_cparams("parallel"),
        name="s5_prep",
    )(row, col, bt, ct, ctt)


def _ssm_pack_kernel(uc_ref, u_ref, x_ref, *, nc_ctx, nc_lat):
    for src, row0, nch in ((uc_ref, 0, nc_ctx), (u_ref, nc_ctx, nc_lat)):
        steps = [src[0, pl.ds(j, nch, stride=SSM_CHUNK), :] for j in range(SSM_CHUNK)]
        lane = lax.broadcasted_iota(jnp.int32, (nch, LANES), 1)
        piece = [(lane >= jj * SSM_GROUP) & (lane < (jj + 1) * SSM_GROUP) for jj in range(SSM_SLAB_GROUPS)]
        for g in range(SSM_SLAB_GROUPS):
            for h in range(SSM_CW // LANES):
                acc = jnp.zeros((nch, LANES), F32)
                for jj in range(SSM_SLAB_GROUPS):
                    z = steps[h * SSM_SLAB_GROUPS + jj]
                    shift = ((jj - g) * SSM_GROUP) % LANES
                    acc = jnp.where(piece[jj], z if shift == 0 else pltpu.roll(z, shift, 1), acc)
                x_ref[g, row0:row0 + nch, h * LANES:(h + 1) * LANES] = acc.astype(BF16)


def _ssm_unpack_kernel(y_ref, oc_ref, o_ref, *, nc_ctx, nc_lat):
    for dst, row0, nch in ((oc_ref, 0, nc_ctx), (o_ref, nc_ctx, nc_lat)):
        lane = lax.broadcasted_iota(jnp.int32, (nch, LANES), 1)
        piece = [(lane >= g * SSM_GROUP) & (lane < (g + 1) * SSM_GROUP) for g in range(SSM_SLAB_GROUPS)]
        for i in range(SSM_CHUNK):
            h, ii = divmod(i, SSM_SLAB_GROUPS)
            acc = jnp.zeros((nch, LANES), F32)
            for g in range(SSM_SLAB_GROUPS):
                z = y_ref[g, row0:row0 + nch, h * LANES:(h + 1) * LANES]
                shift = ((g - ii) * SSM_GROUP) % LANES
                acc = jnp.where(piece[g], z if shift == 0 else pltpu.roll(z, shift, 1), acc)
            dst[0, pl.ds(i, nch, stride=SSM_CHUNK), :] = acc


def _ssm_kernel(x_ref, min_ref, mintra_ref, mstate_ref, al_ref, y_ref, v_scr, s_scr, *, nb, nc_ctx, n_chunks):
    n_grp = x_ref.shape[0]
    for g in range(n_grp):
        v = jnp.dot(x_ref[g], min_ref[g], preferred_element_type=F32)
        v_scr[g, 0] = v[:, 0:LANES]
        v_scr[g, 1] = v[:, LANES:]
    ar = [al_ref[g, 0:1, :] for g in range(n_grp)]
    ai = [al_ref[g, 1:2, :] for g in range(n_grp)]
    lo = lax.broadcasted_iota(jnp.int32, (nb, LANES), 1) < HALF

    def body(k, carry):
        kr = jnp.where(k < nc_ctx, nc_ctx - 1 - k, n_chunks - 1 + nc_ctx - k)
        rf = pl.ds(k, nb, stride=n_chunks)
        rr = pl.ds(kr, nb, stride=n_chunks)
        out = []
        for g in range(n_grp):
            sre, sim = carry[2 * g], carry[2 * g + 1]
            s_scr[g, 0, rf, :] = sre
            s_scr[g, 1, rr, :] = sre
            s_scr[g, 2, rf, :] = sim
            s_scr[g, 3, rr, :] = sim
            vre = jnp.where(lo, v_scr[g, 0, rf, :], v_scr[g, 0, rr, :])
            vim = jnp.where(lo, v_scr[g, 1, rf, :], v_scr[g, 1, rr, :])
            out += [ar[g] * sre - ai[g] * sim + vre, ar[g] * sim + ai[g] * sre + vim]
        return tuple(out)

    zero = jnp.zeros((nb, LANES), F32)
    lax.fori_loop(0, n_chunks, body, (zero,) * (2 * n_grp))
    lo_all = lax.broadcasted_iota(jnp.int32, (s_scr.shape[2], LANES), 1) < HALF
    for g in range(n_grp):
        s_in = jnp.concatenate([jnp.where(lo_all, s_scr[g, 0], s_scr[g, 1]),
                                jnp.where(lo_all, s_scr[g, 2], s_scr[g, 3])], axis=1).astype(BF16)
        y_ref[g] = jnp.dot(x_ref[g], mintra_ref[g], preferred_element_type=F32) + _nt_dot(s_in, mstate_ref[g])


def _ssm_scan(u, uc, mats):
    m_in, m_intra, m_state, a_l = mats
    b, n, _ = u.shape
    nc = uc.shape[1]
    nc_ctx, nc_lat = nc // SSM_CHUNK, n // SSM_CHUNK
    n_chunks = nc_ctx + nc_lat
    r = n_chunks * b
    nat = lambda m: pl.BlockSpec((1, m, LANES), lambda i, s: (i, 0, s))
    grp = pl.BlockSpec((SSM_SLAB_GROUPS, n_chunks, SSM_CW), lambda i, s: (s, i, 0))
    xg = pl.pallas_call(
        functools.partial(_ssm_pack_kernel, nc_ctx=nc_ctx, nc_lat=nc_lat),
        out_shape=_sds((SSM_GROUPS, r, SSM_CW), BF16),
        grid=(b, SSM_GROUPS // SSM_SLAB_GROUPS),
        in_specs=[nat(nc), nat(n)],
        out_specs=grp,
        compiler_params=_cparams("parallel", "parallel"),
        name="ssm_pack",
    )(uc, u)
    gs = SSM_GROUPS_PER_STEP
    mat = pl.BlockSpec((gs, SSM_CW, SSM_CW), lambda g: (g, 0, 0))
    yg = pl.pallas_call(
        functools.partial(_ssm_kernel, nb=b, nc_ctx=nc_ctx, n_chunks=n_chunks),
        out_shape=_sds((SSM_GROUPS, r, SSM_CW), F32),
        grid=(SSM_GROUPS // gs,),
        in_specs=[pl.BlockSpec((gs, r, SSM_CW), lambda g: (g, 0, 0)), mat, mat, mat,
                  pl.BlockSpec((gs, 2, LANES), lambda g: (g, 0, 0))],
        out_specs=pl.BlockSpec((gs, r, SSM_CW), lambda g: (g, 0, 0)),
        scratch_shapes=[pltpu.VMEM((gs, 2, r, LANES), F32), pltpu.VMEM((gs, 4, r, LANES), F32)],
        compiler_params=_cparams("parallel"),
        name="ssm_scan",
    )(xg, m_in, m_intra, m_state, a_l)
    ysc, ys = pl.pallas_call(
        functools.partial(_ssm_unpack_kernel, nc_ctx=nc_ctx, nc_lat=nc_lat),
        out_shape=(_sds((b, nc, SSM_WIDTH), F32), _sds((b, n, SSM_WIDTH), F32)),
        grid=(b, SSM_GROUPS // SSM_SLAB_GROUPS),
        in_specs=[grp],
        out_specs=(nat(nc), nat(n)),
        compiler_params=_cparams("parallel", "parallel"),
        name="ssm_unpack",
    )(yg)
    return ys, ysc


def _split_rows(ref, val):
    m = val.shape[0]
    for s in range(TOKEN_TILE_ROWS):
        ref[pl.ds(s, m, stride=TOKEN_TILE_ROWS), :] = val[:, s * LANES:(s + 1) * LANES]


def _merge_rows(ref):
    m = ref.shape[0] // TOKEN_TILE_ROWS
    return jnp.concatenate([ref[pl.ds(s, m, stride=TOKEN_TILE_ROWS), :] for s in range(TOKEN_TILE_ROWS)], axis=1)


def _token_tile(ref, t):
    return ref.at[pl.ds(pl.multiple_of(t * TOKEN_TILE_ROWS, TOKEN_TILE_ROWS), TOKEN_TILE_ROWS)]


def _router_logits(w, bias, h):
    w_hi, h_hi = w.astype(BF16), h.astype(BF16)
    w_lo, h_lo = (w - w_hi.astype(F32)).astype(BF16), (h - h_hi.astype(F32)).astype(BF16)
    return _nt_dot(w_hi, h_hi) + (_nt_dot(w_hi, h_lo) + _nt_dot(w_lo, h_hi)) + bias


def _post_kernel(*refs, alpha, with_ssm):
    wr_ref, br_ref = refs[-5:-3]
    refs = refs[:-5] + refs[-3:]
    if with_ssm:
        (att_ref, ys_ref, u_ref, dsk_ref, gw_ref, gb_ref, wo_ref, x_ref, g1_ref, lg_ref, lb_ref, sc2_ref, sh2_ref,
         x1_ref, h2_ref, lgt_ref) = refs
        y = ys_ref[0] + u_ref[0] * dsk_ref[...]
        gl = jax.nn.gelu(y)
        gate = jax.nn.sigmoid(jnp.dot(gl.astype(BF16), gw_ref[...], preferred_element_type=F32) + gb_ref[...])
        ssm = (gl * gate).astype(BF16)
        mix = (jnp.dot(att_ref[0], wo_ref[0:SWA_Q_W, :], preferred_element_type=F32)
               + jnp.dot(ssm, wo_ref[SWA_Q_W:, :], preferred_element_type=F32))
    else:
        att_ref, wo_ref, x_ref, g1_ref, lg_ref, lb_ref, sc2_ref, sh2_ref, x1_ref, h2_ref, lgt_ref = refs
        mix = jnp.dot(att_ref[0], wo_ref[...], preferred_element_type=F32)
    x1 = _layer_norm(alpha * x_ref[0] + g1_ref[0] * mix, lg_ref[...], lb_ref[...])
    x1_ref[0] = x1
    h2 = x1 * (1.0 + sc2_ref[0]) + sh2_ref[0]
    _split_rows(h2_ref, h2)
    lgt_ref[...] = _router_logits(wr_ref[...], br_ref[...], h2)


def _post(att, ssm_args, w_out_bf16, x, g1, ln_g, ln_b, sc2, sh2, router_wb, alpha, tm):
    b, n, d = x.shape
    tok = lambda w: pl.BlockSpec((1, tm, w), lambda i, j: (i, j, 0))
    const = lambda a: pl.BlockSpec(a.shape, lambda i, j: (0,) * a.ndim)
    in_specs = [tok(att.shape[-1])]
    args = [att]
    if ssm_args is not None:
        ys, u, dsk, gw, gb = ssm_args
        in_specs += [tok(SSM_WIDTH), tok(SSM_WIDTH), const(dsk), const(gw), const(gb)]
        args += [ys, u, dsk, gw, gb]
    in_specs += [const(w_out_bf16), tok(d), g1.spec(), const(ln_g), const(ln_b), sc2.spec(), sh2.spec()]
    args += [w_out_bf16, x, g1.table, ln_g, ln_b, sc2.table, sh2.table]
    in_specs += [const(a) for a in router_wb]
    args += list(router_wb)
    per_b = n // tm
    n_logit = router_wb[0].shape[0]
    return pl.pallas_call(
        functools.partial(_post_kernel, alpha=alpha, with_ssm=ssm_args is not None),
        out_shape=(_sds((b, n, d), F32), _sds((b * n * TOKEN_TILE_ROWS, LANES), F32), _sds((n_logit, b * n), F32)),
        grid=(b, n // tm),
        in_specs=in_specs,
        out_specs=(tok(d), pl.BlockSpec((tm * TOKEN_TILE_ROWS, LANES), lambda i, j: (i * per_b + j, 0)),
                   pl.BlockSpec((n_logit, tm), lambda i, j: (0, i * per_b + j))),
        compiler_params=_cparams("parallel", "parallel"),
        name="post_mixer_ssm" if ssm_args is not None else "post_mixer",
    )(*args)


def _first_max(v, sub):
    m = jnp.max(v, 0, keepdims=True)
    idx = jnp.min(jnp.where(v == m, sub, float(SUBLANES)), 0, keepdims=True)
    return m, idx


def _stream_blocks(toks, tb):
    starts = [0]
    for t in toks:
        starts.append(starts[-1] + t.shape[0] // (tb * TOKEN_TILE_ROWS))
    return starts


def _stream_spec(tok, tb, start):
    last = tok.shape[0] // (tb * TOKEN_TILE_ROWS) - 1
    return pl.BlockSpec((tb * TOKEN_TILE_ROWS, LANES), lambda i, *_: (jnp.clip(i - start, 0, last), 0))


def _router_kernel(*refs, starts):
    n_streams = len(starts) - 1
    lgt_refs = refs[:n_streams]
    ids_ref, wts_ref, rank_ref, cnt_ref, carry_scr = refs[n_streams:]
    step = pl.program_id(0)

    @pl.when(step == 0)
    def _():
        carry_scr[...] = jnp.zeros_like(carry_scr)

    logits = lgt_refs[0][...]
    for ref, start in zip(lgt_refs[1:], starts[1:]):
        logits = jnp.where(step >= start, ref[...], logits)
    tm = logits.shape[1]
    sub = lax.broadcasted_iota(jnp.int32, (SUBLANES, tm), 0).astype(F32)
    gl = logits[0:SUBLANES]
    gmax, gi = _first_max(gl, sub)
    gp = 1.0 / jnp.sum(jnp.exp(gl - gmax), 0, keepdims=True)
    le = logits[ROUTER_EXPERT_ROW0:ROUTER_EXPERT_ROW0 + MOE_EPG]
    for g in range(1, MOE_GROUPS):
        le = jnp.where(gi == float(g), logits[ROUTER_EXPERT_ROW0 + g * MOE_EPG:ROUTER_EXPERT_ROW0 + (g + 1) * MOE_EPG], le)
    m1, i1 = _first_max(le, sub)
    m2, i2 = _first_max(jnp.where(sub == i1, NEG_INF, le), sub)
    t = jnp.exp(m2 - m1)
    e1 = gi * float(MOE_EPG) + i1
    e2 = gi * float(MOE_EPG) + i2
    ids_ref[0:1, :] = e1.astype(jnp.int32)
    ids_ref[1:2, :] = e2.astype(jnp.int32)
    wts_ref[0:1, :] = gp / (1.0 + t)
    wts_ref[1:2, :] = gp * t / (1.0 + t)
    esub = lax.broadcasted_iota(jnp.int32, (MOE_EXPERTS, tm), 0).astype(F32)
    oh1 = (esub == e1).astype(F32)
    oh2 = (esub == e2).astype(F32)
    both = oh1 + oh2
    earlier = (lax.broadcasted_iota(jnp.int32, (tm, tm), 0) < lax.broadcasted_iota(jnp.int32, (tm, tm), 1))
    prefix = jnp.dot(both.astype(BF16), earlier.astype(BF16), preferred_element_type=F32) + carry_scr[...]
    rank_ref[0:1, :] = jnp.sum(oh1 * prefix, 0, keepdims=True).astype(jnp.int32)
    rank_ref[1:2, :] = jnp.sum(oh2 * prefix, 0, keepdims=True).astype(jnp.int32)
    carry_scr[...] += jnp.sum(both, 1, keepdims=True)
    cnt_ref[...] = jnp.broadcast_to(carry_scr[...], cnt_ref.shape)


def _router_weights(wg, bg, we, be):
    d = wg.shape[0]
    rows = ROUTER_EXPERT_ROW0 + MOE_EXPERTS
    w = jnp.zeros((rows, d), F32)
    w = w.at[:MOE_GROUPS].set(wg.T)
    w = w.at[ROUTER_EXPERT_ROW0:].set(jnp.transpose(we, (0, 2, 1)).reshape(MOE_EXPERTS, d))
    bias = jnp.full((rows, 1), NEG_INF, F32)
    bias = bias.at[:MOE_GROUPS, 0].set(bg)
    bias = bias.at[ROUTER_EXPERT_ROW0:, 0].set(be.reshape(-1))
    return w, bias


def _router(logit_streams, tm):
    starts = [0]
    for lg in logit_streams:
        starts.append(starts[-1] + lg.shape[1] // tm)
    t = starts[-1] * tm
    rows = logit_streams[0].shape[0]
    stream_spec = lambda lg, s0: pl.BlockSpec(
        (rows, tm), lambda i: (0, jnp.clip(i - s0, 0, lg.shape[1] // tm - 1)))
    pair = pl.BlockSpec((MOE_TOPK, tm), lambda i: (0, i))
    ids, wts, rank, cnt = pl.pallas_call(
        functools.partial(_router_kernel, starts=starts),
        out_shape=(_sds((MOE_TOPK, t), jnp.int32), _sds((MOE_TOPK, t), F32), _sds((MOE_TOPK, t), jnp.int32),
                   _sds((MOE_EXPERTS, LANES), F32)),
        grid=(t // tm,),
        in_specs=[stream_spec(lg, s0) for lg, s0 in zip(logit_streams, starts)],
        out_specs=(pair, pair, pair, pl.BlockSpec((MOE_EXPERTS, LANES), lambda i: (0, 0))),
        scratch_shapes=[pltpu.VMEM((MOE_EXPERTS, 1), F32)],
        compiler_params=_cparams("arbitrary"),
        name="moe_router",
    )(*logit_streams)
    return ids, wts, rank, cnt[:, 0].astype(jnp.int32)


def _moe_plan(ids, rank, counts, n_tok):
    tm = MOE_ROW_TILE
    padded = ((counts + tm - 1) // tm) * tm
    ends = jnp.cumsum(padded)
    offs = ends - padded
    experts = jnp.arange(MOE_EXPERTS, dtype=jnp.int32)
    dest = (jnp.sum(jnp.where(ids[..., None] == experts, offs, 0), -1) + rank).astype(jnp.int32)
    n_tiles = (MOE_TOPK * n_tok + MOE_EXPERTS * (tm - 1)) // tm
    starts = jnp.arange(n_tiles, dtype=jnp.int32) * tm
    tile_expert = jnp.minimum(jnp.sum((ends[None, :] <= starts[:, None]).astype(jnp.int32), -1), MOE_EXPERTS - 1)
    n_valid = (ends[-1] // tm).astype(jnp.int32).reshape(1)
    pad = jnp.stack([offs + counts, padded - counts, jnp.broadcast_to(n_valid, counts.shape)]).astype(jnp.int32)
    return dest, tile_expert, n_valid, pad, n_tiles * tm


def _pad_fill(pad_ref, zero_scr, xs_ref, sem, wait):
    def per_expert(e, carry):
        first, count = pad_ref[0, e], pad_ref[1, e]
        piece = MOE_ROW_TILE // 2
        while piece >= 1:
            row = first + (count & ~(2 * piece - 1))
            n = piece * TOKEN_TILE_ROWS

            @pl.when((count & piece) != 0)
            def _(row=row, n=n):
                copy = pltpu.make_async_copy(
                    zero_scr.at[pl.ds(0, n)],
                    xs_ref.at[pl.ds(pl.multiple_of(row * TOKEN_TILE_ROWS, TOKEN_TILE_ROWS), n)], sem)
                copy.wait() if wait else copy.start()

            piece //= 2
        return carry

    lax.fori_loop(0, MOE_EXPERTS, per_expert, 0)
    half = MOE_ROW_TILE // 2 * TOKEN_TILE_ROWS
    n_tiles = xs_ref.shape[0] // (2 * half)

    def per_tile(t, carry):
        for h in range(2):
            copy = pltpu.make_async_copy(zero_scr, xs_ref.at[pl.ds(pl.multiple_of((2 * t + h) * half, half), half)],
                                         sem)
            copy.wait() if wait else copy.start()
        return carry

    lax.fori_loop(pad_ref[2, 0], n_tiles, per_tile, 0)


def _dispatch_kernel(*refs, tb, starts):
    n_streams = len(starts) - 1
    dest_ref, pad_ref = refs[:2]
    tok_refs = refs[2:2 + n_streams]
    xs_ref, zero_scr, sem, pad_sem = refs[2 + n_streams:]
    step = pl.program_id(0)

    @pl.when(step == 0)
    def _():
        zero_scr[...] = jnp.zeros_like(zero_scr)
        _pad_fill(pad_ref, zero_scr, xs_ref, pad_sem, wait=False)

    for s, tok_ref in enumerate(tok_refs):
        @pl.when((step >= starts[s]) & (step < starts[s + 1]))
        def _(tok_ref=tok_ref):
            def body(r, carry):
                for k in range(MOE_TOPK):
                    pltpu.make_async_copy(_token_tile(tok_ref, r), _token_tile(xs_ref, dest_ref[MOE_TOPK * r + k]),
                                          sem).start(priority=k % 2)
                return carry

            lax.fori_loop(0, tb, body, 0, unroll=8)

    for k in range(MOE_TOPK):
        pltpu.make_async_copy(tok_refs[0], xs_ref.at[pl.ds(0, tb * TOKEN_TILE_ROWS)], sem).wait()

    @pl.when(step == 0)
    def _():
        _pad_fill(pad_ref, zero_scr, xs_ref, pad_sem, wait=True)


def _dispatch(toks, dest, pad, n_rows):
    tb = MOE_DISPATCH_TILE
    nblk = dest.shape[0] // (MOE_TOPK * tb)
    starts = _stream_blocks(toks, tb)
    return pl.pallas_call(
        functools.partial(_dispatch_kernel, tb=tb, starts=starts),
        out_shape=_sds((n_rows * TOKEN_TILE_ROWS, LANES), F32),
        grid=(nblk,),
        in_specs=[pl.BlockSpec((MOE_TOPK * tb,), lambda i: (i,), memory_space=pltpu.SMEM),
                  pl.BlockSpec(memory_space=pltpu.SMEM)]
        + [_stream_spec(tok, tb, s0) for tok, s0 in zip(toks, starts)],
        out_specs=pl.BlockSpec(memory_space=pl.ANY),
        scratch_shapes=[pltpu.VMEM((MOE_ROW_TILE // 2 * TOKEN_TILE_ROWS, LANES), F32),
                        pltpu.SemaphoreType.DMA(()), pltpu.SemaphoreType.DMA(())],
        compiler_params=_cparams("arbitrary"),
        name="moe_dispatch",
    )(dest, pad, *toks)


def _ffn_kernel(te_ref, nv_ref, x_ref, w1_ref, w3_ref, w2_ref, y_ref, w13_scr, w2_scr):
    i = pl.program_id(0)
    f = w1_ref.shape[2]

    @pl.when((i == 0) | (te_ref[i] != te_ref[jnp.maximum(i - 1, 0)]))
    def _():
        w13_scr[:, 0:f] = w1_ref[0].astype(BF16)
        w13_scr[:, f:2 * f] = w3_ref[0].astype(BF16)
        w2_scr[...] = w2_ref[0].astype(BF16)

    @pl.when(i < nv_ref[0])
    def _():
        h13 = jnp.dot(_merge_rows(x_ref).astype(BF16), w13_scr[...], preferred_element_type=F32)
        h1 = h13[:, 0:f]
        hh = (h1 * jax.nn.sigmoid(h1) * h13[:, f:2 * f]).astype(BF16)
        _split_rows(y_ref, jnp.dot(hh, w2_scr[...], preferred_element_type=F32))

    @pl.when(i >= nv_ref[0])
    def _():
        y_ref[...] = jnp.zeros_like(y_ref)


def _expert_ffn(tile_expert, n_valid, xs, w1, w3, w2):
    p = xs.shape[0] // TOKEN_TILE_ROWS
    _, d, f = w1.shape
    tm = MOE_ROW_TILE
    rows = pl.BlockSpec((tm * TOKEN_TILE_ROWS, LANES), lambda i, te, nv: (i, 0))
    rows_in = pl.BlockSpec((tm * TOKEN_TILE_ROWS, LANES), lambda i, te, nv: (jnp.minimum(i, nv[0] - 1), 0))
    return pl.pallas_call(
        _ffn_kernel,
        out_shape=_sds(xs.shape, F32),
        grid_spec=pltpu.PrefetchScalarGridSpec(
            num_scalar_prefetch=2,
            grid=(p // tm,),
            in_specs=[rows_in,
                      pl.BlockSpec((1, d, f), lambda i, te, nv: (te[i], 0, 0)),
                      pl.BlockSpec((1, d, f), lambda i, te, nv: (te[i], 0, 0)),
                      pl.BlockSpec((1, f, d), lambda i, te, nv: (te[i], 0, 0))],
            out_specs=rows,
            scratch_shapes=[pltpu.VMEM((d, 2 * f), BF16), pltpu.VMEM((f, d), BF16)]),
        compiler_params=_cparams("arbitrary"),
        name="moe_expert_ffn",
    )(tile_expert, n_valid, xs, w1, w3, w2)


def _combine_ln2_kernel(dest_ref, next_ref, x1_ref, ys_ref, wt_ref, g2_ref, lg_ref, lb_ref, o_ref, buf, sem, *,
                        alpha, tb, nblk):
    step = pl.program_id(0)

    def gather(d_ref, slot):
        def body(r, carry):
            for k in range(MOE_TOPK):
                pltpu.make_async_copy(_token_tile(ys_ref, d_ref[MOE_TOPK * r + k]), _token_tile(buf.at[slot, k], r),
                                      sem.at[slot, k]).start(priority=k % 2)
            return carry

        lax.fori_loop(0, tb, body, 0, unroll=8)

    @pl.when(step == 0)
    def _():
        gather(dest_ref, 0)

    @pl.when(step + 1 < nblk)
    def _():
        gather(next_ref, (step + 1) % 2)

    slot = step % 2
    for k in range(MOE_TOPK):
        pltpu.make_async_copy(ys_ref.at[pl.ds(0, tb * TOKEN_TILE_ROWS)], buf.at[slot, k], sem.at[slot, k]).wait()
    f = wt_ref[:, 0:1] * _merge_rows(buf.at[slot, 0]) + wt_ref[:, 1:2] * _merge_rows(buf.at[slot, 1])
    o_ref[...] = _layer_norm(alpha * x1_ref[...] + g2_ref[0] * f, lg_ref[...], lb_ref[...])


def _combine_ln2(dest, x1, ys, wt, g2, ln_g, ln_b, alpha, n_per_sample):
    t, d = x1.shape
    tb = MOE_COMBINE_TILE
    nblk = t // tb
    const = pl.BlockSpec((1, d), lambda i: (0, 0))
    assert g2.stride == 0 or n_per_sample % tb == 0, (n_per_sample, tb)
    return pl.pallas_call(
        functools.partial(_combine_ln2_kernel, alpha=alpha, tb=tb, nblk=nblk),
        out_shape=_sds((t, d), F32),
        grid=(nblk,),
        in_specs=[pl.BlockSpec((MOE_TOPK * tb,), lambda i: (i,), memory_space=pltpu.SMEM),
                  pl.BlockSpec((MOE_TOPK * tb,), lambda i: (jnp.minimum(i + 1, nblk - 1),), memory_space=pltpu.SMEM),
                  pl.BlockSpec((tb, d), lambda i: (i, 0)),
                  pl.BlockSpec(memory_space=pl.ANY),
                  pl.BlockSpec((tb, MOE_TOPK), lambda i: (i, 0)),
                  g2.spec(lambda i: (i * tb) // n_per_sample),
                  const, const],
        out_specs=pl.BlockSpec((tb, d), lambda i: (i, 0)),
        scratch_shapes=[pltpu.VMEM((2, MOE_TOPK, tb * TOKEN_TILE_ROWS, LANES), F32),
                        pltpu.SemaphoreType.DMA((2, MOE_TOPK))],
        compiler_params=_cparams("arbitrary"),
        name="moe_combine_ln2",
    )(dest, dest, x1, ys, wt, g2.table, ln_g, ln_b)


def _proj1_kernel(*refs, rope, with_q):
    x_ref, sc_ref, sh_ref, w_ref = refs[:4]
    refs = refs[4:]
    tabs = None
    if rope:
        tabs = (refs[0][...], refs[1][...], refs[2][...])
        refs = refs[3:]
    h = (x_ref[0] * (1.0 + sc_ref[0]) + sh_ref[0]).astype(BF16)
    r = jnp.dot(h, w_ref[...], preferred_element_type=F32)
    off = 0
    if with_q:
        q_ref, k_ref, v_ref = refs
        for hd in range(DIF_HEADS):
            q_ref[0, hd] = (_rot(r[:, hd * LANES:(hd + 1) * LANES], tabs) * (LOG2_E * HEAD_DIM ** -0.5)).astype(BF16)
        off = DIF_QK_W
    else:
        k_ref, v_ref = refs
    for hd in range(DIF_HEADS):
        k_ref[0, hd] = _rot(r[:, off + hd * LANES:off + (hd + 1) * LANES], tabs).astype(BF16)
        v_ref[0, hd] = r[:, off + DIF_QK_W + hd * LANES:off + DIF_QK_W + (hd + 1) * LANES].astype(BF16)


def _proj1(x, sc, sh, w_bf16, tabs, with_q, tm):
    b, n, d = x.shape
    rope = tabs is not None
    in_specs = [pl.BlockSpec((1, tm, d), lambda i, j: (i, j, 0)), sc.spec(), sh.spec(),
                pl.BlockSpec(w_bf16.shape, lambda i, j: (0, 0))]
    args = [x, sc.table, sh.table, w_bf16]
    if rope:
        in_specs += [pl.BlockSpec((tm, LANES), lambda i, j: (j, 0))] * 3
        args += list(tabs)
    hm = pl.BlockSpec((1, DIF_HEADS, tm, LANES), lambda i, j: (i, 0, j, 0))
    n_out = 3 if with_q else 2
    return pl.pallas_call(
        functools.partial(_proj1_kernel, rope=rope, with_q=with_q),
        out_shape=(_sds((b, DIF_HEADS, n, LANES), BF16),) * n_out,
        grid=(b, n // tm),
        in_specs=in_specs,
        out_specs=(hm,) * n_out,
        compiler_params=_cparams("parallel", "parallel"),
        name="proj1_qkv" if with_q else "proj1_kv_ctx",
    )(*args)


def _diff_kernel(lam_ref, q_ref, kl_ref, kc_ref, vl_ref, vc_ref, g_ref, o_ref, k_scr, v_scr, *, tq, n_lat, out_scale):
    @pl.when(pl.program_id(2) == 0)
    def _():
        k_scr[0:n_lat] = kl_ref[0, 0]
        k_scr[n_lat:] = kc_ref[0, 0]
        v_scr[0:n_lat, 0:LANES] = vl_ref[0, 0]
        v_scr[n_lat:, 0:LANES] = vc_ref[0, 0]
        v_scr[:, LANES:] = jnp.ones((v_scr.shape[0], LANES), BF16)

    lo = lax.broadcasted_iota(jnp.int32, (DIF_CHAIN_ROWS, LANES), 1) < HALF
    for r0 in range(0, tq, DIF_CHAIN_ROWS):
        q = q_ref[0, 0, r0:r0 + DIF_CHAIN_ROWS, :].astype(F32)
        maps = []
        for qm in (jnp.where(lo, q, 0.0), jnp.where(lo, 0.0, q)):
            s = _nt_dot(qm.astype(BF16), k_scr[...])
            p = jnp.exp2(s - jnp.max(s, -1, keepdims=True)).astype(BF16)
            oe = jnp.dot(p, v_scr[...], preferred_element_type=F32)
            maps.append(oe[:, 0:LANES] * (1.0 / oe[:, LANES:]))
        o = maps[0] - lam_ref[0] * maps[1]
        o = o * lax.rsqrt(jnp.mean(o * o, -1, keepdims=True) + RMS_EPS) * g_ref[...]
        o_ref[0, r0:r0 + DIF_CHAIN_ROWS, :] = (o * out_scale).astype(BF16)


def _diff_attention(lam, q, kl, kc, vl, vc, subln_g, lam_init, tq):
    b, nh, n, _ = q.shape
    nc = kc.shape[2]
    kv = lambda m: pl.BlockSpec((1, 1, m, LANES), lambda i, h, j: (i, h, 0, 0))
    return pl.pallas_call(
        functools.partial(_diff_kernel, tq=tq, n_lat=n, out_scale=1.0 - lam_init),
        out_shape=_sds((b, n, nh * LANES), BF16),
        grid=(b, nh, n // tq),
        in_specs=[pl.BlockSpec(memory_space=pltpu.SMEM),
                  pl.BlockSpec((1, 1, tq, LANES), lambda i, h, j: (i, h, j, 0)),
                  kv(n), kv(nc), kv(n), kv(nc),
                  pl.BlockSpec((1, LANES), lambda i, h, j: (0, 0))],
        out_specs=pl.BlockSpec((1, tq, LANES), lambda i, h, j: (i, j, h)),
        scratch_shapes=[pltpu.VMEM((n + nc, LANES), BF16), pltpu.VMEM((n + nc, 2 * LANES), BF16)],
        compiler_params=_cparams("parallel", "parallel", "arbitrary"),
        name="diff_attention",
    )(lam, q, kl, kc, vl, vc, subln_g)


def _moe_block(toks, logit_streams, layer, moe_w1, moe_w3, moe_w2):
    t = sum(tok.shape[0] for tok in toks) // TOKEN_TILE_ROWS
    ids, wts, rank, counts = _router(logit_streams, tm=512)
    dest, tile_expert, n_valid, pad, n_rows = _moe_plan(ids, rank, counts, t)
    dest = dest.T.reshape(-1)
    xs = _dispatch(toks, dest, pad, n_rows)
    flat = lambda w: w.reshape((-1,) + w.shape[2:])
    ys = _expert_ffn(tile_expert + layer * MOE_EXPERTS, n_valid, xs, flat(moe_w1), flat(moe_w3), flat(moe_w2))
    return ys, dest, wts.T


def kernel(x, c, ctx, c_ctx, mod_w, mod_b, ln1_g, ln1_b, ln2_g, ln2_b, swa_ssm_w_in, swa_ssm_w_out, swa_sink, ssm_a_re, ssm_a_im, ssm_log_step, ssm_b_re, ssm_b_im, ssm_c_re, ssm_c_im, ssm_d, ssm_glu_w, ssm_glu_b, dif_w_in, dif_w_out, dif_lam_q1, dif_lam_k1, dif_lam_q2, dif_lam_k2, dif_subln_g, moe_wg, moe_bg, moe_we, moe_be, moe_w1, moe_w3, moe_w2):
    bsz, n, d = x.shape
    ctx_len = ctx.shape[1]
    depth = mod_w.shape[0]
    alpha = (2 * depth) ** 0.25
    tabs = _rope_tables(n)

    n_vec = 16
    cvec = jnp.zeros((n_vec, d), F32).at[:bsz].set(c).at[bsz].set(c_ctx)
    mods = _modulation(cvec, mod_w, mod_b)
    mod_table = mods.reshape(depth * n_vec * 6, 1, d)

    xl, xc = x, ctx
    for layer in range(depth):
        need_ctx = layer < depth - 1
        i = layer // 2
        lat = [ModVec(mod_table, (layer * n_vec) * 6 + k, 6) for k in range(6)]
        cx = [ModVec(mod_table, (layer * n_vec + bsz) * 6 + k, 0) for k in range(6)]
        sh1, sc1, g1, sh2, sc2, g2 = lat
        csh1, csc1, cg1, csh2, csc2, cg2 = cx
        lg1, lb1 = ln1_g[layer].reshape(1, d), ln1_b[layer].reshape(1, d)
        lg2, lb2 = ln2_g[layer].reshape(1, d), ln2_b[layer].reshape(1, d)
        router_wb = _router_weights(moe_wg[layer], moe_bg[layer], moe_we[layer], moe_be[layer])
        if layer % 2 == 0:
            w_in = swa_ssm_w_in[i].astype(BF16)
            w_out = swa_ssm_w_out[i].astype(BF16)
            q, k, v, u = _proj0(xl, sc1, sh1, w_in, tabs, tm=512)
            qc, kc, vc, uc = _proj0(xc, csc1, csh1, w_in, None, tm=ctx_len)
            sink = swa_sink[i].astype(F32)
            att = _swa_attention(sink, q, k, v, kc, vc)
            mats = _ssm_matrices(ssm_a_re[i], ssm_a_im[i], ssm_log_step[i], ssm_b_re[i], ssm_b_im[i],
                                 ssm_c_re[i], ssm_c_im[i])
            ys, ysc = _ssm_scan(u, uc, mats)
            glu = (ssm_d[i].reshape(1, SSM_WIDTH).astype(F32), ssm_glu_w[i].astype(BF16),
                   ssm_glu_b[i].reshape(1, SSM_WIDTH).astype(F32))
            x1, tok, lgt = _post(att, (ys, u) + glu, w_out, xl, g1, lg1, lb1, sc2, sh2, router_wb, alpha, 512)
            toks, logit_streams = [tok], [lgt]
            if need_ctx:
                att_c = _ctx_attention(sink, qc, kc, vc)
                xc1, tok_c, lgt_c = _post(att_c, (ysc, uc) + glu, w_out, xc, cg1, lg1, lb1, csc2, csh2, router_wb,
                                          alpha, ctx_len)
                toks.append(tok_c)
                logit_streams.append(lgt_c)
        else:
            lam_init = 0.8 - 0.6 * math.exp(-0.3 * layer)
            w_in = dif_w_in[i].astype(BF16)
            w_out = dif_w_out[i].astype(BF16)
            q, k, v = _proj1(xl, sc1, sh1, w_in, tabs, True, tm=512)
            kc, vc = _proj1(xc, csc1, csh1, w_in[:, DIF_QK_W:], None, False, tm=ctx_len)
            lam = (jnp.exp(jnp.sum(dif_lam_q1[i].astype(F32) * dif_lam_k1[i].astype(F32)))
                   - jnp.exp(jnp.sum(dif_lam_q2[i].astype(F32) * dif_lam_k2[i].astype(F32))) + lam_init).reshape(1)
            att = _diff_attention(lam, q, k, kc, v, vc, dif_subln_g[i].reshape(1, DIF_V_HEAD).astype(F32),
                                  lam_init, tq=n)
            x1, tok, lgt = _post(att, None, w_out, xl, g1, lg1, lb1, sc2, sh2, router_wb, alpha, 512)
            toks, logit_streams = [tok], [lgt]
            if need_ctx:
                raise NotImplementedError("a differential-attention layer followed by another layer")
        ys_moe, dest, wt = _moe_block(toks, logit_streams, layer, moe_w1, moe_w3, moe_w2)
        xl = _combine_ln2(dest[:MOE_TOPK * bsz * n], x1.reshape(-1, d), ys_moe, wt[:bsz * n], g2, lg2, lb2, alpha,
                          n).reshape(bsz, n, d)
        if need_ctx:
            xc = _combine_ln2(dest[MOE_TOPK * bsz * n:], xc1.reshape(-1, d), ys_moe, wt[bsz * n:], cg2, lg2, lb2, alpha,
                              ctx_len).reshape(bsz, ctx_len, d)
    return xl
```

```python
import functools
import math
import typing

import jax
import jax.numpy as jnp
from jax import lax
from jax.experimental import pallas as pl
from jax.experimental.pallas import tpu as pltpu

F32 = jnp.float32
BF16 = jnp.bfloat16
HIGHEST = lax.Precision.HIGHEST

D_MODEL = 1024
GRID_W = 64
HEAD_DIM = 64
ROPE_BASE = 10000.0
ROPE_FREQS = HEAD_DIM // 4
LN_EPS = 1e-5
RMS_EPS = 1e-5
NEG_INF = -1e30
LOG2_E = math.log2(math.e)
LANES = 128
HALF = LANES // 2

SWA_HEADS = 8
SWA_KV_HEADS = 2
SWA_WINDOW = 128
SWA_BLOCK = 128
SWA_BLOCKS_PER_STEP = 4
SWA_Q_W = SWA_HEADS * HEAD_DIM
SWA_KV_W = SWA_KV_HEADS * HEAD_DIM

SSM_WIDTH = D_MODEL // 2
SSM_GROUP = 16
SSM_GROUPS = SSM_WIDTH // SSM_GROUP
SSM_STATE = 64
SSM_CHUNK = 16
SSM_CW = SSM_CHUNK * SSM_GROUP
SSM_SLAB_GROUPS = LANES // SSM_GROUP
SSM_GROUPS_PER_STEP = 4

DIF_HEADS = D_MODEL // (2 * HEAD_DIM)
DIF_QK_W = DIF_HEADS * 2 * HEAD_DIM
DIF_V_HEAD = 2 * HEAD_DIM
DIF_CHAIN_ROWS = 128

MOE_GROUPS = 4
MOE_EPG = 8
MOE_EXPERTS = MOE_GROUPS * MOE_EPG
MOE_TOPK = 2
SUBLANES = 8
ROUTER_EXPERT_ROW0 = SUBLANES
MOE_ROW_TILE = 512
MOE_DISPATCH_TILE = 2048
MOE_COMBINE_TILE = 512
TOKEN_TILE_ROWS = D_MODEL // LANES

VMEM_LIMIT = 56 * 1024 * 1024


def _cparams(*sem):
    return pltpu.CompilerParams(dimension_semantics=sem, vmem_limit_bytes=VMEM_LIMIT)


def _sds(shape, dtype):
    return jax.ShapeDtypeStruct(shape, dtype)


def _nt_dot(a, b):
    return lax.dot_general(a, b, (((1,), (1,)), ((), ())), preferred_element_type=F32)


def _layer_norm(r, g, b):
    mu = jnp.mean(r, -1, keepdims=True)
    rc = r - mu
    var = jnp.mean(rc * rc, -1, keepdims=True)
    return rc * lax.rsqrt(var + LN_EPS) * g + b


class ModVec(typing.NamedTuple):
    table: jax.Array
    row0: int
    stride: int

    def spec(self, sample_of_step=lambda i, *_: i):
        d = self.table.shape[-1]
        return pl.BlockSpec((1, 1, d), lambda *idx: (self.row0 + self.stride * sample_of_step(*idx), 0, 0))


def _mod_kernel(c_ref, w_ref, b_ref, o_ref):
    cv = c_ref[...]
    s = cv * jax.nn.sigmoid(cv)
    o_ref[0] = jnp.dot(s, w_ref[0], preferred_element_type=F32, precision=HIGHEST) + b_ref[0]


def _modulation(cvec, mod_w, mod_b):
    depth, d, w6 = mod_w.shape
    tn = 1536
    return pl.pallas_call(
        _mod_kernel,
        out_shape=_sds((depth, cvec.shape[0], w6), F32),
        grid=(depth, w6 // tn),
        in_specs=[pl.BlockSpec(cvec.shape, lambda l, j: (0, 0)),
                  pl.BlockSpec((1, d, tn), lambda l, j: (l, 0, j)),
                  pl.BlockSpec((1, 1, tn), lambda l, j: (l, 0, j))],
        out_specs=pl.BlockSpec((1, cvec.shape[0], tn), lambda l, j: (l, 0, j)),
        compiler_params=_cparams("arbitrary", "arbitrary"),
        name="modulation",
    )(cvec, mod_w, mod_b.reshape(depth, 1, w6))


def _rope_tables(n):
    rows = n // GRID_W
    row = jnp.repeat(jnp.arange(rows, dtype=F32), GRID_W)
    col = jnp.tile(jnp.arange(GRID_W, dtype=F32), rows)
    inv = ROPE_BASE ** (-jnp.arange(ROPE_FREQS, dtype=F32) / ROPE_FREQS)
    ang_r = row[:, None] * inv[None, :]
    ang_c = col[:, None] * inv[None, :]
    zeros = jnp.zeros_like(ang_r)
    cos64 = jnp.concatenate([jnp.cos(ang_r), jnp.cos(ang_r), jnp.cos(ang_c), jnp.cos(ang_c)], -1)
    sa64 = jnp.concatenate([-jnp.sin(ang_r), zeros, -jnp.sin(ang_c), zeros], -1)
    sb64 = jnp.concatenate([zeros, jnp.sin(ang_r), zeros, jnp.sin(ang_c)], -1)
    return tuple(jnp.tile(t, (1, LANES // HEAD_DIM)) for t in (cos64, sa64, sb64))


def _rot(t, tabs):
    if tabs is None:
        return t
    cos, sa, sb = tabs
    return t * cos + pltpu.roll(t, LANES - ROPE_FREQS, 1) * sa + pltpu.roll(t, ROPE_FREQS, 1) * sb


def _dup_halves(t):
    lo = lax.broadcasted_iota(jnp.int32, t.shape, 1) < HALF
    ta = jnp.where(lo, t, 0.0)
    tb = t - ta
    return ta + pltpu.roll(ta, HALF, 1), tb + pltpu.roll(tb, HALF, 1)


def _proj0_kernel(*refs, rope):
    if rope:
        x_ref, sc_ref, sh_ref, w_ref, cos_ref, sa_ref, sb_ref, q_ref, k_ref, v_ref, u_ref = refs
        tabs = (cos_ref[...], sa_ref[...], sb_ref[...])
    else:
        x_ref, sc_ref, sh_ref, w_ref, q_ref, k_ref, v_ref, u_ref = refs
        tabs = None
    h = (x_ref[0] * (1.0 + sc_ref[0]) + sh_ref[0]).astype(BF16)
    r = jnp.dot(h, w_ref[...], preferred_element_type=F32)
    scale = LOG2_E * HEAD_DIM ** -0.5
    for s in range(SWA_Q_W // LANES):
        q_ref[0, :, s * LANES:(s + 1) * LANES] = (_rot(r[:, s * LANES:(s + 1) * LANES], tabs) * scale).astype(BF16)
    k0, k1 = _dup_halves(_rot(r[:, SWA_Q_W:SWA_Q_W + LANES], tabs))
    vv = r[:, SWA_Q_W + LANES:SWA_Q_W + 2 * LANES]
    lo = lax.broadcasted_iota(jnp.int32, vv.shape, 1) < HALF
    k_ref[0, 0] = k0.astype(BF16)
    k_ref[0, 1] = k1.astype(BF16)
    v_ref[0, 0] = jnp.where(lo, vv, 1.0).astype(BF16)
    v_ref[0, 1] = jnp.where(lo, pltpu.roll(vv, HALF, 1), 1.0).astype(BF16)
    u_ref[0] = r[:, SWA_Q_W + 2 * LANES:]


def _proj0(x, sc, sh, w_bf16, tabs, tm):
    b, n, d = x.shape
    rope = tabs is not None
    in_specs = [pl.BlockSpec((1, tm, d), lambda i, j: (i, j, 0)), sc.spec(), sh.spec(),
                pl.BlockSpec(w_bf16.shape, lambda i, j: (0, 0))]
    args = [x, sc.table, sh.table, w_bf16]
    if rope:
        in_specs += [pl.BlockSpec((tm, LANES), lambda i, j: (j, 0))] * 3
        args += list(tabs)
    kv_spec = pl.BlockSpec((1, SWA_KV_HEADS, tm, LANES), lambda i, j: (i, 0, j, 0))
    return pl.pallas_call(
        functools.partial(_proj0_kernel, rope=rope),
        out_shape=(_sds((b, n, SWA_Q_W), BF16), _sds((b, SWA_KV_HEADS, n, LANES), BF16),
                   _sds((b, SWA_KV_HEADS, n, LANES), BF16), _sds((b, n, SSM_WIDTH), F32)),
        grid=(b, n // tm),
        in_specs=in_specs,
        out_specs=(pl.BlockSpec((1, tm, SWA_Q_W), lambda i, j: (i, j, 0)), kv_spec, kv_spec,
                   pl.BlockSpec((1, tm, SSM_WIDTH), lambda i, j: (i, j, 0))),
        compiler_params=_cparams("parallel", "parallel"),
        name="proj0_rope" if rope else "proj0_ctx",
    )(*args)


def _swa_kernel(*refs, tq, nsub, local, n_lat):
    if local:
        sink_ref, q_ref, k_ref, v_ref, kc_ref, vc_ref, o_ref = refs
    else:
        sink_ref, q_ref, kc_ref, vc_ref, o_ref = refs
    rows = 4 * tq
    lo = lax.broadcasted_iota(jnp.int32, (tq, LANES), 1) < HALF
    lo4 = lax.broadcasted_iota(jnp.int32, (rows, LANES), 1) < HALF
    rown = lax.broadcasted_iota(jnp.int32, (rows, 1), 0)
    for sub in range(nsub):
        j = pl.program_id(1) * nsub + sub
        r0 = sub * tq
        if local:
            span = 3 * SWA_BLOCK
            start = pl.multiple_of(jnp.clip((j - 1) * SWA_BLOCK, 0, n_lat - span), SWA_BLOCK)
            rr = lax.broadcasted_iota(jnp.int32, (rows, span), 0)
            cc = lax.broadcasted_iota(jnp.int32, (rows, span), 1)
            qpos = j * tq + (rr & (tq - 1))
            mask = jnp.abs(qpos - (start + cc)) <= SWA_WINDOW
        for h in range(SWA_KV_HEADS):
            qa = q_ref[0, r0:r0 + tq, (2 * h) * LANES:(2 * h + 1) * LANES].astype(F32)
            qb = q_ref[0, r0:r0 + tq, (2 * h + 1) * LANES:(2 * h + 2) * LANES].astype(F32)
            q4 = jnp.concatenate([jnp.where(lo, qa, 0.0), jnp.where(lo, 0.0, qa),
                                  jnp.where(lo, qb, 0.0), jnp.where(lo, 0.0, qb)], 0).astype(BF16)
            sink = LOG2_E * jnp.where(rown < tq, sink_ref[4 * h],
                                      jnp.where(rown < 2 * tq, sink_ref[4 * h + 1],
                                                jnp.where(rown < 3 * tq, sink_ref[4 * h + 2], sink_ref[4 * h + 3])))
            s_ctx = _nt_dot(q4, kc_ref[0, h])
            m = jnp.maximum(jnp.max(s_ctx, -1, keepdims=True), sink)
            if local:
                s_loc = jnp.where(mask, _nt_dot(q4, k_ref[0, h, pl.ds(start, span), :]), NEG_INF)
                m = jnp.maximum(m, jnp.max(s_loc, -1, keepdims=True))
            o4 = jnp.dot(jnp.exp2(s_ctx - m).astype(BF16), vc_ref[0, h], preferred_element_type=F32)
            if local:
                o4 = o4 + jnp.dot(jnp.exp2(s_loc - m).astype(BF16), v_ref[0, h, pl.ds(start, span), :],
                                  preferred_element_type=F32)
            o4 = o4 + jnp.where(lo4, 0.0, jnp.exp2(sink - m))
            o4 = o4 * (1.0 / jnp.where(lo4, pltpu.roll(o4, HALF, 1), 1.0))
            for s in range(2):
                even, odd = o4[2 * s * tq:(2 * s + 1) * tq], o4[(2 * s + 1) * tq:(2 * s + 2) * tq]
                o_ref[0, r0:r0 + tq, (2 * h + s) * LANES:(2 * h + s + 1) * LANES] = jnp.where(
                    lo, even, pltpu.roll(odd, HALF, 1)).astype(BF16)


def _swa_attention(sink, q, k, v, kc, vc):
    b, n, _ = q.shape
    nc = kc.shape[2]
    tq, nsub = SWA_BLOCK, SWA_BLOCKS_PER_STEP
    full = lambda m: pl.BlockSpec((1, SWA_KV_HEADS, m, LANES), lambda i, j: (i, 0, 0, 0))
    return pl.pallas_call(
        functools.partial(_swa_kernel, tq=tq, nsub=nsub, local=True, n_lat=n),
        out_shape=_sds((b, n, SWA_Q_W), BF16),
        grid=(b, n // (tq * nsub)),
        in_specs=[pl.BlockSpec(memory_space=pltpu.SMEM),
                  pl.BlockSpec((1, tq * nsub, SWA_Q_W), lambda i, j: (i, j, 0)),
                  full(n), full(n), full(nc), full(nc)],
        out_specs=pl.BlockSpec((1, tq * nsub, SWA_Q_W), lambda i, j: (i, j, 0)),
        compiler_params=_cparams("parallel", "arbitrary"),
        name="swa_attention",
    )(sink, q, k, v, kc, vc)


def _ctx_attention(sink, qc, kc, vc):
    b, nc, _ = qc.shape
    full = pl.BlockSpec((1, SWA_KV_HEADS, nc, LANES), lambda i, j: (i, 0, 0, 0))
    return pl.pallas_call(
        functools.partial(_swa_kernel, tq=nc, nsub=1, local=False, n_lat=0),
        out_shape=_sds((b, nc, SWA_Q_W), BF16),
        grid=(b, 1),
        in_specs=[pl.BlockSpec(memory_space=pltpu.SMEM),
                  pl.BlockSpec((1, nc, SWA_Q_W), lambda i, j: (i, 0, 0)), full, full],
        out_specs=pl.BlockSpec((1, nc, SWA_Q_W), lambda i, j: (i, 0, 0)),
        compiler_params=_cparams("parallel", "arbitrary"),
        name="ctx_attention",
    )(sink, qc, kc, vc)


def _cpow(e, lr, li):
    mag = jnp.exp(e * lr)
    return mag * jnp.cos(e * li), mag * jnp.sin(e * li)


def _s5_prep_kernel(row_ref, col_ref, bt_ref, ct_ref, ctt_ref, min_ref, mintra_ref, mst_ref, al_ref):
    L, C = SSM_CHUNK, SSM_GROUP
    ar, ai, dt = row_ref[0, 0:1, :], row_ref[0, 1:2, :], row_ref[0, 2:3, :]
    lr, li = dt * ar, dt * ai
    a1_re, a1_im = _cpow(1.0, lr, li)
    den = ar * ar + ai * ai
    nr = a1_re - 1.0
    coef_re = (nr * ar + a1_im * ai) / den
    coef_im = (a1_im * ar - nr * ai) / den
    b_re, b_im = bt_ref[0, 0], bt_ref[0, 1]
    bb_re = coef_re * b_re - coef_im * b_im
    bb_im = coef_re * b_im + coef_im * b_re
    c_re, c_im = ct_ref[0, 0], ct_ref[0, 1]
    fwd = lax.broadcasted_iota(jnp.int32, (L, LANES), 1) < HALF
    step = lax.broadcasted_iota(jnp.int32, (L, LANES), 0).astype(F32)
    pin_re, pin_im = _cpow(jnp.where(fwd, (L - 1.0) - step, step), lr, li)
    pst_re, pst_im = _cpow(jnp.where(fwd, step + 1.0, float(L) - step), lr, li)
    for j in range(L):
        pr, pi = pin_re[j:j + 1], pin_im[j:j + 1]
        min_ref[0, j * C:(j + 1) * C, 0:LANES] = (pr * bb_re - pi * bb_im).astype(BF16)
        min_ref[0, j * C:(j + 1) * C, LANES:] = (pr * bb_im + pi * bb_re).astype(BF16)
        pr, pi = pst_re[j:j + 1], pst_im[j:j + 1]
        mst_ref[0, j * C:(j + 1) * C, 0:LANES] = (pr * c_re - pi * c_im).astype(BF16)
        mst_ref[0, j * C:(j + 1) * C, LANES:] = (-(pr * c_im + pi * c_re)).astype(BF16)
    al_re, al_im = _cpow(float(L), lr, li)
    al_ref[0, 0:1, :] = al_re
    al_ref[0, 1:2, :] = al_im
    lag = (lax.broadcasted_iota(jnp.int32, (SSM_STATE, SSM_CW), 1) // C).astype(F32)
    rep = (lax.broadcasted_iota(jnp.int32, (C, SSM_CW), 1) % C
           == lax.broadcasted_iota(jnp.int32, (C, SSM_CW), 0)).astype(F32)
    ks = []
    for d in range(2):
        col = col_ref[0, d]
        p_re, p_im = _cpow(lag if d == 0 else (L - 1.0) - lag, col[:, 2:3] * col[:, 0:1], col[:, 2:3] * col[:, 1:2])
        ct_re = jnp.dot(ctt_ref[0, d, 0], rep, preferred_element_type=F32, precision=HIGHEST)
        ct_im = jnp.dot(ctt_ref[0, d, 1], rep, preferred_element_type=F32, precision=HIGHEST)
        ca_re = ct_re * p_re - ct_im * p_im
        ca_im = ct_re * p_im + ct_im * p_re
        sl = slice(d * HALF, (d + 1) * HALF)
        ks.append(jnp.dot(bb_re[:, sl], ca_re, preferred_element_type=F32, precision=HIGHEST)
                  - jnp.dot(bb_im[:, sl], ca_im, preferred_element_type=F32, precision=HIGHEST))
    zeros = jnp.zeros((C, SSM_CW), F32)
    lagged = (pltpu.roll(jnp.concatenate([ks[0], zeros], axis=1), (L - 1) * C, 1)
              + jnp.concatenate([ks[1], zeros], axis=1))
    for j in range(L):
        shift = (L - 1 - j) * C
        win = lagged if shift == 0 else pltpu.roll(lagged, 2 * SSM_CW - shift, 1)
        mintra_ref[0, j * C:(j + 1) * C, :] = win[:, 0:SSM_CW].astype(BF16)


def _ssm_matrices(a_re, a_im, log_step, b_re, b_im, c_re, c_im):
    G, P, C = SSM_GROUPS, SSM_STATE, SSM_GROUP
    f = lambda t: t.astype(F32)
    dt = jnp.exp(f(log_step))
    dtp = jnp.broadcast_to(dt[..., None], (2, G, P))
    row = jnp.stack([f(a_re), f(a_im), dtp], 0)
    row = jnp.transpose(row, (2, 0, 1, 3)).reshape(G, 3, 2 * P)
    col = jnp.transpose(jnp.stack([f(a_re), f(a_im), dtp], -1), (1, 0, 2, 3))
    lanes = lambda t: jnp.transpose(t, (1, 2, 0, 3)).reshape(G, C, 2 * P)
    bt = jnp.stack([lanes(jnp.transpose(f(b_re), (0, 1, 3, 2))), lanes(jnp.transpose(f(b_im), (0, 1, 3, 2)))], 1)
    ct = jnp.stack([lanes(f(c_re)), lanes(f(c_im))], 1)
    ctt = jnp.stack([jnp.transpose(f(c_re), (1, 0, 3, 2)), jnp.transpose(f(c_im), (1, 0, 3, 2))], 2)
    mat = pl.BlockSpec((1, SSM_CW, SSM_CW), lambda g: (g, 0, 0))
    return pl.pallas_call(
        _s5_prep_kernel,
        out_shape=(_sds((G, SSM_CW, SSM_CW), BF16),) * 3 + (_sds((G, 2, LANES), F32),),
        grid=(G,),
        in_specs=[pl.BlockSpec((1, 3, LANES), lambda g: (g, 0, 0)),
                  pl.BlockSpec((1, 2, P, 3), lambda g: (g, 0, 0, 0)),
                  pl.BlockSpec((1, 2, C, LANES), lambda g: (g, 0, 0, 0)),
                  pl.BlockSpec((1, 2, C, LANES), lambda g: (g, 0, 0, 0)),
                  pl.BlockSpec((1, 2, 2, P, C), lambda g: (g, 0, 0, 0, 0))],
        out_specs=(mat, mat, mat, pl.BlockSpec((1, 2, LANES), lambda g: (g, 0, 0))),
        compiler_params=_cparams("parallel"),
        name="s5_prep",
    )(row, col, bt, ct, ctt)


def _ssm_pack_kernel(uc_ref, u_ref, x_ref, *, nc_ctx, nc_lat):
    for src, row0, nch in ((uc_ref, 0, nc_ctx), (u_ref, nc_ctx, nc_lat)):
        steps = [src[0, pl.ds(j, nch, stride=SSM_CHUNK), :] for j in range(SSM_CHUNK)]
        lane = lax.broadcasted_iota(jnp.int32, (nch, LANES), 1)
        piece = [(lane >= jj * SSM_GROUP) & (lane < (jj + 1) * SSM_GROUP) for jj in range(SSM_SLAB_GROUPS)]
        for g in range(SSM_SLAB_GROUPS):
            for h in range(SSM_CW // LANES):
                acc = jnp.zeros((nch, LANES), F32)
                for jj in range(SSM_SLAB_GROUPS):
                    z = steps[h * SSM_SLAB_GROUPS + jj]
                    shift = ((jj - g) * SSM_GROUP) % LANES
                    acc = jnp.where(piece[jj], z if shift == 0 else pltpu.roll(z, shift, 1), acc)
                x_ref[g, row0:row0 + nch, h * LANES:(h + 1) * LANES] = acc.astype(BF16)


def _ssm_unpack_kernel(y_ref, oc_ref, o_ref, *, nc_ctx, nc_lat):
    for dst, row0, nch in ((oc_ref, 0, nc_ctx), (o_ref, nc_ctx, nc_lat)):
        lane = lax.broadcasted_iota(jnp.int32, (nch, LANES), 1)
        piece = [(lane >= g * SSM_GROUP) & (lane < (g + 1) * SSM_GROUP) for g in range(SSM_SLAB_GROUPS)]
        for i in range(SSM_CHUNK):
            h, ii = divmod(i, SSM_SLAB_GROUPS)
            acc = jnp.zeros((nch, LANES), F32)
            for g in range(SSM_SLAB_GROUPS):
                z = y_ref[g, row0:row0 + nch, h * LANES:(h + 1) * LANES]
                shift = ((g - ii) * SSM_GROUP) % LANES
                acc = jnp.where(piece[g], z if shift == 0 else pltpu.roll(z, shift, 1), acc)
            dst[0, pl.ds(i, nch, stride=SSM_CHUNK), :] = acc


def _ssm_kernel(x_ref, min_ref, mintra_ref, mstate_ref, al_ref, y_ref, v_scr, s_scr, *, nb, nc_ctx, n_chunks):
    n_grp = x_ref.shape[0]
    for g in range(n_grp):
        v = jnp.dot(x_ref[g], min_ref[g], preferred_element_type=F32)
        v_scr[g, 0] = v[:, 0:LANES]
        v_scr[g, 1] = v[:, LANES:]
    ar = [al_ref[g, 0:1, :] for g in range(n_grp)]
    ai = [al_ref[g, 1:2, :] for g in range(n_grp)]
    lo = lax.broadcasted_iota(jnp.int32, (nb, LANES), 1) < HALF

    def body(k, carry):
        kr = jnp.where(k < nc_ctx, nc_ctx - 1 - k, n_chunks - 1 + nc_ctx - k)
        rf = pl.ds(k, nb, stride=n_chunks)
        rr = pl.ds(kr, nb, stride=n_chunks)
        out = []
        for g in range(n_grp):
            sre, sim = carry[2 * g], carry[2 * g + 1]
            s_scr[g, 0, rf, :] = sre
            s_scr[g, 1, rr, :] = sre
            s_scr[g, 2, rf, :] = sim
            s_scr[g, 3, rr, :] = sim
            vre = jnp.where(lo, v_scr[g, 0, rf, :], v_scr[g, 0, rr, :])
            vim = jnp.where(lo, v_scr[g, 1, rf, :], v_scr[g, 1, rr, :])
            out += [ar[g] * sre - ai[g] * sim + vre, ar[g] * sim + ai[g] * sre + vim]
        return tuple(out)

    zero = jnp.zeros((nb, LANES), F32)
    lax.fori_loop(0, n_chunks, body, (zero,) * (2 * n_grp))
    lo_all = lax.broadcasted_iota(jnp.int32, (s_scr.shape[2], LANES), 1) < HALF
    for g in range(n_grp):
        s_in = jnp.concatenate([jnp.where(lo_all, s_scr[g, 0], s_scr[g, 1]),
                                jnp.where(lo_all, s_scr[g, 2], s_scr[g, 3])], axis=1).astype(BF16)
        y_ref[g] = jnp.dot(x_ref[g], mintra_ref[g], preferred_element_type=F32) + _nt_dot(s_in, mstate_ref[g])


def _ssm_scan(u, uc, mats):
    m_in, m_intra, m_state, a_l = mats
    b, n, _ = u.shape
    nc = uc.shape[1]
    nc_ctx, nc_lat = nc // SSM_CHUNK, n // SSM_CHUNK
    n_chunks = nc_ctx + nc_lat
    r = n_chunks * b
    nat = lambda m: pl.BlockSpec((1, m, LANES), lambda i, s: (i, 0, s))
    grp = pl.BlockSpec((SSM_SLAB_GROUPS, n_chunks, SSM_CW), lambda i, s: (s, i, 0))
    xg = pl.pallas_call(
        functools.partial(_ssm_pack_kernel, nc_ctx=nc_ctx, nc_lat=nc_lat),
        out_shape=_sds((SSM_GROUPS, r, SSM_CW), BF16),
        grid=(b, SSM_GROUPS // SSM_SLAB_GROUPS),
        in_specs=[nat(nc), nat(n)],
        out_specs=grp,
        compiler_params=_cparams("parallel", "parallel"),
        name="ssm_pack",
    )(uc, u)
    gs = SSM_GROUPS_PER_STEP
    mat = pl.BlockSpec((gs, SSM_CW, SSM_CW), lambda g: (g, 0, 0))
    yg = pl.pallas_call(
        functools.partial(_ssm_kernel, nb=b, nc_ctx=nc_ctx, n_chunks=n_chunks),
        out_shape=_sds((SSM_GROUPS, r, SSM_CW), F32),
        grid=(SSM_GROUPS // gs,),
        in_specs=[pl.BlockSpec((gs, r, SSM_CW), lambda g: (g, 0, 0)), mat, mat, mat,
                  pl.BlockSpec((gs, 2, LANES), lambda g: (g, 0, 0))],
        out_specs=pl.BlockSpec((gs, r, SSM_CW), lambda g: (g, 0, 0)),
        scratch_shapes=[pltpu.VMEM((gs, 2, r, LANES), F32), pltpu.VMEM((gs, 4, r, LANES), F32)],
        compiler_params=_cparams("parallel"),
        name="ssm_scan",
    )(xg, m_in, m_intra, m_state, a_l)
    ysc, ys = pl.pallas_call(
        functools.partial(_ssm_unpack_kernel, nc_ctx=nc_ctx, nc_lat=nc_lat),
        out_shape=(_sds((b, nc, SSM_WIDTH), F32), _sds((b, n, SSM_WIDTH), F32)),
        grid=(b, SSM_GROUPS // SSM_SLAB_GROUPS),
        in_specs=[grp],
        out_specs=(nat(nc), nat(n)),
        compiler_params=_cparams("parallel", "parallel"),
        name="ssm_unpack",
    )(yg)
    return ys, ysc


def _split_rows(ref, val):
    m = val.shape[0]
    for s in range(TOKEN_TILE_ROWS):
        ref[pl.ds(s, m, stride=TOKEN_TILE_ROWS), :] = val[:, s * LANES:(s + 1) * LANES]


def _merge_rows(ref):
    m = ref.shape[0] // TOKEN_TILE_ROWS
    return jnp.concatenate([ref[pl.ds(s, m, stride=TOKEN_TILE_ROWS), :] for s in range(TOKEN_TILE_ROWS)], axis=1)


def _token_tile(ref, t):
    return ref.at[pl.ds(pl.multiple_of(t * TOKEN_TILE_ROWS, TOKEN_TILE_ROWS), TOKEN_TILE_ROWS)]


def _router_logits(w, bias, h):
    w_hi, h_hi = w.astype(BF16), h.astype(BF16)
    w_lo, h_lo = (w - w_hi.astype(F32)).astype(BF16), (h - h_hi.astype(F32)).astype(BF16)
    return _nt_dot(w_hi, h_hi) + (_nt_dot(w_hi, h_lo) + _nt_dot(w_lo, h_hi)) + bias


def _post_kernel(*refs, alpha, with_ssm):
    wr_ref, br_ref = refs[-5:-3]
    refs = refs[:-5] + refs[-3:]
    if with_ssm:
        (att_ref, ys_ref, u_ref, dsk_ref, gw_ref, gb_ref, wo_ref, x_ref, g1_ref, lg_ref, lb_ref, sc2_ref, sh2_ref,
         x1_ref, h2_ref, lgt_ref) = refs
        y = ys_ref[0] + u_ref[0] * dsk_ref[...]
        gl = jax.nn.gelu(y)
        gate = jax.nn.sigmoid(jnp.dot(gl.astype(BF16), gw_ref[...], preferred_element_type=F32) + gb_ref[...])
        ssm = (gl * gate).astype(BF16)
        mix = (jnp.dot(att_ref[0], wo_ref[0:SWA_Q_W, :], preferred_element_type=F32)
               + jnp.dot(ssm, wo_ref[SWA_Q_W:, :], preferred_element_type=F32))
    else:
        att_ref, wo_ref, x_ref, g1_ref, lg_ref, lb_ref, sc2_ref, sh2_ref, x1_ref, h2_ref, lgt_ref = refs
        mix = jnp.dot(att_ref[0], wo_ref[...], preferred_element_type=F32)
    x1 = _layer_norm(alpha * x_ref[0] + g1_ref[0] * mix, lg_ref[...], lb_ref[...])
    x1_ref[0] = x1
    h2 = x1 * (1.0 + sc2_ref[0]) + sh2_ref[0]
    _split_rows(h2_ref, h2)
    lgt_ref[...] = _router_logits(wr_ref[...], br_ref[...], h2)


def _post(att, ssm_args, w_out_bf16, x, g1, ln_g, ln_b, sc2, sh2, router_wb, alpha, tm):
    b, n, d = x.shape
    tok = lambda w: pl.BlockSpec((1, tm, w), lambda i, j: (i, j, 0))
    const = lambda a: pl.BlockSpec(a.shape, lambda i, j: (0,) * a.ndim)
    in_specs = [tok(att.shape[-1])]
    args = [att]
    if ssm_args is not None:
        ys, u, dsk, gw, gb = ssm_args
        in_specs += [tok(SSM_WIDTH), tok(SSM_WIDTH), const(dsk), const(gw), const(gb)]
        args += [ys, u, dsk, gw, gb]
    in_specs += [const(w_out_bf16), tok(d), g1.spec(), const(ln_g), const(ln_b), sc2.spec(), sh2.spec()]
    args += [w_out_bf16, x, g1.table, ln_g, ln_b, sc2.table, sh2.table]
    in_specs += [const(a) for a in router_wb]
    args += list(router_wb)
    per_b = n // tm
    n_logit = router_wb[0].shape[0]
    return pl.pallas_call(
        functools.partial(_post_kernel, alpha=alpha, with_ssm=ssm_args is not None),
        out_shape=(_sds((b, n, d), F32), _sds((b * n * TOKEN_TILE_ROWS, LANES), F32), _sds((n_logit, b * n), F32)),
        grid=(b, n // tm),
        in_specs=in_specs,
        out_specs=(tok(d), pl.BlockSpec((tm * TOKEN_TILE_ROWS, LANES), lambda i, j: (i * per_b + j, 0)),
                   pl.BlockSpec((n_logit, tm), lambda i, j: (0, i * per_b + j))),
        compiler_params=_cparams("parallel", "parallel"),
        name="post_mixer_ssm" if ssm_args is not None else "post_mixer",
    )(*args)


def _first_max(v, sub):
    m = jnp.max(v, 0, keepdims=True)
    idx = jnp.min(jnp.where(v == m, sub, float(SUBLANES)), 0, keepdims=True)
    return m, idx


def _stream_blocks(toks, tb):
    starts = [0]
    for t in toks:
        starts.append(starts[-1] + t.shape[0] // (tb * TOKEN_TILE_ROWS))
    return starts


def _stream_spec(tok, tb, start):
    last = tok.shape[0] // (tb * TOKEN_TILE_ROWS) - 1
    return pl.BlockSpec((tb * TOKEN_TILE_ROWS, LANES), lambda i, *_: (jnp.clip(i - start, 0, last), 0))


def _router_kernel(*refs, starts):
    n_streams = len(starts) - 1
    lgt_refs = refs[:n_streams]
    ids_ref, wts_ref, rank_ref, cnt_ref, carry_scr = refs[n_streams:]
    step = pl.program_id(0)

    @pl.when(step == 0)
    def _():
        carry_scr[...] = jnp.zeros_like(carry_scr)

    logits = lgt_refs[0][...]
    for ref, start in zip(lgt_refs[1:], starts[1:]):
        logits = jnp.where(step >= start, ref[...], logits)
    tm = logits.shape[1]
    sub = lax.broadcasted_iota(jnp.int32, (SUBLANES, tm), 0).astype(F32)
    gl = logits[0:SUBLANES]
    gmax, gi = _first_max(gl, sub)
    gp = 1.0 / jnp.sum(jnp.exp(gl - gmax), 0, keepdims=True)
    le = logits[ROUTER_EXPERT_ROW0:ROUTER_EXPERT_ROW0 + MOE_EPG]
    for g in range(1, MOE_GROUPS):
        le = jnp.where(gi == float(g), logits[ROUTER_EXPERT_ROW0 + g * MOE_EPG:ROUTER_EXPERT_ROW0 + (g + 1) * MOE_EPG], le)
    m1, i1 = _first_max(le, sub)
    m2, i2 = _first_max(jnp.where(sub == i1, NEG_INF, le), sub)
    t = jnp.exp(m2 - m1)
    e1 = gi * float(MOE_EPG) + i1
    e2 = gi * float(MOE_EPG) + i2
    ids_ref[0:1, :] = e1.astype(jnp.int32)
    ids_ref[1:2, :] = e2.astype(jnp.int32)
    wts_ref[0:1, :] = gp / (1.0 + t)
    wts_ref[1:2, :] = gp * t / (1.0 + t)
    esub = lax.broadcasted_iota(jnp.int32, (MOE_EXPERTS, tm), 0).astype(F32)
    oh1 = (esub == e1).astype(F32)
    oh2 = (esub == e2).astype(F32)
    both = oh1 + oh2
    earlier = (lax.broadcasted_iota(jnp.int32, (tm, tm), 0) < lax.broadcasted_iota(jnp.int32, (tm, tm), 1))
    prefix = jnp.dot(both.astype(BF16), earlier.astype(BF16), preferred_element_type=F32) + carry_scr[...]
    rank_ref[0:1, :] = jnp.sum(oh1 * prefix, 0, keepdims=True).astype(jnp.int32)
    rank_ref[1:2, :] = jnp.sum(oh2 * prefix, 0, keepdims=True).astype(jnp.int32)
    carry_scr[...] += jnp.sum(both, 1, keepdims=True)
    cnt_ref[...] = jnp.broadcast_to(carry_scr[...], cnt_ref.shape)


def _router_weights(wg, bg, we, be):
    d = wg.shape[0]
    rows = ROUTER_EXPERT_ROW0 + MOE_EXPERTS
    w = jnp.zeros((rows, d), F32)
    w = w.at[:MOE_GROUPS].set(wg.T)
    w = w.at[ROUTER_EXPERT_ROW0:].set(jnp.transpose(we, (0, 2, 1)).reshape(MOE_EXPERTS, d))
    bias = jnp.full((rows, 1), NEG_INF, F32)
    bias = bias.at[:MOE_GROUPS, 0].set(bg)
    bias = bias.at[ROUTER_EXPERT_ROW0:, 0].set(be.reshape(-1))
    return w, bias


def _router(logit_streams, tm):
    starts = [0]
    for lg in logit_streams:
        starts.append(starts[-1] + lg.shape[1] // tm)
    t = starts[-1] * tm
    rows = logit_streams[0].shape[0]
    stream_spec = lambda lg, s0: pl.BlockSpec(
        (rows, tm), lambda i: (0, jnp.clip(i - s0, 0, lg.shape[1] // tm - 1)))
    pair = pl.BlockSpec((MOE_TOPK, tm), lambda i: (0, i))
    ids, wts, rank, cnt = pl.pallas_call(
        functools.partial(_router_kernel, starts=starts),
        out_shape=(_sds((MOE_TOPK, t), jnp.int32), _sds((MOE_TOPK, t), F32), _sds((MOE_TOPK, t), jnp.int32),
                   _sds((MOE_EXPERTS, LANES), F32)),
        grid=(t // tm,),
        in_specs=[stream_spec(lg, s0) for lg, s0 in zip(logit_streams, starts)],
        out_specs=(pair, pair, pair, pl.BlockSpec((MOE_EXPERTS, LANES), lambda i: (0, 0))),
        scratch_shapes=[pltpu.VMEM((MOE_EXPERTS, 1), F32)],
        compiler_params=_cparams("arbitrary"),
        name="moe_router",
    )(*logit_streams)
    return ids, wts, rank, cnt[:, 0].astype(jnp.int32)


def _moe_plan(ids, rank, counts, n_tok):
    tm = MOE_ROW_TILE
    padded = ((counts + tm - 1) // tm) * tm
    ends = jnp.cumsum(padded)
    offs = ends - padded
    experts = jnp.arange(MOE_EXPERTS, dtype=jnp.int32)
    dest = (jnp.sum(jnp.where(ids[..., None] == experts, offs, 0), -1) + rank).astype(jnp.int32)
    n_tiles = (MOE_TOPK * n_tok + MOE_EXPERTS * (tm - 1)) // tm
    starts = jnp.arange(n_tiles, dtype=jnp.int32) * tm
    tile_expert = jnp.minimum(jnp.sum((ends[None, :] <= starts[:, None]).astype(jnp.int32), -1), MOE_EXPERTS - 1)
    n_valid = (ends[-1] // tm).astype(jnp.int32).reshape(1)
    pad = jnp.stack([offs + counts, padded - counts, jnp.broadcast_to(n_valid, counts.shape)]).astype(jnp.int32)
    return dest, tile_expert, n_valid, pad, n_tiles * tm


def _pad_fill(pad_ref, zero_scr, xs_ref, sem, wait):
    def per_expert(e, carry):
        first, count = pad_ref[0, e], pad_ref[1, e]
        piece = MOE_ROW_TILE // 2
        while piece >= 1:
            row = first + (count & ~(2 * piece - 1))
            n = piece * TOKEN_TILE_ROWS

            @pl.when((count & piece) != 0)
            def _(row=row, n=n):
                copy = pltpu.make_async_copy(
                    zero_scr.at[pl.ds(0, n)],
                    xs_ref.at[pl.ds(pl.multiple_of(row * TOKEN_TILE_ROWS, TOKEN_TILE_ROWS), n)], sem)
                copy.wait() if wait else copy.start()

            piece //= 2
        return carry

    lax.fori_loop(0, MOE_EXPERTS, per_expert, 0)
    half = MOE_ROW_TILE // 2 * TOKEN_TILE_ROWS
    n_tiles = xs_ref.shape[0] // (2 * half)

    def per_tile(t, carry):
        for h in range(2):
            copy = pltpu.make_async_copy(zero_scr, xs_ref.at[pl.ds(pl.multiple_of((2 * t + h) * half, half), half)],
                                         sem)
            copy.wait() if wait else copy.start()
        return carry

    lax.fori_loop(pad_ref[2, 0], n_tiles, per_tile, 0)


def _dispatch_kernel(*refs, tb, starts):
    n_streams = len(starts) - 1
    dest_ref, pad_ref = refs[:2]
    tok_refs = refs[2:2 + n_streams]
    xs_ref, zero_scr, sem, pad_sem = refs[2 + n_streams:]
    step = pl.program_id(0)

    @pl.when(step == 0)
    def _():
        zero_scr[...] = jnp.zeros_like(zero_scr)
        _pad_fill(pad_ref, zero_scr, xs_ref, pad_sem, wait=False)

    for s, tok_ref in enumerate(tok_refs):
        @pl.when((step >= starts[s]) & (step < starts[s + 1]))
        def _(tok_ref=tok_ref):
            def body(r, carry):
                for k in range(MOE_TOPK):
                    pltpu.make_async_copy(_token_tile(tok_ref, r), _token_tile(xs_ref, dest_ref[MOE_TOPK * r + k]),
                                          sem).start(priority=k % 2)
                return carry

            lax.fori_loop(0, tb, body, 0, unroll=8)

    for k in range(MOE_TOPK):
        pltpu.make_async_copy(tok_refs[0], xs_ref.at[pl.ds(0, tb * TOKEN_TILE_ROWS)], sem).wait()

    @pl.when(step == 0)
    def _():
        _pad_fill(pad_ref, zero_scr, xs_ref, pad_sem, wait=True)


def _dispatch(toks, dest, pad, n_rows):
    tb = MOE_DISPATCH_TILE
    nblk = dest.shape[0] // (MOE_TOPK * tb)
    starts = _stream_blocks(toks, tb)
    return pl.pallas_call(
        functools.partial(_dispatch_kernel, tb=tb, starts=starts),
        out_shape=_sds((n_rows * TOKEN_TILE_ROWS, LANES), F32),
        grid=(nblk,),
        in_specs=[pl.BlockSpec((MOE_TOPK * tb,), lambda i: (i,), memory_space=pltpu.SMEM),
                  pl.BlockSpec(memory_space=pltpu.SMEM)]
        + [_stream_spec(tok, tb, s0) for tok, s0 in zip(toks, starts)],
        out_specs=pl.BlockSpec(memory_space=pl.ANY),
        scratch_shapes=[pltpu.VMEM((MOE_ROW_TILE // 2 * TOKEN_TILE_ROWS, LANES), F32),
                        pltpu.SemaphoreType.DMA(()), pltpu.SemaphoreType.DMA(())],
        compiler_params=_cparams("arbitrary"),
        name="moe_dispatch",
    )(dest, pad, *toks)


def _ffn_kernel(te_ref, nv_ref, x_ref, w1_ref, w3_ref, w2_ref, y_ref, w13_scr, w2_scr):
    i = pl.program_id(0)
    f = w1_ref.shape[2]

    @pl.when((i == 0) | (te_ref[i] != te_ref[jnp.maximum(i - 1, 0)]))
    def _():
        w13_scr[:, 0:f] = w1_ref[0].astype(BF16)
        w13_scr[:, f:2 * f] = w3_ref[0].astype(BF16)
        w2_scr[...] = w2_ref[0].astype(BF16)

    @pl.when(i < nv_ref[0])
    def _():
        h13 = jnp.dot(_merge_rows(x_ref).astype(BF16), w13_scr[...], preferred_element_type=F32)
        h1 = h13[:, 0:f]
        hh = (h1 * jax.nn.sigmoid(h1) * h13[:, f:2 * f]).astype(BF16)
        _split_rows(y_ref, jnp.dot(hh, w2_scr[...], preferred_element_type=F32))

    @pl.when(i >= nv_ref[0])
    def _():
        y_ref[...] = jnp.zeros_like(y_ref)


def _expert_ffn(tile_expert, n_valid, xs, w1, w3, w2):
    p = xs.shape[0] // TOKEN_TILE_ROWS
    _, d, f = w1.shape
    tm = MOE_ROW_TILE
    rows = pl.BlockSpec((tm * TOKEN_TILE_ROWS, LANES), lambda i, te, nv: (i, 0))
    rows_in = pl.BlockSpec((tm * TOKEN_TILE_ROWS, LANES), lambda i, te, nv: (jnp.minimum(i, nv[0] - 1), 0))
    return pl.pallas_call(
        _ffn_kernel,
        out_shape=_sds(xs.shape, F32),
        grid_spec=pltpu.PrefetchScalarGridSpec(
            num_scalar_prefetch=2,
            grid=(p // tm,),
            in_specs=[rows_in,
                      pl.BlockSpec((1, d, f), lambda i, te, nv: (te[i], 0, 0)),
                      pl.BlockSpec((1, d, f), lambda i, te, nv: (te[i], 0, 0)),
                      pl.BlockSpec((1, f, d), lambda i, te, nv: (te[i], 0, 0))],
            out_specs=rows,
            scratch_shapes=[pltpu.VMEM((d, 2 * f), BF16), pltpu.VMEM((f, d), BF16)]),
        compiler_params=_cparams("arbitrary"),
        name="moe_expert_ffn",
    )(tile_expert, n_valid, xs, w1, w3, w2)


def _combine_ln2_kernel(dest_ref, next_ref, x1_ref, ys_ref, wt_ref, g2_ref, lg_ref, lb_ref, o_ref, buf, sem, *,
                        alpha, tb, nblk):
    step = pl.program_id(0)

    def gather(d_ref, slot):
        def body(r, carry):
            for k in range(MOE_TOPK):
                pltpu.make_async_copy(_token_tile(ys_ref, d_ref[MOE_TOPK * r + k]), _token_tile(buf.at[slot, k], r),
                                      sem.at[slot, k]).start(priority=k % 2)
            return carry

        lax.fori_loop(0, tb, body, 0, unroll=8)

    @pl.when(step == 0)
    def _():
        gather(dest_ref, 0)

    @pl.when(step + 1 < nblk)
    def _():
        gather(next_ref, (step + 1) % 2)

    slot = step % 2
    for k in range(MOE_TOPK):
        pltpu.make_async_copy(ys_ref.at[pl.ds(0, tb * TOKEN_TILE_ROWS)], buf.at[slot, k], sem.at[slot, k]).wait()
    f = wt_ref[:, 0:1] * _merge_rows(buf.at[slot, 0]) + wt_ref[:, 1:2] * _merge_rows(buf.at[slot, 1])
    o_ref[...] = _layer_norm(alpha * x1_ref[...] + g2_ref[0] * f, lg_ref[...], lb_ref[...])


def _combine_ln2(dest, x1, ys, wt, g2, ln_g, ln_b, alpha, n_per_sample):
    t, d = x1.shape
    tb = MOE_COMBINE_TILE
    nblk = t // tb
    const = pl.BlockSpec((1, d), lambda i: (0, 0))
    assert g2.stride == 0 or n_per_sample % tb == 0, (n_per_sample, tb)
    return pl.pallas_call(
        functools.partial(_combine_ln2_kernel, alpha=alpha, tb=tb, nblk=nblk),
        out_shape=_sds((t, d), F32),
        grid=(nblk,),
        in_specs=[pl.BlockSpec((MOE_TOPK * tb,), lambda i: (i,), memory_space=pltpu.SMEM),
                  pl.BlockSpec((MOE_TOPK * tb,), lambda i: (jnp.minimum(i + 1, nblk - 1),), memory_space=pltpu.SMEM),
                  pl.BlockSpec((tb, d), lambda i: (i, 0)),
                  pl.BlockSpec(memory_space=pl.ANY),
                  pl.BlockSpec((tb, MOE_TOPK), lambda i: (i, 0)),
                  g2.spec(lambda i: (i * tb) // n_per_sample),
                  const, const],
        out_specs=pl.BlockSpec((tb, d), lambda i: (i, 0)),
        scratch_shapes=[pltpu.VMEM((2, MOE_TOPK, tb * TOKEN_TILE_ROWS, LANES), F32),
                        pltpu.SemaphoreType.DMA((2, MOE_TOPK))],
        compiler_params=_cparams("arbitrary"),
        name="moe_combine_ln2",
    )(dest, dest, x1, ys, wt, g2.table, ln_g, ln_b)


def _proj1_kernel(*refs, rope, with_q):
    x_ref, sc_ref, sh_ref, w_ref = refs[:4]
    refs = refs[4:]
    tabs = None
    if rope:
        tabs = (refs[0][...], refs[1][...], refs[2][...])
        refs = refs[3:]
    h = (x_ref[0] * (1.0 + sc_ref[0]) + sh_ref[0]).astype(BF16)
    r = jnp.dot(h, w_ref[...], preferred_element_type=F32)
    off = 0
    if with_q:
        q_ref, k_ref, v_ref = refs
        for hd in range(DIF_HEADS):
            q_ref[0, hd] = (_rot(r[:, hd * LANES:(hd + 1) * LANES], tabs) * (LOG2_E * HEAD_DIM ** -0.5)).astype(BF16)
        off = DIF_QK_W
    else:
        k_ref, v_ref = refs
    for hd in range(DIF_HEADS):
        k_ref[0, hd] = _rot(r[:, off + hd * LANES:off + (hd + 1) * LANES], tabs).astype(BF16)
        v_ref[0, hd] = r[:, off + DIF_QK_W + hd * LANES:off + DIF_QK_W + (hd + 1) * LANES].astype(BF16)


def _proj1(x, sc, sh, w_bf16, tabs, with_q, tm):
    b, n, d = x.shape
    rope = tabs is not None
    in_specs = [pl.BlockSpec((1, tm, d), lambda i, j: (i, j, 0)), sc.spec(), sh.spec(),
                pl.BlockSpec(w_bf16.shape, lambda i, j: (0, 0))]
    args = [x, sc.table, sh.table, w_bf16]
    if rope:
        in_specs += [pl.BlockSpec((tm, LANES), lambda i, j: (j, 0))] * 3
        args += list(tabs)
    hm = pl.BlockSpec((1, DIF_HEADS, tm, LANES), lambda i, j: (i, 0, j, 0))
    n_out = 3 if with_q else 2
    return pl.pallas_call(
        functools.partial(_proj1_kernel, rope=rope, with_q=with_q),
        out_shape=(_sds((b, DIF_HEADS, n, LANES), BF16),) * n_out,
        grid=(b, n // tm),
        in_specs=in_specs,
        out_specs=(hm,) * n_out,
        compiler_params=_cparams("parallel", "parallel"),
        name="proj1_qkv" if with_q else "proj1_kv_ctx",
    )(*args)


def _diff_kernel(lam_ref, q_ref, kl_ref, kc_ref, vl_ref, vc_ref, g_ref, o_ref, k_scr, v_scr, *, tq, n_lat, out_scale):
    @pl.when(pl.program_id(2) == 0)
    def _():
        k_scr[0:n_lat] = kl_ref[0, 0]
        k_scr[n_lat:] = kc_ref[0, 0]
        v_scr[0:n_lat, 0:LANES] = vl_ref[0, 0]
        v_scr[n_lat:, 0:LANES] = vc_ref[0, 0]
        v_scr[:, LANES:] = jnp.ones((v_scr.shape[0], LANES), BF16)

    lo = lax.broadcasted_iota(jnp.int32, (DIF_CHAIN_ROWS, LANES), 1) < HALF
    for r0 in range(0, tq, DIF_CHAIN_ROWS):
        q = q_ref[0, 0, r0:r0 + DIF_CHAIN_ROWS, :].astype(F32)
        maps = []
        for qm in (jnp.where(lo, q, 0.0), jnp.where(lo, 0.0, q)):
            s = _nt_dot(qm.astype(BF16), k_scr[...])
            p = jnp.exp2(s - jnp.max(s, -1, keepdims=True)).astype(BF16)
            oe = jnp.dot(p, v_scr[...], preferred_element_type=F32)
            maps.append(oe[:, 0:LANES] * (1.0 / oe[:, LANES:]))
        o = maps[0] - lam_ref[0] * maps[1]
        o = o * lax.rsqrt(jnp.mean(o * o, -1, keepdims=True) + RMS_EPS) * g_ref[...]
        o_ref[0, r0:r0 + DIF_CHAIN_ROWS, :] = (o * out_scale).astype(BF16)


def _diff_attention(lam, q, kl, kc, vl, vc, subln_g, lam_init, tq):
    b, nh, n, _ = q.shape
    nc = kc.shape[2]
    kv = lambda m: pl.BlockSpec((1, 1, m, LANES), lambda i, h, j: (i, h, 0, 0))
    return pl.pallas_call(
        functools.partial(_diff_kernel, tq=tq, n_lat=n, out_scale=1.0 - lam_init),
        out_shape=_sds((b, n, nh * LANES), BF16),
        grid=(b, nh, n // tq),
        in_specs=[pl.BlockSpec(memory_space=pltpu.SMEM),
                  pl.BlockSpec((1, 1, tq, LANES), lambda i, h, j: (i, h, j, 0)),
                  kv(n), kv(nc), kv(n), kv(nc),
                  pl.BlockSpec((1, LANES), lambda i, h, j: (0, 0))],
        out_specs=pl.BlockSpec((1, tq, LANES), lambda i, h, j: (i, j, h)),
        scratch_shapes=[pltpu.VMEM((n + nc, LANES), BF16), pltpu.VMEM((n + nc, 2 * LANES), BF16)],
        compiler_params=_cparams("parallel", "parallel", "arbitrary"),
        name="diff_attention",
    )(lam, q, kl, kc, vl, vc, subln_g)


def _moe_block(toks, logit_streams, layer, moe_w1, moe_w3, moe_w2):
    t = sum(tok.shape[0] for tok in toks) // TOKEN_TILE_ROWS
    ids, wts, rank, counts = _router(logit_streams, tm=512)
    dest, tile_expert, n_valid, pad, n_rows = _moe_plan(ids, rank, counts, t)
    dest = dest.T.reshape(-1)
    xs = _dispatch(toks, dest, pad, n_rows)
    flat = lambda w: w.reshape((-1,) + w.shape[2:])
    ys = _expert_ffn(tile_expert + layer * MOE_EXPERTS, n_valid, xs, flat(moe_w1), flat(moe_w3), flat(moe_w2))
    return ys, dest, wts.T


def kernel(x, c, ctx, c_ctx, mod_w, mod_b, ln1_g, ln1_b, ln2_g, ln2_b, swa_ssm_w_in, swa_ssm_w_out, swa_sink, ssm_a_re, ssm_a_im, ssm_log_step, ssm_b_re, ssm_b_im, ssm_c_re, ssm_c_im, ssm_d, ssm_glu_w, ssm_glu_b, dif_w_in, dif_w_out, dif_lam_q1, dif_lam_k1, dif_lam_q2, dif_lam_k2, dif_subln_g, moe_wg, moe_bg, moe_we, moe_be, moe_w1, moe_w3, moe_w2):
    bsz, n, d = x.shape
    ctx_len = ctx.shape[1]
    depth = mod_w.shape[0]
    alpha = (2 * depth) ** 0.25
    tabs = _rope_tables(n)

    n_vec = 16
    cvec = jnp.zeros((n_vec, d), F32).at[:bsz].set(c).at[bsz].set(c_ctx)
    mods = _modulation(cvec, mod_w, mod_b)
    mod_table = mods.reshape(depth * n_vec * 6, 1, d)

    xl, xc = x, ctx
    for layer in range(depth):
        need_ctx = layer < depth - 1
        i = layer // 2
        lat = [ModVec(mod_table, (layer * n_vec) * 6 + k, 6) for k in range(6)]
        cx = [ModVec(mod_table, (layer * n_vec + bsz) * 6 + k, 0) for k in range(6)]
        sh1, sc1, g1, sh2, sc2, g2 = lat
        csh1, csc1, cg1, csh2, csc2, cg2 = cx
        lg1, lb1 = ln1_g[layer].reshape(1, d), ln1_b[layer].reshape(1, d)
        lg2, lb2 = ln2_g[layer].reshape(1, d), ln2_b[layer].reshape(1, d)
        router_wb = _router_weights(moe_wg[layer], moe_bg[layer], moe_we[layer], moe_be[layer])
        if layer % 2 == 0:
            w_in = swa_ssm_w_in[i].astype(BF16)
            w_out = swa_ssm_w_out[i].astype(BF16)
            q, k, v, u = _proj0(xl, sc1, sh1, w_in, tabs, tm=512)
            qc, kc, vc, uc = _proj0(xc, csc1, csh1, w_in, None, tm=ctx_len)
            sink = swa_sink[i].astype(F32)
            att = _swa_attention(sink, q, k, v, kc, vc)
            mats = _ssm_matrices(ssm_a_re[i], ssm_a_im[i], ssm_log_step[i], ssm_b_re[i], ssm_b_im[i],
                                 ssm_c_re[i], ssm_c_im[i])
            ys, ysc = _ssm_scan(u, uc, mats)
            glu = (ssm_d[i].reshape(1, SSM_WIDTH).astype(F32), ssm_glu_w[i].astype(BF16),
                   ssm_glu_b[i].reshape(1, SSM_WIDTH).astype(F32))
            x1, tok, lgt = _post(att, (ys, u) + glu, w_out, xl, g1, lg1, lb1, sc2, sh2, router_wb, alpha, 512)
            toks, logit_streams = [tok], [lgt]
            if need_ctx:
                att_c = _ctx_attention(sink, qc, kc, vc)
                xc1, tok_c, lgt_c = _post(att_c, (ysc, uc) + glu, w_out, xc, cg1, lg1, lb1, csc2, csh2, router_wb,
                                          alpha, ctx_len)
                toks.append(tok_c)
                logit_streams.append(lgt_c)
        else:
            lam_init = 0.8 - 0.6 * math.exp(-0.3 * layer)
            w_in = dif_w_in[i].astype(BF16)
            w_out = dif_w_out[i].astype(BF16)
            q, k, v = _proj1(xl, sc1, sh1, w_in, tabs, True, tm=512)
            kc, vc = _proj1(xc, csc1, csh1, w_in[:, DIF_QK_W:], None, False, tm=ctx_len)
            lam = (jnp.exp(jnp.sum(dif_lam_q1[i].astype(F32) * dif_lam_k1[i].astype(F32)))
                   - jnp.exp(jnp.sum(dif_lam_q2[i].astype(F32) * dif_lam_k2[i].astype(F32))) + lam_init).reshape(1)
            att = _diff_attention(lam, q, k, kc, v, vc, dif_subln_g[i].reshape(1, DIF_V_HEAD).astype(F32),
                                  lam_init, tq=n)
            x1, tok, lgt = _post(att, None, w_out, xl, g1, lg1, lb1, sc2, sh2, router_wb, alpha, 1024)
            toks, logit_streams = [tok], [lgt]
            if need_ctx:
                raise NotImplementedError("a differential-attention layer followed by another layer")
        ys_moe, dest, wt = _moe_block(toks, logit_streams, layer, moe_w1, moe_w3, moe_w2)
        xl = _combine_ln2(dest[:MOE_TOPK * bsz * n], x1.reshape(-1, d), ys_moe, wt[:bsz * n], g2, lg2, lb2, alpha,
                          n).reshape(bsz, n, d)
        if need_ctx:
            xc = _combine_ln2(dest[MOE_TOPK * bsz * n:], xc1.reshape(-1, d), ys_moe, wt[bsz * n:], cg2, lg2, lb2, alpha,
                              ctx_len).reshape(bsz, ctx_len, d)
    return xl
```

```python
import functools
import math
import typing

import jax
import jax.numpy as jnp
from jax import lax
from jax.experimental import pallas as pl
from jax.experimental.pallas import tpu as pltpu

F32 = jnp.float32
BF16 = jnp.bfloat16
HIGHEST = lax.Precision.HIGHEST

D_MODEL = 1024
GRID_W = 64
HEAD_DIM = 64
ROPE_BASE = 10000.0
ROPE_FREQS = HEAD_DIM // 4
LN_EPS = 1e-5
RMS_EPS = 1e-5
NEG_INF = -1e30
LOG2_E = math.log2(math.e)
LANES = 128
HALF = LANES // 2

SWA_HEADS = 8
SWA_KV_HEADS = 2
SWA_WINDOW = 128
SWA_BLOCK = 128
SWA_BLOCKS_PER_STEP = 4
SWA_Q_W = SWA_HEADS * HEAD_DIM
SWA_KV_W = SWA_KV_HEADS * HEAD_DIM

SSM_WIDTH = D_MODEL // 2
SSM_GROUP = 16
SSM_GROUPS = SSM_WIDTH // SSM_GROUP
SSM_STATE = 64
SSM_CHUNK = 16
SSM_CW = SSM_CHUNK * SSM_GROUP
SSM_SLAB_GROUPS = LANES // SSM_GROUP
SSM_GROUPS_PER_STEP = 4

DIF_HEADS = D_MODEL // (2 * HEAD_DIM)
DIF_QK_W = DIF_HEADS * 2 * HEAD_DIM
DIF_V_HEAD = 2 * HEAD_DIM
DIF_CHAIN_ROWS = 128

MOE_GROUPS = 4
MOE_EPG = 8
MOE_EXPERTS = MOE_GROUPS * MOE_EPG
MOE_TOPK = 2
SUBLANES = 8
ROUTER_EXPERT_ROW0 = SUBLANES
MOE_ROW_TILE = 1024
MOE_DISPATCH_TILE = 2048
MOE_COMBINE_TILE = 512
TOKEN_TILE_ROWS = D_MODEL // LANES

VMEM_LIMIT = 56 * 1024 * 1024


def _cparams(*sem):
    return pltpu.CompilerParams(dimension_semantics=sem, vmem_limit_bytes=VMEM_LIMIT)


def _sds(shape, dtype):
    return jax.ShapeDtypeStruct(shape, dtype)


def _nt_dot(a, b):
    return lax.dot_general(a, b, (((1,), (1,)), ((), ())), preferred_element_type=F32)


def _layer_norm(r, g, b):
    mu = jnp.mean(r, -1, keepdims=True)
    rc = r - mu
    var = jnp.mean(rc * rc, -1, keepdims=True)
    return rc * lax.rsqrt(var + LN_EPS) * g + b


class ModVec(typing.NamedTuple):
    table: jax.Array
    row0: int
    stride: int

    def spec(self, sample_of_step=lambda i, *_: i):
        d = self.table.shape[-1]
        return pl.BlockSpec((1, 1, d), lambda *idx: (self.row0 + self.stride * sample_of_step(*idx), 0, 0))


def _mod_kernel(c_ref, w_ref, b_ref, o_ref):
    cv = c_ref[...]
    s = cv * jax.nn.sigmoid(cv)
    o_ref[0] = jnp.dot(s, w_ref[0], preferred_element_type=F32, precision=HIGHEST) + b_ref[0]


def _modulation(cvec, mod_w, mod_b):
    depth, d, w6 = mod_w.shape
    tn = 1536
    return pl.pallas_call(
        _mod_kernel,
        out_shape=_sds((depth, cvec.shape[0], w6), F32),
        grid=(depth, w6 // tn),
        in_specs=[pl.BlockSpec(cvec.shape, lambda l, j: (0, 0)),
                  pl.BlockSpec((1, d, tn), lambda l, j: (l, 0, j)),
                  pl.BlockSpec((1, 1, tn), lambda l, j: (l, 0, j))],
        out_specs=pl.BlockSpec((1, cvec.shape[0], tn), lambda l, j: (l, 0, j)),
        compiler_params=_cparams("arbitrary", "arbitrary"),
        name="modulation",
    )(cvec, mod_w, mod_b.reshape(depth, 1, w6))


def _rope_tables(n):
    rows = n // GRID_W
    row = jnp.repeat(jnp.arange(rows, dtype=F32), GRID_W)
    col = jnp.tile(jnp.arange(GRID_W, dtype=F32), rows)
    inv = ROPE_BASE ** (-jnp.arange(ROPE_FREQS, dtype=F32) / ROPE_FREQS)
    ang_r = row[:, None] * inv[None, :]
    ang_c = col[:, None] * inv[None, :]
    zeros = jnp.zeros_like(ang_r)
    cos64 = jnp.concatenate([jnp.cos(ang_r), jnp.cos(ang_r), jnp.cos(ang_c), jnp.cos(ang_c)], -1)
    sa64 = jnp.concatenate([-jnp.sin(ang_r), zeros, -jnp.sin(ang_c), zeros], -1)
    sb64 = jnp.concatenate([zeros, jnp.sin(ang_r), zeros, jnp.sin(ang_c)], -1)
    return tuple(jnp.tile(t, (1, LANES // HEAD_DIM)) for t in (cos64, sa64, sb64))


def _rot(t, tabs):
    if tabs is None:
        return t
    cos, sa, sb = tabs
    return t * cos + pltpu.roll(t, LANES - ROPE_FREQS, 1) * sa + pltpu.roll(t, ROPE_FREQS, 1) * sb


def _dup_halves(t):
    lo = lax.broadcasted_iota(jnp.int32, t.shape, 1) < HALF
    ta = jnp.where(lo, t, 0.0)
    tb = t - ta
    return ta + pltpu.roll(ta, HALF, 1), tb + pltpu.roll(tb, HALF, 1)


def _proj0_kernel(*refs, rope):
    if rope:
        x_ref, sc_ref, sh_ref, w_ref, cos_ref, sa_ref, sb_ref, q_ref, k_ref, v_ref, u_ref = refs
        tabs = (cos_ref[...], sa_ref[...], sb_ref[...])
    else:
        x_ref, sc_ref, sh_ref, w_ref, q_ref, k_ref, v_ref, u_ref = refs
        tabs = None
    h = (x_ref[0] * (1.0 + sc_ref[0]) + sh_ref[0]).astype(BF16)
    r = jnp.dot(h, w_ref[...], preferred_element_type=F32)
    scale = LOG2_E * HEAD_DIM ** -0.5
    for s in range(SWA_Q_W // LANES):
        q_ref[0, :, s * LANES:(s + 1) * LANES] = (_rot(r[:, s * LANES:(s + 1) * LANES], tabs) * scale).astype(BF16)
    k0, k1 = _dup_halves(_rot(r[:, SWA_Q_W:SWA_Q_W + LANES], tabs))
    vv = r[:, SWA_Q_W + LANES:SWA_Q_W + 2 * LANES]
    lo = lax.broadcasted_iota(jnp.int32, vv.shape, 1) < HALF
    k_ref[0, 0] = k0.astype(BF16)
    k_ref[0, 1] = k1.astype(BF16)
    v_ref[0, 0] = jnp.where(lo, vv, 1.0).astype(BF16)
    v_ref[0, 1] = jnp.where(lo, pltpu.roll(vv, HALF, 1), 1.0).astype(BF16)
    u_ref[0] = r[:, SWA_Q_W + 2 * LANES:]


def _proj0(x, sc, sh, w_bf16, tabs, tm):
    b, n, d = x.shape
    rope = tabs is not None
    in_specs = [pl.BlockSpec((1, tm, d), lambda i, j: (i, j, 0)), sc.spec(), sh.spec(),
                pl.BlockSpec(w_bf16.shape, lambda i, j: (0, 0))]
    args = [x, sc.table, sh.table, w_bf16]
    if rope:
        in_specs += [pl.BlockSpec((tm, LANES), lambda i, j: (j, 0))] * 3
        args += list(tabs)
    kv_spec = pl.BlockSpec((1, SWA_KV_HEADS, tm, LANES), lambda i, j: (i, 0, j, 0))
    return pl.pallas_call(
        functools.partial(_proj0_kernel, rope=rope),
        out_shape=(_sds((b, n, SWA_Q_W), BF16), _sds((b, SWA_KV_HEADS, n, LANES), BF16),
                   _sds((b, SWA_KV_HEADS, n, LANES), BF16), _sds((b, n, SSM_WIDTH), F32)),
        grid=(b, n // tm),
        in_specs=in_specs,
        out_specs=(pl.BlockSpec((1, tm, SWA_Q_W), lambda i, j: (i, j, 0)), kv_spec, kv_spec,
                   pl.BlockSpec((1, tm, SSM_WIDTH), lambda i, j: (i, j, 0))),
        compiler_params=_cparams("parallel", "parallel"),
        name="proj0_rope" if rope else "proj0_ctx",
    )(*args)


def _swa_kernel(*refs, tq, nsub, local, n_lat):
    if local:
        sink_ref, q_ref, k_ref, v_ref, kc_ref, vc_ref, o_ref = refs
    else:
        sink_ref, q_ref, kc_ref, vc_ref, o_ref = refs
    rows = 4 * tq
    lo = lax.broadcasted_iota(jnp.int32, (tq, LANES), 1) < HALF
    lo4 = lax.broadcasted_iota(jnp.int32, (rows, LANES), 1) < HALF
    rown = lax.broadcasted_iota(jnp.int32, (rows, 1), 0)
    for sub in range(nsub):
        j = pl.program_id(1) * nsub + sub
        r0 = sub * tq
        if local:
            span = 3 * SWA_BLOCK
            start = pl.multiple_of(jnp.clip((j - 1) * SWA_BLOCK, 0, n_lat - span), SWA_BLOCK)
            rr = lax.broadcasted_iota(jnp.int32, (rows, span), 0)
            cc = lax.broadcasted_iota(jnp.int32, (rows, span), 1)
            qpos = j * tq + (rr & (tq - 1))
            mask = jnp.abs(qpos - (start + cc)) <= SWA_WINDOW
        for h in range(SWA_KV_HEADS):
            qa = q_ref[0, r0:r0 + tq, (2 * h) * LANES:(2 * h + 1) * LANES].astype(F32)
            qb = q_ref[0, r0:r0 + tq, (2 * h + 1) * LANES:(2 * h + 2) * LANES].astype(F32)
            q4 = jnp.concatenate([jnp.where(lo, qa, 0.0), jnp.where(lo, 0.0, qa),
                                  jnp.where(lo, qb, 0.0), jnp.where(lo, 0.0, qb)], 0).astype(BF16)
            sink = LOG2_E * jnp.where(rown < tq, sink_ref[4 * h],
                                      jnp.where(rown < 2 * tq, sink_ref[4 * h + 1],
                                                jnp.where(rown < 3 * tq, sink_ref[4 * h + 2], sink_ref[4 * h + 3])))
            s_ctx = _nt_dot(q4, kc_ref[0, h])
            m = jnp.maximum(jnp.max(s_ctx, -1, keepdims=True), sink)
            if local:
                s_loc = jnp.where(mask, _nt_dot(q4, k_ref[0, h, pl.ds(start, span), :]), NEG_INF)
                m = jnp.maximum(m, jnp.max(s_loc, -1, keepdims=True))
            o4 = jnp.dot(jnp.exp2(s_ctx - m).astype(BF16), vc_ref[0, h], preferred_element_type=F32)
            if local:
                o4 = o4 + jnp.dot(jnp.exp2(s_loc - m).astype(BF16), v_ref[0, h, pl.ds(start, span), :],
                                  preferred_element_type=F32)
            o4 = o4 + jnp.where(lo4, 0.0, jnp.exp2(sink - m))
            o4 = o4 * (1.0 / jnp.where(lo4, pltpu.roll(o4, HALF, 1), 1.0))
            for s in range(2):
                even, odd = o4[2 * s * tq:(2 * s + 1) * tq], o4[(2 * s + 1) * tq:(2 * s + 2) * tq]
                o_ref[0, r0:r0 + tq, (2 * h + s) * LANES:(2 * h + s + 1) * LANES] = jnp.where(
                    lo, even, pltpu.roll(odd, HALF, 1)).astype(BF16)


def _swa_attention(sink, q, k, v, kc, vc):
    b, n, _ = q.shape
    nc = kc.shape[2]
    tq, nsub = SWA_BLOCK, SWA_BLOCKS_PER_STEP
    full = lambda m: pl.BlockSpec((1, SWA_KV_HEADS, m, LANES), lambda i, j: (i, 0, 0, 0))
    return pl.pallas_call(
        functools.partial(_swa_kernel, tq=tq, nsub=nsub, local=True, n_lat=n),
        out_shape=_sds((b, n, SWA_Q_W), BF16),
        grid=(b, n // (tq * nsub)),
        in_specs=[pl.BlockSpec(memory_space=pltpu.SMEM),
                  pl.BlockSpec((1, tq * nsub, SWA_Q_W), lambda i, j: (i, j, 0)),
                  full(n), full(n), full(nc), full(nc)],
        out_specs=pl.BlockSpec((1, tq * nsub, SWA_Q_W), lambda i, j: (i, j, 0)),
        compiler_params=_cparams("parallel", "arbitrary"),
        name="swa_attention",
    )(sink, q, k, v, kc, vc)


def _ctx_attention(sink, qc, kc, vc):
    b, nc, _ = qc.shape
    full = pl.BlockSpec((1, SWA_KV_HEADS, nc, LANES), lambda i, j: (i, 0, 0, 0))
    return pl.pallas_call(
        functools.partial(_swa_kernel, tq=nc, nsub=1, local=False, n_lat=0),
        out_shape=_sds((b, nc, SWA_Q_W), BF16),
        grid=(b, 1),
        in_specs=[pl.BlockSpec(memory_space=pltpu.SMEM),
                  pl.BlockSpec((1, nc, SWA_Q_W), lambda i, j: (i, 0, 0)), full, full],
        out_specs=pl.BlockSpec((1, nc, SWA_Q_W), lambda i, j: (i, 0, 0)),
        compiler_params=_cparams("parallel", "arbitrary"),
        name="ctx_attention",
    )(sink, qc, kc, vc)


def _cpow(e, lr, li):
    mag = jnp.exp(e * lr)
    return mag * jnp.cos(e * li), mag * jnp.sin(e * li)


def _s5_prep_kernel(row_ref, col_ref, bt_ref, ct_ref, ctt_ref, min_ref, mintra_ref, mst_ref, al_ref):
    L, C = SSM_CHUNK, SSM_GROUP
    ar, ai, dt = row_ref[0, 0:1, :], row_ref[0, 1:2, :], row_ref[0, 2:3, :]
    lr, li = dt * ar, dt * ai
    a1_re, a1_im = _cpow(1.0, lr, li)
    den = ar * ar + ai * ai
    nr = a1_re - 1.0
    coef_re = (nr * ar + a1_im * ai) / den
    coef_im = (a1_im * ar - nr * ai) / den
    b_re, b_im = bt_ref[0, 0], bt_ref[0, 1]
    bb_re = coef_re * b_re - coef_im * b_im
    bb_im = coef_re * b_im + coef_im * b_re
    c_re, c_im = ct_ref[0, 0], ct_ref[0, 1]
    fwd = lax.broadcasted_iota(jnp.int32, (L, LANES), 1) < HALF
    step = lax.broadcasted_iota(jnp.int32, (L, LANES), 0).astype(F32)
    pin_re, pin_im = _cpow(jnp.where(fwd, (L - 1.0) - step, step), lr, li)
    pst_re, pst_im = _cpow(jnp.where(fwd, step + 1.0, float(L) - step), lr, li)
    for j in range(L):
        pr, pi = pin_re[j:j + 1], pin_im[j:j + 1]
        min_ref[0, j * C:(j + 1) * C, 0:LANES] = (pr * bb_re - pi * bb_im).astype(BF16)
        min_ref[0, j * C:(j + 1) * C, LANES:] = (pr * bb_im + pi * bb_re).astype(BF16)
        pr, pi = pst_re[j:j + 1], pst_im[j:j + 1]
        mst_ref[0, j * C:(j + 1) * C, 0:LANES] = (pr * c_re - pi * c_im).astype(BF16)
        mst_ref[0, j * C:(j + 1) * C, LANES:] = (-(pr * c_im + pi * c_re)).astype(BF16)
    al_re, al_im = _cpow(float(L), lr, li)
    al_ref[0, 0:1, :] = al_re
    al_ref[0, 1:2, :] = al_im
    lag = (lax.broadcasted_iota(jnp.int32, (SSM_STATE, SSM_CW), 1) // C).astype(F32)
    rep = (lax.broadcasted_iota(jnp.int32, (C, SSM_CW), 1) % C
           == lax.broadcasted_iota(jnp.int32, (C, SSM_CW), 0)).astype(F32)
    ks = []
    for d in range(2):
        col = col_ref[0, d]
        p_re, p_im = _cpow(lag if d == 0 else (L - 1.0) - lag, col[:, 2:3] * col[:, 0:1], col[:, 2:3] * col[:, 1:2])
        ct_re = jnp.dot(ctt_ref[0, d, 0], rep, preferred_element_type=F32, precision=HIGHEST)
        ct_im = jnp.dot(ctt_ref[0, d, 1], rep, preferred_element_type=F32, precision=HIGHEST)
        ca_re = ct_re * p_re - ct_im * p_im
        ca_im = ct_re * p_im + ct_im * p_re
        sl = slice(d * HALF, (d + 1) * HALF)
        ks.append(jnp.dot(bb_re[:, sl], ca_re, preferred_element_type=F32, precision=HIGHEST)
                  - jnp.dot(bb_im[:, sl], ca_im, preferred_element_type=F32, precision=HIGHEST))
    zeros = jnp.zeros((C, SSM_CW), F32)
    lagged = (pltpu.roll(jnp.concatenate([ks[0], zeros], axis=1), (L - 1) * C, 1)
              + jnp.concatenate([ks[1], zeros], axis=1))
    for j in range(L):
        shift = (L - 1 - j) * C
        win = lagged if shift == 0 else pltpu.roll(lagged, 2 * SSM_CW - shift, 1)
        mintra_ref[0, j * C:(j + 1) * C, :] = win[:, 0:SSM_CW].astype(BF16)


def _ssm_matrices(a_re, a_im, log_step, b_re, b_im, c_re, c_im):
    G, P, C = SSM_GROUPS, SSM_STATE, SSM_GROUP
    f = lambda t: t.astype(F32)
    dt = jnp.exp(f(log_step))
    dtp = jnp.broadcast_to(dt[..., None], (2, G, P))
    row = jnp.stack([f(a_re), f(a_im), dtp], 0)
    row = jnp.transpose(row, (2, 0, 1, 3)).reshape(G, 3, 2 * P)
    col = jnp.transpose(jnp.stack([f(a_re), f(a_im), dtp], -1), (1, 0, 2, 3))
    lanes = lambda t: jnp.transpose(t, (1, 2, 0, 3)).reshape(G, C, 2 * P)
    bt = jnp.stack([lanes(jnp.transpose(f(b_re), (0, 1, 3, 2))), lanes(jnp.transpose(f(b_im), (0, 1, 3, 2)))], 1)
    ct = jnp.stack([lanes(f(c_re)), lanes(f(c_im))], 1)
    ctt = jnp.stack([jnp.transpose(f(c_re), (1, 0, 3, 2)), jnp.transpose(f(c_im), (1, 0, 3, 2))], 2)
    mat = pl.BlockSpec((1, SSM_CW, SSM_CW), lambda g: (g, 0, 0))
    return pl.pallas_call(
        _s5_prep_kernel,
        out_shape=(_sds((G, SSM_CW, SSM_CW), BF16),) * 3 + (_sds((G, 2, LANES), F32),),
        grid=(G,),
        in_specs=[pl.BlockSpec((1, 3, LANES), lambda g: (g, 0, 0)),
                  pl.BlockSpec((1, 2, P, 3), lambda g: (g, 0, 0, 0)),
                  pl.BlockSpec((1, 2, C, LANES), lambda g: (g, 0, 0, 0)),
                  pl.BlockSpec((1, 2, C, LANES), lambda g: (g, 0, 0, 0)),
                  pl.BlockSpec((1, 2, 2, P, C), lambda g: (g, 0, 0, 0, 0))],
        out_specs=(mat, mat, mat, pl.BlockSpec((1, 2, LANES), lambda g: (g, 0, 0))),
        compiler_params=_cparams("parallel"),
        name="s5_prep",
    )(row, col, bt, ct, ctt)


def _ssm_pack_kernel(uc_ref, u_ref, x_ref, *, nc_ctx, nc_lat):
    for src, row0, nch in ((uc_ref, 0, nc_ctx), (u_ref, nc_ctx, nc_lat)):
        steps = [src[0, pl.ds(j, nch, stride=SSM_CHUNK), :] for j in range(SSM_CHUNK)]
        lane = lax.broadcasted_iota(jnp.int32, (nch, LANES), 1)
        piece = [(lane >= jj * SSM_GROUP) & (lane < (jj + 1) * SSM_GROUP) for jj in range(SSM_SLAB_GROUPS)]
        for g in range(SSM_SLAB_GROUPS):
            for h in range(SSM_CW // LANES):
                acc = jnp.zeros((nch, LANES), F32)
                for jj in range(SSM_SLAB_GROUPS):
                    z = steps[h * SSM_SLAB_GROUPS + jj]
                    shift = ((jj - g) * SSM_GROUP) % LANES
                    acc = jnp.where(piece[jj], z if shift == 0 else pltpu.roll(z, shift, 1), acc)
                x_ref[g, row0:row0 + nch, h * LANES:(h + 1) * LANES] = acc.astype(BF16)


def _ssm_unpack_kernel(y_ref, oc_ref, o_ref, *, nc_ctx, nc_lat):
    for dst, row0, nch in ((oc_ref, 0, nc_ctx), (o_ref, nc_ctx, nc_lat)):
        lane = lax.broadcasted_iota(jnp.int32, (nch, LANES), 1)
        piece = [(lane >= g * SSM_GROUP) & (lane < (g + 1) * SSM_GROUP) for g in range(SSM_SLAB_GROUPS)]
        for i in range(SSM_CHUNK):
            h, ii = divmod(i, SSM_SLAB_GROUPS)
            acc = jnp.zeros((nch, LANES), F32)
            for g in range(SSM_SLAB_GROUPS):
                z = y_ref[g, row0:row0 + nch, h * LANES:(h + 1) * LANES]
                shift = ((g - ii) * SSM_GROUP) % LANES
                acc = jnp.where(piece[g], z if shift == 0 else pltpu.roll(z, shift, 1), acc)
            dst[0, pl.ds(i, nch, stride=SSM_CHUNK), :] = acc


def _ssm_kernel(x_ref, min_ref, mintra_ref, mstate_ref, al_ref, y_ref, v_scr, s_scr, *, nb, nc_ctx, n_chunks):
    n_grp = x_ref.shape[0]
    for g in range(n_grp):
        v = jnp.dot(x_ref[g], min_ref[g], preferred_element_type=F32)
        v_scr[g, 0] = v[:, 0:LANES]
        v_scr[g, 1] = v[:, LANES:]
    ar = [al_ref[g, 0:1, :] for g in range(n_grp)]
    ai = [al_ref[g, 1:2, :] for g in range(n_grp)]
    lo = lax.broadcasted_iota(jnp.int32, (nb, LANES), 1) < HALF

    def body(k, carry):
        kr = jnp.where(k < nc_ctx, nc_ctx - 1 - k, n_chunks - 1 + nc_ctx - k)
        rf = pl.ds(k, nb, stride=n_chunks)
        rr = pl.ds(kr, nb, stride=n_chunks)
        out = []
        for g in range(n_grp):
            sre, sim = carry[2 * g], carry[2 * g + 1]
            s_scr[g, 0, rf, :] = sre
            s_scr[g, 1, rr, :] = sre
            s_scr[g, 2, rf, :] = sim
            s_scr[g, 3, rr, :] = sim
            vre = jnp.where(lo, v_scr[g, 0, rf, :], v_scr[g, 0, rr, :])
            vim = jnp.where(lo, v_scr[g, 1, rf, :], v_scr[g, 1, rr, :])
            out += [ar[g] * sre - ai[g] * sim + vre, ar[g] * sim + ai[g] * sre + vim]
        return tuple(out)

    zero = jnp.zeros((nb, LANES), F32)
    lax.fori_loop(0, n_chunks, body, (zero,) * (2 * n_grp))
    lo_all = lax.broadcasted_iota(jnp.int32, (s_scr.shape[2], LANES), 1) < HALF
    for g in range(n_grp):
        s_in = jnp.concatenate([jnp.where(lo_all, s_scr[g, 0], s_scr[g, 1]),
                                jnp.where(lo_all, s_scr[g, 2], s_scr[g, 3])], axis=1).astype(BF16)
        y_ref[g] = jnp.dot(x_ref[g], mintra_ref[g], preferred_element_type=F32) + _nt_dot(s_in, mstate_ref[g])


def _ssm_scan(u, uc, mats):
    m_in, m_intra, m_state, a_l = mats
    b, n, _ = u.shape
    nc = uc.shape[1]
    nc_ctx, nc_lat = nc // SSM_CHUNK, n // SSM_CHUNK
    n_chunks = nc_ctx + nc_lat
    r = n_chunks * b
    nat = lambda m: pl.BlockSpec((1, m, LANES), lambda i, s: (i, 0, s))
    grp = pl.BlockSpec((SSM_SLAB_GROUPS, n_chunks, SSM_CW), lambda i, s: (s, i, 0))
    xg = pl.pallas_call(
        functools.partial(_ssm_pack_kernel, nc_ctx=nc_ctx, nc_lat=nc_lat),
        out_shape=_sds((SSM_GROUPS, r, SSM_CW), BF16),
        grid=(b, SSM_GROUPS // SSM_SLAB_GROUPS),
        in_specs=[nat(nc), nat(n)],
        out_specs=grp,
        compiler_params=_cparams("parallel", "parallel"),
        name="ssm_pack",
    )(uc, u)
    gs = SSM_GROUPS_PER_STEP
    mat = pl.BlockSpec((gs, SSM_CW, SSM_CW), lambda g: (g, 0, 0))
    yg = pl.pallas_call(
        functools.partial(_ssm_kernel, nb=b, nc_ctx=nc_ctx, n_chunks=n_chunks),
        out_shape=_sds((SSM_GROUPS, r, SSM_CW), F32),
        grid=(SSM_GROUPS // gs,),
        in_specs=[pl.BlockSpec((gs, r, SSM_CW), lambda g: (g, 0, 0)), mat, mat, mat,
                  pl.BlockSpec((gs, 2, LANES), lambda g: (g, 0, 0))],
        out_specs=pl.BlockSpec((gs, r, SSM_CW), lambda g: (g, 0, 0)),
        scratch_shapes=[pltpu.VMEM((gs, 2, r, LANES), F32), pltpu.VMEM((gs, 4, r, LANES), F32)],
        compiler_params=_cparams("parallel"),
        name="ssm_scan",
    )(xg, m_in, m_intra, m_state, a_l)
    ysc, ys = pl.pallas_call(
        functools.partial(_ssm_unpack_kernel, nc_ctx=nc_ctx, nc_lat=nc_lat),
        out_shape=(_sds((b, nc, SSM_WIDTH), F32), _sds((b, n, SSM_WIDTH), F32)),
        grid=(b, SSM_GROUPS // SSM_SLAB_GROUPS),
        in_specs=[grp],
        out_specs=(nat(nc), nat(n)),
        compiler_params=_cparams("parallel", "parallel"),
        name="ssm_unpack",
    )(yg)
    return ys, ysc


def _split_rows(ref, val):
    m = val.shape[0]
    for s in range(TOKEN_TILE_ROWS):
        ref[pl.ds(s, m, stride=TOKEN_TILE_ROWS), :] = val[:, s * LANES:(s + 1) * LANES]


def _merge_rows(ref):
    m = ref.shape[0] // TOKEN_TILE_ROWS
    return jnp.concatenate([ref[pl.ds(s, m, stride=TOKEN_TILE_ROWS), :] for s in range(TOKEN_TILE_ROWS)], axis=1)


def _token_tile(ref, t):
    return ref.at[pl.ds(pl.multiple_of(t * TOKEN_TILE_ROWS, TOKEN_TILE_ROWS), TOKEN_TILE_ROWS)]


def _router_logits(w, bias, h):
    w_hi, h_hi = w.astype(BF16), h.astype(BF16)
    w_lo, h_lo = (w - w_hi.astype(F32)).astype(BF16), (h - h_hi.astype(F32)).astype(BF16)
    return _nt_dot(w_hi, h_hi) + (_nt_dot(w_hi, h_lo) + _nt_dot(w_lo, h_hi)) + bias


def _post_kernel(*refs, alpha, with_ssm):
    wr_ref, br_ref = refs[-5:-3]
    refs = refs[:-5] + refs[-3:]
    if with_ssm:
        (att_ref, ys_ref, u_ref, dsk_ref, gw_ref, gb_ref, wo_ref, x_ref, g1_ref, lg_ref, lb_ref, sc2_ref, sh2_ref,
         x1_ref, h2_ref, lgt_ref) = refs
        y = ys_ref[0] + u_ref[0] * dsk_ref[...]
        gl = jax.nn.gelu(y)
        gate = jax.nn.sigmoid(jnp.dot(gl.astype(BF16), gw_ref[...], preferred_element_type=F32) + gb_ref[...])
        ssm = (gl * gate).astype(BF16)
        mix = (jnp.dot(att_ref[0], wo_ref[0:SWA_Q_W, :], preferred_element_type=F32)
               + jnp.dot(ssm, wo_ref[SWA_Q_W:, :], preferred_element_type=F32))
    else:
        att_ref, wo_ref, x_ref, g1_ref, lg_ref, lb_ref, sc2_ref, sh2_ref, x1_ref, h2_ref, lgt_ref = refs
        mix = jnp.dot(att_ref[0], wo_ref[...], preferred_element_type=F32)
    x1 = _layer_norm(alpha * x_ref[0] + g1_ref[0] * mix, lg_ref[...], lb_ref[...])
    x1_ref[0] = x1
    h2 = x1 * (1.0 + sc2_ref[0]) + sh2_ref[0]
    _split_rows(h2_ref, h2)
    lgt_ref[...] = _router_logits(wr_ref[...], br_ref[...], h2)


def _post(att, ssm_args, w_out_bf16, x, g1, ln_g, ln_b, sc2, sh2, router_wb, alpha, tm):
    b, n, d = x.shape
    tok = lambda w: pl.BlockSpec((1, tm, w), lambda i, j: (i, j, 0))
    const = lambda a: pl.BlockSpec(a.shape, lambda i, j: (0,) * a.ndim)
    in_specs = [tok(att.shape[-1])]
    args = [att]
    if ssm_args is not None:
        ys, u, dsk, gw, gb = ssm_args
        in_specs += [tok(SSM_WIDTH), tok(SSM_WIDTH), const(dsk), const(gw), const(gb)]
        args += [ys, u, dsk, gw, gb]
    in_specs += [const(w_out_bf16), tok(d), g1.spec(), const(ln_g), const(ln_b), sc2.spec(), sh2.spec()]
    args += [w_out_bf16, x, g1.table, ln_g, ln_b, sc2.table, sh2.table]
    in_specs += [const(a) for a in router_wb]
    args += list(router_wb)
    per_b = n // tm
    n_logit = router_wb[0].shape[0]
    return pl.pallas_call(
        functools.partial(_post_kernel, alpha=alpha, with_ssm=ssm_args is not None),
        out_shape=(_sds((b, n, d), F32), _sds((b * n * TOKEN_TILE_ROWS, LANES), F32), _sds((n_logit, b * n), F32)),
        grid=(b, n // tm),
        in_specs=in_specs,
        out_specs=(tok(d), pl.BlockSpec((tm * TOKEN_TILE_ROWS, LANES), lambda i, j: (i * per_b + j, 0)),
                   pl.BlockSpec((n_logit, tm), lambda i, j: (0, i * per_b + j))),
        compiler_params=_cparams("parallel", "parallel"),
        name="post_mixer_ssm" if ssm_args is not None else "post_mixer",
    )(*args)


def _first_max(v, sub):
    m = jnp.max(v, 0, keepdims=True)
    idx = jnp.min(jnp.where(v == m, sub, float(SUBLANES)), 0, keepdims=True)
    return m, idx


def _stream_blocks(toks, tb):
    starts = [0]
    for t in toks:
        starts.append(starts[-1] + t.shape[0] // (tb * TOKEN_TILE_ROWS))
    return starts


def _stream_spec(tok, tb, start):
    last = tok.shape[0] // (tb * TOKEN_TILE_ROWS) - 1
    return pl.BlockSpec((tb * TOKEN_TILE_ROWS, LANES), lambda i, *_: (jnp.clip(i - start, 0, last), 0))


def _router_kernel(*refs, starts):
    n_streams = len(starts) - 1
    lgt_refs = refs[:n_streams]
    ids_ref, wts_ref, rank_ref, cnt_ref, carry_scr = refs[n_streams:]
    step = pl.program_id(0)

    @pl.when(step == 0)
    def _():
        carry_scr[...] = jnp.zeros_like(carry_scr)

    logits = lgt_refs[0][...]
    for ref, start in zip(lgt_refs[1:], starts[1:]):
        logits = jnp.where(step >= start, ref[...], logits)
    tm = logits.shape[1]
    sub = lax.broadcasted_iota(jnp.int32, (SUBLANES, tm), 0).astype(F32)
    gl = logits[0:SUBLANES]
    gmax, gi = _first_max(gl, sub)
    gp = 1.0 / jnp.sum(jnp.exp(gl - gmax), 0, keepdims=True)
    le = logits[ROUTER_EXPERT_ROW0:ROUTER_EXPERT_ROW0 + MOE_EPG]
    for g in range(1, MOE_GROUPS):
        le = jnp.where(gi == float(g), logits[ROUTER_EXPERT_ROW0 + g * MOE_EPG:ROUTER_EXPERT_ROW0 + (g + 1) * MOE_EPG], le)
    m1, i1 = _first_max(le, sub)
    m2, i2 = _first_max(jnp.where(sub == i1, NEG_INF, le), sub)
    t = jnp.exp(m2 - m1)
    e1 = gi * float(MOE_EPG) + i1
    e2 = gi * float(MOE_EPG) + i2
    ids_ref[0:1, :] = e1.astype(jnp.int32)
    ids_ref[1:2, :] = e2.astype(jnp.int32)
    wts_ref[0:1, :] = gp / (1.0 + t)
    wts_ref[1:2, :] = gp * t / (1.0 + t)
    esub = lax.broadcasted_iota(jnp.int32, (MOE_EXPERTS, tm), 0).astype(F32)
    oh1 = (esub == e1).astype(F32)
    oh2 = (esub == e2).astype(F32)
    both = oh1 + oh2
    earlier = (lax.broadcasted_iota(jnp.int32, (tm, tm), 0) < lax.broadcasted_iota(jnp.int32, (tm, tm), 1))
    prefix = jnp.dot(both.astype(BF16), earlier.astype(BF16), preferred_element_type=F32) + carry_scr[...]
    rank_ref[0:1, :] = jnp.sum(oh1 * prefix, 0, keepdims=True).astype(jnp.int32)
    rank_ref[1:2, :] = jnp.sum(oh2 * prefix, 0, keepdims=True).astype(jnp.int32)
    carry_scr[...] += jnp.sum(both, 1, keepdims=True)
    cnt_ref[...] = jnp.broadcast_to(carry_scr[...], cnt_ref.shape)


def _router_weights(wg, bg, we, be):
    d = wg.shape[0]
    rows = ROUTER_EXPERT_ROW0 + MOE_EXPERTS
    w = jnp.zeros((rows, d), F32)
    w = w.at[:MOE_GROUPS].set(wg.T)
    w = w.at[ROUTER_EXPERT_ROW0:].set(jnp.transpose(we, (0, 2, 1)).reshape(MOE_EXPERTS, d))
    bias = jnp.full((rows, 1), NEG_INF, F32)
    bias = bias.at[:MOE_GROUPS, 0].set(bg)
    bias = bias.at[ROUTER_EXPERT_ROW0:, 0].set(be.reshape(-1))
    return w, bias


def _router(logit_streams, tm):
    starts = [0]
    for lg in logit_streams:
        starts.append(starts[-1] + lg.shape[1] // tm)
    t = starts[-1] * tm
    rows = logit_streams[0].shape[0]
    stream_spec = lambda lg, s0: pl.BlockSpec(
        (rows, tm), lambda i: (0, jnp.clip(i - s0, 0, lg.shape[1] // tm - 1)))
    pair = pl.BlockSpec((MOE_TOPK, tm), lambda i: (0, i))
    ids, wts, rank, cnt = pl.pallas_call(
        functools.partial(_router_kernel, starts=starts),
        out_shape=(_sds((MOE_TOPK, t), jnp.int32), _sds((MOE_TOPK, t), F32), _sds((MOE_TOPK, t), jnp.int32),
                   _sds((MOE_EXPERTS, LANES), F32)),
        grid=(t // tm,),
        in_specs=[stream_spec(lg, s0) for lg, s0 in zip(logit_streams, starts)],
        out_specs=(pair, pair, pair, pl.BlockSpec((MOE_EXPERTS, LANES), lambda i: (0, 0))),
        scratch_shapes=[pltpu.VMEM((MOE_EXPERTS, 1), F32)],
        compiler_params=_cparams("arbitrary"),
        name="moe_router",
    )(*logit_streams)
    return ids, wts, rank, cnt[:, 0].astype(jnp.int32)


def _moe_plan(ids, rank, counts, n_tok):
    tm = MOE_ROW_TILE
    padded = ((counts + tm - 1) // tm) * tm
    ends = jnp.cumsum(padded)
    offs = ends - padded
    experts = jnp.arange(MOE_EXPERTS, dtype=jnp.int32)
    dest = (jnp.sum(jnp.where(ids[..., None] == experts, offs, 0), -1) + rank).astype(jnp.int32)
    n_tiles = (MOE_TOPK * n_tok + MOE_EXPERTS * (tm - 1)) // tm
    starts = jnp.arange(n_tiles, dtype=jnp.int32) * tm
    tile_expert = jnp.minimum(jnp.sum((ends[None, :] <= starts[:, None]).astype(jnp.int32), -1), MOE_EXPERTS - 1)
    n_valid = (ends[-1] // tm).astype(jnp.int32).reshape(1)
    pad = jnp.stack([offs + counts, padded - counts, jnp.broadcast_to(n_valid, counts.shape)]).astype(jnp.int32)
    return dest, tile_expert, n_valid, pad, n_tiles * tm


def _pad_fill(pad_ref, zero_scr, xs_ref, sem, wait):
    def per_expert(e, carry):
        first, count = pad_ref[0, e], pad_ref[1, e]
        piece = MOE_ROW_TILE // 2
        while piece >= 1:
            row = first + (count & ~(2 * piece - 1))
            n = piece * TOKEN_TILE_ROWS

            @pl.when((count & piece) != 0)
            def _(row=row, n=n):
                copy = pltpu.make_async_copy(
                    zero_scr.at[pl.ds(0, n)],
                    xs_ref.at[pl.ds(pl.multiple_of(row * TOKEN_TILE_ROWS, TOKEN_TILE_ROWS), n)], sem)
                copy.wait() if wait else copy.start()

            piece //= 2
        return carry

    lax.fori_loop(0, MOE_EXPERTS, per_expert, 0)
    half = MOE_ROW_TILE // 2 * TOKEN_TILE_ROWS
    n_tiles = xs_ref.shape[0] // (2 * half)

    def per_tile(t, carry):
        for h in range(2):
            copy = pltpu.make_async_copy(zero_scr, xs_ref.at[pl.ds(pl.multiple_of((2 * t + h) * half, half), half)],
                                         sem)
            copy.wait() if wait else copy.start()
        return carry

    lax.fori_loop(pad_ref[2, 0], n_tiles, per_tile, 0)


def _dispatch_kernel(*refs, tb, starts):
    n_streams = len(starts) - 1
    dest_ref, pad_ref = refs[:2]
    tok_refs = refs[2:2 + n_streams]
    xs_ref, zero_scr, sem, pad_sem = refs[2 + n_streams:]
    step = pl.program_id(0)

    @pl.when(step == 0)
    def _():
        zero_scr[...] = jnp.zeros_like(zero_scr)
        _pad_fill(pad_ref, zero_scr, xs_ref, pad_sem, wait=False)

    for s, tok_ref in enumerate(tok_refs):
        @pl.when((step >= starts[s]) & (step < starts[s + 1]))
        def _(tok_ref=tok_ref):
            def body(r, carry):
                for k in range(MOE_TOPK):
                    pltpu.make_async_copy(_token_tile(tok_ref, r), _token_tile(xs_ref, dest_ref[MOE_TOPK * r + k]),
                                          sem).start(priority=k % 2)
                return carry

            lax.fori_loop(0, tb, body, 0, unroll=8)

    for k in range(MOE_TOPK):
        pltpu.make_async_copy(tok_refs[0], xs_ref.at[pl.ds(0, tb * TOKEN_TILE_ROWS)], sem).wait()

    @pl.when(step == 0)
    def _():
        _pad_fill(pad_ref, zero_scr, xs_ref, pad_sem, wait=True)


def _dispatch(toks, dest, pad, n_rows):
    tb = MOE_DISPATCH_TILE
    nblk = dest.shape[0] // (MOE_TOPK * tb)
    starts = _stream_blocks(toks, tb)
    return pl.pallas_call(
        functools.partial(_dispatch_kernel, tb=tb, starts=starts),
        out_shape=_sds((n_rows * TOKEN_TILE_ROWS, LANES), F32),
        grid=(nblk,),
        in_specs=[pl.BlockSpec((MOE_TOPK * tb,), lambda i: (i,), memory_space=pltpu.SMEM),
                  pl.BlockSpec(memory_space=pltpu.SMEM)]
        + [_stream_spec(tok, tb, s0) for tok, s0 in zip(toks, starts)],
        out_specs=pl.BlockSpec(memory_space=pl.ANY),
        scratch_shapes=[pltpu.VMEM((MOE_ROW_TILE // 2 * TOKEN_TILE_ROWS, LANES), F32),
                        pltpu.SemaphoreType.DMA(()), pltpu.SemaphoreType.DMA(())],
        compiler_params=_cparams("arbitrary"),
        name="moe_dispatch",
    )(dest, pad, *toks)


def _ffn_kernel(te_ref, nv_ref, x_ref, w1_ref, w3_ref, w2_ref, y_ref, w13_scr, w2_scr):
    i = pl.program_id(0)
    f = w1_ref.shape[2]

    @pl.when((i == 0) | (te_ref[i] != te_ref[jnp.maximum(i - 1, 0)]))
    def _():
        w13_scr[:, 0:f] = w1_ref[0].astype(BF16)
        w13_scr[:, f:2 * f] = w3_ref[0].astype(BF16)
        w2_scr[...] = w2_ref[0].astype(BF16)

    @pl.when(i < nv_ref[0])
    def _():
        h13 = jnp.dot(_merge_rows(x_ref).astype(BF16), w13_scr[...], preferred_element_type=F32)
        h1 = h13[:, 0:f]
        hh = (h1 * jax.nn.sigmoid(h1) * h13[:, f:2 * f]).astype(BF16)
        _split_rows(y_ref, jnp.dot(hh, w2_scr[...], preferred_element_type=F32))

    @pl.when(i >= nv_ref[0])
    def _():
        y_ref[...] = jnp.zeros_like(y_ref)


def _expert_ffn(tile_expert, n_valid, xs, w1, w3, w2):
    p = xs.shape[0] // TOKEN_TILE_ROWS
    _, d, f = w1.shape
    tm = MOE_ROW_TILE
    rows = pl.BlockSpec((tm * TOKEN_TILE_ROWS, LANES), lambda i, te, nv: (i, 0))
    rows_in = pl.BlockSpec((tm * TOKEN_TILE_ROWS, LANES), lambda i, te, nv: (jnp.minimum(i, nv[0] - 1), 0))
    return pl.pallas_call(
        _ffn_kernel,
        out_shape=_sds(xs.shape, F32),
        grid_spec=pltpu.PrefetchScalarGridSpec(
            num_scalar_prefetch=2,
            grid=(p // tm,),
            in_specs=[rows_in,
                      pl.BlockSpec((1, d, f), lambda i, te, nv: (te[i], 0, 0)),
                      pl.BlockSpec((1, d, f), lambda i, te, nv: (te[i], 0, 0)),
                      pl.BlockSpec((1, f, d), lambda i, te, nv: (te[i], 0, 0))],
            out_specs=rows,
            scratch_shapes=[pltpu.VMEM((d, 2 * f), BF16), pltpu.VMEM((f, d), BF16)]),
        compiler_params=_cparams("arbitrary"),
        name="moe_expert_ffn",
    )(tile_expert, n_valid, xs, w1, w3, w2)


def _combine_ln2_kernel(dest_ref, next_ref, x1_ref, ys_ref, wt_ref, g2_ref, lg_ref, lb_ref, o_ref, buf, sem, *,
                        alpha, tb, nblk):
    step = pl.program_id(0)

    def gather(d_ref, slot):
        def body(r, carry):
            for k in range(MOE_TOPK):
                pltpu.make_async_copy(_token_tile(ys_ref, d_ref[MOE_TOPK * r + k]), _token_tile(buf.at[slot, k], r),
                                      sem.at[slot, k]).start(priority=k % 2)
            return carry

        lax.fori_loop(0, tb, body, 0, unroll=8)

    @pl.when(step == 0)
    def _():
        gather(dest_ref, 0)

    @pl.when(step + 1 < nblk)
    def _():
        gather(next_ref, (step + 1) % 2)

    slot = step % 2
    for k in range(MOE_TOPK):
        pltpu.make_async_copy(ys_ref.at[pl.ds(0, tb * TOKEN_TILE_ROWS)], buf.at[slot, k], sem.at[slot, k]).wait()
    f = wt_ref[:, 0:1] * _merge_rows(buf.at[slot, 0]) + wt_ref[:, 1:2] * _merge_rows(buf.at[slot, 1])
    o_ref[...] = _layer_norm(alpha * x1_ref[...] + g2_ref[0] * f, lg_ref[...], lb_ref[...])


def _combine_ln2(dest, x1, ys, wt, g2, ln_g, ln_b, alpha, n_per_sample):
    t, d = x1.shape
    tb = MOE_COMBINE_TILE
    nblk = t // tb
    const = pl.BlockSpec((1, d), lambda i: (0, 0))
    assert g2.stride == 0 or n_per_sample % tb == 0, (n_per_sample, tb)
    return pl.pallas_call(
        functools.partial(_combine_ln2_kernel, alpha=alpha, tb=tb, nblk=nblk),
        out_shape=_sds((t, d), F32),
        grid=(nblk,),
        in_specs=[pl.BlockSpec((MOE_TOPK * tb,), lambda i: (i,), memory_space=pltpu.SMEM),
                  pl.BlockSpec((MOE_TOPK * tb,), lambda i: (jnp.minimum(i + 1, nblk - 1),), memory_space=pltpu.SMEM),
                  pl.BlockSpec((tb, d), lambda i: (i, 0)),
                  pl.BlockSpec(memory_space=pl.ANY),
                  pl.BlockSpec((tb, MOE_TOPK), lambda i: (i, 0)),
                  g2.spec(lambda i: (i * tb) // n_per_sample),
                  const, const],
        out_specs=pl.BlockSpec((tb, d), lambda i: (i, 0)),
        scratch_shapes=[pltpu.VMEM((2, MOE_TOPK, tb * TOKEN_TILE_ROWS, LANES), F32),
                        pltpu.SemaphoreType.DMA((2, MOE_TOPK))],
        compiler_params=_cparams("arbitrary"),
        name="moe_combine_ln2",
    )(dest, dest, x1, ys, wt, g2.table, ln_g, ln_b)


def _proj1_kernel(*refs, rope, with_q):
    x_ref, sc_ref, sh_ref, w_ref = refs[:4]
    refs = refs[4:]
    tabs = None
    if rope:
        tabs = (refs[0][...], refs[1][...], refs[2][...])
        refs = refs[3:]
    h = (x_ref[0] * (1.0 + sc_ref[0]) + sh_ref[0]).astype(BF16)
    r = jnp.dot(h, w_ref[...], preferred_element_type=F32)
    off = 0
    if with_q:
        q_ref, k_ref, v_ref = refs
        for hd in range(DIF_HEADS):
            q_ref[0, hd] = (_rot(r[:, hd * LANES:(hd + 1) * LANES], tabs) * (LOG2_E * HEAD_DIM ** -0.5)).astype(BF16)
        off = DIF_QK_W
    else:
        k_ref, v_ref = refs
    for hd in range(DIF_HEADS):
        k_ref[0, hd] = _rot(r[:, off + hd * LANES:off + (hd + 1) * LANES], tabs).astype(BF16)
        v_ref[0, hd] = r[:, off + DIF_QK_W + hd * LANES:off + DIF_QK_W + (hd + 1) * LANES].astype(BF16)


def _proj1(x, sc, sh, w_bf16, tabs, with_q, tm):
    b, n, d = x.shape
    rope = tabs is not None
    in_specs = [pl.BlockSpec((1, tm, d), lambda i, j: (i, j, 0)), sc.spec(), sh.spec(),
                pl.BlockSpec(w_bf16.shape, lambda i, j: (0, 0))]
    args = [x, sc.table, sh.table, w_bf16]
    if rope:
        in_specs += [pl.BlockSpec((tm, LANES), lambda i, j: (j, 0))] * 3
        args += list(tabs)
    hm = pl.BlockSpec((1, DIF_HEADS, tm, LANES), lambda i, j: (i, 0, j, 0))
    n_out = 3 if with_q else 2
    return pl.pallas_call(
        functools.partial(_proj1_kernel, rope=rope, with_q=with_q),
        out_shape=(_sds((b, DIF_HEADS, n, LANES), BF16),) * n_out,
        grid=(b, n // tm),
        in_specs=in_specs,
        out_specs=(hm,) * n_out,
        compiler_params=_cparams("parallel", "parallel"),
        name="proj1_qkv" if with_q else "proj1_kv_ctx",
    )(*args)


def _diff_kernel(lam_ref, q_ref, kl_ref, kc_ref, vl_ref, vc_ref, g_ref, o_ref, k_scr, v_scr, *, tq, n_lat, out_scale):
    @pl.when(pl.program_id(2) == 0)
    def _():
        k_scr[0:n_lat] = kl_ref[0, 0]
        k_scr[n_lat:] = kc_ref[0, 0]
        v_scr[0:n_lat, 0:LANES] = vl_ref[0, 0]
        v_scr[n_lat:, 0:LANES] = vc_ref[0, 0]
        v_scr[:, LANES:] = jnp.ones((v_scr.shape[0], LANES), BF16)

    lo = lax.broadcasted_iota(jnp.int32, (DIF_CHAIN_ROWS, LANES), 1) < HALF
    for r0 in range(0, tq, DIF_CHAIN_ROWS):
        q = q_ref[0, 0, r0:r0 + DIF_CHAIN_ROWS, :].astype(F32)
        maps = []
        for qm in (jnp.where(lo, q, 0.0), jnp.where(lo, 0.0, q)):
            s = _nt_dot(qm.astype(BF16), k_scr[...])
            p = jnp.exp2(s - jnp.max(s, -1, keepdims=True)).astype(BF16)
            oe = jnp.dot(p, v_scr[...], preferred_element_type=F32)
            maps.append(oe[:, 0:LANES] * (1.0 / oe[:, LANES:]))
        o = maps[0] - lam_ref[0] * maps[1]
        o = o * lax.rsqrt(jnp.mean(o * o, -1, keepdims=True) + RMS_EPS) * g_ref[...]
        o_ref[0, r0:r0 + DIF_CHAIN_ROWS, :] = (o * out_scale).astype(BF16)


def _diff_attention(lam, q, kl, kc, vl, vc, subln_g, lam_init, tq):
    b, nh, n, _ = q.shape
    nc = kc.shape[2]
    kv = lambda m: pl.BlockSpec((1, 1, m, LANES), lambda i, h, j: (i, h, 0, 0))
    return pl.pallas_call(
        functools.partial(_diff_kernel, tq=tq, n_lat=n, out_scale=1.0 - lam_init),
        out_shape=_sds((b, n, nh * LANES), BF16),
        grid=(b, nh, n // tq),
        in_specs=[pl.BlockSpec(memory_space=pltpu.SMEM),
                  pl.BlockSpec((1, 1, tq, LANES), lambda i, h, j: (i, h, j, 0)),
                  kv(n), kv(nc), kv(n), kv(nc),
                  pl.BlockSpec((1, LANES), lambda i, h, j: (0, 0))],
        out_specs=pl.BlockSpec((1, tq, LANES), lambda i, h, j: (i, j, h)),
        scratch_shapes=[pltpu.VMEM((n + nc, LANES), BF16), pltpu.VMEM((n + nc, 2 * LANES), BF16)],
        compiler_params=_cparams("parallel", "parallel", "arbitrary"),
        name="diff_attention",
    )(lam, q, kl, kc, vl, vc, subln_g)


def _moe_block(toks, logit_streams, layer, moe_w1, moe_w3, moe_w2):
    t = sum(tok.shape[0] for tok in toks) // TOKEN_TILE_ROWS
    ids, wts, rank, counts = _router(logit_streams, tm=512)
    dest, tile_expert, n_valid, pad, n_rows = _moe_plan(ids, rank, counts, t)
    dest = dest.T.reshape(-1)
    xs = _dispatch(toks, dest, pad, n_rows)
    flat = lambda w: w.reshape((-1,) + w.shape[2:])
    ys = _expert_ffn(tile_expert + layer * MOE_EXPERTS, n_valid, xs, flat(moe_w1), flat(moe_w3), flat(moe_w2))
    return ys, dest, wts.T


def kernel(x, c, ctx, c_ctx, mod_w, mod_b, ln1_g, ln1_b, ln2_g, ln2_b, swa_ssm_w_in, swa_ssm_w_out, swa_sink, ssm_a_re, ssm_a_im, ssm_log_step, ssm_b_re, ssm_b_im, ssm_c_re, ssm_c_im, ssm_d, ssm_glu_w, ssm_glu_b, dif_w_in, dif_w_out, dif_lam_q1, dif_lam_k1, dif_lam_q2, dif_lam_k2, dif_subln_g, moe_wg, moe_bg, moe_we, moe_be, moe_w1, moe_w3, moe_w2):
    bsz, n, d = x.shape
    ctx_len = ctx.shape[1]
    depth = mod_w.shape[0]
    alpha = (2 * depth) ** 0.25
    tabs = _rope_tables(n)

    n_vec = 16
    cvec = jnp.zeros((n_vec, d), F32).at[:bsz].set(c).at[bsz].set(c_ctx)
    mods = _modulation(cvec, mod_w, mod_b)
    mod_table = mods.reshape(depth * n_vec * 6, 1, d)

    xl, xc = x, ctx
    for layer in range(depth):
        need_ctx = layer < depth - 1
        i = layer // 2
        lat = [ModVec(mod_table, (layer * n_vec) * 6 + k, 6) for k in range(6)]
        cx = [ModVec(mod_table, (layer * n_vec + bsz) * 6 + k, 0) for k in range(6)]
        sh1, sc1, g1, sh2, sc2, g2 = lat
        csh1, csc1, cg1, csh2, csc2, cg2 = cx
        lg1, lb1 = ln1_g[layer].reshape(1, d), ln1_b[layer].reshape(1, d)
        lg2, lb2 = ln2_g[layer].reshape(1, d), ln2_b[layer].reshape(1, d)
        router_wb = _router_weights(moe_wg[layer], moe_bg[layer], moe_we[layer], moe_be[layer])
        if layer % 2 == 0:
            w_in = swa_ssm_w_in[i].astype(BF16)
            w_out = swa_ssm_w_out[i].astype(BF16)
            q, k, v, u = _proj0(xl, sc1, sh1, w_in, tabs, tm=512)
            qc, kc, vc, uc = _proj0(xc, csc1, csh1, w_in, None, tm=ctx_len)
            sink = swa_sink[i].astype(F32)
            att = _swa_attention(sink, q, k, v, kc, vc)
            mats = _ssm_matrices(ssm_a_re[i], ssm_a_im[i], ssm_log_step[i], ssm_b_re[i], ssm_b_im[i],
                                 ssm_c_re[i], ssm_c_im[i])
            ys, ysc = _ssm_scan(u, uc, mats)
            glu = (ssm_d[i].reshape(1, SSM_WIDTH).astype(F32), ssm_glu_w[i].astype(BF16),
                   ssm_glu_b[i].reshape(1, SSM_WIDTH).astype(F32))
            x1, tok, lgt = _post(att, (ys, u) + glu, w_out, xl, g1, lg1, lb1, sc2, sh2, router_wb, alpha, 512)
            toks, logit_streams = [tok], [lgt]
            if need_ctx:
                att_c = _ctx_attention(sink, qc, kc, vc)
                xc1, tok_c, lgt_c = _post(att_c, (ysc, uc) + glu, w_out, xc, cg1, lg1, lb1, csc2, csh2, router_wb,
                                          alpha, ctx_len)
                toks.append(tok_c)
                logit_streams.append(lgt_c)
        else:
            lam_init = 0.8 - 0.6 * math.exp(-0.3 * layer)
            w_in = dif_w_in[i].astype(BF16)
            w_out = dif_w_out[i].astype(BF16)
            q, k, v = _proj1(xl, sc1, sh1, w_in, tabs, True, tm=512)
            kc, vc = _proj1(xc, csc1, csh1, w_in[:, DIF_QK_W:], None, False, tm=ctx_len)
            lam = (jnp.exp(jnp.sum(dif_lam_q1[i].astype(F32) * dif_lam_k1[i].astype(F32)))
                   - jnp.exp(jnp.sum(dif_lam_q2[i].astype(F32) * dif_lam_k2[i].astype(F32))) + lam_init).reshape(1)
            att = _diff_attention(lam, q, k, kc, v, vc, dif_subln_g[i].reshape(1, DIF_V_HEAD).astype(F32),
                                  lam_init, tq=n)
            x1, tok, lgt = _post(att, None, w_out, xl, g1, lg1, lb1, sc2, sh2, router_wb, alpha, 1024)
            toks, logit_streams = [tok], [lgt]
            if need_ctx:
                raise NotImplementedError("a differential-attention layer followed by another layer")
        ys_moe, dest, wt = _moe_block(toks, logit_streams, layer, moe_w1, moe_w3, moe_w2)
        xl = _combine_ln2(dest[:MOE_TOPK * bsz * n], x1.reshape(-1, d), ys_moe, wt[:bsz * n], g2, lg2, lb2, alpha,
                          n).reshape(bsz, n, d)
        if need_ctx:
            xc = _combine_ln2(dest[MOE_TOPK * bsz * n:], xc1.reshape(-1, d), ys_moe, wt[bsz * n:], cg2, lg2, lb2, alpha,
                              ctx_len).reshape(bsz, ctx_len, d)
    return xl
```
